```python
import math
import jax, jax.numpy as jnp
from jax import lax
import numpy as np

D_MODEL = 1024
BATCH = 4
SEQ = 4096
DEPTH = 2

CHUNK = 64
HEAD_DIM = 64
N_A_LAYERS = DEPTH // 2
N_B_LAYERS = DEPTH - N_A_LAYERS
RMS_EPS = 1e-6
A_HEADS = D_MODEL // HEAD_DIM
A_WIDTH = A_HEADS * HEAD_DIM
A_LEFT_CHUNKS = 8
A_BAND = (A_LEFT_CHUNKS + 1) * CHUNK
A_REL_CLIP = 256
B_Q_HEADS = D_MODEL // HEAD_DIM
B_KV_HEADS = max(1, B_Q_HEADS // 8)
B_GROUP = B_Q_HEADS // B_KV_HEADS
B_WIDTH = B_Q_HEADS * HEAD_DIM
B_KV_WIDTH = B_KV_HEADS * HEAD_DIM
B_WINDOW = 128
B_LEFT_CHUNKS = (B_WINDOW - 1) // CHUNK + 1
B_BAND = (B_LEFT_CHUNKS + 1) * CHUNK
T5_BUCKETS = 32
T5_MAX_DIST = 128

kernel_name = "yoco_chunk_relbias_swa_sink_hybrid"


def rmsnorm(x, g):
    xf = x.astype(jnp.float32)
    y = xf * lax.rsqrt(jnp.mean(xf * xf, axis=-1, keepdims=True) + RMS_EPS)
    return (y * g.astype(jnp.float32)).astype(x.dtype)


def t5_bucket(rel):
    nb = T5_BUCKETS // 2
    max_exact = nb // 2
    ret = jnp.where(rel > 0, nb, 0)
    n = jnp.abs(rel)
    nf = jnp.maximum(n, 1).astype(jnp.float32)
    large = max_exact + (jnp.log(nf / max_exact) / math.log(T5_MAX_DIST / max_exact)
                         * (nb - max_exact)).astype(jnp.int32)
    large = jnp.minimum(large, nb - 1)
    return ret + jnp.where(n < max_exact, n, large)


def mixer_a(x, w_in, rel_bias, w_out):
    b, s, _ = x.shape
    nc = s // CHUNK
    pad = A_LEFT_CHUNKS * CHUNK
    q, k, v, g = jnp.split(x @ w_in, 4, axis=-1)
    q = q.reshape(b, nc, CHUNK, A_HEADS, HEAD_DIM)
    k = jnp.pad(k.reshape(b, s, A_HEADS, HEAD_DIM), ((0, 0), (pad, 0), (0, 0), (0, 0)))
    v = jnp.pad(v.reshape(b, s, A_HEADS, HEAD_DIM), ((0, 0), (pad, 0), (0, 0), (0, 0)))
    dist = jnp.arange(CHUNK)[:, None] + pad - jnp.arange(A_BAND)[None, :]
    idx = jnp.clip(dist, -A_REL_CLIP, A_REL_CLIP) + A_REL_CLIP
    bias = jnp.transpose(rel_bias[idx], (2, 0, 1)).astype(jnp.float32)
    scale = HEAD_DIM ** -0.5

    def one_chunk(args):
        qc, c = args
        kc = lax.dynamic_slice_in_dim(k, c * CHUNK, A_BAND, axis=1)
        vc = lax.dynamic_slice_in_dim(v, c * CHUNK, A_BAND, axis=1)
        logits = jnp.einsum('bqhd,bkhd->bhqk', qc, kc).astype(jnp.float32) * scale + bias
        key_pos = c * CHUNK - pad + jnp.arange(A_BAND)
        logits = jnp.where((key_pos >= 0)[None, None, None, :], logits, -jnp.inf)
        p = jax.nn.softmax(logits, axis=-1).astype(vc.dtype)
        return jnp.einsum('bhqk,bkhd->bqhd', p, vc)

    out = lax.map(one_chunk, (jnp.moveaxis(q, 1, 0), jnp.arange(nc)))
    out = jnp.moveaxis(out, 0, 1).reshape(b, s, A_WIDTH)
    return (out * jax.nn.silu(g)) @ w_out


def mixer_b(x, w_in, sinks, t5_table, k_sh, v_sh, w_out):
    b, s, _ = x.shape
    nc = s // CHUNK
    pad = B_LEFT_CHUNKS * CHUNK
    q, g = jnp.split(x @ w_in, 2, axis=-1)
    q = q.reshape(b, nc, CHUNK, B_KV_HEADS, B_GROUP, HEAD_DIM)
    kp = jnp.pad(k_sh, ((0, 0), (pad, 0), (0, 0), (0, 0)))
    vp = jnp.pad(v_sh, ((0, 0), (pad, 0), (0, 0), (0, 0)))
    band_idx = jnp.arange(nc)[:, None] * CHUNK + jnp.arange(B_BAND)[None, :]
    kb = kp[:, band_idx]
    vb = vp[:, band_idx]
    rel = jnp.arange(B_BAND)[None, :] - pad - jnp.arange(CHUNK)[:, None]
    bias = jnp.transpose(t5_table[t5_bucket(rel)], (2, 0, 1)).astype(jnp.float32)
    bias = bias.reshape(B_KV_HEADS, B_GROUP, 1, CHUNK, B_BAND)
    scale = HEAD_DIM ** -0.5
    logits = jnp.einsum('bcqhgd,bckhd->bhgcqk', q, kb).astype(jnp.float32) * scale + bias
    key_pos = band_idx - pad
    logits = jnp.where((key_pos >= 0)[:, None, :], logits, -jnp.inf)
    sink = jnp.broadcast_to(
        sinks.astype(jnp.float32).reshape(1, B_KV_HEADS, B_GROUP, 1, 1, 1),
        logits.shape[:-1] + (1,))
    p = jax.nn.softmax(jnp.concatenate([logits, sink], axis=-1), axis=-1)[..., :-1]
    out = jnp.einsum('bhgcqk,bckhd->bcqhgd', p.astype(vb.dtype), vb)
    out = out.reshape(b, s, B_WIDTH)
    return (out * jax.nn.silu(g)) @ w_out


def setup_inputs(seed: int = 0) -> dict:
    key = jax.random.key(seed)
    ks = jax.random.split(key, 13)
    f32 = jnp.float32
    nrm = lambda k, shape, sc: (jax.random.normal(k, shape, f32) * sc).astype(f32)
    return {
        "x": nrm(ks[0], (BATCH, SEQ, D_MODEL), 1.0),
        "a_norm": 1.0 + nrm(ks[1], (N_A_LAYERS, D_MODEL), 0.02),
        "a_w_in": nrm(ks[2], (N_A_LAYERS, D_MODEL, 4 * A_WIDTH), D_MODEL ** -0.5),
        "a_rel_bias": nrm(ks[3], (N_A_LAYERS, 2 * A_REL_CLIP + 1, A_HEADS), 0.3),
        "a_w_out": nrm(ks[4], (N_A_LAYERS, A_WIDTH, D_MODEL), A_WIDTH ** -0.5),
        "kv_norm": 1.0 + nrm(ks[5], (D_MODEL,), 0.02),
        "kv_w": nrm(ks[6], (D_MODEL, 2 * B_KV_WIDTH), D_MODEL ** -0.5),
        "t5_bias": nrm(ks[7], (T5_BUCKETS, B_Q_HEADS), 0.3),
        "b_norm": 1.0 + nrm(ks[8], (N_B_LAYERS, D_MODEL), 0.02),
        "b_w_in": nrm(ks[9], (N_B_LAYERS, D_MODEL, 2 * B_WIDTH), D_MODEL ** -0.5),
        "b_sinks": nrm(ks[10], (N_B_LAYERS, B_Q_HEADS), 0.5),
        "b_w_out": nrm(ks[11], (N_B_LAYERS, B_WIDTH, D_MODEL), B_WIDTH ** -0.5),
        "final_norm": 1.0 + nrm(ks[12], (D_MODEL,), 0.02),
    }


def reference(x, a_norm, a_w_in, a_rel_bias, a_w_out, kv_norm, kv_w, t5_bias,
              b_norm, b_w_in, b_sinks, b_w_out, final_norm):
    h = x
    k_sh = v_sh = None
    for layer in range(DEPTH):
        if layer == N_A_LAYERS:
            kv = rmsnorm(h, kv_norm) @ kv_w
            k_sh, v_sh = jnp.split(kv, 2, axis=-1)
            k_sh = k_sh.reshape(h.shape[0], h.shape[1], B_KV_HEADS, HEAD_DIM)
            v_sh = v_sh.reshape(h.shape[0], h.shape[1], B_KV_HEADS, HEAD_DIM)
        if layer < N_A_LAYERS:
            h = h + mixer_a(rmsnorm(h, a_norm[layer]), a_w_in[layer],
                            a_rel_bias[layer], a_w_out[layer])
        else:
            j = layer - N_A_LAYERS
            h = h + mixer_b(rmsnorm(h, b_norm[j]), b_w_in[j], b_sinks[j], t5_bias,
                            k_sh, v_sh, b_w_out[j])
    return rmsnorm(h, final_norm)
```

```python
import functools
import math

import jax
import jax.numpy as jnp
from jax import lax
from jax.experimental import pallas as pl
from jax.experimental.pallas import tpu as pltpu

D_MODEL = 1024
HEAD_DIM = 64
N_HEADS = D_MODEL // HEAD_DIM
N_PAIRS = N_HEADS // 2
PAIR_W = 2 * HEAD_DIM
CHUNK = 64
RMS_EPS = 1e-6
A_LEFT_CHUNKS = 8
A_REL_CLIP = 256
B_KV_HEADS = 2
B_GROUP = N_HEADS // B_KV_HEADS
B_LEFT_CHUNKS = 2
T5_BUCKETS = 32
T5_MAX_DIST = 128

TQ = 256
A_KBLOCKS = A_LEFT_CHUNKS * CHUNK // TQ + 1
B_PREV = B_LEFT_CHUNKS * CHUNK
TM_PROJ = 512
FEAT_CHUNK = 512
MASKED = -1e30
VMEM_LIMIT = 56 * 1024 * 1024

_NT = (((1,), (1,)), ((), ()))
_TN = (((0,), (0,)), ((), ()))


def _rms_scale(xf):
    return lax.rsqrt(jnp.mean(xf * xf, axis=-1, keepdims=True) + RMS_EPS)


def _silu(v):
    return v * jax.nn.sigmoid(v)


def _proj_a_kernel(x_ref, g_ref, wk_ref, wt_ref, qt_ref, k_ref, vt_ref, gt_ref):
    xf = x_ref[0]
    xn = (xf * _rms_scale(xf) * g_ref[...]).astype(jnp.bfloat16)
    k = jnp.dot(xn, wk_ref[...], preferred_element_type=jnp.float32)
    for p in range(N_PAIRS):
        k_ref[0, p] = k[:, p * PAIR_W:(p + 1) * PAIR_W].astype(jnp.bfloat16)
    n_chunks = D_MODEL // FEAT_CHUNK
    for r in range(3 * n_chunks):
        w = wt_ref[r * FEAT_CHUNK:(r + 1) * FEAT_CHUNK, :]
        yt = lax.dot_general(w, xn, _NT, preferred_element_type=jnp.float32)
        which, c = divmod(r, n_chunks)
        rows = slice(c * FEAT_CHUNK, (c + 1) * FEAT_CHUNK)
        if which == 0:
            qt_ref[0, rows, :] = yt.astype(jnp.bfloat16)
        elif which == 1:
            vt_ref[0, rows, :] = yt.astype(jnp.bfloat16)
        else:
            gt_ref[0, rows, :] = _silu(yt).astype(jnp.bfloat16)


def _proj_a(x, gain, wk, wt):
    b, s, d = x.shape
    feat = jax.ShapeDtypeStruct((b, d, s), jnp.bfloat16)
    return pl.pallas_call(
        _proj_a_kernel,
        grid=(b, s // TM_PROJ),
        in_specs=[
            pl.BlockSpec((1, TM_PROJ, d), lambda bi, i: (bi, i, 0)),
            pl.BlockSpec((1, d), lambda bi, i: (0, 0)),
            pl.BlockSpec((d, d), lambda bi, i: (0, 0)),
            pl.BlockSpec((3 * d, d), lambda bi, i: (0, 0)),
        ],
        out_specs=[
            pl.BlockSpec((1, d, TM_PROJ), lambda bi, i: (bi, 0, i)),
            pl.BlockSpec((1, N_PAIRS, TM_PROJ, PAIR_W), lambda bi, i: (bi, 0, i, 0)),
            pl.BlockSpec((1, d, TM_PROJ), lambda bi, i: (bi, 0, i)),
            pl.BlockSpec((1, d, TM_PROJ), lambda bi, i: (bi, 0, i)),
        ],
        out_shape=[feat, jax.ShapeDtypeStruct((b, N_PAIRS, s, PAIR_W), jnp.bfloat16), feat, feat],
        compiler_params=pltpu.CompilerParams(
            dimension_semantics=("arbitrary", "arbitrary"), vmem_limit_bytes=VMEM_LIMIT),
        name="proj_a",
    )(x, gain, wk, wt)


def _softmax_pv(scores, values, extra_logit=None):
    m = functools.reduce(jnp.maximum, [jnp.max(s, axis=0, keepdims=True) for s in scores])
    if extra_logit is not None:
        m = jnp.maximum(m, extra_logit)
    ps = [jnp.exp(s - m) for s in scores]
    l = functools.reduce(jnp.add, [jnp.sum(p, axis=0, keepdims=True) for p in ps])
    if extra_logit is not None:
        l = l + jnp.exp(extra_logit - m)
    o = functools.reduce(jnp.add, [
        jnp.dot(v, p.astype(jnp.bfloat16), preferred_element_type=jnp.float32)
        for v, p in zip(values, ps)])
    return o / l


def _out_and_residual(zt_ref, wout_ref, res):
    y = lax.dot_general(zt_ref[...], wout_ref[...], _TN, preferred_element_type=jnp.float32)
    return res + y


def _attn_a_kernel(x_ref, qt_ref, k0_ref, k1_ref, k2_ref, v0_ref, v1_ref, v2_ref, gt_ref,
                   bias_ref, wout_ref, kvg_ref, bg_ref, kvwk_ref, kvwvt_ref, bwt_ref,
                   h_ref, ksh_ref, vsht_ref, qbt_ref, gbt_ref, zt_ref):
    i = pl.program_id(1)
    k_refs = (k0_ref, k1_ref, k2_ref)
    v_refs = (v0_ref, v1_ref, v2_ref)
    n_slabs = N_HEADS * A_KBLOCKS
    rows = lax.broadcasted_iota(jnp.int32, (PAIR_W, TQ), 0)

    def pair_body(p, carry):
        q2 = qt_ref[0, p]
        for hh in range(2):
            own_rows = rows < HEAD_DIM if hh == 0 else rows >= HEAD_DIM
            qz = jnp.where(own_rows, q2, jnp.zeros_like(q2))
            scores, values = [], []
            for j in range(A_KBLOCKS):
                in_seq = i - (A_KBLOCKS - 1) + j >= 0
                slab = jnp.where(in_seq, (2 * p + hh) * A_KBLOCKS + j, n_slabs)
                s = jnp.dot(k_refs[j][0, p], qz, preferred_element_type=jnp.float32)
                scores.append(s + bias_ref[slab])
                values.append(v_refs[j][0, p, hh * HEAD_DIM:(hh + 1) * HEAD_DIM, :])
            o = _softmax_pv(scores, values)
            gate = gt_ref[0, p, hh * HEAD_DIM:(hh + 1) * HEAD_DIM, :].astype(jnp.float32)
            row0 = pl.multiple_of(p * PAIR_W + hh * HEAD_DIM, HEAD_DIM)
            zt_ref[pl.ds(row0, HEAD_DIM), :] = (o * gate).astype(jnp.bfloat16)
        return carry

    lax.fori_loop(0, N_PAIRS, pair_body, 0)

    h = _out_and_residual(zt_ref, wout_ref, x_ref[0])
    h_ref[0] = h
    hn = h * _rms_scale(h)
    xkv = (hn * kvg_ref[...]).astype(jnp.bfloat16)
    xb = (hn * bg_ref[...]).astype(jnp.bfloat16)
    ksh_ref[0] = jnp.dot(xkv, kvwk_ref[...], preferred_element_type=jnp.float32).astype(jnp.bfloat16)
    vsht_ref[0] = lax.dot_general(kvwvt_ref[...], xkv, _NT,
                                  preferred_element_type=jnp.float32).astype(jnp.bfloat16)
    n_chunks = D_MODEL // FEAT_CHUNK
    for r in range(2 * n_chunks):
        w = bwt_ref[r * FEAT_CHUNK:(r + 1) * FEAT_CHUNK, :]
        yt = lax.dot_general(w, xb, _NT, preferred_element_type=jnp.float32)
        which, c = divmod(r, n_chunks)
        rws = slice(c * FEAT_CHUNK, (c + 1) * FEAT_CHUNK)
        if which == 0:
            qbt_ref[0, rws, :] = yt.astype(jnp.bfloat16)
        else:
            gbt_ref[0, rws, :] = _silu(yt).astype(jnp.bfloat16)


def _const_spec(shape):
    return pl.BlockSpec(shape, lambda bi, i: (0,) * len(shape), pipeline_mode=pl.Buffered(1))


def _attn_a(x, qt, k, vt, gt, bias, wout, kvg, bg, kvwk, kvwvt, bwt):
    b, s, d = x.shape
    qt4 = qt.reshape(b, N_PAIRS, PAIR_W, s)
    vt4 = vt.reshape(b, N_PAIRS, PAIR_W, s)
    gt4 = gt.reshape(b, N_PAIRS, PAIR_W, s)
    back = A_KBLOCKS - 1

    def kspec(j):
        return pl.BlockSpec((1, N_PAIRS, TQ, PAIR_W),
                            lambda bi, i: (bi, 0, jnp.maximum(i - back + j, 0), 0))

    def vspec(j):
        return pl.BlockSpec((1, N_PAIRS, PAIR_W, TQ),
                            lambda bi, i: (bi, 0, 0, jnp.maximum(i - back + j, 0)))

    feat_spec = pl.BlockSpec((1, N_PAIRS, PAIR_W, TQ), lambda bi, i: (bi, 0, 0, i))
    featout_spec = pl.BlockSpec((1, d, TQ), lambda bi, i: (bi, 0, i))
    feat = jax.ShapeDtypeStruct((b, d, s), jnp.bfloat16)
    return pl.pallas_call(
        _attn_a_kernel,
        grid=(b, s // TQ),
        in_specs=[
            pl.BlockSpec((1, TQ, d), lambda bi, i: (bi, i, 0)),
            feat_spec,
            kspec(0), kspec(1), kspec(2),
            vspec(0), vspec(1), vspec(2),
            feat_spec,
            _const_spec(bias.shape),
            _const_spec(wout.shape),
            _const_spec(kvg.shape),
            _const_spec(bg.shape),
            _const_spec(kvwk.shape),
            _const_spec(kvwvt.shape),
            _const_spec(bwt.shape),
        ],
        out_specs=[
            pl.BlockSpec((1, TQ, d), lambda bi, i: (bi, i, 0)),
            pl.BlockSpec((1, TQ, PAIR_W), lambda bi, i: (bi, i, 0)),
            pl.BlockSpec((1, PAIR_W, TQ), lambda bi, i: (bi, 0, i)),
            featout_spec,
            featout_spec,
        ],
        out_shape=[
            jax.ShapeDtypeStruct((b, s, d), jnp.float32),
            jax.ShapeDtypeStruct((b, s, PAIR_W), jnp.bfloat16),
            jax.ShapeDtypeStruct((b, PAIR_W, s), jnp.bfloat16),
            feat, feat,
        ],
        scratch_shapes=[pltpu.VMEM((d, TQ), jnp.bfloat16)],
        compiler_params=pltpu.CompilerParams(
            dimension_semantics=("arbitrary", "arbitrary"), vmem_limit_bytes=VMEM_LIMIT),
        name="attn_a",
    )(x, qt4, k, k, k, vt4, vt4, vt4, gt4, bias, wout, kvg, bg, kvwk, kvwvt, bwt)


def _attn_b_kernel(h_ref, qt_ref, kp_ref, ko_ref, vp_ref, vo_ref, gt_ref, biasp_ref, biaso_ref,
                   sink_ref, wout_ref, fg_ref, out_ref, zt_ref):
    i = pl.program_id(1)
    k_prev = kp_ref[0, TQ - B_PREV:, :]
    k_own = ko_ref[0]
    zeros = jnp.zeros((HEAD_DIM, TQ), jnp.bfloat16)
    for kvh in range(B_KV_HEADS):
        v_prev = vp_ref[0, kvh * HEAD_DIM:(kvh + 1) * HEAD_DIM, TQ - B_PREV:]
        v_own = vo_ref[0, kvh * HEAD_DIM:(kvh + 1) * HEAD_DIM, :]

        def pair_body(p, carry, kvh=kvh, v_prev=v_prev, v_own=v_own):
            q2 = qt_ref[0, p]
            for hh in range(2):
                head = 2 * p + hh
                qh = q2[hh * HEAD_DIM:(hh + 1) * HEAD_DIM, :]
                qz = jnp.concatenate([qh, zeros] if kvh == 0 else [zeros, qh], axis=0)
                slab = jnp.where(i >= 1, head, N_HEADS)
                s_prev = jnp.dot(k_prev, qz, preferred_element_type=jnp.float32) + biasp_ref[slab]
                s_own = jnp.dot(k_own, qz, preferred_element_type=jnp.float32) + biaso_ref[head]
                o = _softmax_pv([s_prev, s_own], [v_prev, v_own], extra_logit=sink_ref[head])
                gate = gt_ref[0, p, hh * HEAD_DIM:(hh + 1) * HEAD_DIM, :].astype(jnp.float32)
                row0 = pl.multiple_of(p * PAIR_W + hh * HEAD_DIM, HEAD_DIM)
                zt_ref[pl.ds(row0, HEAD_DIM), :] = (o * gate).astype(jnp.bfloat16)
            return carry

        lax.fori_loop(kvh * (N_PAIRS // B_KV_HEADS), (kvh + 1) * (N_PAIRS // B_KV_HEADS), pair_body, 0)

    h2 = _out_and_residual(zt_ref, wout_ref, h_ref[0])
    out_ref[0] = h2 * _rms_scale(h2) * fg_ref[...]


def _attn_b(h, qbt, ksh, vsht, gbt, biasp, biaso, sinks, wout, fg):
    b, s, d = h.shape
    qt4 = qbt.reshape(b, N_PAIRS, PAIR_W, s)
    gt4 = gbt.reshape(b, N_PAIRS, PAIR_W, s)
    feat_spec = pl.BlockSpec((1, N_PAIRS, PAIR_W, TQ), lambda bi, i: (bi, 0, 0, i))
    return pl.pallas_call(
        _attn_b_kernel,
        grid=(b, s // TQ),
        in_specs=[
            pl.BlockSpec((1, TQ, d), lambda bi, i: (bi, i, 0)),
            feat_spec,
            pl.BlockSpec((1, TQ, PAIR_W), lambda bi, i: (bi, jnp.maximum(i - 1, 0), 0)),
            pl.BlockSpec((1, TQ, PAIR_W), lambda bi, i: (bi, i, 0)),
            pl.BlockSpec((1, PAIR_W, TQ), lambda bi, i: (bi, 0, jnp.maximum(i - 1, 0))),
            pl.BlockSpec((1, PAIR_W, TQ), lambda bi, i: (bi, 0, i)),
            feat_spec,
            _const_spec(biasp.shape),
            _const_spec(biaso.shape),
            _const_spec(sinks.shape),
            _const_spec(wout.shape),
            _const_spec(fg.shape),
        ],
        out_specs=pl.BlockSpec((1, TQ, d), lambda bi, i: (bi, i, 0)),
        out_shape=jax.ShapeDtypeStruct((b, s, d), jnp.float32),
        scratch_shapes=[pltpu.VMEM((d, TQ), jnp.bfloat16)],
        compiler_params=pltpu.CompilerParams(
            dimension_semantics=("arbitrary", "arbitrary"), vmem_limit_bytes=VMEM_LIMIT),
        name="attn_b",
    )(h, qt4, ksh, ksh, vsht, vsht, gt4, biasp, biaso, sinks, wout, fg)


def _bias_table_a(rel_bias):
    n_keys = A_KBLOCKS * TQ
    kr = jnp.arange(n_keys)[:, None]
    qc = jnp.arange(TQ)[None, :]
    dist = (A_KBLOCKS - 1) * TQ + qc - kr
    idx = jnp.clip(dist, -A_REL_CLIP, A_REL_CLIP) + A_REL_CLIP
    lag = (A_KBLOCKS - 1) * (TQ // CHUNK) + qc // CHUNK - kr // CHUNK
    valid = (lag >= 0) & (lag <= A_LEFT_CHUNKS)
    tab = jnp.transpose(rel_bias[idx], (2, 0, 1)).astype(jnp.float32)
    tab = jnp.where(valid[None], tab, MASKED)
    tab = tab.reshape(N_HEADS * A_KBLOCKS, TQ, TQ)
    return jnp.concatenate([tab, jnp.full((1, TQ, TQ), MASKED, jnp.float32)], axis=0)


def _t5_bucket(rel):
    nb = T5_BUCKETS // 2
    max_exact = nb // 2
    ret = jnp.where(rel > 0, nb, 0)
    n = jnp.abs(rel)
    nf = jnp.maximum(n, 1).astype(jnp.float32)
    large = max_exact + (jnp.log(nf / max_exact) / math.log(T5_MAX_DIST / max_exact)
                         * (nb - max_exact)).astype(jnp.int32)
    large = jnp.minimum(large, nb - 1)
    return ret + jnp.where(n < max_exact, n, large)


def _bias_tables_b(t5_table):
    n_keys = B_PREV + TQ
    kr = jnp.arange(n_keys)[:, None]
    qc = jnp.arange(TQ)[None, :]
    rel = kr - B_PREV - qc
    lag = qc // CHUNK - (kr - B_PREV) // CHUNK
    valid = (lag >= 0) & (lag <= B_LEFT_CHUNKS)
    tab = jnp.transpose(t5_table[_t5_bucket(rel)], (2, 0, 1)).astype(jnp.float32)
    tab = jnp.where(valid[None], tab, MASKED)
    prev = jnp.concatenate([tab[:, :B_PREV], jnp.full((1, B_PREV, TQ), MASKED, jnp.float32)], axis=0)
    return prev, tab[:, B_PREV:]


def kernel(x, a_norm, a_w_in, a_rel_bias, a_w_out, kv_norm, kv_w, t5_bias,
           b_norm, b_w_in, b_sinks, b_w_out, final_norm):
    assert a_norm.shape[0] == 1 and b_norm.shape[0] == 1, "one A layer then one B layer"
    bf = jnp.bfloat16
    scale = HEAD_DIM ** -0.5
    wq, wk, wv, wg = jnp.split(a_w_in[0], 4, axis=-1)
    wt_a = jnp.concatenate([(wq * scale).T, wv.T, wg.T], axis=0).astype(bf)
    qt, k, vt, gt = _proj_a(x, a_norm[0][None, :], wk.astype(bf), wt_a)

    wq_b, wg_b = jnp.split(b_w_in[0], 2, axis=-1)
    bwt = jnp.concatenate([(wq_b * scale).T, wg_b.T], axis=0).astype(bf)
    kvw_k, kvw_v = jnp.split(kv_w, 2, axis=-1)
    h, ksh, vsht, qbt, gbt = _attn_a(
        x, qt, k, vt, gt, _bias_table_a(a_rel_bias[0]), a_w_out[0].astype(bf),
        kv_norm[None, :], b_norm[0][None, :], kvw_k.astype(bf), kvw_v.T.astype(bf), bwt)

    biasp, biaso = _bias_tables_b(t5_bias)
    sinks = jnp.broadcast_to(b_sinks[0].astype(jnp.float32)[:, None, None], (N_HEADS, 1, TQ))
    return _attn_b(h, qbt, ksh, vsht, gbt, biasp, biaso, sinks, b_w_out[0].astype(bf),
                   final_norm[None, :])
```

```python
import functools
import math

import jax
import jax.numpy as jnp
from jax import lax
from jax.experimental import pallas as pl
from jax.experimental.pallas import tpu as pltpu

D_MODEL = 1024
HEAD_DIM = 64
N_HEADS = D_MODEL // HEAD_DIM
N_PAIRS = N_HEADS // 2
PAIR_W = 2 * HEAD_DIM
CHUNK = 64
RMS_EPS = 1e-6
A_LEFT_CHUNKS = 8
A_REL_CLIP = 256
B_KV_HEADS = 2
B_GROUP = N_HEADS // B_KV_HEADS
B_LEFT_CHUNKS = 2
T5_BUCKETS = 32
T5_MAX_DIST = 128

TQ = 256
A_KBLOCKS = A_LEFT_CHUNKS * CHUNK // TQ + 1
B_PREV = B_LEFT_CHUNKS * CHUNK
TM_PROJ = 512
FEAT_CHUNK = 512
MASKED = -1e30
BIAS_PERIOD = 2 * TQ
VMEM_LIMIT = 56 * 1024 * 1024

_NT = (((1,), (1,)), ((), ()))
_TN = (((0,), (0,)), ((), ()))


def _rms_scale(xf):
    return lax.rsqrt(jnp.mean(xf * xf, axis=-1, keepdims=True) + RMS_EPS)


def _silu(v):
    return v * jax.nn.sigmoid(v)


def _proj_a_kernel(x_ref, g_ref, wk_ref, wt_ref, qt_ref, k_ref, vt_ref, gt_ref):
    xf = x_ref[0]
    xn = (xf * _rms_scale(xf) * g_ref[...]).astype(jnp.bfloat16)
    k = jnp.dot(xn, wk_ref[...], preferred_element_type=jnp.float32)
    for p in range(N_PAIRS):
        k_ref[0, p] = k[:, p * PAIR_W:(p + 1) * PAIR_W].astype(jnp.bfloat16)
    n_chunks = D_MODEL // FEAT_CHUNK
    for r in range(3 * n_chunks):
        w = wt_ref[r * FEAT_CHUNK:(r + 1) * FEAT_CHUNK, :]
        yt = lax.dot_general(w, xn, _NT, preferred_element_type=jnp.float32)
        which, c = divmod(r, n_chunks)
        rows = slice(c * FEAT_CHUNK, (c + 1) * FEAT_CHUNK)
        if which == 0:
            qt_ref[0, rows, :] = yt.astype(jnp.bfloat16)
        elif which == 1:
            vt_ref[0, rows, :] = yt.astype(jnp.bfloat16)
        else:
            gt_ref[0, rows, :] = _silu(yt).astype(jnp.bfloat16)


def _proj_a(x, gain, wk, wt):
    b, s, d = x.shape
    feat = jax.ShapeDtypeStruct((b, d, s), jnp.bfloat16)
    return pl.pallas_call(
        _proj_a_kernel,
        grid=(b, s // TM_PROJ),
        in_specs=[
            pl.BlockSpec((1, TM_PROJ, d), lambda bi, i: (bi, i, 0)),
            pl.BlockSpec((1, d), lambda bi, i: (0, 0)),
            pl.BlockSpec((d, d), lambda bi, i: (0, 0)),
            pl.BlockSpec((3 * d, d), lambda bi, i: (0, 0)),
        ],
        out_specs=[
            pl.BlockSpec((1, d, TM_PROJ), lambda bi, i: (bi, 0, i)),
            pl.BlockSpec((1, N_PAIRS, TM_PROJ, PAIR_W), lambda bi, i: (bi, 0, i, 0)),
            pl.BlockSpec((1, d, TM_PROJ), lambda bi, i: (bi, 0, i)),
            pl.BlockSpec((1, d, TM_PROJ), lambda bi, i: (bi, 0, i)),
        ],
        out_shape=[feat, jax.ShapeDtypeStruct((b, N_PAIRS, s, PAIR_W), jnp.bfloat16), feat, feat],
        compiler_params=pltpu.CompilerParams(
            dimension_semantics=("arbitrary", "arbitrary"), vmem_limit_bytes=VMEM_LIMIT),
        name="proj_a",
    )(x, gain, wk, wt)


def _chunk_iota(shape, axis):
    return lax.shift_right_logical(lax.broadcasted_iota(jnp.int32, shape, axis), CHUNK.bit_length() - 1)


def _fill_bias_slabs(vec_ref, bias_ref, n_keys, valids):
    n = len(valids)

    def head_body(h, carry):
        for j, valid in enumerate(valids):
            v = jnp.broadcast_to(vec_ref[j, h], (n_keys, BIAS_PERIOD))
            t = pltpu.roll(v, 0, 1, stride=1, stride_axis=0)[:, :TQ]
            if valid is not None:
                t = jnp.where(valid, t, MASKED)
            bias_ref[h * n + j] = t
        return carry

    lax.fori_loop(0, N_HEADS, head_body, 0)


def _softmax_pv(scores, values, extra_logit=None):
    m = functools.reduce(jnp.maximum, [jnp.max(s, axis=0, keepdims=True) for s in scores])
    if extra_logit is not None:
        m = jnp.maximum(m, extra_logit)
    ps = [jnp.exp(s - m) for s in scores]
    l = functools.reduce(jnp.add, [jnp.sum(p, axis=0, keepdims=True) for p in ps])
    if extra_logit is not None:
        l = l + jnp.exp(extra_logit - m)
    o = functools.reduce(jnp.add, [
        jnp.dot(v, p.astype(jnp.bfloat16), preferred_element_type=jnp.float32)
        for v, p in zip(values, ps)])
    return o / l


def _out_and_residual(zt_ref, wout_ref, res):
    y = lax.dot_general(zt_ref[...], wout_ref[...], _TN, preferred_element_type=jnp.float32)
    return res + y


def _attn_a_kernel(x_ref, qt_ref, k0_ref, k1_ref, k2_ref, v0_ref, v1_ref, v2_ref, gt_ref,
                   bvec_ref, wout_ref, kvg_ref, bg_ref, kvwk_ref, kvwvt_ref, bwt_ref,
                   h_ref, ksh_ref, vsht_ref, qbt_ref, gbt_ref, zt_ref, bias_ref):
    i = pl.program_id(1)

    @pl.when((pl.program_id(0) == 0) & (i == 0))
    def _():
        kch = _chunk_iota((TQ, TQ), 0)
        qch = _chunk_iota((TQ, TQ), 1)
        _fill_bias_slabs(bvec_ref, bias_ref, TQ, (kch >= qch, None, kch <= qch))
        bias_ref[N_HEADS * A_KBLOCKS] = jnp.full((TQ, TQ), MASKED, jnp.float32)

    k_refs = (k0_ref, k1_ref, k2_ref)
    v_refs = (v0_ref, v1_ref, v2_ref)
    n_slabs = N_HEADS * A_KBLOCKS
    rows = lax.broadcasted_iota(jnp.int32, (PAIR_W, TQ), 0)

    def pair_body(p, carry):
        q2 = qt_ref[0, p]
        for hh in range(2):
            own_rows = rows < HEAD_DIM if hh == 0 else rows >= HEAD_DIM
            qz = jnp.where(own_rows, q2, jnp.zeros_like(q2))
            scores, values = [], []
            for j in range(A_KBLOCKS):
                in_seq = i - (A_KBLOCKS - 1) + j >= 0
                slab = jnp.where(in_seq, (2 * p + hh) * A_KBLOCKS + j, n_slabs)
                s = jnp.dot(k_refs[j][0, p], qz, preferred_element_type=jnp.float32)
                scores.append(s + bias_ref[slab])
                values.append(v_refs[j][0, p, hh * HEAD_DIM:(hh + 1) * HEAD_DIM, :])
            o = _softmax_pv(scores, values)
            gate = gt_ref[0, p, hh * HEAD_DIM:(hh + 1) * HEAD_DIM, :].astype(jnp.float32)
            row0 = pl.multiple_of(p * PAIR_W + hh * HEAD_DIM, HEAD_DIM)
            zt_ref[pl.ds(row0, HEAD_DIM), :] = (o * gate).astype(jnp.bfloat16)
        return carry

    lax.fori_loop(0, N_PAIRS, pair_body, 0)

    h = _out_and_residual(zt_ref, wout_ref, x_ref[0])
    h_ref[0] = h
    hn = h * _rms_scale(h)
    xkv = (hn * kvg_ref[...]).astype(jnp.bfloat16)
    xb = (hn * bg_ref[...]).astype(jnp.bfloat16)
    ksh_ref[0] = jnp.dot(xkv, kvwk_ref[...], preferred_element_type=jnp.float32).astype(jnp.bfloat16)
    vsht_ref[0] = lax.dot_general(kvwvt_ref[...], xkv, _NT,
                                  preferred_element_type=jnp.float32).astype(jnp.bfloat16)
    n_chunks = D_MODEL // FEAT_CHUNK
    for r in range(2 * n_chunks):
        w = bwt_ref[r * FEAT_CHUNK:(r + 1) * FEAT_CHUNK, :]
        yt = lax.dot_general(w, xb, _NT, preferred_element_type=jnp.float32)
        which, c = divmod(r, n_chunks)
        rws = slice(c * FEAT_CHUNK, (c + 1) * FEAT_CHUNK)
        if which == 0:
            qbt_ref[0, rws, :] = yt.astype(jnp.bfloat16)
        else:
            gbt_ref[0, rws, :] = _silu(yt).astype(jnp.bfloat16)


def _const_spec(shape):
    return pl.BlockSpec(shape, lambda bi, i: (0,) * len(shape), pipeline_mode=pl.Buffered(1))


def _attn_a(x, qt, k, vt, gt, bvec, wout, kvg, bg, kvwk, kvwvt, bwt):
    b, s, d = x.shape
    qt4 = qt.reshape(b, N_PAIRS, PAIR_W, s)
    vt4 = vt.reshape(b, N_PAIRS, PAIR_W, s)
    gt4 = gt.reshape(b, N_PAIRS, PAIR_W, s)
    back = A_KBLOCKS - 1

    def kspec(j):
        return pl.BlockSpec((1, N_PAIRS, TQ, PAIR_W),
                            lambda bi, i: (bi, 0, jnp.maximum(i - back + j, 0), 0))

    def vspec(j):
        return pl.BlockSpec((1, N_PAIRS, PAIR_W, TQ),
                            lambda bi, i: (bi, 0, 0, jnp.maximum(i - back + j, 0)))

    feat_spec = pl.BlockSpec((1, N_PAIRS, PAIR_W, TQ), lambda bi, i: (bi, 0, 0, i))
    featout_spec = pl.BlockSpec((1, d, TQ), lambda bi, i: (bi, 0, i))
    feat = jax.ShapeDtypeStruct((b, d, s), jnp.bfloat16)
    return pl.pallas_call(
        _attn_a_kernel,
        grid=(b, s // TQ),
        in_specs=[
            pl.BlockSpec((1, TQ, d), lambda bi, i: (bi, i, 0)),
            feat_spec,
            kspec(0), kspec(1), kspec(2),
            vspec(0), vspec(1), vspec(2),
            feat_spec,
            _const_spec(bvec.shape),
            _const_spec(wout.shape),
            _const_spec(kvg.shape),
            _const_spec(bg.shape),
            _const_spec(kvwk.shape),
            _const_spec(kvwvt.shape),
            _const_spec(bwt.shape),
        ],
        out_specs=[
            pl.BlockSpec((1, TQ, d), lambda bi, i: (bi, i, 0)),
            pl.BlockSpec((1, TQ, PAIR_W), lambda bi, i: (bi, i, 0)),
            pl.BlockSpec((1, PAIR_W, TQ), lambda bi, i: (bi, 0, i)),
            featout_spec,
            featout_spec,
        ],
        out_shape=[
            jax.ShapeDtypeStruct((b, s, d), jnp.float32),
            jax.ShapeDtypeStruct((b, s, PAIR_W), jnp.bfloat16),
            jax.ShapeDtypeStruct((b, PAIR_W, s), jnp.bfloat16),
            feat, feat,
        ],
        scratch_shapes=[pltpu.VMEM((d, TQ), jnp.bfloat16),
                        pltpu.VMEM((N_HEADS * A_KBLOCKS + 1, TQ, TQ), jnp.float32)],
        compiler_params=pltpu.CompilerParams(
            dimension_semantics=("arbitrary", "arbitrary"), vmem_limit_bytes=VMEM_LIMIT),
        name="attn_a",
    )(x, qt4, k, k, k, vt4, vt4, vt4, gt4, bvec, wout, kvg, bg, kvwk, kvwvt, bwt)


def _attn_b_kernel(h_ref, qt_ref, kp_ref, ko_ref, vp_ref, vo_ref, gt_ref, vecp_ref, veco_ref,
                   sink_ref, wout_ref, fg_ref, out_ref, zt_ref, biasp_ref, biaso_ref):
    i = pl.program_id(1)

    @pl.when((pl.program_id(0) == 0) & (i == 0))
    def _():
        lag_prev = _chunk_iota((B_PREV, TQ), 1) - _chunk_iota((B_PREV, TQ), 0) + B_LEFT_CHUNKS
        _fill_bias_slabs(vecp_ref, biasp_ref, B_PREV, (lag_prev <= B_LEFT_CHUNKS,))
        biasp_ref[N_HEADS] = jnp.full((B_PREV, TQ), MASKED, jnp.float32)
        lag_own = _chunk_iota((TQ, TQ), 1) - _chunk_iota((TQ, TQ), 0)
        _fill_bias_slabs(veco_ref, biaso_ref, TQ, ((lag_own >= 0) & (lag_own <= B_LEFT_CHUNKS),))

    k_prev = kp_ref[0, TQ - B_PREV:, :]
    k_own = ko_ref[0]
    zeros = jnp.zeros((HEAD_DIM, TQ), jnp.bfloat16)
    for kvh in range(B_KV_HEADS):
        v_prev = vp_ref[0, kvh * HEAD_DIM:(kvh + 1) * HEAD_DIM, TQ - B_PREV:]
        v_own = vo_ref[0, kvh * HEAD_DIM:(kvh + 1) * HEAD_DIM, :]

        def pair_body(p, carry, kvh=kvh, v_prev=v_prev, v_own=v_own):
            q2 = qt_ref[0, p]
            for hh in range(2):
                head = 2 * p + hh
                qh = q2[hh * HEAD_DIM:(hh + 1) * HEAD_DIM, :]
                qz = jnp.concatenate([qh, zeros] if kvh == 0 else [zeros, qh], axis=0)
                slab = jnp.where(i >= 1, head, N_HEADS)
                s_prev = jnp.dot(k_prev, qz, preferred_element_type=jnp.float32) + biasp_ref[slab]
                s_own = jnp.dot(k_own, qz, preferred_element_type=jnp.float32) + biaso_ref[head]
                o = _softmax_pv([s_prev, s_own], [v_prev, v_own], extra_logit=sink_ref[head])
                gate = gt_ref[0, p, hh * HEAD_DIM:(hh + 1) * HEAD_DIM, :].astype(jnp.float32)
                row0 = pl.multiple_of(p * PAIR_W + hh * HEAD_DIM, HEAD_DIM)
                zt_ref[pl.ds(row0, HEAD_DIM), :] = (o * gate).astype(jnp.bfloat16)
            return carry

        lax.fori_loop(kvh * (N_PAIRS // B_KV_HEADS), (kvh + 1) * (N_PAIRS // B_KV_HEADS), pair_body, 0)

    h2 = _out_and_residual(zt_ref, wout_ref, h_ref[0])
    out_ref[0] = h2 * _rms_scale(h2) * fg_ref[...]


def _attn_b(h, qbt, ksh, vsht, gbt, vecp, veco, sinks, wout, fg):
    b, s, d = h.shape
    qt4 = qbt.reshape(b, N_PAIRS, PAIR_W, s)
    gt4 = gbt.reshape(b, N_PAIRS, PAIR_W, s)
    feat_spec = pl.BlockSpec((1, N_PAIRS, PAIR_W, TQ), lambda bi, i: (bi, 0, 0, i))
    return pl.pallas_call(
        _attn_b_kernel,
        grid=(b, s // TQ),
        in_specs=[
            pl.BlockSpec((1, TQ, d), lambda bi, i: (bi, i, 0)),
            feat_spec,
            pl.BlockSpec((1, TQ, PAIR_W), lambda bi, i: (bi, jnp.maximum(i - 1, 0), 0)),
            pl.BlockSpec((1, TQ, PAIR_W), lambda bi, i: (bi, i, 0)),
            pl.BlockSpec((1, PAIR_W, TQ), lambda bi, i: (bi, 0, jnp.maximum(i - 1, 0))),
            pl.BlockSpec((1, PAIR_W, TQ), lambda bi, i: (bi, 0, i)),
            feat_spec,
            _const_spec(vecp.shape),
            _const_spec(veco.shape),
            _const_spec(sinks.shape),
            _const_spec(wout.shape),
            _const_spec(fg.shape),
        ],
        out_specs=pl.BlockSpec((1, TQ, d), lambda bi, i: (bi, i, 0)),
        out_shape=jax.ShapeDtypeStruct((b, s, d), jnp.float32),
        scratch_shapes=[pltpu.VMEM((d, TQ), jnp.bfloat16),
                        pltpu.VMEM((N_HEADS + 1, B_PREV, TQ), jnp.float32),
                        pltpu.VMEM((N_HEADS, TQ, TQ), jnp.float32)],
        compiler_params=pltpu.CompilerParams(
            dimension_semantics=("arbitrary", "arbitrary"), vmem_limit_bytes=VMEM_LIMIT),
        name="attn_b",
    )(h, qt4, ksh, ksh, vsht, vsht, gt4, vecp, veco, sinks, wout, fg)


def _slab_diff():
    u = jnp.arange(BIAS_PERIOD)
    return jnp.where(u < TQ, u, u - BIAS_PERIOD)


def _bias_vecs_a(rel_bias):
    back = (A_KBLOCKS - 1 - jnp.arange(A_KBLOCKS))[:, None] * TQ
    dist = back + _slab_diff()[None, :]
    idx = jnp.clip(dist, -A_REL_CLIP, A_REL_CLIP) + A_REL_CLIP
    vec = jnp.transpose(rel_bias[idx], (0, 2, 1)).astype(jnp.float32)
    return vec[:, :, None, :]


def _t5_bucket(rel):
    nb = T5_BUCKETS // 2
    max_exact = nb // 2
    ret = jnp.where(rel > 0, nb, 0)
    n = jnp.abs(rel)
    nf = jnp.maximum(n, 1).astype(jnp.float32)
    large = max_exact + (jnp.log(nf / max_exact) / math.log(T5_MAX_DIST / max_exact)
                         * (nb - max_exact)).astype(jnp.int32)
    large = jnp.minimum(large, nb - 1)
    return ret + jnp.where(n < max_exact, n, large)


def _bias_vecs_b(t5_table):
    rel_own = -_slab_diff()
    rel = jnp.stack([rel_own - B_PREV, rel_own])
    vec = jnp.transpose(t5_table[_t5_bucket(rel)], (0, 2, 1)).astype(jnp.float32)
    return vec[0:1, :, None, :], vec[1:2, :, None, :]


def kernel(x, a_norm, a_w_in, a_rel_bias, a_w_out, kv_norm, kv_w, t5_bias,
           b_norm, b_w_in, b_sinks, b_w_out, final_norm):
    assert a_norm.shape[0] == 1 and b_norm.shape[0] == 1, "one A layer then one B layer"
    bf = jnp.bfloat16
    scale = HEAD_DIM ** -0.5
    wq, wk, wv, wg = jnp.split(a_w_in[0], 4, axis=-1)
    wt_a = jnp.concatenate([(wq * scale).T, wv.T, wg.T], axis=0).astype(bf)
    qt, k, vt, gt = _proj_a(x, a_norm[0][None, :], wk.astype(bf), wt_a)

    wq_b, wg_b = jnp.split(b_w_in[0], 2, axis=-1)
    bwt = jnp.concatenate([(wq_b * scale).T, wg_b.T], axis=0).astype(bf)
    kvw_k, kvw_v = jnp.split(kv_w, 2, axis=-1)
    h, ksh, vsht, qbt, gbt = _attn_a(
        x, qt, k, vt, gt, _bias_vecs_a(a_rel_bias[0]), a_w_out[0].astype(bf),
        kv_norm[None, :], b_norm[0][None, :], kvw_k.astype(bf), kvw_v.T.astype(bf), bwt)

    vecp, veco = _bias_vecs_b(t5_bias)
    sinks = jnp.broadcast_to(b_sinks[0].astype(jnp.float32)[:, None, None], (N_HEADS, 1, TQ))
    return _attn_b(h, qbt, ksh, vsht, gbt, vecp, veco, sinks, b_w_out[0].astype(bf),
                   final_norm[None, :])
```

```python
import functools
import math

import jax
import jax.numpy as jnp
from jax import lax
from jax.experimental import pallas as pl
from jax.experimental.pallas import tpu as pltpu

D_MODEL = 1024
HEAD_DIM = 64
N_HEADS = D_MODEL // HEAD_DIM
N_PAIRS = N_HEADS // 2
PAIR_W = 2 * HEAD_DIM
CHUNK = 64
RMS_EPS = 1e-6
A_LEFT_CHUNKS = 8
A_REL_CLIP = 256
B_KV_HEADS = 2
B_GROUP = N_HEADS // B_KV_HEADS
B_LEFT_CHUNKS = 2
T5_BUCKETS = 32
T5_MAX_DIST = 128

TQ = 256
A_KBLOCKS = A_LEFT_CHUNKS * CHUNK // TQ + 1
B_PREV = B_LEFT_CHUNKS * CHUNK
TM_PROJ = 512
FEAT_CHUNK = 512
MASKED = -1e30
BIAS_PERIOD = 2 * TQ
VMEM_LIMIT = 56 * 1024 * 1024

_NT = (((1,), (1,)), ((), ()))
_TN = (((0,), (0,)), ((), ()))


def _rms_scale(xf):
    return lax.rsqrt(jnp.mean(xf * xf, axis=-1, keepdims=True) + RMS_EPS)


def _silu(v):
    return v * jax.nn.sigmoid(v)


def _proj_a_kernel(x_ref, g_ref, wk_ref, wt_ref, qt_ref, k_ref, vt_ref, gt_ref):
    xf = x_ref[0]
    xn = (xf * _rms_scale(xf) * g_ref[...]).astype(jnp.bfloat16)
    k = jnp.dot(xn, wk_ref[...], preferred_element_type=jnp.float32)
    for p in range(N_PAIRS):
        k_ref[0, p] = k[:, p * PAIR_W:(p + 1) * PAIR_W].astype(jnp.bfloat16)
    n_chunks = D_MODEL // FEAT_CHUNK
    for r in range(3 * n_chunks):
        w = wt_ref[r * FEAT_CHUNK:(r + 1) * FEAT_CHUNK, :]
        yt = lax.dot_general(w, xn, _NT, preferred_element_type=jnp.float32)
        which, c = divmod(r, n_chunks)
        rows = slice(c * FEAT_CHUNK, (c + 1) * FEAT_CHUNK)
        if which == 0:
            qt_ref[0, rows, :] = yt.astype(jnp.bfloat16)
        elif which == 1:
            vt_ref[0, rows, :] = yt.astype(jnp.bfloat16)
        else:
            gt_ref[0, rows, :] = _silu(yt).astype(jnp.bfloat16)


def _proj_a(x, gain, wk, wt):
    b, s, d = x.shape
    feat = jax.ShapeDtypeStruct((b, d, s), jnp.bfloat16)
    return pl.pallas_call(
        _proj_a_kernel,
        grid=(b, s // TM_PROJ),
        in_specs=[
            pl.BlockSpec((1, TM_PROJ, d), lambda bi, i: (bi, i, 0)),
            pl.BlockSpec((1, d), lambda bi, i: (0, 0)),
            pl.BlockSpec((d, d), lambda bi, i: (0, 0)),
            pl.BlockSpec((3 * d, d), lambda bi, i: (0, 0)),
        ],
        out_specs=[
            pl.BlockSpec((1, d, TM_PROJ), lambda bi, i: (bi, 0, i)),
            pl.BlockSpec((1, N_PAIRS, TM_PROJ, PAIR_W), lambda bi, i: (bi, 0, i, 0)),
            pl.BlockSpec((1, d, TM_PROJ), lambda bi, i: (bi, 0, i)),
            pl.BlockSpec((1, d, TM_PROJ), lambda bi, i: (bi, 0, i)),
        ],
        out_shape=[feat, jax.ShapeDtypeStruct((b, N_PAIRS, s, PAIR_W), jnp.bfloat16), feat, feat],
        compiler_params=pltpu.CompilerParams(
            dimension_semantics=("arbitrary", "arbitrary"), vmem_limit_bytes=VMEM_LIMIT),
        name="proj_a",
    )(x, gain, wk, wt)


def _chunk_iota(shape, axis):
    return lax.shift_right_logical(lax.broadcasted_iota(jnp.int32, shape, axis), CHUNK.bit_length() - 1)


def _fill_bias_slabs(vec_ref, bias_ref, n_keys, valids):
    n = len(valids)

    def head_body(h, carry):
        for j, valid in enumerate(valids):
            v = jnp.broadcast_to(vec_ref[j, h], (n_keys, BIAS_PERIOD))
            t = pltpu.roll(v, 0, 1, stride=1, stride_axis=0)[:, :TQ]
            if valid is not None:
                t = jnp.where(valid, t, MASKED)
            bias_ref[h * n + j] = t
        return carry

    lax.fori_loop(0, N_HEADS, head_body, 0)


def _scores_stage(k_tiles, qz, bias_tiles, s_ref):
    m, row = None, 0
    for kt, bt in zip(k_tiles, bias_tiles):
        s = jnp.dot(kt, qz, preferred_element_type=jnp.float32) + bt
        s_ref[row:row + s.shape[0], :] = s
        mj = jnp.max(s, axis=0, keepdims=True)
        m = mj if m is None else jnp.maximum(m, mj)
        row += s.shape[0]
    return m


def _pv_stage(s_ref, m, v_tiles, extra_logit=None):
    if extra_logit is not None:
        m = jnp.maximum(m, extra_logit)
    l = jnp.exp(extra_logit - m) if extra_logit is not None else None
    o, row = None, 0
    for vt in v_tiles:
        n = vt.shape[1]
        p = jnp.exp(s_ref[row:row + n, :] - m)
        lj = jnp.sum(p, axis=0, keepdims=True)
        oj = jnp.dot(vt, p.astype(jnp.bfloat16), preferred_element_type=jnp.float32)
        l = lj if l is None else l + lj
        o = oj if o is None else o + oj
        row += n
    return o / l


def _head_pipeline(n_heads, scores_fn, pv_fn, s_refs):
    sa_ref, sb_ref = s_refs
    n_pairs = n_heads // 2

    def pair_body(p, m_a):
        pv_fn(p, 0, sa_ref, m_a)
        m_b = scores_fn(p, 1, sb_ref)
        pv_fn(p, 1, sb_ref, m_b)
        return scores_fn(p + 1, 0, sa_ref)

    m_a = lax.fori_loop(0, n_pairs - 1, pair_body, scores_fn(0, 0, sa_ref))
    last = n_pairs - 1
    pv_fn(last, 0, sa_ref, m_a)
    m_b = scores_fn(last, 1, sb_ref)
    pv_fn(last, 1, sb_ref, m_b)


def _out_and_residual(zt_ref, wout_ref, res):
    y = lax.dot_general(zt_ref[...], wout_ref[...], _TN, preferred_element_type=jnp.float32)
    return res + y


def _attn_a_kernel(x_ref, qt_ref, k0_ref, k1_ref, k2_ref, v0_ref, v1_ref, v2_ref, gt_ref,
                   bvec_ref, wout_ref, kvg_ref, bg_ref, kvwk_ref, kvwvt_ref, bwt_ref,
                   h_ref, ksh_ref, vsht_ref, qbt_ref, gbt_ref, zt_ref, bias_ref, sa_ref, sb_ref):
    i = pl.program_id(1)

    @pl.when((pl.program_id(0) == 0) & (i == 0))
    def _():
        kch = _chunk_iota((TQ, TQ), 0)
        qch = _chunk_iota((TQ, TQ), 1)
        _fill_bias_slabs(bvec_ref, bias_ref, TQ, (kch >= qch, None, kch <= qch))
        bias_ref[N_HEADS * A_KBLOCKS] = jnp.full((TQ, TQ), MASKED, jnp.float32)

    k_refs = (k0_ref, k1_ref, k2_ref)
    v_refs = (v0_ref, v1_ref, v2_ref)
    n_slabs = N_HEADS * A_KBLOCKS
    rows = lax.broadcasted_iota(jnp.int32, (PAIR_W, TQ), 0)

    def scores_fn(p, hh, s_ref):
        q2 = qt_ref[0, p]
        own_rows = rows < HEAD_DIM if hh == 0 else rows >= HEAD_DIM
        qz = jnp.where(own_rows, q2, jnp.zeros_like(q2))
        k_tiles, bias_tiles = [], []
        for j in range(A_KBLOCKS):
            in_seq = i - (A_KBLOCKS - 1) + j >= 0
            slab = jnp.where(in_seq, (2 * p + hh) * A_KBLOCKS + j, n_slabs)
            k_tiles.append(k_refs[j][0, p])
            bias_tiles.append(bias_ref[slab])
        return _scores_stage(k_tiles, qz, bias_tiles, s_ref)

    def pv_fn(p, hh, s_ref, m):
        half = slice(hh * HEAD_DIM, (hh + 1) * HEAD_DIM)
        o = _pv_stage(s_ref, m, [v_ref[0, p, half, :] for v_ref in v_refs])
        gate = gt_ref[0, p, half, :].astype(jnp.float32)
        row0 = pl.multiple_of(p * PAIR_W + hh * HEAD_DIM, HEAD_DIM)
        zt_ref[pl.ds(row0, HEAD_DIM), :] = (o * gate).astype(jnp.bfloat16)

    _head_pipeline(N_HEADS, scores_fn, pv_fn, (sa_ref, sb_ref))

    h = _out_and_residual(zt_ref, wout_ref, x_ref[0])
    h_ref[0] = h
    hn = h * _rms_scale(h)
    xkv = (hn * kvg_ref[...]).astype(jnp.bfloat16)
    xb = (hn * bg_ref[...]).astype(jnp.bfloat16)
    ksh_ref[0] = jnp.dot(xkv, kvwk_ref[...], preferred_element_type=jnp.float32).astype(jnp.bfloat16)
    vsht_ref[0] = lax.dot_general(kvwvt_ref[...], xkv, _NT,
                                  preferred_element_type=jnp.float32).astype(jnp.bfloat16)
    n_chunks = D_MODEL // FEAT_CHUNK
    for r in range(2 * n_chunks):
        w = bwt_ref[r * FEAT_CHUNK:(r + 1) * FEAT_CHUNK, :]
        yt = lax.dot_general(w, xb, _NT, preferred_element_type=jnp.float32)
        which, c = divmod(r, n_chunks)
        rws = slice(c * FEAT_CHUNK, (c + 1) * FEAT_CHUNK)
        if which == 0:
            qbt_ref[0, rws, :] = yt.astype(jnp.bfloat16)
        else:
            gbt_ref[0, rws, :] = _silu(yt).astype(jnp.bfloat16)


def _const_spec(shape):
    return pl.BlockSpec(shape, lambda bi, i: (0,) * len(shape), pipeline_mode=pl.Buffered(1))


def _attn_a(x, qt, k, vt, gt, bvec, wout, kvg, bg, kvwk, kvwvt, bwt):
    b, s, d = x.shape
    qt4 = qt.reshape(b, N_PAIRS, PAIR_W, s)
    vt4 = vt.reshape(b, N_PAIRS, PAIR_W, s)
    gt4 = gt.reshape(b, N_PAIRS, PAIR_W, s)
    back = A_KBLOCKS - 1

    def kspec(j):
        return pl.BlockSpec((1, N_PAIRS, TQ, PAIR_W),
                            lambda bi, i: (bi, 0, jnp.maximum(i - back + j, 0), 0))

    def vspec(j):
        return pl.BlockSpec((1, N_PAIRS, PAIR_W, TQ),
                            lambda bi, i: (bi, 0, 0, jnp.maximum(i - back + j, 0)))

    feat_spec = pl.BlockSpec((1, N_PAIRS, PAIR_W, TQ), lambda bi, i: (bi, 0, 0, i))
    featout_spec = pl.BlockSpec((1, d, TQ), lambda bi, i: (bi, 0, i))
    feat = jax.ShapeDtypeStruct((b, d, s), jnp.bfloat16)
    return pl.pallas_call(
        _attn_a_kernel,
        grid=(b, s // TQ),
        in_specs=[
            pl.BlockSpec((1, TQ, d), lambda bi, i: (bi, i, 0)),
            feat_spec,
            kspec(0), kspec(1), kspec(2),
            vspec(0), vspec(1), vspec(2),
            feat_spec,
            _const_spec(bvec.shape),
            _const_spec(wout.shape),
            _const_spec(kvg.shape),
            _const_spec(bg.shape),
            _const_spec(kvwk.shape),
            _const_spec(kvwvt.shape),
            _const_spec(bwt.shape),
        ],
        out_specs=[
            pl.BlockSpec((1, TQ, d), lambda bi, i: (bi, i, 0)),
            pl.BlockSpec((1, TQ, PAIR_W), lambda bi, i: (bi, i, 0)),
            pl.BlockSpec((1, PAIR_W, TQ), lambda bi, i: (bi, 0, i)),
            featout_spec,
            featout_spec,
        ],
        out_shape=[
            jax.ShapeDtypeStruct((b, s, d), jnp.float32),
            jax.ShapeDtypeStruct((b, s, PAIR_W), jnp.bfloat16),
            jax.ShapeDtypeStruct((b, PAIR_W, s), jnp.bfloat16),
            feat, feat,
        ],
        scratch_shapes=[pltpu.VMEM((d, TQ), jnp.bfloat16),
                        pltpu.VMEM((N_HEADS * A_KBLOCKS + 1, TQ, TQ), jnp.float32),
                        pltpu.VMEM((A_KBLOCKS * TQ, TQ), jnp.float32),
                        pltpu.VMEM((A_KBLOCKS * TQ, TQ), jnp.float32)],
        compiler_params=pltpu.CompilerParams(
            dimension_semantics=("arbitrary", "arbitrary"), vmem_limit_bytes=VMEM_LIMIT),
        name="attn_a",
    )(x, qt4, k, k, k, vt4, vt4, vt4, gt4, bvec, wout, kvg, bg, kvwk, kvwvt, bwt)


def _attn_b_kernel(h_ref, qt_ref, kp_ref, ko_ref, vp_ref, vo_ref, gt_ref, vecp_ref, veco_ref,
                   sink_ref, wout_ref, fg_ref, out_ref, zt_ref, biasp_ref, biaso_ref, sa_ref, sb_ref):
    i = pl.program_id(1)

    @pl.when((pl.program_id(0) == 0) & (i == 0))
    def _():
        lag_prev = _chunk_iota((B_PREV, TQ), 1) - _chunk_iota((B_PREV, TQ), 0) + B_LEFT_CHUNKS
        _fill_bias_slabs(vecp_ref, biasp_ref, B_PREV, (lag_prev <= B_LEFT_CHUNKS,))
        biasp_ref[N_HEADS] = jnp.full((B_PREV, TQ), MASKED, jnp.float32)
        lag_own = _chunk_iota((TQ, TQ), 1) - _chunk_iota((TQ, TQ), 0)
        _fill_bias_slabs(veco_ref, biaso_ref, TQ, ((lag_own >= 0) & (lag_own <= B_LEFT_CHUNKS),))

    rows = lax.broadcasted_iota(jnp.int32, (PAIR_W, TQ), 0)
    pairs_per_kv = N_PAIRS // B_KV_HEADS

    def scores_fn(p, hh, s_ref):
        head = 2 * p + hh
        qh = qt_ref[0, p, hh * HEAD_DIM:(hh + 1) * HEAD_DIM, :]
        kv0 = jnp.where(p >= pairs_per_kv, HEAD_DIM, 0)
        kv_rows = (rows >= kv0) & (rows < kv0 + HEAD_DIM)
        qz = jnp.where(kv_rows, jnp.concatenate([qh, qh], axis=0), jnp.zeros((PAIR_W, TQ), qh.dtype))
        slab = jnp.where(i >= 1, head, N_HEADS)
        k_tiles = [kp_ref[0, TQ - B_PREV:, :], ko_ref[0]]
        return _scores_stage(k_tiles, qz, [biasp_ref[slab], biaso_ref[head]], s_ref)

    def pv_fn(p, hh, s_ref, m):
        head = 2 * p + hh
        kv0 = pl.multiple_of(jnp.where(p >= pairs_per_kv, HEAD_DIM, 0), HEAD_DIM)
        v_tiles = [vp_ref[0, pl.ds(kv0, HEAD_DIM), TQ - B_PREV:],
                   vo_ref[0, pl.ds(kv0, HEAD_DIM), :]]
        o = _pv_stage(s_ref, m, v_tiles, extra_logit=sink_ref[head])
        gate = gt_ref[0, p, hh * HEAD_DIM:(hh + 1) * HEAD_DIM, :].astype(jnp.float32)
        row0 = pl.multiple_of(p * PAIR_W + hh * HEAD_DIM, HEAD_DIM)
        zt_ref[pl.ds(row0, HEAD_DIM), :] = (o * gate).astype(jnp.bfloat16)

    _head_pipeline(N_HEADS, scores_fn, pv_fn, (sa_ref, sb_ref))

    h2 = _out_and_residual(zt_ref, wout_ref, h_ref[0])
    out_ref[0] = h2 * _rms_scale(h2) * fg_ref[...]


def _attn_b(h, qbt, ksh, vsht, gbt, vecp, veco, sinks, wout, fg):
    b, s, d = h.shape
    qt4 = qbt.reshape(b, N_PAIRS, PAIR_W, s)
    gt4 = gbt.reshape(b, N_PAIRS, PAIR_W, s)
    feat_spec = pl.BlockSpec((1, N_PAIRS, PAIR_W, TQ), lambda bi, i: (bi, 0, 0, i))
    return pl.pallas_call(
        _attn_b_kernel,
        grid=(b, s // TQ),
        in_specs=[
            pl.BlockSpec((1, TQ, d), lambda bi, i: (bi, i, 0)),
            feat_spec,
            pl.BlockSpec((1, TQ, PAIR_W), lambda bi, i: (bi, jnp.maximum(i - 1, 0), 0)),
            pl.BlockSpec((1, TQ, PAIR_W), lambda bi, i: (bi, i, 0)),
            pl.BlockSpec((1, PAIR_W, TQ), lambda bi, i: (bi, 0, jnp.maximum(i - 1, 0))),
            pl.BlockSpec((1, PAIR_W, TQ), lambda bi, i: (bi, 0, i)),
            feat_spec,
            _const_spec(vecp.shape),
            _const_spec(veco.shape),
            _const_spec(sinks.shape),
            _const_spec(wout.shape),
            _const_spec(fg.shape),
        ],
        out_specs=pl.BlockSpec((1, TQ, d), lambda bi, i: (bi, i, 0)),
        out_shape=jax.ShapeDtypeStruct((b, s, d), jnp.float32),
        scratch_shapes=[pltpu.VMEM((d, TQ), jnp.bfloat16),
                        pltpu.VMEM((N_HEADS + 1, B_PREV, TQ), jnp.float32),
                        pltpu.VMEM((N_HEADS, TQ, TQ), jnp.float32),
                        pltpu.VMEM((B_PREV + TQ, TQ), jnp.float32),
                        pltpu.VMEM((B_PREV + TQ, TQ), jnp.float32)],
        compiler_params=pltpu.CompilerParams(
            dimension_semantics=("arbitrary", "arbitrary"), vmem_limit_bytes=VMEM_LIMIT),
        name="attn_b",
    )(h, qt4, ksh, ksh, vsht, vsht, gt4, vecp, veco, sinks, wout, fg)


def _slab_diff():
    u = jnp.arange(BIAS_PERIOD)
    return jnp.where(u < TQ, u, u - BIAS_PERIOD)


def _bias_vecs_a(rel_bias):
    back = (A_KBLOCKS - 1 - jnp.arange(A_KBLOCKS))[:, None] * TQ
    dist = back + _slab_diff()[None, :]
    idx = jnp.clip(dist, -A_REL_CLIP, A_REL_CLIP) + A_REL_CLIP
    vec = jnp.transpose(rel_bias[idx], (0, 2, 1)).astype(jnp.float32)
    return vec[:, :, None, :]


def _t5_bucket(rel):
    nb = T5_BUCKETS // 2
    max_exact = nb // 2
    ret = jnp.where(rel > 0, nb, 0)
    n = jnp.abs(rel)
    nf = jnp.maximum(n, 1).astype(jnp.float32)
    large = max_exact + (jnp.log(nf / max_exact) / math.log(T5_MAX_DIST / max_exact)
                         * (nb - max_exact)).astype(jnp.int32)
    large = jnp.minimum(large, nb - 1)
    return ret + jnp.where(n < max_exact, n, large)


def _bias_vecs_b(t5_table):
    rel_own = -_slab_diff()
    rel = jnp.stack([rel_own - B_PREV, rel_own])
    vec = jnp.transpose(t5_table[_t5_bucket(rel)], (0, 2, 1)).astype(jnp.float32)
    return vec[0:1, :, None, :], vec[1:2, :, None, :]


def kernel(x, a_norm, a_w_in, a_rel_bias, a_w_out, kv_norm, kv_w, t5_bias,
           b_norm, b_w_in, b_sinks, b_w_out, final_norm):
    assert a_norm.shape[0] == 1 and b_norm.shape[0] == 1, "one A layer then one B layer"
    bf = jnp.bfloat16
    scale = HEAD_DIM ** -0.5
    wq, wk, wv, wg = jnp.split(a_w_in[0], 4, axis=-1)
    wt_a = jnp.concatenate([(wq * scale).T, wv.T, wg.T], axis=0).astype(bf)
    qt, k, vt, gt = _proj_a(x, a_norm[0][None, :], wk.astype(bf), wt_a)

    wq_b, wg_b = jnp.split(b_w_in[0], 2, axis=-1)
    bwt = jnp.concatenate([(wq_b * scale).T, wg_b.T], axis=0).astype(bf)
    kvw_k, kvw_v = jnp.split(kv_w, 2, axis=-1)
    h, ksh, vsht, qbt, gbt = _attn_a(
        x, qt, k, vt, gt, _bias_vecs_a(a_rel_bias[0]), a_w_out[0].astype(bf),
        kv_norm[None, :], b_norm[0][None, :], kvw_k.astype(bf), kvw_v.T.astype(bf), bwt)

    vecp, veco = _bias_vecs_b(t5_bias)
    sinks = jnp.broadcast_to(b_sinks[0].astype(jnp.float32)[:, None, None], (N_HEADS, 1, TQ))
    return _attn_b(h, qbt, ksh, vsht, gbt, vecp, veco, sinks, b_w_out[0].astype(bf),
                   final_norm[None, :])
```

```python
import functools
import math

import jax
import jax.numpy as jnp
from jax import lax
from jax.experimental import pallas as pl
from jax.experimental.pallas import tpu as pltpu

D_MODEL = 1024
HEAD_DIM = 64
N_HEADS = D_MODEL // HEAD_DIM
N_PAIRS = N_HEADS // 2
PAIR_W = 2 * HEAD_DIM
CHUNK = 64
RMS_EPS = 1e-6
A_LEFT_CHUNKS = 8
A_REL_CLIP = 256
B_KV_HEADS = 2
B_GROUP = N_HEADS // B_KV_HEADS
B_LEFT_CHUNKS = 2
T5_BUCKETS = 32
T5_MAX_DIST = 128

TQ = 256
A_KBLOCKS = A_LEFT_CHUNKS * CHUNK // TQ + 1
B_PREV = B_LEFT_CHUNKS * CHUNK
TM_PROJ = 512
FEAT_CHUNK = 512
MASKED = -1e30
BIAS_PERIOD = 2 * TQ
BF16_ROWS = 16
LOG2E = math.log2(math.e)
N_SCORE_BUFS = 4
VMEM_LIMIT = 56 * 1024 * 1024

_NT = (((1,), (1,)), ((), ()))
_TN = (((0,), (0,)), ((), ()))


def _rms_scale(xf):
    return lax.rsqrt(jnp.mean(xf * xf, axis=-1, keepdims=True) + RMS_EPS)


def _silu(v):
    return v * jax.nn.sigmoid(v)


def _proj_a_kernel(x_ref, g_ref, wk_ref, wt_ref, qt_ref, k_ref, vt_ref, gt_ref):
    xf = x_ref[0]
    xn = (xf * _rms_scale(xf) * g_ref[...]).astype(jnp.bfloat16)
    k = jnp.dot(xn, wk_ref[...], preferred_element_type=jnp.float32)
    for p in range(N_PAIRS):
        k_ref[0, p] = k[:, p * PAIR_W:(p + 1) * PAIR_W].astype(jnp.bfloat16)
    n_chunks = D_MODEL // FEAT_CHUNK
    for r in range(3 * n_chunks):
        w = wt_ref[r * FEAT_CHUNK:(r + 1) * FEAT_CHUNK, :]
        yt = lax.dot_general(w, xn, _NT, preferred_element_type=jnp.float32)
        which, c = divmod(r, n_chunks)
        rows = slice(c * FEAT_CHUNK, (c + 1) * FEAT_CHUNK)
        if which == 0:
            qt_ref[0, rows, :] = yt.astype(jnp.bfloat16)
        elif which == 1:
            vt_ref[0, rows, :] = yt.astype(jnp.bfloat16)
        else:
            gt_ref[0, rows, :] = _silu(yt).astype(jnp.bfloat16)


def _proj_a(x, gain, wk, wt):
    b, s, d = x.shape
    feat = jax.ShapeDtypeStruct((b, d, s), jnp.bfloat16)
    return pl.pallas_call(
        _proj_a_kernel,
        grid=(b, s // TM_PROJ),
        in_specs=[
            pl.BlockSpec((1, TM_PROJ, d), lambda bi, i: (bi, i, 0)),
            pl.BlockSpec((1, d), lambda bi, i: (0, 0)),
            pl.BlockSpec((d, d), lambda bi, i: (0, 0)),
            pl.BlockSpec((3 * d, d), lambda bi, i: (0, 0)),
        ],
        out_specs=[
            pl.BlockSpec((1, d, TM_PROJ), lambda bi, i: (bi, 0, i)),
            pl.BlockSpec((1, N_PAIRS, TM_PROJ, PAIR_W), lambda bi, i: (bi, 0, i, 0)),
            pl.BlockSpec((1, d, TM_PROJ), lambda bi, i: (bi, 0, i)),
            pl.BlockSpec((1, d, TM_PROJ), lambda bi, i: (bi, 0, i)),
        ],
        out_shape=[feat, jax.ShapeDtypeStruct((b, N_PAIRS, s, PAIR_W), jnp.bfloat16), feat, feat],
        compiler_params=pltpu.CompilerParams(
            dimension_semantics=("arbitrary", "arbitrary"), vmem_limit_bytes=VMEM_LIMIT),
        name="proj_a",
    )(x, gain, wk, wt)


def _chunk_iota(shape, axis):
    return lax.shift_right_logical(lax.broadcasted_iota(jnp.int32, shape, axis), CHUNK.bit_length() - 1)


def _fill_bias_slabs(vec_ref, bias_ref, n_keys, valids):
    n = len(valids)

    def head_body(h, carry):
        for j, valid in enumerate(valids):
            v = jnp.broadcast_to(vec_ref[j, h], (n_keys, BIAS_PERIOD))
            t = pltpu.roll(v, 0, 1, stride=1, stride_axis=0)[:, :TQ]
            if valid is not None:
                t = jnp.where(valid, t, MASKED)
            bias_ref[h * n + j] = t
        return carry

    lax.fori_loop(0, N_HEADS, head_body, 0)


def _scores_stage(k_tiles, qz, bias_tiles, s_ref):
    m, row = None, 0
    for kt, bt in zip(k_tiles, bias_tiles):
        s = jnp.dot(kt, qz, preferred_element_type=jnp.float32) + bt
        s_ref[row:row + s.shape[0], :] = s
        mj = jnp.max(s, axis=0, keepdims=True)
        m = mj if m is None else jnp.maximum(m, mj)
        row += s.shape[0]
    return m


def _pv_stage(s_ref, m, v_tiles, extra_logit=None):
    if extra_logit is not None:
        m = jnp.maximum(m, extra_logit)
    acc, row = None, 0
    for vt in v_tiles:
        n = vt.shape[1]
        p = jnp.exp2(s_ref[row:row + n, :] - m).astype(jnp.bfloat16)
        v_ones = jnp.concatenate([vt, jnp.ones((BF16_ROWS, n), vt.dtype)], axis=0)
        part = jnp.dot(v_ones, p, preferred_element_type=jnp.float32)
        acc = part if acc is None else acc + part
        row += n
    l = acc[HEAD_DIM:HEAD_DIM + 1, :]
    if extra_logit is not None:
        l = l + jnp.exp2(extra_logit - m)
    return acc[:HEAD_DIM, :] / l


def _head_pipeline(n_heads, scores_fn, pv_fn, s_refs):
    x_refs, y_refs = s_refs[:2], s_refs[2:]
    n_pairs = n_heads // 2

    def group(pair, src, m_src, dst, next_pair):
        m_dst = None
        if next_pair is not None:
            m_dst = scores_fn(next_pair, 0, dst[0]), scores_fn(next_pair, 1, dst[1])
        pv_fn(pair, 0, src[0], m_src[0])
        pv_fn(pair, 1, src[1], m_src[1])
        return m_dst

    m = scores_fn(0, 0, x_refs[0]), scores_fn(0, 1, x_refs[1])
    for pair in range(n_pairs):
        src, dst = (x_refs, y_refs) if pair % 2 == 0 else (y_refs, x_refs)
        m = group(pair, src, m, dst, pair + 1 if pair + 1 < n_pairs else None)


def _out_and_residual(zt_ref, wout_ref, res):
    y = lax.dot_general(zt_ref[...], wout_ref[...], _TN, preferred_element_type=jnp.float32)
    return res + y


def _attn_a_kernel(x_ref, qt_ref, k0_ref, k1_ref, k2_ref, v0_ref, v1_ref, v2_ref, gt_ref,
                   bvec_ref, wout_ref, kvg_ref, bg_ref, kvwk_ref, kvwvt_ref, bwt_ref,
                   h_ref, ksh_ref, vsht_ref, qbt_ref, gbt_ref, zt_ref, bias_ref, *s_refs):
    i = pl.program_id(1)

    @pl.when((pl.program_id(0) == 0) & (i == 0))
    def _():
        kch = _chunk_iota((TQ, TQ), 0)
        qch = _chunk_iota((TQ, TQ), 1)
        _fill_bias_slabs(bvec_ref, bias_ref, TQ, (kch >= qch, None, kch <= qch))
        bias_ref[N_HEADS * A_KBLOCKS] = jnp.full((TQ, TQ), MASKED, jnp.float32)

    k_refs = (k0_ref, k1_ref, k2_ref)
    v_refs = (v0_ref, v1_ref, v2_ref)
    n_slabs = N_HEADS * A_KBLOCKS
    rows = lax.broadcasted_iota(jnp.int32, (PAIR_W, TQ), 0)

    def scores_fn(p, hh, s_ref):
        q2 = qt_ref[0, p]
        own_rows = rows < HEAD_DIM if hh == 0 else rows >= HEAD_DIM
        qz = jnp.where(own_rows, q2, jnp.zeros_like(q2))
        k_tiles, bias_tiles = [], []
        for j in range(A_KBLOCKS):
            in_seq = i - (A_KBLOCKS - 1) + j >= 0
            slab = jnp.where(in_seq, (2 * p + hh) * A_KBLOCKS + j, n_slabs)
            k_tiles.append(k_refs[j][0, p])
            bias_tiles.append(bias_ref[slab])
        return _scores_stage(k_tiles, qz, bias_tiles, s_ref)

    def pv_fn(p, hh, s_ref, m):
        half = slice(hh * HEAD_DIM, (hh + 1) * HEAD_DIM)
        o = _pv_stage(s_ref, m, [v_ref[0, p, half, :] for v_ref in v_refs])
        gate = gt_ref[0, p, half, :].astype(jnp.float32)
        row0 = pl.multiple_of(p * PAIR_W + hh * HEAD_DIM, HEAD_DIM)
        zt_ref[pl.ds(row0, HEAD_DIM), :] = (o * gate).astype(jnp.bfloat16)

    _head_pipeline(N_HEADS, scores_fn, pv_fn, s_refs)

    h = _out_and_residual(zt_ref, wout_ref, x_ref[0])
    h_ref[0] = h
    hn = h * _rms_scale(h)
    xkv = (hn * kvg_ref[...]).astype(jnp.bfloat16)
    xb = (hn * bg_ref[...]).astype(jnp.bfloat16)
    ksh_ref[0] = jnp.dot(xkv, kvwk_ref[...], preferred_element_type=jnp.float32).astype(jnp.bfloat16)
    vsht_ref[0] = lax.dot_general(kvwvt_ref[...], xkv, _NT,
                                  preferred_element_type=jnp.float32).astype(jnp.bfloat16)
    n_chunks = D_MODEL // FEAT_CHUNK
    for r in range(2 * n_chunks):
        w = bwt_ref[r * FEAT_CHUNK:(r + 1) * FEAT_CHUNK, :]
        yt = lax.dot_general(w, xb, _NT, preferred_element_type=jnp.float32)
        which, c = divmod(r, n_chunks)
        rws = slice(c * FEAT_CHUNK, (c + 1) * FEAT_CHUNK)
        if which == 0:
            qbt_ref[0, rws, :] = yt.astype(jnp.bfloat16)
        else:
            gbt_ref[0, rws, :] = _silu(yt).astype(jnp.bfloat16)


def _const_spec(shape):
    return pl.BlockSpec(shape, lambda bi, i: (0,) * len(shape), pipeline_mode=pl.Buffered(1))


def _attn_a(x, qt, k, vt, gt, bvec, wout, kvg, bg, kvwk, kvwvt, bwt):
    b, s, d = x.shape
    qt4 = qt.reshape(b, N_PAIRS, PAIR_W, s)
    vt4 = vt.reshape(b, N_PAIRS, PAIR_W, s)
    gt4 = gt.reshape(b, N_PAIRS, PAIR_W, s)
    back = A_KBLOCKS - 1

    def kspec(j):
        return pl.BlockSpec((1, N_PAIRS, TQ, PAIR_W),
                            lambda bi, i: (bi, 0, jnp.maximum(i - back + j, 0), 0))

    def vspec(j):
        return pl.BlockSpec((1, N_PAIRS, PAIR_W, TQ),
                            lambda bi, i: (bi, 0, 0, jnp.maximum(i - back + j, 0)))

    feat_spec = pl.BlockSpec((1, N_PAIRS, PAIR_W, TQ), lambda bi, i: (bi, 0, 0, i))
    featout_spec = pl.BlockSpec((1, d, TQ), lambda bi, i: (bi, 0, i))
    feat = jax.ShapeDtypeStruct((b, d, s), jnp.bfloat16)
    return pl.pallas_call(
        _attn_a_kernel,
        grid=(b, s // TQ),
        in_specs=[
            pl.BlockSpec((1, TQ, d), lambda bi, i: (bi, i, 0)),
            feat_spec,
            kspec(0), kspec(1), kspec(2),
            vspec(0), vspec(1), vspec(2),
            feat_spec,
            _const_spec(bvec.shape),
            _const_spec(wout.shape),
            _const_spec(kvg.shape),
            _const_spec(bg.shape),
            _const_spec(kvwk.shape),
            _const_spec(kvwvt.shape),
            _const_spec(bwt.shape),
        ],
        out_specs=[
            pl.BlockSpec((1, TQ, d), lambda bi, i: (bi, i, 0)),
            pl.BlockSpec((1, TQ, PAIR_W), lambda bi, i: (bi, i, 0)),
            pl.BlockSpec((1, PAIR_W, TQ), lambda bi, i: (bi, 0, i)),
            featout_spec,
            featout_spec,
        ],
        out_shape=[
            jax.ShapeDtypeStruct((b, s, d), jnp.float32),
            jax.ShapeDtypeStruct((b, s, PAIR_W), jnp.bfloat16),
            jax.ShapeDtypeStruct((b, PAIR_W, s), jnp.bfloat16),
            feat, feat,
        ],
        scratch_shapes=[pltpu.VMEM((d, TQ), jnp.bfloat16),
                        pltpu.VMEM((N_HEADS * A_KBLOCKS + 1, TQ, TQ), jnp.float32),
                        *[pltpu.VMEM((A_KBLOCKS * TQ, TQ), jnp.float32)] * N_SCORE_BUFS],
        compiler_params=pltpu.CompilerParams(
            dimension_semantics=("arbitrary", "arbitrary"), vmem_limit_bytes=VMEM_LIMIT),
        name="attn_a",
    )(x, qt4, k, k, k, vt4, vt4, vt4, gt4, bvec, wout, kvg, bg, kvwk, kvwvt, bwt)


def _attn_b_kernel(h_ref, qt_ref, kp_ref, ko_ref, vp_ref, vo_ref, gt_ref, vecp_ref, veco_ref,
                   sink_ref, wout_ref, fg_ref, out_ref, zt_ref, biasp_ref, biaso_ref, *s_refs):
    i = pl.program_id(1)

    @pl.when((pl.program_id(0) == 0) & (i == 0))
    def _():
        lag_prev = _chunk_iota((B_PREV, TQ), 1) - _chunk_iota((B_PREV, TQ), 0) + B_LEFT_CHUNKS
        _fill_bias_slabs(vecp_ref, biasp_ref, B_PREV, (lag_prev <= B_LEFT_CHUNKS,))
        biasp_ref[N_HEADS] = jnp.full((B_PREV, TQ), MASKED, jnp.float32)
        lag_own = _chunk_iota((TQ, TQ), 1) - _chunk_iota((TQ, TQ), 0)
        _fill_bias_slabs(veco_ref, biaso_ref, TQ, ((lag_own >= 0) & (lag_own <= B_LEFT_CHUNKS),))

    rows = lax.broadcasted_iota(jnp.int32, (PAIR_W, TQ), 0)
    pairs_per_kv = N_PAIRS // B_KV_HEADS

    def scores_fn(p, hh, s_ref):
        head = 2 * p + hh
        qh = qt_ref[0, p, hh * HEAD_DIM:(hh + 1) * HEAD_DIM, :]
        kv0 = jnp.where(p >= pairs_per_kv, HEAD_DIM, 0)
        kv_rows = (rows >= kv0) & (rows < kv0 + HEAD_DIM)
        qz = jnp.where(kv_rows, jnp.concatenate([qh, qh], axis=0), jnp.zeros((PAIR_W, TQ), qh.dtype))
        slab = jnp.where(i >= 1, head, N_HEADS)
        k_tiles = [kp_ref[0, TQ - B_PREV:, :], ko_ref[0]]
        return _scores_stage(k_tiles, qz, [biasp_ref[slab], biaso_ref[head]], s_ref)

    def pv_fn(p, hh, s_ref, m):
        head = 2 * p + hh
        kv0 = pl.multiple_of(jnp.where(p >= pairs_per_kv, HEAD_DIM, 0), HEAD_DIM)
        v_tiles = [vp_ref[0, pl.ds(kv0, HEAD_DIM), TQ - B_PREV:],
                   vo_ref[0, pl.ds(kv0, HEAD_DIM), :]]
        o = _pv_stage(s_ref, m, v_tiles, extra_logit=sink_ref[head])
        gate = gt_ref[0, p, hh * HEAD_DIM:(hh + 1) * HEAD_DIM, :].astype(jnp.float32)
        row0 = pl.multiple_of(p * PAIR_W + hh * HEAD_DIM, HEAD_DIM)
        zt_ref[pl.ds(row0, HEAD_DIM), :] = (o * gate).astype(jnp.bfloat16)

    _head_pipeline(N_HEADS, scores_fn, pv_fn, s_refs)

    h2 = _out_and_residual(zt_ref, wout_ref, h_ref[0])
    out_ref[0] = h2 * _rms_scale(h2) * fg_ref[...]


def _attn_b(h, qbt, ksh, vsht, gbt, vecp, veco, sinks, wout, fg):
    b, s, d = h.shape
    qt4 = qbt.reshape(b, N_PAIRS, PAIR_W, s)
    gt4 = gbt.reshape(b, N_PAIRS, PAIR_W, s)
    feat_spec = pl.BlockSpec((1, N_PAIRS, PAIR_W, TQ), lambda bi, i: (bi, 0, 0, i))
    return pl.pallas_call(
        _attn_b_kernel,
        grid=(b, s // TQ),
        in_specs=[
            pl.BlockSpec((1, TQ, d), lambda bi, i: (bi, i, 0)),
            feat_spec,
            pl.BlockSpec((1, TQ, PAIR_W), lambda bi, i: (bi, jnp.maximum(i - 1, 0), 0)),
            pl.BlockSpec((1, TQ, PAIR_W), lambda bi, i: (bi, i, 0)),
            pl.BlockSpec((1, PAIR_W, TQ), lambda bi, i: (bi, 0, jnp.maximum(i - 1, 0))),
            pl.BlockSpec((1, PAIR_W, TQ), lambda bi, i: (bi, 0, i)),
            feat_spec,
            _const_spec(vecp.shape),
            _const_spec(veco.shape),
            _const_spec(sinks.shape),
            _const_spec(wout.shape),
            _const_spec(fg.shape),
        ],
        out_specs=pl.BlockSpec((1, TQ, d), lambda bi, i: (bi, i, 0)),
        out_shape=jax.ShapeDtypeStruct((b, s, d), jnp.float32),
        scratch_shapes=[pltpu.VMEM((d, TQ), jnp.bfloat16),
                        pltpu.VMEM((N_HEADS + 1, B_PREV, TQ), jnp.float32),
                        pltpu.VMEM((N_HEADS, TQ, TQ), jnp.float32),
                        *[pltpu.VMEM((B_PREV + TQ, TQ), jnp.float32)] * N_SCORE_BUFS],
        compiler_params=pltpu.CompilerParams(
            dimension_semantics=("arbitrary", "arbitrary"), vmem_limit_bytes=VMEM_LIMIT),
        name="attn_b",
    )(h, qt4, ksh, ksh, vsht, vsht, gt4, vecp, veco, sinks, wout, fg)


def _slab_diff():
    u = jnp.arange(BIAS_PERIOD)
    return jnp.where(u < TQ, u, u - BIAS_PERIOD)


def _bias_vecs_a(rel_bias):
    back = (A_KBLOCKS - 1 - jnp.arange(A_KBLOCKS))[:, None] * TQ
    dist = back + _slab_diff()[None, :]
    idx = jnp.clip(dist, -A_REL_CLIP, A_REL_CLIP) + A_REL_CLIP
    vec = jnp.transpose(rel_bias[idx], (0, 2, 1)).astype(jnp.float32)
    return vec[:, :, None, :]


def _t5_bucket(rel):
    nb = T5_BUCKETS // 2
    max_exact = nb // 2
    ret = jnp.where(rel > 0, nb, 0)
    n = jnp.abs(rel)
    nf = jnp.maximum(n, 1).astype(jnp.float32)
    large = max_exact + (jnp.log(nf / max_exact) / math.log(T5_MAX_DIST / max_exact)
                         * (nb - max_exact)).astype(jnp.int32)
    large = jnp.minimum(large, nb - 1)
    return ret + jnp.where(n < max_exact, n, large)


def _bias_vecs_b(t5_table):
    rel_own = -_slab_diff()
    rel = jnp.stack([rel_own - B_PREV, rel_own])
    vec = jnp.transpose(t5_table[_t5_bucket(rel)], (0, 2, 1)).astype(jnp.float32)
    return vec[0:1, :, None, :], vec[1:2, :, None, :]


def kernel(x, a_norm, a_w_in, a_rel_bias, a_w_out, kv_norm, kv_w, t5_bias,
           b_norm, b_w_in, b_sinks, b_w_out, final_norm):
    assert a_norm.shape[0] == 1 and b_norm.shape[0] == 1, "one A layer then one B layer"
    bf = jnp.bfloat16
    scale = HEAD_DIM ** -0.5 * LOG2E
    wq, wk, wv, wg = jnp.split(a_w_in[0], 4, axis=-1)
    wt_a = jnp.concatenate([(wq * scale).T, wv.T, wg.T], axis=0).astype(bf)
    qt, k, vt, gt = _proj_a(x, a_norm[0][None, :], wk.astype(bf), wt_a)

    wq_b, wg_b = jnp.split(b_w_in[0], 2, axis=-1)
    bwt = jnp.concatenate([(wq_b * scale).T, wg_b.T], axis=0).astype(bf)
    kvw_k, kvw_v = jnp.split(kv_w, 2, axis=-1)
    h, ksh, vsht, qbt, gbt = _attn_a(
        x, qt, k, vt, gt, _bias_vecs_a(a_rel_bias[0] * LOG2E), a_w_out[0].astype(bf),
        kv_norm[None, :], b_norm[0][None, :], kvw_k.astype(bf), kvw_v.T.astype(bf), bwt)

    vecp, veco = _bias_vecs_b(t5_bias * LOG2E)
    sinks = jnp.broadcast_to((b_sinks[0].astype(jnp.float32) * LOG2E)[:, None, None], (N_HEADS, 1, TQ))
    return _attn_b(h, qbt, ksh, vsht, gbt, vecp, veco, sinks, b_w_out[0].astype(bf),
                   final_norm[None, :])
```

```python
import functools
import math

import jax
import jax.numpy as jnp
from jax import lax
from jax.experimental import pallas as pl
from jax.experimental.pallas import tpu as pltpu

D_MODEL = 1024
HEAD_DIM = 64
N_HEADS = D_MODEL // HEAD_DIM
N_PAIRS = N_HEADS // 2
PAIR_W = 2 * HEAD_DIM
CHUNK = 64
RMS_EPS = 1e-6
A_LEFT_CHUNKS = 8
A_REL_CLIP = 256
B_KV_HEADS = 2
B_GROUP = N_HEADS // B_KV_HEADS
B_LEFT_CHUNKS = 2
T5_BUCKETS = 32
T5_MAX_DIST = 128

TQ = 256
A_KBLOCKS = A_LEFT_CHUNKS * CHUNK // TQ + 1
B_PREV = B_LEFT_CHUNKS * CHUNK
TM_PROJ = 512
FEAT_CHUNK = 512
MASKED = -1e30
BIAS_PERIOD = 2 * TQ
BF16_ROWS = 16
LOG2E = math.log2(math.e)
N_SCORE_BUFS = 4
VMEM_LIMIT = 56 * 1024 * 1024

_NT = (((1,), (1,)), ((), ()))
_TN = (((0,), (0,)), ((), ()))


def _rms_scale(xf):
    return lax.rsqrt(jnp.mean(xf * xf, axis=-1, keepdims=True) + RMS_EPS)


def _silu(v):
    return v * jax.nn.sigmoid(v)


def _proj_a_kernel(x_ref, g_ref, wt_ref, qt_ref, k_ref, vt_ref, gt_ref):
    xf = x_ref[0]
    xn = (xf * _rms_scale(xf) * g_ref[...]).astype(jnp.bfloat16)
    k = lax.dot_general(xn, wt_ref[D_MODEL:2 * D_MODEL, :], _NT,
                        preferred_element_type=jnp.float32)
    for p in range(N_PAIRS):
        k_ref[0, p] = k[:, p * PAIR_W:(p + 1) * PAIR_W].astype(jnp.bfloat16)
    for out_ref, base, act in ((qt_ref, 0, None), (vt_ref, 2 * D_MODEL, None), (gt_ref, 3 * D_MODEL, _silu)):
        for c in range(0, D_MODEL, FEAT_CHUNK):
            w = wt_ref[base + c:base + c + FEAT_CHUNK, :]
            yt = lax.dot_general(w, xn, _NT, preferred_element_type=jnp.float32)
            out_ref[0, c:c + FEAT_CHUNK, :] = (yt if act is None else act(yt)).astype(jnp.bfloat16)


def _proj_a(x, gain, wt):
    b, s, d = x.shape
    feat = jax.ShapeDtypeStruct((b, d, s), jnp.bfloat16)
    return pl.pallas_call(
        _proj_a_kernel,
        grid=(b, s // TM_PROJ),
        in_specs=[
            pl.BlockSpec((1, TM_PROJ, d), lambda bi, i: (bi, i, 0)),
            _const_spec(gain.shape),
            _const_spec(wt.shape),
        ],
        out_specs=[
            pl.BlockSpec((1, d, TM_PROJ), lambda bi, i: (bi, 0, i)),
            pl.BlockSpec((1, N_PAIRS, TM_PROJ, PAIR_W), lambda bi, i: (bi, 0, i, 0)),
            pl.BlockSpec((1, d, TM_PROJ), lambda bi, i: (bi, 0, i)),
            pl.BlockSpec((1, d, TM_PROJ), lambda bi, i: (bi, 0, i)),
        ],
        out_shape=[feat, jax.ShapeDtypeStruct((b, N_PAIRS, s, PAIR_W), jnp.bfloat16), feat, feat],
        compiler_params=pltpu.CompilerParams(
            dimension_semantics=("arbitrary", "arbitrary"), vmem_limit_bytes=VMEM_LIMIT),
        name="proj_a",
    )(x, gain, wt)


def _chunk_iota(shape, axis):
    return lax.shift_right_logical(lax.broadcasted_iota(jnp.int32, shape, axis), CHUNK.bit_length() - 1)


def _fill_bias_slabs(vec_ref, bias_ref, n_keys, valids):
    n = len(valids)

    def head_body(h, carry):
        for j, valid in enumerate(valids):
            v = jnp.broadcast_to(vec_ref[j, h], (n_keys, BIAS_PERIOD))
            t = pltpu.roll(v, 0, 1, stride=1, stride_axis=0)[:, :TQ]
            if valid is not None:
                t = jnp.where(valid, t, MASKED)
            bias_ref[h * n + j] = t
        return carry

    lax.fori_loop(0, N_HEADS, head_body, 0)


def _scores_stage(k_tiles, qz, bias_tiles, s_ref):
    m, row = None, 0
    for kt, bt in zip(k_tiles, bias_tiles):
        s = jnp.dot(kt, qz, preferred_element_type=jnp.float32) + bt
        s_ref[row:row + s.shape[0], :] = s
        mj = jnp.max(s, axis=0, keepdims=True)
        m = mj if m is None else jnp.maximum(m, mj)
        row += s.shape[0]
    return m


def _pv_stage(s_ref, m, v_tiles, extra_logit=None):
    if extra_logit is not None:
        m = jnp.maximum(m, extra_logit)
    acc, row = None, 0
    for vt in v_tiles:
        n = vt.shape[1]
        p = jnp.exp2(s_ref[row:row + n, :] - m).astype(jnp.bfloat16)
        v_ones = jnp.concatenate([vt, jnp.ones((BF16_ROWS, n), vt.dtype)], axis=0)
        part = jnp.dot(v_ones, p, preferred_element_type=jnp.float32)
        acc = part if acc is None else acc + part
        row += n
    l = acc[HEAD_DIM:HEAD_DIM + 1, :]
    if extra_logit is not None:
        l = l + jnp.exp2(extra_logit - m)
    return acc[:HEAD_DIM, :] / l


def _head_pipeline(n_heads, scores_fn, pv_fn, s_refs):
    x_refs, y_refs = s_refs[:2], s_refs[2:]
    n_pairs = n_heads // 2

    def group(pair, src, m_src, dst, next_pair):
        m_dst = None
        if next_pair is not None:
            m_dst = scores_fn(next_pair, 0, dst[0]), scores_fn(next_pair, 1, dst[1])
        pv_fn(pair, 0, src[0], m_src[0])
        pv_fn(pair, 1, src[1], m_src[1])
        return m_dst

    m = scores_fn(0, 0, x_refs[0]), scores_fn(0, 1, x_refs[1])
    for pair in range(n_pairs):
        src, dst = (x_refs, y_refs) if pair % 2 == 0 else (y_refs, x_refs)
        m = group(pair, src, m, dst, pair + 1 if pair + 1 < n_pairs else None)


def _out_and_residual(zt_ref, wout_ref, res):
    y = lax.dot_general(zt_ref[...], wout_ref[...], _TN, preferred_element_type=jnp.float32)
    return res + y


def _attn_a_kernel(x_ref, qt_ref, k0_ref, k1_ref, k2_ref, v0_ref, v1_ref, v2_ref, gt_ref,
                   bvec_ref, wout_ref, kvg_ref, bg_ref, kvwk_ref, kvwvt_ref, bwt_ref,
                   h_ref, ksh_ref, vsht_ref, qbt_ref, gbt_ref, zt_ref, bias_ref, *s_refs):
    i = pl.program_id(1)

    @pl.when((pl.program_id(0) == 0) & (i == 0))
    def _():
        kch = _chunk_iota((TQ, TQ), 0)
        qch = _chunk_iota((TQ, TQ), 1)
        _fill_bias_slabs(bvec_ref, bias_ref, TQ, (kch >= qch, None, kch <= qch))
        bias_ref[N_HEADS * A_KBLOCKS] = jnp.full((TQ, TQ), MASKED, jnp.float32)

    k_refs = (k0_ref, k1_ref, k2_ref)
    v_refs = (v0_ref, v1_ref, v2_ref)
    n_slabs = N_HEADS * A_KBLOCKS
    rows = lax.broadcasted_iota(jnp.int32, (PAIR_W, TQ), 0)

    def scores_fn(p, hh, s_ref):
        q2 = qt_ref[0, p]
        own_rows = rows < HEAD_DIM if hh == 0 else rows >= HEAD_DIM
        qz = jnp.where(own_rows, q2, jnp.zeros_like(q2))
        k_tiles, bias_tiles = [], []
        for j in range(A_KBLOCKS):
            in_seq = i - (A_KBLOCKS - 1) + j >= 0
            slab = jnp.where(in_seq, (2 * p + hh) * A_KBLOCKS + j, n_slabs)
            k_tiles.append(k_refs[j][0, p])
            bias_tiles.append(bias_ref[slab])
        return _scores_stage(k_tiles, qz, bias_tiles, s_ref)

    def pv_fn(p, hh, s_ref, m):
        half = slice(hh * HEAD_DIM, (hh + 1) * HEAD_DIM)
        o = _pv_stage(s_ref, m, [v_ref[0, p, half, :] for v_ref in v_refs])
        gate = gt_ref[0, p, half, :].astype(jnp.float32)
        row0 = pl.multiple_of(p * PAIR_W + hh * HEAD_DIM, HEAD_DIM)
        zt_ref[pl.ds(row0, HEAD_DIM), :] = (o * gate).astype(jnp.bfloat16)

    _head_pipeline(N_HEADS, scores_fn, pv_fn, s_refs)

    h = _out_and_residual(zt_ref, wout_ref, x_ref[0])
    h_ref[0] = h
    hn = h * _rms_scale(h)
    xkv = (hn * kvg_ref[...]).astype(jnp.bfloat16)
    xb = (hn * bg_ref[...]).astype(jnp.bfloat16)
    ksh_ref[0] = jnp.dot(xkv, kvwk_ref[...], preferred_element_type=jnp.float32).astype(jnp.bfloat16)
    vsht_ref[0] = lax.dot_general(kvwvt_ref[...], xkv, _NT,
                                  preferred_element_type=jnp.float32).astype(jnp.bfloat16)
    n_chunks = D_MODEL // FEAT_CHUNK
    for r in range(2 * n_chunks):
        w = bwt_ref[r * FEAT_CHUNK:(r + 1) * FEAT_CHUNK, :]
        yt = lax.dot_general(w, xb, _NT, preferred_element_type=jnp.float32)
        which, c = divmod(r, n_chunks)
        rws = slice(c * FEAT_CHUNK, (c + 1) * FEAT_CHUNK)
        if which == 0:
            qbt_ref[0, rws, :] = yt.astype(jnp.bfloat16)
        else:
            gbt_ref[0, rws, :] = _silu(yt).astype(jnp.bfloat16)


def _const_spec(shape):
    return pl.BlockSpec(shape, lambda bi, i: (0,) * len(shape), pipeline_mode=pl.Buffered(1))


def _attn_a(x, qt, k, vt, gt, bvec, wout, kvg, bg, kvwk, kvwvt, bwt):
    b, s, d = x.shape
    qt4 = qt.reshape(b, N_PAIRS, PAIR_W, s)
    vt4 = vt.reshape(b, N_PAIRS, PAIR_W, s)
    gt4 = gt.reshape(b, N_PAIRS, PAIR_W, s)
    back = A_KBLOCKS - 1

    def kspec(j):
        return pl.BlockSpec((1, N_PAIRS, TQ, PAIR_W),
                            lambda bi, i: (bi, 0, jnp.maximum(i - back + j, 0), 0))

    def vspec(j):
        return pl.BlockSpec((1, N_PAIRS, PAIR_W, TQ),
                            lambda bi, i: (bi, 0, 0, jnp.maximum(i - back + j, 0)))

    feat_spec = pl.BlockSpec((1, N_PAIRS, PAIR_W, TQ), lambda bi, i: (bi, 0, 0, i))
    featout_spec = pl.BlockSpec((1, d, TQ), lambda bi, i: (bi, 0, i))
    feat = jax.ShapeDtypeStruct((b, d, s), jnp.bfloat16)
    return pl.pallas_call(
        _attn_a_kernel,
        grid=(b, s // TQ),
        in_specs=[
            pl.BlockSpec((1, TQ, d), lambda bi, i: (bi, i, 0)),
            feat_spec,
            kspec(0), kspec(1), kspec(2),
            vspec(0), vspec(1), vspec(2),
            feat_spec,
            _const_spec(bvec.shape),
            _const_spec(wout.shape),
            _const_spec(kvg.shape),
            _const_spec(bg.shape),
            _const_spec(kvwk.shape),
            _const_spec(kvwvt.shape),
            _const_spec(bwt.shape),
        ],
        out_specs=[
            pl.BlockSpec((1, TQ, d), lambda bi, i: (bi, i, 0)),
            pl.BlockSpec((1, TQ, PAIR_W), lambda bi, i: (bi, i, 0)),
            pl.BlockSpec((1, PAIR_W, TQ), lambda bi, i: (bi, 0, i)),
            featout_spec,
            featout_spec,
        ],
        out_shape=[
            jax.ShapeDtypeStruct((b, s, d), jnp.float32),
            jax.ShapeDtypeStruct((b, s, PAIR_W), jnp.bfloat16),
            jax.ShapeDtypeStruct((b, PAIR_W, s), jnp.bfloat16),
            feat, feat,
        ],
        scratch_shapes=[pltpu.VMEM((d, TQ), jnp.bfloat16),
                        pltpu.VMEM((N_HEADS * A_KBLOCKS + 1, TQ, TQ), jnp.float32),
                        *[pltpu.VMEM((A_KBLOCKS * TQ, TQ), jnp.float32)] * N_SCORE_BUFS],
        compiler_params=pltpu.CompilerParams(
            dimension_semantics=("arbitrary", "arbitrary"), vmem_limit_bytes=VMEM_LIMIT),
        name="attn_a",
    )(x, qt4, k, k, k, vt4, vt4, vt4, gt4, bvec, wout, kvg, bg, kvwk, kvwvt, bwt)


def _attn_b_kernel(h_ref, qt_ref, kp_ref, ko_ref, vp_ref, vo_ref, gt_ref, vecp_ref, veco_ref,
                   sink_ref, wout_ref, fg_ref, out_ref, zt_ref, biasp_ref, biaso_ref, *s_refs):
    i = pl.program_id(1)

    @pl.when((pl.program_id(0) == 0) & (i == 0))
    def _():
        lag_prev = _chunk_iota((B_PREV, TQ), 1) - _chunk_iota((B_PREV, TQ), 0) + B_LEFT_CHUNKS
        _fill_bias_slabs(vecp_ref, biasp_ref, B_PREV, (lag_prev <= B_LEFT_CHUNKS,))
        biasp_ref[N_HEADS] = jnp.full((B_PREV, TQ), MASKED, jnp.float32)
        lag_own = _chunk_iota((TQ, TQ), 1) - _chunk_iota((TQ, TQ), 0)
        _fill_bias_slabs(veco_ref, biaso_ref, TQ, ((lag_own >= 0) & (lag_own <= B_LEFT_CHUNKS),))

    rows = lax.broadcasted_iota(jnp.int32, (PAIR_W, TQ), 0)
    pairs_per_kv = N_PAIRS // B_KV_HEADS

    def scores_fn(p, hh, s_ref):
        head = 2 * p + hh
        qh = qt_ref[0, p, hh * HEAD_DIM:(hh + 1) * HEAD_DIM, :]
        kv0 = jnp.where(p >= pairs_per_kv, HEAD_DIM, 0)
        kv_rows = (rows >= kv0) & (rows < kv0 + HEAD_DIM)
        qz = jnp.where(kv_rows, jnp.concatenate([qh, qh], axis=0), jnp.zeros((PAIR_W, TQ), qh.dtype))
        slab = jnp.where(i >= 1, head, N_HEADS)
        k_tiles = [kp_ref[0, TQ - B_PREV:, :], ko_ref[0]]
        return _scores_stage(k_tiles, qz, [biasp_ref[slab], biaso_ref[head]], s_ref)

    def pv_fn(p, hh, s_ref, m):
        head = 2 * p + hh
        kv0 = pl.multiple_of(jnp.where(p >= pairs_per_kv, HEAD_DIM, 0), HEAD_DIM)
        v_tiles = [vp_ref[0, pl.ds(kv0, HEAD_DIM), TQ - B_PREV:],
                   vo_ref[0, pl.ds(kv0, HEAD_DIM), :]]
        o = _pv_stage(s_ref, m, v_tiles, extra_logit=sink_ref[head])
        gate = gt_ref[0, p, hh * HEAD_DIM:(hh + 1) * HEAD_DIM, :].astype(jnp.float32)
        row0 = pl.multiple_of(p * PAIR_W + hh * HEAD_DIM, HEAD_DIM)
        zt_ref[pl.ds(row0, HEAD_DIM), :] = (o * gate).astype(jnp.bfloat16)

    _head_pipeline(N_HEADS, scores_fn, pv_fn, s_refs)

    h2 = _out_and_residual(zt_ref, wout_ref, h_ref[0])
    out_ref[0] = h2 * _rms_scale(h2) * fg_ref[...]


def _attn_b(h, qbt, ksh, vsht, gbt, vecp, veco, sinks, wout, fg):
    b, s, d = h.shape
    qt4 = qbt.reshape(b, N_PAIRS, PAIR_W, s)
    gt4 = gbt.reshape(b, N_PAIRS, PAIR_W, s)
    feat_spec = pl.BlockSpec((1, N_PAIRS, PAIR_W, TQ), lambda bi, i: (bi, 0, 0, i))
    return pl.pallas_call(
        _attn_b_kernel,
        grid=(b, s // TQ),
        in_specs=[
            pl.BlockSpec((1, TQ, d), lambda bi, i: (bi, i, 0)),
            feat_spec,
            pl.BlockSpec((1, TQ, PAIR_W), lambda bi, i: (bi, jnp.maximum(i - 1, 0), 0)),
            pl.BlockSpec((1, TQ, PAIR_W), lambda bi, i: (bi, i, 0)),
            pl.BlockSpec((1, PAIR_W, TQ), lambda bi, i: (bi, 0, jnp.maximum(i - 1, 0))),
            pl.BlockSpec((1, PAIR_W, TQ), lambda bi, i: (bi, 0, i)),
            feat_spec,
            _const_spec(vecp.shape),
            _const_spec(veco.shape),
            _const_spec(sinks.shape),
            _const_spec(wout.shape),
            _const_spec(fg.shape),
        ],
        out_specs=pl.BlockSpec((1, TQ, d), lambda bi, i: (bi, i, 0)),
        out_shape=jax.ShapeDtypeStruct((b, s, d), jnp.float32),
        scratch_shapes=[pltpu.VMEM((d, TQ), jnp.bfloat16),
                        pltpu.VMEM((N_HEADS + 1, B_PREV, TQ), jnp.float32),
                        pltpu.VMEM((N_HEADS, TQ, TQ), jnp.float32),
                        *[pltpu.VMEM((B_PREV + TQ, TQ), jnp.float32)] * N_SCORE_BUFS],
        compiler_params=pltpu.CompilerParams(
            dimension_semantics=("arbitrary", "arbitrary"), vmem_limit_bytes=VMEM_LIMIT),
        name="attn_b",
    )(h, qt4, ksh, ksh, vsht, vsht, gt4, vecp, veco, sinks, wout, fg)


def _slab_diff():
    u = jnp.arange(BIAS_PERIOD)
    return jnp.where(u < TQ, u, u - BIAS_PERIOD)


def _bias_vecs_a(rel_bias):
    back = (A_KBLOCKS - 1 - jnp.arange(A_KBLOCKS))[:, None] * TQ
    dist = back + _slab_diff()[None, :]
    idx = jnp.clip(dist, -A_REL_CLIP, A_REL_CLIP) + A_REL_CLIP
    vec = jnp.transpose(rel_bias[idx], (0, 2, 1)).astype(jnp.float32)
    return vec[:, :, None, :]


def _t5_bucket(rel):
    nb = T5_BUCKETS // 2
    max_exact = nb // 2
    ret = jnp.where(rel > 0, nb, 0)
    n = jnp.abs(rel)
    nf = jnp.maximum(n, 1).astype(jnp.float32)
    large = max_exact + (jnp.log(nf / max_exact) / math.log(T5_MAX_DIST / max_exact)
                         * (nb - max_exact)).astype(jnp.int32)
    large = jnp.minimum(large, nb - 1)
    return ret + jnp.where(n < max_exact, n, large)


def _bias_vecs_b(t5_table):
    rel_own = -_slab_diff()
    rel = jnp.stack([rel_own - B_PREV, rel_own])
    vec = jnp.transpose(t5_table[_t5_bucket(rel)], (0, 2, 1)).astype(jnp.float32)
    return vec[0:1, :, None, :], vec[1:2, :, None, :]


def kernel(x, a_norm, a_w_in, a_rel_bias, a_w_out, kv_norm, kv_w, t5_bias,
           b_norm, b_w_in, b_sinks, b_w_out, final_norm):
    assert a_norm.shape[0] == 1 and b_norm.shape[0] == 1, "one A layer then one B layer"
    bf = jnp.bfloat16
    scale = HEAD_DIM ** -0.5 * LOG2E

    def transposed_with_scaled_q(w):
        col_scale = jnp.where(jnp.arange(w.shape[1]) < D_MODEL, scale, 1.0).astype(w.dtype)
        return (w * col_scale[None, :]).T.astype(bf)

    qt, k, vt, gt = _proj_a(x, a_norm[0][None, :], transposed_with_scaled_q(a_w_in[0]))

    bwt = transposed_with_scaled_q(b_w_in[0])
    kvw_k, kvw_v = jnp.split(kv_w, 2, axis=-1)
    h, ksh, vsht, qbt, gbt = _attn_a(
        x, qt, k, vt, gt, _bias_vecs_a(a_rel_bias[0] * LOG2E), a_w_out[0].astype(bf),
        kv_norm[None, :], b_norm[0][None, :], kvw_k.astype(bf), kvw_v.T.astype(bf), bwt)

    vecp, veco = _bias_vecs_b(t5_bias * LOG2E)
    sinks = jnp.broadcast_to((b_sinks[0].astype(jnp.float32) * LOG2E)[:, None, None], (N_HEADS, 1, TQ))
    return _attn_b(h, qbt, ksh, vsht, gbt, vecp, veco, sinks, b_w_out[0].astype(bf),
                   final_norm[None, :])
```

```python
import functools
import math

import jax
import jax.numpy as jnp
from jax import lax
from jax.experimental import pallas as pl
from jax.experimental.pallas import tpu as pltpu

D_MODEL = 1024
HEAD_DIM = 64
N_HEADS = D_MODEL // HEAD_DIM
N_PAIRS = N_HEADS // 2
PAIR_W = 2 * HEAD_DIM
CHUNK = 64
RMS_EPS = 1e-6
A_LEFT_CHUNKS = 8
A_REL_CLIP = 256
B_KV_HEADS = 2
B_GROUP = N_HEADS // B_KV_HEADS
B_LEFT_CHUNKS = 2
T5_BUCKETS = 32
T5_MAX_DIST = 128

TQ = 256
A_KBLOCKS = A_LEFT_CHUNKS * CHUNK // TQ + 1
B_PREV = B_LEFT_CHUNKS * CHUNK
TM_PROJ = 512
FEAT_CHUNK = 512
MASKED = -1e30
BIAS_PERIOD = 2 * TQ
SUBLANES = 8
LANES = 128
MAX_CHAINS = 2
BF16_ROWS = 16
LOG2E = math.log2(math.e)
N_SCORE_BUFS = 4
VMEM_LIMIT = 56 * 1024 * 1024

_NT = (((1,), (1,)), ((), ()))
_TN = (((0,), (0,)), ((), ()))


def _rms_scale(xf):
    return lax.rsqrt(jnp.mean(xf * xf, axis=-1, keepdims=True) + RMS_EPS)


def _silu(v):
    return v * jax.nn.sigmoid(v)


def _proj_a_kernel(x_ref, g_ref, wt_ref, qt_ref, k_ref, vt_ref, gt_ref):
    xf = x_ref[0]
    xn = (xf * _rms_scale(xf) * g_ref[...]).astype(jnp.bfloat16)
    k = lax.dot_general(xn, wt_ref[D_MODEL:2 * D_MODEL, :], _NT,
                        preferred_element_type=jnp.float32)
    for p in range(N_PAIRS):
        k_ref[0, p] = k[:, p * PAIR_W:(p + 1) * PAIR_W].astype(jnp.bfloat16)
    for out_ref, base, act in ((qt_ref, 0, None), (vt_ref, 2 * D_MODEL, None), (gt_ref, 3 * D_MODEL, _silu)):
        for c in range(0, D_MODEL, FEAT_CHUNK):
            w = wt_ref[base + c:base + c + FEAT_CHUNK, :]
            yt = lax.dot_general(w, xn, _NT, preferred_element_type=jnp.float32)
            out_ref[0, c:c + FEAT_CHUNK, :] = (yt if act is None else act(yt)).astype(jnp.bfloat16)


def _proj_a(x, gain, wt):
    b, s, d = x.shape
    feat = jax.ShapeDtypeStruct((b, d, s), jnp.bfloat16)
    return pl.pallas_call(
        _proj_a_kernel,
        grid=(b, s // TM_PROJ),
        in_specs=[
            pl.BlockSpec((1, TM_PROJ, d), lambda bi, i: (bi, i, 0)),
            _const_spec(gain.shape),
            _const_spec(wt.shape),
        ],
        out_specs=[
            pl.BlockSpec((1, d, TM_PROJ), lambda bi, i: (bi, 0, i)),
            pl.BlockSpec((1, N_PAIRS, TM_PROJ, PAIR_W), lambda bi, i: (bi, 0, i, 0)),
            pl.BlockSpec((1, d, TM_PROJ), lambda bi, i: (bi, 0, i)),
            pl.BlockSpec((1, d, TM_PROJ), lambda bi, i: (bi, 0, i)),
        ],
        out_shape=[feat, jax.ShapeDtypeStruct((b, N_PAIRS, s, PAIR_W), jnp.bfloat16), feat, feat],
        compiler_params=pltpu.CompilerParams(
            dimension_semantics=("arbitrary", "arbitrary"), vmem_limit_bytes=VMEM_LIMIT),
        name="proj_a",
    )(x, gain, wt)


def _chunk_iota(shape, axis):
    return lax.shift_right_logical(lax.broadcasted_iota(jnp.int32, shape, axis), CHUNK.bit_length() - 1)


def _fill_bias_slabs(vec_ref, bias_ref, n_keys, bands):
    n = len(bands)
    kch = _chunk_iota((n_keys, TQ), 0)
    qch = _chunk_iota((n_keys, TQ), 1)

    def head_body(h, carry):
        for j, band in enumerate(bands):
            v = jnp.broadcast_to(vec_ref[j, h], (n_keys, BIAS_PERIOD))
            t = pltpu.roll(v, 0, 1, stride=1, stride_axis=0)[:, :TQ]
            if band is not None:
                t = jnp.where(band(kch, qch), t, MASKED)
            bias_ref[h * n + j] = t
        return carry

    lax.fori_loop(0, N_HEADS, head_body, 0)


def _live_rows(band, n_keys):
    chunks_per_half = LANES // CHUNK
    out = []
    for half in range(TQ // LANES):
        qchs = range(half * chunks_per_half, (half + 1) * chunks_per_half)
        live = [kc for kc in range(n_keys // CHUNK) if band is None or any(band(kc, qc) for qc in qchs)]
        out.append((live[0] * CHUNK, (live[-1] + 1) * CHUNK) if live else (0, 0))
    return tuple(out)


def _scores_stage(k_tiles, qz, bias_fns, live, s_ref):
    maxes, row = [], 0
    for half in range(TQ // LANES):
        maxes.append([None] * MAX_CHAINS)
    for kt, bias_fn, live_j in zip(k_tiles, bias_fns, live):
        s = jnp.dot(kt, qz, preferred_element_type=jnp.float32)
        for half, (r0, r1) in enumerate(live_j):
            lanes = slice(half * LANES, (half + 1) * LANES)
            acc = maxes[half]
            for r in range(r0 // SUBLANES, r1 // SUBLANES):
                rows = slice(r * SUBLANES, (r + 1) * SUBLANES)
                grp = s[rows, lanes] + bias_fn(rows, lanes)
                s_ref[row + r * SUBLANES:row + (r + 1) * SUBLANES, lanes] = grp
                c = r % MAX_CHAINS
                acc[c] = grp if acc[c] is None else jnp.maximum(acc[c], grp)
        row += s.shape[0]
    cols = [jnp.max(functools.reduce(jnp.maximum, [a for a in acc if a is not None]), axis=0, keepdims=True)
            for acc in maxes]
    return jnp.concatenate(cols, axis=1)


def _pv_stage(s_ref, m, v_tiles, live, extra_logit=None):
    if extra_logit is not None:
        m = jnp.maximum(m, extra_logit)
    acc, row = None, 0
    for vt, live_j in zip(v_tiles, live):
        n = vt.shape[1]
        halves = []
        for half, (r0, r1) in enumerate(live_j):
            lanes = slice(half * LANES, (half + 1) * LANES)
            parts = [jnp.zeros((r0, LANES), jnp.bfloat16)] if r0 else []
            if r1 > r0:
                parts.append(jnp.exp2(s_ref[row + r0:row + r1, lanes] - m[:, lanes]).astype(jnp.bfloat16))
            if n > r1:
                parts.append(jnp.zeros((n - r1, LANES), jnp.bfloat16))
            halves.append(parts[0] if len(parts) == 1 else jnp.concatenate(parts, axis=0))
        p = jnp.concatenate(halves, axis=1)
        v_ones = jnp.concatenate([vt, jnp.ones((BF16_ROWS, n), vt.dtype)], axis=0)
        part = jnp.dot(v_ones, p, preferred_element_type=jnp.float32)
        acc = part if acc is None else acc + part
        row += n
    l = acc[HEAD_DIM:HEAD_DIM + 1, :]
    if extra_logit is not None:
        l = l + jnp.exp2(extra_logit - m)
    return acc[:HEAD_DIM, :] / l


def _head_pipeline(n_heads, scores_fn, pv_fn, s_refs):
    x_refs, y_refs = s_refs[:2], s_refs[2:]
    n_pairs = n_heads // 2

    def group(pair, src, m_src, dst, next_pair):
        m_dst = None
        if next_pair is not None:
            m_dst = scores_fn(next_pair, 0, dst[0]), scores_fn(next_pair, 1, dst[1])
        pv_fn(pair, 0, src[0], m_src[0])
        pv_fn(pair, 1, src[1], m_src[1])
        return m_dst

    m = scores_fn(0, 0, x_refs[0]), scores_fn(0, 1, x_refs[1])
    for pair in range(n_pairs):
        src, dst = (x_refs, y_refs) if pair % 2 == 0 else (y_refs, x_refs)
        m = group(pair, src, m, dst, pair + 1 if pair + 1 < n_pairs else None)


A_BANDS = (lambda kc, qc: kc >= qc,
           None,
           lambda kc, qc: kc <= qc)
A_LIVE = tuple(_live_rows(band, TQ) for band in A_BANDS)
B_BANDS = (lambda kc, qc: qc <= kc,
           lambda kc, qc: (qc >= kc) & (qc - kc <= B_LEFT_CHUNKS))
B_LIVE = (_live_rows(B_BANDS[0], B_PREV), _live_rows(B_BANDS[1], TQ))


def _out_and_residual(zt_ref, wout_ref, res):
    y = lax.dot_general(zt_ref[...], wout_ref[...], _TN, preferred_element_type=jnp.float32)
    return res + y


def _attn_a_kernel(x_ref, qt_ref, k0_ref, k1_ref, k2_ref, v0_ref, v1_ref, v2_ref, gt_ref,
                   bvec_ref, wout_ref, kvg_ref, bg_ref, kvwk_ref, kvwvt_ref, bwt_ref,
                   h_ref, ksh_ref, vsht_ref, qbt_ref, gbt_ref, zt_ref, bias_ref, *s_refs):
    i = pl.program_id(1)

    @pl.when((pl.program_id(0) == 0) & (i == 0))
    def _():
        _fill_bias_slabs(bvec_ref, bias_ref, TQ, A_BANDS)
        bias_ref[N_HEADS * A_KBLOCKS] = jnp.full((TQ, TQ), MASKED, jnp.float32)

    k_refs = (k0_ref, k1_ref, k2_ref)
    v_refs = (v0_ref, v1_ref, v2_ref)
    n_slabs = N_HEADS * A_KBLOCKS
    rows = lax.broadcasted_iota(jnp.int32, (PAIR_W, TQ), 0)

    def scores_fn(p, hh, s_ref):
        q2 = qt_ref[0, p]
        own_rows = rows < HEAD_DIM if hh == 0 else rows >= HEAD_DIM
        qz = jnp.where(own_rows, q2, jnp.zeros_like(q2))
        k_tiles, bias_fns = [], []
        for j in range(A_KBLOCKS):
            in_seq = i - (A_KBLOCKS - 1) + j >= 0
            slab = jnp.where(in_seq, (2 * p + hh) * A_KBLOCKS + j, n_slabs)
            k_tiles.append(k_refs[j][0, p])
            bias_fns.append(lambda rws, lanes, slab=slab: bias_ref[slab, rws, lanes])
        return _scores_stage(k_tiles, qz, bias_fns, A_LIVE, s_ref)

    def pv_fn(p, hh, s_ref, m):
        half = slice(hh * HEAD_DIM, (hh + 1) * HEAD_DIM)
        o = _pv_stage(s_ref, m, [v_ref[0, p, half, :] for v_ref in v_refs], A_LIVE)
        gate = gt_ref[0, p, half, :].astype(jnp.float32)
        row0 = p * PAIR_W + hh * HEAD_DIM
        zt_ref[row0:row0 + HEAD_DIM, :] = (o * gate).astype(jnp.bfloat16)

    _head_pipeline(N_HEADS, scores_fn, pv_fn, s_refs)

    h = _out_and_residual(zt_ref, wout_ref, x_ref[0])
    h_ref[0] = h
    hn = h * _rms_scale(h)
    xkv = (hn * kvg_ref[...]).astype(jnp.bfloat16)
    xb = (hn * bg_ref[...]).astype(jnp.bfloat16)
    ksh_ref[0] = jnp.dot(xkv, kvwk_ref[...], preferred_element_type=jnp.float32).astype(jnp.bfloat16)
    vsht_ref[0] = lax.dot_general(kvwvt_ref[...], xkv, _NT,
                                  preferred_element_type=jnp.float32).astype(jnp.bfloat16)
    n_chunks = D_MODEL // FEAT_CHUNK
    for r in range(2 * n_chunks):
        w = bwt_ref[r * FEAT_CHUNK:(r + 1) * FEAT_CHUNK, :]
        yt = lax.dot_general(w, xb, _NT, preferred_element_type=jnp.float32)
        which, c = divmod(r, n_chunks)
        rws = slice(c * FEAT_CHUNK, (c + 1) * FEAT_CHUNK)
        if which == 0:
            qbt_ref[0, rws, :] = yt.astype(jnp.bfloat16)
        else:
            gbt_ref[0, rws, :] = _silu(yt).astype(jnp.bfloat16)


def _const_spec(shape):
    return pl.BlockSpec(shape, lambda bi, i: (0,) * len(shape), pipeline_mode=pl.Buffered(1))


def _attn_a(x, qt, k, vt, gt, bvec, wout, kvg, bg, kvwk, kvwvt, bwt):
    b, s, d = x.shape
    qt4 = qt.reshape(b, N_PAIRS, PAIR_W, s)
    vt4 = vt.reshape(b, N_PAIRS, PAIR_W, s)
    gt4 = gt.reshape(b, N_PAIRS, PAIR_W, s)
    back = A_KBLOCKS - 1

    def kspec(j):
        return pl.BlockSpec((1, N_PAIRS, TQ, PAIR_W),
                            lambda bi, i: (bi, 0, jnp.maximum(i - back + j, 0), 0))

    def vspec(j):
        return pl.BlockSpec((1, N_PAIRS, PAIR_W, TQ),
                            lambda bi, i: (bi, 0, 0, jnp.maximum(i - back + j, 0)))

    feat_spec = pl.BlockSpec((1, N_PAIRS, PAIR_W, TQ), lambda bi, i: (bi, 0, 0, i))
    featout_spec = pl.BlockSpec((1, d, TQ), lambda bi, i: (bi, 0, i))
    feat = jax.ShapeDtypeStruct((b, d, s), jnp.bfloat16)
    return pl.pallas_call(
        _attn_a_kernel,
        grid=(b, s // TQ),
        in_specs=[
            pl.BlockSpec((1, TQ, d), lambda bi, i: (bi, i, 0)),
            feat_spec,
            kspec(0), kspec(1), kspec(2),
            vspec(0), vspec(1), vspec(2),
            feat_spec,
            _const_spec(bvec.shape),
            _const_spec(wout.shape),
            _const_spec(kvg.shape),
            _const_spec(bg.shape),
            _const_spec(kvwk.shape),
            _const_spec(kvwvt.shape),
            _const_spec(bwt.shape),
        ],
        out_specs=[
            pl.BlockSpec((1, TQ, d), lambda bi, i: (bi, i, 0)),
            pl.BlockSpec((1, TQ, PAIR_W), lambda bi, i: (bi, i, 0)),
            pl.BlockSpec((1, PAIR_W, TQ), lambda bi, i: (bi, 0, i)),
            featout_spec,
            featout_spec,
        ],
        out_shape=[
            jax.ShapeDtypeStruct((b, s, d), jnp.float32),
            jax.ShapeDtypeStruct((b, s, PAIR_W), jnp.bfloat16),
            jax.ShapeDtypeStruct((b, PAIR_W, s), jnp.bfloat16),
            feat, feat,
        ],
        scratch_shapes=[pltpu.VMEM((d, TQ), jnp.bfloat16),
                        pltpu.VMEM((N_HEADS * A_KBLOCKS + 1, TQ, TQ), jnp.float32),
                        *[pltpu.VMEM((A_KBLOCKS * TQ, TQ), jnp.float32)] * N_SCORE_BUFS],
        compiler_params=pltpu.CompilerParams(
            dimension_semantics=("arbitrary", "arbitrary"), vmem_limit_bytes=VMEM_LIMIT),
        name="attn_a",
    )(x, qt4, k, k, k, vt4, vt4, vt4, gt4, bvec, wout, kvg, bg, kvwk, kvwvt, bwt)


def _attn_b_kernel(h_ref, qt_ref, kp_ref, ko_ref, vp_ref, vo_ref, gt_ref, vecp_ref, veco_ref,
                   sink_ref, wout_ref, fg_ref, out_ref, zt_ref, biasp_ref, biaso_ref, *s_refs):
    i = pl.program_id(1)

    @pl.when((pl.program_id(0) == 0) & (i == 0))
    def _():
        _fill_bias_slabs(vecp_ref, biasp_ref, B_PREV, B_BANDS[:1])
        biasp_ref[N_HEADS] = jnp.full((B_PREV, TQ), MASKED, jnp.float32)
        _fill_bias_slabs(veco_ref, biaso_ref, TQ, B_BANDS[1:])

    rows = lax.broadcasted_iota(jnp.int32, (PAIR_W, TQ), 0)
    pairs_per_kv = N_PAIRS // B_KV_HEADS

    def scores_fn(p, hh, s_ref):
        head = 2 * p + hh
        qh = qt_ref[0, p, hh * HEAD_DIM:(hh + 1) * HEAD_DIM, :]
        kv0 = HEAD_DIM * (p // pairs_per_kv)
        kv_rows = (rows >= kv0) & (rows < kv0 + HEAD_DIM)
        qz = jnp.where(kv_rows, jnp.concatenate([qh, qh], axis=0), jnp.zeros((PAIR_W, TQ), qh.dtype))
        slab = jnp.where(i >= 1, head, N_HEADS)
        k_tiles = [kp_ref[0, TQ - B_PREV:, :], ko_ref[0]]
        bias_fns = [lambda rws, lanes: biasp_ref[slab, rws, lanes],
                    lambda rws, lanes: biaso_ref[head, rws, lanes]]
        return _scores_stage(k_tiles, qz, bias_fns, B_LIVE, s_ref)

    def pv_fn(p, hh, s_ref, m):
        head = 2 * p + hh
        kv_rows = slice(HEAD_DIM * (p // pairs_per_kv), HEAD_DIM * (p // pairs_per_kv + 1))
        v_tiles = [vp_ref[0, kv_rows, TQ - B_PREV:],
                   vo_ref[0, kv_rows, :]]
        o = _pv_stage(s_ref, m, v_tiles, B_LIVE, extra_logit=sink_ref[head])
        gate = gt_ref[0, p, hh * HEAD_DIM:(hh + 1) * HEAD_DIM, :].astype(jnp.float32)
        row0 = p * PAIR_W + hh * HEAD_DIM
        zt_ref[row0:row0 + HEAD_DIM, :] = (o * gate).astype(jnp.bfloat16)

    _head_pipeline(N_HEADS, scores_fn, pv_fn, s_refs)

    h2 = _out_and_residual(zt_ref, wout_ref, h_ref[0])
    out_ref[0] = h2 * _rms_scale(h2) * fg_ref[...]


def _attn_b(h, qbt, ksh, vsht, gbt, vecp, veco, sinks, wout, fg):
    b, s, d = h.shape
    qt4 = qbt.reshape(b, N_PAIRS, PAIR_W, s)
    gt4 = gbt.reshape(b, N_PAIRS, PAIR_W, s)
    feat_spec = pl.BlockSpec((1, N_PAIRS, PAIR_W, TQ), lambda bi, i: (bi, 0, 0, i))
    return pl.pallas_call(
        _attn_b_kernel,
        grid=(b, s // TQ),
        in_specs=[
            pl.BlockSpec((1, TQ, d), lambda bi, i: (bi, i, 0)),
            feat_spec,
            pl.BlockSpec((1, TQ, PAIR_W), lambda bi, i: (bi, jnp.maximum(i - 1, 0), 0)),
            pl.BlockSpec((1, TQ, PAIR_W), lambda bi, i: (bi, i, 0)),
            pl.BlockSpec((1, PAIR_W, TQ), lambda bi, i: (bi, 0, jnp.maximum(i - 1, 0))),
            pl.BlockSpec((1, PAIR_W, TQ), lambda bi, i: (bi, 0, i)),
            feat_spec,
            _const_spec(vecp.shape),
            _const_spec(veco.shape),
            _const_spec(sinks.shape),
            _const_spec(wout.shape),
            _const_spec(fg.shape),
        ],
        out_specs=pl.BlockSpec((1, TQ, d), lambda bi, i: (bi, i, 0)),
        out_shape=jax.ShapeDtypeStruct((b, s, d), jnp.float32),
        scratch_shapes=[pltpu.VMEM((d, TQ), jnp.bfloat16),
                        pltpu.VMEM((N_HEADS + 1, B_PREV, TQ), jnp.float32),
                        pltpu.VMEM((N_HEADS, TQ, TQ), jnp.float32),
                        *[pltpu.VMEM((B_PREV + TQ, TQ), jnp.float32)] * N_SCORE_BUFS],
        compiler_params=pltpu.CompilerParams(
            dimension_semantics=("arbitrary", "arbitrary"), vmem_limit_bytes=VMEM_LIMIT),
        name="attn_b",
    )(h, qt4, ksh, ksh, vsht, vsht, gt4, vecp, veco, sinks, wout, fg)


def _slab_diff():
    u = jnp.arange(BIAS_PERIOD)
    return jnp.where(u < TQ, u, u - BIAS_PERIOD)


def _bias_vecs_a(rel_bias):
    back = (A_KBLOCKS - 1 - jnp.arange(A_KBLOCKS))[:, None] * TQ
    dist = back + _slab_diff()[None, :]
    idx = jnp.clip(dist, -A_REL_CLIP, A_REL_CLIP) + A_REL_CLIP
    vec = jnp.transpose(rel_bias[idx], (0, 2, 1)).astype(jnp.float32)
    return vec[:, :, None, :]


def _t5_bucket(rel):
    nb = T5_BUCKETS // 2
    max_exact = nb // 2
    ret = jnp.where(rel > 0, nb, 0)
    n = jnp.abs(rel)
    nf = jnp.maximum(n, 1).astype(jnp.float32)
    large = max_exact + (jnp.log(nf / max_exact) / math.log(T5_MAX_DIST / max_exact)
                         * (nb - max_exact)).astype(jnp.int32)
    large = jnp.minimum(large, nb - 1)
    return ret + jnp.where(n < max_exact, n, large)


def _bias_vecs_b(t5_table):
    rel_own = -_slab_diff()
    rel = jnp.stack([rel_own - B_PREV, rel_own])
    vec = jnp.transpose(t5_table[_t5_bucket(rel)], (0, 2, 1)).astype(jnp.float32)
    return vec[0:1, :, None, :], vec[1:2, :, None, :]


def kernel(x, a_norm, a_w_in, a_rel_bias, a_w_out, kv_norm, kv_w, t5_bias,
           b_norm, b_w_in, b_sinks, b_w_out, final_norm):
    assert a_norm.shape[0] == 1 and b_norm.shape[0] == 1, "one A layer then one B layer"
    bf = jnp.bfloat16
    scale = HEAD_DIM ** -0.5 * LOG2E

    def transposed_with_scaled_q(w):
        col_scale = jnp.where(jnp.arange(w.shape[1]) < D_MODEL, scale, 1.0).astype(w.dtype)
        return (w * col_scale[None, :]).T.astype(bf)

    qt, k, vt, gt = _proj_a(x, a_norm[0][None, :], transposed_with_scaled_q(a_w_in[0]))

    bwt = transposed_with_scaled_q(b_w_in[0])
    kvw_k, kvw_v = jnp.split(kv_w, 2, axis=-1)
    h, ksh, vsht, qbt, gbt = _attn_a(
        x, qt, k, vt, gt, _bias_vecs_a(a_rel_bias[0] * LOG2E), a_w_out[0].astype(bf),
        kv_norm[None, :], b_norm[0][None, :], kvw_k.astype(bf), kvw_v.T.astype(bf), bwt)

    vecp, veco = _bias_vecs_b(t5_bias * LOG2E)
    sinks = jnp.broadcast_to((b_sinks[0].astype(jnp.float32) * LOG2E)[:, None, None], (N_HEADS, 1, TQ))
    return _attn_b(h, qbt, ksh, vsht, gbt, vecp, veco, sinks, b_w_out[0].astype(bf),
                   final_norm[None, :])
```

```python
import functools
import math

import jax
import jax.numpy as jnp
from jax import lax
from jax.experimental import pallas as pl
from jax.experimental.pallas import tpu as pltpu

D_MODEL = 1024
HEAD_DIM = 64
N_HEADS = D_MODEL // HEAD_DIM
N_PAIRS = N_HEADS // 2
PAIR_W = 2 * HEAD_DIM
CHUNK = 64
RMS_EPS = 1e-6
A_LEFT_CHUNKS = 8
A_REL_CLIP = 256
B_KV_HEADS = 2
B_GROUP = N_HEADS // B_KV_HEADS
B_LEFT_CHUNKS = 2
T5_BUCKETS = 32
T5_MAX_DIST = 128

TQ = 256
A_KBLOCKS = A_LEFT_CHUNKS * CHUNK // TQ + 1
B_PREV = B_LEFT_CHUNKS * CHUNK
TM_PROJ = 512
FEAT_CHUNK = 512
MASKED = -1e30
BIAS_PERIOD = 2 * TQ
SUBLANES = 8
LANES = 128
MAX_CHAINS = 2
BF16_ROWS = 16
LOG2E = math.log2(math.e)
N_SCORE_BUFS = 4
VMEM_LIMIT = 56 * 1024 * 1024

_NT = (((1,), (1,)), ((), ()))
_TN = (((0,), (0,)), ((), ()))


def _rms_scale(xf):
    return lax.rsqrt(jnp.mean(xf * xf, axis=-1, keepdims=True) + RMS_EPS)


def _silu(v):
    return v * jax.nn.sigmoid(v)


def _proj_a_kernel(x_ref, g_ref, wt_ref, qt_ref, k_ref, vt_ref, gt_ref):
    xf = x_ref[0]
    xn = (xf * _rms_scale(xf) * g_ref[...]).astype(jnp.bfloat16)
    k = lax.dot_general(xn, wt_ref[D_MODEL:2 * D_MODEL, :], _NT,
                        preferred_element_type=jnp.float32)
    for p in range(N_PAIRS):
        k_ref[0, p] = k[:, p * PAIR_W:(p + 1) * PAIR_W].astype(jnp.bfloat16)
    for out_ref, base, act in ((qt_ref, 0, None), (vt_ref, 2 * D_MODEL, None), (gt_ref, 3 * D_MODEL, _silu)):
        for c in range(0, D_MODEL, FEAT_CHUNK):
            w = wt_ref[base + c:base + c + FEAT_CHUNK, :]
            yt = lax.dot_general(w, xn, _NT, preferred_element_type=jnp.float32)
            out_ref[0, c:c + FEAT_CHUNK, :] = (yt if act is None else act(yt)).astype(jnp.bfloat16)


def _proj_a(x, gain, wt):
    b, s, d = x.shape
    feat = jax.ShapeDtypeStruct((b, d, s), jnp.bfloat16)
    return pl.pallas_call(
        _proj_a_kernel,
        grid=(b, s // TM_PROJ),
        in_specs=[
            pl.BlockSpec((1, TM_PROJ, d), lambda bi, i: (bi, i, 0)),
            _const_spec(gain.shape),
            _const_spec(wt.shape),
        ],
        out_specs=[
            pl.BlockSpec((1, d, TM_PROJ), lambda bi, i: (bi, 0, i)),
            pl.BlockSpec((1, N_PAIRS, TM_PROJ, PAIR_W), lambda bi, i: (bi, 0, i, 0)),
            pl.BlockSpec((1, d, TM_PROJ), lambda bi, i: (bi, 0, i)),
            pl.BlockSpec((1, d, TM_PROJ), lambda bi, i: (bi, 0, i)),
        ],
        out_shape=[feat, jax.ShapeDtypeStruct((b, N_PAIRS, s, PAIR_W), jnp.bfloat16), feat, feat],
        compiler_params=pltpu.CompilerParams(
            dimension_semantics=("arbitrary", "arbitrary"), vmem_limit_bytes=VMEM_LIMIT),
        name="proj_a",
    )(x, gain, wt)


def _chunk_iota(shape, axis):
    return lax.shift_right_logical(lax.broadcasted_iota(jnp.int32, shape, axis), CHUNK.bit_length() - 1)


def _fill_bias_slabs(vec_ref, bias_ref, n_keys, bands):
    n = len(bands)
    kch = _chunk_iota((n_keys, TQ), 0)
    qch = _chunk_iota((n_keys, TQ), 1)

    def head_body(h, carry):
        for j, band in enumerate(bands):
            v = jnp.broadcast_to(vec_ref[j, h], (n_keys, BIAS_PERIOD))
            t = pltpu.roll(v, 0, 1, stride=1, stride_axis=0)[:, :TQ]
            if band is not None:
                t = jnp.where(band(kch, qch), t, MASKED)
            bias_ref[h * n + j] = t
        return carry

    lax.fori_loop(0, N_HEADS, head_body, 0)


def _live_rows(band, n_keys):
    chunks_per_half = LANES // CHUNK
    out = []
    for half in range(TQ // LANES):
        qchs = range(half * chunks_per_half, (half + 1) * chunks_per_half)
        live = [kc for kc in range(n_keys // CHUNK) if band is None or any(band(kc, qc) for qc in qchs)]
        out.append((live[0] * CHUNK, (live[-1] + 1) * CHUNK) if live else (0, 0))
    return tuple(out)


def _scores_stage(k_tiles, qz, bias_fns, live, s_ref):
    maxes, row = [], 0
    for half in range(TQ // LANES):
        maxes.append([None] * MAX_CHAINS)
    for kt, bias_fn, live_j in zip(k_tiles, bias_fns, live):
        s = jnp.dot(kt, qz, preferred_element_type=jnp.float32)
        for half, (r0, r1) in enumerate(live_j):
            lanes = slice(half * LANES, (half + 1) * LANES)
            acc = maxes[half]
            for r in range(r0 // SUBLANES, r1 // SUBLANES):
                rows = slice(r * SUBLANES, (r + 1) * SUBLANES)
                grp = s[rows, lanes] + bias_fn(rows, lanes)
                s_ref[row + r * SUBLANES:row + (r + 1) * SUBLANES, lanes] = grp
                c = r % MAX_CHAINS
                acc[c] = grp if acc[c] is None else jnp.maximum(acc[c], grp)
        row += s.shape[0]
    cols = [jnp.max(functools.reduce(jnp.maximum, [a for a in acc if a is not None]), axis=0, keepdims=True)
            for acc in maxes]
    return jnp.concatenate(cols, axis=1)


def _pv_stage(s_ref, m, v_tiles, live, extra_logit=None):
    if extra_logit is not None:
        m = jnp.maximum(m, extra_logit)
    acc, row = None, 0
    for vt, live_j in zip(v_tiles, live):
        n = vt.shape[1]
        halves = []
        for half, (r0, r1) in enumerate(live_j):
            lanes = slice(half * LANES, (half + 1) * LANES)
            parts = [jnp.zeros((r0, LANES), jnp.bfloat16)] if r0 else []
            if r1 > r0:
                parts.append(jnp.exp2(s_ref[row + r0:row + r1, lanes] - m[:, lanes]).astype(jnp.bfloat16))
            if n > r1:
                parts.append(jnp.zeros((n - r1, LANES), jnp.bfloat16))
            halves.append(parts[0] if len(parts) == 1 else jnp.concatenate(parts, axis=0))
        p = jnp.concatenate(halves, axis=1)
        v_ones = jnp.concatenate([vt, jnp.ones((BF16_ROWS, n), vt.dtype)], axis=0)
        part = jnp.dot(v_ones, p, preferred_element_type=jnp.float32)
        acc = part if acc is None else acc + part
        row += n
    l = acc[HEAD_DIM:HEAD_DIM + 1, :]
    if extra_logit is not None:
        l = l + jnp.exp2(extra_logit - m)
    return acc[:HEAD_DIM, :] / l


def _head_pipeline(n_heads, scores_fn, pv_fn, s_refs):
    x_refs, y_refs = s_refs[:2], s_refs[2:]
    n_pairs = n_heads // 2

    def group(pair, src, m_src, dst, next_pair):
        m_dst = None
        if next_pair is not None:
            m_dst = scores_fn(next_pair, 0, dst[0]), scores_fn(next_pair, 1, dst[1])
        pv_fn(pair, 0, src[0], m_src[0])
        pv_fn(pair, 1, src[1], m_src[1])
        return m_dst

    m = scores_fn(0, 0, x_refs[0]), scores_fn(0, 1, x_refs[1])
    for pair in range(n_pairs):
        src, dst = (x_refs, y_refs) if pair % 2 == 0 else (y_refs, x_refs)
        m = group(pair, src, m, dst, pair + 1 if pair + 1 < n_pairs else None)


A_BANDS = (lambda kc, qc: kc >= qc,
           None,
           lambda kc, qc: kc <= qc)
A_LIVE = tuple(_live_rows(band, TQ) for band in A_BANDS)
B_BANDS = (lambda kc, qc: qc <= kc,
           lambda kc, qc: (qc >= kc) & (qc - kc <= B_LEFT_CHUNKS))
B_LIVE = (_live_rows(B_BANDS[0], B_PREV), _live_rows(B_BANDS[1], TQ))


def _out_and_residual(zt_ref, wout_ref, res):
    y = lax.dot_general(zt_ref[...], wout_ref[...], _TN, preferred_element_type=jnp.float32)
    return res + y


def _attn_a_kernel(x_ref, qt_ref, k0_ref, k1_ref, k2_ref, v0_ref, v1_ref, v2_ref, gt_ref,
                   bvec_ref, wout_ref, kvwk_ref, nwt_ref,
                   h_ref, ksh_ref, vsht_ref, qbt_ref, gbt_ref, zt_ref, bias_ref, *s_refs):
    i = pl.program_id(1)

    @pl.when((pl.program_id(0) == 0) & (i == 0))
    def _():
        _fill_bias_slabs(bvec_ref, bias_ref, TQ, A_BANDS)
        bias_ref[N_HEADS * A_KBLOCKS] = jnp.full((TQ, TQ), MASKED, jnp.float32)

    k_refs = (k0_ref, k1_ref, k2_ref)
    v_refs = (v0_ref, v1_ref, v2_ref)
    n_slabs = N_HEADS * A_KBLOCKS
    rows = lax.broadcasted_iota(jnp.int32, (PAIR_W, TQ), 0)

    def scores_fn(p, hh, s_ref):
        q2 = qt_ref[0, p]
        own_rows = rows < HEAD_DIM if hh == 0 else rows >= HEAD_DIM
        qz = jnp.where(own_rows, q2, jnp.zeros_like(q2))
        k_tiles, bias_fns = [], []
        for j in range(A_KBLOCKS):
            in_seq = i - (A_KBLOCKS - 1) + j >= 0
            slab = jnp.where(in_seq, (2 * p + hh) * A_KBLOCKS + j, n_slabs)
            k_tiles.append(k_refs[j][0, p])
            bias_fns.append(lambda rws, lanes, slab=slab: bias_ref[slab, rws, lanes])
        return _scores_stage(k_tiles, qz, bias_fns, A_LIVE, s_ref)

    def pv_fn(p, hh, s_ref, m):
        half = slice(hh * HEAD_DIM, (hh + 1) * HEAD_DIM)
        o = _pv_stage(s_ref, m, [v_ref[0, p, half, :] for v_ref in v_refs], A_LIVE)
        gate = gt_ref[0, p, half, :].astype(jnp.float32)
        row0 = p * PAIR_W + hh * HEAD_DIM
        zt_ref[row0:row0 + HEAD_DIM, :] = (o * gate).astype(jnp.bfloat16)

    _head_pipeline(N_HEADS, scores_fn, pv_fn, s_refs)

    h = _out_and_residual(zt_ref, wout_ref, x_ref[0])
    h_ref[0] = h
    hn = (h * _rms_scale(h)).astype(jnp.bfloat16)
    ksh_ref[0] = jnp.dot(hn, kvwk_ref[...], preferred_element_type=jnp.float32).astype(jnp.bfloat16)
    for out_ref, base, act in ((qbt_ref, 0, None), (gbt_ref, D_MODEL, _silu)):
        for c in range(0, D_MODEL, FEAT_CHUNK):
            last = out_ref is gbt_ref and c + FEAT_CHUNK == D_MODEL
            stop = base + c + FEAT_CHUNK + (PAIR_W if last else 0)
            yt = lax.dot_general(nwt_ref[base + c:stop, :], hn, _NT, preferred_element_type=jnp.float32)
            main = yt[:FEAT_CHUNK]
            out_ref[0, c:c + FEAT_CHUNK, :] = (main if act is None else act(main)).astype(jnp.bfloat16)
            if last:
                vsht_ref[0] = yt[FEAT_CHUNK:].astype(jnp.bfloat16)


def _const_spec(shape):
    return pl.BlockSpec(shape, lambda bi, i: (0,) * len(shape), pipeline_mode=pl.Buffered(1))


def _attn_a(x, qt, k, vt, gt, bvec, wout, kvwk, nwt):
    b, s, d = x.shape
    qt4 = qt.reshape(b, N_PAIRS, PAIR_W, s)
    vt4 = vt.reshape(b, N_PAIRS, PAIR_W, s)
    gt4 = gt.reshape(b, N_PAIRS, PAIR_W, s)
    back = A_KBLOCKS - 1

    def kspec(j):
        return pl.BlockSpec((1, N_PAIRS, TQ, PAIR_W),
                            lambda bi, i: (bi, 0, jnp.maximum(i - back + j, 0), 0))

    def vspec(j):
        return pl.BlockSpec((1, N_PAIRS, PAIR_W, TQ),
                            lambda bi, i: (bi, 0, 0, jnp.maximum(i - back + j, 0)))

    feat_spec = pl.BlockSpec((1, N_PAIRS, PAIR_W, TQ), lambda bi, i: (bi, 0, 0, i))
    featout_spec = pl.BlockSpec((1, d, TQ), lambda bi, i: (bi, 0, i))
    feat = jax.ShapeDtypeStruct((b, d, s), jnp.bfloat16)
    return pl.pallas_call(
        _attn_a_kernel,
        grid=(b, s // TQ),
        in_specs=[
            pl.BlockSpec((1, TQ, d), lambda bi, i: (bi, i, 0)),
            feat_spec,
            kspec(0), kspec(1), kspec(2),
            vspec(0), vspec(1), vspec(2),
            feat_spec,
            _const_spec(bvec.shape),
            _const_spec(wout.shape),
            _const_spec(kvwk.shape),
            _const_spec(nwt.shape),
        ],
        out_specs=[
            pl.BlockSpec((1, TQ, d), lambda bi, i: (bi, i, 0)),
            pl.BlockSpec((1, TQ, PAIR_W), lambda bi, i: (bi, i, 0)),
            pl.BlockSpec((1, PAIR_W, TQ), lambda bi, i: (bi, 0, i)),
            featout_spec,
            featout_spec,
        ],
        out_shape=[
            jax.ShapeDtypeStruct((b, s, d), jnp.float32),
            jax.ShapeDtypeStruct((b, s, PAIR_W), jnp.bfloat16),
            jax.ShapeDtypeStruct((b, PAIR_W, s), jnp.bfloat16),
            feat, feat,
        ],
        scratch_shapes=[pltpu.VMEM((d, TQ), jnp.bfloat16),
                        pltpu.VMEM((N_HEADS * A_KBLOCKS + 1, TQ, TQ), jnp.float32),
                        *[pltpu.VMEM((A_KBLOCKS * TQ, TQ), jnp.float32)] * N_SCORE_BUFS],
        compiler_params=pltpu.CompilerParams(
            dimension_semantics=("arbitrary", "arbitrary"), vmem_limit_bytes=VMEM_LIMIT),
        name="attn_a",
    )(x, qt4, k, k, k, vt4, vt4, vt4, gt4, bvec, wout, kvwk, nwt)


def _attn_b_kernel(h_ref, qt_ref, kp_ref, ko_ref, vp_ref, vo_ref, gt_ref, vecp_ref, veco_ref,
                   sink_ref, wout_ref, fg_ref, out_ref, zt_ref, biasp_ref, biaso_ref, *s_refs):
    i = pl.program_id(1)

    @pl.when((pl.program_id(0) == 0) & (i == 0))
    def _():
        _fill_bias_slabs(vecp_ref, biasp_ref, B_PREV, B_BANDS[:1])
        biasp_ref[N_HEADS] = jnp.full((B_PREV, TQ), MASKED, jnp.float32)
        _fill_bias_slabs(veco_ref, biaso_ref, TQ, B_BANDS[1:])

    rows = lax.broadcasted_iota(jnp.int32, (PAIR_W, TQ), 0)
    pairs_per_kv = N_PAIRS // B_KV_HEADS

    def scores_fn(p, hh, s_ref):
        head = 2 * p + hh
        qh = qt_ref[0, p, hh * HEAD_DIM:(hh + 1) * HEAD_DIM, :]
        kv0 = HEAD_DIM * (p // pairs_per_kv)
        kv_rows = (rows >= kv0) & (rows < kv0 + HEAD_DIM)
        qz = jnp.where(kv_rows, jnp.concatenate([qh, qh], axis=0), jnp.zeros((PAIR_W, TQ), qh.dtype))
        slab = jnp.where(i >= 1, head, N_HEADS)
        k_tiles = [kp_ref[0, TQ - B_PREV:, :], ko_ref[0]]
        bias_fns = [lambda rws, lanes: biasp_ref[slab, rws, lanes],
                    lambda rws, lanes: biaso_ref[head, rws, lanes]]
        return _scores_stage(k_tiles, qz, bias_fns, B_LIVE, s_ref)

    def pv_fn(p, hh, s_ref, m):
        head = 2 * p + hh
        kv_rows = slice(HEAD_DIM * (p // pairs_per_kv), HEAD_DIM * (p // pairs_per_kv + 1))
        v_tiles = [vp_ref[0, kv_rows, TQ - B_PREV:],
                   vo_ref[0, kv_rows, :]]
        o = _pv_stage(s_ref, m, v_tiles, B_LIVE, extra_logit=sink_ref[head])
        gate = gt_ref[0, p, hh * HEAD_DIM:(hh + 1) * HEAD_DIM, :].astype(jnp.float32)
        row0 = p * PAIR_W + hh * HEAD_DIM
        zt_ref[row0:row0 + HEAD_DIM, :] = (o * gate).astype(jnp.bfloat16)

    _head_pipeline(N_HEADS, scores_fn, pv_fn, s_refs)

    h2 = _out_and_residual(zt_ref, wout_ref, h_ref[0])
    out_ref[0] = h2 * _rms_scale(h2) * fg_ref[...]


def _attn_b(h, qbt, ksh, vsht, gbt, vecp, veco, sinks, wout, fg):
    b, s, d = h.shape
    qt4 = qbt.reshape(b, N_PAIRS, PAIR_W, s)
    gt4 = gbt.reshape(b, N_PAIRS, PAIR_W, s)
    feat_spec = pl.BlockSpec((1, N_PAIRS, PAIR_W, TQ), lambda bi, i: (bi, 0, 0, i))
    return pl.pallas_call(
        _attn_b_kernel,
        grid=(b, s // TQ),
        in_specs=[
            pl.BlockSpec((1, TQ, d), lambda bi, i: (bi, i, 0)),
            feat_spec,
            pl.BlockSpec((1, TQ, PAIR_W), lambda bi, i: (bi, jnp.maximum(i - 1, 0), 0)),
            pl.BlockSpec((1, TQ, PAIR_W), lambda bi, i: (bi, i, 0)),
            pl.BlockSpec((1, PAIR_W, TQ), lambda bi, i: (bi, 0, jnp.maximum(i - 1, 0))),
            pl.BlockSpec((1, PAIR_W, TQ), lambda bi, i: (bi, 0, i)),
            feat_spec,
            _const_spec(vecp.shape),
            _const_spec(veco.shape),
            _const_spec(sinks.shape),
            _const_spec(wout.shape),
            _const_spec(fg.shape),
        ],
        out_specs=pl.BlockSpec((1, TQ, d), lambda bi, i: (bi, i, 0)),
        out_shape=jax.ShapeDtypeStruct((b, s, d), jnp.float32),
        scratch_shapes=[pltpu.VMEM((d, TQ), jnp.bfloat16),
                        pltpu.VMEM((N_HEADS + 1, B_PREV, TQ), jnp.float32),
                        pltpu.VMEM((N_HEADS, TQ, TQ), jnp.float32),
                        *[pltpu.VMEM((B_PREV + TQ, TQ), jnp.float32)] * N_SCORE_BUFS],
        compiler_params=pltpu.CompilerParams(
            dimension_semantics=("arbitrary", "arbitrary"), vmem_limit_bytes=VMEM_LIMIT),
        name="attn_b",
    )(h, qt4, ksh, ksh, vsht, vsht, gt4, vecp, veco, sinks, wout, fg)


def _slab_diff():
    u = jnp.arange(BIAS_PERIOD)
    return jnp.where(u < TQ, u, u - BIAS_PERIOD)


def _bias_vecs_a(rel_bias):
    back = (A_KBLOCKS - 1 - jnp.arange(A_KBLOCKS))[:, None] * TQ
    dist = back + _slab_diff()[None, :]
    idx = jnp.clip(dist, -A_REL_CLIP, A_REL_CLIP) + A_REL_CLIP
    vec = jnp.transpose(rel_bias[idx], (0, 2, 1)).astype(jnp.float32)
    return vec[:, :, None, :]


def _t5_bucket(rel):
    nb = T5_BUCKETS // 2
    max_exact = nb // 2
    ret = jnp.where(rel > 0, nb, 0)
    n = jnp.abs(rel)
    nf = jnp.maximum(n, 1).astype(jnp.float32)
    large = max_exact + (jnp.log(nf / max_exact) / math.log(T5_MAX_DIST / max_exact)
                         * (nb - max_exact)).astype(jnp.int32)
    large = jnp.minimum(large, nb - 1)
    return ret + jnp.where(n < max_exact, n, large)


def _bias_vecs_b(t5_table):
    rel_own = -_slab_diff()
    rel = jnp.stack([rel_own - B_PREV, rel_own])
    vec = jnp.transpose(t5_table[_t5_bucket(rel)], (0, 2, 1)).astype(jnp.float32)
    return vec[0:1, :, None, :], vec[1:2, :, None, :]


def kernel(x, a_norm, a_w_in, a_rel_bias, a_w_out, kv_norm, kv_w, t5_bias,
           b_norm, b_w_in, b_sinks, b_w_out, final_norm):
    assert a_norm.shape[0] == 1 and b_norm.shape[0] == 1, "one A layer then one B layer"
    bf = jnp.bfloat16
    scale = HEAD_DIM ** -0.5 * LOG2E

    def transposed_with_scaled_q(w, row_gain=None):
        col_scale = jnp.where(jnp.arange(w.shape[1]) < D_MODEL, scale, 1.0).astype(w.dtype)
        if row_gain is not None:
            w = w * row_gain[:, None]
        return (w * col_scale[None, :]).T.astype(bf)

    qt, k, vt, gt = _proj_a(x, a_norm[0][None, :], transposed_with_scaled_q(a_w_in[0]))

    kvw_k, kvw_v = jnp.split(kv_w * kv_norm[:, None], 2, axis=-1)
    nwt = jnp.concatenate([transposed_with_scaled_q(b_w_in[0], b_norm[0]), kvw_v.T.astype(bf)], axis=0)
    h, ksh, vsht, qbt, gbt = _attn_a(
        x, qt, k, vt, gt, _bias_vecs_a(a_rel_bias[0] * LOG2E), a_w_out[0].astype(bf),
        kvw_k.astype(bf), nwt)

    vecp, veco = _bias_vecs_b(t5_bias * LOG2E)
    sinks = jnp.broadcast_to((b_sinks[0].astype(jnp.float32) * LOG2E)[:, None, None], (N_HEADS, 1, TQ))
    return _attn_b(h, qbt, ksh, vsht, gbt, vecp, veco, sinks, b_w_out[0].astype(bf),
                   final_norm[None, :])
```

```python
import functools
import math

import jax
import jax.numpy as jnp
from jax import lax
from jax.experimental import pallas as pl
from jax.experimental.pallas import tpu as pltpu

D_MODEL = 1024
HEAD_DIM = 64
N_HEADS = D_MODEL // HEAD_DIM
N_PAIRS = N_HEADS // 2
PAIR_W = 2 * HEAD_DIM
CHUNK = 64
RMS_EPS = 1e-6
A_LEFT_CHUNKS = 8
A_REL_CLIP = 256
B_KV_HEADS = 2
B_GROUP = N_HEADS // B_KV_HEADS
B_LEFT_CHUNKS = 2
T5_BUCKETS = 32
T5_MAX_DIST = 128

TQ = 256
A_KBLOCKS = A_LEFT_CHUNKS * CHUNK // TQ + 1
B_PREV = B_LEFT_CHUNKS * CHUNK
TM_PROJ = 512
FEAT_CHUNK = 512
W_T_COLS = 512
MASKED = -1e30
BIAS_PERIOD = 2 * TQ
SUBLANES = 8
LANES = 128
MAX_CHAINS = 2
BF16_ROWS = 16
LOG2E = math.log2(math.e)
N_SCORE_BUFS = 4
VMEM_LIMIT = 56 * 1024 * 1024

_NT = (((1,), (1,)), ((), ()))
_TN = (((0,), (0,)), ((), ()))


def _rms_scale(xf):
    return lax.rsqrt(jnp.mean(xf * xf, axis=-1, keepdims=True) + RMS_EPS)


def _silu(v):
    return v * jax.nn.sigmoid(v)


def _weight_t_kernel(*refs, has_head):
    if has_head:
        head_ref, hgain_ref, w_ref, cs_ref, gain_ref, o_ref = refs
        j = pl.program_id(0)

        @pl.when(j == 0)
        def _():
            o_ref[...] = (head_ref[...].T * hgain_ref[...]).astype(jnp.bfloat16)

        @pl.when(j > 0)
        def _():
            o_ref[...] = ((w_ref[...] * cs_ref[...]).T * gain_ref[...]).astype(jnp.bfloat16)
    else:
        w_ref, cs_ref, gain_ref, o_ref = refs
        o_ref[...] = ((w_ref[...] * cs_ref[...]).T * gain_ref[...]).astype(jnp.bfloat16)


def _weight_t(w, col_scale, row_gain, tc, head=None, head_block=0, head_gain=None):
    d, n = w.shape
    off = 0 if head is None else 1
    main_specs = [
        pl.BlockSpec((d, tc), lambda j: (0, jnp.maximum(j - off, 0))),
        pl.BlockSpec((1, tc), lambda j: (0, jnp.maximum(j - off, 0))),
        pl.BlockSpec((1, d), lambda j: (0, 0)),
    ]
    head_specs, head_args = [], []
    if head is not None:
        assert head.shape[0] == d and head.shape[1] % tc == 0
        head_specs = [pl.BlockSpec((d, tc), lambda j: (0, head_block)), pl.BlockSpec((1, d), lambda j: (0, 0))]
        head_args = [head, head_gain[None, :]]
    return pl.pallas_call(
        functools.partial(_weight_t_kernel, has_head=head is not None),
        grid=(n // tc + off,),
        in_specs=head_specs + main_specs,
        out_specs=pl.BlockSpec((tc, d), lambda j: (j, 0)),
        out_shape=jax.ShapeDtypeStruct((n + off * tc, d), jnp.bfloat16),
        compiler_params=pltpu.CompilerParams(dimension_semantics=("arbitrary",), vmem_limit_bytes=VMEM_LIMIT),
        name="weight_t",
    )(*head_args, w, col_scale[None, :], row_gain[None, :])


def _proj_a_kernel(x_ref, wt_ref, qt_ref, k_ref, vt_ref, gt_ref):
    xf = x_ref[0]
    xn = (xf * _rms_scale(xf)).astype(jnp.bfloat16)
    k = lax.dot_general(xn, wt_ref[D_MODEL:2 * D_MODEL, :], _NT,
                        preferred_element_type=jnp.float32)
    for p in range(N_PAIRS):
        k_ref[0, p] = k[:, p * PAIR_W:(p + 1) * PAIR_W].astype(jnp.bfloat16)
    for out_ref, base, act in ((qt_ref, 0, None), (vt_ref, 2 * D_MODEL, None), (gt_ref, 3 * D_MODEL, _silu)):
        for c in range(0, D_MODEL, FEAT_CHUNK):
            w = wt_ref[base + c:base + c + FEAT_CHUNK, :]
            yt = lax.dot_general(w, xn, _NT, preferred_element_type=jnp.float32)
            out_ref[0, c:c + FEAT_CHUNK, :] = (yt if act is None else act(yt)).astype(jnp.bfloat16)


def _proj_a(x, wt):
    b, s, d = x.shape
    feat = jax.ShapeDtypeStruct((b, d, s), jnp.bfloat16)
    return pl.pallas_call(
        _proj_a_kernel,
        grid=(b, s // TM_PROJ),
        in_specs=[
            pl.BlockSpec((1, TM_PROJ, d), lambda bi, i: (bi, i, 0)),
            _const_spec(wt.shape),
        ],
        out_specs=[
            pl.BlockSpec((1, d, TM_PROJ), lambda bi, i: (bi, 0, i)),
            pl.BlockSpec((1, N_PAIRS, TM_PROJ, PAIR_W), lambda bi, i: (bi, 0, i, 0)),
            pl.BlockSpec((1, d, TM_PROJ), lambda bi, i: (bi, 0, i)),
            pl.BlockSpec((1, d, TM_PROJ), lambda bi, i: (bi, 0, i)),
        ],
        out_shape=[feat, jax.ShapeDtypeStruct((b, N_PAIRS, s, PAIR_W), jnp.bfloat16), feat, feat],
        compiler_params=pltpu.CompilerParams(
            dimension_semantics=("arbitrary", "arbitrary"), vmem_limit_bytes=VMEM_LIMIT),
        name="proj_a",
    )(x, wt)


def _chunk_iota(shape, axis):
    return lax.shift_right_logical(lax.broadcasted_iota(jnp.int32, shape, axis), CHUNK.bit_length() - 1)


def _fill_bias_slabs(vec_ref, bias_ref, n_keys, bands):
    n = len(bands)
    kch = _chunk_iota((n_keys, TQ), 0)
    qch = _chunk_iota((n_keys, TQ), 1)

    def head_body(h, carry):
        for j, band in enumerate(bands):
            v = jnp.broadcast_to(vec_ref[j, h], (n_keys, BIAS_PERIOD))
            t = pltpu.roll(v, 0, 1, stride=1, stride_axis=0)[:, :TQ]
            if band is not None:
                t = jnp.where(band(kch, qch), t, MASKED)
            bias_ref[h * n + j] = t
        return carry

    lax.fori_loop(0, N_HEADS, head_body, 0)


def _live_rows(band, n_keys):
    chunks_per_half = LANES // CHUNK
    out = []
    for half in range(TQ // LANES):
        qchs = range(half * chunks_per_half, (half + 1) * chunks_per_half)
        live = [kc for kc in range(n_keys // CHUNK) if band is None or any(band(kc, qc) for qc in qchs)]
        out.append((live[0] * CHUNK, (live[-1] + 1) * CHUNK) if live else (0, 0))
    return tuple(out)


def _scores_stage(k_tiles, qz, bias_fns, live, s_ref):
    maxes, row = [], 0
    for half in range(TQ // LANES):
        maxes.append([None] * MAX_CHAINS)
    for kt, bias_fn, live_j in zip(k_tiles, bias_fns, live):
        s = jnp.dot(kt, qz, preferred_element_type=jnp.float32)
        for half, (r0, r1) in enumerate(live_j):
            lanes = slice(half * LANES, (half + 1) * LANES)
            acc = maxes[half]
            for r in range(r0 // SUBLANES, r1 // SUBLANES):
                rows = slice(r * SUBLANES, (r + 1) * SUBLANES)
                grp = s[rows, lanes] + bias_fn(rows, lanes)
                s_ref[row + r * SUBLANES:row + (r + 1) * SUBLANES, lanes] = grp
                c = r % MAX_CHAINS
                acc[c] = grp if acc[c] is None else jnp.maximum(acc[c], grp)
        row += s.shape[0]
    cols = [jnp.max(functools.reduce(jnp.maximum, [a for a in acc if a is not None]), axis=0, keepdims=True)
            for acc in maxes]
    return jnp.concatenate(cols, axis=1)


def _pv_stage(s_ref, m, v_tiles, live, extra_logit=None):
    if extra_logit is not None:
        m = jnp.maximum(m, extra_logit)
    acc, row = None, 0
    for vt, live_j in zip(v_tiles, live):
        n = vt.shape[1]
        halves = []
        for half, (r0, r1) in enumerate(live_j):
            lanes = slice(half * LANES, (half + 1) * LANES)
            parts = [jnp.zeros((r0, LANES), jnp.bfloat16)] if r0 else []
            if r1 > r0:
                parts.append(jnp.exp2(s_ref[row + r0:row + r1, lanes] - m[:, lanes]).astype(jnp.bfloat16))
            if n > r1:
                parts.append(jnp.zeros((n - r1, LANES), jnp.bfloat16))
            halves.append(parts[0] if len(parts) == 1 else jnp.concatenate(parts, axis=0))
        p = jnp.concatenate(halves, axis=1)
        v_ones = jnp.concatenate([vt, jnp.ones((BF16_ROWS, n), vt.dtype)], axis=0)
        part = jnp.dot(v_ones, p, preferred_element_type=jnp.float32)
        acc = part if acc is None else acc + part
        row += n
    l = acc[HEAD_DIM:HEAD_DIM + 1, :]
    if extra_logit is not None:
        l = l + jnp.exp2(extra_logit - m)
    return acc[:HEAD_DIM, :] / l


def _head_pipeline(n_heads, scores_fn, pv_fn, s_refs):
    x_refs, y_refs = s_refs[:2], s_refs[2:]
    n_pairs = n_heads // 2

    def group(pair, src, m_src, dst, next_pair):
        m_dst = None
        if next_pair is not None:
            m_dst = scores_fn(next_pair, 0, dst[0]), scores_fn(next_pair, 1, dst[1])
        pv_fn(pair, 0, src[0], m_src[0])
        pv_fn(pair, 1, src[1], m_src[1])
        return m_dst

    m = scores_fn(0, 0, x_refs[0]), scores_fn(0, 1, x_refs[1])
    for pair in range(n_pairs):
        src, dst = (x_refs, y_refs) if pair % 2 == 0 else (y_refs, x_refs)
        m = group(pair, src, m, dst, pair + 1 if pair + 1 < n_pairs else None)


A_BANDS = (lambda kc, qc: kc >= qc,
           None,
           lambda kc, qc: kc <= qc)
A_LIVE = tuple(_live_rows(band, TQ) for band in A_BANDS)
B_BANDS = (lambda kc, qc: qc <= kc,
           lambda kc, qc: (qc >= kc) & (qc - kc <= B_LEFT_CHUNKS))
B_LIVE = (_live_rows(B_BANDS[0], B_PREV), _live_rows(B_BANDS[1], TQ))


def _out_and_residual(zt_ref, wout_ref, res):
    y = lax.dot_general(zt_ref[...], wout_ref[...], _TN, preferred_element_type=jnp.float32)
    return res + y


def _attn_a_kernel(x_ref, qt_ref, k0_ref, k1_ref, k2_ref, v0_ref, v1_ref, v2_ref, gt_ref,
                   bvec_ref, wout_ref, kvwk_ref, nwt_ref,
                   h_ref, ksh_ref, vsht_ref, qbt_ref, gbt_ref, zt_ref, bias_ref, *s_refs):
    i = pl.program_id(1)

    @pl.when((pl.program_id(0) == 0) & (i == 0))
    def _():
        _fill_bias_slabs(bvec_ref, bias_ref, TQ, A_BANDS)
        bias_ref[N_HEADS * A_KBLOCKS] = jnp.full((TQ, TQ), MASKED, jnp.float32)

    k_refs = (k0_ref, k1_ref, k2_ref)
    v_refs = (v0_ref, v1_ref, v2_ref)
    n_slabs = N_HEADS * A_KBLOCKS
    rows = lax.broadcasted_iota(jnp.int32, (PAIR_W, TQ), 0)

    def scores_fn(p, hh, s_ref):
        q2 = qt_ref[0, p]
        own_rows = rows < HEAD_DIM if hh == 0 else rows >= HEAD_DIM
        qz = jnp.where(own_rows, q2, jnp.zeros_like(q2))
        k_tiles, bias_fns = [], []
        for j in range(A_KBLOCKS):
            in_seq = i - (A_KBLOCKS - 1) + j >= 0
            slab = jnp.where(in_seq, (2 * p + hh) * A_KBLOCKS + j, n_slabs)
            k_tiles.append(k_refs[j][0, p])
            bias_fns.append(lambda rws, lanes, slab=slab: bias_ref[slab, rws, lanes])
        return _scores_stage(k_tiles, qz, bias_fns, A_LIVE, s_ref)

    def pv_fn(p, hh, s_ref, m):
        half = slice(hh * HEAD_DIM, (hh + 1) * HEAD_DIM)
        o = _pv_stage(s_ref, m, [v_ref[0, p, half, :] for v_ref in v_refs], A_LIVE)
        gate = gt_ref[0, p, half, :].astype(jnp.float32)
        row0 = p * PAIR_W + hh * HEAD_DIM
        zt_ref[row0:row0 + HEAD_DIM, :] = (o * gate).astype(jnp.bfloat16)

    _head_pipeline(N_HEADS, scores_fn, pv_fn, s_refs)

    h = _out_and_residual(zt_ref, wout_ref, x_ref[0])
    h_ref[0] = h
    hn = (h * _rms_scale(h)).astype(jnp.bfloat16)
    ksh_ref[0] = jnp.dot(hn, kvwk_ref[...], preferred_element_type=jnp.float32).astype(jnp.bfloat16)
    for out_ref, base, act in ((qbt_ref, PAIR_W, None), (gbt_ref, PAIR_W + D_MODEL, _silu)):
        for c in range(0, D_MODEL, FEAT_CHUNK):
            first = out_ref is qbt_ref and c == 0
            start = base + c - (PAIR_W if first else 0)
            yt = lax.dot_general(nwt_ref[start:base + c + FEAT_CHUNK, :], hn, _NT,
                                 preferred_element_type=jnp.float32)
            if first:
                vsht_ref[0] = yt[:PAIR_W].astype(jnp.bfloat16)
                yt = yt[PAIR_W:]
            out_ref[0, c:c + FEAT_CHUNK, :] = (yt if act is None else act(yt)).astype(jnp.bfloat16)


def _const_spec(shape):
    return pl.BlockSpec(shape, lambda bi, i: (0,) * len(shape), pipeline_mode=pl.Buffered(1))


def _attn_a(x, qt, k, vt, gt, bvec, wout, kvwk, nwt):
    b, s, d = x.shape
    qt4 = qt.reshape(b, N_PAIRS, PAIR_W, s)
    vt4 = vt.reshape(b, N_PAIRS, PAIR_W, s)
    gt4 = gt.reshape(b, N_PAIRS, PAIR_W, s)
    back = A_KBLOCKS - 1

    def kspec(j):
        return pl.BlockSpec((1, N_PAIRS, TQ, PAIR_W),
                            lambda bi, i: (bi, 0, jnp.maximum(i - back + j, 0), 0))

    def vspec(j):
        return pl.BlockSpec((1, N_PAIRS, PAIR_W, TQ),
                            lambda bi, i: (bi, 0, 0, jnp.maximum(i - back + j, 0)))

    feat_spec = pl.BlockSpec((1, N_PAIRS, PAIR_W, TQ), lambda bi, i: (bi, 0, 0, i))
    featout_spec = pl.BlockSpec((1, d, TQ), lambda bi, i: (bi, 0, i))
    feat = jax.ShapeDtypeStruct((b, d, s), jnp.bfloat16)
    return pl.pallas_call(
        _attn_a_kernel,
        grid=(b, s // TQ),
        in_specs=[
            pl.BlockSpec((1, TQ, d), lambda bi, i: (bi, i, 0)),
            feat_spec,
            kspec(0), kspec(1), kspec(2),
            vspec(0), vspec(1), vspec(2),
            feat_spec,
            _const_spec(bvec.shape),
            _const_spec(wout.shape),
            _const_spec(kvwk.shape),
            _const_spec(nwt.shape),
        ],
        out_specs=[
            pl.BlockSpec((1, TQ, d), lambda bi, i: (bi, i, 0)),
            pl.BlockSpec((1, TQ, PAIR_W), lambda bi, i: (bi, i, 0)),
            pl.BlockSpec((1, PAIR_W, TQ), lambda bi, i: (bi, 0, i)),
            featout_spec,
            featout_spec,
        ],
        out_shape=[
            jax.ShapeDtypeStruct((b, s, d), jnp.float32),
            jax.ShapeDtypeStruct((b, s, PAIR_W), jnp.bfloat16),
            jax.ShapeDtypeStruct((b, PAIR_W, s), jnp.bfloat16),
            feat, feat,
        ],
        scratch_shapes=[pltpu.VMEM((d, TQ), jnp.bfloat16),
                        pltpu.VMEM((N_HEADS * A_KBLOCKS + 1, TQ, TQ), jnp.float32),
                        *[pltpu.VMEM((A_KBLOCKS * TQ, TQ), jnp.float32)] * N_SCORE_BUFS],
        compiler_params=pltpu.CompilerParams(
            dimension_semantics=("arbitrary", "arbitrary"), vmem_limit_bytes=VMEM_LIMIT),
        name="attn_a",
    )(x, qt4, k, k, k, vt4, vt4, vt4, gt4, bvec, wout, kvwk, nwt)


def _attn_b_kernel(h_ref, qt_ref, kp_ref, ko_ref, vp_ref, vo_ref, gt_ref, vecp_ref, veco_ref,
                   sink_ref, wout_ref, fg_ref, out_ref, zt_ref, biasp_ref, biaso_ref, *s_refs):
    i = pl.program_id(1)

    @pl.when((pl.program_id(0) == 0) & (i == 0))
    def _():
        _fill_bias_slabs(vecp_ref, biasp_ref, B_PREV, B_BANDS[:1])
        biasp_ref[N_HEADS] = jnp.full((B_PREV, TQ), MASKED, jnp.float32)
        _fill_bias_slabs(veco_ref, biaso_ref, TQ, B_BANDS[1:])

    rows = lax.broadcasted_iota(jnp.int32, (PAIR_W, TQ), 0)
    pairs_per_kv = N_PAIRS // B_KV_HEADS

    def scores_fn(p, hh, s_ref):
        head = 2 * p + hh
        qh = qt_ref[0, p, hh * HEAD_DIM:(hh + 1) * HEAD_DIM, :]
        kv0 = HEAD_DIM * (p // pairs_per_kv)
        kv_rows = (rows >= kv0) & (rows < kv0 + HEAD_DIM)
        qz = jnp.where(kv_rows, jnp.concatenate([qh, qh], axis=0), jnp.zeros((PAIR_W, TQ), qh.dtype))
        slab = jnp.where(i >= 1, head, N_HEADS)
        k_tiles = [kp_ref[0, TQ - B_PREV:, :], ko_ref[0]]
        bias_fns = [lambda rws, lanes: biasp_ref[slab, rws, lanes],
                    lambda rws, lanes: biaso_ref[head, rws, lanes]]
        return _scores_stage(k_tiles, qz, bias_fns, B_LIVE, s_ref)

    def pv_fn(p, hh, s_ref, m):
        head = 2 * p + hh
        kv_rows = slice(HEAD_DIM * (p // pairs_per_kv), HEAD_DIM * (p // pairs_per_kv + 1))
        v_tiles = [vp_ref[0, kv_rows, TQ - B_PREV:],
                   vo_ref[0, kv_rows, :]]
        o = _pv_stage(s_ref, m, v_tiles, B_LIVE, extra_logit=sink_ref[head])
        gate = gt_ref[0, p, hh * HEAD_DIM:(hh + 1) * HEAD_DIM, :].astype(jnp.float32)
        row0 = p * PAIR_W + hh * HEAD_DIM
        zt_ref[row0:row0 + HEAD_DIM, :] = (o * gate).astype(jnp.bfloat16)

    _head_pipeline(N_HEADS, scores_fn, pv_fn, s_refs)

    h2 = _out_and_residual(zt_ref, wout_ref, h_ref[0])
    out_ref[0] = h2 * _rms_scale(h2) * fg_ref[...]


def _attn_b(h, qbt, ksh, vsht, gbt, vecp, veco, sinks, wout, fg):
    b, s, d = h.shape
    qt4 = qbt.reshape(b, N_PAIRS, PAIR_W, s)
    gt4 = gbt.reshape(b, N_PAIRS, PAIR_W, s)
    feat_spec = pl.BlockSpec((1, N_PAIRS, PAIR_W, TQ), lambda bi, i: (bi, 0, 0, i))
    return pl.pallas_call(
        _attn_b_kernel,
        grid=(b, s // TQ),
        in_specs=[
            pl.BlockSpec((1, TQ, d), lambda bi, i: (bi, i, 0)),
            feat_spec,
            pl.BlockSpec((1, TQ, PAIR_W), lambda bi, i: (bi, jnp.maximum(i - 1, 0), 0)),
            pl.BlockSpec((1, TQ, PAIR_W), lambda bi, i: (bi, i, 0)),
            pl.BlockSpec((1, PAIR_W, TQ), lambda bi, i: (bi, 0, jnp.maximum(i - 1, 0))),
            pl.BlockSpec((1, PAIR_W, TQ), lambda bi, i: (bi, 0, i)),
            feat_spec,
            _const_spec(vecp.shape),
            _const_spec(veco.shape),
            _const_spec(sinks.shape),
            _const_spec(wout.shape),
            _const_spec(fg.shape),
        ],
        out_specs=pl.BlockSpec((1, TQ, d), lambda bi, i: (bi, i, 0)),
        out_shape=jax.ShapeDtypeStruct((b, s, d), jnp.float32),
        scratch_shapes=[pltpu.VMEM((d, TQ), jnp.bfloat16),
                        pltpu.VMEM((N_HEADS + 1, B_PREV, TQ), jnp.float32),
                        pltpu.VMEM((N_HEADS, TQ, TQ), jnp.float32),
                        *[pltpu.VMEM((B_PREV + TQ, TQ), jnp.float32)] * N_SCORE_BUFS],
        compiler_params=pltpu.CompilerParams(
            dimension_semantics=("arbitrary", "arbitrary"), vmem_limit_bytes=VMEM_LIMIT),
        name="attn_b",
    )(h, qt4, ksh, ksh, vsht, vsht, gt4, vecp, veco, sinks, wout, fg)


def _slab_diff():
    u = jnp.arange(BIAS_PERIOD)
    return jnp.where(u < TQ, u, u - BIAS_PERIOD)


def _bias_vecs_a(rel_bias):
    back = (A_KBLOCKS - 1 - jnp.arange(A_KBLOCKS))[:, None] * TQ
    dist = back + _slab_diff()[None, :]
    idx = jnp.clip(dist, -A_REL_CLIP, A_REL_CLIP) + A_REL_CLIP
    vec = jnp.transpose(rel_bias[idx], (0, 2, 1)).astype(jnp.float32)
    return vec[:, :, None, :]


def _t5_bucket(rel):
    nb = T5_BUCKETS // 2
    max_exact = nb // 2
    ret = jnp.where(rel > 0, nb, 0)
    n = jnp.abs(rel)
    nf = jnp.maximum(n, 1).astype(jnp.float32)
    large = max_exact + (jnp.log(nf / max_exact) / math.log(T5_MAX_DIST / max_exact)
                         * (nb - max_exact)).astype(jnp.int32)
    large = jnp.minimum(large, nb - 1)
    return ret + jnp.where(n < max_exact, n, large)


def _bias_vecs_b(t5_table):
    rel_own = -_slab_diff()
    rel = jnp.stack([rel_own - B_PREV, rel_own])
    vec = jnp.transpose(t5_table[_t5_bucket(rel)], (0, 2, 1)).astype(jnp.float32)
    return vec[0:1, :, None, :], vec[1:2, :, None, :]


def kernel(x, a_norm, a_w_in, a_rel_bias, a_w_out, kv_norm, kv_w, t5_bias,
           b_norm, b_w_in, b_sinks, b_w_out, final_norm):
    assert a_norm.shape[0] == 1 and b_norm.shape[0] == 1, "one A layer then one B layer"
    bf = jnp.bfloat16
    scale = HEAD_DIM ** -0.5 * LOG2E

    def q_col_scale(n):
        return jnp.where(jnp.arange(n) < D_MODEL, scale, 1.0).astype(jnp.float32)

    wt_a = _weight_t(a_w_in[0], q_col_scale(a_w_in.shape[2]), a_norm[0], W_T_COLS)
    qt, k, vt, gt = _proj_a(x, wt_a)

    nwt = _weight_t(b_w_in[0], q_col_scale(b_w_in.shape[2]), b_norm[0], PAIR_W,
                    head=kv_w, head_block=1, head_gain=kv_norm)
    kvw_k = (kv_w[:, :PAIR_W] * kv_norm[:, None]).astype(bf)
    h, ksh, vsht, qbt, gbt = _attn_a(
        x, qt, k, vt, gt, _bias_vecs_a(a_rel_bias[0] * LOG2E), a_w_out[0].astype(bf), kvw_k, nwt)

    vecp, veco = _bias_vecs_b(t5_bias * LOG2E)
    sinks = jnp.broadcast_to((b_sinks[0].astype(jnp.float32) * LOG2E)[:, None, None], (N_HEADS, 1, TQ))
    return _attn_b(h, qbt, ksh, vsht, gbt, vecp, veco, sinks, b_w_out[0].astype(bf),
                   final_norm[None, :])
```

```python
import functools
import math

import jax
import jax.numpy as jnp
from jax import lax
from jax.experimental import pallas as pl
from jax.experimental.pallas import tpu as pltpu

D_MODEL = 1024
HEAD_DIM = 64
N_HEADS = D_MODEL // HEAD_DIM
N_PAIRS = N_HEADS // 2
PAIR_W = 2 * HEAD_DIM
CHUNK = 64
RMS_EPS = 1e-6
A_LEFT_CHUNKS = 8
A_REL_CLIP = 256
B_KV_HEADS = 2
B_GROUP = N_HEADS // B_KV_HEADS
B_LEFT_CHUNKS = 2
T5_BUCKETS = 32
T5_MAX_DIST = 128

TQ = 256
A_KBLOCKS = A_LEFT_CHUNKS * CHUNK // TQ + 1
B_PREV = B_LEFT_CHUNKS * CHUNK
TM_PROJ = 512
FEAT_CHUNK = 512
W_T_COLS = 512
MASKED = -1e30
BIAS_PERIOD = 2 * TQ
SUBLANES = 8
LANES = 128
MAX_CHAINS = 2
BF16_ROWS = 16
LOG2E = math.log2(math.e)
N_SCORE_BUFS = 4
VMEM_LIMIT = 56 * 1024 * 1024

_NT = (((1,), (1,)), ((), ()))
_TN = (((0,), (0,)), ((), ()))


def _rms_scale(xf):
    return lax.rsqrt(jnp.mean(xf * xf, axis=-1, keepdims=True) + RMS_EPS)


def _silu(v):
    return v * jax.nn.sigmoid(v)


def _weight_t_kernel(*refs, has_head):
    if has_head:
        head_ref, hgain_ref, w_ref, cs_ref, gain_ref, o_ref = refs
        j = pl.program_id(0)

        @pl.when(j == 0)
        def _():
            pad = o_ref.shape[0] - head_ref.shape[1]
            o_ref[:pad, :] = jnp.zeros((pad, o_ref.shape[1]), jnp.bfloat16)
            o_ref[pad:, :] = (head_ref[...].T * hgain_ref[...]).astype(jnp.bfloat16)

        @pl.when(j > 0)
        def _():
            o_ref[...] = ((w_ref[...] * cs_ref[...]).T * gain_ref[...]).astype(jnp.bfloat16)
    else:
        w_ref, cs_ref, gain_ref, o_ref = refs
        o_ref[...] = ((w_ref[...] * cs_ref[...]).T * gain_ref[...]).astype(jnp.bfloat16)


def _weight_t(w, col_scale, row_gain, tc, head=None, head_block=0, head_w=None, head_gain=None):
    d, n = w.shape
    off = 0 if head is None else 1
    main_specs = [
        pl.BlockSpec((d, tc), lambda j: (0, jnp.maximum(j - off, 0))),
        pl.BlockSpec((1, tc), lambda j: (0, jnp.maximum(j - off, 0))),
        pl.BlockSpec((1, d), lambda j: (0, 0)),
    ]
    head_specs, head_args = [], []
    if head is not None:
        assert head.shape[0] == d and head.shape[1] % head_w == 0 and head_w <= tc
        head_specs = [pl.BlockSpec((d, head_w), lambda j: (0, head_block)), pl.BlockSpec((1, d), lambda j: (0, 0))]
        head_args = [head, head_gain[None, :]]
    return pl.pallas_call(
        functools.partial(_weight_t_kernel, has_head=head is not None),
        grid=(n // tc + off,),
        in_specs=head_specs + main_specs,
        out_specs=pl.BlockSpec((tc, d), lambda j: (j, 0)),
        out_shape=jax.ShapeDtypeStruct((n + off * tc, d), jnp.bfloat16),
        compiler_params=pltpu.CompilerParams(dimension_semantics=("arbitrary",), vmem_limit_bytes=VMEM_LIMIT),
        name="weight_t",
    )(*head_args, w, col_scale[None, :], row_gain[None, :])


def _proj_a_kernel(x_ref, wt_ref, qt_ref, k_ref, vt_ref, gt_ref):
    xf = x_ref[0]
    xn = (xf * _rms_scale(xf)).astype(jnp.bfloat16)
    k = lax.dot_general(xn, wt_ref[D_MODEL:2 * D_MODEL, :], _NT,
                        preferred_element_type=jnp.float32)
    for p in range(N_PAIRS):
        k_ref[0, p] = k[:, p * PAIR_W:(p + 1) * PAIR_W].astype(jnp.bfloat16)
    for out_ref, base, act in ((gt_ref, 3 * D_MODEL, _silu), (vt_ref, 2 * D_MODEL, None), (qt_ref, 0, None)):
        for c in range(0, D_MODEL, FEAT_CHUNK):
            w = wt_ref[base + c:base + c + FEAT_CHUNK, :]
            yt = lax.dot_general(w, xn, _NT, preferred_element_type=jnp.float32)
            out_ref[0, c:c + FEAT_CHUNK, :] = (yt if act is None else act(yt)).astype(jnp.bfloat16)


def _proj_a(x, wt):
    b, s, d = x.shape
    feat = jax.ShapeDtypeStruct((b, d, s), jnp.bfloat16)
    return pl.pallas_call(
        _proj_a_kernel,
        grid=(b, s // TM_PROJ),
        in_specs=[
            pl.BlockSpec((1, TM_PROJ, d), lambda bi, i: (bi, i, 0)),
            _const_spec(wt.shape),
        ],
        out_specs=[
            pl.BlockSpec((1, d, TM_PROJ), lambda bi, i: (bi, 0, i)),
            pl.BlockSpec((1, N_PAIRS, TM_PROJ, PAIR_W), lambda bi, i: (bi, 0, i, 0)),
            pl.BlockSpec((1, d, TM_PROJ), lambda bi, i: (bi, 0, i)),
            pl.BlockSpec((1, d, TM_PROJ), lambda bi, i: (bi, 0, i)),
        ],
        out_shape=[feat, jax.ShapeDtypeStruct((b, N_PAIRS, s, PAIR_W), jnp.bfloat16), feat, feat],
        compiler_params=pltpu.CompilerParams(
            dimension_semantics=("arbitrary", "arbitrary"), vmem_limit_bytes=VMEM_LIMIT),
        name="proj_a",
    )(x, wt)


def _chunk_iota(shape, axis):
    return lax.shift_right_logical(lax.broadcasted_iota(jnp.int32, shape, axis), CHUNK.bit_length() - 1)


def _fill_bias_slabs(vec_ref, bias_ref, n_keys, bands):
    n = len(bands)
    kch = _chunk_iota((n_keys, TQ), 0)
    qch = _chunk_iota((n_keys, TQ), 1)

    def head_body(h, carry):
        for j, band in enumerate(bands):
            v = jnp.broadcast_to(vec_ref[j, h], (n_keys, BIAS_PERIOD))
            t = pltpu.roll(v, 0, 1, stride=1, stride_axis=0)[:, :TQ]
            if band is not None:
                t = jnp.where(band(kch, qch), t, MASKED)
            bias_ref[h * n + j] = t
        return carry

    lax.fori_loop(0, N_HEADS, head_body, 0)


def _live_rows(band, n_keys):
    chunks_per_half = LANES // CHUNK
    out = []
    for half in range(TQ // LANES):
        qchs = range(half * chunks_per_half, (half + 1) * chunks_per_half)
        live = [kc for kc in range(n_keys // CHUNK) if band is None or any(band(kc, qc) for qc in qchs)]
        out.append((live[0] * CHUNK, (live[-1] + 1) * CHUNK) if live else (0, 0))
    return tuple(out)


def _scores_stage(k_tiles, qz, bias_fns, live, s_ref):
    maxes, row = [], 0
    for half in range(TQ // LANES):
        maxes.append([None] * MAX_CHAINS)
    for kt, bias_fn, live_j in zip(k_tiles, bias_fns, live):
        s = jnp.dot(kt, qz, preferred_element_type=jnp.float32)
        for half, (r0, r1) in enumerate(live_j):
            lanes = slice(half * LANES, (half + 1) * LANES)
            acc = maxes[half]
            for r in range(r0 // SUBLANES, r1 // SUBLANES):
                rows = slice(r * SUBLANES, (r + 1) * SUBLANES)
                grp = s[rows, lanes] + bias_fn(rows, lanes)
                s_ref[row + r * SUBLANES:row + (r + 1) * SUBLANES, lanes] = grp
                c = r % MAX_CHAINS
                acc[c] = grp if acc[c] is None else jnp.maximum(acc[c], grp)
        row += s.shape[0]
    cols = [jnp.max(functools.reduce(jnp.maximum, [a for a in acc if a is not None]), axis=0, keepdims=True)
            for acc in maxes]
    return jnp.concatenate(cols, axis=1)


def _pv_stage(s_ref, m, v_tiles, live, extra_logit=None):
    if extra_logit is not None:
        m = jnp.maximum(m, extra_logit)
    acc, row = None, 0
    for vt, live_j in zip(v_tiles, live):
        n = vt.shape[1]
        halves = []
        for half, (r0, r1) in enumerate(live_j):
            lanes = slice(half * LANES, (half + 1) * LANES)
            parts = [jnp.zeros((r0, LANES), jnp.bfloat16)] if r0 else []
            if r1 > r0:
                parts.append(jnp.exp2(s_ref[row + r0:row + r1, lanes] - m[:, lanes]).astype(jnp.bfloat16))
            if n > r1:
                parts.append(jnp.zeros((n - r1, LANES), jnp.bfloat16))
            halves.append(parts[0] if len(parts) == 1 else jnp.concatenate(parts, axis=0))
        p = jnp.concatenate(halves, axis=1)
        v_ones = jnp.concatenate([vt, jnp.ones((BF16_ROWS, n), vt.dtype)], axis=0)
        part = jnp.dot(v_ones, p, preferred_element_type=jnp.float32)
        acc = part if acc is None else acc + part
        row += n
    l = acc[HEAD_DIM:HEAD_DIM + 1, :]
    if extra_logit is not None:
        l = l + jnp.exp2(extra_logit - m)
    return acc[:HEAD_DIM, :] / l


def _head_pipeline(n_heads, scores_fn, pv_fn, s_refs):
    x_refs, y_refs = s_refs[:2], s_refs[2:]
    n_pairs = n_heads // 2

    def group(pair, src, m_src, dst, next_pair):
        m_dst = None
        if next_pair is not None:
            m_dst = scores_fn(next_pair, 0, dst[0]), scores_fn(next_pair, 1, dst[1])
        pv_fn(pair, 0, src[0], m_src[0])
        pv_fn(pair, 1, src[1], m_src[1])
        return m_dst

    m = scores_fn(0, 0, x_refs[0]), scores_fn(0, 1, x_refs[1])
    for pair in range(n_pairs):
        src, dst = (x_refs, y_refs) if pair % 2 == 0 else (y_refs, x_refs)
        m = group(pair, src, m, dst, pair + 1 if pair + 1 < n_pairs else None)


A_BANDS = (lambda kc, qc: kc >= qc,
           None,
           lambda kc, qc: kc <= qc)
A_LIVE = tuple(_live_rows(band, TQ) for band in A_BANDS)
B_BANDS = (lambda kc, qc: qc <= kc,
           lambda kc, qc: (qc >= kc) & (qc - kc <= B_LEFT_CHUNKS))
B_LIVE = (_live_rows(B_BANDS[0], B_PREV), _live_rows(B_BANDS[1], TQ))


def _out_and_residual(zt_ref, wout_ref, res):
    y = lax.dot_general(zt_ref[...], wout_ref[...], _TN, preferred_element_type=jnp.float32)
    return res + y


def _attn_a_kernel(x_ref, qt_ref, k0_ref, k1_ref, k2_ref, v0_ref, v1_ref, v2_ref, gt_ref,
                   bvec_ref, wout_ref, kvwk_ref, nwt_ref,
                   h_ref, ksh_ref, vsht_ref, qbt_ref, gbt_ref, zt_ref, bias_ref, *s_refs):
    i = pl.program_id(1)

    @pl.when((pl.program_id(0) == 0) & (i == 0))
    def _():
        _fill_bias_slabs(bvec_ref, bias_ref, TQ, A_BANDS)
        bias_ref[N_HEADS * A_KBLOCKS] = jnp.full((TQ, TQ), MASKED, jnp.float32)

    k_refs = (k0_ref, k1_ref, k2_ref)
    v_refs = (v0_ref, v1_ref, v2_ref)
    n_slabs = N_HEADS * A_KBLOCKS
    rows = lax.broadcasted_iota(jnp.int32, (PAIR_W, TQ), 0)

    def scores_fn(p, hh, s_ref):
        q2 = qt_ref[0, p]
        own_rows = rows < HEAD_DIM if hh == 0 else rows >= HEAD_DIM
        qz = jnp.where(own_rows, q2, jnp.zeros_like(q2))
        k_tiles, bias_fns = [], []
        for j in range(A_KBLOCKS):
            in_seq = i - (A_KBLOCKS - 1) + j >= 0
            slab = jnp.where(in_seq, (2 * p + hh) * A_KBLOCKS + j, n_slabs)
            k_tiles.append(k_refs[j][0, p])
            bias_fns.append(lambda rws, lanes, slab=slab: bias_ref[slab, rws, lanes])
        return _scores_stage(k_tiles, qz, bias_fns, A_LIVE, s_ref)

    def pv_fn(p, hh, s_ref, m):
        half = slice(hh * HEAD_DIM, (hh + 1) * HEAD_DIM)
        o = _pv_stage(s_ref, m, [v_ref[0, p, half, :] for v_ref in v_refs], A_LIVE)
        gate = gt_ref[0, p, half, :].astype(jnp.float32)
        row0 = p * PAIR_W + hh * HEAD_DIM
        zt_ref[row0:row0 + HEAD_DIM, :] = (o * gate).astype(jnp.bfloat16)

    _head_pipeline(N_HEADS, scores_fn, pv_fn, s_refs)

    h = _out_and_residual(zt_ref, wout_ref, x_ref[0])
    h_ref[0] = h
    hn = (h * _rms_scale(h)).astype(jnp.bfloat16)
    ksh_ref[0] = jnp.dot(hn, kvwk_ref[...], preferred_element_type=jnp.float32).astype(jnp.bfloat16)
    for out_ref, base, act in ((qbt_ref, W_T_COLS, None), (gbt_ref, W_T_COLS + D_MODEL, _silu)):
        for c in range(0, D_MODEL, FEAT_CHUNK):
            first = out_ref is qbt_ref and c == 0
            start = base + c - (PAIR_W if first else 0)
            yt = lax.dot_general(nwt_ref[start:base + c + FEAT_CHUNK, :], hn, _NT,
                                 preferred_element_type=jnp.float32)
            if first:
                vsht_ref[0] = yt[:PAIR_W].astype(jnp.bfloat16)
                yt = yt[PAIR_W:]
            out_ref[0, c:c + FEAT_CHUNK, :] = (yt if act is None else act(yt)).astype(jnp.bfloat16)


def _const_spec(shape):
    return pl.BlockSpec(shape, lambda bi, i: (0,) * len(shape), pipeline_mode=pl.Buffered(1))


def _attn_a(x, qt, k, vt, gt, bvec, wout, kvwk, nwt):
    b, s, d = x.shape
    qt4 = qt.reshape(b, N_PAIRS, PAIR_W, s)
    vt4 = vt.reshape(b, N_PAIRS, PAIR_W, s)
    gt4 = gt.reshape(b, N_PAIRS, PAIR_W, s)
    back = A_KBLOCKS - 1

    def kspec(j):
        return pl.BlockSpec((1, N_PAIRS, TQ, PAIR_W),
                            lambda bi, i: (bi, 0, jnp.maximum(i - back + j, 0), 0))

    def vspec(j):
        return pl.BlockSpec((1, N_PAIRS, PAIR_W, TQ),
                            lambda bi, i: (bi, 0, 0, jnp.maximum(i - back + j, 0)))

    feat_spec = pl.BlockSpec((1, N_PAIRS, PAIR_W, TQ), lambda bi, i: (bi, 0, 0, i))
    featout_spec = pl.BlockSpec((1, d, TQ), lambda bi, i: (bi, 0, i))
    feat = jax.ShapeDtypeStruct((b, d, s), jnp.bfloat16)
    return pl.pallas_call(
        _attn_a_kernel,
        grid=(b, s // TQ),
        in_specs=[
            pl.BlockSpec((1, TQ, d), lambda bi, i: (bi, i, 0)),
            feat_spec,
            kspec(0), kspec(1), kspec(2),
            vspec(0), vspec(1), vspec(2),
            feat_spec,
            _const_spec(bvec.shape),
            _const_spec(wout.shape),
            _const_spec(kvwk.shape),
            _const_spec(nwt.shape),
        ],
        out_specs=[
            pl.BlockSpec((1, TQ, d), lambda bi, i: (bi, i, 0)),
            pl.BlockSpec((1, TQ, PAIR_W), lambda bi, i: (bi, i, 0)),
            pl.BlockSpec((1, PAIR_W, TQ), lambda bi, i: (bi, 0, i)),
            featout_spec,
            featout_spec,
        ],
        out_shape=[
            jax.ShapeDtypeStruct((b, s, d), jnp.float32),
            jax.ShapeDtypeStruct((b, s, PAIR_W), jnp.bfloat16),
            jax.ShapeDtypeStruct((b, PAIR_W, s), jnp.bfloat16),
            feat, feat,
        ],
        scratch_shapes=[pltpu.VMEM((d, TQ), jnp.bfloat16),
                        pltpu.VMEM((N_HEADS * A_KBLOCKS + 1, TQ, TQ), jnp.float32),
                        *[pltpu.VMEM((A_KBLOCKS * TQ, TQ), jnp.float32)] * N_SCORE_BUFS],
        compiler_params=pltpu.CompilerParams(
            dimension_semantics=("arbitrary", "arbitrary"), vmem_limit_bytes=VMEM_LIMIT),
        name="attn_a",
    )(x, qt4, k, k, k, vt4, vt4, vt4, gt4, bvec, wout, kvwk, nwt)


def _attn_b_kernel(h_ref, qt_ref, kp_ref, ko_ref, vp_ref, vo_ref, gt_ref, vecp_ref, veco_ref,
                   sink_ref, wout_ref, fg_ref, out_ref, zt_ref, biasp_ref, biaso_ref, *s_refs):
    i = pl.program_id(1)

    @pl.when((pl.program_id(0) == 0) & (i == 0))
    def _():
        _fill_bias_slabs(vecp_ref, biasp_ref, B_PREV, B_BANDS[:1])
        biasp_ref[N_HEADS] = jnp.full((B_PREV, TQ), MASKED, jnp.float32)
        _fill_bias_slabs(veco_ref, biaso_ref, TQ, B_BANDS[1:])

    rows = lax.broadcasted_iota(jnp.int32, (PAIR_W, TQ), 0)
    pairs_per_kv = N_PAIRS // B_KV_HEADS

    def scores_fn(p, hh, s_ref):
        head = 2 * p + hh
        qh = qt_ref[0, p, hh * HEAD_DIM:(hh + 1) * HEAD_DIM, :]
        kv0 = HEAD_DIM * (p // pairs_per_kv)
        kv_rows = (rows >= kv0) & (rows < kv0 + HEAD_DIM)
        qz = jnp.where(kv_rows, jnp.concatenate([qh, qh], axis=0), jnp.zeros((PAIR_W, TQ), qh.dtype))
        slab = jnp.where(i >= 1, head, N_HEADS)
        k_tiles = [kp_ref[0, TQ - B_PREV:, :], ko_ref[0]]
        bias_fns = [lambda rws, lanes: biasp_ref[slab, rws, lanes],
                    lambda rws, lanes: biaso_ref[head, rws, lanes]]
        return _scores_stage(k_tiles, qz, bias_fns, B_LIVE, s_ref)

    def pv_fn(p, hh, s_ref, m):
        head = 2 * p + hh
        kv_rows = slice(HEAD_DIM * (p // pairs_per_kv), HEAD_DIM * (p // pairs_per_kv + 1))
        v_tiles = [vp_ref[0, kv_rows, TQ - B_PREV:],
                   vo_ref[0, kv_rows, :]]
        o = _pv_stage(s_ref, m, v_tiles, B_LIVE, extra_logit=sink_ref[head])
        gate = gt_ref[0, p, hh * HEAD_DIM:(hh + 1) * HEAD_DIM, :].astype(jnp.float32)
        row0 = p * PAIR_W + hh * HEAD_DIM
        zt_ref[row0:row0 + HEAD_DIM, :] = (o * gate).astype(jnp.bfloat16)

    _head_pipeline(N_HEADS, scores_fn, pv_fn, s_refs)

    h2 = _out_and_residual(zt_ref, wout_ref, h_ref[0])
    out_ref[0] = h2 * _rms_scale(h2) * fg_ref[...]


def _attn_b(h, qbt, ksh, vsht, gbt, vecp, veco, sinks, wout, fg):
    b, s, d = h.shape
    qt4 = qbt.reshape(b, N_PAIRS, PAIR_W, s)
    gt4 = gbt.reshape(b, N_PAIRS, PAIR_W, s)
    feat_spec = pl.BlockSpec((1, N_PAIRS, PAIR_W, TQ), lambda bi, i: (bi, 0, 0, i))
    return pl.pallas_call(
        _attn_b_kernel,
        grid=(b, s // TQ),
        in_specs=[
            pl.BlockSpec((1, TQ, d), lambda bi, i: (bi, i, 0)),
            feat_spec,
            pl.BlockSpec((1, TQ, PAIR_W), lambda bi, i: (bi, jnp.maximum(i - 1, 0), 0)),
            pl.BlockSpec((1, TQ, PAIR_W), lambda bi, i: (bi, i, 0)),
            pl.BlockSpec((1, PAIR_W, TQ), lambda bi, i: (bi, 0, jnp.maximum(i - 1, 0))),
            pl.BlockSpec((1, PAIR_W, TQ), lambda bi, i: (bi, 0, i)),
            feat_spec,
            _const_spec(vecp.shape),
            _const_spec(veco.shape),
            _const_spec(sinks.shape),
            _const_spec(wout.shape),
            _const_spec(fg.shape),
        ],
        out_specs=pl.BlockSpec((1, TQ, d), lambda bi, i: (bi, i, 0)),
        out_shape=jax.ShapeDtypeStruct((b, s, d), jnp.float32),
        scratch_shapes=[pltpu.VMEM((d, TQ), jnp.bfloat16),
                        pltpu.VMEM((N_HEADS + 1, B_PREV, TQ), jnp.float32),
                        pltpu.VMEM((N_HEADS, TQ, TQ), jnp.float32),
                        *[pltpu.VMEM((B_PREV + TQ, TQ), jnp.float32)] * N_SCORE_BUFS],
        compiler_params=pltpu.CompilerParams(
            dimension_semantics=("arbitrary", "arbitrary"), vmem_limit_bytes=VMEM_LIMIT),
        name="attn_b",
    )(h, qt4, ksh, ksh, vsht, vsht, gt4, vecp, veco, sinks, wout, fg)


def _slab_diff():
    u = jnp.arange(BIAS_PERIOD)
    return jnp.where(u < TQ, u, u - BIAS_PERIOD)


def _bias_vecs_a(rel_bias):
    back = (A_KBLOCKS - 1 - jnp.arange(A_KBLOCKS))[:, None] * TQ
    dist = back + _slab_diff()[None, :]
    idx = jnp.clip(dist, -A_REL_CLIP, A_REL_CLIP) + A_REL_CLIP
    vec = jnp.transpose(rel_bias[idx], (0, 2, 1)).astype(jnp.float32)
    return vec[:, :, None, :]


def _t5_bucket(rel):
    nb = T5_BUCKETS // 2
    max_exact = nb // 2
    ret = jnp.where(rel > 0, nb, 0)
    n = jnp.abs(rel)
    nf = jnp.maximum(n, 1).astype(jnp.float32)
    large = max_exact + (jnp.log(nf / max_exact) / math.log(T5_MAX_DIST / max_exact)
                         * (nb - max_exact)).astype(jnp.int32)
    large = jnp.minimum(large, nb - 1)
    return ret + jnp.where(n < max_exact, n, large)


def _bias_vecs_b(t5_table):
    rel_own = -_slab_diff()
    rel = jnp.stack([rel_own - B_PREV, rel_own])
    vec = jnp.transpose(t5_table[_t5_bucket(rel)], (0, 2, 1)).astype(jnp.float32)
    return vec[0:1, :, None, :], vec[1:2, :, None, :]


def kernel(x, a_norm, a_w_in, a_rel_bias, a_w_out, kv_norm, kv_w, t5_bias,
           b_norm, b_w_in, b_sinks, b_w_out, final_norm):
    assert a_norm.shape[0] == 1 and b_norm.shape[0] == 1, "one A layer then one B layer"
    bf = jnp.bfloat16
    scale = HEAD_DIM ** -0.5 * LOG2E

    def q_col_scale(n):
        return jnp.where(jnp.arange(n) < D_MODEL, scale, 1.0).astype(jnp.float32)

    wt_a = _weight_t(a_w_in[0], q_col_scale(a_w_in.shape[2]), a_norm[0], W_T_COLS)
    qt, k, vt, gt = _proj_a(x, wt_a)

    nwt = _weight_t(b_w_in[0], q_col_scale(b_w_in.shape[2]), b_norm[0], W_T_COLS,
                    head=kv_w, head_block=1, head_w=PAIR_W, head_gain=kv_norm)
    kvw_k = (kv_w[:, :PAIR_W] * kv_norm[:, None]).astype(bf)
    h, ksh, vsht, qbt, gbt = _attn_a(
        x, qt, k, vt, gt, _bias_vecs_a(a_rel_bias[0] * LOG2E), a_w_out[0].astype(bf), kvw_k, nwt)

    vecp, veco = _bias_vecs_b(t5_bias * LOG2E)
    sinks = jnp.broadcast_to((b_sinks[0].astype(jnp.float32) * LOG2E)[:, None, None], (N_HEADS, 1, TQ))
    return _attn_b(h, qbt, ksh, vsht, gbt, vecp, veco, sinks, b_w_out[0].astype(bf),
                   final_norm[None, :])
```

```python
import functools
import math

import jax
import jax.numpy as jnp
from jax import lax
from jax.experimental import pallas as pl
from jax.experimental.pallas import tpu as pltpu

D_MODEL = 1024
HEAD_DIM = 64
N_HEADS = D_MODEL // HEAD_DIM
N_PAIRS = N_HEADS // 2
PAIR_W = 2 * HEAD_DIM
CHUNK = 64
RMS_EPS = 1e-6
A_LEFT_CHUNKS = 8
A_REL_CLIP = 256
B_KV_HEADS = 2
B_GROUP = N_HEADS // B_KV_HEADS
B_LEFT_CHUNKS = 2
T5_BUCKETS = 32
T5_MAX_DIST = 128

TQ = 256
A_KBLOCKS = A_LEFT_CHUNKS * CHUNK // TQ + 1
B_PREV = B_LEFT_CHUNKS * CHUNK
TM_PROJ = 1024
FEAT_CHUNK = 512
W_T_COLS = 512
MASKED = -1e30
BIAS_PERIOD = 2 * TQ
SUBLANES = 8
LANES = 128
MAX_CHAINS = 2
BF16_ROWS = 16
LOG2E = math.log2(math.e)
N_SCORE_BUFS = 4
VMEM_LIMIT = 56 * 1024 * 1024

_NT = (((1,), (1,)), ((), ()))
_TN = (((0,), (0,)), ((), ()))


def _rms_scale(xf):
    return lax.rsqrt(jnp.mean(xf * xf, axis=-1, keepdims=True) + RMS_EPS)


def _silu(v):
    return v * jax.nn.sigmoid(v)


def _weight_t_kernel(*refs, has_head):
    if has_head:
        head_ref, hgain_ref, w_ref, cs_ref, gain_ref, o_ref = refs
        j = pl.program_id(0)

        @pl.when(j == 0)
        def _():
            pad = o_ref.shape[0] - head_ref.shape[1]
            o_ref[:pad, :] = jnp.zeros((pad, o_ref.shape[1]), jnp.bfloat16)
            o_ref[pad:, :] = (head_ref[...].T * hgain_ref[...]).astype(jnp.bfloat16)

        @pl.when(j > 0)
        def _():
            o_ref[...] = ((w_ref[...] * cs_ref[...]).T * gain_ref[...]).astype(jnp.bfloat16)
    else:
        w_ref, cs_ref, gain_ref, o_ref = refs
        o_ref[...] = ((w_ref[...] * cs_ref[...]).T * gain_ref[...]).astype(jnp.bfloat16)


def _weight_t(w, col_scale, row_gain, tc, head=None, head_block=0, head_w=None, head_gain=None):
    d, n = w.shape
    off = 0 if head is None else 1
    main_specs = [
        pl.BlockSpec((d, tc), lambda j: (0, jnp.maximum(j - off, 0))),
        pl.BlockSpec((1, tc), lambda j: (0, jnp.maximum(j - off, 0))),
        pl.BlockSpec((1, d), lambda j: (0, 0)),
    ]
    head_specs, head_args = [], []
    if head is not None:
        assert head.shape[0] == d and head.shape[1] % head_w == 0 and head_w <= tc
        head_specs = [pl.BlockSpec((d, head_w), lambda j: (0, head_block)), pl.BlockSpec((1, d), lambda j: (0, 0))]
        head_args = [head, head_gain[None, :]]
    return pl.pallas_call(
        functools.partial(_weight_t_kernel, has_head=head is not None),
        grid=(n // tc + off,),
        in_specs=head_specs + main_specs,
        out_specs=pl.BlockSpec((tc, d), lambda j: (j, 0)),
        out_shape=jax.ShapeDtypeStruct((n + off * tc, d), jnp.bfloat16),
        compiler_params=pltpu.CompilerParams(dimension_semantics=("arbitrary",), vmem_limit_bytes=VMEM_LIMIT),
        name="weight_t",
    )(*head_args, w, col_scale[None, :], row_gain[None, :])


def _proj_a_kernel(x_ref, wt_ref, qt_ref, k_ref, vt_ref, gt_ref):
    xf = x_ref[0]
    xn = (xf * _rms_scale(xf)).astype(jnp.bfloat16)
    k = lax.dot_general(xn, wt_ref[D_MODEL:2 * D_MODEL, :], _NT,
                        preferred_element_type=jnp.float32)
    for p in range(N_PAIRS):
        k_ref[0, p] = k[:, p * PAIR_W:(p + 1) * PAIR_W].astype(jnp.bfloat16)
    for out_ref, base, act in ((gt_ref, 3 * D_MODEL, _silu), (vt_ref, 2 * D_MODEL, None), (qt_ref, 0, None)):
        for c in range(0, D_MODEL, FEAT_CHUNK):
            w = wt_ref[base + c:base + c + FEAT_CHUNK, :]
            yt = lax.dot_general(w, xn, _NT, preferred_element_type=jnp.float32)
            out_ref[0, c:c + FEAT_CHUNK, :] = (yt if act is None else act(yt)).astype(jnp.bfloat16)


def _proj_a(x, wt):
    b, s, d = x.shape
    feat = jax.ShapeDtypeStruct((b, d, s), jnp.bfloat16)
    return pl.pallas_call(
        _proj_a_kernel,
        grid=(b, s // TM_PROJ),
        in_specs=[
            pl.BlockSpec((1, TM_PROJ, d), lambda bi, i: (bi, i, 0)),
            _const_spec(wt.shape),
        ],
        out_specs=[
            pl.BlockSpec((1, d, TM_PROJ), lambda bi, i: (bi, 0, i)),
            pl.BlockSpec((1, N_PAIRS, TM_PROJ, PAIR_W), lambda bi, i: (bi, 0, i, 0)),
            pl.BlockSpec((1, d, TM_PROJ), lambda bi, i: (bi, 0, i)),
            pl.BlockSpec((1, d, TM_PROJ), lambda bi, i: (bi, 0, i)),
        ],
        out_shape=[feat, jax.ShapeDtypeStruct((b, N_PAIRS, s, PAIR_W), jnp.bfloat16), feat, feat],
        compiler_params=pltpu.CompilerParams(
            dimension_semantics=("arbitrary", "arbitrary"), vmem_limit_bytes=VMEM_LIMIT),
        name="proj_a",
    )(x, wt)


def _chunk_iota(shape, axis):
    return lax.shift_right_logical(lax.broadcasted_iota(jnp.int32, shape, axis), CHUNK.bit_length() - 1)


def _fill_bias_slabs(vec_ref, bias_ref, n_keys, bands):
    n = len(bands)
    kch = _chunk_iota((n_keys, TQ), 0)
    qch = _chunk_iota((n_keys, TQ), 1)

    def head_body(h, carry):
        for j, band in enumerate(bands):
            v = jnp.broadcast_to(vec_ref[j, h], (n_keys, BIAS_PERIOD))
            t = pltpu.roll(v, 0, 1, stride=1, stride_axis=0)[:, :TQ]
            if band is not None:
                t = jnp.where(band(kch, qch), t, MASKED)
            bias_ref[h * n + j] = t
        return carry

    lax.fori_loop(0, N_HEADS, head_body, 0)


def _live_rows(band, n_keys):
    chunks_per_half = LANES // CHUNK
    out = []
    for half in range(TQ // LANES):
        qchs = range(half * chunks_per_half, (half + 1) * chunks_per_half)
        live = [kc for kc in range(n_keys // CHUNK) if band is None or any(band(kc, qc) for qc in qchs)]
        out.append((live[0] * CHUNK, (live[-1] + 1) * CHUNK) if live else (0, 0))
    return tuple(out)


def _scores_stage(k_tiles, qz, bias_fns, live, s_ref):
    maxes, row = [], 0
    for half in range(TQ // LANES):
        maxes.append([None] * MAX_CHAINS)
    for kt, bias_fn, live_j in zip(k_tiles, bias_fns, live):
        s = jnp.dot(kt, qz, preferred_element_type=jnp.float32)
        for half, (r0, r1) in enumerate(live_j):
            lanes = slice(half * LANES, (half + 1) * LANES)
            acc = maxes[half]
            for r in range(r0 // SUBLANES, r1 // SUBLANES):
                rows = slice(r * SUBLANES, (r + 1) * SUBLANES)
                grp = s[rows, lanes] + bias_fn(rows, lanes)
                s_ref[row + r * SUBLANES:row + (r + 1) * SUBLANES, lanes] = grp
                c = r % MAX_CHAINS
                acc[c] = grp if acc[c] is None else jnp.maximum(acc[c], grp)
        row += s.shape[0]
    cols = [jnp.max(functools.reduce(jnp.maximum, [a for a in acc if a is not None]), axis=0, keepdims=True)
            for acc in maxes]
    return jnp.concatenate(cols, axis=1)


def _pv_stage(s_ref, m, v_tiles, live, extra_logit=None):
    if extra_logit is not None:
        m = jnp.maximum(m, extra_logit)
    acc, row = None, 0
    for vt, live_j in zip(v_tiles, live):
        n = vt.shape[1]
        halves = []
        for half, (r0, r1) in enumerate(live_j):
            lanes = slice(half * LANES, (half + 1) * LANES)
            parts = [jnp.zeros((r0, LANES), jnp.bfloat16)] if r0 else []
            if r1 > r0:
                parts.append(jnp.exp2(s_ref[row + r0:row + r1, lanes] - m[:, lanes]).astype(jnp.bfloat16))
            if n > r1:
                parts.append(jnp.zeros((n - r1, LANES), jnp.bfloat16))
            halves.append(parts[0] if len(parts) == 1 else jnp.concatenate(parts, axis=0))
        p = jnp.concatenate(halves, axis=1)
        v_ones = jnp.concatenate([vt, jnp.ones((BF16_ROWS, n), vt.dtype)], axis=0)
        part = jnp.dot(v_ones, p, preferred_element_type=jnp.float32)
        acc = part if acc is None else acc + part
        row += n
    l = acc[HEAD_DIM:HEAD_DIM + 1, :]
    if extra_logit is not None:
        l = l + jnp.exp2(extra_logit - m)
    return acc[:HEAD_DIM, :] / l


def _head_pipeline(n_heads, scores_fn, pv_fn, s_refs):
    x_refs, y_refs = s_refs[:2], s_refs[2:]
    n_pairs = n_heads // 2

    def group(pair, src, m_src, dst, next_pair):
        m_dst = None
        if next_pair is not None:
            m_dst = scores_fn(next_pair, 0, dst[0]), scores_fn(next_pair, 1, dst[1])
        pv_fn(pair, 0, src[0], m_src[0])
        pv_fn(pair, 1, src[1], m_src[1])
        return m_dst

    m = scores_fn(0, 0, x_refs[0]), scores_fn(0, 1, x_refs[1])
    for pair in range(n_pairs):
        src, dst = (x_refs, y_refs) if pair % 2 == 0 else (y_refs, x_refs)
        m = group(pair, src, m, dst, pair + 1 if pair + 1 < n_pairs else None)


A_BANDS = (lambda kc, qc: kc >= qc,
           None,
           lambda kc, qc: kc <= qc)
A_LIVE = tuple(_live_rows(band, TQ) for band in A_BANDS)
B_BANDS = (lambda kc, qc: qc <= kc,
           lambda kc, qc: (qc >= kc) & (qc - kc <= B_LEFT_CHUNKS))
B_LIVE = (_live_rows(B_BANDS[0], B_PREV), _live_rows(B_BANDS[1], TQ))


def _out_and_residual(zt_ref, wout_ref, res):
    y = lax.dot_general(zt_ref[...], wout_ref[...], _TN, preferred_element_type=jnp.float32)
    return res + y


def _attn_a_kernel(x_ref, qt_ref, k0_ref, k1_ref, k2_ref, v0_ref, v1_ref, v2_ref, gt_ref,
                   bvec_ref, wout_ref, kvwk_ref, nwt_ref,
                   h_ref, ksh_ref, vsht_ref, qbt_ref, gbt_ref, zt_ref, bias_ref, *s_refs):
    i = pl.program_id(1)

    @pl.when((pl.program_id(0) == 0) & (i == 0))
    def _():
        _fill_bias_slabs(bvec_ref, bias_ref, TQ, A_BANDS)
        bias_ref[N_HEADS * A_KBLOCKS] = jnp.full((TQ, TQ), MASKED, jnp.float32)

    k_refs = (k0_ref, k1_ref, k2_ref)
    v_refs = (v0_ref, v1_ref, v2_ref)
    n_slabs = N_HEADS * A_KBLOCKS
    rows = lax.broadcasted_iota(jnp.int32, (PAIR_W, TQ), 0)

    def scores_fn(p, hh, s_ref):
        q2 = qt_ref[0, p]
        own_rows = rows < HEAD_DIM if hh == 0 else rows >= HEAD_DIM
        qz = jnp.where(own_rows, q2, jnp.zeros_like(q2))
        k_tiles, bias_fns = [], []
        for j in range(A_KBLOCKS):
            in_seq = i - (A_KBLOCKS - 1) + j >= 0
            slab = jnp.where(in_seq, (2 * p + hh) * A_KBLOCKS + j, n_slabs)
            k_tiles.append(k_refs[j][0, p])
            bias_fns.append(lambda rws, lanes, slab=slab: bias_ref[slab, rws, lanes])
        return _scores_stage(k_tiles, qz, bias_fns, A_LIVE, s_ref)

    def pv_fn(p, hh, s_ref, m):
        half = slice(hh * HEAD_DIM, (hh + 1) * HEAD_DIM)
        o = _pv_stage(s_ref, m, [v_ref[0, p, half, :] for v_ref in v_refs], A_LIVE)
        gate = gt_ref[0, p, half, :].astype(jnp.float32)
        row0 = p * PAIR_W + hh * HEAD_DIM
        zt_ref[row0:row0 + HEAD_DIM, :] = (o * gate).astype(jnp.bfloat16)

    _head_pipeline(N_HEADS, scores_fn, pv_fn, s_refs)

    h = _out_and_residual(zt_ref, wout_ref, x_ref[0])
    h_ref[0] = h
    hn = (h * _rms_scale(h)).astype(jnp.bfloat16)
    ksh_ref[0] = jnp.dot(hn, kvwk_ref[...], preferred_element_type=jnp.float32).astype(jnp.bfloat16)
    for out_ref, base, act in ((qbt_ref, W_T_COLS, None), (gbt_ref, W_T_COLS + D_MODEL, _silu)):
        for c in range(0, D_MODEL, FEAT_CHUNK):
            first = out_ref is qbt_ref and c == 0
            start = base + c - (PAIR_W if first else 0)
            yt = lax.dot_general(nwt_ref[start:base + c + FEAT_CHUNK, :], hn, _NT,
                                 preferred_element_type=jnp.float32)
            if first:
                vsht_ref[0] = yt[:PAIR_W].astype(jnp.bfloat16)
                yt = yt[PAIR_W:]
            out_ref[0, c:c + FEAT_CHUNK, :] = (yt if act is None else act(yt)).astype(jnp.bfloat16)


def _const_spec(shape):
    return pl.BlockSpec(shape, lambda bi, i: (0,) * len(shape), pipeline_mode=pl.Buffered(1))


def _attn_a(x, qt, k, vt, gt, bvec, wout, kvwk, nwt):
    b, s, d = x.shape
    qt4 = qt.reshape(b, N_PAIRS, PAIR_W, s)
    vt4 = vt.reshape(b, N_PAIRS, PAIR_W, s)
    gt4 = gt.reshape(b, N_PAIRS, PAIR_W, s)
    back = A_KBLOCKS - 1

    def kspec(j):
        return pl.BlockSpec((1, N_PAIRS, TQ, PAIR_W),
                            lambda bi, i: (bi, 0, jnp.maximum(i - back + j, 0), 0))

    def vspec(j):
        return pl.BlockSpec((1, N_PAIRS, PAIR_W, TQ),
                            lambda bi, i: (bi, 0, 0, jnp.maximum(i - back + j, 0)))

    feat_spec = pl.BlockSpec((1, N_PAIRS, PAIR_W, TQ), lambda bi, i: (bi, 0, 0, i))
    featout_spec = pl.BlockSpec((1, d, TQ), lambda bi, i: (bi, 0, i))
    feat = jax.ShapeDtypeStruct((b, d, s), jnp.bfloat16)
    return pl.pallas_call(
        _attn_a_kernel,
        grid=(b, s // TQ),
        in_specs=[
            pl.BlockSpec((1, TQ, d), lambda bi, i: (bi, i, 0)),
            feat_spec,
            kspec(0), kspec(1), kspec(2),
            vspec(0), vspec(1), vspec(2),
            feat_spec,
            _const_spec(bvec.shape),
            _const_spec(wout.shape),
            _const_spec(kvwk.shape),
            _const_spec(nwt.shape),
        ],
        out_specs=[
            pl.BlockSpec((1, TQ, d), lambda bi, i: (bi, i, 0)),
            pl.BlockSpec((1, TQ, PAIR_W), lambda bi, i: (bi, i, 0)),
            pl.BlockSpec((1, PAIR_W, TQ), lambda bi, i: (bi, 0, i)),
            featout_spec,
            featout_spec,
        ],
        out_shape=[
            jax.ShapeDtypeStruct((b, s, d), jnp.float32),
            jax.ShapeDtypeStruct((b, s, PAIR_W), jnp.bfloat16),
            jax.ShapeDtypeStruct((b, PAIR_W, s), jnp.bfloat16),
            feat, feat,
        ],
        scratch_shapes=[pltpu.VMEM((d, TQ), jnp.bfloat16),
                        pltpu.VMEM((N_HEADS * A_KBLOCKS + 1, TQ, TQ), jnp.float32),
                        *[pltpu.VMEM((A_KBLOCKS * TQ, TQ), jnp.float32)] * N_SCORE_BUFS],
        compiler_params=pltpu.CompilerParams(
            dimension_semantics=("arbitrary", "arbitrary"), vmem_limit_bytes=VMEM_LIMIT),
        name="attn_a",
    )(x, qt4, k, k, k, vt4, vt4, vt4, gt4, bvec, wout, kvwk, nwt)


def _attn_b_kernel(h_ref, qt_ref, kp_ref, ko_ref, vp_ref, vo_ref, gt_ref, vecp_ref, veco_ref,
                   sink_ref, wout_ref, fg_ref, out_ref, zt_ref, biasp_ref, biaso_ref, *s_refs):
    i = pl.program_id(1)

    @pl.when((pl.program_id(0) == 0) & (i == 0))
    def _():
        _fill_bias_slabs(vecp_ref, biasp_ref, B_PREV, B_BANDS[:1])
        biasp_ref[N_HEADS] = jnp.full((B_PREV, TQ), MASKED, jnp.float32)
        _fill_bias_slabs(veco_ref, biaso_ref, TQ, B_BANDS[1:])

    rows = lax.broadcasted_iota(jnp.int32, (PAIR_W, TQ), 0)
    pairs_per_kv = N_PAIRS // B_KV_HEADS

    def scores_fn(p, hh, s_ref):
        head = 2 * p + hh
        qh = qt_ref[0, p, hh * HEAD_DIM:(hh + 1) * HEAD_DIM, :]
        kv0 = HEAD_DIM * (p // pairs_per_kv)
        kv_rows = (rows >= kv0) & (rows < kv0 + HEAD_DIM)
        qz = jnp.where(kv_rows, jnp.concatenate([qh, qh], axis=0), jnp.zeros((PAIR_W, TQ), qh.dtype))
        slab = jnp.where(i >= 1, head, N_HEADS)
        k_tiles = [kp_ref[0, TQ - B_PREV:, :], ko_ref[0]]
        bias_fns = [lambda rws, lanes: biasp_ref[slab, rws, lanes],
                    lambda rws, lanes: biaso_ref[head, rws, lanes]]
        return _scores_stage(k_tiles, qz, bias_fns, B_LIVE, s_ref)

    def pv_fn(p, hh, s_ref, m):
        head = 2 * p + hh
        kv_rows = slice(HEAD_DIM * (p // pairs_per_kv), HEAD_DIM * (p // pairs_per_kv + 1))
        v_tiles = [vp_ref[0, kv_rows, TQ - B_PREV:],
                   vo_ref[0, kv_rows, :]]
        o = _pv_stage(s_ref, m, v_tiles, B_LIVE, extra_logit=sink_ref[head])
        gate = gt_ref[0, p, hh * HEAD_DIM:(hh + 1) * HEAD_DIM, :].astype(jnp.float32)
        row0 = p * PAIR_W + hh * HEAD_DIM
        zt_ref[row0:row0 + HEAD_DIM, :] = (o * gate).astype(jnp.bfloat16)

    _head_pipeline(N_HEADS, scores_fn, pv_fn, s_refs)

    h2 = _out_and_residual(zt_ref, wout_ref, h_ref[0])
    out_ref[0] = h2 * _rms_scale(h2) * fg_ref[...]


def _attn_b(h, qbt, ksh, vsht, gbt, vecp, veco, sinks, wout, fg):
    b, s, d = h.shape
    qt4 = qbt.reshape(b, N_PAIRS, PAIR_W, s)
    gt4 = gbt.reshape(b, N_PAIRS, PAIR_W, s)
    feat_spec = pl.BlockSpec((1, N_PAIRS, PAIR_W, TQ), lambda bi, i: (bi, 0, 0, i))
    return pl.pallas_call(
        _attn_b_kernel,
        grid=(b, s // TQ),
        in_specs=[
            pl.BlockSpec((1, TQ, d), lambda bi, i: (bi, i, 0)),
            feat_spec,
            pl.BlockSpec((1, TQ, PAIR_W), lambda bi, i: (bi, jnp.maximum(i - 1, 0), 0)),
            pl.BlockSpec((1, TQ, PAIR_W), lambda bi, i: (bi, i, 0)),
            pl.BlockSpec((1, PAIR_W, TQ), lambda bi, i: (bi, 0, jnp.maximum(i - 1, 0))),
            pl.BlockSpec((1, PAIR_W, TQ), lambda bi, i: (bi, 0, i)),
            feat_spec,
            _const_spec(vecp.shape),
            _const_spec(veco.shape),
            _const_spec(sinks.shape),
            _const_spec(wout.shape),
            _const_spec(fg.shape),
        ],
        out_specs=pl.BlockSpec((1, TQ, d), lambda bi, i: (bi, i, 0)),
        out_shape=jax.ShapeDtypeStruct((b, s, d), jnp.float32),
        scratch_shapes=[pltpu.VMEM((d, TQ), jnp.bfloat16),
                        pltpu.VMEM((N_HEADS + 1, B_PREV, TQ), jnp.float32),
                        pltpu.VMEM((N_HEADS, TQ, TQ), jnp.float32),
                        *[pltpu.VMEM((B_PREV + TQ, TQ), jnp.float32)] * N_SCORE_BUFS],
        compiler_params=pltpu.CompilerParams(
            dimension_semantics=("arbitrary", "arbitrary"), vmem_limit_bytes=VMEM_LIMIT),
        name="attn_b",
    )(h, qt4, ksh, ksh, vsht, vsht, gt4, vecp, veco, sinks, wout, fg)


def _slab_diff():
    u = jnp.arange(BIAS_PERIOD)
    return jnp.where(u < TQ, u, u - BIAS_PERIOD)


def _bias_vecs_a(rel_bias):
    back = (A_KBLOCKS - 1 - jnp.arange(A_KBLOCKS))[:, None] * TQ
    dist = back + _slab_diff()[None, :]
    idx = jnp.clip(dist, -A_REL_CLIP, A_REL_CLIP) + A_REL_CLIP
    vec = jnp.transpose(rel_bias[idx], (0, 2, 1)).astype(jnp.float32)
    return vec[:, :, None, :]


def _t5_bucket(rel):
    nb = T5_BUCKETS // 2
    max_exact = nb // 2
    ret = jnp.where(rel > 0, nb, 0)
    n = jnp.abs(rel)
    nf = jnp.maximum(n, 1).astype(jnp.float32)
    large = max_exact + (jnp.log(nf / max_exact) / math.log(T5_MAX_DIST / max_exact)
                         * (nb - max_exact)).astype(jnp.int32)
    large = jnp.minimum(large, nb - 1)
    return ret + jnp.where(n < max_exact, n, large)


def _bias_vecs_b(t5_table):
    rel_own = -_slab_diff()
    rel = jnp.stack([rel_own - B_PREV, rel_own])
    vec = jnp.transpose(t5_table[_t5_bucket(rel)], (0, 2, 1)).astype(jnp.float32)
    return vec[0:1, :, None, :], vec[1:2, :, None, :]


def kernel(x, a_norm, a_w_in, a_rel_bias, a_w_out, kv_norm, kv_w, t5_bias,
           b_norm, b_w_in, b_sinks, b_w_out, final_norm):
    assert a_norm.shape[0] == 1 and b_norm.shape[0] == 1, "one A layer then one B layer"
    bf = jnp.bfloat16
    scale = HEAD_DIM ** -0.5 * LOG2E

    def q_col_scale(n):
        return jnp.where(jnp.arange(n) < D_MODEL, scale, 1.0).astype(jnp.float32)

    wt_a = _weight_t(a_w_in[0], q_col_scale(a_w_in.shape[2]), a_norm[0], W_T_COLS)
    qt, k, vt, gt = _proj_a(x, wt_a)

    nwt = _weight_t(b_w_in[0], q_col_scale(b_w_in.shape[2]), b_norm[0], W_T_COLS,
                    head=kv_w, head_block=1, head_w=PAIR_W, head_gain=kv_norm)
    kvw_k = (kv_w[:, :PAIR_W] * kv_norm[:, None]).astype(bf)
    h, ksh, vsht, qbt, gbt = _attn_a(
        x, qt, k, vt, gt, _bias_vecs_a(a_rel_bias[0] * LOG2E), a_w_out[0].astype(bf), kvw_k, nwt)

    vecp, veco = _bias_vecs_b(t5_bias * LOG2E)
    sinks = jnp.broadcast_to((b_sinks[0].astype(jnp.float32) * LOG2E)[:, None, None], (N_HEADS, 1, TQ))
    return _attn_b(h, qbt, ksh, vsht, gbt, vecp, veco, sinks, b_w_out[0].astype(bf),
                   final_norm[None, :])
```

```python
import functools
import math

import jax
import jax.numpy as jnp
from jax import lax
from jax.experimental import pallas as pl
from jax.experimental.pallas import tpu as pltpu

D_MODEL = 1024
HEAD_DIM = 64
N_HEADS = D_MODEL // HEAD_DIM
N_PAIRS = N_HEADS // 2
PAIR_W = 2 * HEAD_DIM
CHUNK = 64
RMS_EPS = 1e-6
A_LEFT_CHUNKS = 8
A_REL_CLIP = 256
B_KV_HEADS = 2
B_GROUP = N_HEADS // B_KV_HEADS
B_LEFT_CHUNKS = 2
T5_BUCKETS = 32
T5_MAX_DIST = 128

TQ = 256
A_KBLOCKS = A_LEFT_CHUNKS * CHUNK // TQ + 1
B_PREV = B_LEFT_CHUNKS * CHUNK
TM_PROJ = 1024
FEAT_CHUNK = 512
W_T_COLS = 512
MASKED = -1e30
BIAS_PERIOD = 2 * TQ
SUBLANES = 8
LANES = 128
MAX_CHAINS = 2
BF16_ROWS = 16
LOG2E = math.log2(math.e)
N_SCORE_BUFS = 4
N_SCORE_BUFS_B = 8
VMEM_LIMIT = 56 * 1024 * 1024

_NT = (((1,), (1,)), ((), ()))
_TN = (((0,), (0,)), ((), ()))


def _rms_scale(xf):
    return lax.rsqrt(jnp.mean(xf * xf, axis=-1, keepdims=True) + RMS_EPS)


def _silu(v):
    return v * jax.nn.sigmoid(v)


def _weight_t_kernel(*refs, has_head):
    if has_head:
        head_ref, hgain_ref, w_ref, cs_ref, gain_ref, o_ref = refs
        j = pl.program_id(0)

        @pl.when(j == 0)
        def _():
            pad = o_ref.shape[0] - head_ref.shape[1]
            o_ref[:pad, :] = jnp.zeros((pad, o_ref.shape[1]), jnp.bfloat16)
            o_ref[pad:, :] = (head_ref[...].T * hgain_ref[...]).astype(jnp.bfloat16)

        @pl.when(j > 0)
        def _():
            o_ref[...] = ((w_ref[...] * cs_ref[...]).T * gain_ref[...]).astype(jnp.bfloat16)
    else:
        w_ref, cs_ref, gain_ref, o_ref = refs
        o_ref[...] = ((w_ref[...] * cs_ref[...]).T * gain_ref[...]).astype(jnp.bfloat16)


def _weight_t(w, col_scale, row_gain, tc, head=None, head_block=0, head_w=None, head_gain=None):
    d, n = w.shape
    off = 0 if head is None else 1
    main_specs = [
        pl.BlockSpec((d, tc), lambda j: (0, jnp.maximum(j - off, 0))),
        pl.BlockSpec((1, tc), lambda j: (0, jnp.maximum(j - off, 0))),
        pl.BlockSpec((1, d), lambda j: (0, 0)),
    ]
    head_specs, head_args = [], []
    if head is not None:
        assert head.shape[0] == d and head.shape[1] % head_w == 0 and head_w <= tc
        head_specs = [pl.BlockSpec((d, head_w), lambda j: (0, head_block)), pl.BlockSpec((1, d), lambda j: (0, 0))]
        head_args = [head, head_gain[None, :]]
    return pl.pallas_call(
        functools.partial(_weight_t_kernel, has_head=head is not None),
        grid=(n // tc + off,),
        in_specs=head_specs + main_specs,
        out_specs=pl.BlockSpec((tc, d), lambda j: (j, 0)),
        out_shape=jax.ShapeDtypeStruct((n + off * tc, d), jnp.bfloat16),
        compiler_params=pltpu.CompilerParams(dimension_semantics=("arbitrary",), vmem_limit_bytes=VMEM_LIMIT),
        name="weight_t",
    )(*head_args, w, col_scale[None, :], row_gain[None, :])


def _proj_a_kernel(x_ref, wt_ref, qt_ref, k_ref, vt_ref, gt_ref):
    xf = x_ref[0]
    xn = (xf * _rms_scale(xf)).astype(jnp.bfloat16)
    k = lax.dot_general(xn, wt_ref[D_MODEL:2 * D_MODEL, :], _NT,
                        preferred_element_type=jnp.float32)
    for p in range(N_PAIRS):
        k_ref[0, p] = k[:, p * PAIR_W:(p + 1) * PAIR_W].astype(jnp.bfloat16)
    for out_ref, base, act in ((gt_ref, 3 * D_MODEL, _silu), (vt_ref, 2 * D_MODEL, None), (qt_ref, 0, None)):
        for c in range(0, D_MODEL, FEAT_CHUNK):
            w = wt_ref[base + c:base + c + FEAT_CHUNK, :]
            yt = lax.dot_general(w, xn, _NT, preferred_element_type=jnp.float32)
            out_ref[0, c:c + FEAT_CHUNK, :] = (yt if act is None else act(yt)).astype(jnp.bfloat16)


def _proj_a(x, wt):
    b, s, d = x.shape
    feat = jax.ShapeDtypeStruct((b, d, s), jnp.bfloat16)
    return pl.pallas_call(
        _proj_a_kernel,
        grid=(b, s // TM_PROJ),
        in_specs=[
            pl.BlockSpec((1, TM_PROJ, d), lambda bi, i: (bi, i, 0)),
            _const_spec(wt.shape),
        ],
        out_specs=[
            pl.BlockSpec((1, d, TM_PROJ), lambda bi, i: (bi, 0, i)),
            pl.BlockSpec((1, N_PAIRS, TM_PROJ, PAIR_W), lambda bi, i: (bi, 0, i, 0)),
            pl.BlockSpec((1, d, TM_PROJ), lambda bi, i: (bi, 0, i)),
            pl.BlockSpec((1, d, TM_PROJ), lambda bi, i: (bi, 0, i)),
        ],
        out_shape=[feat, jax.ShapeDtypeStruct((b, N_PAIRS, s, PAIR_W), jnp.bfloat16), feat, feat],
        compiler_params=pltpu.CompilerParams(
            dimension_semantics=("arbitrary", "arbitrary"), vmem_limit_bytes=VMEM_LIMIT),
        name="proj_a",
    )(x, wt)


def _chunk_iota(shape, axis):
    return lax.shift_right_logical(lax.broadcasted_iota(jnp.int32, shape, axis), CHUNK.bit_length() - 1)


def _toeplitz(vec_row, n_keys):
    return pltpu.roll(jnp.broadcast_to(vec_row, (n_keys, BIAS_PERIOD)), 0, 1, stride=1, stride_axis=0)


def _fill_bias_slabs(vec_ref, bias_ref, n_keys, bands):
    n = len(bands)
    kch = _chunk_iota((n_keys, TQ), 0)
    qch = _chunk_iota((n_keys, TQ), 1)

    def head_body(h, carry):
        for j, band in enumerate(bands):
            t = _toeplitz(vec_ref[j, h], n_keys)[:, :TQ]
            if band is not None:
                t = jnp.where(band(kch, qch), t, MASKED)
            bias_ref[h * n + j] = t
        return carry

    lax.fori_loop(0, N_HEADS, head_body, 0)


def _live_rows(band, n_keys):
    chunks_per_half = LANES // CHUNK
    out = []
    for half in range(TQ // LANES):
        qchs = range(half * chunks_per_half, (half + 1) * chunks_per_half)
        live = [kc for kc in range(n_keys // CHUNK) if band is None or any(band(kc, qc) for qc in qchs)]
        out.append((live[0] * CHUNK, (live[-1] + 1) * CHUNK) if live else (0, 0))
    return tuple(out)


def _scores_stage(k_tiles, qz, bias_fns, live, s_ref):
    maxes, row = [], 0
    for half in range(TQ // LANES):
        maxes.append([None] * MAX_CHAINS)
    for kt, bias_fn, live_j in zip(k_tiles, bias_fns, live):
        s = jnp.dot(kt, qz, preferred_element_type=jnp.float32)
        for half, (r0, r1) in enumerate(live_j):
            lanes = slice(half * LANES, (half + 1) * LANES)
            acc = maxes[half]
            for r in range(r0 // SUBLANES, r1 // SUBLANES):
                rows = slice(r * SUBLANES, (r + 1) * SUBLANES)
                grp = s[rows, lanes] + bias_fn(rows, lanes)
                s_ref[row + r * SUBLANES:row + (r + 1) * SUBLANES, lanes] = grp
                c = r % MAX_CHAINS
                acc[c] = grp if acc[c] is None else jnp.maximum(acc[c], grp)
        row += s.shape[0]
    cols = [jnp.max(functools.reduce(jnp.maximum, [a for a in acc if a is not None]), axis=0, keepdims=True)
            for acc in maxes]
    return jnp.concatenate(cols, axis=1)


def _pv_stage(s_ref, m, v_tiles, live, extra_logit=None):
    if extra_logit is not None:
        m = jnp.maximum(m, extra_logit)
    acc, row = None, 0
    for vt, live_j in zip(v_tiles, live):
        n = vt.shape[1]
        halves = []
        for half, (r0, r1) in enumerate(live_j):
            lanes = slice(half * LANES, (half + 1) * LANES)
            parts = [jnp.zeros((r0, LANES), jnp.bfloat16)] if r0 else []
            if r1 > r0:
                parts.append(jnp.exp2(s_ref[row + r0:row + r1, lanes] - m[:, lanes]).astype(jnp.bfloat16))
            if n > r1:
                parts.append(jnp.zeros((n - r1, LANES), jnp.bfloat16))
            halves.append(parts[0] if len(parts) == 1 else jnp.concatenate(parts, axis=0))
        p = jnp.concatenate(halves, axis=1)
        v_ones = jnp.concatenate([vt, jnp.ones((BF16_ROWS, n), vt.dtype)], axis=0)
        part = jnp.dot(v_ones, p, preferred_element_type=jnp.float32)
        acc = part if acc is None else acc + part
        row += n
    l = acc[HEAD_DIM:HEAD_DIM + 1, :]
    if extra_logit is not None:
        l = l + jnp.exp2(extra_logit - m)
    return acc[:HEAD_DIM, :] / l


def _head_pipeline(n_tiles, scores_fn, pv_fn, s_refs):
    per_group = len(s_refs) // 2
    x_refs, y_refs = s_refs[:per_group], s_refs[per_group:]
    n_groups = n_tiles // per_group

    def score_group(g, dst):
        return [scores_fn(t // 2, t % 2, ref) for t, ref in zip(range(g * per_group, (g + 1) * per_group), dst)]

    m = score_group(0, x_refs)
    for g in range(n_groups):
        src, dst = (x_refs, y_refs) if g % 2 == 0 else (y_refs, x_refs)
        m_next = score_group(g + 1, dst) if g + 1 < n_groups else None
        for t, ref, m_t in zip(range(g * per_group, (g + 1) * per_group), src, m):
            pv_fn(t // 2, t % 2, ref, m_t)
        m = m_next


A_BANDS = (lambda kc, qc: kc >= qc,
           None,
           lambda kc, qc: kc <= qc)
A_LIVE = tuple(_live_rows(band, TQ) for band in A_BANDS)
B_WIN = B_PREV + LANES
B_BAND = lambda kc, qc: (kc >= qc) & (kc <= qc + B_LEFT_CHUNKS)
B_LIVE = (((0, B_WIN),) * (TQ // LANES),)


def _out_and_residual(zt_ref, wout_ref, res):
    y = lax.dot_general(zt_ref[...], wout_ref[...], _TN, preferred_element_type=jnp.float32)
    return res + y


def _attn_a_kernel(x_ref, qt_ref, k0_ref, k1_ref, k2_ref, v0_ref, v1_ref, v2_ref, gt_ref,
                   bvec_ref, wout_ref, kvwk_ref, nwt_ref,
                   h_ref, ksh_ref, vsht_ref, qbt_ref, gbt_ref, zt_ref, bias_ref, *s_refs):
    i = pl.program_id(1)

    @pl.when((pl.program_id(0) == 0) & (i == 0))
    def _():
        _fill_bias_slabs(bvec_ref, bias_ref, TQ, A_BANDS)
        bias_ref[N_HEADS * A_KBLOCKS] = jnp.full((TQ, TQ), MASKED, jnp.float32)

    k_refs = (k0_ref, k1_ref, k2_ref)
    v_refs = (v0_ref, v1_ref, v2_ref)
    n_slabs = N_HEADS * A_KBLOCKS
    rows = lax.broadcasted_iota(jnp.int32, (PAIR_W, TQ), 0)

    def scores_fn(p, hh, s_ref):
        q2 = qt_ref[0, p]
        own_rows = rows < HEAD_DIM if hh == 0 else rows >= HEAD_DIM
        qz = jnp.where(own_rows, q2, jnp.zeros_like(q2))
        k_tiles, bias_fns = [], []
        for j in range(A_KBLOCKS):
            in_seq = i - (A_KBLOCKS - 1) + j >= 0
            slab = jnp.where(in_seq, (2 * p + hh) * A_KBLOCKS + j, n_slabs)
            k_tiles.append(k_refs[j][0, p])
            bias_fns.append(lambda rws, lanes, slab=slab: bias_ref[slab, rws, lanes])
        return _scores_stage(k_tiles, qz, bias_fns, A_LIVE, s_ref)

    def pv_fn(p, hh, s_ref, m):
        half = slice(hh * HEAD_DIM, (hh + 1) * HEAD_DIM)
        o = _pv_stage(s_ref, m, [v_ref[0, p, half, :] for v_ref in v_refs], A_LIVE)
        gate = gt_ref[0, p, half, :].astype(jnp.float32)
        row0 = p * PAIR_W + hh * HEAD_DIM
        zt_ref[row0:row0 + HEAD_DIM, :] = (o * gate).astype(jnp.bfloat16)

    _head_pipeline(N_HEADS, scores_fn, pv_fn, s_refs)

    h = _out_and_residual(zt_ref, wout_ref, x_ref[0])
    h_ref[0] = h
    hn = (h * _rms_scale(h)).astype(jnp.bfloat16)
    ksh_ref[0] = jnp.dot(hn, kvwk_ref[...], preferred_element_type=jnp.float32).astype(jnp.bfloat16)
    for out_ref, base, act in ((qbt_ref, W_T_COLS, None), (gbt_ref, W_T_COLS + D_MODEL, _silu)):
        for c in range(0, D_MODEL, FEAT_CHUNK):
            first = out_ref is qbt_ref and c == 0
            start = base + c - (PAIR_W if first else 0)
            yt = lax.dot_general(nwt_ref[start:base + c + FEAT_CHUNK, :], hn, _NT,
                                 preferred_element_type=jnp.float32)
            if first:
                vsht_ref[0] = yt[:PAIR_W].astype(jnp.bfloat16)
                yt = yt[PAIR_W:]
            out_ref[0, c:c + FEAT_CHUNK, :] = (yt if act is None else act(yt)).astype(jnp.bfloat16)


def _const_spec(shape):
    return pl.BlockSpec(shape, lambda bi, i: (0,) * len(shape), pipeline_mode=pl.Buffered(1))


def _attn_a(x, qt, k, vt, gt, bvec, wout, kvwk, nwt):
    b, s, d = x.shape
    qt4 = qt.reshape(b, N_PAIRS, PAIR_W, s)
    vt4 = vt.reshape(b, N_PAIRS, PAIR_W, s)
    gt4 = gt.reshape(b, N_PAIRS, PAIR_W, s)
    back = A_KBLOCKS - 1

    def kspec(j):
        return pl.BlockSpec((1, N_PAIRS, TQ, PAIR_W),
                            lambda bi, i: (bi, 0, jnp.maximum(i - back + j, 0), 0))

    def vspec(j):
        return pl.BlockSpec((1, N_PAIRS, PAIR_W, TQ),
                            lambda bi, i: (bi, 0, 0, jnp.maximum(i - back + j, 0)))

    feat_spec = pl.BlockSpec((1, N_PAIRS, PAIR_W, TQ), lambda bi, i: (bi, 0, 0, i))
    featout_spec = pl.BlockSpec((1, d, TQ), lambda bi, i: (bi, 0, i))
    feat = jax.ShapeDtypeStruct((b, d, s), jnp.bfloat16)
    return pl.pallas_call(
        _attn_a_kernel,
        grid=(b, s // TQ),
        in_specs=[
            pl.BlockSpec((1, TQ, d), lambda bi, i: (bi, i, 0)),
            feat_spec,
            kspec(0), kspec(1), kspec(2),
            vspec(0), vspec(1), vspec(2),
            feat_spec,
            _const_spec(bvec.shape),
            _const_spec(wout.shape),
            _const_spec(kvwk.shape),
            _const_spec(nwt.shape),
        ],
        out_specs=[
            pl.BlockSpec((1, TQ, d), lambda bi, i: (bi, i, 0)),
            pl.BlockSpec((1, TQ, PAIR_W), lambda bi, i: (bi, i, 0)),
            pl.BlockSpec((1, PAIR_W, TQ), lambda bi, i: (bi, 0, i)),
            featout_spec,
            featout_spec,
        ],
        out_shape=[
            jax.ShapeDtypeStruct((b, s, d), jnp.float32),
            jax.ShapeDtypeStruct((b, s, PAIR_W), jnp.bfloat16),
            jax.ShapeDtypeStruct((b, PAIR_W, s), jnp.bfloat16),
            feat, feat,
        ],
        scratch_shapes=[pltpu.VMEM((d, TQ), jnp.bfloat16),
                        pltpu.VMEM((N_HEADS * A_KBLOCKS + 1, TQ, TQ), jnp.float32),
                        *[pltpu.VMEM((A_KBLOCKS * TQ, TQ), jnp.float32)] * N_SCORE_BUFS],
        compiler_params=pltpu.CompilerParams(
            dimension_semantics=("arbitrary", "arbitrary"), vmem_limit_bytes=VMEM_LIMIT),
        name="attn_a",
    )(x, qt4, k, k, k, vt4, vt4, vt4, gt4, bvec, wout, kvwk, nwt)


def _attn_b_kernel(h_ref, qt_ref, kp_ref, ko_ref, vp_ref, vo_ref, gt_ref, vec_ref,
                   sink_ref, wout_ref, fg_ref, out_ref, zt_ref, bias_ref, *s_refs):
    i = pl.program_id(1)

    @pl.when((pl.program_id(0) == 0) & (i == 0))
    def _():
        in_band = B_BAND(_chunk_iota((B_WIN, LANES), 0), _chunk_iota((B_WIN, LANES), 1))
        before_block = lax.broadcasted_iota(jnp.int32, (B_WIN, LANES), 0) < B_PREV

        def head_body(h, carry):
            t = jnp.where(in_band, _toeplitz(vec_ref[0, h], B_WIN)[:, :LANES], MASKED)
            bias_ref[2 * h] = t
            bias_ref[2 * h + 1] = jnp.where(before_block, MASKED, t)
            return carry

        lax.fori_loop(0, N_HEADS, head_body, 0)

    rows = lax.broadcasted_iota(jnp.int32, (PAIR_W, TQ), 0)
    pairs_per_kv = N_PAIRS // B_KV_HEADS

    def scores_fn(p, u, s_ref):
        qlanes = slice(u * LANES, (u + 1) * LANES)
        q2 = jnp.concatenate([qt_ref[0, p, :HEAD_DIM, qlanes], qt_ref[0, p, HEAD_DIM:, qlanes]], axis=1)
        kv0 = HEAD_DIM * (p // pairs_per_kv)
        kv_rows = (rows >= kv0) & (rows < kv0 + HEAD_DIM)
        qz = jnp.where(kv_rows, jnp.concatenate([q2, q2], axis=0), jnp.zeros((PAIR_W, TQ), q2.dtype))
        if u == 0:
            k_tile = jnp.concatenate([kp_ref[0, TQ - B_PREV:, :], ko_ref[0, :LANES, :]], axis=0)
            variant = jnp.where(i == 0, 1, 0)
        else:
            k_tile, variant = ko_ref[0], 0

        def bias_fn(rws, lanes):
            head = 2 * p + lanes.start // LANES
            return bias_ref[2 * head + variant, rws, :]

        return _scores_stage([k_tile], qz, [bias_fn], B_LIVE, s_ref)

    def pv_fn(p, u, s_ref, m):
        qlanes = slice(u * LANES, (u + 1) * LANES)
        kv_rows = slice(HEAD_DIM * (p // pairs_per_kv), HEAD_DIM * (p // pairs_per_kv + 1))
        if u == 0:
            v_tile = jnp.concatenate([vp_ref[0, kv_rows, TQ - B_PREV:], vo_ref[0, kv_rows, :LANES]], axis=1)
        else:
            v_tile = vo_ref[0, kv_rows, :]
        o = _pv_stage(s_ref, m, [v_tile], B_LIVE, extra_logit=sink_ref[p])
        for hh in range(2):
            hrows = slice(hh * HEAD_DIM, (hh + 1) * HEAD_DIM)
            gate = gt_ref[0, p, hrows, qlanes].astype(jnp.float32)
            row0 = p * PAIR_W + hh * HEAD_DIM
            zt_ref[row0:row0 + HEAD_DIM, qlanes] = (o[:, hh * LANES:(hh + 1) * LANES] * gate).astype(jnp.bfloat16)

    _head_pipeline(N_HEADS, scores_fn, pv_fn, s_refs)

    h2 = _out_and_residual(zt_ref, wout_ref, h_ref[0])
    out_ref[0] = h2 * _rms_scale(h2) * fg_ref[...]


def _attn_b(h, qbt, ksh, vsht, gbt, vec, sinks, wout, fg):
    b, s, d = h.shape
    qt4 = qbt.reshape(b, N_PAIRS, PAIR_W, s)
    gt4 = gbt.reshape(b, N_PAIRS, PAIR_W, s)
    feat_spec = pl.BlockSpec((1, N_PAIRS, PAIR_W, TQ), lambda bi, i: (bi, 0, 0, i))
    return pl.pallas_call(
        _attn_b_kernel,
        grid=(b, s // TQ),
        in_specs=[
            pl.BlockSpec((1, TQ, d), lambda bi, i: (bi, i, 0)),
            feat_spec,
            pl.BlockSpec((1, TQ, PAIR_W), lambda bi, i: (bi, jnp.maximum(i - 1, 0), 0)),
            pl.BlockSpec((1, TQ, PAIR_W), lambda bi, i: (bi, i, 0)),
            pl.BlockSpec((1, PAIR_W, TQ), lambda bi, i: (bi, 0, jnp.maximum(i - 1, 0))),
            pl.BlockSpec((1, PAIR_W, TQ), lambda bi, i: (bi, 0, i)),
            feat_spec,
            _const_spec(vec.shape),
            _const_spec(sinks.shape),
            _const_spec(wout.shape),
            _const_spec(fg.shape),
        ],
        out_specs=pl.BlockSpec((1, TQ, d), lambda bi, i: (bi, i, 0)),
        out_shape=jax.ShapeDtypeStruct((b, s, d), jnp.float32),
        scratch_shapes=[pltpu.VMEM((d, TQ), jnp.bfloat16),
                        pltpu.VMEM((2 * N_HEADS, B_WIN, LANES), jnp.float32),
                        *[pltpu.VMEM((B_WIN, TQ), jnp.float32)] * N_SCORE_BUFS_B],
        compiler_params=pltpu.CompilerParams(
            dimension_semantics=("arbitrary", "arbitrary"), vmem_limit_bytes=VMEM_LIMIT),
        name="attn_b",
    )(h, qt4, ksh, ksh, vsht, vsht, gt4, vec, sinks, wout, fg)


def _slab_diff():
    u = jnp.arange(BIAS_PERIOD)
    return jnp.where(u < TQ, u, u - BIAS_PERIOD)


def _bias_vecs_a(rel_bias):
    back = (A_KBLOCKS - 1 - jnp.arange(A_KBLOCKS))[:, None] * TQ
    dist = back + _slab_diff()[None, :]
    idx = jnp.clip(dist, -A_REL_CLIP, A_REL_CLIP) + A_REL_CLIP
    vec = jnp.transpose(rel_bias[idx], (0, 2, 1)).astype(jnp.float32)
    return vec[:, :, None, :]


def _t5_bucket(rel):
    nb = T5_BUCKETS // 2
    max_exact = nb // 2
    ret = jnp.where(rel > 0, nb, 0)
    n = jnp.abs(rel)
    nf = jnp.maximum(n, 1).astype(jnp.float32)
    large = max_exact + (jnp.log(nf / max_exact) / math.log(T5_MAX_DIST / max_exact)
                         * (nb - max_exact)).astype(jnp.int32)
    large = jnp.minimum(large, nb - 1)
    return ret + jnp.where(n < max_exact, n, large)


def _bias_vec_b(t5_table):
    rel = -_slab_diff() - B_PREV
    vec = jnp.transpose(t5_table[_t5_bucket(rel)], (1, 0)).astype(jnp.float32)
    return vec[None, :, None, :]


def kernel(x, a_norm, a_w_in, a_rel_bias, a_w_out, kv_norm, kv_w, t5_bias,
           b_norm, b_w_in, b_sinks, b_w_out, final_norm):
    assert a_norm.shape[0] == 1 and b_norm.shape[0] == 1, "one A layer then one B layer"
    bf = jnp.bfloat16
    scale = HEAD_DIM ** -0.5 * LOG2E

    def q_col_scale(n):
        return jnp.where(jnp.arange(n) < D_MODEL, scale, 1.0).astype(jnp.float32)

    wt_a = _weight_t(a_w_in[0], q_col_scale(a_w_in.shape[2]), a_norm[0], W_T_COLS)
    qt, k, vt, gt = _proj_a(x, wt_a)

    nwt = _weight_t(b_w_in[0], q_col_scale(b_w_in.shape[2]), b_norm[0], W_T_COLS,
                    head=kv_w, head_block=1, head_w=PAIR_W, head_gain=kv_norm)
    kvw_k = (kv_w[:, :PAIR_W] * kv_norm[:, None]).astype(bf)
    h, ksh, vsht, qbt, gbt = _attn_a(
        x, qt, k, vt, gt, _bias_vecs_a(a_rel_bias[0] * LOG2E), a_w_out[0].astype(bf), kvw_k, nwt)

    sinks = jnp.repeat((b_sinks[0].astype(jnp.float32) * LOG2E).reshape(N_PAIRS, 1, 2), LANES, axis=2)
    return _attn_b(h, qbt, ksh, vsht, gbt, _bias_vec_b(t5_bias * LOG2E), sinks, b_w_out[0].astype(bf),
                   final_norm[None, :])
```

```python
import functools
import math

import jax
import jax.numpy as jnp
from jax import lax
from jax.experimental import pallas as pl
from jax.experimental.pallas import tpu as pltpu

D_MODEL = 1024
HEAD_DIM = 64
N_HEADS = D_MODEL // HEAD_DIM
N_PAIRS = N_HEADS // 2
PAIR_W = 2 * HEAD_DIM
CHUNK = 64
RMS_EPS = 1e-6
A_LEFT_CHUNKS = 8
A_REL_CLIP = 256
B_KV_HEADS = 2
B_GROUP = N_HEADS // B_KV_HEADS
B_LEFT_CHUNKS = 2
T5_BUCKETS = 32
T5_MAX_DIST = 128

TQ = 256
A_KBLOCKS = A_LEFT_CHUNKS * CHUNK // TQ + 1
B_PREV = B_LEFT_CHUNKS * CHUNK
TM_PROJ = 1024
FEAT_CHUNK = 512
W_T_COLS_A = 1024
W_T_COLS_B = 512
MASKED = -1e30
BIAS_PERIOD = 2 * TQ
SUBLANES = 8
LANES = 128
MAX_CHAINS = 2
BF16_ROWS = 16
LOG2E = math.log2(math.e)
N_SCORE_BUFS = 4
N_SCORE_BUFS_B = 8
VMEM_LIMIT = 56 * 1024 * 1024

_NT = (((1,), (1,)), ((), ()))
_TN = (((0,), (0,)), ((), ()))


def _rms_scale(xf):
    return lax.rsqrt(jnp.mean(xf * xf, axis=-1, keepdims=True) + RMS_EPS)


def _silu(v):
    return v * jax.nn.sigmoid(v)


def _weight_t_kernel(*refs, has_head):
    if has_head:
        head_ref, hgain_ref, w_ref, cs_ref, gain_ref, o_ref = refs
        j = pl.program_id(0)

        @pl.when(j == 0)
        def _():
            pad = o_ref.shape[0] - head_ref.shape[1]
            o_ref[:pad, :] = jnp.zeros((pad, o_ref.shape[1]), jnp.bfloat16)
            o_ref[pad:, :] = (head_ref[...].T * hgain_ref[...]).astype(jnp.bfloat16)

        @pl.when(j > 0)
        def _():
            o_ref[...] = ((w_ref[...] * cs_ref[...]).T * gain_ref[...]).astype(jnp.bfloat16)
    else:
        w_ref, cs_ref, gain_ref, o_ref = refs
        o_ref[...] = ((w_ref[...] * cs_ref[...]).T * gain_ref[...]).astype(jnp.bfloat16)


def _weight_t(w, col_scale, row_gain, tc, head=None, head_block=0, head_w=None, head_gain=None):
    d, n = w.shape
    off = 0 if head is None else 1
    main_specs = [
        pl.BlockSpec((d, tc), lambda j: (0, jnp.maximum(j - off, 0))),
        pl.BlockSpec((1, tc), lambda j: (0, jnp.maximum(j - off, 0))),
        pl.BlockSpec((1, d), lambda j: (0, 0)),
    ]
    head_specs, head_args = [], []
    if head is not None:
        assert head.shape[0] == d and head.shape[1] % head_w == 0 and head_w <= tc
        head_specs = [pl.BlockSpec((d, head_w), lambda j: (0, head_block)), pl.BlockSpec((1, d), lambda j: (0, 0))]
        head_args = [head, head_gain[None, :]]
    return pl.pallas_call(
        functools.partial(_weight_t_kernel, has_head=head is not None),
        grid=(n // tc + off,),
        in_specs=head_specs + main_specs,
        out_specs=pl.BlockSpec((tc, d), lambda j: (j, 0)),
        out_shape=jax.ShapeDtypeStruct((n + off * tc, d), jnp.bfloat16),
        compiler_params=pltpu.CompilerParams(dimension_semantics=("arbitrary",), vmem_limit_bytes=VMEM_LIMIT),
        name="weight_t",
    )(*head_args, w, col_scale[None, :], row_gain[None, :])


def _proj_a_kernel(x_ref, wt_ref, qt_ref, k_ref, vt_ref, gt_ref):
    xf = x_ref[0]
    xn = (xf * _rms_scale(xf)).astype(jnp.bfloat16)
    k = lax.dot_general(xn, wt_ref[D_MODEL:2 * D_MODEL, :], _NT,
                        preferred_element_type=jnp.float32)
    for p in range(N_PAIRS):
        k_ref[0, p] = k[:, p * PAIR_W:(p + 1) * PAIR_W].astype(jnp.bfloat16)
    for out_ref, base, act in ((gt_ref, 3 * D_MODEL, _silu), (vt_ref, 2 * D_MODEL, None), (qt_ref, 0, None)):
        for c in range(0, D_MODEL, FEAT_CHUNK):
            w = wt_ref[base + c:base + c + FEAT_CHUNK, :]
            yt = lax.dot_general(w, xn, _NT, preferred_element_type=jnp.float32)
            out_ref[0, c:c + FEAT_CHUNK, :] = (yt if act is None else act(yt)).astype(jnp.bfloat16)


def _proj_a(x, wt):
    b, s, d = x.shape
    feat = jax.ShapeDtypeStruct((b, d, s), jnp.bfloat16)
    return pl.pallas_call(
        _proj_a_kernel,
        grid=(b, s // TM_PROJ),
        in_specs=[
            pl.BlockSpec((1, TM_PROJ, d), lambda bi, i: (bi, i, 0)),
            _const_spec(wt.shape),
        ],
        out_specs=[
            pl.BlockSpec((1, d, TM_PROJ), lambda bi, i: (bi, 0, i)),
            pl.BlockSpec((1, N_PAIRS, TM_PROJ, PAIR_W), lambda bi, i: (bi, 0, i, 0)),
            pl.BlockSpec((1, d, TM_PROJ), lambda bi, i: (bi, 0, i)),
            pl.BlockSpec((1, d, TM_PROJ), lambda bi, i: (bi, 0, i)),
        ],
        out_shape=[feat, jax.ShapeDtypeStruct((b, N_PAIRS, s, PAIR_W), jnp.bfloat16), feat, feat],
        compiler_params=pltpu.CompilerParams(
            dimension_semantics=("arbitrary", "arbitrary"), vmem_limit_bytes=VMEM_LIMIT),
        name="proj_a",
    )(x, wt)


def _chunk_iota(shape, axis):
    return lax.shift_right_logical(lax.broadcasted_iota(jnp.int32, shape, axis), CHUNK.bit_length() - 1)


def _toeplitz(vec_row, n_keys):
    return pltpu.roll(jnp.broadcast_to(vec_row, (n_keys, BIAS_PERIOD)), 0, 1, stride=1, stride_axis=0)


def _fill_bias_slabs(vec_ref, bias_ref, n_keys, bands):
    n = len(bands)
    kch = _chunk_iota((n_keys, TQ), 0)
    qch = _chunk_iota((n_keys, TQ), 1)

    def head_body(h, carry):
        for j, band in enumerate(bands):
            t = _toeplitz(vec_ref[j, h], n_keys)[:, :TQ]
            if band is not None:
                t = jnp.where(band(kch, qch), t, MASKED)
            bias_ref[h * n + j] = t
        return carry

    lax.fori_loop(0, N_HEADS, head_body, 0)


def _live_rows(band, n_keys):
    chunks_per_half = LANES // CHUNK
    out = []
    for half in range(TQ // LANES):
        qchs = range(half * chunks_per_half, (half + 1) * chunks_per_half)
        live = [kc for kc in range(n_keys // CHUNK) if band is None or any(band(kc, qc) for qc in qchs)]
        out.append((live[0] * CHUNK, (live[-1] + 1) * CHUNK) if live else (0, 0))
    return tuple(out)


def _scores_stage(k_tiles, qz, bias_fns, live, s_ref):
    maxes, row = [], 0
    for half in range(TQ // LANES):
        maxes.append([None] * MAX_CHAINS)
    for kt, bias_fn, live_j in zip(k_tiles, bias_fns, live):
        s = jnp.dot(kt, qz, preferred_element_type=jnp.float32)
        for half, (r0, r1) in enumerate(live_j):
            lanes = slice(half * LANES, (half + 1) * LANES)
            acc = maxes[half]
            for r in range(r0 // SUBLANES, r1 // SUBLANES):
                rows = slice(r * SUBLANES, (r + 1) * SUBLANES)
                grp = s[rows, lanes] + bias_fn(rows, lanes)
                s_ref[row + r * SUBLANES:row + (r + 1) * SUBLANES, lanes] = grp
                c = r % MAX_CHAINS
                acc[c] = grp if acc[c] is None else jnp.maximum(acc[c], grp)
        row += s.shape[0]
    cols = [jnp.max(functools.reduce(jnp.maximum, [a for a in acc if a is not None]), axis=0, keepdims=True)
            for acc in maxes]
    return jnp.concatenate(cols, axis=1)


def _pv_stage(s_ref, m, v_tiles, live, extra_logit=None):
    if extra_logit is not None:
        m = jnp.maximum(m, extra_logit)
    acc, row = None, 0
    for vt, live_j in zip(v_tiles, live):
        n = vt.shape[1]
        halves = []
        for half, (r0, r1) in enumerate(live_j):
            lanes = slice(half * LANES, (half + 1) * LANES)
            parts = [jnp.zeros((r0, LANES), jnp.bfloat16)] if r0 else []
            if r1 > r0:
                parts.append(jnp.exp2(s_ref[row + r0:row + r1, lanes] - m[:, lanes]).astype(jnp.bfloat16))
            if n > r1:
                parts.append(jnp.zeros((n - r1, LANES), jnp.bfloat16))
            halves.append(parts[0] if len(parts) == 1 else jnp.concatenate(parts, axis=0))
        p = jnp.concatenate(halves, axis=1)
        v_ones = jnp.concatenate([vt, jnp.ones((BF16_ROWS, n), vt.dtype)], axis=0)
        part = jnp.dot(v_ones, p, preferred_element_type=jnp.float32)
        acc = part if acc is None else acc + part
        row += n
    l = acc[HEAD_DIM:HEAD_DIM + 1, :]
    if extra_logit is not None:
        l = l + jnp.exp2(extra_logit - m)
    return acc[:HEAD_DIM, :] / l


def _head_pipeline(n_tiles, scores_fn, pv_fn, s_refs, scores_first):
    per_group = len(s_refs) // 2
    x_refs, y_refs = s_refs[:per_group], s_refs[per_group:]
    n_groups = n_tiles // per_group

    m = [scores_fn(t // 2, t % 2, ref) for t, ref in zip(range(per_group), x_refs)]
    for g in range(n_groups):
        src, dst = (x_refs, y_refs) if g % 2 == 0 else (y_refs, x_refs)
        nxt = [(g + 1) * per_group + j for j in range(per_group)] if g + 1 < n_groups else []
        m_next = [scores_fn(t // 2, t % 2, ref) for t, ref in zip(nxt, dst)] if scores_first else []
        for j in range(per_group):
            if not scores_first and nxt:
                m_next.append(scores_fn(nxt[j] // 2, nxt[j] % 2, dst[j]))
            t = g * per_group + j
            pv_fn(t // 2, t % 2, src[j], m[j])
        m = m_next


A_BANDS = (lambda kc, qc: kc >= qc,
           None,
           lambda kc, qc: kc <= qc)
A_LIVE = tuple(_live_rows(band, TQ) for band in A_BANDS)
B_WIN = B_PREV + LANES
B_BAND = lambda kc, qc: (kc >= qc) & (kc <= qc + B_LEFT_CHUNKS)
B_LIVE = (((0, B_WIN),) * (TQ // LANES),)


def _out_and_residual(zt_ref, wout_ref, res):
    y = lax.dot_general(zt_ref[...], wout_ref[...], _TN, preferred_element_type=jnp.float32)
    return res + y


def _attn_a_kernel(x_ref, qt_ref, k0_ref, k1_ref, k2_ref, v0_ref, v1_ref, v2_ref, gt_ref,
                   bvec_ref, wout_ref, kvwk_ref, nwt_ref,
                   h_ref, ksh_ref, vsht_ref, qbt_ref, gbt_ref, zt_ref, bias_ref, *s_refs):
    i = pl.program_id(1)

    @pl.when((pl.program_id(0) == 0) & (i == 0))
    def _():
        _fill_bias_slabs(bvec_ref, bias_ref, TQ, A_BANDS)
        bias_ref[N_HEADS * A_KBLOCKS] = jnp.full((TQ, TQ), MASKED, jnp.float32)

    k_refs = (k0_ref, k1_ref, k2_ref)
    v_refs = (v0_ref, v1_ref, v2_ref)
    n_slabs = N_HEADS * A_KBLOCKS
    rows = lax.broadcasted_iota(jnp.int32, (PAIR_W, TQ), 0)

    def scores_fn(p, hh, s_ref):
        q2 = qt_ref[0, p]
        own_rows = rows < HEAD_DIM if hh == 0 else rows >= HEAD_DIM
        qz = jnp.where(own_rows, q2, jnp.zeros_like(q2))
        k_tiles, bias_fns = [], []
        for j in range(A_KBLOCKS):
            in_seq = i - (A_KBLOCKS - 1) + j >= 0
            slab = jnp.where(in_seq, (2 * p + hh) * A_KBLOCKS + j, n_slabs)
            k_tiles.append(k_refs[j][0, p])
            bias_fns.append(lambda rws, lanes, slab=slab: bias_ref[slab, rws, lanes])
        return _scores_stage(k_tiles, qz, bias_fns, A_LIVE, s_ref)

    def pv_fn(p, hh, s_ref, m):
        half = slice(hh * HEAD_DIM, (hh + 1) * HEAD_DIM)
        o = _pv_stage(s_ref, m, [v_ref[0, p, half, :] for v_ref in v_refs], A_LIVE)
        gate = gt_ref[0, p, half, :].astype(jnp.float32)
        row0 = p * PAIR_W + hh * HEAD_DIM
        zt_ref[row0:row0 + HEAD_DIM, :] = (o * gate).astype(jnp.bfloat16)

    _head_pipeline(N_HEADS, scores_fn, pv_fn, s_refs, scores_first=True)

    h = _out_and_residual(zt_ref, wout_ref, x_ref[0])
    h_ref[0] = h
    hn = (h * _rms_scale(h)).astype(jnp.bfloat16)
    ksh_ref[0] = jnp.dot(hn, kvwk_ref[...], preferred_element_type=jnp.float32).astype(jnp.bfloat16)
    for out_ref, base, act in ((qbt_ref, W_T_COLS_B, None), (gbt_ref, W_T_COLS_B + D_MODEL, _silu)):
        for c in range(0, D_MODEL, FEAT_CHUNK):
            first = out_ref is qbt_ref and c == 0
            start = base + c - (PAIR_W if first else 0)
            yt = lax.dot_general(nwt_ref[start:base + c + FEAT_CHUNK, :], hn, _NT,
                                 preferred_element_type=jnp.float32)
            if first:
                vsht_ref[0] = yt[:PAIR_W].astype(jnp.bfloat16)
                yt = yt[PAIR_W:]
            out_ref[0, c:c + FEAT_CHUNK, :] = (yt if act is None else act(yt)).astype(jnp.bfloat16)


def _const_spec(shape):
    return pl.BlockSpec(shape, lambda bi, i: (0,) * len(shape), pipeline_mode=pl.Buffered(1))


def _attn_a(x, qt, k, vt, gt, bvec, wout, kvwk, nwt):
    b, s, d = x.shape
    qt4 = qt.reshape(b, N_PAIRS, PAIR_W, s)
    vt4 = vt.reshape(b, N_PAIRS, PAIR_W, s)
    gt4 = gt.reshape(b, N_PAIRS, PAIR_W, s)
    back = A_KBLOCKS - 1

    def kspec(j):
        return pl.BlockSpec((1, N_PAIRS, TQ, PAIR_W),
                            lambda bi, i: (bi, 0, jnp.maximum(i - back + j, 0), 0))

    def vspec(j):
        return pl.BlockSpec((1, N_PAIRS, PAIR_W, TQ),
                            lambda bi, i: (bi, 0, 0, jnp.maximum(i - back + j, 0)))

    feat_spec = pl.BlockSpec((1, N_PAIRS, PAIR_W, TQ), lambda bi, i: (bi, 0, 0, i))
    featout_spec = pl.BlockSpec((1, d, TQ), lambda bi, i: (bi, 0, i))
    feat = jax.ShapeDtypeStruct((b, d, s), jnp.bfloat16)
    return pl.pallas_call(
        _attn_a_kernel,
        grid=(b, s // TQ),
        in_specs=[
            pl.BlockSpec((1, TQ, d), lambda bi, i: (bi, i, 0)),
            feat_spec,
            kspec(0), kspec(1), kspec(2),
            vspec(0), vspec(1), vspec(2),
            feat_spec,
            _const_spec(bvec.shape),
            _const_spec(wout.shape),
            _const_spec(kvwk.shape),
            _const_spec(nwt.shape),
        ],
        out_specs=[
            pl.BlockSpec((1, TQ, d), lambda bi, i: (bi, i, 0)),
            pl.BlockSpec((1, TQ, PAIR_W), lambda bi, i: (bi, i, 0)),
            pl.BlockSpec((1, PAIR_W, TQ), lambda bi, i: (bi, 0, i)),
            featout_spec,
            featout_spec,
        ],
        out_shape=[
            jax.ShapeDtypeStruct((b, s, d), jnp.float32),
            jax.ShapeDtypeStruct((b, s, PAIR_W), jnp.bfloat16),
            jax.ShapeDtypeStruct((b, PAIR_W, s), jnp.bfloat16),
            feat, feat,
        ],
        scratch_shapes=[pltpu.VMEM((d, TQ), jnp.bfloat16),
                        pltpu.VMEM((N_HEADS * A_KBLOCKS + 1, TQ, TQ), jnp.float32),
                        *[pltpu.VMEM((A_KBLOCKS * TQ, TQ), jnp.float32)] * N_SCORE_BUFS],
        compiler_params=pltpu.CompilerParams(
            dimension_semantics=("arbitrary", "arbitrary"), vmem_limit_bytes=VMEM_LIMIT),
        name="attn_a",
    )(x, qt4, k, k, k, vt4, vt4, vt4, gt4, bvec, wout, kvwk, nwt)


def _attn_b_kernel(h_ref, qt_ref, kp_ref, ko_ref, vp_ref, vo_ref, gt_ref, vec_ref,
                   sink_ref, wout_ref, fg_ref, out_ref, zt_ref, bias_ref, *s_refs):
    i = pl.program_id(1)

    @pl.when((pl.program_id(0) == 0) & (i == 0))
    def _():
        in_band = B_BAND(_chunk_iota((B_WIN, LANES), 0), _chunk_iota((B_WIN, LANES), 1))
        before_block = lax.broadcasted_iota(jnp.int32, (B_WIN, LANES), 0) < B_PREV

        def head_body(h, carry):
            t = jnp.where(in_band, _toeplitz(vec_ref[0, h], B_WIN)[:, :LANES], MASKED)
            bias_ref[2 * h] = t
            bias_ref[2 * h + 1] = jnp.where(before_block, MASKED, t)
            return carry

        lax.fori_loop(0, N_HEADS, head_body, 0)

    rows = lax.broadcasted_iota(jnp.int32, (PAIR_W, TQ), 0)
    pairs_per_kv = N_PAIRS // B_KV_HEADS

    def scores_fn(p, u, s_ref):
        qlanes = slice(u * LANES, (u + 1) * LANES)
        q2 = jnp.concatenate([qt_ref[0, p, :HEAD_DIM, qlanes], qt_ref[0, p, HEAD_DIM:, qlanes]], axis=1)
        kv0 = HEAD_DIM * (p // pairs_per_kv)
        kv_rows = (rows >= kv0) & (rows < kv0 + HEAD_DIM)
        qz = jnp.where(kv_rows, jnp.concatenate([q2, q2], axis=0), jnp.zeros((PAIR_W, TQ), q2.dtype))
        if u == 0:
            k_tile = jnp.concatenate([kp_ref[0, TQ - B_PREV:, :], ko_ref[0, :LANES, :]], axis=0)
            variant = jnp.where(i == 0, 1, 0)
        else:
            k_tile, variant = ko_ref[0], 0

        def bias_fn(rws, lanes):
            head = 2 * p + lanes.start // LANES
            return bias_ref[2 * head + variant, rws, :]

        return _scores_stage([k_tile], qz, [bias_fn], B_LIVE, s_ref)

    def pv_fn(p, u, s_ref, m):
        qlanes = slice(u * LANES, (u + 1) * LANES)
        kv_rows = slice(HEAD_DIM * (p // pairs_per_kv), HEAD_DIM * (p // pairs_per_kv + 1))
        if u == 0:
            v_tile = jnp.concatenate([vp_ref[0, kv_rows, TQ - B_PREV:], vo_ref[0, kv_rows, :LANES]], axis=1)
        else:
            v_tile = vo_ref[0, kv_rows, :]
        o = _pv_stage(s_ref, m, [v_tile], B_LIVE, extra_logit=sink_ref[p])
        for hh in range(2):
            hrows = slice(hh * HEAD_DIM, (hh + 1) * HEAD_DIM)
            gate = gt_ref[0, p, hrows, qlanes].astype(jnp.float32)
            row0 = p * PAIR_W + hh * HEAD_DIM
            zt_ref[row0:row0 + HEAD_DIM, qlanes] = (o[:, hh * LANES:(hh + 1) * LANES] * gate).astype(jnp.bfloat16)

    _head_pipeline(N_HEADS, scores_fn, pv_fn, s_refs, scores_first=False)

    h2 = _out_and_residual(zt_ref, wout_ref, h_ref[0])
    out_ref[0] = h2 * _rms_scale(h2) * fg_ref[...]


def _attn_b(h, qbt, ksh, vsht, gbt, vec, sinks, wout, fg):
    b, s, d = h.shape
    qt4 = qbt.reshape(b, N_PAIRS, PAIR_W, s)
    gt4 = gbt.reshape(b, N_PAIRS, PAIR_W, s)
    feat_spec = pl.BlockSpec((1, N_PAIRS, PAIR_W, TQ), lambda bi, i: (bi, 0, 0, i))
    return pl.pallas_call(
        _attn_b_kernel,
        grid=(b, s // TQ),
        in_specs=[
            pl.BlockSpec((1, TQ, d), lambda bi, i: (bi, i, 0)),
            feat_spec,
            pl.BlockSpec((1, TQ, PAIR_W), lambda bi, i: (bi, jnp.maximum(i - 1, 0), 0)),
            pl.BlockSpec((1, TQ, PAIR_W), lambda bi, i: (bi, i, 0)),
            pl.BlockSpec((1, PAIR_W, TQ), lambda bi, i: (bi, 0, jnp.maximum(i - 1, 0))),
            pl.BlockSpec((1, PAIR_W, TQ), lambda bi, i: (bi, 0, i)),
            feat_spec,
            _const_spec(vec.shape),
            _const_spec(sinks.shape),
            _const_spec(wout.shape),
            _const_spec(fg.shape),
        ],
        out_specs=pl.BlockSpec((1, TQ, d), lambda bi, i: (bi, i, 0)),
        out_shape=jax.ShapeDtypeStruct((b, s, d), jnp.float32),
        scratch_shapes=[pltpu.VMEM((d, TQ), jnp.bfloat16),
                        pltpu.VMEM((2 * N_HEADS, B_WIN, LANES), jnp.float32),
                        *[pltpu.VMEM((B_WIN, TQ), jnp.float32)] * N_SCORE_BUFS_B],
        compiler_params=pltpu.CompilerParams(
            dimension_semantics=("arbitrary", "arbitrary"), vmem_limit_bytes=VMEM_LIMIT),
        name="attn_b",
    )(h, qt4, ksh, ksh, vsht, vsht, gt4, vec, sinks, wout, fg)


def _slab_diff():
    u = jnp.arange(BIAS_PERIOD)
    return jnp.where(u < TQ, u, u - BIAS_PERIOD)


def _bias_vecs_a(rel_bias):
    back = (A_KBLOCKS - 1 - jnp.arange(A_KBLOCKS))[:, None] * TQ
    dist = back + _slab_diff()[None, :]
    idx = jnp.clip(dist, -A_REL_CLIP, A_REL_CLIP) + A_REL_CLIP
    vec = jnp.transpose(rel_bias[idx], (0, 2, 1)).astype(jnp.float32)
    return vec[:, :, None, :]


def _t5_bucket(rel):
    nb = T5_BUCKETS // 2
    max_exact = nb // 2
    ret = jnp.where(rel > 0, nb, 0)
    n = jnp.abs(rel)
    nf = jnp.maximum(n, 1).astype(jnp.float32)
    large = max_exact + (jnp.log(nf / max_exact) / math.log(T5_MAX_DIST / max_exact)
                         * (nb - max_exact)).astype(jnp.int32)
    large = jnp.minimum(large, nb - 1)
    return ret + jnp.where(n < max_exact, n, large)


def _bias_vec_b(t5_table):
    rel = -_slab_diff() - B_PREV
    vec = jnp.transpose(t5_table[_t5_bucket(rel)], (1, 0)).astype(jnp.float32)
    return vec[None, :, None, :]


def kernel(x, a_norm, a_w_in, a_rel_bias, a_w_out, kv_norm, kv_w, t5_bias,
           b_norm, b_w_in, b_sinks, b_w_out, final_norm):
    assert a_norm.shape[0] == 1 and b_norm.shape[0] == 1, "one A layer then one B layer"
    bf = jnp.bfloat16
    scale = HEAD_DIM ** -0.5 * LOG2E

    def q_col_scale(n):
        return jnp.where(jnp.arange(n) < D_MODEL, scale, 1.0).astype(jnp.float32)

    wt_a = _weight_t(a_w_in[0], q_col_scale(a_w_in.shape[2]), a_norm[0], W_T_COLS_A)
    qt, k, vt, gt = _proj_a(x, wt_a)

    nwt = _weight_t(b_w_in[0], q_col_scale(b_w_in.shape[2]), b_norm[0], W_T_COLS_B,
                    head=kv_w, head_block=1, head_w=PAIR_W, head_gain=kv_norm)
    kvw_k = (kv_w[:, :PAIR_W] * kv_norm[:, None]).astype(bf)
    h, ksh, vsht, qbt, gbt = _attn_a(
        x, qt, k, vt, gt, _bias_vecs_a(a_rel_bias[0] * LOG2E), a_w_out[0].astype(bf), kvw_k, nwt)

    sinks = jnp.repeat((b_sinks[0].astype(jnp.float32) * LOG2E).reshape(N_PAIRS, 1, 2), LANES, axis=2)
    return _attn_b(h, qbt, ksh, vsht, gbt, _bias_vec_b(t5_bias * LOG2E), sinks, b_w_out[0].astype(bf),
                   final_norm[None, :])
```

```python
import functools
import math

import jax
import jax.numpy as jnp
from jax import lax
from jax.experimental import pallas as pl
from jax.experimental.pallas import tpu as pltpu

D_MODEL = 1024
HEAD_DIM = 64
N_HEADS = D_MODEL // HEAD_DIM
N_PAIRS = N_HEADS // 2
PAIR_W = 2 * HEAD_DIM
CHUNK = 64
RMS_EPS = 1e-6
A_LEFT_CHUNKS = 8
A_REL_CLIP = 256
B_KV_HEADS = 2
B_GROUP = N_HEADS // B_KV_HEADS
B_LEFT_CHUNKS = 2
T5_BUCKETS = 32
T5_MAX_DIST = 128

TQ = 256
A_KBLOCKS = A_LEFT_CHUNKS * CHUNK // TQ + 1
B_PREV = B_LEFT_CHUNKS * CHUNK
TM_PROJ = 1024
FEAT_CHUNK = 512
W_T_COLS_A = 1024
W_T_COLS_B = 512
MASKED = -1e30
BIAS_PERIOD = 2 * TQ
SUBLANES = 8
LANES = 128
MAX_CHAINS = 2
BF16_ROWS = 16
LOG2E = math.log2(math.e)
N_SCORE_BUFS = 4
N_SCORE_BUFS_B = 8
VMEM_LIMIT = 56 * 1024 * 1024

_NT = (((1,), (1,)), ((), ()))
_TN = (((0,), (0,)), ((), ()))


def _rms_scale(xf):
    return lax.rsqrt(jnp.mean(xf * xf, axis=-1, keepdims=True) + RMS_EPS)


def _silu(v):
    return v * jax.nn.sigmoid(v)


def _weight_t_kernel(*refs, has_head):
    if has_head:
        head_ref, hgain_ref, w_ref, cs_ref, gain_ref, o_ref = refs
        j = pl.program_id(0)

        @pl.when(j == 0)
        def _():
            pad = o_ref.shape[0] - head_ref.shape[1]
            o_ref[:pad, :] = jnp.zeros((pad, o_ref.shape[1]), jnp.bfloat16)
            o_ref[pad:, :] = (head_ref[...].T * hgain_ref[...]).astype(jnp.bfloat16)

        @pl.when(j > 0)
        def _():
            o_ref[...] = ((w_ref[...] * cs_ref[...]).T * gain_ref[...]).astype(jnp.bfloat16)
    else:
        w_ref, cs_ref, gain_ref, o_ref = refs
        o_ref[...] = ((w_ref[...] * cs_ref[...]).T * gain_ref[...]).astype(jnp.bfloat16)


def _weight_t(w, col_scale, row_gain, tc, head=None, head_block=0, head_w=None, head_gain=None):
    d, n = w.shape
    off = 0 if head is None else 1
    main_specs = [
        pl.BlockSpec((d, tc), lambda j: (0, jnp.maximum(j - off, 0))),
        pl.BlockSpec((1, tc), lambda j: (0, jnp.maximum(j - off, 0))),
        pl.BlockSpec((1, d), lambda j: (0, 0)),
    ]
    head_specs, head_args = [], []
    if head is not None:
        assert head.shape[0] == d and head.shape[1] % head_w == 0 and head_w <= tc
        head_specs = [pl.BlockSpec((d, head_w), lambda j: (0, head_block)), pl.BlockSpec((1, d), lambda j: (0, 0))]
        head_args = [head, head_gain[None, :]]
    return pl.pallas_call(
        functools.partial(_weight_t_kernel, has_head=head is not None),
        grid=(n // tc + off,),
        in_specs=head_specs + main_specs,
        out_specs=pl.BlockSpec((tc, d), lambda j: (j, 0)),
        out_shape=jax.ShapeDtypeStruct((n + off * tc, d), jnp.bfloat16),
        compiler_params=pltpu.CompilerParams(dimension_semantics=("arbitrary",), vmem_limit_bytes=VMEM_LIMIT),
        name="weight_t",
    )(*head_args, w, col_scale[None, :], row_gain[None, :])


def _proj_a_kernel(x_ref, wt_ref, qt_ref, k_ref, vt_ref, gt_ref):
    xf = x_ref[0]
    xn = (xf * _rms_scale(xf)).astype(jnp.bfloat16)
    k = lax.dot_general(xn, wt_ref[D_MODEL:2 * D_MODEL, :], _NT,
                        preferred_element_type=jnp.float32)
    for p in range(N_PAIRS):
        k_ref[0, p] = k[:, p * PAIR_W:(p + 1) * PAIR_W].astype(jnp.bfloat16)
    for out_ref, base, act in ((gt_ref, 3 * D_MODEL, _silu), (vt_ref, 2 * D_MODEL, None), (qt_ref, 0, None)):
        for c in range(0, D_MODEL, FEAT_CHUNK):
            w = wt_ref[base + c:base + c + FEAT_CHUNK, :]
            yt = lax.dot_general(w, xn, _NT, preferred_element_type=jnp.float32)
            out_ref[0, c:c + FEAT_CHUNK, :] = (yt if act is None else act(yt)).astype(jnp.bfloat16)


def _proj_a(x, wt):
    b, s, d = x.shape
    feat = jax.ShapeDtypeStruct((b, d, s), jnp.bfloat16)
    return pl.pallas_call(
        _proj_a_kernel,
        grid=(b, s // TM_PROJ),
        in_specs=[
            pl.BlockSpec((1, TM_PROJ, d), lambda bi, i: (bi, i, 0)),
            _const_spec(wt.shape),
        ],
        out_specs=[
            pl.BlockSpec((1, d, TM_PROJ), lambda bi, i: (bi, 0, i)),
            pl.BlockSpec((1, N_PAIRS, TM_PROJ, PAIR_W), lambda bi, i: (bi, 0, i, 0)),
            pl.BlockSpec((1, d, TM_PROJ), lambda bi, i: (bi, 0, i)),
            pl.BlockSpec((1, d, TM_PROJ), lambda bi, i: (bi, 0, i)),
        ],
        out_shape=[feat, jax.ShapeDtypeStruct((b, N_PAIRS, s, PAIR_W), jnp.bfloat16), feat, feat],
        compiler_params=pltpu.CompilerParams(
            dimension_semantics=("arbitrary", "arbitrary"), vmem_limit_bytes=VMEM_LIMIT),
        name="proj_a",
    )(x, wt)


def _chunk_iota(shape, axis):
    return lax.shift_right_logical(lax.broadcasted_iota(jnp.int32, shape, axis), CHUNK.bit_length() - 1)


def _toeplitz(vec_row, n_keys):
    return pltpu.roll(jnp.broadcast_to(vec_row, (n_keys, BIAS_PERIOD)), 0, 1, stride=1, stride_axis=0)


def _fill_bias_slabs(vec_ref, bias_ref, n_keys, bands):
    n = len(bands)
    kch = _chunk_iota((n_keys, TQ), 0)
    qch = _chunk_iota((n_keys, TQ), 1)

    def head_body(h, carry):
        for j, band in enumerate(bands):
            t = _toeplitz(vec_ref[j, h], n_keys)[:, :TQ]
            if band is not None:
                t = jnp.where(band(kch, qch), t, MASKED)
            bias_ref[h * n + j] = t
        return carry

    lax.fori_loop(0, N_HEADS, head_body, 0)


def _live_rows(band, n_keys):
    chunks_per_half = LANES // CHUNK
    out = []
    for half in range(TQ // LANES):
        qchs = range(half * chunks_per_half, (half + 1) * chunks_per_half)
        live = [kc for kc in range(n_keys // CHUNK) if band is None or any(band(kc, qc) for qc in qchs)]
        out.append((live[0] * CHUNK, (live[-1] + 1) * CHUNK) if live else (0, 0))
    return tuple(out)


def _scores_stage(k_tiles, q_tiles, bias_fns, live, s_ref):
    maxes, row = [], 0
    for half in range(TQ // LANES):
        maxes.append([None] * MAX_CHAINS)
    for kt, qz, bias_fn, live_j in zip(k_tiles, q_tiles, bias_fns, live):
        s = jnp.dot(kt, qz, preferred_element_type=jnp.float32)
        for half, (r0, r1) in enumerate(live_j):
            lanes = slice(half * LANES, (half + 1) * LANES)
            acc = maxes[half]
            for r in range(r0 // SUBLANES, r1 // SUBLANES):
                rows = slice(r * SUBLANES, (r + 1) * SUBLANES)
                grp = s[rows, lanes] if bias_fn is None else s[rows, lanes] + bias_fn(rows, lanes)
                s_ref[row + r * SUBLANES:row + (r + 1) * SUBLANES, lanes] = grp
                c = r % MAX_CHAINS
                acc[c] = grp if acc[c] is None else jnp.maximum(acc[c], grp)
        row += s.shape[0]
    cols = [jnp.max(functools.reduce(jnp.maximum, [a for a in acc if a is not None]), axis=0, keepdims=True)
            for acc in maxes]
    return jnp.concatenate(cols, axis=1)


def _pv_stage(s_ref, m, v_tiles, live, extra_logit=None):
    if extra_logit is not None:
        m = jnp.maximum(m, extra_logit)
    acc, row = None, 0
    for vt, live_j in zip(v_tiles, live):
        n = vt.shape[1]
        halves = []
        for half, (r0, r1) in enumerate(live_j):
            lanes = slice(half * LANES, (half + 1) * LANES)
            parts = [jnp.zeros((r0, LANES), jnp.bfloat16)] if r0 else []
            if r1 > r0:
                parts.append(jnp.exp2(s_ref[row + r0:row + r1, lanes] - m[:, lanes]).astype(jnp.bfloat16))
            if n > r1:
                parts.append(jnp.zeros((n - r1, LANES), jnp.bfloat16))
            halves.append(parts[0] if len(parts) == 1 else jnp.concatenate(parts, axis=0))
        p = jnp.concatenate(halves, axis=1)
        v_ones = jnp.concatenate([vt, jnp.ones((BF16_ROWS, n), vt.dtype)], axis=0)
        part = jnp.dot(v_ones, p, preferred_element_type=jnp.float32)
        acc = part if acc is None else acc + part
        row += n
    l = acc[HEAD_DIM:HEAD_DIM + 1, :]
    if extra_logit is not None:
        l = l + jnp.exp2(extra_logit - m)
    return acc[:HEAD_DIM, :] / l


def _head_pipeline(n_tiles, scores_fn, pv_fn, s_refs, scores_first):
    per_group = len(s_refs) // 2
    x_refs, y_refs = s_refs[:per_group], s_refs[per_group:]
    n_groups = n_tiles // per_group

    m = [scores_fn(t // 2, t % 2, ref) for t, ref in zip(range(per_group), x_refs)]
    for g in range(n_groups):
        src, dst = (x_refs, y_refs) if g % 2 == 0 else (y_refs, x_refs)
        nxt = [(g + 1) * per_group + j for j in range(per_group)] if g + 1 < n_groups else []
        m_next = [scores_fn(t // 2, t % 2, ref) for t, ref in zip(nxt, dst)] if scores_first else []
        for j in range(per_group):
            if not scores_first and nxt:
                m_next.append(scores_fn(nxt[j] // 2, nxt[j] % 2, dst[j]))
            t = g * per_group + j
            pv_fn(t // 2, t % 2, src[j], m[j])
        m = m_next


A_BANDS = (lambda kc, qc: kc >= qc,
           None,
           lambda kc, qc: kc <= qc)
A_LIVE = tuple(_live_rows(band, TQ) for band in A_BANDS)
B_WIN = B_PREV + LANES
B_SUB = PAIR_W
B_BAND = lambda kc, qc: (kc >= qc) & (kc <= qc + B_LEFT_CHUNKS)
B_LIVE = (((0, B_WIN),) * (TQ // LANES),)


def _out_and_residual(zt_ref, wout_ref, res):
    y = lax.dot_general(zt_ref[...], wout_ref[...], _TN, preferred_element_type=jnp.float32)
    return res + y


def _attn_a_kernel(x_ref, qt_ref, k0_ref, k1_ref, k2_ref, v0_ref, v1_ref, v2_ref, gt_ref,
                   bvec_ref, wout_ref, kvwk_ref, nwt_ref,
                   h_ref, ksh_ref, vsht_ref, qbt_ref, gbt_ref, zt_ref, bias_ref, *s_refs):
    i = pl.program_id(1)

    @pl.when((pl.program_id(0) == 0) & (i == 0))
    def _():
        _fill_bias_slabs(bvec_ref, bias_ref, TQ, A_BANDS)
        bias_ref[N_HEADS * A_KBLOCKS] = jnp.full((TQ, TQ), MASKED, jnp.float32)

    k_refs = (k0_ref, k1_ref, k2_ref)
    v_refs = (v0_ref, v1_ref, v2_ref)
    n_slabs = N_HEADS * A_KBLOCKS
    rows = lax.broadcasted_iota(jnp.int32, (PAIR_W, TQ), 0)

    def scores_fn(p, hh, s_ref):
        q2 = qt_ref[0, p]
        own_rows = rows < HEAD_DIM if hh == 0 else rows >= HEAD_DIM
        qz = jnp.where(own_rows, q2, jnp.zeros_like(q2))
        k_tiles, bias_fns = [], []
        for j in range(A_KBLOCKS):
            in_seq = i - (A_KBLOCKS - 1) + j >= 0
            slab = jnp.where(in_seq, (2 * p + hh) * A_KBLOCKS + j, n_slabs)
            k_tiles.append(k_refs[j][0, p])
            bias_fns.append(lambda rws, lanes, slab=slab: bias_ref[slab, rws, lanes])
        return _scores_stage(k_tiles, [qz] * A_KBLOCKS, bias_fns, A_LIVE, s_ref)

    def pv_fn(p, hh, s_ref, m):
        half = slice(hh * HEAD_DIM, (hh + 1) * HEAD_DIM)
        o = _pv_stage(s_ref, m, [v_ref[0, p, half, :] for v_ref in v_refs], A_LIVE)
        gate = gt_ref[0, p, half, :].astype(jnp.float32)
        row0 = p * PAIR_W + hh * HEAD_DIM
        zt_ref[row0:row0 + HEAD_DIM, :] = (o * gate).astype(jnp.bfloat16)

    _head_pipeline(N_HEADS, scores_fn, pv_fn, s_refs, scores_first=True)

    h = _out_and_residual(zt_ref, wout_ref, x_ref[0])
    h_ref[0] = h
    hn = (h * _rms_scale(h)).astype(jnp.bfloat16)
    ksh_ref[0] = jnp.dot(hn, kvwk_ref[...], preferred_element_type=jnp.float32).astype(jnp.bfloat16)
    for out_ref, base, act in ((qbt_ref, W_T_COLS_B, None), (gbt_ref, W_T_COLS_B + D_MODEL, _silu)):
        for c in range(0, D_MODEL, FEAT_CHUNK):
            first = out_ref is qbt_ref and c == 0
            start = base + c - (PAIR_W if first else 0)
            yt = lax.dot_general(nwt_ref[start:base + c + FEAT_CHUNK, :], hn, _NT,
                                 preferred_element_type=jnp.float32)
            if first:
                vsht_ref[0] = yt[:PAIR_W].astype(jnp.bfloat16)
                yt = yt[PAIR_W:]
            out_ref[0, c:c + FEAT_CHUNK, :] = (yt if act is None else act(yt)).astype(jnp.bfloat16)


def _const_spec(shape):
    return pl.BlockSpec(shape, lambda bi, i: (0,) * len(shape), pipeline_mode=pl.Buffered(1))


def _attn_a(x, qt, k, vt, gt, bvec, wout, kvwk, nwt):
    b, s, d = x.shape
    qt4 = qt.reshape(b, N_PAIRS, PAIR_W, s)
    vt4 = vt.reshape(b, N_PAIRS, PAIR_W, s)
    gt4 = gt.reshape(b, N_PAIRS, PAIR_W, s)
    back = A_KBLOCKS - 1

    def kspec(j):
        return pl.BlockSpec((1, N_PAIRS, TQ, PAIR_W),
                            lambda bi, i: (bi, 0, jnp.maximum(i - back + j, 0), 0))

    def vspec(j):
        return pl.BlockSpec((1, N_PAIRS, PAIR_W, TQ),
                            lambda bi, i: (bi, 0, 0, jnp.maximum(i - back + j, 0)))

    feat_spec = pl.BlockSpec((1, N_PAIRS, PAIR_W, TQ), lambda bi, i: (bi, 0, 0, i))
    featout_spec = pl.BlockSpec((1, d, TQ), lambda bi, i: (bi, 0, i))
    feat = jax.ShapeDtypeStruct((b, d, s), jnp.bfloat16)
    return pl.pallas_call(
        _attn_a_kernel,
        grid=(b, s // TQ),
        in_specs=[
            pl.BlockSpec((1, TQ, d), lambda bi, i: (bi, i, 0)),
            feat_spec,
            kspec(0), kspec(1), kspec(2),
            vspec(0), vspec(1), vspec(2),
            feat_spec,
            _const_spec(bvec.shape),
            _const_spec(wout.shape),
            _const_spec(kvwk.shape),
            _const_spec(nwt.shape),
        ],
        out_specs=[
            pl.BlockSpec((1, TQ, d), lambda bi, i: (bi, i, 0)),
            pl.BlockSpec((1, TQ, PAIR_W), lambda bi, i: (bi, i, 0)),
            pl.BlockSpec((1, PAIR_W, TQ), lambda bi, i: (bi, 0, i)),
            featout_spec,
            featout_spec,
        ],
        out_shape=[
            jax.ShapeDtypeStruct((b, s, d), jnp.float32),
            jax.ShapeDtypeStruct((b, s, PAIR_W), jnp.bfloat16),
            jax.ShapeDtypeStruct((b, PAIR_W, s), jnp.bfloat16),
            feat, feat,
        ],
        scratch_shapes=[pltpu.VMEM((d, TQ), jnp.bfloat16),
                        pltpu.VMEM((N_HEADS * A_KBLOCKS + 1, TQ, TQ), jnp.float32),
                        *[pltpu.VMEM((A_KBLOCKS * TQ, TQ), jnp.float32)] * N_SCORE_BUFS],
        compiler_params=pltpu.CompilerParams(
            dimension_semantics=("arbitrary", "arbitrary"), vmem_limit_bytes=VMEM_LIMIT),
        name="attn_a",
    )(x, qt4, k, k, k, vt4, vt4, vt4, gt4, bvec, wout, kvwk, nwt)


def _attn_b_kernel(h_ref, qt_ref, kp_ref, ko_ref, vp_ref, vo_ref, gt_ref, vec_ref,
                   sink_ref, wout_ref, fg_ref, out_ref, zt_ref, bias_ref, *s_refs):
    i = pl.program_id(1)

    @pl.when((pl.program_id(0) == 0) & (i == 0))
    def _():
        in_band = B_BAND(_chunk_iota((B_WIN, LANES), 0), _chunk_iota((B_WIN, LANES), 1))
        before_block = lax.broadcasted_iota(jnp.int32, (B_WIN, LANES), 0) < B_PREV

        def pair_body(p, carry):
            for hh in range(2):
                t = jnp.where(in_band, _toeplitz(vec_ref[0, 2 * p + hh], B_WIN)[:, :LANES], MASKED)
                lanes = slice(hh * LANES, (hh + 1) * LANES)
                bias_ref[2 * p, :, lanes] = t.astype(jnp.bfloat16)
                bias_ref[2 * p + 1, :, lanes] = jnp.where(before_block, MASKED, t).astype(jnp.bfloat16)
            return carry

        lax.fori_loop(0, N_PAIRS, pair_body, 0)

    rows = lax.broadcasted_iota(jnp.int32, (PAIR_W, TQ), 0)
    pairs_per_kv = N_PAIRS // B_KV_HEADS
    eye = (lax.broadcasted_iota(jnp.int32, (B_SUB, B_SUB), 0)
           == lax.broadcasted_iota(jnp.int32, (B_SUB, B_SUB), 1)).astype(jnp.bfloat16)

    def scores_fn(p, u, s_ref):
        qlanes = slice(u * LANES, (u + 1) * LANES)
        q2 = jnp.concatenate([qt_ref[0, p, :HEAD_DIM, qlanes], qt_ref[0, p, HEAD_DIM:, qlanes]], axis=1)
        kv0 = HEAD_DIM * (p // pairs_per_kv)
        kv_rows = (rows >= kv0) & (rows < kv0 + HEAD_DIM)
        qz = jnp.where(kv_rows, jnp.concatenate([q2, q2], axis=0), jnp.zeros((PAIR_W, TQ), q2.dtype))
        if u == 0:
            k_tile = jnp.concatenate([kp_ref[0, TQ - B_PREV:, :], ko_ref[0, :LANES, :]], axis=0)
            variant = jnp.where(i == 0, 1, 0)
        else:
            k_tile, variant = ko_ref[0], 0

        lhs, rhs = [], []
        for r0 in range(0, B_WIN, B_SUB):
            lhs.append(jnp.concatenate([k_tile[r0:r0 + B_SUB, :], eye], axis=1))
            rhs.append(jnp.concatenate([qz, bias_ref[2 * p + variant, r0:r0 + B_SUB, :]], axis=0))
        sub_live = (((0, B_SUB),) * (TQ // LANES),) * len(lhs)
        return _scores_stage(lhs, rhs, [None] * len(lhs), sub_live, s_ref)

    def pv_fn(p, u, s_ref, m):
        qlanes = slice(u * LANES, (u + 1) * LANES)
        kv_rows = slice(HEAD_DIM * (p // pairs_per_kv), HEAD_DIM * (p // pairs_per_kv + 1))
        if u == 0:
            v_tile = jnp.concatenate([vp_ref[0, kv_rows, TQ - B_PREV:], vo_ref[0, kv_rows, :LANES]], axis=1)
        else:
            v_tile = vo_ref[0, kv_rows, :]
        o = _pv_stage(s_ref, m, [v_tile], B_LIVE, extra_logit=sink_ref[p])
        for hh in range(2):
            hrows = slice(hh * HEAD_DIM, (hh + 1) * HEAD_DIM)
            gate = gt_ref[0, p, hrows, qlanes].astype(jnp.float32)
            row0 = p * PAIR_W + hh * HEAD_DIM
            zt_ref[row0:row0 + HEAD_DIM, qlanes] = (o[:, hh * LANES:(hh + 1) * LANES] * gate).astype(jnp.bfloat16)

    _head_pipeline(N_HEADS, scores_fn, pv_fn, s_refs, scores_first=False)

    h2 = _out_and_residual(zt_ref, wout_ref, h_ref[0])
    out_ref[0] = h2 * _rms_scale(h2) * fg_ref[...]


def _attn_b(h, qbt, ksh, vsht, gbt, vec, sinks, wout, fg):
    b, s, d = h.shape
    qt4 = qbt.reshape(b, N_PAIRS, PAIR_W, s)
    gt4 = gbt.reshape(b, N_PAIRS, PAIR_W, s)
    feat_spec = pl.BlockSpec((1, N_PAIRS, PAIR_W, TQ), lambda bi, i: (bi, 0, 0, i))
    return pl.pallas_call(
        _attn_b_kernel,
        grid=(b, s // TQ),
        in_specs=[
            pl.BlockSpec((1, TQ, d), lambda bi, i: (bi, i, 0)),
            feat_spec,
            pl.BlockSpec((1, TQ, PAIR_W), lambda bi, i: (bi, jnp.maximum(i - 1, 0), 0)),
            pl.BlockSpec((1, TQ, PAIR_W), lambda bi, i: (bi, i, 0)),
            pl.BlockSpec((1, PAIR_W, TQ), lambda bi, i: (bi, 0, jnp.maximum(i - 1, 0))),
            pl.BlockSpec((1, PAIR_W, TQ), lambda bi, i: (bi, 0, i)),
            feat_spec,
            _const_spec(vec.shape),
            _const_spec(sinks.shape),
            _const_spec(wout.shape),
            _const_spec(fg.shape),
        ],
        out_specs=pl.BlockSpec((1, TQ, d), lambda bi, i: (bi, i, 0)),
        out_shape=jax.ShapeDtypeStruct((b, s, d), jnp.float32),
        scratch_shapes=[pltpu.VMEM((d, TQ), jnp.bfloat16),
                        pltpu.VMEM((2 * N_PAIRS, B_WIN, TQ), jnp.bfloat16),
                        *[pltpu.VMEM((B_WIN, TQ), jnp.float32)] * N_SCORE_BUFS_B],
        compiler_params=pltpu.CompilerParams(
            dimension_semantics=("arbitrary", "arbitrary"), vmem_limit_bytes=VMEM_LIMIT),
        name="attn_b",
    )(h, qt4, ksh, ksh, vsht, vsht, gt4, vec, sinks, wout, fg)


def _slab_diff():
    u = jnp.arange(BIAS_PERIOD)
    return jnp.where(u < TQ, u, u - BIAS_PERIOD)


def _bias_vecs_a(rel_bias):
    back = (A_KBLOCKS - 1 - jnp.arange(A_KBLOCKS))[:, None] * TQ
    dist = back + _slab_diff()[None, :]
    idx = jnp.clip(dist, -A_REL_CLIP, A_REL_CLIP) + A_REL_CLIP
    vec = jnp.transpose(rel_bias[idx], (0, 2, 1)).astype(jnp.float32)
    return vec[:, :, None, :]


def _t5_bucket(rel):
    nb = T5_BUCKETS // 2
    max_exact = nb // 2
    ret = jnp.where(rel > 0, nb, 0)
    n = jnp.abs(rel)
    nf = jnp.maximum(n, 1).astype(jnp.float32)
    large = max_exact + (jnp.log(nf / max_exact) / math.log(T5_MAX_DIST / max_exact)
                         * (nb - max_exact)).astype(jnp.int32)
    large = jnp.minimum(large, nb - 1)
    return ret + jnp.where(n < max_exact, n, large)


def _bias_vec_b(t5_table):
    rel = -_slab_diff() - B_PREV
    vec = jnp.transpose(t5_table[_t5_bucket(rel)], (1, 0)).astype(jnp.float32)
    return vec[None, :, None, :]


def kernel(x, a_norm, a_w_in, a_rel_bias, a_w_out, kv_norm, kv_w, t5_bias,
           b_norm, b_w_in, b_sinks, b_w_out, final_norm):
    assert a_norm.shape[0] == 1 and b_norm.shape[0] == 1, "one A layer then one B layer"
    bf = jnp.bfloat16
    scale = HEAD_DIM ** -0.5 * LOG2E

    def q_col_scale(n):
        return jnp.where(jnp.arange(n) < D_MODEL, scale, 1.0).astype(jnp.float32)

    wt_a = _weight_t(a_w_in[0], q_col_scale(a_w_in.shape[2]), a_norm[0], W_T_COLS_A)
    qt, k, vt, gt = _proj_a(x, wt_a)

    nwt = _weight_t(b_w_in[0], q_col_scale(b_w_in.shape[2]), b_norm[0], W_T_COLS_B,
                    head=kv_w, head_block=1, head_w=PAIR_W, head_gain=kv_norm)
    kvw_k = (kv_w[:, :PAIR_W] * kv_norm[:, None]).astype(bf)
    h, ksh, vsht, qbt, gbt = _attn_a(
        x, qt, k, vt, gt, _bias_vecs_a(a_rel_bias[0] * LOG2E), a_w_out[0].astype(bf), kvw_k, nwt)

    sinks = jnp.repeat((b_sinks[0].astype(jnp.float32) * LOG2E).reshape(N_PAIRS, 1, 2), LANES, axis=2)
    return _attn_b(h, qbt, ksh, vsht, gbt, _bias_vec_b(t5_bias * LOG2E), sinks, b_w_out[0].astype(bf),
                   final_norm[None, :])
```

```python
import functools
import math

import jax
import jax.numpy as jnp
from jax import lax
from jax.experimental import pallas as pl
from jax.experimental.pallas import tpu as pltpu

D_MODEL = 1024
HEAD_DIM = 64
N_HEADS = D_MODEL // HEAD_DIM
N_PAIRS = N_HEADS // 2
PAIR_W = 2 * HEAD_DIM
CHUNK = 64
RMS_EPS = 1e-6
A_LEFT_CHUNKS = 8
A_REL_CLIP = 256
B_KV_HEADS = 2
B_LEFT_CHUNKS = 2
T5_BUCKETS = 32
T5_MAX_DIST = 128

TQ = 256
A_KBLOCKS = A_LEFT_CHUNKS * CHUNK // TQ + 1
B_PREV = B_LEFT_CHUNKS * CHUNK
TM_PROJ = 1024
FEAT_CHUNK = 256
W_T_COLS_A = 1024
W_T_COLS_B = 512
MASKED = -1e30
BIAS_PERIOD = 2 * TQ
SUBLANES = 8
LANES = 128
MAX_CHAINS = 2
BF16_ROWS = 16
LOG2E = math.log2(math.e)
N_SCORE_BUFS = 4
N_SCORE_BUFS_B = 8
VMEM_LIMIT = 56 * 1024 * 1024

_NT = (((1,), (1,)), ((), ()))
_TN = (((0,), (0,)), ((), ()))


def _rms_scale(xf):
    return lax.rsqrt(jnp.mean(xf * xf, axis=-1, keepdims=True) + RMS_EPS)


def _silu(v):
    return v * jax.nn.sigmoid(v)


def _weight_t_kernel(*refs, has_head):
    if has_head:
        head_ref, hgain_ref, w_ref, cs_ref, gain_ref, o_ref = refs
        j = pl.program_id(0)

        @pl.when(j == 0)
        def _():
            pad = o_ref.shape[0] - head_ref.shape[1]
            o_ref[:pad, :] = jnp.zeros((pad, o_ref.shape[1]), jnp.bfloat16)
            o_ref[pad:, :] = (head_ref[...].T * hgain_ref[...]).astype(jnp.bfloat16)

        @pl.when(j > 0)
        def _():
            o_ref[...] = ((w_ref[...] * cs_ref[...]).T * gain_ref[...]).astype(jnp.bfloat16)
    else:
        w_ref, cs_ref, gain_ref, o_ref = refs
        o_ref[...] = ((w_ref[...] * cs_ref[...]).T * gain_ref[...]).astype(jnp.bfloat16)


def _weight_t(w, col_scale, row_gain, tc, head=None, head_block=0, head_w=None, head_gain=None):
    d, n = w.shape
    off = 0 if head is None else 1
    main_specs = [
        pl.BlockSpec((d, tc), lambda j: (0, jnp.maximum(j - off, 0))),
        pl.BlockSpec((1, tc), lambda j: (0, jnp.maximum(j - off, 0))),
        pl.BlockSpec((1, d), lambda j: (0, 0)),
    ]
    head_specs, head_args = [], []
    if head is not None:
        assert head.shape[0] == d and head.shape[1] % head_w == 0 and head_w <= tc
        head_specs = [pl.BlockSpec((d, head_w), lambda j: (0, head_block)), pl.BlockSpec((1, d), lambda j: (0, 0))]
        head_args = [head, head_gain[None, :]]
    return pl.pallas_call(
        functools.partial(_weight_t_kernel, has_head=head is not None),
        grid=(n // tc + off,),
        in_specs=head_specs + main_specs,
        out_specs=pl.BlockSpec((tc, d), lambda j: (j, 0)),
        out_shape=jax.ShapeDtypeStruct((n + off * tc, d), jnp.bfloat16),
        compiler_params=pltpu.CompilerParams(dimension_semantics=("arbitrary",), vmem_limit_bytes=VMEM_LIMIT),
        name="weight_t",
    )(*head_args, w, col_scale[None, :], row_gain[None, :])


def _proj_a_kernel(x_ref, wt_ref, qt_ref, k_ref, vt_ref, gt_ref):
    xf = x_ref[0]
    xn = (xf * _rms_scale(xf)).astype(jnp.bfloat16)
    k = lax.dot_general(xn, wt_ref[D_MODEL:2 * D_MODEL, :], _NT,
                        preferred_element_type=jnp.float32)
    for p in range(N_PAIRS):
        k_ref[0, p] = k[:, p * PAIR_W:(p + 1) * PAIR_W].astype(jnp.bfloat16)
    for out_ref, base, act in ((gt_ref, 3 * D_MODEL, _silu), (vt_ref, 2 * D_MODEL, None), (qt_ref, 0, None)):
        for c in range(0, D_MODEL, FEAT_CHUNK):
            w = wt_ref[base + c:base + c + FEAT_CHUNK, :]
            yt = lax.dot_general(w, xn, _NT, preferred_element_type=jnp.float32)
            out_ref[0, c:c + FEAT_CHUNK, :] = (yt if act is None else act(yt)).astype(jnp.bfloat16)


def _proj_a(x, wt):
    b, s, d = x.shape
    feat = jax.ShapeDtypeStruct((b, d, s), jnp.bfloat16)
    return pl.pallas_call(
        _proj_a_kernel,
        grid=(b, s // TM_PROJ),
        in_specs=[
            pl.BlockSpec((1, TM_PROJ, d), lambda bi, i: (bi, i, 0)),
            _const_spec(wt.shape),
        ],
        out_specs=[
            pl.BlockSpec((1, d, TM_PROJ), lambda bi, i: (bi, 0, i)),
            pl.BlockSpec((1, N_PAIRS, TM_PROJ, PAIR_W), lambda bi, i: (bi, 0, i, 0)),
            pl.BlockSpec((1, d, TM_PROJ), lambda bi, i: (bi, 0, i)),
            pl.BlockSpec((1, d, TM_PROJ), lambda bi, i: (bi, 0, i)),
        ],
        out_shape=[feat, jax.ShapeDtypeStruct((b, N_PAIRS, s, PAIR_W), jnp.bfloat16), feat, feat],
        compiler_params=pltpu.CompilerParams(
            dimension_semantics=("arbitrary", "arbitrary"), vmem_limit_bytes=VMEM_LIMIT),
        name="proj_a",
    )(x, wt)


def _chunk_iota(shape, axis):
    return lax.shift_right_logical(lax.broadcasted_iota(jnp.int32, shape, axis), CHUNK.bit_length() - 1)


def _identity_bf16(n):
    return (lax.broadcasted_iota(jnp.int32, (n, n), 0) == lax.broadcasted_iota(jnp.int32, (n, n), 1)).astype(jnp.bfloat16)


def _toeplitz(vec_row, n_keys):
    return pltpu.roll(jnp.broadcast_to(vec_row, (n_keys, BIAS_PERIOD)), 0, 1, stride=1, stride_axis=0)


def _fill_bias_slabs(vec_ref, bias_ref, n_keys, bands):
    n = len(bands)
    kch = _chunk_iota((n_keys, TQ), 0)
    qch = _chunk_iota((n_keys, TQ), 1)

    def head_body(h, carry):
        for j, band in enumerate(bands):
            t = _toeplitz(vec_ref[j, h], n_keys)[:, :TQ]
            if band is not None:
                t = jnp.where(band(kch, qch), t, MASKED)
            bias_ref[h * n + j] = t
        return carry

    lax.fori_loop(0, N_HEADS, head_body, 0)


def _live_rows(band, n_keys):
    chunks_per_half = LANES // CHUNK
    out = []
    for half in range(TQ // LANES):
        qchs = range(half * chunks_per_half, (half + 1) * chunks_per_half)
        live = [kc for kc in range(n_keys // CHUNK) if band is None or any(band(kc, qc) for qc in qchs)]
        out.append((live[0] * CHUNK, (live[-1] + 1) * CHUNK) if live else (0, 0))
    return tuple(out)


def _scores_stage(k_tiles, q_tiles, bias_fns, live, s_ref):
    maxes, row = [], 0
    for half in range(TQ // LANES):
        maxes.append([None] * MAX_CHAINS)
    for kt, qz, bias_fn, live_j in zip(k_tiles, q_tiles, bias_fns, live):
        s = jnp.dot(kt, qz, preferred_element_type=jnp.float32)
        for half, (r0, r1) in enumerate(live_j):
            lanes = slice(half * LANES, (half + 1) * LANES)
            acc = maxes[half]
            for r in range(r0 // SUBLANES, r1 // SUBLANES):
                rows = slice(r * SUBLANES, (r + 1) * SUBLANES)
                grp = s[rows, lanes] if bias_fn is None else s[rows, lanes] + bias_fn(rows, lanes)
                s_ref[row + r * SUBLANES:row + (r + 1) * SUBLANES, lanes] = grp
                c = r % MAX_CHAINS
                acc[c] = grp if acc[c] is None else jnp.maximum(acc[c], grp)
        row += s.shape[0]
    cols = [jnp.max(functools.reduce(jnp.maximum, [a for a in acc if a is not None]), axis=0, keepdims=True)
            for acc in maxes]
    return jnp.concatenate(cols, axis=1)


def _pv_stage(s_ref, m, v_tiles, live, extra_logit=None):
    if extra_logit is not None:
        m = jnp.maximum(m, extra_logit)
    acc, row = None, 0
    for vt, live_j in zip(v_tiles, live):
        n = vt.shape[1]
        halves = []
        for half, (r0, r1) in enumerate(live_j):
            lanes = slice(half * LANES, (half + 1) * LANES)
            parts = [jnp.zeros((r0, LANES), jnp.bfloat16)] if r0 else []
            if r1 > r0:
                parts.append(jnp.exp2(s_ref[row + r0:row + r1, lanes] - m[:, lanes]).astype(jnp.bfloat16))
            if n > r1:
                parts.append(jnp.zeros((n - r1, LANES), jnp.bfloat16))
            halves.append(parts[0] if len(parts) == 1 else jnp.concatenate(parts, axis=0))
        p = jnp.concatenate(halves, axis=1)
        v_ones = jnp.concatenate([vt, jnp.ones((BF16_ROWS, n), vt.dtype)], axis=0)
        part = jnp.dot(v_ones, p, preferred_element_type=jnp.float32)
        acc = part if acc is None else acc + part
        row += n
    l = acc[HEAD_DIM:HEAD_DIM + 1, :]
    if extra_logit is not None:
        l = l + jnp.exp2(extra_logit - m)
    return acc[:HEAD_DIM, :] / l


def _head_pipeline(n_tiles, scores_fn, pv_fn, s_refs, scores_first):
    per_group = len(s_refs) // 2
    x_refs, y_refs = s_refs[:per_group], s_refs[per_group:]
    n_groups = n_tiles // per_group

    m = [scores_fn(t // 2, t % 2, ref) for t, ref in zip(range(per_group), x_refs)]
    for g in range(n_groups):
        src, dst = (x_refs, y_refs) if g % 2 == 0 else (y_refs, x_refs)
        nxt = [(g + 1) * per_group + j for j in range(per_group)] if g + 1 < n_groups else []
        m_next = [scores_fn(t // 2, t % 2, ref) for t, ref in zip(nxt, dst)] if scores_first else []
        for j in range(per_group):
            if not scores_first and nxt:
                m_next.append(scores_fn(nxt[j] // 2, nxt[j] % 2, dst[j]))
            t = g * per_group + j
            pv_fn(t // 2, t % 2, src[j], m[j])
        m = m_next


A_BANDS = (lambda kc, qc: kc >= qc,
           None,
           lambda kc, qc: kc <= qc)
A_LIVE = tuple(_live_rows(band, TQ) for band in A_BANDS)
B_WIN = B_PREV + LANES
B_SUB = PAIR_W
B_BAND = lambda kc, qc: (kc >= qc) & (kc <= qc + B_LEFT_CHUNKS)
B_LIVE = (((0, B_WIN),) * (TQ // LANES),)


def _out_and_residual(zt_ref, wout_ref, res):
    y = lax.dot_general(zt_ref[...], wout_ref[...], _TN, preferred_element_type=jnp.float32)
    return res + y


def _attn_a_kernel(x_ref, qt_ref, k0_ref, k1_ref, k2_ref, v0_ref, v1_ref, v2_ref, gt_ref,
                   bvec_ref, wout_ref, kvwk_ref, nwt_ref,
                   h_ref, ksh_ref, vsht_ref, qbt_ref, gbt_ref, zt_ref, bias_ref, *s_refs):
    i = pl.program_id(1)

    @pl.when((pl.program_id(0) == 0) & (i == 0))
    def _():
        _fill_bias_slabs(bvec_ref, bias_ref, TQ, A_BANDS)
        bias_ref[N_HEADS * A_KBLOCKS] = jnp.full((TQ, TQ), MASKED, jnp.float32)

    k_refs = (k0_ref, k1_ref, k2_ref)
    v_refs = (v0_ref, v1_ref, v2_ref)
    n_slabs = N_HEADS * A_KBLOCKS
    rows = lax.broadcasted_iota(jnp.int32, (PAIR_W, TQ), 0)

    def scores_fn(p, hh, s_ref):
        q2 = qt_ref[0, p]
        own_rows = rows < HEAD_DIM if hh == 0 else rows >= HEAD_DIM
        qz = jnp.where(own_rows, q2, jnp.zeros_like(q2))
        k_tiles, bias_fns = [], []
        for j in range(A_KBLOCKS):
            in_seq = i - (A_KBLOCKS - 1) + j >= 0
            slab = jnp.where(in_seq, (2 * p + hh) * A_KBLOCKS + j, n_slabs)
            k_tiles.append(k_refs[j][0, p])
            bias_fns.append(lambda rws, lanes, slab=slab: bias_ref[slab, rws, lanes])
        return _scores_stage(k_tiles, [qz] * A_KBLOCKS, bias_fns, A_LIVE, s_ref)

    def pv_fn(p, hh, s_ref, m):
        half = slice(hh * HEAD_DIM, (hh + 1) * HEAD_DIM)
        o = _pv_stage(s_ref, m, [v_ref[0, p, half, :] for v_ref in v_refs], A_LIVE)
        gate = gt_ref[0, p, half, :].astype(jnp.float32)
        row0 = p * PAIR_W + hh * HEAD_DIM
        zt_ref[row0:row0 + HEAD_DIM, :] = (o * gate).astype(jnp.bfloat16)

    _head_pipeline(N_HEADS, scores_fn, pv_fn, s_refs, scores_first=True)

    h = _out_and_residual(zt_ref, wout_ref, x_ref[0])
    h_ref[0] = h
    hn = (h * _rms_scale(h)).astype(jnp.bfloat16)
    ksh_ref[0] = jnp.dot(hn, kvwk_ref[...], preferred_element_type=jnp.float32).astype(jnp.bfloat16)
    for out_ref, base, act in ((qbt_ref, W_T_COLS_B, None), (gbt_ref, W_T_COLS_B + D_MODEL, _silu)):
        for c in range(0, D_MODEL, FEAT_CHUNK):
            first = out_ref is qbt_ref and c == 0
            start = base + c - (PAIR_W if first else 0)
            yt = lax.dot_general(nwt_ref[start:base + c + FEAT_CHUNK, :], hn, _NT,
                                 preferred_element_type=jnp.float32)
            if first:
                vsht_ref[0] = yt[:PAIR_W].astype(jnp.bfloat16)
                yt = yt[PAIR_W:]
            out_ref[0, c:c + FEAT_CHUNK, :] = (yt if act is None else act(yt)).astype(jnp.bfloat16)


def _const_spec(shape):
    return pl.BlockSpec(shape, lambda bi, i: (0,) * len(shape), pipeline_mode=pl.Buffered(1))


def _attn_a(x, qt, k, vt, gt, bvec, wout, kvwk, nwt):
    b, s, d = x.shape
    qt4 = qt.reshape(b, N_PAIRS, PAIR_W, s)
    vt4 = vt.reshape(b, N_PAIRS, PAIR_W, s)
    gt4 = gt.reshape(b, N_PAIRS, PAIR_W, s)
    back = A_KBLOCKS - 1

    def kspec(j):
        return pl.BlockSpec((1, N_PAIRS, TQ, PAIR_W),
                            lambda bi, i: (bi, 0, jnp.maximum(i - back + j, 0), 0))

    def vspec(j):
        return pl.BlockSpec((1, N_PAIRS, PAIR_W, TQ),
                            lambda bi, i: (bi, 0, 0, jnp.maximum(i - back + j, 0)))

    feat_spec = pl.BlockSpec((1, N_PAIRS, PAIR_W, TQ), lambda bi, i: (bi, 0, 0, i))
    featout_spec = pl.BlockSpec((1, d, TQ), lambda bi, i: (bi, 0, i))
    feat = jax.ShapeDtypeStruct((b, d, s), jnp.bfloat16)
    return pl.pallas_call(
        _attn_a_kernel,
        grid=(b, s // TQ),
        in_specs=[
            pl.BlockSpec((1, TQ, d), lambda bi, i: (bi, i, 0)),
            feat_spec,
            kspec(0), kspec(1), kspec(2),
            vspec(0), vspec(1), vspec(2),
            feat_spec,
            _const_spec(bvec.shape),
            _const_spec(wout.shape),
            _const_spec(kvwk.shape),
            _const_spec(nwt.shape),
        ],
        out_specs=[
            pl.BlockSpec((1, TQ, d), lambda bi, i: (bi, i, 0)),
            pl.BlockSpec((1, TQ, PAIR_W), lambda bi, i: (bi, i, 0)),
            pl.BlockSpec((1, PAIR_W, TQ), lambda bi, i: (bi, 0, i)),
            featout_spec,
            featout_spec,
        ],
        out_shape=[
            jax.ShapeDtypeStruct((b, s, d), jnp.float32),
            jax.ShapeDtypeStruct((b, s, PAIR_W), jnp.bfloat16),
            jax.ShapeDtypeStruct((b, PAIR_W, s), jnp.bfloat16),
            feat, feat,
        ],
        scratch_shapes=[pltpu.VMEM((d, TQ), jnp.bfloat16),
                        pltpu.VMEM((N_HEADS * A_KBLOCKS + 1, TQ, TQ), jnp.float32),
                        *[pltpu.VMEM((A_KBLOCKS * TQ, TQ), jnp.float32)] * N_SCORE_BUFS],
        compiler_params=pltpu.CompilerParams(
            dimension_semantics=("arbitrary", "arbitrary"), vmem_limit_bytes=VMEM_LIMIT),
        name="attn_a",
    )(x, qt4, k, k, k, vt4, vt4, vt4, gt4, bvec, wout, kvwk, nwt)


def _attn_b_kernel(h_ref, qt_ref, kp_ref, ko_ref, vp_ref, vo_ref, gt_ref, vec_ref,
                   sink_ref, wout_ref, fg_ref, out_ref, zt_ref, bias_ref, *s_refs):
    i = pl.program_id(1)

    @pl.when((pl.program_id(0) == 0) & (i == 0))
    def _():
        in_band = B_BAND(_chunk_iota((B_WIN, LANES), 0), _chunk_iota((B_WIN, LANES), 1))
        before_block = lax.broadcasted_iota(jnp.int32, (B_WIN, LANES), 0) < B_PREV

        def pair_body(p, carry):
            for hh in range(2):
                t = jnp.where(in_band, _toeplitz(vec_ref[0, 2 * p + hh], B_WIN)[:, :LANES], MASKED)
                lanes = slice(hh * LANES, (hh + 1) * LANES)
                bias_ref[2 * p, :, lanes] = t.astype(jnp.bfloat16)
                bias_ref[2 * p + 1, :, lanes] = jnp.where(before_block, MASKED, t).astype(jnp.bfloat16)
            return carry

        lax.fori_loop(0, N_PAIRS, pair_body, 0)

    rows = lax.broadcasted_iota(jnp.int32, (PAIR_W, TQ), 0)
    pairs_per_kv = N_PAIRS // B_KV_HEADS
    eye = _identity_bf16(B_SUB)

    def scores_fn(p, u, s_ref):
        qlanes = slice(u * LANES, (u + 1) * LANES)
        q2 = jnp.concatenate([qt_ref[0, p, :HEAD_DIM, qlanes], qt_ref[0, p, HEAD_DIM:, qlanes]], axis=1)
        kv0 = HEAD_DIM * (p // pairs_per_kv)
        kv_rows = (rows >= kv0) & (rows < kv0 + HEAD_DIM)
        qz = jnp.where(kv_rows, jnp.concatenate([q2, q2], axis=0), jnp.zeros((PAIR_W, TQ), q2.dtype))
        if u == 0:
            k_tile = jnp.concatenate([kp_ref[0, TQ - B_PREV:, :], ko_ref[0, :LANES, :]], axis=0)
            variant = jnp.where(i == 0, 1, 0)
        else:
            k_tile, variant = ko_ref[0], 0

        lhs, rhs = [], []
        for r0 in range(0, B_WIN, B_SUB):
            lhs.append(jnp.concatenate([k_tile[r0:r0 + B_SUB, :], eye], axis=1))
            rhs.append(jnp.concatenate([qz, bias_ref[2 * p + variant, r0:r0 + B_SUB, :]], axis=0))
        sub_live = (((0, B_SUB),) * (TQ // LANES),) * len(lhs)
        return _scores_stage(lhs, rhs, [None] * len(lhs), sub_live, s_ref)

    def pv_fn(p, u, s_ref, m):
        qlanes = slice(u * LANES, (u + 1) * LANES)
        kv_rows = slice(HEAD_DIM * (p // pairs_per_kv), HEAD_DIM * (p // pairs_per_kv + 1))
        if u == 0:
            v_tile = jnp.concatenate([vp_ref[0, kv_rows, TQ - B_PREV:], vo_ref[0, kv_rows, :LANES]], axis=1)
        else:
            v_tile = vo_ref[0, kv_rows, :]
        o = _pv_stage(s_ref, m, [v_tile], B_LIVE, extra_logit=sink_ref[p])
        for hh in range(2):
            hrows = slice(hh * HEAD_DIM, (hh + 1) * HEAD_DIM)
            gate = gt_ref[0, p, hrows, qlanes].astype(jnp.float32)
            row0 = p * PAIR_W + hh * HEAD_DIM
            zt_ref[row0:row0 + HEAD_DIM, qlanes] = (o[:, hh * LANES:(hh + 1) * LANES] * gate).astype(jnp.bfloat16)

    _head_pipeline(N_HEADS, scores_fn, pv_fn, s_refs, scores_first=False)

    h2 = _out_and_residual(zt_ref, wout_ref, h_ref[0])
    out_ref[0] = h2 * _rms_scale(h2) * fg_ref[...]


def _attn_b(h, qbt, ksh, vsht, gbt, vec, sinks, wout, fg):
    b, s, d = h.shape
    qt4 = qbt.reshape(b, N_PAIRS, PAIR_W, s)
    gt4 = gbt.reshape(b, N_PAIRS, PAIR_W, s)
    feat_spec = pl.BlockSpec((1, N_PAIRS, PAIR_W, TQ), lambda bi, i: (bi, 0, 0, i))
    return pl.pallas_call(
        _attn_b_kernel,
        grid=(b, s // TQ),
        in_specs=[
            pl.BlockSpec((1, TQ, d), lambda bi, i: (bi, i, 0)),
            feat_spec,
            pl.BlockSpec((1, TQ, PAIR_W), lambda bi, i: (bi, jnp.maximum(i - 1, 0), 0)),
            pl.BlockSpec((1, TQ, PAIR_W), lambda bi, i: (bi, i, 0)),
            pl.BlockSpec((1, PAIR_W, TQ), lambda bi, i: (bi, 0, jnp.maximum(i - 1, 0))),
            pl.BlockSpec((1, PAIR_W, TQ), lambda bi, i: (bi, 0, i)),
            feat_spec,
            _const_spec(vec.shape),
            _const_spec(sinks.shape),
            _const_spec(wout.shape),
            _const_spec(fg.shape),
        ],
        out_specs=pl.BlockSpec((1, TQ, d), lambda bi, i: (bi, i, 0)),
        out_shape=jax.ShapeDtypeStruct((b, s, d), jnp.float32),
        scratch_shapes=[pltpu.VMEM((d, TQ), jnp.bfloat16),
                        pltpu.VMEM((2 * N_PAIRS, B_WIN, TQ), jnp.bfloat16),
                        *[pltpu.VMEM((B_WIN, TQ), jnp.float32)] * N_SCORE_BUFS_B],
        compiler_params=pltpu.CompilerParams(
            dimension_semantics=("arbitrary", "arbitrary"), vmem_limit_bytes=VMEM_LIMIT),
        name="attn_b",
    )(h, qt4, ksh, ksh, vsht, vsht, gt4, vec, sinks, wout, fg)


def _slab_diff():
    u = jnp.arange(BIAS_PERIOD)
    return jnp.where(u < TQ, u, u - BIAS_PERIOD)


def _bias_vecs_a(rel_bias):
    back = (A_KBLOCKS - 1 - jnp.arange(A_KBLOCKS))[:, None] * TQ
    dist = back + _slab_diff()[None, :]
    idx = jnp.clip(dist, -A_REL_CLIP, A_REL_CLIP) + A_REL_CLIP
    vec = jnp.transpose(rel_bias[idx], (0, 2, 1)).astype(jnp.float32)
    return vec[:, :, None, :]


def _t5_bucket(rel):
    nb = T5_BUCKETS // 2
    max_exact = nb // 2
    ret = jnp.where(rel > 0, nb, 0)
    n = jnp.abs(rel)
    nf = jnp.maximum(n, 1).astype(jnp.float32)
    large = max_exact + (jnp.log(nf / max_exact) / math.log(T5_MAX_DIST / max_exact)
                         * (nb - max_exact)).astype(jnp.int32)
    large = jnp.minimum(large, nb - 1)
    return ret + jnp.where(n < max_exact, n, large)


def _bias_vec_b(t5_table):
    rel = -_slab_diff() - B_PREV
    vec = jnp.transpose(t5_table[_t5_bucket(rel)], (1, 0)).astype(jnp.float32)
    return vec[None, :, None, :]


def kernel(x, a_norm, a_w_in, a_rel_bias, a_w_out, kv_norm, kv_w, t5_bias,
           b_norm, b_w_in, b_sinks, b_w_out, final_norm):
    assert a_norm.shape[0] == 1 and b_norm.shape[0] == 1, "one A layer then one B layer"
    bf = jnp.bfloat16
    scale = HEAD_DIM ** -0.5 * LOG2E

    def q_col_scale(n):
        return jnp.where(jnp.arange(n) < D_MODEL, scale, 1.0).astype(jnp.float32)

    wt_a = _weight_t(a_w_in[0], q_col_scale(a_w_in.shape[2]), a_norm[0], W_T_COLS_A)
    qt, k, vt, gt = _proj_a(x, wt_a)

    nwt = _weight_t(b_w_in[0], q_col_scale(b_w_in.shape[2]), b_norm[0], W_T_COLS_B,
                    head=kv_w, head_block=1, head_w=PAIR_W, head_gain=kv_norm)
    kvw_k = (kv_w[:, :PAIR_W] * kv_norm[:, None]).astype(bf)
    h, ksh, vsht, qbt, gbt = _attn_a(
        x, qt, k, vt, gt, _bias_vecs_a(a_rel_bias[0] * LOG2E), a_w_out[0].astype(bf), kvw_k, nwt)

    sinks = jnp.repeat((b_sinks[0].astype(jnp.float32) * LOG2E).reshape(N_PAIRS, 1, 2), LANES, axis=2)
    return _attn_b(h, qbt, ksh, vsht, gbt, _bias_vec_b(t5_bias * LOG2E), sinks, b_w_out[0].astype(bf),
                   final_norm[None, :])
```

```python
import functools
import math

import jax
import jax.numpy as jnp
from jax import lax
from jax.experimental import pallas as pl
from jax.experimental.pallas import tpu as pltpu

D_MODEL = 1024
HEAD_DIM = 64
N_HEADS = D_MODEL // HEAD_DIM
N_PAIRS = N_HEADS // 2
PAIR_W = 2 * HEAD_DIM
CHUNK = 64
RMS_EPS = 1e-6
A_LEFT_CHUNKS = 8
A_REL_CLIP = 256
B_KV_HEADS = 2
B_LEFT_CHUNKS = 2
T5_BUCKETS = 32
T5_MAX_DIST = 128

TQ = 256
A_KBLOCKS = A_LEFT_CHUNKS * CHUNK // TQ + 1
B_PREV = B_LEFT_CHUNKS * CHUNK
TM_PROJ = 1024
FEAT_CHUNK = 256
W_T_COLS_A = 1024
W_T_COLS_B = 512
MASKED = -1e30
BIAS_PERIOD = 2 * TQ
SUBLANES = 8
LANES = 128
MAX_CHAINS = 2
BF16_ROWS = 16
LOG2E = math.log2(math.e)
N_SCORE_BUFS = 4
N_SCORE_BUFS_B = 8
VMEM_LIMIT = 56 * 1024 * 1024

_NT = (((1,), (1,)), ((), ()))
_TN = (((0,), (0,)), ((), ()))


def _rms_scale(xf):
    return lax.rsqrt(jnp.mean(xf * xf, axis=-1, keepdims=True) + RMS_EPS)


def _token_scales(xf):
    col = _rms_scale(xf)
    row = jnp.transpose(jnp.broadcast_to(col, (xf.shape[0], LANES)))[0:1, :]
    return col, row


def _silu(v):
    return v * jax.nn.sigmoid(v)


def _weight_t_kernel(*refs, has_head):
    if has_head:
        head_ref, hgain_ref, w_ref, cs_ref, gain_ref, o_ref = refs
        j = pl.program_id(0)

        @pl.when(j == 0)
        def _():
            pad = o_ref.shape[0] - head_ref.shape[1]
            o_ref[:pad, :] = jnp.zeros((pad, o_ref.shape[1]), jnp.bfloat16)
            o_ref[pad:, :] = (head_ref[...].T * hgain_ref[...]).astype(jnp.bfloat16)

        @pl.when(j > 0)
        def _():
            o_ref[...] = ((w_ref[...] * cs_ref[...]).T * gain_ref[...]).astype(jnp.bfloat16)
    else:
        w_ref, cs_ref, gain_ref, o_ref = refs
        o_ref[...] = ((w_ref[...] * cs_ref[...]).T * gain_ref[...]).astype(jnp.bfloat16)


def _weight_t(w, col_scale, row_gain, tc, head=None, head_block=0, head_w=None, head_gain=None):
    d, n = w.shape
    off = 0 if head is None else 1
    main_specs = [
        pl.BlockSpec((d, tc), lambda j: (0, jnp.maximum(j - off, 0))),
        pl.BlockSpec((1, tc), lambda j: (0, jnp.maximum(j - off, 0))),
        pl.BlockSpec((1, d), lambda j: (0, 0)),
    ]
    head_specs, head_args = [], []
    if head is not None:
        assert head.shape[0] == d and head.shape[1] % head_w == 0 and head_w <= tc
        head_specs = [pl.BlockSpec((d, head_w), lambda j: (0, head_block)), pl.BlockSpec((1, d), lambda j: (0, 0))]
        head_args = [head, head_gain[None, :]]
    return pl.pallas_call(
        functools.partial(_weight_t_kernel, has_head=head is not None),
        grid=(n // tc + off,),
        in_specs=head_specs + main_specs,
        out_specs=pl.BlockSpec((tc, d), lambda j: (j, 0)),
        out_shape=jax.ShapeDtypeStruct((n + off * tc, d), jnp.bfloat16),
        compiler_params=pltpu.CompilerParams(dimension_semantics=("arbitrary",), vmem_limit_bytes=VMEM_LIMIT),
        name="weight_t",
    )(*head_args, w, col_scale[None, :], row_gain[None, :])


def _proj_a_kernel(x_ref, wt_ref, qt_ref, k_ref, vt_ref, gt_ref):
    xf = x_ref[0]
    xb = xf.astype(jnp.bfloat16)
    r_col, r_row = _token_scales(xf)
    k = lax.dot_general(xb, wt_ref[D_MODEL:2 * D_MODEL, :], _NT,
                        preferred_element_type=jnp.float32) * r_col
    for p in range(N_PAIRS):
        k_ref[0, p] = k[:, p * PAIR_W:(p + 1) * PAIR_W].astype(jnp.bfloat16)
    for out_ref, base, act in ((gt_ref, 3 * D_MODEL, _silu), (vt_ref, 2 * D_MODEL, None), (qt_ref, 0, None)):
        for c in range(0, D_MODEL, FEAT_CHUNK):
            w = wt_ref[base + c:base + c + FEAT_CHUNK, :]
            yt = lax.dot_general(w, xb, _NT, preferred_element_type=jnp.float32) * r_row
            out_ref[0, c:c + FEAT_CHUNK, :] = (yt if act is None else act(yt)).astype(jnp.bfloat16)


def _proj_a(x, wt):
    b, s, d = x.shape
    feat = jax.ShapeDtypeStruct((b, d, s), jnp.bfloat16)
    return pl.pallas_call(
        _proj_a_kernel,
        grid=(b, s // TM_PROJ),
        in_specs=[
            pl.BlockSpec((1, TM_PROJ, d), lambda bi, i: (bi, i, 0)),
            _const_spec(wt.shape),
        ],
        out_specs=[
            pl.BlockSpec((1, d, TM_PROJ), lambda bi, i: (bi, 0, i)),
            pl.BlockSpec((1, N_PAIRS, TM_PROJ, PAIR_W), lambda bi, i: (bi, 0, i, 0)),
            pl.BlockSpec((1, d, TM_PROJ), lambda bi, i: (bi, 0, i)),
            pl.BlockSpec((1, d, TM_PROJ), lambda bi, i: (bi, 0, i)),
        ],
        out_shape=[feat, jax.ShapeDtypeStruct((b, N_PAIRS, s, PAIR_W), jnp.bfloat16), feat, feat],
        compiler_params=pltpu.CompilerParams(
            dimension_semantics=("arbitrary", "arbitrary"), vmem_limit_bytes=VMEM_LIMIT),
        name="proj_a",
    )(x, wt)


def _chunk_iota(shape, axis):
    return lax.shift_right_logical(lax.broadcasted_iota(jnp.int32, shape, axis), CHUNK.bit_length() - 1)


def _identity_bf16(n):
    return (lax.broadcasted_iota(jnp.int32, (n, n), 0) == lax.broadcasted_iota(jnp.int32, (n, n), 1)).astype(jnp.bfloat16)


def _toeplitz(vec_row, n_keys):
    return pltpu.roll(jnp.broadcast_to(vec_row, (n_keys, BIAS_PERIOD)), 0, 1, stride=1, stride_axis=0)


def _fill_bias_slabs(vec_ref, bias_ref, n_keys, bands):
    n = len(bands)
    kch = _chunk_iota((n_keys, TQ), 0)
    qch = _chunk_iota((n_keys, TQ), 1)

    def head_body(h, carry):
        for j, band in enumerate(bands):
            t = _toeplitz(vec_ref[j, h], n_keys)[:, :TQ]
            if band is not None:
                t = jnp.where(band(kch, qch), t, MASKED)
            bias_ref[h * n + j] = t
        return carry

    lax.fori_loop(0, N_HEADS, head_body, 0)


def _live_rows(band, n_keys):
    chunks_per_half = LANES // CHUNK
    out = []
    for half in range(TQ // LANES):
        qchs = range(half * chunks_per_half, (half + 1) * chunks_per_half)
        live = [kc for kc in range(n_keys // CHUNK) if band is None or any(band(kc, qc) for qc in qchs)]
        out.append((live[0] * CHUNK, (live[-1] + 1) * CHUNK) if live else (0, 0))
    return tuple(out)


def _scores_stage(k_tiles, q_tiles, bias_fns, live, s_ref):
    maxes, row = [], 0
    for half in range(TQ // LANES):
        maxes.append([None] * MAX_CHAINS)
    for kt, qz, bias_fn, live_j in zip(k_tiles, q_tiles, bias_fns, live):
        s = jnp.dot(kt, qz, preferred_element_type=jnp.float32)
        for half, (r0, r1) in enumerate(live_j):
            lanes = slice(half * LANES, (half + 1) * LANES)
            acc = maxes[half]
            for r in range(r0 // SUBLANES, r1 // SUBLANES):
                rows = slice(r * SUBLANES, (r + 1) * SUBLANES)
                grp = s[rows, lanes] if bias_fn is None else s[rows, lanes] + bias_fn(rows, lanes)
                s_ref[row + r * SUBLANES:row + (r + 1) * SUBLANES, lanes] = grp
                c = r % MAX_CHAINS
                acc[c] = grp if acc[c] is None else jnp.maximum(acc[c], grp)
        row += s.shape[0]
    cols = [jnp.max(functools.reduce(jnp.maximum, [a for a in acc if a is not None]), axis=0, keepdims=True)
            for acc in maxes]
    return jnp.concatenate(cols, axis=1)


def _pv_stage(s_ref, m, v_tiles, live, extra_logit=None):
    if extra_logit is not None:
        m = jnp.maximum(m, extra_logit)
    acc, row = None, 0
    for vt, live_j in zip(v_tiles, live):
        n = vt.shape[1]
        halves = []
        for half, (r0, r1) in enumerate(live_j):
            lanes = slice(half * LANES, (half + 1) * LANES)
            parts = [jnp.zeros((r0, LANES), jnp.bfloat16)] if r0 else []
            if r1 > r0:
                parts.append(jnp.exp2(s_ref[row + r0:row + r1, lanes] - m[:, lanes]).astype(jnp.bfloat16))
            if n > r1:
                parts.append(jnp.zeros((n - r1, LANES), jnp.bfloat16))
            halves.append(parts[0] if len(parts) == 1 else jnp.concatenate(parts, axis=0))
        p = jnp.concatenate(halves, axis=1)
        v_ones = jnp.concatenate([vt, jnp.ones((BF16_ROWS, n), vt.dtype)], axis=0)
        part = jnp.dot(v_ones, p, preferred_element_type=jnp.float32)
        acc = part if acc is None else acc + part
        row += n
    l = acc[HEAD_DIM:HEAD_DIM + 1, :]
    if extra_logit is not None:
        l = l + jnp.exp2(extra_logit - m)
    return acc[:HEAD_DIM, :] / l


def _head_pipeline(n_tiles, scores_fn, pv_fn, s_refs, scores_first):
    per_group = len(s_refs) // 2
    x_refs, y_refs = s_refs[:per_group], s_refs[per_group:]
    n_groups = n_tiles // per_group

    m = [scores_fn(t // 2, t % 2, ref) for t, ref in zip(range(per_group), x_refs)]
    for g in range(n_groups):
        src, dst = (x_refs, y_refs) if g % 2 == 0 else (y_refs, x_refs)
        nxt = [(g + 1) * per_group + j for j in range(per_group)] if g + 1 < n_groups else []
        m_next = [scores_fn(t // 2, t % 2, ref) for t, ref in zip(nxt, dst)] if scores_first else []
        for j in range(per_group):
            if not scores_first and nxt:
                m_next.append(scores_fn(nxt[j] // 2, nxt[j] % 2, dst[j]))
            t = g * per_group + j
            pv_fn(t // 2, t % 2, src[j], m[j])
        m = m_next


A_BANDS = (lambda kc, qc: kc >= qc,
           None,
           lambda kc, qc: kc <= qc)
A_LIVE = tuple(_live_rows(band, TQ) for band in A_BANDS)
B_WIN = B_PREV + LANES
B_SUB = PAIR_W
B_BAND = lambda kc, qc: (kc >= qc) & (kc <= qc + B_LEFT_CHUNKS)
B_LIVE = (((0, B_WIN),) * (TQ // LANES),)


def _out_and_residual(zt_ref, wout_ref, res):
    y = lax.dot_general(zt_ref[...], wout_ref[...], _TN, preferred_element_type=jnp.float32)
    return res + y


def _attn_a_kernel(x_ref, qt_ref, k0_ref, k1_ref, k2_ref, v0_ref, v1_ref, v2_ref, gt_ref,
                   bvec_ref, wout_ref, kvwk_ref, nwt_ref,
                   h_ref, ksh_ref, vsht_ref, qbt_ref, gbt_ref, zt_ref, bias_ref, *s_refs):
    i = pl.program_id(1)

    @pl.when((pl.program_id(0) == 0) & (i == 0))
    def _():
        _fill_bias_slabs(bvec_ref, bias_ref, TQ, A_BANDS)
        bias_ref[N_HEADS * A_KBLOCKS] = jnp.full((TQ, TQ), MASKED, jnp.float32)

    k_refs = (k0_ref, k1_ref, k2_ref)
    v_refs = (v0_ref, v1_ref, v2_ref)
    n_slabs = N_HEADS * A_KBLOCKS
    rows = lax.broadcasted_iota(jnp.int32, (PAIR_W, TQ), 0)

    def scores_fn(p, hh, s_ref):
        q2 = qt_ref[0, p]
        own_rows = rows < HEAD_DIM if hh == 0 else rows >= HEAD_DIM
        qz = jnp.where(own_rows, q2, jnp.zeros_like(q2))
        k_tiles, bias_fns = [], []
        for j in range(A_KBLOCKS):
            in_seq = i - (A_KBLOCKS - 1) + j >= 0
            slab = jnp.where(in_seq, (2 * p + hh) * A_KBLOCKS + j, n_slabs)
            k_tiles.append(k_refs[j][0, p])
            bias_fns.append(lambda rws, lanes, slab=slab: bias_ref[slab, rws, lanes])
        return _scores_stage(k_tiles, [qz] * A_KBLOCKS, bias_fns, A_LIVE, s_ref)

    def pv_fn(p, hh, s_ref, m):
        half = slice(hh * HEAD_DIM, (hh + 1) * HEAD_DIM)
        o = _pv_stage(s_ref, m, [v_ref[0, p, half, :] for v_ref in v_refs], A_LIVE)
        gate = gt_ref[0, p, half, :].astype(jnp.float32)
        row0 = p * PAIR_W + hh * HEAD_DIM
        zt_ref[row0:row0 + HEAD_DIM, :] = (o * gate).astype(jnp.bfloat16)

    _head_pipeline(N_HEADS, scores_fn, pv_fn, s_refs, scores_first=True)

    h = _out_and_residual(zt_ref, wout_ref, x_ref[0])
    h_ref[0] = h
    hb = h.astype(jnp.bfloat16)
    r_col, r_row = _token_scales(h)
    ksh_ref[0] = (jnp.dot(hb, kvwk_ref[...], preferred_element_type=jnp.float32) * r_col).astype(jnp.bfloat16)
    for out_ref, base, act in ((qbt_ref, W_T_COLS_B, None), (gbt_ref, W_T_COLS_B + D_MODEL, _silu)):
        for c in range(0, D_MODEL, FEAT_CHUNK):
            first = out_ref is qbt_ref and c == 0
            start = base + c - (PAIR_W if first else 0)
            yt = lax.dot_general(nwt_ref[start:base + c + FEAT_CHUNK, :], hb, _NT,
                                 preferred_element_type=jnp.float32) * r_row
            if first:
                vsht_ref[0] = yt[:PAIR_W].astype(jnp.bfloat16)
                yt = yt[PAIR_W:]
            out_ref[0, c:c + FEAT_CHUNK, :] = (yt if act is None else act(yt)).astype(jnp.bfloat16)


def _const_spec(shape):
    return pl.BlockSpec(shape, lambda bi, i: (0,) * len(shape), pipeline_mode=pl.Buffered(1))


def _attn_a(x, qt, k, vt, gt, bvec, wout, kvwk, nwt):
    b, s, d = x.shape
    qt4 = qt.reshape(b, N_PAIRS, PAIR_W, s)
    vt4 = vt.reshape(b, N_PAIRS, PAIR_W, s)
    gt4 = gt.reshape(b, N_PAIRS, PAIR_W, s)
    back = A_KBLOCKS - 1

    def kspec(j):
        return pl.BlockSpec((1, N_PAIRS, TQ, PAIR_W),
                            lambda bi, i: (bi, 0, jnp.maximum(i - back + j, 0), 0))

    def vspec(j):
        return pl.BlockSpec((1, N_PAIRS, PAIR_W, TQ),
                            lambda bi, i: (bi, 0, 0, jnp.maximum(i - back + j, 0)))

    feat_spec = pl.BlockSpec((1, N_PAIRS, PAIR_W, TQ), lambda bi, i: (bi, 0, 0, i))
    featout_spec = pl.BlockSpec((1, d, TQ), lambda bi, i: (bi, 0, i))
    feat = jax.ShapeDtypeStruct((b, d, s), jnp.bfloat16)
    return pl.pallas_call(
        _attn_a_kernel,
        grid=(b, s // TQ),
        in_specs=[
            pl.BlockSpec((1, TQ, d), lambda bi, i: (bi, i, 0)),
            feat_spec,
            kspec(0), kspec(1), kspec(2),
            vspec(0), vspec(1), vspec(2),
            feat_spec,
            _const_spec(bvec.shape),
            _const_spec(wout.shape),
            _const_spec(kvwk.shape),
            _const_spec(nwt.shape),
        ],
        out_specs=[
            pl.BlockSpec((1, TQ, d), lambda bi, i: (bi, i, 0)),
            pl.BlockSpec((1, TQ, PAIR_W), lambda bi, i: (bi, i, 0)),
            pl.BlockSpec((1, PAIR_W, TQ), lambda bi, i: (bi, 0, i)),
            featout_spec,
            featout_spec,
        ],
        out_shape=[
            jax.ShapeDtypeStruct((b, s, d), jnp.float32),
            jax.ShapeDtypeStruct((b, s, PAIR_W), jnp.bfloat16),
            jax.ShapeDtypeStruct((b, PAIR_W, s), jnp.bfloat16),
            feat, feat,
        ],
        scratch_shapes=[pltpu.VMEM((d, TQ), jnp.bfloat16),
                        pltpu.VMEM((N_HEADS * A_KBLOCKS + 1, TQ, TQ), jnp.float32),
                        *[pltpu.VMEM((A_KBLOCKS * TQ, TQ), jnp.float32)] * N_SCORE_BUFS],
        compiler_params=pltpu.CompilerParams(
            dimension_semantics=("arbitrary", "arbitrary"), vmem_limit_bytes=VMEM_LIMIT),
        name="attn_a",
    )(x, qt4, k, k, k, vt4, vt4, vt4, gt4, bvec, wout, kvwk, nwt)


def _attn_b_kernel(h_ref, qt_ref, kp_ref, ko_ref, vp_ref, vo_ref, gt_ref, vec_ref,
                   sink_ref, wout_ref, fg_ref, out_ref, zt_ref, bias_ref, *s_refs):
    i = pl.program_id(1)

    @pl.when((pl.program_id(0) == 0) & (i == 0))
    def _():
        in_band = B_BAND(_chunk_iota((B_WIN, LANES), 0), _chunk_iota((B_WIN, LANES), 1))
        before_block = lax.broadcasted_iota(jnp.int32, (B_WIN, LANES), 0) < B_PREV

        def pair_body(p, carry):
            for hh in range(2):
                t = jnp.where(in_band, _toeplitz(vec_ref[0, 2 * p + hh], B_WIN)[:, :LANES], MASKED)
                lanes = slice(hh * LANES, (hh + 1) * LANES)
                bias_ref[2 * p, :, lanes] = t.astype(jnp.bfloat16)
                bias_ref[2 * p + 1, :, lanes] = jnp.where(before_block, MASKED, t).astype(jnp.bfloat16)
            return carry

        lax.fori_loop(0, N_PAIRS, pair_body, 0)

    rows = lax.broadcasted_iota(jnp.int32, (PAIR_W, TQ), 0)
    pairs_per_kv = N_PAIRS // B_KV_HEADS
    eye = _identity_bf16(B_SUB)

    def scores_fn(p, u, s_ref):
        qlanes = slice(u * LANES, (u + 1) * LANES)
        q2 = jnp.concatenate([qt_ref[0, p, :HEAD_DIM, qlanes], qt_ref[0, p, HEAD_DIM:, qlanes]], axis=1)
        kv0 = HEAD_DIM * (p // pairs_per_kv)
        kv_rows = (rows >= kv0) & (rows < kv0 + HEAD_DIM)
        qz = jnp.where(kv_rows, jnp.concatenate([q2, q2], axis=0), jnp.zeros((PAIR_W, TQ), q2.dtype))
        if u == 0:
            k_tile = jnp.concatenate([kp_ref[0, TQ - B_PREV:, :], ko_ref[0, :LANES, :]], axis=0)
            variant = jnp.where(i == 0, 1, 0)
        else:
            k_tile, variant = ko_ref[0], 0

        lhs, rhs = [], []
        for r0 in range(0, B_WIN, B_SUB):
            lhs.append(jnp.concatenate([k_tile[r0:r0 + B_SUB, :], eye], axis=1))
            rhs.append(jnp.concatenate([qz, bias_ref[2 * p + variant, r0:r0 + B_SUB, :]], axis=0))
        sub_live = (((0, B_SUB),) * (TQ // LANES),) * len(lhs)
        return _scores_stage(lhs, rhs, [None] * len(lhs), sub_live, s_ref)

    def pv_fn(p, u, s_ref, m):
        qlanes = slice(u * LANES, (u + 1) * LANES)
        kv_rows = slice(HEAD_DIM * (p // pairs_per_kv), HEAD_DIM * (p // pairs_per_kv + 1))
        if u == 0:
            v_tile = jnp.concatenate([vp_ref[0, kv_rows, TQ - B_PREV:], vo_ref[0, kv_rows, :LANES]], axis=1)
        else:
            v_tile = vo_ref[0, kv_rows, :]
        o = _pv_stage(s_ref, m, [v_tile], B_LIVE, extra_logit=sink_ref[p])
        for hh in range(2):
            hrows = slice(hh * HEAD_DIM, (hh + 1) * HEAD_DIM)
            gate = gt_ref[0, p, hrows, qlanes].astype(jnp.float32)
            row0 = p * PAIR_W + hh * HEAD_DIM
            zt_ref[row0:row0 + HEAD_DIM, qlanes] = (o[:, hh * LANES:(hh + 1) * LANES] * gate).astype(jnp.bfloat16)

    _head_pipeline(N_HEADS, scores_fn, pv_fn, s_refs, scores_first=False)

    h2 = _out_and_residual(zt_ref, wout_ref, h_ref[0])
    out_ref[0] = h2 * _rms_scale(h2) * fg_ref[...]


def _attn_b(h, qbt, ksh, vsht, gbt, vec, sinks, wout, fg):
    b, s, d = h.shape
    qt4 = qbt.reshape(b, N_PAIRS, PAIR_W, s)
    gt4 = gbt.reshape(b, N_PAIRS, PAIR_W, s)
    feat_spec = pl.BlockSpec((1, N_PAIRS, PAIR_W, TQ), lambda bi, i: (bi, 0, 0, i))
    return pl.pallas_call(
        _attn_b_kernel,
        grid=(b, s // TQ),
        in_specs=[
            pl.BlockSpec((1, TQ, d), lambda bi, i: (bi, i, 0)),
            feat_spec,
            pl.BlockSpec((1, TQ, PAIR_W), lambda bi, i: (bi, jnp.maximum(i - 1, 0), 0)),
            pl.BlockSpec((1, TQ, PAIR_W), lambda bi, i: (bi, i, 0)),
            pl.BlockSpec((1, PAIR_W, TQ), lambda bi, i: (bi, 0, jnp.maximum(i - 1, 0))),
            pl.BlockSpec((1, PAIR_W, TQ), lambda bi, i: (bi, 0, i)),
            feat_spec,
            _const_spec(vec.shape),
            _const_spec(sinks.shape),
            _const_spec(wout.shape),
            _const_spec(fg.shape),
        ],
        out_specs=pl.BlockSpec((1, TQ, d), lambda bi, i: (bi, i, 0)),
        out_shape=jax.ShapeDtypeStruct((b, s, d), jnp.float32),
        scratch_shapes=[pltpu.VMEM((d, TQ), jnp.bfloat16),
                        pltpu.VMEM((2 * N_PAIRS, B_WIN, TQ), jnp.bfloat16),
                        *[pltpu.VMEM((B_WIN, TQ), jnp.float32)] * N_SCORE_BUFS_B],
        compiler_params=pltpu.CompilerParams(
            dimension_semantics=("arbitrary", "arbitrary"), vmem_limit_bytes=VMEM_LIMIT),
        name="attn_b",
    )(h, qt4, ksh, ksh, vsht, vsht, gt4, vec, sinks, wout, fg)


def _slab_diff():
    u = jnp.arange(BIAS_PERIOD)
    return jnp.where(u < TQ, u, u - BIAS_PERIOD)


def _bias_vecs_a(rel_bias):
    back = (A_KBLOCKS - 1 - jnp.arange(A_KBLOCKS))[:, None] * TQ
    dist = back + _slab_diff()[None, :]
    idx = jnp.clip(dist, -A_REL_CLIP, A_REL_CLIP) + A_REL_CLIP
    vec = jnp.transpose(rel_bias[idx], (0, 2, 1)).astype(jnp.float32)
    return vec[:, :, None, :]


def _t5_bucket(rel):
    nb = T5_BUCKETS // 2
    max_exact = nb // 2
    ret = jnp.where(rel > 0, nb, 0)
    n = jnp.abs(rel)
    nf = jnp.maximum(n, 1).astype(jnp.float32)
    large = max_exact + (jnp.log(nf / max_exact) / math.log(T5_MAX_DIST / max_exact)
                         * (nb - max_exact)).astype(jnp.int32)
    large = jnp.minimum(large, nb - 1)
    return ret + jnp.where(n < max_exact, n, large)


def _bias_vec_b(t5_table):
    rel = -_slab_diff() - B_PREV
    vec = jnp.transpose(t5_table[_t5_bucket(rel)], (1, 0)).astype(jnp.float32)
    return vec[None, :, None, :]


def kernel(x, a_norm, a_w_in, a_rel_bias, a_w_out, kv_norm, kv_w, t5_bias,
           b_norm, b_w_in, b_sinks, b_w_out, final_norm):
    assert a_norm.shape[0] == 1 and b_norm.shape[0] == 1, "one A layer then one B layer"
    bf = jnp.bfloat16
    scale = HEAD_DIM ** -0.5 * LOG2E

    def q_col_scale(n):
        return jnp.where(jnp.arange(n) < D_MODEL, scale, 1.0).astype(jnp.float32)

    wt_a = _weight_t(a_w_in[0], q_col_scale(a_w_in.shape[2]), a_norm[0], W_T_COLS_A)
    qt, k, vt, gt = _proj_a(x, wt_a)

    nwt = _weight_t(b_w_in[0], q_col_scale(b_w_in.shape[2]), b_norm[0], W_T_COLS_B,
                    head=kv_w, head_block=1, head_w=PAIR_W, head_gain=kv_norm)
    kvw_k = (kv_w[:, :PAIR_W] * kv_norm[:, None]).astype(bf)
    h, ksh, vsht, qbt, gbt = _attn_a(
        x, qt, k, vt, gt, _bias_vecs_a(a_rel_bias[0] * LOG2E), a_w_out[0].astype(bf), kvw_k, nwt)

    sinks = jnp.repeat((b_sinks[0].astype(jnp.float32) * LOG2E).reshape(N_PAIRS, 1, 2), LANES, axis=2)
    return _attn_b(h, qbt, ksh, vsht, gbt, _bias_vec_b(t5_bias * LOG2E), sinks, b_w_out[0].astype(bf),
                   final_norm[None, :])
```

```python
import functools
import math

import jax
import jax.numpy as jnp
from jax import lax
from jax.experimental import pallas as pl
from jax.experimental.pallas import tpu as pltpu

D_MODEL = 1024
HEAD_DIM = 64
N_HEADS = D_MODEL // HEAD_DIM
N_PAIRS = N_HEADS // 2
PAIR_W = 2 * HEAD_DIM
CHUNK = 64
RMS_EPS = 1e-6
A_LEFT_CHUNKS = 8
A_REL_CLIP = 256
B_KV_HEADS = 2
B_LEFT_CHUNKS = 2
T5_BUCKETS = 32
T5_MAX_DIST = 128

TQ = 256
A_KBLOCKS = A_LEFT_CHUNKS * CHUNK // TQ + 1
B_PREV = B_LEFT_CHUNKS * CHUNK
TM_PROJ = 1024
FEAT_CHUNK = 256
W_T_COLS_A = 1024
W_T_COLS_B = 512
MASKED = -1e30
BIAS_PERIOD = 2 * TQ
SUBLANES = 8
LANES = 128
MAX_CHAINS = 2
BF16_ROWS = 16
LOG2E = math.log2(math.e)
N_SCORE_BUFS = 8
N_SCORE_BUFS_B = 8
VMEM_LIMIT = 56 * 1024 * 1024

_NT = (((1,), (1,)), ((), ()))
_TN = (((0,), (0,)), ((), ()))


def _rms_scale(xf):
    return lax.rsqrt(jnp.mean(xf * xf, axis=-1, keepdims=True) + RMS_EPS)


def _token_scales(xf):
    col = _rms_scale(xf)
    row = jnp.transpose(jnp.broadcast_to(col, (xf.shape[0], LANES)))[0:1, :]
    return col, row


def _silu(v):
    return v * jax.nn.sigmoid(v)


def _weight_t_kernel(*refs, has_head):
    if has_head:
        head_ref, hgain_ref, w_ref, cs_ref, gain_ref, o_ref = refs
        j = pl.program_id(0)

        @pl.when(j == 0)
        def _():
            pad = o_ref.shape[0] - head_ref.shape[1]
            o_ref[:pad, :] = jnp.zeros((pad, o_ref.shape[1]), jnp.bfloat16)
            o_ref[pad:, :] = (head_ref[...].T * hgain_ref[...]).astype(jnp.bfloat16)

        @pl.when(j > 0)
        def _():
            o_ref[...] = ((w_ref[...] * cs_ref[...]).T * gain_ref[...]).astype(jnp.bfloat16)
    else:
        w_ref, cs_ref, gain_ref, o_ref = refs
        o_ref[...] = ((w_ref[...] * cs_ref[...]).T * gain_ref[...]).astype(jnp.bfloat16)


def _weight_t(w, col_scale, row_gain, tc, head=None, head_block=0, head_w=None, head_gain=None):
    d, n = w.shape
    off = 0 if head is None else 1
    main_specs = [
        pl.BlockSpec((d, tc), lambda j: (0, jnp.maximum(j - off, 0))),
        pl.BlockSpec((1, tc), lambda j: (0, jnp.maximum(j - off, 0))),
        pl.BlockSpec((1, d), lambda j: (0, 0)),
    ]
    head_specs, head_args = [], []
    if head is not None:
        assert head.shape[0] == d and head.shape[1] % head_w == 0 and head_w <= tc
        head_specs = [pl.BlockSpec((d, head_w), lambda j: (0, head_block)), pl.BlockSpec((1, d), lambda j: (0, 0))]
        head_args = [head, head_gain[None, :]]
    return pl.pallas_call(
        functools.partial(_weight_t_kernel, has_head=head is not None),
        grid=(n // tc + off,),
        in_specs=head_specs + main_specs,
        out_specs=pl.BlockSpec((tc, d), lambda j: (j, 0)),
        out_shape=jax.ShapeDtypeStruct((n + off * tc, d), jnp.bfloat16),
        compiler_params=pltpu.CompilerParams(dimension_semantics=("arbitrary",), vmem_limit_bytes=VMEM_LIMIT),
        name="weight_t",
    )(*head_args, w, col_scale[None, :], row_gain[None, :])


def _proj_a_kernel(x_ref, wt_ref, qt_ref, k_ref, vt_ref, gt_ref):
    xf = x_ref[0]
    xb = xf.astype(jnp.bfloat16)
    r_col, r_row = _token_scales(xf)
    k = lax.dot_general(xb, wt_ref[D_MODEL:2 * D_MODEL, :], _NT,
                        preferred_element_type=jnp.float32) * r_col
    for p in range(N_PAIRS):
        k_ref[0, p] = k[:, p * PAIR_W:(p + 1) * PAIR_W].astype(jnp.bfloat16)
    for out_ref, base, act in ((gt_ref, 3 * D_MODEL, _silu), (vt_ref, 2 * D_MODEL, None), (qt_ref, 0, None)):
        for c in range(0, D_MODEL, FEAT_CHUNK):
            w = wt_ref[base + c:base + c + FEAT_CHUNK, :]
            yt = lax.dot_general(w, xb, _NT, preferred_element_type=jnp.float32) * r_row
            out_ref[0, c:c + FEAT_CHUNK, :] = (yt if act is None else act(yt)).astype(jnp.bfloat16)


def _proj_a(x, wt):
    b, s, d = x.shape
    feat = jax.ShapeDtypeStruct((b, d, s), jnp.bfloat16)
    return pl.pallas_call(
        _proj_a_kernel,
        grid=(b, s // TM_PROJ),
        in_specs=[
            pl.BlockSpec((1, TM_PROJ, d), lambda bi, i: (bi, i, 0)),
            _const_spec(wt.shape),
        ],
        out_specs=[
            pl.BlockSpec((1, d, TM_PROJ), lambda bi, i: (bi, 0, i)),
            pl.BlockSpec((1, N_PAIRS, TM_PROJ, PAIR_W), lambda bi, i: (bi, 0, i, 0)),
            pl.BlockSpec((1, d, TM_PROJ), lambda bi, i: (bi, 0, i)),
            pl.BlockSpec((1, d, TM_PROJ), lambda bi, i: (bi, 0, i)),
        ],
        out_shape=[feat, jax.ShapeDtypeStruct((b, N_PAIRS, s, PAIR_W), jnp.bfloat16), feat, feat],
        compiler_params=pltpu.CompilerParams(
            dimension_semantics=("arbitrary", "arbitrary"), vmem_limit_bytes=VMEM_LIMIT),
        name="proj_a",
    )(x, wt)


def _chunk_iota(shape, axis):
    return lax.shift_right_logical(lax.broadcasted_iota(jnp.int32, shape, axis), CHUNK.bit_length() - 1)


def _identity_bf16(n):
    return (lax.broadcasted_iota(jnp.int32, (n, n), 0) == lax.broadcasted_iota(jnp.int32, (n, n), 1)).astype(jnp.bfloat16)


def _toeplitz(vec_row, n_keys):
    return pltpu.roll(jnp.broadcast_to(vec_row, (n_keys, BIAS_PERIOD)), 0, 1, stride=1, stride_axis=0)


def _fill_bias_slabs(vec_ref, bias_ref, n_keys, bands):
    n = len(bands)
    kch = _chunk_iota((n_keys, TQ), 0)
    qch = _chunk_iota((n_keys, TQ), 1)

    def head_body(h, carry):
        for j, band in enumerate(bands):
            t = _toeplitz(vec_ref[j, h], n_keys)[:, :TQ]
            if band is not None:
                t = jnp.where(band(kch, qch), t, MASKED)
            bias_ref[h * n + j] = t
        return carry

    lax.fori_loop(0, N_HEADS, head_body, 0)


def _live_rows(band, n_keys):
    chunks_per_half = LANES // CHUNK
    out = []
    for half in range(TQ // LANES):
        qchs = range(half * chunks_per_half, (half + 1) * chunks_per_half)
        live = [kc for kc in range(n_keys // CHUNK) if band is None or any(band(kc, qc) for qc in qchs)]
        out.append((live[0] * CHUNK, (live[-1] + 1) * CHUNK) if live else (0, 0))
    return tuple(out)


def _scores_stage(k_tiles, q_tiles, bias_fns, live, s_ref):
    maxes, row = [], 0
    for half in range(TQ // LANES):
        maxes.append([None] * MAX_CHAINS)
    for kt, qz, bias_fn, live_j in zip(k_tiles, q_tiles, bias_fns, live):
        s = jnp.dot(kt, qz, preferred_element_type=jnp.float32)
        for half, (r0, r1) in enumerate(live_j):
            lanes = slice(half * LANES, (half + 1) * LANES)
            acc = maxes[half]
            for r in range(r0 // SUBLANES, r1 // SUBLANES):
                rows = slice(r * SUBLANES, (r + 1) * SUBLANES)
                grp = s[rows, lanes] if bias_fn is None else s[rows, lanes] + bias_fn(rows, lanes)
                s_ref[row + r * SUBLANES:row + (r + 1) * SUBLANES, lanes] = grp
                c = r % MAX_CHAINS
                acc[c] = grp if acc[c] is None else jnp.maximum(acc[c], grp)
        row += s.shape[0]
    cols = [jnp.max(functools.reduce(jnp.maximum, [a for a in acc if a is not None]), axis=0, keepdims=True)
            for acc in maxes]
    return jnp.concatenate(cols, axis=1)


def _pv_stage(load_fn, m, v_tiles, live, extra_logit=None):
    if extra_logit is not None:
        m = jnp.maximum(m, extra_logit)
    acc = None
    for j, (vt, live_j) in enumerate(zip(v_tiles, live)):
        n = vt.shape[1]
        halves = []
        for half, (r0, r1) in enumerate(live_j):
            lanes = slice(half * LANES, (half + 1) * LANES)
            parts = [jnp.zeros((r0, LANES), jnp.bfloat16)] if r0 else []
            if r1 > r0:
                parts.append(jnp.exp2(load_fn(j, half, r0, r1) - m[:, lanes]).astype(jnp.bfloat16))
            if n > r1:
                parts.append(jnp.zeros((n - r1, LANES), jnp.bfloat16))
            halves.append(parts[0] if len(parts) == 1 else jnp.concatenate(parts, axis=0))
        p = jnp.concatenate(halves, axis=1)
        v_ones = jnp.concatenate([vt, jnp.ones((BF16_ROWS, n), vt.dtype)], axis=0)
        part = jnp.dot(v_ones, p, preferred_element_type=jnp.float32)
        acc = part if acc is None else acc + part
    l = acc[HEAD_DIM:HEAD_DIM + 1, :]
    if extra_logit is not None:
        l = l + jnp.exp2(extra_logit - m)
    return acc[:HEAD_DIM, :] / l


def _head_pipeline(n_tiles, scores_fn, pv_fn, s_refs, scores_first):
    per_group = len(s_refs) // 2
    x_refs, y_refs = s_refs[:per_group], s_refs[per_group:]
    n_groups = n_tiles // per_group

    m = [scores_fn(t // 2, t % 2, ref) for t, ref in zip(range(per_group), x_refs)]
    for g in range(n_groups):
        src, dst = (x_refs, y_refs) if g % 2 == 0 else (y_refs, x_refs)
        nxt = [(g + 1) * per_group + j for j in range(per_group)] if g + 1 < n_groups else []
        m_next = [scores_fn(t // 2, t % 2, ref) for t, ref in zip(nxt, dst)] if scores_first else []
        for j in range(per_group):
            if not scores_first and nxt:
                m_next.append(scores_fn(nxt[j] // 2, nxt[j] % 2, dst[j]))
            t = g * per_group + j
            pv_fn(t // 2, t % 2, j, src, m)
        m = m_next


A_BANDS = (lambda kc, qc: kc >= qc,
           None,
           lambda kc, qc: kc <= qc)
A_LIVE = tuple(_live_rows(band, TQ) for band in A_BANDS)
A_WIN = A_LEFT_CHUNKS * CHUNK + LANES
A_SUB = LANES
A_BAND = lambda kc, qc: (kc >= qc) & (kc <= qc + A_LEFT_CHUNKS)
B_WIN = B_PREV + LANES
B_SUB = PAIR_W
B_BAND = lambda kc, qc: (kc >= qc) & (kc <= qc + B_LEFT_CHUNKS)
B_LIVE = (((0, B_WIN),) * (TQ // LANES),)


def _out_and_residual(zt_ref, wout_ref, res):
    y = lax.dot_general(zt_ref[...], wout_ref[...], _TN, preferred_element_type=jnp.float32)
    return res + y


def _attn_a_kernel(x_ref, qt_ref, k0_ref, k1_ref, k2_ref, v0_ref, v1_ref, v2_ref, gt_ref,
                   bvec_ref, wout_ref, kvwk_ref, nwt_ref,
                   h_ref, ksh_ref, vsht_ref, qbt_ref, gbt_ref, zt_ref, bias_ref, *s_refs):
    i = pl.program_id(1)

    @pl.when((pl.program_id(0) == 0) & (i == 0))
    def _():
        qch = _chunk_iota((A_SUB, LANES), 1)

        def head_body(h, carry):
            for r in range(A_WIN // A_SUB):
                kch = _chunk_iota((A_SUB, LANES), 0) + r * (A_SUB // CHUNK)
                t = _toeplitz(bvec_ref[r, h], A_SUB)[:, :LANES]
                bias_ref[h, r * A_SUB:(r + 1) * A_SUB, :] = jnp.where(A_BAND(kch, qch), t, MASKED)
            return carry

        lax.fori_loop(0, N_HEADS, head_body, 0)
        bias_ref[N_HEADS] = jnp.full((A_WIN, LANES), MASKED, jnp.float32)

    k_refs = (k0_ref, k1_ref, k2_ref)
    v_refs = (v0_ref, v1_ref, v2_ref)
    rows = lax.broadcasted_iota(jnp.int32, (PAIR_W, TQ), 0)
    lanes_i = lax.broadcasted_iota(jnp.int32, (PAIR_W, TQ), 1)
    own_dims = (rows < HEAD_DIM) == (lanes_i < LANES)

    def scores_fn(p, u, s_ref):
        qlanes = slice(u * LANES, (u + 1) * LANES)
        q2 = qt_ref[0, p, :, qlanes]
        qz = jnp.where(own_dims, jnp.concatenate([q2, q2], axis=1), jnp.zeros((PAIR_W, TQ), q2.dtype))
        k_tiles, bias_fns = [], []
        for j in range(A_KBLOCKS):
            lo, hi = max(u * A_SUB - j * TQ, 0), min(u * A_SUB + A_WIN - j * TQ, TQ)
            if hi <= lo:
                continue
            win0 = j * TQ + lo - u * A_SUB
            in_seq = i - (A_KBLOCKS - 1) + j >= 0
            k_tiles.append(k_refs[j][0, p, lo:hi, :])

            def bias_fn(rws, lanes, in_seq=in_seq, win0=win0):
                head = jnp.where(in_seq, 2 * p + lanes.start // LANES, N_HEADS)
                return bias_ref[head, win0 + rws.start:win0 + rws.stop, :]

            bias_fns.append(bias_fn)
        live = tuple(((0, kt.shape[0]),) * (TQ // LANES) for kt in k_tiles)
        return _scores_stage(k_tiles, [qz] * len(k_tiles), bias_fns, live, s_ref)

    def pv_fn(p, hh, j, src, ms):
        half = slice(hh * HEAD_DIM, (hh + 1) * HEAD_DIM)
        hl = slice(hh * LANES, (hh + 1) * LANES)
        t0 = j - hh
        m = jnp.concatenate([ms[t0][:, hl], ms[t0 + 1][:, hl]], axis=1)

        def load_fn(jb, u, r0, r1):
            win0 = jb * TQ - u * A_SUB
            return src[t0 + u][win0 + r0:win0 + r1, hl]

        o = _pv_stage(load_fn, m, [v_ref[0, p, half, :] for v_ref in v_refs], A_LIVE)
        gate = gt_ref[0, p, half, :].astype(jnp.float32)
        row0 = p * PAIR_W + hh * HEAD_DIM
        zt_ref[row0:row0 + HEAD_DIM, :] = (o * gate).astype(jnp.bfloat16)

    _head_pipeline(N_HEADS, scores_fn, pv_fn, s_refs, scores_first=True)

    h = _out_and_residual(zt_ref, wout_ref, x_ref[0])
    h_ref[0] = h
    hb = h.astype(jnp.bfloat16)
    r_col, r_row = _token_scales(h)
    ksh_ref[0] = (jnp.dot(hb, kvwk_ref[...], preferred_element_type=jnp.float32) * r_col).astype(jnp.bfloat16)
    for out_ref, base, act in ((gbt_ref, W_T_COLS_B + D_MODEL, _silu), (qbt_ref, W_T_COLS_B, None)):
        for c in range(0, D_MODEL, FEAT_CHUNK):
            first = out_ref is qbt_ref and c == 0
            start = base + c - (PAIR_W if first else 0)
            yt = lax.dot_general(nwt_ref[start:base + c + FEAT_CHUNK, :], hb, _NT,
                                 preferred_element_type=jnp.float32) * r_row
            if first:
                vsht_ref[0] = yt[:PAIR_W].astype(jnp.bfloat16)
                yt = yt[PAIR_W:]
            out_ref[0, c:c + FEAT_CHUNK, :] = (yt if act is None else act(yt)).astype(jnp.bfloat16)


def _const_spec(shape):
    return pl.BlockSpec(shape, lambda bi, i: (0,) * len(shape), pipeline_mode=pl.Buffered(1))


def _attn_a(x, qt, k, vt, gt, bvec, wout, kvwk, nwt):
    b, s, d = x.shape
    qt4 = qt.reshape(b, N_PAIRS, PAIR_W, s)
    vt4 = vt.reshape(b, N_PAIRS, PAIR_W, s)
    gt4 = gt.reshape(b, N_PAIRS, PAIR_W, s)
    back = A_KBLOCKS - 1

    def kspec(j):
        return pl.BlockSpec((1, N_PAIRS, TQ, PAIR_W),
                            lambda bi, i: (bi, 0, jnp.maximum(i - back + j, 0), 0))

    def vspec(j):
        return pl.BlockSpec((1, N_PAIRS, PAIR_W, TQ),
                            lambda bi, i: (bi, 0, 0, jnp.maximum(i - back + j, 0)))

    feat_spec = pl.BlockSpec((1, N_PAIRS, PAIR_W, TQ), lambda bi, i: (bi, 0, 0, i))
    featout_spec = pl.BlockSpec((1, d, TQ), lambda bi, i: (bi, 0, i))
    feat = jax.ShapeDtypeStruct((b, d, s), jnp.bfloat16)
    return pl.pallas_call(
        _attn_a_kernel,
        grid=(b, s // TQ),
        in_specs=[
            pl.BlockSpec((1, TQ, d), lambda bi, i: (bi, i, 0)),
            feat_spec,
            kspec(0), kspec(1), kspec(2),
            vspec(0), vspec(1), vspec(2),
            feat_spec,
            _const_spec(bvec.shape),
            _const_spec(wout.shape),
            _const_spec(kvwk.shape),
            _const_spec(nwt.shape),
        ],
        out_specs=[
            pl.BlockSpec((1, TQ, d), lambda bi, i: (bi, i, 0)),
            pl.BlockSpec((1, TQ, PAIR_W), lambda bi, i: (bi, i, 0)),
            pl.BlockSpec((1, PAIR_W, TQ), lambda bi, i: (bi, 0, i)),
            featout_spec,
            featout_spec,
        ],
        out_shape=[
            jax.ShapeDtypeStruct((b, s, d), jnp.float32),
            jax.ShapeDtypeStruct((b, s, PAIR_W), jnp.bfloat16),
            jax.ShapeDtypeStruct((b, PAIR_W, s), jnp.bfloat16),
            feat, feat,
        ],
        scratch_shapes=[pltpu.VMEM((d, TQ), jnp.bfloat16),
                        pltpu.VMEM((N_HEADS + 1, A_WIN, LANES), jnp.float32),
                        *[pltpu.VMEM((A_WIN, TQ), jnp.float32)] * N_SCORE_BUFS],
        compiler_params=pltpu.CompilerParams(
            dimension_semantics=("arbitrary", "arbitrary"), vmem_limit_bytes=VMEM_LIMIT),
        name="attn_a",
    )(x, qt4, k, k, k, vt4, vt4, vt4, gt4, bvec, wout, kvwk, nwt)


def _attn_b_kernel(h_ref, qt_ref, kp_ref, ko_ref, vp_ref, vo_ref, gt_ref, vec_ref,
                   sink_ref, wout_ref, fg_ref, out_ref, zt_ref, bias_ref, *s_refs):
    i = pl.program_id(1)

    @pl.when((pl.program_id(0) == 0) & (i == 0))
    def _():
        in_band = B_BAND(_chunk_iota((B_WIN, LANES), 0), _chunk_iota((B_WIN, LANES), 1))
        before_block = lax.broadcasted_iota(jnp.int32, (B_WIN, LANES), 0) < B_PREV

        def pair_body(p, carry):
            for hh in range(2):
                t = jnp.where(in_band, _toeplitz(vec_ref[0, 2 * p + hh], B_WIN)[:, :LANES], MASKED)
                lanes = slice(hh * LANES, (hh + 1) * LANES)
                bias_ref[2 * p, :, lanes] = t.astype(jnp.bfloat16)
                bias_ref[2 * p + 1, :, lanes] = jnp.where(before_block, MASKED, t).astype(jnp.bfloat16)
            return carry

        lax.fori_loop(0, N_PAIRS, pair_body, 0)

    rows = lax.broadcasted_iota(jnp.int32, (PAIR_W, TQ), 0)
    pairs_per_kv = N_PAIRS // B_KV_HEADS
    eye = _identity_bf16(B_SUB)

    def scores_fn(p, u, s_ref):
        qlanes = slice(u * LANES, (u + 1) * LANES)
        q2 = jnp.concatenate([qt_ref[0, p, :HEAD_DIM, qlanes], qt_ref[0, p, HEAD_DIM:, qlanes]], axis=1)
        kv0 = HEAD_DIM * (p // pairs_per_kv)
        kv_rows = (rows >= kv0) & (rows < kv0 + HEAD_DIM)
        qz = jnp.where(kv_rows, jnp.concatenate([q2, q2], axis=0), jnp.zeros((PAIR_W, TQ), q2.dtype))
        if u == 0:
            k_tile = jnp.concatenate([kp_ref[0, TQ - B_PREV:, :], ko_ref[0, :LANES, :]], axis=0)
            variant = jnp.where(i == 0, 1, 0)
        else:
            k_tile, variant = ko_ref[0], 0

        lhs, rhs = [], []
        for r0 in range(0, B_WIN, B_SUB):
            lhs.append(jnp.concatenate([k_tile[r0:r0 + B_SUB, :], eye], axis=1))
            rhs.append(jnp.concatenate([qz, bias_ref[2 * p + variant, r0:r0 + B_SUB, :]], axis=0))
        sub_live = (((0, B_SUB),) * (TQ // LANES),) * len(lhs)
        return _scores_stage(lhs, rhs, [None] * len(lhs), sub_live, s_ref)

    def pv_fn(p, u, j, src, ms):
        qlanes = slice(u * LANES, (u + 1) * LANES)
        kv_rows = slice(HEAD_DIM * (p // pairs_per_kv), HEAD_DIM * (p // pairs_per_kv + 1))
        if u == 0:
            v_tile = jnp.concatenate([vp_ref[0, kv_rows, TQ - B_PREV:], vo_ref[0, kv_rows, :LANES]], axis=1)
        else:
            v_tile = vo_ref[0, kv_rows, :]
        load_fn = lambda _, half, r0, r1: src[j][r0:r1, half * LANES:(half + 1) * LANES]
        o = _pv_stage(load_fn, ms[j], [v_tile], B_LIVE, extra_logit=sink_ref[p])
        for hh in range(2):
            hrows = slice(hh * HEAD_DIM, (hh + 1) * HEAD_DIM)
            gate = gt_ref[0, p, hrows, qlanes].astype(jnp.float32)
            row0 = p * PAIR_W + hh * HEAD_DIM
            zt_ref[row0:row0 + HEAD_DIM, qlanes] = (o[:, hh * LANES:(hh + 1) * LANES] * gate).astype(jnp.bfloat16)

    _head_pipeline(N_HEADS, scores_fn, pv_fn, s_refs, scores_first=False)

    h2 = _out_and_residual(zt_ref, wout_ref, h_ref[0])
    out_ref[0] = h2 * _rms_scale(h2) * fg_ref[...]


def _attn_b(h, qbt, ksh, vsht, gbt, vec, sinks, wout, fg):
    b, s, d = h.shape
    qt4 = qbt.reshape(b, N_PAIRS, PAIR_W, s)
    gt4 = gbt.reshape(b, N_PAIRS, PAIR_W, s)
    feat_spec = pl.BlockSpec((1, N_PAIRS, PAIR_W, TQ), lambda bi, i: (bi, 0, 0, i))
    return pl.pallas_call(
        _attn_b_kernel,
        grid=(b, s // TQ),
        in_specs=[
            pl.BlockSpec((1, TQ, d), lambda bi, i: (bi, i, 0)),
            feat_spec,
            pl.BlockSpec((1, TQ, PAIR_W), lambda bi, i: (bi, jnp.maximum(i - 1, 0), 0)),
            pl.BlockSpec((1, TQ, PAIR_W), lambda bi, i: (bi, i, 0)),
            pl.BlockSpec((1, PAIR_W, TQ), lambda bi, i: (bi, 0, jnp.maximum(i - 1, 0))),
            pl.BlockSpec((1, PAIR_W, TQ), lambda bi, i: (bi, 0, i)),
            feat_spec,
            _const_spec(vec.shape),
            _const_spec(sinks.shape),
            _const_spec(wout.shape),
            _const_spec(fg.shape),
        ],
        out_specs=pl.BlockSpec((1, TQ, d), lambda bi, i: (bi, i, 0)),
        out_shape=jax.ShapeDtypeStruct((b, s, d), jnp.float32),
        scratch_shapes=[pltpu.VMEM((d, TQ), jnp.bfloat16),
                        pltpu.VMEM((2 * N_PAIRS, B_WIN, TQ), jnp.bfloat16),
                        *[pltpu.VMEM((B_WIN, TQ), jnp.float32)] * N_SCORE_BUFS_B],
        compiler_params=pltpu.CompilerParams(
            dimension_semantics=("arbitrary", "arbitrary"), vmem_limit_bytes=VMEM_LIMIT),
        name="attn_b",
    )(h, qt4, ksh, ksh, vsht, vsht, gt4, vec, sinks, wout, fg)


def _slab_diff():
    u = jnp.arange(BIAS_PERIOD)
    return jnp.where(u < TQ, u, u - BIAS_PERIOD)


def _bias_vecs_a(rel_bias):
    back = (A_LEFT_CHUNKS * CHUNK - jnp.arange(A_WIN // A_SUB) * A_SUB)[:, None]
    dist = back + _slab_diff()[None, :]
    idx = jnp.clip(dist, -A_REL_CLIP, A_REL_CLIP) + A_REL_CLIP
    vec = jnp.transpose(rel_bias[idx], (0, 2, 1)).astype(jnp.float32)
    return vec[:, :, None, :]


def _t5_bucket(rel):
    nb = T5_BUCKETS // 2
    max_exact = nb // 2
    ret = jnp.where(rel > 0, nb, 0)
    n = jnp.abs(rel)
    nf = jnp.maximum(n, 1).astype(jnp.float32)
    large = max_exact + (jnp.log(nf / max_exact) / math.log(T5_MAX_DIST / max_exact)
                         * (nb - max_exact)).astype(jnp.int32)
    large = jnp.minimum(large, nb - 1)
    return ret + jnp.where(n < max_exact, n, large)


def _bias_vec_b(t5_table):
    rel = -_slab_diff() - B_PREV
    vec = jnp.transpose(t5_table[_t5_bucket(rel)], (1, 0)).astype(jnp.float32)
    return vec[None, :, None, :]


def kernel(x, a_norm, a_w_in, a_rel_bias, a_w_out, kv_norm, kv_w, t5_bias,
           b_norm, b_w_in, b_sinks, b_w_out, final_norm):
    assert a_norm.shape[0] == 1 and b_norm.shape[0] == 1, "one A layer then one B layer"
    bf = jnp.bfloat16
    scale = HEAD_DIM ** -0.5 * LOG2E

    def q_col_scale(n):
        return jnp.where(jnp.arange(n) < D_MODEL, scale, 1.0).astype(jnp.float32)

    wt_a = _weight_t(a_w_in[0], q_col_scale(a_w_in.shape[2]), a_norm[0], W_T_COLS_A)
    qt, k, vt, gt = _proj_a(x, wt_a)

    nwt = _weight_t(b_w_in[0], q_col_scale(b_w_in.shape[2]), b_norm[0], W_T_COLS_B,
                    head=kv_w, head_block=1, head_w=PAIR_W, head_gain=kv_norm)
    kvw_k = (kv_w[:, :PAIR_W] * kv_norm[:, None]).astype(bf)
    h, ksh, vsht, qbt, gbt = _attn_a(
        x, qt, k, vt, gt, _bias_vecs_a(a_rel_bias[0] * LOG2E), a_w_out[0].astype(bf), kvw_k, nwt)

    sinks = jnp.repeat((b_sinks[0].astype(jnp.float32) * LOG2E).reshape(N_PAIRS, 1, 2), LANES, axis=2)
    return _attn_b(h, qbt, ksh, vsht, gbt, _bias_vec_b(t5_bias * LOG2E), sinks, b_w_out[0].astype(bf),
                   final_norm[None, :])
```

```python
import functools
import math

import jax
import jax.numpy as jnp
from jax import lax
from jax.experimental import pallas as pl
from jax.experimental.pallas import tpu as pltpu

D_MODEL = 1024
HEAD_DIM = 64
N_HEADS = D_MODEL // HEAD_DIM
N_PAIRS = N_HEADS // 2
PAIR_W = 2 * HEAD_DIM
CHUNK = 64
RMS_EPS = 1e-6
A_LEFT_CHUNKS = 8
A_REL_CLIP = 256
B_KV_HEADS = 2
B_LEFT_CHUNKS = 2
T5_BUCKETS = 32
T5_MAX_DIST = 128

TQ = 256
A_KBLOCKS = A_LEFT_CHUNKS * CHUNK // TQ + 1
B_PREV = B_LEFT_CHUNKS * CHUNK
TM_PROJ = 1024
FEAT_CHUNK = 256
W_T_COLS_A = 1024
W_T_COLS_B = 512
MASKED = -1e30
BIAS_PERIOD = 2 * TQ
SUBLANES = 8
LANES = 128
MAX_CHAINS = 1
BF16_ROWS = 16
LOG2E = math.log2(math.e)
N_SCORE_BUFS = 4
N_SCORE_BUFS_B = 8
VMEM_LIMIT = 56 * 1024 * 1024

_NT = (((1,), (1,)), ((), ()))
_TN = (((0,), (0,)), ((), ()))


def _rms_scale(xf):
    return lax.rsqrt(jnp.mean(xf * xf, axis=-1, keepdims=True) + RMS_EPS)


def _token_scales(xf):
    col = _rms_scale(xf)
    row = jnp.transpose(jnp.broadcast_to(col, (xf.shape[0], LANES)))[0:1, :]
    return col, row


def _silu(v):
    return v * jax.nn.sigmoid(v)


def _weight_t_kernel(*refs, has_head):
    if has_head:
        head_ref, hgain_ref, w_ref, cs_ref, gain_ref, o_ref = refs
        j = pl.program_id(0)

        @pl.when(j == 0)
        def _():
            pad = o_ref.shape[0] - head_ref.shape[1]
            o_ref[:pad, :] = jnp.zeros((pad, o_ref.shape[1]), jnp.bfloat16)
            o_ref[pad:, :] = (head_ref[...].T * hgain_ref[...]).astype(jnp.bfloat16)

        @pl.when(j > 0)
        def _():
            o_ref[...] = ((w_ref[...] * cs_ref[...]).T * gain_ref[...]).astype(jnp.bfloat16)
    else:
        w_ref, cs_ref, gain_ref, o_ref = refs
        o_ref[...] = ((w_ref[...] * cs_ref[...]).T * gain_ref[...]).astype(jnp.bfloat16)


def _weight_t(w, col_scale, row_gain, tc, head=None, head_block=0, head_w=None, head_gain=None):
    d, n = w.shape
    off = 0 if head is None else 1
    main_specs = [
        pl.BlockSpec((d, tc), lambda j: (0, jnp.maximum(j - off, 0))),
        pl.BlockSpec((1, tc), lambda j: (0, jnp.maximum(j - off, 0))),
        pl.BlockSpec((1, d), lambda j: (0, 0)),
    ]
    head_specs, head_args = [], []
    if head is not None:
        assert head.shape[0] == d and head.shape[1] % head_w == 0 and head_w <= tc
        head_specs = [pl.BlockSpec((d, head_w), lambda j: (0, head_block)), pl.BlockSpec((1, d), lambda j: (0, 0))]
        head_args = [head, head_gain[None, :]]
    return pl.pallas_call(
        functools.partial(_weight_t_kernel, has_head=head is not None),
        grid=(n // tc + off,),
        in_specs=head_specs + main_specs,
        out_specs=pl.BlockSpec((tc, d), lambda j: (j, 0)),
        out_shape=jax.ShapeDtypeStruct((n + off * tc, d), jnp.bfloat16),
        compiler_params=pltpu.CompilerParams(dimension_semantics=("arbitrary",), vmem_limit_bytes=VMEM_LIMIT),
        name="weight_t",
    )(*head_args, w, col_scale[None, :], row_gain[None, :])


def _proj_a_kernel(x_ref, wt_ref, qt_ref, k_ref, vt_ref, gt_ref):
    xf = x_ref[0]
    xb = xf.astype(jnp.bfloat16)
    r_col, r_row = _token_scales(xf)
    k = lax.dot_general(xb, wt_ref[D_MODEL:2 * D_MODEL, :], _NT,
                        preferred_element_type=jnp.float32) * r_col
    for p in range(N_PAIRS):
        k_ref[0, p] = k[:, p * PAIR_W:(p + 1) * PAIR_W].astype(jnp.bfloat16)
    for out_ref, base, act in ((gt_ref, 3 * D_MODEL, _silu), (vt_ref, 2 * D_MODEL, None), (qt_ref, 0, None)):
        for c in range(0, D_MODEL, FEAT_CHUNK):
            w = wt_ref[base + c:base + c + FEAT_CHUNK, :]
            yt = lax.dot_general(w, xb, _NT, preferred_element_type=jnp.float32) * r_row
            out_ref[0, c:c + FEAT_CHUNK, :] = (yt if act is None else act(yt)).astype(jnp.bfloat16)


def _proj_a(x, wt):
    b, s, d = x.shape
    feat = jax.ShapeDtypeStruct((b, d, s), jnp.bfloat16)
    return pl.pallas_call(
        _proj_a_kernel,
        grid=(b, s // TM_PROJ),
        in_specs=[
            pl.BlockSpec((1, TM_PROJ, d), lambda bi, i: (bi, i, 0)),
            _const_spec(wt.shape),
        ],
        out_specs=[
            pl.BlockSpec((1, d, TM_PROJ), lambda bi, i: (bi, 0, i)),
            pl.BlockSpec((1, N_PAIRS, TM_PROJ, PAIR_W), lambda bi, i: (bi, 0, i, 0)),
            pl.BlockSpec((1, d, TM_PROJ), lambda bi, i: (bi, 0, i)),
            pl.BlockSpec((1, d, TM_PROJ), lambda bi, i: (bi, 0, i)),
        ],
        out_shape=[feat, jax.ShapeDtypeStruct((b, N_PAIRS, s, PAIR_W), jnp.bfloat16), feat, feat],
        compiler_params=pltpu.CompilerParams(
            dimension_semantics=("arbitrary", "arbitrary"), vmem_limit_bytes=VMEM_LIMIT),
        name="proj_a",
    )(x, wt)


def _chunk_iota(shape, axis):
    return lax.shift_right_logical(lax.broadcasted_iota(jnp.int32, shape, axis), CHUNK.bit_length() - 1)


def _identity_bf16(n):
    return (lax.broadcasted_iota(jnp.int32, (n, n), 0) == lax.broadcasted_iota(jnp.int32, (n, n), 1)).astype(jnp.bfloat16)


def _toeplitz(vec_row, n_keys):
    return pltpu.roll(jnp.broadcast_to(vec_row, (n_keys, BIAS_PERIOD)), 0, 1, stride=1, stride_axis=0)


def _fill_bias_slabs(vec_ref, bias_ref, n_keys, bands):
    n = len(bands)
    kch = _chunk_iota((n_keys, TQ), 0)
    qch = _chunk_iota((n_keys, TQ), 1)

    def head_body(h, carry):
        for j, band in enumerate(bands):
            t = _toeplitz(vec_ref[j, h], n_keys)[:, :TQ]
            if band is not None:
                t = jnp.where(band(kch, qch), t, MASKED)
            bias_ref[h * n + j] = t
        return carry

    lax.fori_loop(0, N_HEADS, head_body, 0)


def _live_rows(band, n_keys):
    chunks_per_half = LANES // CHUNK
    out = []
    for half in range(TQ // LANES):
        qchs = range(half * chunks_per_half, (half + 1) * chunks_per_half)
        live = [kc for kc in range(n_keys // CHUNK) if band is None or any(band(kc, qc) for qc in qchs)]
        out.append((live[0] * CHUNK, (live[-1] + 1) * CHUNK) if live else (0, 0))
    return tuple(out)


def _scores_stage(k_tiles, q_tiles, bias_fns, live, s_ref):
    maxes, row = [], 0
    for half in range(TQ // LANES):
        maxes.append([None] * MAX_CHAINS)
    for kt, qz, bias_fn, live_j in zip(k_tiles, q_tiles, bias_fns, live):
        s = jnp.dot(kt, qz, preferred_element_type=jnp.float32)
        for r in range(s.shape[0] // SUBLANES):
            rows = slice(r * SUBLANES, (r + 1) * SUBLANES)
            for half, (r0, r1) in enumerate(live_j):
                if not r0 <= r * SUBLANES < r1:
                    continue
                lanes = slice(half * LANES, (half + 1) * LANES)
                acc = maxes[half]
                grp = s[rows, lanes] if bias_fn is None else s[rows, lanes] + bias_fn(rows, lanes)
                s_ref[row + r * SUBLANES:row + (r + 1) * SUBLANES, lanes] = grp
                c = r % MAX_CHAINS
                acc[c] = grp if acc[c] is None else jnp.maximum(acc[c], grp)
        row += s.shape[0]
    cols = [jnp.max(functools.reduce(jnp.maximum, [a for a in acc if a is not None]), axis=0, keepdims=True)
            for acc in maxes]
    return jnp.concatenate(cols, axis=1)


def _pv_stage(s_ref, m, v_tiles, live, extra_logit=None):
    if extra_logit is not None:
        m = jnp.maximum(m, extra_logit)
    acc, row = None, 0
    for vt, live_j in zip(v_tiles, live):
        n = vt.shape[1]
        halves = []
        for half, (r0, r1) in enumerate(live_j):
            lanes = slice(half * LANES, (half + 1) * LANES)
            parts = [jnp.zeros((r0, LANES), jnp.bfloat16)] if r0 else []
            if r1 > r0:
                parts.append(jnp.exp2(s_ref[row + r0:row + r1, lanes] - m[:, lanes]).astype(jnp.bfloat16))
            if n > r1:
                parts.append(jnp.zeros((n - r1, LANES), jnp.bfloat16))
            halves.append(parts[0] if len(parts) == 1 else jnp.concatenate(parts, axis=0))
        p = jnp.concatenate(halves, axis=1)
        v_ones = jnp.concatenate([vt, jnp.ones((BF16_ROWS, n), vt.dtype)], axis=0)
        part = jnp.dot(v_ones, p, preferred_element_type=jnp.float32)
        acc = part if acc is None else acc + part
        row += n
    l = acc[HEAD_DIM:HEAD_DIM + 1, :]
    if extra_logit is not None:
        l = l + jnp.exp2(extra_logit - m)
    return acc[:HEAD_DIM, :] / l


def _head_pipeline(n_tiles, scores_fn, pv_fn, s_refs, scores_first):
    per_group = len(s_refs) // 2
    x_refs, y_refs = s_refs[:per_group], s_refs[per_group:]
    n_groups = n_tiles // per_group

    m = [scores_fn(t // 2, t % 2, ref) for t, ref in zip(range(per_group), x_refs)]
    for g in range(n_groups):
        src, dst = (x_refs, y_refs) if g % 2 == 0 else (y_refs, x_refs)
        nxt = [(g + 1) * per_group + j for j in range(per_group)] if g + 1 < n_groups else []
        m_next = [scores_fn(t // 2, t % 2, ref) for t, ref in zip(nxt, dst)] if scores_first else []
        for j in range(per_group):
            if not scores_first and nxt:
                m_next.append(scores_fn(nxt[j] // 2, nxt[j] % 2, dst[j]))
            t = g * per_group + j
            pv_fn(t // 2, t % 2, src[j], m[j])
        m = m_next


A_BANDS = (lambda kc, qc: kc >= qc,
           None,
           lambda kc, qc: kc <= qc)
A_LIVE = tuple(_live_rows(band, TQ) for band in A_BANDS)
B_WIN = B_PREV + LANES
B_SUB = PAIR_W
B_BAND = lambda kc, qc: (kc >= qc) & (kc <= qc + B_LEFT_CHUNKS)
B_LIVE = (((0, B_WIN),) * (TQ // LANES),)


def _out_and_residual(zt_ref, wout_ref, res):
    y = lax.dot_general(zt_ref[...], wout_ref[...], _TN, preferred_element_type=jnp.float32)
    return res + y


def _attn_a_kernel(x_ref, qt_ref, k0_ref, k1_ref, k2_ref, v0_ref, v1_ref, v2_ref, gt_ref,
                   bvec_ref, wout_ref, kvwk_ref, nwt_ref,
                   h_ref, ksh_ref, vsht_ref, qbt_ref, gbt_ref, zt_ref, bias_ref, *s_refs):
    i = pl.program_id(1)

    @pl.when((pl.program_id(0) == 0) & (i == 0))
    def _():
        _fill_bias_slabs(bvec_ref, bias_ref, TQ, A_BANDS)
        bias_ref[N_HEADS * A_KBLOCKS] = jnp.full((TQ, TQ), MASKED, jnp.float32)

    k_refs = (k0_ref, k1_ref, k2_ref)
    v_refs = (v0_ref, v1_ref, v2_ref)
    n_slabs = N_HEADS * A_KBLOCKS
    rows = lax.broadcasted_iota(jnp.int32, (PAIR_W, TQ), 0)

    def scores_fn(p, hh, s_ref):
        q2 = qt_ref[0, p]
        own_rows = rows < HEAD_DIM if hh == 0 else rows >= HEAD_DIM
        qz = jnp.where(own_rows, q2, jnp.zeros_like(q2))
        k_tiles, bias_fns = [], []
        for j in range(A_KBLOCKS):
            in_seq = i - (A_KBLOCKS - 1) + j >= 0
            slab = jnp.where(in_seq, (2 * p + hh) * A_KBLOCKS + j, n_slabs)
            k_tiles.append(k_refs[j][0, p])
            bias_fns.append(lambda rws, lanes, slab=slab: bias_ref[slab, rws, lanes])
        return _scores_stage(k_tiles, [qz] * A_KBLOCKS, bias_fns, A_LIVE, s_ref)

    def pv_fn(p, hh, s_ref, m):
        half = slice(hh * HEAD_DIM, (hh + 1) * HEAD_DIM)
        o = _pv_stage(s_ref, m, [v_ref[0, p, half, :] for v_ref in v_refs], A_LIVE)
        gate = gt_ref[0, p, half, :].astype(jnp.float32)
        row0 = p * PAIR_W + hh * HEAD_DIM
        zt_ref[row0:row0 + HEAD_DIM, :] = (o * gate).astype(jnp.bfloat16)

    _head_pipeline(N_HEADS, scores_fn, pv_fn, s_refs, scores_first=True)

    h = _out_and_residual(zt_ref, wout_ref, x_ref[0])
    h_ref[0] = h
    hb = h.astype(jnp.bfloat16)
    r_col, r_row = _token_scales(h)
    ksh_ref[0] = (jnp.dot(hb, kvwk_ref[...], preferred_element_type=jnp.float32) * r_col).astype(jnp.bfloat16)
    for out_ref, base, act in ((gbt_ref, W_T_COLS_B + D_MODEL, _silu), (qbt_ref, W_T_COLS_B, None)):
        for c in range(0, D_MODEL, FEAT_CHUNK):
            first = out_ref is qbt_ref and c == 0
            start = base + c - (PAIR_W if first else 0)
            yt = lax.dot_general(nwt_ref[start:base + c + FEAT_CHUNK, :], hb, _NT,
                                 preferred_element_type=jnp.float32) * r_row
            if first:
                vsht_ref[0] = yt[:PAIR_W].astype(jnp.bfloat16)
                yt = yt[PAIR_W:]
            out_ref[0, c:c + FEAT_CHUNK, :] = (yt if act is None else act(yt)).astype(jnp.bfloat16)


def _const_spec(shape):
    return pl.BlockSpec(shape, lambda bi, i: (0,) * len(shape), pipeline_mode=pl.Buffered(1))


def _attn_a(x, qt, k, vt, gt, bvec, wout, kvwk, nwt):
    b, s, d = x.shape
    qt4 = qt.reshape(b, N_PAIRS, PAIR_W, s)
    vt4 = vt.reshape(b, N_PAIRS, PAIR_W, s)
    gt4 = gt.reshape(b, N_PAIRS, PAIR_W, s)
    back = A_KBLOCKS - 1

    def kspec(j):
        return pl.BlockSpec((1, N_PAIRS, TQ, PAIR_W),
                            lambda bi, i: (bi, 0, jnp.maximum(i - back + j, 0), 0))

    def vspec(j):
        return pl.BlockSpec((1, N_PAIRS, PAIR_W, TQ),
                            lambda bi, i: (bi, 0, 0, jnp.maximum(i - back + j, 0)))

    feat_spec = pl.BlockSpec((1, N_PAIRS, PAIR_W, TQ), lambda bi, i: (bi, 0, 0, i))
    featout_spec = pl.BlockSpec((1, d, TQ), lambda bi, i: (bi, 0, i))
    feat = jax.ShapeDtypeStruct((b, d, s), jnp.bfloat16)
    return pl.pallas_call(
        _attn_a_kernel,
        grid=(b, s // TQ),
        in_specs=[
            pl.BlockSpec((1, TQ, d), lambda bi, i: (bi, i, 0)),
            feat_spec,
            kspec(0), kspec(1), kspec(2),
            vspec(0), vspec(1), vspec(2),
            feat_spec,
            _const_spec(bvec.shape),
            _const_spec(wout.shape),
            _const_spec(kvwk.shape),
            _const_spec(nwt.shape),
        ],
        out_specs=[
            pl.BlockSpec((1, TQ, d), lambda bi, i: (bi, i, 0)),
            pl.BlockSpec((1, TQ, PAIR_W), lambda bi, i: (bi, i, 0)),
            pl.BlockSpec((1, PAIR_W, TQ), lambda bi, i: (bi, 0, i)),
            featout_spec,
            featout_spec,
        ],
        out_shape=[
            jax.ShapeDtypeStruct((b, s, d), jnp.float32),
            jax.ShapeDtypeStruct((b, s, PAIR_W), jnp.bfloat16),
            jax.ShapeDtypeStruct((b, PAIR_W, s), jnp.bfloat16),
            feat, feat,
        ],
        scratch_shapes=[pltpu.VMEM((d, TQ), jnp.bfloat16),
                        pltpu.VMEM((N_HEADS * A_KBLOCKS + 1, TQ, TQ), jnp.float32),
                        *[pltpu.VMEM((A_KBLOCKS * TQ, TQ), jnp.float32)] * N_SCORE_BUFS],
        compiler_params=pltpu.CompilerParams(
            dimension_semantics=("arbitrary", "arbitrary"), vmem_limit_bytes=VMEM_LIMIT),
        name="attn_a",
    )(x, qt4, k, k, k, vt4, vt4, vt4, gt4, bvec, wout, kvwk, nwt)


def _attn_b_kernel(h_ref, qt_ref, kp_ref, ko_ref, vp_ref, vo_ref, gt_ref, vec_ref,
                   sink_ref, wout_ref, fg_ref, out_ref, zt_ref, bias_ref, *s_refs):
    i = pl.program_id(1)

    @pl.when((pl.program_id(0) == 0) & (i == 0))
    def _():
        in_band = B_BAND(_chunk_iota((B_WIN, LANES), 0), _chunk_iota((B_WIN, LANES), 1))
        before_block = lax.broadcasted_iota(jnp.int32, (B_WIN, LANES), 0) < B_PREV

        def pair_body(p, carry):
            for hh in range(2):
                t = jnp.where(in_band, _toeplitz(vec_ref[0, 2 * p + hh], B_WIN)[:, :LANES], MASKED)
                lanes = slice(hh * LANES, (hh + 1) * LANES)
                bias_ref[2 * p, :, lanes] = t.astype(jnp.bfloat16)
                bias_ref[2 * p + 1, :, lanes] = jnp.where(before_block, MASKED, t).astype(jnp.bfloat16)
            return carry

        lax.fori_loop(0, N_PAIRS, pair_body, 0)

    rows = lax.broadcasted_iota(jnp.int32, (PAIR_W, TQ), 0)
    pairs_per_kv = N_PAIRS // B_KV_HEADS
    eye = _identity_bf16(B_SUB)

    def scores_fn(p, u, s_ref):
        qlanes = slice(u * LANES, (u + 1) * LANES)
        q2 = jnp.concatenate([qt_ref[0, p, :HEAD_DIM, qlanes], qt_ref[0, p, HEAD_DIM:, qlanes]], axis=1)
        kv0 = HEAD_DIM * (p // pairs_per_kv)
        kv_rows = (rows >= kv0) & (rows < kv0 + HEAD_DIM)
        qz = jnp.where(kv_rows, jnp.concatenate([q2, q2], axis=0), jnp.zeros((PAIR_W, TQ), q2.dtype))
        if u == 0:
            k_tile = jnp.concatenate([kp_ref[0, TQ - B_PREV:, :], ko_ref[0, :LANES, :]], axis=0)
            variant = jnp.where(i == 0, 1, 0)
        else:
            k_tile, variant = ko_ref[0], 0

        lhs, rhs = [], []
        for r0 in range(0, B_WIN, B_SUB):
            lhs.append(jnp.concatenate([k_tile[r0:r0 + B_SUB, :], eye], axis=1))
            rhs.append(jnp.concatenate([qz, bias_ref[2 * p + variant, r0:r0 + B_SUB, :]], axis=0))
        sub_live = (((0, B_SUB),) * (TQ // LANES),) * len(lhs)
        return _scores_stage(lhs, rhs, [None] * len(lhs), sub_live, s_ref)

    def pv_fn(p, u, s_ref, m):
        qlanes = slice(u * LANES, (u + 1) * LANES)
        kv_rows = slice(HEAD_DIM * (p // pairs_per_kv), HEAD_DIM * (p // pairs_per_kv + 1))
        if u == 0:
            v_tile = jnp.concatenate([vp_ref[0, kv_rows, TQ - B_PREV:], vo_ref[0, kv_rows, :LANES]], axis=1)
        else:
            v_tile = vo_ref[0, kv_rows, :]
        o = _pv_stage(s_ref, m, [v_tile], B_LIVE, extra_logit=sink_ref[p])
        for hh in range(2):
            hrows = slice(hh * HEAD_DIM, (hh + 1) * HEAD_DIM)
            gate = gt_ref[0, p, hrows, qlanes].astype(jnp.float32)
            row0 = p * PAIR_W + hh * HEAD_DIM
            zt_ref[row0:row0 + HEAD_DIM, qlanes] = (o[:, hh * LANES:(hh + 1) * LANES] * gate).astype(jnp.bfloat16)

    _head_pipeline(N_HEADS, scores_fn, pv_fn, s_refs, scores_first=False)

    h2 = _out_and_residual(zt_ref, wout_ref, h_ref[0])
    out_ref[0] = (h2 * fg_ref[...]) * _rms_scale(h2)


def _attn_b(h, qbt, ksh, vsht, gbt, vec, sinks, wout, fg):
    b, s, d = h.shape
    qt4 = qbt.reshape(b, N_PAIRS, PAIR_W, s)
    gt4 = gbt.reshape(b, N_PAIRS, PAIR_W, s)
    feat_spec = pl.BlockSpec((1, N_PAIRS, PAIR_W, TQ), lambda bi, i: (bi, 0, 0, i))
    return pl.pallas_call(
        _attn_b_kernel,
        grid=(b, s // TQ),
        in_specs=[
            pl.BlockSpec((1, TQ, d), lambda bi, i: (bi, i, 0)),
            feat_spec,
            pl.BlockSpec((1, TQ, PAIR_W), lambda bi, i: (bi, jnp.maximum(i - 1, 0), 0)),
            pl.BlockSpec((1, TQ, PAIR_W), lambda bi, i: (bi, i, 0)),
            pl.BlockSpec((1, PAIR_W, TQ), lambda bi, i: (bi, 0, jnp.maximum(i - 1, 0))),
            pl.BlockSpec((1, PAIR_W, TQ), lambda bi, i: (bi, 0, i)),
            feat_spec,
            _const_spec(vec.shape),
            _const_spec(sinks.shape),
            _const_spec(wout.shape),
            _const_spec(fg.shape),
        ],
        out_specs=pl.BlockSpec((1, TQ, d), lambda bi, i: (bi, i, 0)),
        out_shape=jax.ShapeDtypeStruct((b, s, d), jnp.float32),
        scratch_shapes=[pltpu.VMEM((d, TQ), jnp.bfloat16),
                        pltpu.VMEM((2 * N_PAIRS, B_WIN, TQ), jnp.bfloat16),
                        *[pltpu.VMEM((B_WIN, TQ), jnp.float32)] * N_SCORE_BUFS_B],
        compiler_params=pltpu.CompilerParams(
            dimension_semantics=("arbitrary", "arbitrary"), vmem_limit_bytes=VMEM_LIMIT),
        name="attn_b",
    )(h, qt4, ksh, ksh, vsht, vsht, gt4, vec, sinks, wout, fg)


def _slab_diff():
    u = jnp.arange(BIAS_PERIOD)
    return jnp.where(u < TQ, u, u - BIAS_PERIOD)


def _bias_vecs_a(rel_bias):
    back = (A_KBLOCKS - 1 - jnp.arange(A_KBLOCKS))[:, None] * TQ
    dist = back + _slab_diff()[None, :]
    idx = jnp.clip(dist, -A_REL_CLIP, A_REL_CLIP) + A_REL_CLIP
    vec = jnp.transpose(rel_bias[idx], (0, 2, 1)).astype(jnp.float32)
    return vec[:, :, None, :]


def _t5_bucket(rel):
    nb = T5_BUCKETS // 2
    max_exact = nb // 2
    ret = jnp.where(rel > 0, nb, 0)
    n = jnp.abs(rel)
    nf = jnp.maximum(n, 1).astype(jnp.float32)
    large = max_exact + (jnp.log(nf / max_exact) / math.log(T5_MAX_DIST / max_exact)
                         * (nb - max_exact)).astype(jnp.int32)
    large = jnp.minimum(large, nb - 1)
    return ret + jnp.where(n < max_exact, n, large)


def _bias_vec_b(t5_table):
    rel = -_slab_diff() - B_PREV
    vec = jnp.transpose(t5_table[_t5_bucket(rel)], (1, 0)).astype(jnp.float32)
    return vec[None, :, None, :]


def kernel(x, a_norm, a_w_in, a_rel_bias, a_w_out, kv_norm, kv_w, t5_bias,
           b_norm, b_w_in, b_sinks, b_w_out, final_norm):
    assert a_norm.shape[0] == 1 and b_norm.shape[0] == 1, "one A layer then one B layer"
    bf = jnp.bfloat16
    scale = HEAD_DIM ** -0.5 * LOG2E

    def q_col_scale(n):
        return jnp.where(jnp.arange(n) < D_MODEL, scale, 1.0).astype(jnp.float32)

    wt_a = _weight_t(a_w_in[0], q_col_scale(a_w_in.shape[2]), a_norm[0], W_T_COLS_A)
    qt, k, vt, gt = _proj_a(x, wt_a)

    nwt = _weight_t(b_w_in[0], q_col_scale(b_w_in.shape[2]), b_norm[0], W_T_COLS_B,
                    head=kv_w, head_block=1, head_w=PAIR_W, head_gain=kv_norm)
    kvw_k = (kv_w[:, :PAIR_W] * kv_norm[:, None]).astype(bf)
    h, ksh, vsht, qbt, gbt = _attn_a(
        x, qt, k, vt, gt, _bias_vecs_a(a_rel_bias[0] * LOG2E), a_w_out[0].astype(bf), kvw_k, nwt)

    sinks = jnp.repeat((b_sinks[0].astype(jnp.float32) * LOG2E).reshape(N_PAIRS, 1, 2), LANES, axis=2)
    return _attn_b(h, qbt, ksh, vsht, gbt, _bias_vec_b(t5_bias * LOG2E), sinks, b_w_out[0].astype(bf),
                   final_norm[None, :])
```

```python
import functools
import math

import jax
import jax.numpy as jnp
from jax import lax
from jax.experimental import pallas as pl
from jax.experimental.pallas import tpu as pltpu

D_MODEL = 1024
HEAD_DIM = 64
N_HEADS = D_MODEL // HEAD_DIM
N_PAIRS = N_HEADS // 2
PAIR_W = 2 * HEAD_DIM
CHUNK = 64
RMS_EPS = 1e-6
A_LEFT_CHUNKS = 8
A_REL_CLIP = 256
B_KV_HEADS = 2
B_LEFT_CHUNKS = 2
T5_BUCKETS = 32
T5_MAX_DIST = 128

TQ = 256
A_KBLOCKS = A_LEFT_CHUNKS * CHUNK // TQ + 1
B_PREV = B_LEFT_CHUNKS * CHUNK
TM_PROJ = 1024
FEAT_CHUNK = 256
W_T_COLS_A = 1024
W_T_COLS_B = 512
MASKED = -1e30
BIAS_PERIOD = 2 * TQ
SUBLANES = 8
LANES = 128
MAX_CHAINS = 1
BF16_ROWS = 16
LOG2E = math.log2(math.e)
N_SCORE_BUFS = 4
S_SKEW = 8
S_STRIDE = A_KBLOCKS * TQ + 16
N_SCORE_BUFS_B = 8
VMEM_LIMIT = 56 * 1024 * 1024

_NT = (((1,), (1,)), ((), ()))
_TN = (((0,), (0,)), ((), ()))


def _rms_scale(xf):
    return lax.rsqrt(jnp.mean(xf * xf, axis=-1, keepdims=True) + RMS_EPS)


def _token_scales(xf):
    col = _rms_scale(xf)
    row = jnp.transpose(jnp.broadcast_to(col, (xf.shape[0], LANES)))[0:1, :]
    return col, row


def _silu(v):
    return v * jax.nn.sigmoid(v)


def _weight_t_kernel(*refs, has_head):
    if has_head:
        head_ref, hgain_ref, w_ref, cs_ref, gain_ref, o_ref = refs
        j = pl.program_id(0)

        @pl.when(j == 0)
        def _():
            pad = o_ref.shape[0] - head_ref.shape[1]
            o_ref[:pad, :] = jnp.zeros((pad, o_ref.shape[1]), jnp.bfloat16)
            o_ref[pad:, :] = (head_ref[...].T * hgain_ref[...]).astype(jnp.bfloat16)

        @pl.when(j > 0)
        def _():
            o_ref[...] = ((w_ref[...] * cs_ref[...]).T * gain_ref[...]).astype(jnp.bfloat16)
    else:
        w_ref, cs_ref, gain_ref, o_ref = refs
        o_ref[...] = ((w_ref[...] * cs_ref[...]).T * gain_ref[...]).astype(jnp.bfloat16)


def _weight_t(w, col_scale, row_gain, tc, head=None, head_block=0, head_w=None, head_gain=None):
    d, n = w.shape
    off = 0 if head is None else 1
    main_specs = [
        pl.BlockSpec((d, tc), lambda j: (0, jnp.maximum(j - off, 0))),
        pl.BlockSpec((1, tc), lambda j: (0, jnp.maximum(j - off, 0))),
        pl.BlockSpec((1, d), lambda j: (0, 0)),
    ]
    head_specs, head_args = [], []
    if head is not None:
        assert head.shape[0] == d and head.shape[1] % head_w == 0 and head_w <= tc
        head_specs = [pl.BlockSpec((d, head_w), lambda j: (0, head_block)), pl.BlockSpec((1, d), lambda j: (0, 0))]
        head_args = [head, head_gain[None, :]]
    return pl.pallas_call(
        functools.partial(_weight_t_kernel, has_head=head is not None),
        grid=(n // tc + off,),
        in_specs=head_specs + main_specs,
        out_specs=pl.BlockSpec((tc, d), lambda j: (j, 0)),
        out_shape=jax.ShapeDtypeStruct((n + off * tc, d), jnp.bfloat16),
        compiler_params=pltpu.CompilerParams(dimension_semantics=("arbitrary",), vmem_limit_bytes=VMEM_LIMIT),
        name="weight_t",
    )(*head_args, w, col_scale[None, :], row_gain[None, :])


def _proj_a_kernel(x_ref, wt_ref, qt_ref, k_ref, vt_ref, gt_ref):
    xf = x_ref[0]
    xb = xf.astype(jnp.bfloat16)
    r_col, r_row = _token_scales(xf)
    k = lax.dot_general(xb, wt_ref[D_MODEL:2 * D_MODEL, :], _NT,
                        preferred_element_type=jnp.float32) * r_col
    for p in range(N_PAIRS):
        k_ref[0, p] = k[:, p * PAIR_W:(p + 1) * PAIR_W].astype(jnp.bfloat16)
    for out_ref, base, act in ((gt_ref, 3 * D_MODEL, _silu), (vt_ref, 2 * D_MODEL, None), (qt_ref, 0, None)):
        for c in range(0, D_MODEL, FEAT_CHUNK):
            w = wt_ref[base + c:base + c + FEAT_CHUNK, :]
            yt = lax.dot_general(w, xb, _NT, preferred_element_type=jnp.float32) * r_row
            out_ref[0, c:c + FEAT_CHUNK, :] = (yt if act is None else act(yt)).astype(jnp.bfloat16)


def _proj_a(x, wt):
    b, s, d = x.shape
    feat = jax.ShapeDtypeStruct((b, d, s), jnp.bfloat16)
    return pl.pallas_call(
        _proj_a_kernel,
        grid=(b, s // TM_PROJ),
        in_specs=[
            pl.BlockSpec((1, TM_PROJ, d), lambda bi, i: (bi, i, 0)),
            _const_spec(wt.shape),
        ],
        out_specs=[
            pl.BlockSpec((1, d, TM_PROJ), lambda bi, i: (bi, 0, i)),
            pl.BlockSpec((1, N_PAIRS, TM_PROJ, PAIR_W), lambda bi, i: (bi, 0, i, 0)),
            pl.BlockSpec((1, d, TM_PROJ), lambda bi, i: (bi, 0, i)),
            pl.BlockSpec((1, d, TM_PROJ), lambda bi, i: (bi, 0, i)),
        ],
        out_shape=[feat, jax.ShapeDtypeStruct((b, N_PAIRS, s, PAIR_W), jnp.bfloat16), feat, feat],
        compiler_params=pltpu.CompilerParams(
            dimension_semantics=("arbitrary", "arbitrary"), vmem_limit_bytes=VMEM_LIMIT),
        name="proj_a",
    )(x, wt)


def _chunk_iota(shape, axis):
    return lax.shift_right_logical(lax.broadcasted_iota(jnp.int32, shape, axis), CHUNK.bit_length() - 1)


def _identity_bf16(n):
    return (lax.broadcasted_iota(jnp.int32, (n, n), 0) == lax.broadcasted_iota(jnp.int32, (n, n), 1)).astype(jnp.bfloat16)


def _toeplitz(vec_row, n_keys):
    return pltpu.roll(jnp.broadcast_to(vec_row, (n_keys, BIAS_PERIOD)), 0, 1, stride=1, stride_axis=0)


def _fill_bias_slabs(vec_ref, bias_ref, n_keys, bands):
    n = len(bands)
    kch = _chunk_iota((n_keys, TQ), 0)
    qch = _chunk_iota((n_keys, TQ), 1)

    def head_body(h, carry):
        for j, band in enumerate(bands):
            t = _toeplitz(vec_ref[j, h], n_keys)[:, :TQ]
            if band is not None:
                t = jnp.where(band(kch, qch), t, MASKED)
            bias_ref[h * n + j] = t
        return carry

    lax.fori_loop(0, N_HEADS, head_body, 0)


def _live_rows(band, n_keys):
    chunks_per_half = LANES // CHUNK
    out = []
    for half in range(TQ // LANES):
        qchs = range(half * chunks_per_half, (half + 1) * chunks_per_half)
        live = [kc for kc in range(n_keys // CHUNK) if band is None or any(band(kc, qc) for qc in qchs)]
        out.append((live[0] * CHUNK, (live[-1] + 1) * CHUNK) if live else (0, 0))
    return tuple(out)


def _scores_stage(k_tiles, q_tiles, bias_fns, live, s_ref):
    maxes, row = [], 0
    for half in range(TQ // LANES):
        maxes.append([None] * MAX_CHAINS)
    for kt, qz, bias_fn, live_j in zip(k_tiles, q_tiles, bias_fns, live):
        s = jnp.dot(kt, qz, preferred_element_type=jnp.float32)
        for r in range(s.shape[0] // SUBLANES):
            rows = slice(r * SUBLANES, (r + 1) * SUBLANES)
            for half, (r0, r1) in enumerate(live_j):
                if not r0 <= r * SUBLANES < r1:
                    continue
                lanes = slice(half * LANES, (half + 1) * LANES)
                acc = maxes[half]
                grp = s[rows, lanes] if bias_fn is None else s[rows, lanes] + bias_fn(rows, lanes)
                s_ref[row + r * SUBLANES:row + (r + 1) * SUBLANES, lanes] = grp
                c = r % MAX_CHAINS
                acc[c] = grp if acc[c] is None else jnp.maximum(acc[c], grp)
        row += s.shape[0]
    cols = [jnp.max(functools.reduce(jnp.maximum, [a for a in acc if a is not None]), axis=0, keepdims=True)
            for acc in maxes]
    return jnp.concatenate(cols, axis=1)


def _pv_stage(s_ref, m, v_tiles, live, extra_logit=None):
    if extra_logit is not None:
        m = jnp.maximum(m, extra_logit)
    acc, row = None, 0
    for vt, live_j in zip(v_tiles, live):
        n = vt.shape[1]
        halves = []
        for half, (r0, r1) in enumerate(live_j):
            lanes = slice(half * LANES, (half + 1) * LANES)
            parts = [jnp.zeros((r0, LANES), jnp.bfloat16)] if r0 else []
            if r1 > r0:
                parts.append(jnp.exp2(s_ref[row + r0:row + r1, lanes] - m[:, lanes]).astype(jnp.bfloat16))
            if n > r1:
                parts.append(jnp.zeros((n - r1, LANES), jnp.bfloat16))
            halves.append(parts[0] if len(parts) == 1 else jnp.concatenate(parts, axis=0))
        p = jnp.concatenate(halves, axis=1)
        v_ones = jnp.concatenate([vt, jnp.ones((BF16_ROWS, n), vt.dtype)], axis=0)
        part = jnp.dot(v_ones, p, preferred_element_type=jnp.float32)
        acc = part if acc is None else acc + part
        row += n
    l = acc[HEAD_DIM:HEAD_DIM + 1, :]
    if extra_logit is not None:
        l = l + jnp.exp2(extra_logit - m)
    return acc[:HEAD_DIM, :] / l


def _head_pipeline(n_tiles, scores_fn, pv_fn, s_refs, scores_first):
    per_group = len(s_refs) // 2
    x_refs, y_refs = s_refs[:per_group], s_refs[per_group:]
    n_groups = n_tiles // per_group

    m = [scores_fn(t // 2, t % 2, ref) for t, ref in zip(range(per_group), x_refs)]
    for g in range(n_groups):
        src, dst = (x_refs, y_refs) if g % 2 == 0 else (y_refs, x_refs)
        nxt = [(g + 1) * per_group + j for j in range(per_group)] if g + 1 < n_groups else []
        m_next = [scores_fn(t // 2, t % 2, ref) for t, ref in zip(nxt, dst)] if scores_first else []
        for j in range(per_group):
            if not scores_first and nxt:
                m_next.append(scores_fn(nxt[j] // 2, nxt[j] % 2, dst[j]))
            t = g * per_group + j
            pv_fn(t // 2, t % 2, src[j], m[j])
        m = m_next


A_BANDS = (lambda kc, qc: kc >= qc,
           None,
           lambda kc, qc: kc <= qc)
A_LIVE = tuple(_live_rows(band, TQ) for band in A_BANDS)
B_WIN = B_PREV + LANES
B_SUB = PAIR_W
B_BAND = lambda kc, qc: (kc >= qc) & (kc <= qc + B_LEFT_CHUNKS)
B_LIVE = (((0, B_WIN),) * (TQ // LANES),)


def _out_and_residual(zt_ref, wout_ref, res):
    y = lax.dot_general(zt_ref[...], wout_ref[...], _TN, preferred_element_type=jnp.float32)
    return res + y


def _attn_a_kernel(x_ref, qt_ref, k0_ref, k1_ref, k2_ref, v0_ref, v1_ref, v2_ref, gt_ref,
                   bvec_ref, wout_ref, kvwk_ref, nwt_ref,
                   h_ref, ksh_ref, vsht_ref, qbt_ref, gbt_ref, zt_ref, bias_ref, s_all_ref):
    i = pl.program_id(1)
    s_refs = [s_all_ref.at[pl.ds(S_SKEW + n * S_STRIDE, A_KBLOCKS * TQ)] for n in range(N_SCORE_BUFS)]

    @pl.when((pl.program_id(0) == 0) & (i == 0))
    def _():
        _fill_bias_slabs(bvec_ref, bias_ref, TQ, A_BANDS)
        bias_ref[N_HEADS * A_KBLOCKS] = jnp.full((TQ, TQ), MASKED, jnp.float32)

    k_refs = (k0_ref, k1_ref, k2_ref)
    v_refs = (v0_ref, v1_ref, v2_ref)
    n_slabs = N_HEADS * A_KBLOCKS
    rows = lax.broadcasted_iota(jnp.int32, (PAIR_W, TQ), 0)

    def scores_fn(p, hh, s_ref):
        q2 = qt_ref[0, p]
        own_rows = rows < HEAD_DIM if hh == 0 else rows >= HEAD_DIM
        qz = jnp.where(own_rows, q2, jnp.zeros_like(q2))
        k_tiles, bias_fns = [], []
        for j in range(A_KBLOCKS):
            in_seq = i - (A_KBLOCKS - 1) + j >= 0
            slab = jnp.where(in_seq, (2 * p + hh) * A_KBLOCKS + j, n_slabs)
            k_tiles.append(k_refs[j][0, p])
            bias_fns.append(lambda rws, lanes, slab=slab: bias_ref[slab, rws, lanes])
        return _scores_stage(k_tiles, [qz] * A_KBLOCKS, bias_fns, A_LIVE, s_ref)

    def pv_fn(p, hh, s_ref, m):
        half = slice(hh * HEAD_DIM, (hh + 1) * HEAD_DIM)
        o = _pv_stage(s_ref, m, [v_ref[0, p, half, :] for v_ref in v_refs], A_LIVE)
        gate = gt_ref[0, p, half, :].astype(jnp.float32)
        row0 = p * PAIR_W + hh * HEAD_DIM
        zt_ref[row0:row0 + HEAD_DIM, :] = (o * gate).astype(jnp.bfloat16)

    _head_pipeline(N_HEADS, scores_fn, pv_fn, s_refs, scores_first=True)

    h = _out_and_residual(zt_ref, wout_ref, x_ref[0])
    h_ref[0] = h
    hb = h.astype(jnp.bfloat16)
    r_col, r_row = _token_scales(h)
    ksh_ref[0] = (jnp.dot(hb, kvwk_ref[...], preferred_element_type=jnp.float32) * r_col).astype(jnp.bfloat16)
    for out_ref, base, act in ((gbt_ref, W_T_COLS_B + D_MODEL, _silu), (qbt_ref, W_T_COLS_B, None)):
        for c in range(0, D_MODEL, FEAT_CHUNK):
            first = out_ref is qbt_ref and c == 0
            start = base + c - (PAIR_W if first else 0)
            yt = lax.dot_general(nwt_ref[start:base + c + FEAT_CHUNK, :], hb, _NT,
                                 preferred_element_type=jnp.float32) * r_row
            if first:
                vsht_ref[0] = yt[:PAIR_W].astype(jnp.bfloat16)
                yt = yt[PAIR_W:]
            out_ref[0, c:c + FEAT_CHUNK, :] = (yt if act is None else act(yt)).astype(jnp.bfloat16)


def _const_spec(shape):
    return pl.BlockSpec(shape, lambda bi, i: (0,) * len(shape), pipeline_mode=pl.Buffered(1))


def _attn_a(x, qt, k, vt, gt, bvec, wout, kvwk, nwt):
    b, s, d = x.shape
    qt4 = qt.reshape(b, N_PAIRS, PAIR_W, s)
    vt4 = vt.reshape(b, N_PAIRS, PAIR_W, s)
    gt4 = gt.reshape(b, N_PAIRS, PAIR_W, s)
    back = A_KBLOCKS - 1

    def kspec(j):
        return pl.BlockSpec((1, N_PAIRS, TQ, PAIR_W),
                            lambda bi, i: (bi, 0, jnp.maximum(i - back + j, 0), 0))

    def vspec(j):
        return pl.BlockSpec((1, N_PAIRS, PAIR_W, TQ),
                            lambda bi, i: (bi, 0, 0, jnp.maximum(i - back + j, 0)))

    feat_spec = pl.BlockSpec((1, N_PAIRS, PAIR_W, TQ), lambda bi, i: (bi, 0, 0, i))
    featout_spec = pl.BlockSpec((1, d, TQ), lambda bi, i: (bi, 0, i))
    feat = jax.ShapeDtypeStruct((b, d, s), jnp.bfloat16)
    return pl.pallas_call(
        _attn_a_kernel,
        grid=(b, s // TQ),
        in_specs=[
            pl.BlockSpec((1, TQ, d), lambda bi, i: (bi, i, 0)),
            feat_spec,
            kspec(0), kspec(1), kspec(2),
            vspec(0), vspec(1), vspec(2),
            feat_spec,
            _const_spec(bvec.shape),
            _const_spec(wout.shape),
            _const_spec(kvwk.shape),
            _const_spec(nwt.shape),
        ],
        out_specs=[
            pl.BlockSpec((1, TQ, d), lambda bi, i: (bi, i, 0)),
            pl.BlockSpec((1, TQ, PAIR_W), lambda bi, i: (bi, i, 0)),
            pl.BlockSpec((1, PAIR_W, TQ), lambda bi, i: (bi, 0, i)),
            featout_spec,
            featout_spec,
        ],
        out_shape=[
            jax.ShapeDtypeStruct((b, s, d), jnp.float32),
            jax.ShapeDtypeStruct((b, s, PAIR_W), jnp.bfloat16),
            jax.ShapeDtypeStruct((b, PAIR_W, s), jnp.bfloat16),
            feat, feat,
        ],
        scratch_shapes=[pltpu.VMEM((d, TQ), jnp.bfloat16),
                        pltpu.VMEM((N_HEADS * A_KBLOCKS + 1, TQ, TQ), jnp.float32),
                        pltpu.VMEM((S_SKEW + N_SCORE_BUFS * S_STRIDE, TQ), jnp.float32)],
        compiler_params=pltpu.CompilerParams(
            dimension_semantics=("arbitrary", "arbitrary"), vmem_limit_bytes=VMEM_LIMIT),
        name="attn_a",
    )(x, qt4, k, k, k, vt4, vt4, vt4, gt4, bvec, wout, kvwk, nwt)


def _attn_b_kernel(h_ref, qt_ref, kp_ref, ko_ref, vp_ref, vo_ref, gt_ref, vec_ref,
                   sink_ref, wout_ref, fg_ref, out_ref, zt_ref, bias_ref, *s_refs):
    i = pl.program_id(1)

    @pl.when((pl.program_id(0) == 0) & (i == 0))
    def _():
        in_band = B_BAND(_chunk_iota((B_WIN, LANES), 0), _chunk_iota((B_WIN, LANES), 1))
        before_block = lax.broadcasted_iota(jnp.int32, (B_WIN, LANES), 0) < B_PREV

        def pair_body(p, carry):
            for hh in range(2):
                t = jnp.where(in_band, _toeplitz(vec_ref[0, 2 * p + hh], B_WIN)[:, :LANES], MASKED)
                lanes = slice(hh * LANES, (hh + 1) * LANES)
                bias_ref[2 * p, :, lanes] = t.astype(jnp.bfloat16)
                bias_ref[2 * p + 1, :, lanes] = jnp.where(before_block, MASKED, t).astype(jnp.bfloat16)
            return carry

        lax.fori_loop(0, N_PAIRS, pair_body, 0)

    rows = lax.broadcasted_iota(jnp.int32, (PAIR_W, TQ), 0)
    pairs_per_kv = N_PAIRS // B_KV_HEADS
    eye = _identity_bf16(B_SUB)

    def scores_fn(p, u, s_ref):
        qlanes = slice(u * LANES, (u + 1) * LANES)
        q2 = jnp.concatenate([qt_ref[0, p, :HEAD_DIM, qlanes], qt_ref[0, p, HEAD_DIM:, qlanes]], axis=1)
        kv0 = HEAD_DIM * (p // pairs_per_kv)
        kv_rows = (rows >= kv0) & (rows < kv0 + HEAD_DIM)
        qz = jnp.where(kv_rows, jnp.concatenate([q2, q2], axis=0), jnp.zeros((PAIR_W, TQ), q2.dtype))
        if u == 0:
            k_tile = jnp.concatenate([kp_ref[0, TQ - B_PREV:, :], ko_ref[0, :LANES, :]], axis=0)
            variant = jnp.where(i == 0, 1, 0)
        else:
            k_tile, variant = ko_ref[0], 0

        lhs, rhs = [], []
        for r0 in range(0, B_WIN, B_SUB):
            lhs.append(jnp.concatenate([k_tile[r0:r0 + B_SUB, :], eye], axis=1))
            rhs.append(jnp.concatenate([qz, bias_ref[2 * p + variant, r0:r0 + B_SUB, :]], axis=0))
        sub_live = (((0, B_SUB),) * (TQ // LANES),) * len(lhs)
        return _scores_stage(lhs, rhs, [None] * len(lhs), sub_live, s_ref)

    def pv_fn(p, u, s_ref, m):
        qlanes = slice(u * LANES, (u + 1) * LANES)
        kv_rows = slice(HEAD_DIM * (p // pairs_per_kv), HEAD_DIM * (p // pairs_per_kv + 1))
        if u == 0:
            v_tile = jnp.concatenate([vp_ref[0, kv_rows, TQ - B_PREV:], vo_ref[0, kv_rows, :LANES]], axis=1)
        else:
            v_tile = vo_ref[0, kv_rows, :]
        o = _pv_stage(s_ref, m, [v_tile], B_LIVE, extra_logit=sink_ref[p])
        for hh in range(2):
            hrows = slice(hh * HEAD_DIM, (hh + 1) * HEAD_DIM)
            gate = gt_ref[0, p, hrows, qlanes].astype(jnp.float32)
            row0 = p * PAIR_W + hh * HEAD_DIM
            zt_ref[row0:row0 + HEAD_DIM, qlanes] = (o[:, hh * LANES:(hh + 1) * LANES] * gate).astype(jnp.bfloat16)

    _head_pipeline(N_HEADS, scores_fn, pv_fn, s_refs, scores_first=False)

    h2 = _out_and_residual(zt_ref, wout_ref, h_ref[0])
    out_ref[0] = (h2 * fg_ref[...]) * _rms_scale(h2)


def _attn_b(h, qbt, ksh, vsht, gbt, vec, sinks, wout, fg):
    b, s, d = h.shape
    qt4 = qbt.reshape(b, N_PAIRS, PAIR_W, s)
    gt4 = gbt.reshape(b, N_PAIRS, PAIR_W, s)
    feat_spec = pl.BlockSpec((1, N_PAIRS, PAIR_W, TQ), lambda bi, i: (bi, 0, 0, i))
    return pl.pallas_call(
        _attn_b_kernel,
        grid=(b, s // TQ),
        in_specs=[
            pl.BlockSpec((1, TQ, d), lambda bi, i: (bi, i, 0)),
            feat_spec,
            pl.BlockSpec((1, TQ, PAIR_W), lambda bi, i: (bi, jnp.maximum(i - 1, 0), 0)),
            pl.BlockSpec((1, TQ, PAIR_W), lambda bi, i: (bi, i, 0)),
            pl.BlockSpec((1, PAIR_W, TQ), lambda bi, i: (bi, 0, jnp.maximum(i - 1, 0))),
            pl.BlockSpec((1, PAIR_W, TQ), lambda bi, i: (bi, 0, i)),
            feat_spec,
            _const_spec(vec.shape),
            _const_spec(sinks.shape),
            _const_spec(wout.shape),
            _const_spec(fg.shape),
        ],
        out_specs=pl.BlockSpec((1, TQ, d), lambda bi, i: (bi, i, 0)),
        out_shape=jax.ShapeDtypeStruct((b, s, d), jnp.float32),
        scratch_shapes=[pltpu.VMEM((d, TQ), jnp.bfloat16),
                        pltpu.VMEM((2 * N_PAIRS, B_WIN, TQ), jnp.bfloat16),
                        *[pltpu.VMEM((B_WIN, TQ), jnp.float32)] * N_SCORE_BUFS_B],
        compiler_params=pltpu.CompilerParams(
            dimension_semantics=("arbitrary", "arbitrary"), vmem_limit_bytes=VMEM_LIMIT),
        name="attn_b",
    )(h, qt4, ksh, ksh, vsht, vsht, gt4, vec, sinks, wout, fg)


def _slab_diff():
    u = jnp.arange(BIAS_PERIOD)
    return jnp.where(u < TQ, u, u - BIAS_PERIOD)


def _bias_vecs_a(rel_bias):
    back = (A_KBLOCKS - 1 - jnp.arange(A_KBLOCKS))[:, None] * TQ
    dist = back + _slab_diff()[None, :]
    idx = jnp.clip(dist, -A_REL_CLIP, A_REL_CLIP) + A_REL_CLIP
    vec = jnp.transpose(rel_bias[idx], (0, 2, 1)).astype(jnp.float32)
    return vec[:, :, None, :]


def _t5_bucket(rel):
    nb = T5_BUCKETS // 2
    max_exact = nb // 2
    ret = jnp.where(rel > 0, nb, 0)
    n = jnp.abs(rel)
    nf = jnp.maximum(n, 1).astype(jnp.float32)
    large = max_exact + (jnp.log(nf / max_exact) / math.log(T5_MAX_DIST / max_exact)
                         * (nb - max_exact)).astype(jnp.int32)
    large = jnp.minimum(large, nb - 1)
    return ret + jnp.where(n < max_exact, n, large)


def _bias_vec_b(t5_table):
    rel = -_slab_diff() - B_PREV
    vec = jnp.transpose(t5_table[_t5_bucket(rel)], (1, 0)).astype(jnp.float32)
    return vec[None, :, None, :]


def kernel(x, a_norm, a_w_in, a_rel_bias, a_w_out, kv_norm, kv_w, t5_bias,
           b_norm, b_w_in, b_sinks, b_w_out, final_norm):
    assert a_norm.shape[0] == 1 and b_norm.shape[0] == 1, "one A layer then one B layer"
    bf = jnp.bfloat16
    scale = HEAD_DIM ** -0.5 * LOG2E

    def q_col_scale(n):
        return jnp.where(jnp.arange(n) < D_MODEL, scale, 1.0).astype(jnp.float32)

    wt_a = _weight_t(a_w_in[0], q_col_scale(a_w_in.shape[2]), a_norm[0], W_T_COLS_A)
    qt, k, vt, gt = _proj_a(x, wt_a)

    nwt = _weight_t(b_w_in[0], q_col_scale(b_w_in.shape[2]), b_norm[0], W_T_COLS_B,
                    head=kv_w, head_block=1, head_w=PAIR_W, head_gain=kv_norm)
    kvw_k = (kv_w[:, :PAIR_W] * kv_norm[:, None]).astype(bf)
    h, ksh, vsht, qbt, gbt = _attn_a(
        x, qt, k, vt, gt, _bias_vecs_a(a_rel_bias[0] * LOG2E), a_w_out[0].astype(bf), kvw_k, nwt)

    sinks = jnp.repeat((b_sinks[0].astype(jnp.float32) * LOG2E).reshape(N_PAIRS, 1, 2), LANES, axis=2)
    return _attn_b(h, qbt, ksh, vsht, gbt, _bias_vec_b(t5_bias * LOG2E), sinks, b_w_out[0].astype(bf),
                   final_norm[None, :])
```

```python
import functools
import math

import jax
import jax.numpy as jnp
from jax import lax
from jax.experimental import pallas as pl
from jax.experimental.pallas import tpu as pltpu

D_MODEL = 1024
HEAD_DIM = 64
N_HEADS = D_MODEL // HEAD_DIM
N_PAIRS = N_HEADS // 2
PAIR_W = 2 * HEAD_DIM
CHUNK = 64
RMS_EPS = 1e-6
A_LEFT_CHUNKS = 8
A_REL_CLIP = 256
B_KV_HEADS = 2
B_LEFT_CHUNKS = 2
T5_BUCKETS = 32
T5_MAX_DIST = 128

TQ = 256
TQB = 512
A_KBLOCKS = A_LEFT_CHUNKS * CHUNK // TQ + 1
B_PREV = B_LEFT_CHUNKS * CHUNK
TM_PROJ = 1024
FEAT_CHUNK = 256
W_T_COLS_A = 1024
W_T_COLS_B = 512
MASKED = -1e30
BIAS_PERIOD = 2 * TQ
SUBLANES = 8
LANES = 128
MAX_CHAINS = 1
BF16_ROWS = 16
LOG2E = math.log2(math.e)
N_SCORE_BUFS = 4
N_SCORE_BUFS_B = 8
VMEM_LIMIT = 56 * 1024 * 1024

_NT = (((1,), (1,)), ((), ()))
_TN = (((0,), (0,)), ((), ()))


def _rms_scale(xf):
    return lax.rsqrt(jnp.mean(xf * xf, axis=-1, keepdims=True) + RMS_EPS)


def _token_scales(xf):
    col = _rms_scale(xf)
    row = jnp.transpose(jnp.broadcast_to(col, (xf.shape[0], LANES)))[0:1, :]
    return col, row


def _silu(v):
    return v * jax.nn.sigmoid(v)


def _weight_t_kernel(*refs, has_head):
    if has_head:
        head_ref, hgain_ref, w_ref, cs_ref, gain_ref, o_ref = refs
        j = pl.program_id(0)

        @pl.when(j == 0)
        def _():
            pad = o_ref.shape[0] - head_ref.shape[1]
            o_ref[:pad, :] = jnp.zeros((pad, o_ref.shape[1]), jnp.bfloat16)
            o_ref[pad:, :] = (head_ref[...].T * hgain_ref[...]).astype(jnp.bfloat16)

        @pl.when(j > 0)
        def _():
            o_ref[...] = ((w_ref[...] * cs_ref[...]).T * gain_ref[...]).astype(jnp.bfloat16)
    else:
        w_ref, cs_ref, gain_ref, o_ref = refs
        o_ref[...] = ((w_ref[...] * cs_ref[...]).T * gain_ref[...]).astype(jnp.bfloat16)


def _weight_t(w, col_scale, row_gain, tc, head=None, head_block=0, head_w=None, head_gain=None):
    d, n = w.shape
    off = 0 if head is None else 1
    main_specs = [
        pl.BlockSpec((d, tc), lambda j: (0, jnp.maximum(j - off, 0))),
        pl.BlockSpec((1, tc), lambda j: (0, jnp.maximum(j - off, 0))),
        pl.BlockSpec((1, d), lambda j: (0, 0)),
    ]
    head_specs, head_args = [], []
    if head is not None:
        assert head.shape[0] == d and head.shape[1] % head_w == 0 and head_w <= tc
        head_specs = [pl.BlockSpec((d, head_w), lambda j: (0, head_block)), pl.BlockSpec((1, d), lambda j: (0, 0))]
        head_args = [head, head_gain[None, :]]
    return pl.pallas_call(
        functools.partial(_weight_t_kernel, has_head=head is not None),
        grid=(n // tc + off,),
        in_specs=head_specs + main_specs,
        out_specs=pl.BlockSpec((tc, d), lambda j: (j, 0)),
        out_shape=jax.ShapeDtypeStruct((n + off * tc, d), jnp.bfloat16),
        compiler_params=pltpu.CompilerParams(dimension_semantics=("arbitrary",), vmem_limit_bytes=VMEM_LIMIT),
        name="weight_t",
    )(*head_args, w, col_scale[None, :], row_gain[None, :])


def _proj_a_kernel(x_ref, wt_ref, qt_ref, k_ref, vt_ref, gt_ref):
    xf = x_ref[0]
    xb = xf.astype(jnp.bfloat16)
    r_col, r_row = _token_scales(xf)
    k = lax.dot_general(xb, wt_ref[D_MODEL:2 * D_MODEL, :], _NT,
                        preferred_element_type=jnp.float32) * r_col
    for p in range(N_PAIRS):
        k_ref[0, p] = k[:, p * PAIR_W:(p + 1) * PAIR_W].astype(jnp.bfloat16)
    for out_ref, base, act in ((gt_ref, 3 * D_MODEL, _silu), (vt_ref, 2 * D_MODEL, None), (qt_ref, 0, None)):
        for c in range(0, D_MODEL, FEAT_CHUNK):
            w = wt_ref[base + c:base + c + FEAT_CHUNK, :]
            yt = lax.dot_general(w, xb, _NT, preferred_element_type=jnp.float32) * r_row
            out_ref[0, c:c + FEAT_CHUNK, :] = (yt if act is None else act(yt)).astype(jnp.bfloat16)


def _proj_a(x, wt):
    b, s, d = x.shape
    feat = jax.ShapeDtypeStruct((b, d, s), jnp.bfloat16)
    return pl.pallas_call(
        _proj_a_kernel,
        grid=(b, s // TM_PROJ),
        in_specs=[
            pl.BlockSpec((1, TM_PROJ, d), lambda bi, i: (bi, i, 0)),
            _const_spec(wt.shape),
        ],
        out_specs=[
            pl.BlockSpec((1, d, TM_PROJ), lambda bi, i: (bi, 0, i)),
            pl.BlockSpec((1, N_PAIRS, TM_PROJ, PAIR_W), lambda bi, i: (bi, 0, i, 0)),
            pl.BlockSpec((1, d, TM_PROJ), lambda bi, i: (bi, 0, i)),
            pl.BlockSpec((1, d, TM_PROJ), lambda bi, i: (bi, 0, i)),
        ],
        out_shape=[feat, jax.ShapeDtypeStruct((b, N_PAIRS, s, PAIR_W), jnp.bfloat16), feat, feat],
        compiler_params=pltpu.CompilerParams(
            dimension_semantics=("arbitrary", "arbitrary"), vmem_limit_bytes=VMEM_LIMIT),
        name="proj_a",
    )(x, wt)


def _chunk_iota(shape, axis):
    return lax.shift_right_logical(lax.broadcasted_iota(jnp.int32, shape, axis), CHUNK.bit_length() - 1)


def _identity_bf16(n):
    return (lax.broadcasted_iota(jnp.int32, (n, n), 0) == lax.broadcasted_iota(jnp.int32, (n, n), 1)).astype(jnp.bfloat16)


def _toeplitz(vec_row, n_keys):
    return pltpu.roll(jnp.broadcast_to(vec_row, (n_keys, BIAS_PERIOD)), 0, 1, stride=1, stride_axis=0)


def _fill_bias_slabs(vec_ref, bias_ref, n_keys, bands):
    n = len(bands)
    kch = _chunk_iota((n_keys, TQ), 0)
    qch = _chunk_iota((n_keys, TQ), 1)

    def head_body(h, carry):
        for j, band in enumerate(bands):
            t = _toeplitz(vec_ref[j, h], n_keys)[:, :TQ]
            if band is not None:
                t = jnp.where(band(kch, qch), t, MASKED)
            bias_ref[h * n + j] = t
        return carry

    lax.fori_loop(0, N_HEADS, head_body, 0)


def _live_rows(band, n_keys):
    chunks_per_half = LANES // CHUNK
    out = []
    for half in range(TQ // LANES):
        qchs = range(half * chunks_per_half, (half + 1) * chunks_per_half)
        live = [kc for kc in range(n_keys // CHUNK) if band is None or any(band(kc, qc) for qc in qchs)]
        out.append((live[0] * CHUNK, (live[-1] + 1) * CHUNK) if live else (0, 0))
    return tuple(out)


def _scores_stage(k_tiles, q_tiles, bias_fns, live, s_ref):
    maxes, row = [], 0
    for half in range(TQ // LANES):
        maxes.append([None] * MAX_CHAINS)
    for kt, qz, bias_fn, live_j in zip(k_tiles, q_tiles, bias_fns, live):
        s = jnp.dot(kt, qz, preferred_element_type=jnp.float32)
        for r in range(s.shape[0] // SUBLANES):
            rows = slice(r * SUBLANES, (r + 1) * SUBLANES)
            for half, (r0, r1) in enumerate(live_j):
                if not r0 <= r * SUBLANES < r1:
                    continue
                lanes = slice(half * LANES, (half + 1) * LANES)
                acc = maxes[half]
                grp = s[rows, lanes] if bias_fn is None else s[rows, lanes] + bias_fn(rows, lanes)
                s_ref[row + r * SUBLANES:row + (r + 1) * SUBLANES, lanes] = grp
                c = r % MAX_CHAINS
                acc[c] = grp if acc[c] is None else jnp.maximum(acc[c], grp)
        row += s.shape[0]
    cols = [jnp.max(functools.reduce(jnp.maximum, [a for a in acc if a is not None]), axis=0, keepdims=True)
            for acc in maxes]
    return jnp.concatenate(cols, axis=1)


def _pv_stage(s_ref, m, v_tiles, live, extra_logit=None):
    if extra_logit is not None:
        m = jnp.maximum(m, extra_logit)
    acc, row = None, 0
    for vt, live_j in zip(v_tiles, live):
        n = vt.shape[1]
        halves = []
        for half, (r0, r1) in enumerate(live_j):
            lanes = slice(half * LANES, (half + 1) * LANES)
            parts = [jnp.zeros((r0, LANES), jnp.bfloat16)] if r0 else []
            if r1 > r0:
                parts.append(jnp.exp2(s_ref[row + r0:row + r1, lanes] - m[:, lanes]).astype(jnp.bfloat16))
            if n > r1:
                parts.append(jnp.zeros((n - r1, LANES), jnp.bfloat16))
            halves.append(parts[0] if len(parts) == 1 else jnp.concatenate(parts, axis=0))
        p = jnp.concatenate(halves, axis=1)
        v_ones = jnp.concatenate([vt, jnp.ones((BF16_ROWS, n), vt.dtype)], axis=0)
        part = jnp.dot(v_ones, p, preferred_element_type=jnp.float32)
        acc = part if acc is None else acc + part
        row += n
    l = acc[HEAD_DIM:HEAD_DIM + 1, :]
    if extra_logit is not None:
        l = l + jnp.exp2(extra_logit - m)
    return acc[:HEAD_DIM, :] / l


def _head_pipeline(n_tiles, scores_fn, pv_fn, s_refs, scores_first, per_pair=2):
    per_group = len(s_refs) // 2
    x_refs, y_refs = s_refs[:per_group], s_refs[per_group:]
    n_groups = n_tiles // per_group

    m = [scores_fn(t // per_pair, t % per_pair, ref) for t, ref in zip(range(per_group), x_refs)]
    for g in range(n_groups):
        src, dst = (x_refs, y_refs) if g % 2 == 0 else (y_refs, x_refs)
        nxt = [(g + 1) * per_group + j for j in range(per_group)] if g + 1 < n_groups else []
        m_next = [scores_fn(t // per_pair, t % per_pair, ref) for t, ref in zip(nxt, dst)] if scores_first else []
        for j in range(per_group):
            if not scores_first and nxt:
                m_next.append(scores_fn(nxt[j] // per_pair, nxt[j] % per_pair, dst[j]))
            t = g * per_group + j
            pv_fn(t // per_pair, t % per_pair, src[j], m[j])
        m = m_next


A_BANDS = (lambda kc, qc: kc >= qc,
           None,
           lambda kc, qc: kc <= qc)
A_LIVE = tuple(_live_rows(band, TQ) for band in A_BANDS)
B_WIN = B_PREV + LANES
B_SUB = PAIR_W
B_BAND = lambda kc, qc: (kc >= qc) & (kc <= qc + B_LEFT_CHUNKS)
B_LIVE = (((0, B_WIN),) * (TQ // LANES),)


def _out_and_residual(zt_ref, wout_ref, res):
    y = lax.dot_general(zt_ref[...], wout_ref[...], _TN, preferred_element_type=jnp.float32)
    return res + y


def _attn_a_kernel(x_ref, qt_ref, k0_ref, k1_ref, k2_ref, v0_ref, v1_ref, v2_ref, gt_ref,
                   bvec_ref, wout_ref, kvwk_ref, nwt_ref,
                   h_ref, ksh_ref, vsht_ref, qbt_ref, gbt_ref, zt_ref, bias_ref, *s_refs):
    i = pl.program_id(1)

    @pl.when((pl.program_id(0) == 0) & (i == 0))
    def _():
        _fill_bias_slabs(bvec_ref, bias_ref, TQ, A_BANDS)
        bias_ref[N_HEADS * A_KBLOCKS] = jnp.full((TQ, TQ), MASKED, jnp.float32)

    k_refs = (k0_ref, k1_ref, k2_ref)
    v_refs = (v0_ref, v1_ref, v2_ref)
    n_slabs = N_HEADS * A_KBLOCKS
    rows = lax.broadcasted_iota(jnp.int32, (PAIR_W, TQ), 0)

    def scores_fn(p, hh, s_ref):
        q2 = qt_ref[0, p]
        own_rows = rows < HEAD_DIM if hh == 0 else rows >= HEAD_DIM
        qz = jnp.where(own_rows, q2, jnp.zeros_like(q2))
        k_tiles, bias_fns = [], []
        for j in range(A_KBLOCKS):
            in_seq = i - (A_KBLOCKS - 1) + j >= 0
            slab = jnp.where(in_seq, (2 * p + hh) * A_KBLOCKS + j, n_slabs)
            k_tiles.append(k_refs[j][0, p])
            bias_fns.append(lambda rws, lanes, slab=slab: bias_ref[slab, rws, lanes])
        return _scores_stage(k_tiles, [qz] * A_KBLOCKS, bias_fns, A_LIVE, s_ref)

    def pv_fn(p, hh, s_ref, m):
        half = slice(hh * HEAD_DIM, (hh + 1) * HEAD_DIM)
        o = _pv_stage(s_ref, m, [v_ref[0, p, half, :] for v_ref in v_refs], A_LIVE)
        gate = gt_ref[0, p, half, :].astype(jnp.float32)
        row0 = p * PAIR_W + hh * HEAD_DIM
        zt_ref[row0:row0 + HEAD_DIM, :] = (o * gate).astype(jnp.bfloat16)

    _head_pipeline(N_HEADS, scores_fn, pv_fn, s_refs, scores_first=True)

    h = _out_and_residual(zt_ref, wout_ref, x_ref[0])
    h_ref[0] = h
    hb = h.astype(jnp.bfloat16)
    r_col, r_row = _token_scales(h)
    ksh_ref[0] = (jnp.dot(hb, kvwk_ref[...], preferred_element_type=jnp.float32) * r_col).astype(jnp.bfloat16)
    for out_ref, base, act in ((gbt_ref, W_T_COLS_B + D_MODEL, _silu), (qbt_ref, W_T_COLS_B, None)):
        for c in range(0, D_MODEL, FEAT_CHUNK):
            first = out_ref is qbt_ref and c == 0
            start = base + c - (PAIR_W if first else 0)
            yt = lax.dot_general(nwt_ref[start:base + c + FEAT_CHUNK, :], hb, _NT,
                                 preferred_element_type=jnp.float32) * r_row
            if first:
                vsht_ref[0] = yt[:PAIR_W].astype(jnp.bfloat16)
                yt = yt[PAIR_W:]
            out_ref[0, c:c + FEAT_CHUNK, :] = (yt if act is None else act(yt)).astype(jnp.bfloat16)


def _const_spec(shape):
    return pl.BlockSpec(shape, lambda bi, i: (0,) * len(shape), pipeline_mode=pl.Buffered(1))


def _attn_a(x, qt, k, vt, gt, bvec, wout, kvwk, nwt):
    b, s, d = x.shape
    qt4 = qt.reshape(b, N_PAIRS, PAIR_W, s)
    vt4 = vt.reshape(b, N_PAIRS, PAIR_W, s)
    gt4 = gt.reshape(b, N_PAIRS, PAIR_W, s)
    back = A_KBLOCKS - 1

    def kspec(j):
        return pl.BlockSpec((1, N_PAIRS, TQ, PAIR_W),
                            lambda bi, i: (bi, 0, jnp.maximum(i - back + j, 0), 0))

    def vspec(j):
        return pl.BlockSpec((1, N_PAIRS, PAIR_W, TQ),
                            lambda bi, i: (bi, 0, 0, jnp.maximum(i - back + j, 0)))

    feat_spec = pl.BlockSpec((1, N_PAIRS, PAIR_W, TQ), lambda bi, i: (bi, 0, 0, i))
    featout_spec = pl.BlockSpec((1, d, TQ), lambda bi, i: (bi, 0, i))
    feat = jax.ShapeDtypeStruct((b, d, s), jnp.bfloat16)
    return pl.pallas_call(
        _attn_a_kernel,
        grid=(b, s // TQ),
        in_specs=[
            pl.BlockSpec((1, TQ, d), lambda bi, i: (bi, i, 0)),
            feat_spec,
            kspec(0), kspec(1), kspec(2),
            vspec(0), vspec(1), vspec(2),
            feat_spec,
            _const_spec(bvec.shape),
            _const_spec(wout.shape),
            _const_spec(kvwk.shape),
            _const_spec(nwt.shape),
        ],
        out_specs=[
            pl.BlockSpec((1, TQ, d), lambda bi, i: (bi, i, 0)),
            pl.BlockSpec((1, TQ, PAIR_W), lambda bi, i: (bi, i, 0)),
            pl.BlockSpec((1, PAIR_W, TQ), lambda bi, i: (bi, 0, i)),
            featout_spec,
            featout_spec,
        ],
        out_shape=[
            jax.ShapeDtypeStruct((b, s, d), jnp.float32),
            jax.ShapeDtypeStruct((b, s, PAIR_W), jnp.bfloat16),
            jax.ShapeDtypeStruct((b, PAIR_W, s), jnp.bfloat16),
            feat, feat,
        ],
        scratch_shapes=[pltpu.VMEM((d, TQ), jnp.bfloat16),
                        pltpu.VMEM((N_HEADS * A_KBLOCKS + 1, TQ, TQ), jnp.float32),
                        *[pltpu.VMEM((A_KBLOCKS * TQ, TQ), jnp.float32)] * N_SCORE_BUFS],
        compiler_params=pltpu.CompilerParams(
            dimension_semantics=("arbitrary", "arbitrary"), vmem_limit_bytes=VMEM_LIMIT),
        name="attn_a",
    )(x, qt4, k, k, k, vt4, vt4, vt4, gt4, bvec, wout, kvwk, nwt)


def _attn_b_kernel(h_ref, qt_ref, kp_ref, ko_ref, vp_ref, vo_ref, gt_ref, vec_ref,
                   sink_ref, wout_ref, fg_ref, out_ref, zt_ref, bias_ref, *s_refs):
    i = pl.program_id(1)

    @pl.when((pl.program_id(0) == 0) & (i == 0))
    def _():
        in_band = B_BAND(_chunk_iota((B_WIN, LANES), 0), _chunk_iota((B_WIN, LANES), 1))
        before_block = lax.broadcasted_iota(jnp.int32, (B_WIN, LANES), 0) < B_PREV

        def pair_body(p, carry):
            for hh in range(2):
                t = jnp.where(in_band, _toeplitz(vec_ref[0, 2 * p + hh], B_WIN)[:, :LANES], MASKED)
                lanes = slice(hh * LANES, (hh + 1) * LANES)
                bias_ref[2 * p, :, lanes] = t.astype(jnp.bfloat16)
                bias_ref[2 * p + 1, :, lanes] = jnp.where(before_block, MASKED, t).astype(jnp.bfloat16)
            return carry

        lax.fori_loop(0, N_PAIRS, pair_body, 0)

    rows = lax.broadcasted_iota(jnp.int32, (PAIR_W, TQ), 0)
    pairs_per_kv = N_PAIRS // B_KV_HEADS
    eye = _identity_bf16(B_SUB)

    def scores_fn(p, u, s_ref):
        qlanes = slice(u * LANES, (u + 1) * LANES)
        q2 = jnp.concatenate([qt_ref[0, p, :HEAD_DIM, qlanes], qt_ref[0, p, HEAD_DIM:, qlanes]], axis=1)
        kv0 = HEAD_DIM * (p // pairs_per_kv)
        kv_rows = (rows >= kv0) & (rows < kv0 + HEAD_DIM)
        qz = jnp.where(kv_rows, jnp.concatenate([q2, q2], axis=0), jnp.zeros((PAIR_W, TQ), q2.dtype))
        if u == 0:
            k_tile = jnp.concatenate([kp_ref[0], ko_ref[0, :LANES, :]], axis=0)
            variant = jnp.where(i == 0, 1, 0)
        else:
            k_tile, variant = ko_ref[0, u * LANES - B_PREV:(u + 1) * LANES, :], 0

        lhs, rhs = [], []
        for r0 in range(0, B_WIN, B_SUB):
            lhs.append(jnp.concatenate([k_tile[r0:r0 + B_SUB, :], eye], axis=1))
            rhs.append(jnp.concatenate([qz, bias_ref[2 * p + variant, r0:r0 + B_SUB, :]], axis=0))
        sub_live = (((0, B_SUB),) * (TQ // LANES),) * len(lhs)
        return _scores_stage(lhs, rhs, [None] * len(lhs), sub_live, s_ref)

    def pv_fn(p, u, s_ref, m):
        qlanes = slice(u * LANES, (u + 1) * LANES)
        kv_rows = slice(HEAD_DIM * (p // pairs_per_kv), HEAD_DIM * (p // pairs_per_kv + 1))
        if u == 0:
            v_tile = jnp.concatenate([vp_ref[0, kv_rows, :], vo_ref[0, kv_rows, :LANES]], axis=1)
        else:
            v_tile = vo_ref[0, kv_rows, u * LANES - B_PREV:(u + 1) * LANES]
        o = _pv_stage(s_ref, m, [v_tile], B_LIVE, extra_logit=sink_ref[p])
        for hh in range(2):
            hrows = slice(hh * HEAD_DIM, (hh + 1) * HEAD_DIM)
            gate = gt_ref[0, p, hrows, qlanes].astype(jnp.float32)
            row0 = p * PAIR_W + hh * HEAD_DIM
            zt_ref[row0:row0 + HEAD_DIM, qlanes] = (o[:, hh * LANES:(hh + 1) * LANES] * gate).astype(jnp.bfloat16)

    _head_pipeline(N_PAIRS * (TQB // LANES), scores_fn, pv_fn, s_refs, scores_first=False, per_pair=TQB // LANES)

    h2 = _out_and_residual(zt_ref, wout_ref, h_ref[0])
    out_ref[0] = (h2 * fg_ref[...]) * _rms_scale(h2)


def _attn_b(h, qbt, ksh, vsht, gbt, vec, sinks, wout, fg):
    b, s, d = h.shape
    qt4 = qbt.reshape(b, N_PAIRS, PAIR_W, s)
    gt4 = gbt.reshape(b, N_PAIRS, PAIR_W, s)
    feat_spec = pl.BlockSpec((1, N_PAIRS, PAIR_W, TQB), lambda bi, i: (bi, 0, 0, i))
    prev = lambda i: jnp.maximum(i * (TQB // B_PREV) - 1, 0)
    return pl.pallas_call(
        _attn_b_kernel,
        grid=(b, s // TQB),
        in_specs=[
            pl.BlockSpec((1, TQB, d), lambda bi, i: (bi, i, 0)),
            feat_spec,
            pl.BlockSpec((1, B_PREV, PAIR_W), lambda bi, i: (bi, prev(i), 0)),
            pl.BlockSpec((1, TQB, PAIR_W), lambda bi, i: (bi, i, 0)),
            pl.BlockSpec((1, PAIR_W, B_PREV), lambda bi, i: (bi, 0, prev(i))),
            pl.BlockSpec((1, PAIR_W, TQB), lambda bi, i: (bi, 0, i)),
            feat_spec,
            _const_spec(vec.shape),
            _const_spec(sinks.shape),
            _const_spec(wout.shape),
            _const_spec(fg.shape),
        ],
        out_specs=pl.BlockSpec((1, TQB, d), lambda bi, i: (bi, i, 0)),
        out_shape=jax.ShapeDtypeStruct((b, s, d), jnp.float32),
        scratch_shapes=[pltpu.VMEM((d, TQB), jnp.bfloat16),
                        pltpu.VMEM((2 * N_PAIRS, B_WIN, TQ), jnp.bfloat16),
                        *[pltpu.VMEM((B_WIN, TQ), jnp.float32)] * N_SCORE_BUFS_B],
        compiler_params=pltpu.CompilerParams(
            dimension_semantics=("arbitrary", "arbitrary"), vmem_limit_bytes=VMEM_LIMIT),
        name="attn_b",
    )(h, qt4, ksh, ksh, vsht, vsht, gt4, vec, sinks, wout, fg)


def _slab_diff():
    u = jnp.arange(BIAS_PERIOD)
    return jnp.where(u < TQ, u, u - BIAS_PERIOD)


def _bias_vecs_a(rel_bias):
    back = (A_KBLOCKS - 1 - jnp.arange(A_KBLOCKS))[:, None] * TQ
    dist = back + _slab_diff()[None, :]
    idx = jnp.clip(dist, -A_REL_CLIP, A_REL_CLIP) + A_REL_CLIP
    vec = jnp.transpose(rel_bias[idx], (0, 2, 1)).astype(jnp.float32)
    return vec[:, :, None, :]


def _t5_bucket(rel):
    nb = T5_BUCKETS // 2
    max_exact = nb // 2
    ret = jnp.where(rel > 0, nb, 0)
    n = jnp.abs(rel)
    nf = jnp.maximum(n, 1).astype(jnp.float32)
    large = max_exact + (jnp.log(nf / max_exact) / math.log(T5_MAX_DIST / max_exact)
                         * (nb - max_exact)).astype(jnp.int32)
    large = jnp.minimum(large, nb - 1)
    return ret + jnp.where(n < max_exact, n, large)


def _bias_vec_b(t5_table):
    rel = -_slab_diff() - B_PREV
    vec = jnp.transpose(t5_table[_t5_bucket(rel)], (1, 0)).astype(jnp.float32)
    return vec[None, :, None, :]


def kernel(x, a_norm, a_w_in, a_rel_bias, a_w_out, kv_norm, kv_w, t5_bias,
           b_norm, b_w_in, b_sinks, b_w_out, final_norm):
    assert a_norm.shape[0] == 1 and b_norm.shape[0] == 1, "one A layer then one B layer"
    bf = jnp.bfloat16
    scale = HEAD_DIM ** -0.5 * LOG2E

    def q_col_scale(n):
        return jnp.where(jnp.arange(n) < D_MODEL, scale, 1.0).astype(jnp.float32)

    wt_a = _weight_t(a_w_in[0], q_col_scale(a_w_in.shape[2]), a_norm[0], W_T_COLS_A)
    qt, k, vt, gt = _proj_a(x, wt_a)

    nwt = _weight_t(b_w_in[0], q_col_scale(b_w_in.shape[2]), b_norm[0], W_T_COLS_B,
                    head=kv_w, head_block=1, head_w=PAIR_W, head_gain=kv_norm)
    kvw_k = (kv_w[:, :PAIR_W] * kv_norm[:, None]).astype(bf)
    h, ksh, vsht, qbt, gbt = _attn_a(
        x, qt, k, vt, gt, _bias_vecs_a(a_rel_bias[0] * LOG2E), a_w_out[0].astype(bf), kvw_k, nwt)

    sinks = jnp.repeat((b_sinks[0].astype(jnp.float32) * LOG2E).reshape(N_PAIRS, 1, 2), LANES, axis=2)
    return _attn_b(h, qbt, ksh, vsht, gbt, _bias_vec_b(t5_bias * LOG2E), sinks, b_w_out[0].astype(bf),
                   final_norm[None, :])
```

```python
import functools
import math

import jax
import jax.numpy as jnp
from jax import lax
from jax.experimental import pallas as pl
from jax.experimental.pallas import tpu as pltpu

D_MODEL = 1024
HEAD_DIM = 64
N_HEADS = D_MODEL // HEAD_DIM
N_PAIRS = N_HEADS // 2
PAIR_W = 2 * HEAD_DIM
CHUNK = 64
RMS_EPS = 1e-6
A_LEFT_CHUNKS = 8
A_REL_CLIP = 256
B_KV_HEADS = 2
B_LEFT_CHUNKS = 2
T5_BUCKETS = 32
T5_MAX_DIST = 128

TQ = 256
TQB = 512
TQA = 512
A_KBLOCKS = A_LEFT_CHUNKS * CHUNK // TQ + 1
B_PREV = B_LEFT_CHUNKS * CHUNK
TM_PROJ = 1024
FEAT_CHUNK = 256
W_T_COLS_A = 1024
W_T_COLS_B = 512
MASKED = -1e30
BIAS_PERIOD = 2 * TQ
SUBLANES = 8
LANES = 128
MAX_CHAINS = 1
BF16_ROWS = 16
LOG2E = math.log2(math.e)
N_SCORE_BUFS = 4
N_SCORE_BUFS_B = 8
VMEM_LIMIT = 56 * 1024 * 1024

_NT = (((1,), (1,)), ((), ()))
_TN = (((0,), (0,)), ((), ()))


def _rms_scale(xf):
    return lax.rsqrt(jnp.mean(xf * xf, axis=-1, keepdims=True) + RMS_EPS)


def _token_scales(xf):
    col = _rms_scale(xf)
    row = jnp.transpose(jnp.broadcast_to(col, (xf.shape[0], LANES)))[0:1, :]
    return col, row


def _silu(v):
    return v * jax.nn.sigmoid(v)


def _weight_t_kernel(*refs, has_head):
    if has_head:
        head_ref, hgain_ref, w_ref, cs_ref, gain_ref, o_ref = refs
        j = pl.program_id(0)

        @pl.when(j == 0)
        def _():
            pad = o_ref.shape[0] - head_ref.shape[1]
            o_ref[:pad, :] = jnp.zeros((pad, o_ref.shape[1]), jnp.bfloat16)
            o_ref[pad:, :] = (head_ref[...].T * hgain_ref[...]).astype(jnp.bfloat16)

        @pl.when(j > 0)
        def _():
            o_ref[...] = ((w_ref[...] * cs_ref[...]).T * gain_ref[...]).astype(jnp.bfloat16)
    else:
        w_ref, cs_ref, gain_ref, o_ref = refs
        o_ref[...] = ((w_ref[...] * cs_ref[...]).T * gain_ref[...]).astype(jnp.bfloat16)


def _weight_t(w, col_scale, row_gain, tc, head=None, head_block=0, head_w=None, head_gain=None):
    d, n = w.shape
    off = 0 if head is None else 1
    main_specs = [
        pl.BlockSpec((d, tc), lambda j: (0, jnp.maximum(j - off, 0))),
        pl.BlockSpec((1, tc), lambda j: (0, jnp.maximum(j - off, 0))),
        pl.BlockSpec((1, d), lambda j: (0, 0)),
    ]
    head_specs, head_args = [], []
    if head is not None:
        assert head.shape[0] == d and head.shape[1] % head_w == 0 and head_w <= tc
        head_specs = [pl.BlockSpec((d, head_w), lambda j: (0, head_block)), pl.BlockSpec((1, d), lambda j: (0, 0))]
        head_args = [head, head_gain[None, :]]
    return pl.pallas_call(
        functools.partial(_weight_t_kernel, has_head=head is not None),
        grid=(n // tc + off,),
        in_specs=head_specs + main_specs,
        out_specs=pl.BlockSpec((tc, d), lambda j: (j, 0)),
        out_shape=jax.ShapeDtypeStruct((n + off * tc, d), jnp.bfloat16),
        compiler_params=pltpu.CompilerParams(dimension_semantics=("arbitrary",), vmem_limit_bytes=VMEM_LIMIT),
        name="weight_t",
    )(*head_args, w, col_scale[None, :], row_gain[None, :])


def _proj_a_kernel(x_ref, wt_ref, qt_ref, k_ref, vt_ref, gt_ref):
    xf = x_ref[0]
    xb = xf.astype(jnp.bfloat16)
    r_col, r_row = _token_scales(xf)
    k = lax.dot_general(xb, wt_ref[D_MODEL:2 * D_MODEL, :], _NT,
                        preferred_element_type=jnp.float32) * r_col
    for p in range(N_PAIRS):
        k_ref[0, p] = k[:, p * PAIR_W:(p + 1) * PAIR_W].astype(jnp.bfloat16)
    for out_ref, base, act in ((gt_ref, 3 * D_MODEL, _silu), (vt_ref, 2 * D_MODEL, None), (qt_ref, 0, None)):
        for c in range(0, D_MODEL, FEAT_CHUNK):
            w = wt_ref[base + c:base + c + FEAT_CHUNK, :]
            yt = lax.dot_general(w, xb, _NT, preferred_element_type=jnp.float32) * r_row
            out_ref[0, c:c + FEAT_CHUNK, :] = (yt if act is None else act(yt)).astype(jnp.bfloat16)


def _proj_a(x, wt):
    b, s, d = x.shape
    feat = jax.ShapeDtypeStruct((b, d, s), jnp.bfloat16)
    return pl.pallas_call(
        _proj_a_kernel,
        grid=(b, s // TM_PROJ),
        in_specs=[
            pl.BlockSpec((1, TM_PROJ, d), lambda bi, i: (bi, i, 0)),
            _const_spec(wt.shape),
        ],
        out_specs=[
            pl.BlockSpec((1, d, TM_PROJ), lambda bi, i: (bi, 0, i)),
            pl.BlockSpec((1, N_PAIRS, TM_PROJ, PAIR_W), lambda bi, i: (bi, 0, i, 0)),
            pl.BlockSpec((1, d, TM_PROJ), lambda bi, i: (bi, 0, i)),
            pl.BlockSpec((1, d, TM_PROJ), lambda bi, i: (bi, 0, i)),
        ],
        out_shape=[feat, jax.ShapeDtypeStruct((b, N_PAIRS, s, PAIR_W), jnp.bfloat16), feat, feat],
        compiler_params=pltpu.CompilerParams(
            dimension_semantics=("arbitrary", "arbitrary"), vmem_limit_bytes=VMEM_LIMIT),
        name="proj_a",
    )(x, wt)


def _chunk_iota(shape, axis):
    return lax.shift_right_logical(lax.broadcasted_iota(jnp.int32, shape, axis), CHUNK.bit_length() - 1)


def _identity_bf16(n):
    return (lax.broadcasted_iota(jnp.int32, (n, n), 0) == lax.broadcasted_iota(jnp.int32, (n, n), 1)).astype(jnp.bfloat16)


def _toeplitz(vec_row, n_keys):
    return pltpu.roll(jnp.broadcast_to(vec_row, (n_keys, BIAS_PERIOD)), 0, 1, stride=1, stride_axis=0)


def _fill_bias_slabs(vec_ref, bias_ref, n_keys, bands):
    n = len(bands)
    kch = _chunk_iota((n_keys, TQ), 0)
    qch = _chunk_iota((n_keys, TQ), 1)

    def head_body(h, carry):
        for j, band in enumerate(bands):
            t = _toeplitz(vec_ref[j, h], n_keys)[:, :TQ]
            if band is not None:
                t = jnp.where(band(kch, qch), t, MASKED)
            bias_ref[h * n + j] = t
        return carry

    lax.fori_loop(0, N_HEADS, head_body, 0)


def _live_rows(band, n_keys):
    chunks_per_half = LANES // CHUNK
    out = []
    for half in range(TQ // LANES):
        qchs = range(half * chunks_per_half, (half + 1) * chunks_per_half)
        live = [kc for kc in range(n_keys // CHUNK) if band is None or any(band(kc, qc) for qc in qchs)]
        out.append((live[0] * CHUNK, (live[-1] + 1) * CHUNK) if live else (0, 0))
    return tuple(out)


def _scores_stage(k_tiles, q_tiles, bias_fns, live, s_ref):
    maxes, row = [], 0
    for half in range(TQ // LANES):
        maxes.append([None] * MAX_CHAINS)
    for kt, qz, bias_fn, live_j in zip(k_tiles, q_tiles, bias_fns, live):
        s = jnp.dot(kt, qz, preferred_element_type=jnp.float32)
        for r in range(s.shape[0] // SUBLANES):
            rows = slice(r * SUBLANES, (r + 1) * SUBLANES)
            for half, (r0, r1) in enumerate(live_j):
                if not r0 <= r * SUBLANES < r1:
                    continue
                lanes = slice(half * LANES, (half + 1) * LANES)
                acc = maxes[half]
                grp = s[rows, lanes] if bias_fn is None else s[rows, lanes] + bias_fn(rows, lanes)
                s_ref[row + r * SUBLANES:row + (r + 1) * SUBLANES, lanes] = grp
                c = r % MAX_CHAINS
                acc[c] = grp if acc[c] is None else jnp.maximum(acc[c], grp)
        row += s.shape[0]
    cols = [jnp.max(functools.reduce(jnp.maximum, [a for a in acc if a is not None]), axis=0, keepdims=True)
            for acc in maxes]
    return jnp.concatenate(cols, axis=1)


def _pv_stage(s_ref, m, v_tiles, live, extra_logit=None):
    if extra_logit is not None:
        m = jnp.maximum(m, extra_logit)
    acc, row = None, 0
    for vt, live_j in zip(v_tiles, live):
        n = vt.shape[1]
        halves = []
        for half, (r0, r1) in enumerate(live_j):
            lanes = slice(half * LANES, (half + 1) * LANES)
            parts = [jnp.zeros((r0, LANES), jnp.bfloat16)] if r0 else []
            if r1 > r0:
                parts.append(jnp.exp2(s_ref[row + r0:row + r1, lanes] - m[:, lanes]).astype(jnp.bfloat16))
            if n > r1:
                parts.append(jnp.zeros((n - r1, LANES), jnp.bfloat16))
            halves.append(parts[0] if len(parts) == 1 else jnp.concatenate(parts, axis=0))
        p = jnp.concatenate(halves, axis=1)
        v_ones = jnp.concatenate([vt, jnp.ones((BF16_ROWS, n), vt.dtype)], axis=0)
        part = jnp.dot(v_ones, p, preferred_element_type=jnp.float32)
        acc = part if acc is None else acc + part
        row += n
    l = acc[HEAD_DIM:HEAD_DIM + 1, :]
    if extra_logit is not None:
        l = l + jnp.exp2(extra_logit - m)
    return acc[:HEAD_DIM, :] / l


def _head_pipeline(n_tiles, scores_fn, pv_fn, s_refs, scores_first, per_pair=2):
    per_group = len(s_refs) // 2
    x_refs, y_refs = s_refs[:per_group], s_refs[per_group:]
    n_groups = n_tiles // per_group

    m = [scores_fn(t // per_pair, t % per_pair, ref) for t, ref in zip(range(per_group), x_refs)]
    for g in range(n_groups):
        src, dst = (x_refs, y_refs) if g % 2 == 0 else (y_refs, x_refs)
        nxt = [(g + 1) * per_group + j for j in range(per_group)] if g + 1 < n_groups else []
        m_next = [scores_fn(t // per_pair, t % per_pair, ref) for t, ref in zip(nxt, dst)] if scores_first else []
        for j in range(per_group):
            if not scores_first and nxt:
                m_next.append(scores_fn(nxt[j] // per_pair, nxt[j] % per_pair, dst[j]))
            t = g * per_group + j
            pv_fn(t // per_pair, t % per_pair, src[j], m[j])
        m = m_next


A_BANDS = (lambda kc, qc: kc >= qc,
           None,
           lambda kc, qc: kc <= qc)
A_LIVE = tuple(_live_rows(band, TQ) for band in A_BANDS)
B_WIN = B_PREV + LANES
B_SUB = PAIR_W
B_BAND = lambda kc, qc: (kc >= qc) & (kc <= qc + B_LEFT_CHUNKS)
B_LIVE = (((0, B_WIN),) * (TQ // LANES),)


def _out_and_residual(zt_ref, wout_ref, res):
    y = lax.dot_general(zt_ref[...], wout_ref[...], _TN, preferred_element_type=jnp.float32)
    return res + y


def _attn_a_kernel(x_ref, qt_ref, kp_ref, ko_ref, vp_ref, vo_ref, gt_ref,
                   bvec_ref, wout_ref, kvwk_ref, nwt_ref,
                   h_ref, ksh_ref, vsht_ref, qbt_ref, gbt_ref, zt_ref, bias_ref, *s_refs):
    i = pl.program_id(1)
    subs = TQA // TQ

    @pl.when((pl.program_id(0) == 0) & (i == 0))
    def _():
        _fill_bias_slabs(bvec_ref, bias_ref, TQ, A_BANDS)
        bias_ref[N_HEADS * A_KBLOCKS] = jnp.full((TQ, TQ), MASKED, jnp.float32)

    n_slabs = N_HEADS * A_KBLOCKS
    rows = lax.broadcasted_iota(jnp.int32, (PAIR_W, TQ), 0)

    def key_block(sb, j):
        n = sb + j
        return n // subs, (n % subs) * TQ

    def scores_fn(sp, hh, s_ref):
        sb, p = divmod(sp, N_PAIRS)
        q2 = qt_ref[0, p, :, sb * TQ:(sb + 1) * TQ]
        own_rows = rows < HEAD_DIM if hh == 0 else rows >= HEAD_DIM
        qz = jnp.where(own_rows, q2, jnp.zeros_like(q2))
        k_tiles, bias_fns = [], []
        for j in range(A_KBLOCKS):
            in_seq = i * subs + sb - (A_KBLOCKS - 1) + j >= 0
            slab = jnp.where(in_seq, (2 * p + hh) * A_KBLOCKS + j, n_slabs)
            own, t0 = key_block(sb, j)
            k_tiles.append((ko_ref if own else kp_ref)[0, p, t0:t0 + TQ, :])
            bias_fns.append(lambda rws, lanes, slab=slab: bias_ref[slab, rws, lanes])
        return _scores_stage(k_tiles, [qz] * A_KBLOCKS, bias_fns, A_LIVE, s_ref)

    def pv_fn(sp, hh, s_ref, m):
        sb, p = divmod(sp, N_PAIRS)
        half = slice(hh * HEAD_DIM, (hh + 1) * HEAD_DIM)
        qcols = slice(sb * TQ, (sb + 1) * TQ)
        v_tiles = []
        for j in range(A_KBLOCKS):
            own, t0 = key_block(sb, j)
            v_tiles.append((vo_ref if own else vp_ref)[0, p, half, t0:t0 + TQ])
        o = _pv_stage(s_ref, m, v_tiles, A_LIVE)
        gate = gt_ref[0, p, half, qcols].astype(jnp.float32)
        row0 = p * PAIR_W + hh * HEAD_DIM
        zt_ref[row0:row0 + HEAD_DIM, qcols] = (o * gate).astype(jnp.bfloat16)

    _head_pipeline(subs * N_HEADS, scores_fn, pv_fn, s_refs, scores_first=True)

    h = _out_and_residual(zt_ref, wout_ref, x_ref[0])
    h_ref[0] = h
    hb = h.astype(jnp.bfloat16)
    r_col, r_row = _token_scales(h)
    ksh_ref[0] = (jnp.dot(hb, kvwk_ref[...], preferred_element_type=jnp.float32) * r_col).astype(jnp.bfloat16)
    for out_ref, base, act in ((gbt_ref, W_T_COLS_B + D_MODEL, _silu), (qbt_ref, W_T_COLS_B, None)):
        for c in range(0, D_MODEL, FEAT_CHUNK):
            first = out_ref is qbt_ref and c == 0
            start = base + c - (PAIR_W if first else 0)
            yt = lax.dot_general(nwt_ref[start:base + c + FEAT_CHUNK, :], hb, _NT,
                                 preferred_element_type=jnp.float32) * r_row
            if first:
                vsht_ref[0] = yt[:PAIR_W].astype(jnp.bfloat16)
                yt = yt[PAIR_W:]
            out_ref[0, c:c + FEAT_CHUNK, :] = (yt if act is None else act(yt)).astype(jnp.bfloat16)


def _const_spec(shape):
    return pl.BlockSpec(shape, lambda bi, i: (0,) * len(shape), pipeline_mode=pl.Buffered(1))


def _attn_a(x, qt, k, vt, gt, bvec, wout, kvwk, nwt):
    b, s, d = x.shape
    qt4 = qt.reshape(b, N_PAIRS, PAIR_W, s)
    vt4 = vt.reshape(b, N_PAIRS, PAIR_W, s)
    gt4 = gt.reshape(b, N_PAIRS, PAIR_W, s)
    assert (A_KBLOCKS - 1) * TQ <= TQA, "the keys before a grid block must fit in one previous block"
    prev = lambda i: jnp.maximum(i - 1, 0)

    def kspec(index):
        return pl.BlockSpec((1, N_PAIRS, TQA, PAIR_W), lambda bi, i: (bi, 0, index(i), 0))

    def vspec(index):
        return pl.BlockSpec((1, N_PAIRS, PAIR_W, TQA), lambda bi, i: (bi, 0, 0, index(i)))

    feat_spec = pl.BlockSpec((1, N_PAIRS, PAIR_W, TQA), lambda bi, i: (bi, 0, 0, i))
    featout_spec = pl.BlockSpec((1, d, TQA), lambda bi, i: (bi, 0, i))
    feat = jax.ShapeDtypeStruct((b, d, s), jnp.bfloat16)
    return pl.pallas_call(
        _attn_a_kernel,
        grid=(b, s // TQA),
        in_specs=[
            pl.BlockSpec((1, TQA, d), lambda bi, i: (bi, i, 0)),
            feat_spec,
            kspec(prev), kspec(lambda i: i),
            vspec(prev), vspec(lambda i: i),
            feat_spec,
            _const_spec(bvec.shape),
            _const_spec(wout.shape),
            _const_spec(kvwk.shape),
            _const_spec(nwt.shape),
        ],
        out_specs=[
            pl.BlockSpec((1, TQA, d), lambda bi, i: (bi, i, 0)),
            pl.BlockSpec((1, TQA, PAIR_W), lambda bi, i: (bi, i, 0)),
            pl.BlockSpec((1, PAIR_W, TQA), lambda bi, i: (bi, 0, i)),
            featout_spec,
            featout_spec,
        ],
        out_shape=[
            jax.ShapeDtypeStruct((b, s, d), jnp.float32),
            jax.ShapeDtypeStruct((b, s, PAIR_W), jnp.bfloat16),
            jax.ShapeDtypeStruct((b, PAIR_W, s), jnp.bfloat16),
            feat, feat,
        ],
        scratch_shapes=[pltpu.VMEM((d, TQA), jnp.bfloat16),
                        pltpu.VMEM((N_HEADS * A_KBLOCKS + 1, TQ, TQ), jnp.float32),
                        *[pltpu.VMEM((A_KBLOCKS * TQ, TQ), jnp.float32)] * N_SCORE_BUFS],
        compiler_params=pltpu.CompilerParams(
            dimension_semantics=("arbitrary", "arbitrary"), vmem_limit_bytes=VMEM_LIMIT),
        name="attn_a",
    )(x, qt4, k, k, vt4, vt4, gt4, bvec, wout, kvwk, nwt)


def _attn_b_kernel(h_ref, qt_ref, kp_ref, ko_ref, vp_ref, vo_ref, gt_ref, vec_ref,
                   sink_ref, wout_ref, fg_ref, out_ref, zt_ref, bias_ref, *s_refs):
    i = pl.program_id(1)

    @pl.when((pl.program_id(0) == 0) & (i == 0))
    def _():
        in_band = B_BAND(_chunk_iota((B_WIN, LANES), 0), _chunk_iota((B_WIN, LANES), 1))
        before_block = lax.broadcasted_iota(jnp.int32, (B_WIN, LANES), 0) < B_PREV

        def pair_body(p, carry):
            for hh in range(2):
                t = jnp.where(in_band, _toeplitz(vec_ref[0, 2 * p + hh], B_WIN)[:, :LANES], MASKED)
                lanes = slice(hh * LANES, (hh + 1) * LANES)
                bias_ref[2 * p, :, lanes] = t.astype(jnp.bfloat16)
                bias_ref[2 * p + 1, :, lanes] = jnp.where(before_block, MASKED, t).astype(jnp.bfloat16)
            return carry

        lax.fori_loop(0, N_PAIRS, pair_body, 0)

    rows = lax.broadcasted_iota(jnp.int32, (PAIR_W, TQ), 0)
    pairs_per_kv = N_PAIRS // B_KV_HEADS
    eye = _identity_bf16(B_SUB)

    def scores_fn(p, u, s_ref):
        qlanes = slice(u * LANES, (u + 1) * LANES)
        q2 = jnp.concatenate([qt_ref[0, p, :HEAD_DIM, qlanes], qt_ref[0, p, HEAD_DIM:, qlanes]], axis=1)
        kv0 = HEAD_DIM * (p // pairs_per_kv)
        kv_rows = (rows >= kv0) & (rows < kv0 + HEAD_DIM)
        qz = jnp.where(kv_rows, jnp.concatenate([q2, q2], axis=0), jnp.zeros((PAIR_W, TQ), q2.dtype))
        if u == 0:
            k_tile = jnp.concatenate([kp_ref[0], ko_ref[0, :LANES, :]], axis=0)
            variant = jnp.where(i == 0, 1, 0)
        else:
            k_tile, variant = ko_ref[0, u * LANES - B_PREV:(u + 1) * LANES, :], 0

        lhs, rhs = [], []
        for r0 in range(0, B_WIN, B_SUB):
            lhs.append(jnp.concatenate([k_tile[r0:r0 + B_SUB, :], eye], axis=1))
            rhs.append(jnp.concatenate([qz, bias_ref[2 * p + variant, r0:r0 + B_SUB, :]], axis=0))
        sub_live = (((0, B_SUB),) * (TQ // LANES),) * len(lhs)
        return _scores_stage(lhs, rhs, [None] * len(lhs), sub_live, s_ref)

    def pv_fn(p, u, s_ref, m):
        qlanes = slice(u * LANES, (u + 1) * LANES)
        kv_rows = slice(HEAD_DIM * (p // pairs_per_kv), HEAD_DIM * (p // pairs_per_kv + 1))
        if u == 0:
            v_tile = jnp.concatenate([vp_ref[0, kv_rows, :], vo_ref[0, kv_rows, :LANES]], axis=1)
        else:
            v_tile = vo_ref[0, kv_rows, u * LANES - B_PREV:(u + 1) * LANES]
        o = _pv_stage(s_ref, m, [v_tile], B_LIVE, extra_logit=sink_ref[p])
        for hh in range(2):
            hrows = slice(hh * HEAD_DIM, (hh + 1) * HEAD_DIM)
            gate = gt_ref[0, p, hrows, qlanes].astype(jnp.float32)
            row0 = p * PAIR_W + hh * HEAD_DIM
            zt_ref[row0:row0 + HEAD_DIM, qlanes] = (o[:, hh * LANES:(hh + 1) * LANES] * gate).astype(jnp.bfloat16)

    _head_pipeline(N_PAIRS * (TQB // LANES), scores_fn, pv_fn, s_refs, scores_first=False, per_pair=TQB // LANES)

    h2 = _out_and_residual(zt_ref, wout_ref, h_ref[0])
    out_ref[0] = (h2 * fg_ref[...]) * _rms_scale(h2)


def _attn_b(h, qbt, ksh, vsht, gbt, vec, sinks, wout, fg):
    b, s, d = h.shape
    qt4 = qbt.reshape(b, N_PAIRS, PAIR_W, s)
    gt4 = gbt.reshape(b, N_PAIRS, PAIR_W, s)
    feat_spec = pl.BlockSpec((1, N_PAIRS, PAIR_W, TQB), lambda bi, i: (bi, 0, 0, i))
    prev = lambda i: jnp.maximum(i * (TQB // B_PREV) - 1, 0)
    return pl.pallas_call(
        _attn_b_kernel,
        grid=(b, s // TQB),
        in_specs=[
            pl.BlockSpec((1, TQB, d), lambda bi, i: (bi, i, 0)),
            feat_spec,
            pl.BlockSpec((1, B_PREV, PAIR_W), lambda bi, i: (bi, prev(i), 0)),
            pl.BlockSpec((1, TQB, PAIR_W), lambda bi, i: (bi, i, 0)),
            pl.BlockSpec((1, PAIR_W, B_PREV), lambda bi, i: (bi, 0, prev(i))),
            pl.BlockSpec((1, PAIR_W, TQB), lambda bi, i: (bi, 0, i)),
            feat_spec,
            _const_spec(vec.shape),
            _const_spec(sinks.shape),
            _const_spec(wout.shape),
            _const_spec(fg.shape),
        ],
        out_specs=pl.BlockSpec((1, TQB, d), lambda bi, i: (bi, i, 0)),
        out_shape=jax.ShapeDtypeStruct((b, s, d), jnp.float32),
        scratch_shapes=[pltpu.VMEM((d, TQB), jnp.bfloat16),
                        pltpu.VMEM((2 * N_PAIRS, B_WIN, TQ), jnp.bfloat16),
                        *[pltpu.VMEM((B_WIN, TQ), jnp.float32)] * N_SCORE_BUFS_B],
        compiler_params=pltpu.CompilerParams(
            dimension_semantics=("arbitrary", "arbitrary"), vmem_limit_bytes=VMEM_LIMIT),
        name="attn_b",
    )(h, qt4, ksh, ksh, vsht, vsht, gt4, vec, sinks, wout, fg)


def _slab_diff():
    u = jnp.arange(BIAS_PERIOD)
    return jnp.where(u < TQ, u, u - BIAS_PERIOD)


def _bias_vecs_a(rel_bias):
    back = (A_KBLOCKS - 1 - jnp.arange(A_KBLOCKS))[:, None] * TQ
    dist = back + _slab_diff()[None, :]
    idx = jnp.clip(dist, -A_REL_CLIP, A_REL_CLIP) + A_REL_CLIP
    vec = jnp.transpose(rel_bias[idx], (0, 2, 1)).astype(jnp.float32)
    return vec[:, :, None, :]


def _t5_bucket(rel):
    nb = T5_BUCKETS // 2
    max_exact = nb // 2
    ret = jnp.where(rel > 0, nb, 0)
    n = jnp.abs(rel)
    nf = jnp.maximum(n, 1).astype(jnp.float32)
    large = max_exact + (jnp.log(nf / max_exact) / math.log(T5_MAX_DIST / max_exact)
                         * (nb - max_exact)).astype(jnp.int32)
    large = jnp.minimum(large, nb - 1)
    return ret + jnp.where(n < max_exact, n, large)


def _bias_vec_b(t5_table):
    rel = -_slab_diff() - B_PREV
    vec = jnp.transpose(t5_table[_t5_bucket(rel)], (1, 0)).astype(jnp.float32)
    return vec[None, :, None, :]


def kernel(x, a_norm, a_w_in, a_rel_bias, a_w_out, kv_norm, kv_w, t5_bias,
           b_norm, b_w_in, b_sinks, b_w_out, final_norm):
    assert a_norm.shape[0] == 1 and b_norm.shape[0] == 1, "one A layer then one B layer"
    bf = jnp.bfloat16
    scale = HEAD_DIM ** -0.5 * LOG2E

    def q_col_scale(n):
        return jnp.where(jnp.arange(n) < D_MODEL, scale, 1.0).astype(jnp.float32)

    wt_a = _weight_t(a_w_in[0], q_col_scale(a_w_in.shape[2]), a_norm[0], W_T_COLS_A)
    qt, k, vt, gt = _proj_a(x, wt_a)

    nwt = _weight_t(b_w_in[0], q_col_scale(b_w_in.shape[2]), b_norm[0], W_T_COLS_B,
                    head=kv_w, head_block=1, head_w=PAIR_W, head_gain=kv_norm)
    kvw_k = (kv_w[:, :PAIR_W] * kv_norm[:, None]).astype(bf)
    h, ksh, vsht, qbt, gbt = _attn_a(
        x, qt, k, vt, gt, _bias_vecs_a(a_rel_bias[0] * LOG2E), a_w_out[0].astype(bf), kvw_k, nwt)

    sinks = jnp.repeat((b_sinks[0].astype(jnp.float32) * LOG2E).reshape(N_PAIRS, 1, 2), LANES, axis=2)
    return _attn_b(h, qbt, ksh, vsht, gbt, _bias_vec_b(t5_bias * LOG2E), sinks, b_w_out[0].astype(bf),
                   final_norm[None, :])
```

```python
import functools
import math

import jax
import jax.numpy as jnp
from jax import lax
from jax.experimental import pallas as pl
from jax.experimental.pallas import tpu as pltpu

D_MODEL = 1024
HEAD_DIM = 64
N_HEADS = D_MODEL // HEAD_DIM
N_PAIRS = N_HEADS // 2
PAIR_W = 2 * HEAD_DIM
CHUNK = 64
RMS_EPS = 1e-6
A_LEFT_CHUNKS = 8
A_REL_CLIP = 256
B_KV_HEADS = 2
B_LEFT_CHUNKS = 2
T5_BUCKETS = 32
T5_MAX_DIST = 128

TQ = 256
TQB = 1024
TQA = 512
A_KBLOCKS = A_LEFT_CHUNKS * CHUNK // TQ + 1
B_PREV = B_LEFT_CHUNKS * CHUNK
TM_PROJ = 1024
FEAT_CHUNK = 256
W_T_COLS_A = 1024
W_T_COLS_B = 512
MASKED = -1e30
BIAS_PERIOD = 2 * TQ
SUBLANES = 8
LANES = 128
MAX_CHAINS = 1
BF16_ROWS = 16
LOG2E = math.log2(math.e)
N_SCORE_BUFS = 4
N_SCORE_BUFS_B = 8
VMEM_LIMIT = 56 * 1024 * 1024

_NT = (((1,), (1,)), ((), ()))
_TN = (((0,), (0,)), ((), ()))


def _rms_scale(xf):
    return lax.rsqrt(jnp.mean(xf * xf, axis=-1, keepdims=True) + RMS_EPS)


def _token_scales(xf):
    col = _rms_scale(xf)
    row = jnp.transpose(jnp.broadcast_to(col, (xf.shape[0], LANES)))[0:1, :]
    return col, row


def _silu(v):
    return v * jax.nn.sigmoid(v)


def _weight_t_kernel(*refs, has_head):
    if has_head:
        head_ref, hgain_ref, w_ref, cs_ref, gain_ref, o_ref = refs
        j = pl.program_id(0)

        @pl.when(j == 0)
        def _():
            pad = o_ref.shape[0] - head_ref.shape[1]
            o_ref[:pad, :] = jnp.zeros((pad, o_ref.shape[1]), jnp.bfloat16)
            o_ref[pad:, :] = (head_ref[...].T * hgain_ref[...]).astype(jnp.bfloat16)

        @pl.when(j > 0)
        def _():
            o_ref[...] = ((w_ref[...] * cs_ref[...]).T * gain_ref[...]).astype(jnp.bfloat16)
    else:
        w_ref, cs_ref, gain_ref, o_ref = refs
        o_ref[...] = ((w_ref[...] * cs_ref[...]).T * gain_ref[...]).astype(jnp.bfloat16)


def _weight_t(w, col_scale, row_gain, tc, head=None, head_block=0, head_w=None, head_gain=None):
    d, n = w.shape
    off = 0 if head is None else 1
    main_specs = [
        pl.BlockSpec((d, tc), lambda j: (0, jnp.maximum(j - off, 0))),
        pl.BlockSpec((1, tc), lambda j: (0, jnp.maximum(j - off, 0))),
        pl.BlockSpec((1, d), lambda j: (0, 0)),
    ]
    head_specs, head_args = [], []
    if head is not None:
        assert head.shape[0] == d and head.shape[1] % head_w == 0 and head_w <= tc
        head_specs = [pl.BlockSpec((d, head_w), lambda j: (0, head_block)), pl.BlockSpec((1, d), lambda j: (0, 0))]
        head_args = [head, head_gain[None, :]]
    return pl.pallas_call(
        functools.partial(_weight_t_kernel, has_head=head is not None),
        grid=(n // tc + off,),
        in_specs=head_specs + main_specs,
        out_specs=pl.BlockSpec((tc, d), lambda j: (j, 0)),
        out_shape=jax.ShapeDtypeStruct((n + off * tc, d), jnp.bfloat16),
        compiler_params=pltpu.CompilerParams(dimension_semantics=("arbitrary",), vmem_limit_bytes=VMEM_LIMIT),
        name="weight_t",
    )(*head_args, w, col_scale[None, :], row_gain[None, :])


def _proj_a_kernel(x_ref, wt_ref, qt_ref, k_ref, vt_ref, gt_ref):
    xf = x_ref[0]
    xb = xf.astype(jnp.bfloat16)
    r_col, r_row = _token_scales(xf)
    k = lax.dot_general(xb, wt_ref[D_MODEL:2 * D_MODEL, :], _NT,
                        preferred_element_type=jnp.float32) * r_col
    for p in range(N_PAIRS):
        k_ref[0, p] = k[:, p * PAIR_W:(p + 1) * PAIR_W].astype(jnp.bfloat16)
    for out_ref, base, act in ((gt_ref, 3 * D_MODEL, _silu), (vt_ref, 2 * D_MODEL, None), (qt_ref, 0, None)):
        for c in range(0, D_MODEL, FEAT_CHUNK):
            w = wt_ref[base + c:base + c + FEAT_CHUNK, :]
            yt = lax.dot_general(w, xb, _NT, preferred_element_type=jnp.float32) * r_row
            out_ref[0, c:c + FEAT_CHUNK, :] = (yt if act is None else act(yt)).astype(jnp.bfloat16)


def _proj_a(x, wt):
    b, s, d = x.shape
    feat = jax.ShapeDtypeStruct((b, d, s), jnp.bfloat16)
    return pl.pallas_call(
        _proj_a_kernel,
        grid=(b, s // TM_PROJ),
        in_specs=[
            pl.BlockSpec((1, TM_PROJ, d), lambda bi, i: (bi, i, 0)),
            _const_spec(wt.shape),
        ],
        out_specs=[
            pl.BlockSpec((1, d, TM_PROJ), lambda bi, i: (bi, 0, i)),
            pl.BlockSpec((1, N_PAIRS, TM_PROJ, PAIR_W), lambda bi, i: (bi, 0, i, 0)),
            pl.BlockSpec((1, d, TM_PROJ), lambda bi, i: (bi, 0, i)),
            pl.BlockSpec((1, d, TM_PROJ), lambda bi, i: (bi, 0, i)),
        ],
        out_shape=[feat, jax.ShapeDtypeStruct((b, N_PAIRS, s, PAIR_W), jnp.bfloat16), feat, feat],
        compiler_params=pltpu.CompilerParams(
            dimension_semantics=("arbitrary", "arbitrary"), vmem_limit_bytes=VMEM_LIMIT),
        name="proj_a",
    )(x, wt)


def _chunk_iota(shape, axis):
    return lax.shift_right_logical(lax.broadcasted_iota(jnp.int32, shape, axis), CHUNK.bit_length() - 1)


def _identity_bf16(n):
    return (lax.broadcasted_iota(jnp.int32, (n, n), 0) == lax.broadcasted_iota(jnp.int32, (n, n), 1)).astype(jnp.bfloat16)


def _toeplitz(vec_row, n_keys):
    return pltpu.roll(jnp.broadcast_to(vec_row, (n_keys, BIAS_PERIOD)), 0, 1, stride=1, stride_axis=0)


def _fill_bias_slabs(vec_ref, bias_ref, n_keys, bands):
    n = len(bands)
    kch = _chunk_iota((n_keys, TQ), 0)
    qch = _chunk_iota((n_keys, TQ), 1)

    def head_body(h, carry):
        for j, band in enumerate(bands):
            t = _toeplitz(vec_ref[j, h], n_keys)[:, :TQ]
            if band is not None:
                t = jnp.where(band(kch, qch), t, MASKED)
            bias_ref[h * n + j] = t
        return carry

    lax.fori_loop(0, N_HEADS, head_body, 0)


def _live_rows(band, n_keys):
    chunks_per_half = LANES // CHUNK
    out = []
    for half in range(TQ // LANES):
        qchs = range(half * chunks_per_half, (half + 1) * chunks_per_half)
        live = [kc for kc in range(n_keys // CHUNK) if band is None or any(band(kc, qc) for qc in qchs)]
        out.append((live[0] * CHUNK, (live[-1] + 1) * CHUNK) if live else (0, 0))
    return tuple(out)


def _scores_stage(k_tiles, q_tiles, bias_fns, live, s_ref):
    maxes, row = [], 0
    for half in range(TQ // LANES):
        maxes.append([None] * MAX_CHAINS)
    for kt, qz, bias_fn, live_j in zip(k_tiles, q_tiles, bias_fns, live):
        s = jnp.dot(kt, qz, preferred_element_type=jnp.float32)
        for r in range(s.shape[0] // SUBLANES):
            rows = slice(r * SUBLANES, (r + 1) * SUBLANES)
            for half, (r0, r1) in enumerate(live_j):
                if not r0 <= r * SUBLANES < r1:
                    continue
                lanes = slice(half * LANES, (half + 1) * LANES)
                acc = maxes[half]
                grp = s[rows, lanes] if bias_fn is None else s[rows, lanes] + bias_fn(rows, lanes)
                s_ref[row + r * SUBLANES:row + (r + 1) * SUBLANES, lanes] = grp
                c = r % MAX_CHAINS
                acc[c] = grp if acc[c] is None else jnp.maximum(acc[c], grp)
        row += s.shape[0]
    cols = [jnp.max(functools.reduce(jnp.maximum, [a for a in acc if a is not None]), axis=0, keepdims=True)
            for acc in maxes]
    return jnp.concatenate(cols, axis=1)


def _pv_stage(s_ref, m, v_tiles, live, extra_logit=None):
    if extra_logit is not None:
        m = jnp.maximum(m, extra_logit)
    acc, row = None, 0
    for vt, live_j in zip(v_tiles, live):
        n = vt.shape[1]
        halves = []
        for half, (r0, r1) in enumerate(live_j):
            lanes = slice(half * LANES, (half + 1) * LANES)
            parts = [jnp.zeros((r0, LANES), jnp.bfloat16)] if r0 else []
            if r1 > r0:
                parts.append(jnp.exp2(s_ref[row + r0:row + r1, lanes] - m[:, lanes]).astype(jnp.bfloat16))
            if n > r1:
                parts.append(jnp.zeros((n - r1, LANES), jnp.bfloat16))
            halves.append(parts[0] if len(parts) == 1 else jnp.concatenate(parts, axis=0))
        p = jnp.concatenate(halves, axis=1)
        v_ones = jnp.concatenate([vt, jnp.ones((BF16_ROWS, n), vt.dtype)], axis=0)
        part = jnp.dot(v_ones, p, preferred_element_type=jnp.float32)
        acc = part if acc is None else acc + part
        row += n
    l = acc[HEAD_DIM:HEAD_DIM + 1, :]
    if extra_logit is not None:
        l = l + jnp.exp2(extra_logit - m)
    return acc[:HEAD_DIM, :] / l


def _head_pipeline(n_tiles, scores_fn, pv_fn, s_refs, scores_first, per_pair=2):
    per_group = len(s_refs) // 2
    x_refs, y_refs = s_refs[:per_group], s_refs[per_group:]
    n_groups = n_tiles // per_group

    m = [scores_fn(t // per_pair, t % per_pair, ref) for t, ref in zip(range(per_group), x_refs)]
    for g in range(n_groups):
        src, dst = (x_refs, y_refs) if g % 2 == 0 else (y_refs, x_refs)
        nxt = [(g + 1) * per_group + j for j in range(per_group)] if g + 1 < n_groups else []
        m_next = [scores_fn(t // per_pair, t % per_pair, ref) for t, ref in zip(nxt, dst)] if scores_first else []
        for j in range(per_group):
            if not scores_first and nxt:
                m_next.append(scores_fn(nxt[j] // per_pair, nxt[j] % per_pair, dst[j]))
            t = g * per_group + j
            pv_fn(t // per_pair, t % per_pair, src[j], m[j])
        m = m_next


A_BANDS = (lambda kc, qc: kc >= qc,
           None,
           lambda kc, qc: kc <= qc)
A_LIVE = tuple(_live_rows(band, TQ) for band in A_BANDS)
B_WIN = B_PREV + LANES
B_SUB = PAIR_W
B_BAND = lambda kc, qc: (kc >= qc) & (kc <= qc + B_LEFT_CHUNKS)
B_LIVE = (((0, B_WIN),) * (TQ // LANES),)


def _out_and_residual(zt_ref, wout_ref, res):
    y = lax.dot_general(zt_ref[...], wout_ref[...], _TN, preferred_element_type=jnp.float32)
    return res + y


def _attn_a_kernel(x_ref, qt_ref, kp_ref, ko_ref, vp_ref, vo_ref, gt_ref,
                   bvec_ref, wout_ref, kvwk_ref, nwt_ref,
                   h_ref, ksh_ref, vsht_ref, qbt_ref, gbt_ref, zt_ref, bias_ref, *s_refs):
    i = pl.program_id(1)
    subs = TQA // TQ

    @pl.when((pl.program_id(0) == 0) & (i == 0))
    def _():
        _fill_bias_slabs(bvec_ref, bias_ref, TQ, A_BANDS)
        bias_ref[N_HEADS * A_KBLOCKS] = jnp.full((TQ, TQ), MASKED, jnp.float32)

    n_slabs = N_HEADS * A_KBLOCKS
    rows = lax.broadcasted_iota(jnp.int32, (PAIR_W, TQ), 0)

    def key_block(sb, j):
        n = sb + j
        return n // subs, (n % subs) * TQ

    def scores_fn(sp, hh, s_ref):
        sb, p = divmod(sp, N_PAIRS)
        q2 = qt_ref[0, p, :, sb * TQ:(sb + 1) * TQ]
        own_rows = rows < HEAD_DIM if hh == 0 else rows >= HEAD_DIM
        qz = jnp.where(own_rows, q2, jnp.zeros_like(q2))
        k_tiles, bias_fns = [], []
        for j in range(A_KBLOCKS):
            in_seq = i * subs + sb - (A_KBLOCKS - 1) + j >= 0
            slab = jnp.where(in_seq, (2 * p + hh) * A_KBLOCKS + j, n_slabs)
            own, t0 = key_block(sb, j)
            k_tiles.append((ko_ref if own else kp_ref)[0, p, t0:t0 + TQ, :])
            bias_fns.append(lambda rws, lanes, slab=slab: bias_ref[slab, rws, lanes])
        return _scores_stage(k_tiles, [qz] * A_KBLOCKS, bias_fns, A_LIVE, s_ref)

    def pv_fn(sp, hh, s_ref, m):
        sb, p = divmod(sp, N_PAIRS)
        half = slice(hh * HEAD_DIM, (hh + 1) * HEAD_DIM)
        qcols = slice(sb * TQ, (sb + 1) * TQ)
        v_tiles = []
        for j in range(A_KBLOCKS):
            own, t0 = key_block(sb, j)
            v_tiles.append((vo_ref if own else vp_ref)[0, p, half, t0:t0 + TQ])
        o = _pv_stage(s_ref, m, v_tiles, A_LIVE)
        gate = gt_ref[0, p, half, qcols].astype(jnp.float32)
        row0 = p * PAIR_W + hh * HEAD_DIM
        zt_ref[row0:row0 + HEAD_DIM, qcols] = (o * gate).astype(jnp.bfloat16)

    _head_pipeline(subs * N_HEADS, scores_fn, pv_fn, s_refs, scores_first=True)

    h = _out_and_residual(zt_ref, wout_ref, x_ref[0])
    h_ref[0] = h
    hb = h.astype(jnp.bfloat16)
    r_col, r_row = _token_scales(h)
    ksh_ref[0] = (jnp.dot(hb, kvwk_ref[...], preferred_element_type=jnp.float32) * r_col).astype(jnp.bfloat16)
    for out_ref, base, act in ((gbt_ref, W_T_COLS_B + D_MODEL, _silu), (qbt_ref, W_T_COLS_B, None)):
        for c in range(0, D_MODEL, FEAT_CHUNK):
            first = out_ref is qbt_ref and c == 0
            start = base + c - (PAIR_W if first else 0)
            yt = lax.dot_general(nwt_ref[start:base + c + FEAT_CHUNK, :], hb, _NT,
                                 preferred_element_type=jnp.float32) * r_row
            if first:
                vsht_ref[0] = yt[:PAIR_W].astype(jnp.bfloat16)
                yt = yt[PAIR_W:]
            out_ref[0, c:c + FEAT_CHUNK, :] = (yt if act is None else act(yt)).astype(jnp.bfloat16)


def _const_spec(shape):
    return pl.BlockSpec(shape, lambda bi, i: (0,) * len(shape), pipeline_mode=pl.Buffered(1))


def _attn_a(x, qt, k, vt, gt, bvec, wout, kvwk, nwt):
    b, s, d = x.shape
    qt4 = qt.reshape(b, N_PAIRS, PAIR_W, s)
    vt4 = vt.reshape(b, N_PAIRS, PAIR_W, s)
    gt4 = gt.reshape(b, N_PAIRS, PAIR_W, s)
    assert (A_KBLOCKS - 1) * TQ <= TQA, "the keys before a grid block must fit in one previous block"
    prev = lambda i: jnp.maximum(i - 1, 0)

    def kspec(index):
        return pl.BlockSpec((1, N_PAIRS, TQA, PAIR_W), lambda bi, i: (bi, 0, index(i), 0))

    def vspec(index):
        return pl.BlockSpec((1, N_PAIRS, PAIR_W, TQA), lambda bi, i: (bi, 0, 0, index(i)))

    feat_spec = pl.BlockSpec((1, N_PAIRS, PAIR_W, TQA), lambda bi, i: (bi, 0, 0, i))
    featout_spec = pl.BlockSpec((1, d, TQA), lambda bi, i: (bi, 0, i))
    feat = jax.ShapeDtypeStruct((b, d, s), jnp.bfloat16)
    return pl.pallas_call(
        _attn_a_kernel,
        grid=(b, s // TQA),
        in_specs=[
            pl.BlockSpec((1, TQA, d), lambda bi, i: (bi, i, 0)),
            feat_spec,
            kspec(prev), kspec(lambda i: i),
            vspec(prev), vspec(lambda i: i),
            feat_spec,
            _const_spec(bvec.shape),
            _const_spec(wout.shape),
            _const_spec(kvwk.shape),
            _const_spec(nwt.shape),
        ],
        out_specs=[
            pl.BlockSpec((1, TQA, d), lambda bi, i: (bi, i, 0)),
            pl.BlockSpec((1, TQA, PAIR_W), lambda bi, i: (bi, i, 0)),
            pl.BlockSpec((1, PAIR_W, TQA), lambda bi, i: (bi, 0, i)),
            featout_spec,
            featout_spec,
        ],
        out_shape=[
            jax.ShapeDtypeStruct((b, s, d), jnp.float32),
            jax.ShapeDtypeStruct((b, s, PAIR_W), jnp.bfloat16),
            jax.ShapeDtypeStruct((b, PAIR_W, s), jnp.bfloat16),
            feat, feat,
        ],
        scratch_shapes=[pltpu.VMEM((d, TQA), jnp.bfloat16),
                        pltpu.VMEM((N_HEADS * A_KBLOCKS + 1, TQ, TQ), jnp.float32),
                        *[pltpu.VMEM((A_KBLOCKS * TQ, TQ), jnp.float32)] * N_SCORE_BUFS],
        compiler_params=pltpu.CompilerParams(
            dimension_semantics=("arbitrary", "arbitrary"), vmem_limit_bytes=VMEM_LIMIT),
        name="attn_a",
    )(x, qt4, k, k, vt4, vt4, gt4, bvec, wout, kvwk, nwt)


def _attn_b_kernel(h_ref, qt_ref, kp_ref, ko_ref, vp_ref, vo_ref, gt_ref, vec_ref,
                   sink_ref, wout_ref, fg_ref, out_ref, zt_ref, bias_ref, *s_refs):
    i = pl.program_id(1)

    @pl.when((pl.program_id(0) == 0) & (i == 0))
    def _():
        in_band = B_BAND(_chunk_iota((B_WIN, LANES), 0), _chunk_iota((B_WIN, LANES), 1))
        before_block = lax.broadcasted_iota(jnp.int32, (B_WIN, LANES), 0) < B_PREV

        def pair_body(p, carry):
            for hh in range(2):
                t = jnp.where(in_band, _toeplitz(vec_ref[0, 2 * p + hh], B_WIN)[:, :LANES], MASKED)
                lanes = slice(hh * LANES, (hh + 1) * LANES)
                bias_ref[2 * p, :, lanes] = t.astype(jnp.bfloat16)
                bias_ref[2 * p + 1, :, lanes] = jnp.where(before_block, MASKED, t).astype(jnp.bfloat16)
            return carry

        lax.fori_loop(0, N_PAIRS, pair_body, 0)

    rows = lax.broadcasted_iota(jnp.int32, (PAIR_W, TQ), 0)
    pairs_per_kv = N_PAIRS // B_KV_HEADS
    eye = _identity_bf16(B_SUB)

    def scores_fn(p, u, s_ref):
        qlanes = slice(u * LANES, (u + 1) * LANES)
        q2 = jnp.concatenate([qt_ref[0, p, :HEAD_DIM, qlanes], qt_ref[0, p, HEAD_DIM:, qlanes]], axis=1)
        kv0 = HEAD_DIM * (p // pairs_per_kv)
        kv_rows = (rows >= kv0) & (rows < kv0 + HEAD_DIM)
        qz = jnp.where(kv_rows, jnp.concatenate([q2, q2], axis=0), jnp.zeros((PAIR_W, TQ), q2.dtype))
        if u == 0:
            k_tile = jnp.concatenate([kp_ref[0], ko_ref[0, :LANES, :]], axis=0)
            variant = jnp.where(i == 0, 1, 0)
        else:
            k_tile, variant = ko_ref[0, u * LANES - B_PREV:(u + 1) * LANES, :], 0

        lhs, rhs = [], []
        for r0 in range(0, B_WIN, B_SUB):
            lhs.append(jnp.concatenate([k_tile[r0:r0 + B_SUB, :], eye], axis=1))
            rhs.append(jnp.concatenate([qz, bias_ref[2 * p + variant, r0:r0 + B_SUB, :]], axis=0))
        sub_live = (((0, B_SUB),) * (TQ // LANES),) * len(lhs)
        return _scores_stage(lhs, rhs, [None] * len(lhs), sub_live, s_ref)

    def pv_fn(p, u, s_ref, m):
        qlanes = slice(u * LANES, (u + 1) * LANES)
        kv_rows = slice(HEAD_DIM * (p // pairs_per_kv), HEAD_DIM * (p // pairs_per_kv + 1))
        if u == 0:
            v_tile = jnp.concatenate([vp_ref[0, kv_rows, :], vo_ref[0, kv_rows, :LANES]], axis=1)
        else:
            v_tile = vo_ref[0, kv_rows, u * LANES - B_PREV:(u + 1) * LANES]
        o = _pv_stage(s_ref, m, [v_tile], B_LIVE, extra_logit=sink_ref[p])
        for hh in range(2):
            hrows = slice(hh * HEAD_DIM, (hh + 1) * HEAD_DIM)
            gate = gt_ref[0, p, hrows, qlanes].astype(jnp.float32)
            row0 = p * PAIR_W + hh * HEAD_DIM
            zt_ref[row0:row0 + HEAD_DIM, qlanes] = (o[:, hh * LANES:(hh + 1) * LANES] * gate).astype(jnp.bfloat16)

    _head_pipeline(N_PAIRS * (TQB // LANES), scores_fn, pv_fn, s_refs, scores_first=False, per_pair=TQB // LANES)

    h2 = _out_and_residual(zt_ref, wout_ref, h_ref[0])
    out_ref[0] = (h2 * fg_ref[...]) * _rms_scale(h2)


def _attn_b(h, qbt, ksh, vsht, gbt, vec, sinks, wout, fg):
    b, s, d = h.shape
    qt4 = qbt.reshape(b, N_PAIRS, PAIR_W, s)
    gt4 = gbt.reshape(b, N_PAIRS, PAIR_W, s)
    feat_spec = pl.BlockSpec((1, N_PAIRS, PAIR_W, TQB), lambda bi, i: (bi, 0, 0, i))
    prev = lambda i: jnp.maximum(i * (TQB // B_PREV) - 1, 0)
    return pl.pallas_call(
        _attn_b_kernel,
        grid=(b, s // TQB),
        in_specs=[
            pl.BlockSpec((1, TQB, d), lambda bi, i: (bi, i, 0)),
            feat_spec,
            pl.BlockSpec((1, B_PREV, PAIR_W), lambda bi, i: (bi, prev(i), 0)),
            pl.BlockSpec((1, TQB, PAIR_W), lambda bi, i: (bi, i, 0)),
            pl.BlockSpec((1, PAIR_W, B_PREV), lambda bi, i: (bi, 0, prev(i))),
            pl.BlockSpec((1, PAIR_W, TQB), lambda bi, i: (bi, 0, i)),
            feat_spec,
            _const_spec(vec.shape),
            _const_spec(sinks.shape),
            _const_spec(wout.shape),
            _const_spec(fg.shape),
        ],
        out_specs=pl.BlockSpec((1, TQB, d), lambda bi, i: (bi, i, 0)),
        out_shape=jax.ShapeDtypeStruct((b, s, d), jnp.float32),
        scratch_shapes=[pltpu.VMEM((d, TQB), jnp.bfloat16),
                        pltpu.VMEM((2 * N_PAIRS, B_WIN, TQ), jnp.bfloat16),
                        *[pltpu.VMEM((B_WIN, TQ), jnp.float32)] * N_SCORE_BUFS_B],
        compiler_params=pltpu.CompilerParams(
            dimension_semantics=("arbitrary", "arbitrary"), vmem_limit_bytes=VMEM_LIMIT),
        name="attn_b",
    )(h, qt4, ksh, ksh, vsht, vsht, gt4, vec, sinks, wout, fg)


def _slab_diff():
    u = jnp.arange(BIAS_PERIOD)
    return jnp.where(u < TQ, u, u - BIAS_PERIOD)


def _bias_vecs_a(rel_bias):
    back = (A_KBLOCKS - 1 - jnp.arange(A_KBLOCKS))[:, None] * TQ
    dist = back + _slab_diff()[None, :]
    idx = jnp.clip(dist, -A_REL_CLIP, A_REL_CLIP) + A_REL_CLIP
    vec = jnp.transpose(rel_bias[idx], (0, 2, 1)).astype(jnp.float32)
    return vec[:, :, None, :]


def _t5_bucket(rel):
    nb = T5_BUCKETS // 2
    max_exact = nb // 2
    ret = jnp.where(rel > 0, nb, 0)
    n = jnp.abs(rel)
    nf = jnp.maximum(n, 1).astype(jnp.float32)
    large = max_exact + (jnp.log(nf / max_exact) / math.log(T5_MAX_DIST / max_exact)
                         * (nb - max_exact)).astype(jnp.int32)
    large = jnp.minimum(large, nb - 1)
    return ret + jnp.where(n < max_exact, n, large)


def _bias_vec_b(t5_table):
    rel = -_slab_diff() - B_PREV
    vec = jnp.transpose(t5_table[_t5_bucket(rel)], (1, 0)).astype(jnp.float32)
    return vec[None, :, None, :]


def kernel(x, a_norm, a_w_in, a_rel_bias, a_w_out, kv_norm, kv_w, t5_bias,
           b_norm, b_w_in, b_sinks, b_w_out, final_norm):
    assert a_norm.shape[0] == 1 and b_norm.shape[0] == 1, "one A layer then one B layer"
    bf = jnp.bfloat16
    scale = HEAD_DIM ** -0.5 * LOG2E

    def q_col_scale(n):
        return jnp.where(jnp.arange(n) < D_MODEL, scale, 1.0).astype(jnp.float32)

    wt_a = _weight_t(a_w_in[0], q_col_scale(a_w_in.shape[2]), a_norm[0], W_T_COLS_A)
    qt, k, vt, gt = _proj_a(x, wt_a)

    nwt = _weight_t(b_w_in[0], q_col_scale(b_w_in.shape[2]), b_norm[0], W_T_COLS_B,
                    head=kv_w, head_block=1, head_w=PAIR_W, head_gain=kv_norm)
    kvw_k = (kv_w[:, :PAIR_W] * kv_norm[:, None]).astype(bf)
    h, ksh, vsht, qbt, gbt = _attn_a(
        x, qt, k, vt, gt, _bias_vecs_a(a_rel_bias[0] * LOG2E), a_w_out[0].astype(bf), kvw_k, nwt)

    sinks = jnp.repeat((b_sinks[0].astype(jnp.float32) * LOG2E).reshape(N_PAIRS, 1, 2), LANES, axis=2)
    return _attn_b(h, qbt, ksh, vsht, gbt, _bias_vec_b(t5_bias * LOG2E), sinks, b_w_out[0].astype(bf),
                   final_norm[None, :])
```

```python
import functools
import math

import jax
import jax.numpy as jnp
from jax import lax
from jax.experimental import pallas as pl
from jax.experimental.pallas import tpu as pltpu

D_MODEL = 1024
HEAD_DIM = 64
N_HEADS = D_MODEL // HEAD_DIM
N_PAIRS = N_HEADS // 2
PAIR_W = 2 * HEAD_DIM
CHUNK = 64
RMS_EPS = 1e-6
A_LEFT_CHUNKS = 8
A_REL_CLIP = 256
B_KV_HEADS = 2
B_LEFT_CHUNKS = 2
T5_BUCKETS = 32
T5_MAX_DIST = 128

TQ = 256
TQB = 512
TQA = 512
A_KBLOCKS = A_LEFT_CHUNKS * CHUNK // TQ + 1
B_PREV = B_LEFT_CHUNKS * CHUNK
TM_PROJ = 1024
FEAT_CHUNK = 256
W_T_COLS_A = 1024
W_T_COLS_B = 512
MASKED = -1e30
BIAS_PERIOD = 2 * TQ
SUBLANES = 8
LANES = 128
MAX_CHAINS = 1
BF16_ROWS = 16
LOG2E = math.log2(math.e)
N_SCORE_BUFS = 4
N_SCORE_BUFS_B = 8
VMEM_LIMIT = 56 * 1024 * 1024

_NT = (((1,), (1,)), ((), ()))
_TN = (((0,), (0,)), ((), ()))


def _rms_scale(xf):
    return lax.rsqrt(jnp.mean(xf * xf, axis=-1, keepdims=True) + RMS_EPS)


def _token_scales(xf):
    col = _rms_scale(xf)
    row = jnp.transpose(jnp.broadcast_to(col, (xf.shape[0], LANES)))[0:1, :]
    return col, row


def _silu(v):
    return v * jax.nn.sigmoid(v)


def _weight_t_kernel(w_ref, cs_ref, gain_ref, o_ref):
    o_ref[...] = ((w_ref[...] * cs_ref[...]).T * gain_ref[...]).astype(jnp.bfloat16)


def _weight_t(w, col_scale, row_gain, tc):
    d, n = w.shape
    return pl.pallas_call(
        _weight_t_kernel,
        grid=(n // tc,),
        in_specs=[
            pl.BlockSpec((d, tc), lambda j: (0, j)),
            pl.BlockSpec((1, tc), lambda j: (0, j)),
            pl.BlockSpec((1, d), lambda j: (0, 0)),
        ],
        out_specs=pl.BlockSpec((tc, d), lambda j: (j, 0)),
        out_shape=jax.ShapeDtypeStruct((n, d), jnp.bfloat16),
        compiler_params=pltpu.CompilerParams(dimension_semantics=("arbitrary",), vmem_limit_bytes=VMEM_LIMIT),
        name="weight_t",
    )(w, col_scale[None, :], row_gain[None, :])


def _proj_a_kernel(x_ref, wt_ref, head_ref, bw_ref, cs_ref, gain_ref, qt_ref, k_ref, vt_ref, gt_ref, nwt_ref):
    step = pl.program_id(0) * pl.num_programs(1) + pl.program_id(1)
    src = jnp.where(step == 0, head_ref[...], bw_ref[...])
    nwt_ref[...] = ((src * cs_ref[...]).T * gain_ref[0]).astype(jnp.bfloat16)

    xf = x_ref[0]
    xb = xf.astype(jnp.bfloat16)
    r_col, r_row = _token_scales(xf)
    k = lax.dot_general(xb, wt_ref[D_MODEL:2 * D_MODEL, :], _NT,
                        preferred_element_type=jnp.float32) * r_col
    for p in range(N_PAIRS):
        k_ref[0, p] = k[:, p * PAIR_W:(p + 1) * PAIR_W].astype(jnp.bfloat16)
    for out_ref, base, act in ((gt_ref, 3 * D_MODEL, _silu), (vt_ref, 2 * D_MODEL, None), (qt_ref, 0, None)):
        for c in range(0, D_MODEL, FEAT_CHUNK):
            w = wt_ref[base + c:base + c + FEAT_CHUNK, :]
            yt = lax.dot_general(w, xb, _NT, preferred_element_type=jnp.float32) * r_row
            out_ref[0, c:c + FEAT_CHUNK, :] = (yt if act is None else act(yt)).astype(jnp.bfloat16)


def _proj_a(x, wt, head_src, b_w, b_col_scale, gains):
    b, s, d = x.shape
    n_i = s // TM_PROJ
    n_blocks = 1 + b_w.shape[1] // W_T_COLS_B
    assert n_blocks <= b * n_i and head_src.shape == (d, W_T_COLS_B)
    blk = lambda bi, i: jnp.minimum(bi * n_i + i, n_blocks - 1)
    cs_all = jnp.concatenate([jnp.ones((W_T_COLS_B,), jnp.float32), b_col_scale])[None, :]
    feat = jax.ShapeDtypeStruct((b, d, s), jnp.bfloat16)
    return pl.pallas_call(
        _proj_a_kernel,
        grid=(b, n_i),
        in_specs=[
            pl.BlockSpec((1, TM_PROJ, d), lambda bi, i: (bi, i, 0)),
            _const_spec(wt.shape),
            _const_spec(head_src.shape),
            pl.BlockSpec((d, W_T_COLS_B), lambda bi, i: (0, jnp.maximum(blk(bi, i) - 1, 0))),
            pl.BlockSpec((1, W_T_COLS_B), lambda bi, i: (0, blk(bi, i))),
            pl.BlockSpec((1, 1, d), lambda bi, i: (jnp.minimum(blk(bi, i), 1), 0, 0)),
        ],
        out_specs=[
            pl.BlockSpec((1, d, TM_PROJ), lambda bi, i: (bi, 0, i)),
            pl.BlockSpec((1, N_PAIRS, TM_PROJ, PAIR_W), lambda bi, i: (bi, 0, i, 0)),
            pl.BlockSpec((1, d, TM_PROJ), lambda bi, i: (bi, 0, i)),
            pl.BlockSpec((1, d, TM_PROJ), lambda bi, i: (bi, 0, i)),
            pl.BlockSpec((W_T_COLS_B, d), lambda bi, i: (blk(bi, i), 0)),
        ],
        out_shape=[feat, jax.ShapeDtypeStruct((b, N_PAIRS, s, PAIR_W), jnp.bfloat16), feat, feat,
                   jax.ShapeDtypeStruct((n_blocks * W_T_COLS_B, d), jnp.bfloat16)],
        compiler_params=pltpu.CompilerParams(
            dimension_semantics=("arbitrary", "arbitrary"), vmem_limit_bytes=VMEM_LIMIT),
        name="proj_a",
    )(x, wt, head_src, b_w, cs_all, gains[:, None, :])


def _chunk_iota(shape, axis):
    return lax.shift_right_logical(lax.broadcasted_iota(jnp.int32, shape, axis), CHUNK.bit_length() - 1)


def _identity_bf16(n):
    return (lax.broadcasted_iota(jnp.int32, (n, n), 0) == lax.broadcasted_iota(jnp.int32, (n, n), 1)).astype(jnp.bfloat16)


def _toeplitz(vec_row, n_keys):
    return pltpu.roll(jnp.broadcast_to(vec_row, (n_keys, BIAS_PERIOD)), 0, 1, stride=1, stride_axis=0)


def _fill_bias_slabs(vec_ref, bias_ref, n_keys, bands):
    n = len(bands)
    kch = _chunk_iota((n_keys, TQ), 0)
    qch = _chunk_iota((n_keys, TQ), 1)

    def head_body(h, carry):
        for j, band in enumerate(bands):
            t = _toeplitz(vec_ref[j, h], n_keys)[:, :TQ]
            if band is not None:
                t = jnp.where(band(kch, qch), t, MASKED)
            bias_ref[h * n + j] = t
        return carry

    lax.fori_loop(0, N_HEADS, head_body, 0)


def _live_rows(band, n_keys):
    chunks_per_half = LANES // CHUNK
    out = []
    for half in range(TQ // LANES):
        qchs = range(half * chunks_per_half, (half + 1) * chunks_per_half)
        live = [kc for kc in range(n_keys // CHUNK) if band is None or any(band(kc, qc) for qc in qchs)]
        out.append((live[0] * CHUNK, (live[-1] + 1) * CHUNK) if live else (0, 0))
    return tuple(out)


def _scores_stage(k_tiles, q_tiles, bias_fns, live, s_ref):
    maxes, row = [], 0
    for half in range(TQ // LANES):
        maxes.append([None] * MAX_CHAINS)
    for kt, qz, bias_fn, live_j in zip(k_tiles, q_tiles, bias_fns, live):
        s = jnp.dot(kt, qz, preferred_element_type=jnp.float32)
        for r in range(s.shape[0] // SUBLANES):
            rows = slice(r * SUBLANES, (r + 1) * SUBLANES)
            for half, (r0, r1) in enumerate(live_j):
                if not r0 <= r * SUBLANES < r1:
                    continue
                lanes = slice(half * LANES, (half + 1) * LANES)
                acc = maxes[half]
                grp = s[rows, lanes] if bias_fn is None else s[rows, lanes] + bias_fn(rows, lanes)
                s_ref[row + r * SUBLANES:row + (r + 1) * SUBLANES, lanes] = grp
                c = r % MAX_CHAINS
                acc[c] = grp if acc[c] is None else jnp.maximum(acc[c], grp)
        row += s.shape[0]
    cols = [jnp.max(functools.reduce(jnp.maximum, [a for a in acc if a is not None]), axis=0, keepdims=True)
            for acc in maxes]
    return jnp.concatenate(cols, axis=1)


def _pv_stage(s_ref, m, v_tiles, live, extra_logit=None):
    if extra_logit is not None:
        m = jnp.maximum(m, extra_logit)
    acc, row = None, 0
    for vt, live_j in zip(v_tiles, live):
        n = vt.shape[1]
        halves = []
        for half, (r0, r1) in enumerate(live_j):
            lanes = slice(half * LANES, (half + 1) * LANES)
            parts = [jnp.zeros((r0, LANES), jnp.bfloat16)] if r0 else []
            if r1 > r0:
                parts.append(jnp.exp2(s_ref[row + r0:row + r1, lanes] - m[:, lanes]).astype(jnp.bfloat16))
            if n > r1:
                parts.append(jnp.zeros((n - r1, LANES), jnp.bfloat16))
            halves.append(parts[0] if len(parts) == 1 else jnp.concatenate(parts, axis=0))
        p = jnp.concatenate(halves, axis=1)
        v_ones = jnp.concatenate([vt, jnp.ones((BF16_ROWS, n), vt.dtype)], axis=0)
        part = jnp.dot(v_ones, p, preferred_element_type=jnp.float32)
        acc = part if acc is None else acc + part
        row += n
    l = acc[HEAD_DIM:HEAD_DIM + 1, :]
    if extra_logit is not None:
        l = l + jnp.exp2(extra_logit - m)
    return acc[:HEAD_DIM, :] / l


def _head_pipeline(n_tiles, scores_fn, pv_fn, s_refs, scores_first, per_pair=2):
    per_group = len(s_refs) // 2
    x_refs, y_refs = s_refs[:per_group], s_refs[per_group:]
    n_groups = n_tiles // per_group

    m = [scores_fn(t // per_pair, t % per_pair, ref) for t, ref in zip(range(per_group), x_refs)]
    for g in range(n_groups):
        src, dst = (x_refs, y_refs) if g % 2 == 0 else (y_refs, x_refs)
        nxt = [(g + 1) * per_group + j for j in range(per_group)] if g + 1 < n_groups else []
        m_next = [scores_fn(t // per_pair, t % per_pair, ref) for t, ref in zip(nxt, dst)] if scores_first else []
        for j in range(per_group):
            if not scores_first and nxt:
                m_next.append(scores_fn(nxt[j] // per_pair, nxt[j] % per_pair, dst[j]))
            t = g * per_group + j
            pv_fn(t // per_pair, t % per_pair, src[j], m[j])
        m = m_next


A_BANDS = (lambda kc, qc: kc >= qc,
           None,
           lambda kc, qc: kc <= qc)
A_LIVE = tuple(_live_rows(band, TQ) for band in A_BANDS)
B_WIN = B_PREV + LANES
B_SUB = PAIR_W
B_BAND = lambda kc, qc: (kc >= qc) & (kc <= qc + B_LEFT_CHUNKS)
B_LIVE = (((0, B_WIN),) * (TQ // LANES),)


def _out_and_residual(zt_ref, wout_ref, res):
    y = lax.dot_general(zt_ref[...], wout_ref[...], _TN, preferred_element_type=jnp.float32)
    return res + y


def _attn_a_kernel(x_ref, qt_ref, kp_ref, ko_ref, vp_ref, vo_ref, gt_ref,
                   bvec_ref, wout_ref, kvwk_ref, nwt_ref,
                   h_ref, ksh_ref, vsht_ref, qbt_ref, gbt_ref, zt_ref, bias_ref, *s_refs):
    i = pl.program_id(1)
    subs = TQA // TQ

    @pl.when((pl.program_id(0) == 0) & (i == 0))
    def _():
        _fill_bias_slabs(bvec_ref, bias_ref, TQ, A_BANDS)
        bias_ref[N_HEADS * A_KBLOCKS] = jnp.full((TQ, TQ), MASKED, jnp.float32)

    n_slabs = N_HEADS * A_KBLOCKS
    rows = lax.broadcasted_iota(jnp.int32, (PAIR_W, TQ), 0)

    def key_block(sb, j):
        n = sb + j
        return n // subs, (n % subs) * TQ

    def scores_fn(sp, hh, s_ref):
        sb, p = divmod(sp, N_PAIRS)
        q2 = qt_ref[0, p, :, sb * TQ:(sb + 1) * TQ]
        own_rows = rows < HEAD_DIM if hh == 0 else rows >= HEAD_DIM
        qz = jnp.where(own_rows, q2, jnp.zeros_like(q2))
        k_tiles, bias_fns = [], []
        for j in range(A_KBLOCKS):
            in_seq = i * subs + sb - (A_KBLOCKS - 1) + j >= 0
            slab = jnp.where(in_seq, (2 * p + hh) * A_KBLOCKS + j, n_slabs)
            own, t0 = key_block(sb, j)
            k_tiles.append((ko_ref if own else kp_ref)[0, p, t0:t0 + TQ, :])
            bias_fns.append(lambda rws, lanes, slab=slab: bias_ref[slab, rws, lanes])
        return _scores_stage(k_tiles, [qz] * A_KBLOCKS, bias_fns, A_LIVE, s_ref)

    def pv_fn(sp, hh, s_ref, m):
        sb, p = divmod(sp, N_PAIRS)
        half = slice(hh * HEAD_DIM, (hh + 1) * HEAD_DIM)
        qcols = slice(sb * TQ, (sb + 1) * TQ)
        v_tiles = []
        for j in range(A_KBLOCKS):
            own, t0 = key_block(sb, j)
            v_tiles.append((vo_ref if own else vp_ref)[0, p, half, t0:t0 + TQ])
        o = _pv_stage(s_ref, m, v_tiles, A_LIVE)
        gate = gt_ref[0, p, half, qcols].astype(jnp.float32)
        row0 = p * PAIR_W + hh * HEAD_DIM
        zt_ref[row0:row0 + HEAD_DIM, qcols] = (o * gate).astype(jnp.bfloat16)

    _head_pipeline(subs * N_HEADS, scores_fn, pv_fn, s_refs, scores_first=True)

    h = _out_and_residual(zt_ref, wout_ref, x_ref[0])
    h_ref[0] = h
    hb = h.astype(jnp.bfloat16)
    r_col, r_row = _token_scales(h)
    ksh_ref[0] = (jnp.dot(hb, kvwk_ref[...], preferred_element_type=jnp.float32) * r_col).astype(jnp.bfloat16)
    for out_ref, base, act in ((gbt_ref, W_T_COLS_B + D_MODEL, _silu), (qbt_ref, W_T_COLS_B, None)):
        for c in range(0, D_MODEL, FEAT_CHUNK):
            first = out_ref is qbt_ref and c == 0
            start = base + c - (PAIR_W if first else 0)
            yt = lax.dot_general(nwt_ref[start:base + c + FEAT_CHUNK, :], hb, _NT,
                                 preferred_element_type=jnp.float32) * r_row
            if first:
                vsht_ref[0] = yt[:PAIR_W].astype(jnp.bfloat16)
                yt = yt[PAIR_W:]
            out_ref[0, c:c + FEAT_CHUNK, :] = (yt if act is None else act(yt)).astype(jnp.bfloat16)


def _const_spec(shape):
    return pl.BlockSpec(shape, lambda bi, i: (0,) * len(shape), pipeline_mode=pl.Buffered(1))


def _attn_a(x, qt, k, vt, gt, bvec, wout, kvwk, nwt):
    b, s, d = x.shape
    qt4 = qt.reshape(b, N_PAIRS, PAIR_W, s)
    vt4 = vt.reshape(b, N_PAIRS, PAIR_W, s)
    gt4 = gt.reshape(b, N_PAIRS, PAIR_W, s)
    assert (A_KBLOCKS - 1) * TQ <= TQA, "the keys before a grid block must fit in one previous block"
    prev = lambda i: jnp.maximum(i - 1, 0)

    def kspec(index):
        return pl.BlockSpec((1, N_PAIRS, TQA, PAIR_W), lambda bi, i: (bi, 0, index(i), 0))

    def vspec(index):
        return pl.BlockSpec((1, N_PAIRS, PAIR_W, TQA), lambda bi, i: (bi, 0, 0, index(i)))

    feat_spec = pl.BlockSpec((1, N_PAIRS, PAIR_W, TQA), lambda bi, i: (bi, 0, 0, i))
    featout_spec = pl.BlockSpec((1, d, TQA), lambda bi, i: (bi, 0, i))
    feat = jax.ShapeDtypeStruct((b, d, s), jnp.bfloat16)
    return pl.pallas_call(
        _attn_a_kernel,
        grid=(b, s // TQA),
        in_specs=[
            pl.BlockSpec((1, TQA, d), lambda bi, i: (bi, i, 0)),
            feat_spec,
            kspec(prev), kspec(lambda i: i),
            vspec(prev), vspec(lambda i: i),
            feat_spec,
            _const_spec(bvec.shape),
            _const_spec(wout.shape),
            _const_spec(kvwk.shape),
            _const_spec(nwt.shape),
        ],
        out_specs=[
            pl.BlockSpec((1, TQA, d), lambda bi, i: (bi, i, 0)),
            pl.BlockSpec((1, TQA, PAIR_W), lambda bi, i: (bi, i, 0)),
            pl.BlockSpec((1, PAIR_W, TQA), lambda bi, i: (bi, 0, i)),
            featout_spec,
            featout_spec,
        ],
        out_shape=[
            jax.ShapeDtypeStruct((b, s, d), jnp.float32),
            jax.ShapeDtypeStruct((b, s, PAIR_W), jnp.bfloat16),
            jax.ShapeDtypeStruct((b, PAIR_W, s), jnp.bfloat16),
            feat, feat,
        ],
        scratch_shapes=[pltpu.VMEM((d, TQA), jnp.bfloat16),
                        pltpu.VMEM((N_HEADS * A_KBLOCKS + 1, TQ, TQ), jnp.float32),
                        *[pltpu.VMEM((A_KBLOCKS * TQ, TQ), jnp.float32)] * N_SCORE_BUFS],
        compiler_params=pltpu.CompilerParams(
            dimension_semantics=("arbitrary", "arbitrary"), vmem_limit_bytes=VMEM_LIMIT),
        name="attn_a",
    )(x, qt4, k, k, vt4, vt4, gt4, bvec, wout, kvwk, nwt)


def _attn_b_kernel(h_ref, qt_ref, kp_ref, ko_ref, vp_ref, vo_ref, gt_ref, vec_ref,
                   sink_ref, wout_ref, fg_ref, out_ref, zt_ref, bias_ref, *s_refs):
    i = pl.program_id(1)

    @pl.when((pl.program_id(0) == 0) & (i == 0))
    def _():
        in_band = B_BAND(_chunk_iota((B_WIN, LANES), 0), _chunk_iota((B_WIN, LANES), 1))
        before_block = lax.broadcasted_iota(jnp.int32, (B_WIN, LANES), 0) < B_PREV

        def pair_body(p, carry):
            for hh in range(2):
                t = jnp.where(in_band, _toeplitz(vec_ref[0, 2 * p + hh], B_WIN)[:, :LANES], MASKED)
                lanes = slice(hh * LANES, (hh + 1) * LANES)
                bias_ref[2 * p, :, lanes] = t.astype(jnp.bfloat16)
                bias_ref[2 * p + 1, :, lanes] = jnp.where(before_block, MASKED, t).astype(jnp.bfloat16)
            return carry

        lax.fori_loop(0, N_PAIRS, pair_body, 0)

    rows = lax.broadcasted_iota(jnp.int32, (PAIR_W, TQ), 0)
    pairs_per_kv = N_PAIRS // B_KV_HEADS
    eye = _identity_bf16(B_SUB)

    def scores_fn(p, u, s_ref):
        qlanes = slice(u * LANES, (u + 1) * LANES)
        q2 = jnp.concatenate([qt_ref[0, p, :HEAD_DIM, qlanes], qt_ref[0, p, HEAD_DIM:, qlanes]], axis=1)
        kv0 = HEAD_DIM * (p // pairs_per_kv)
        kv_rows = (rows >= kv0) & (rows < kv0 + HEAD_DIM)
        qz = jnp.where(kv_rows, jnp.concatenate([q2, q2], axis=0), jnp.zeros((PAIR_W, TQ), q2.dtype))
        if u == 0:
            k_tile = jnp.concatenate([kp_ref[0], ko_ref[0, :LANES, :]], axis=0)
            variant = jnp.where(i == 0, 1, 0)
        else:
            k_tile, variant = ko_ref[0, u * LANES - B_PREV:(u + 1) * LANES, :], 0

        lhs, rhs = [], []
        for r0 in range(0, B_WIN, B_SUB):
            lhs.append(jnp.concatenate([k_tile[r0:r0 + B_SUB, :], eye], axis=1))
            rhs.append(jnp.concatenate([qz, bias_ref[2 * p + variant, r0:r0 + B_SUB, :]], axis=0))
        sub_live = (((0, B_SUB),) * (TQ // LANES),) * len(lhs)
        return _scores_stage(lhs, rhs, [None] * len(lhs), sub_live, s_ref)

    def pv_fn(p, u, s_ref, m):
        qlanes = slice(u * LANES, (u + 1) * LANES)
        kv_rows = slice(HEAD_DIM * (p // pairs_per_kv), HEAD_DIM * (p // pairs_per_kv + 1))
        if u == 0:
            v_tile = jnp.concatenate([vp_ref[0, kv_rows, :], vo_ref[0, kv_rows, :LANES]], axis=1)
        else:
            v_tile = vo_ref[0, kv_rows, u * LANES - B_PREV:(u + 1) * LANES]
        o = _pv_stage(s_ref, m, [v_tile], B_LIVE, extra_logit=sink_ref[p])
        for hh in range(2):
            hrows = slice(hh * HEAD_DIM, (hh + 1) * HEAD_DIM)
            gate = gt_ref[0, p, hrows, qlanes].astype(jnp.float32)
            row0 = p * PAIR_W + hh * HEAD_DIM
            zt_ref[row0:row0 + HEAD_DIM, qlanes] = (o[:, hh * LANES:(hh + 1) * LANES] * gate).astype(jnp.bfloat16)

    _head_pipeline(N_PAIRS * (TQB // LANES), scores_fn, pv_fn, s_refs, scores_first=False, per_pair=TQB // LANES)

    h2 = _out_and_residual(zt_ref, wout_ref, h_ref[0])
    out_ref[0] = (h2 * fg_ref[...]) * _rms_scale(h2)


def _attn_b(h, qbt, ksh, vsht, gbt, vec, sinks, wout, fg):
    b, s, d = h.shape
    qt4 = qbt.reshape(b, N_PAIRS, PAIR_W, s)
    gt4 = gbt.reshape(b, N_PAIRS, PAIR_W, s)
    feat_spec = pl.BlockSpec((1, N_PAIRS, PAIR_W, TQB), lambda bi, i: (bi, 0, 0, i))
    prev = lambda i: jnp.maximum(i * (TQB // B_PREV) - 1, 0)
    return pl.pallas_call(
        _attn_b_kernel,
        grid=(b, s // TQB),
        in_specs=[
            pl.BlockSpec((1, TQB, d), lambda bi, i: (bi, i, 0)),
            feat_spec,
            pl.BlockSpec((1, B_PREV, PAIR_W), lambda bi, i: (bi, prev(i), 0)),
            pl.BlockSpec((1, TQB, PAIR_W), lambda bi, i: (bi, i, 0)),
            pl.BlockSpec((1, PAIR_W, B_PREV), lambda bi, i: (bi, 0, prev(i))),
            pl.BlockSpec((1, PAIR_W, TQB), lambda bi, i: (bi, 0, i)),
            feat_spec,
            _const_spec(vec.shape),
            _const_spec(sinks.shape),
            _const_spec(wout.shape),
            _const_spec(fg.shape),
        ],
        out_specs=pl.BlockSpec((1, TQB, d), lambda bi, i: (bi, i, 0)),
        out_shape=jax.ShapeDtypeStruct((b, s, d), jnp.float32),
        scratch_shapes=[pltpu.VMEM((d, TQB), jnp.bfloat16),
                        pltpu.VMEM((2 * N_PAIRS, B_WIN, TQ), jnp.bfloat16),
                        *[pltpu.VMEM((B_WIN, TQ), jnp.float32)] * N_SCORE_BUFS_B],
        compiler_params=pltpu.CompilerParams(
            dimension_semantics=("arbitrary", "arbitrary"), vmem_limit_bytes=VMEM_LIMIT),
        name="attn_b",
    )(h, qt4, ksh, ksh, vsht, vsht, gt4, vec, sinks, wout, fg)


def _slab_diff():
    u = jnp.arange(BIAS_PERIOD)
    return jnp.where(u < TQ, u, u - BIAS_PERIOD)


def _bias_vecs_a(rel_bias):
    back = (A_KBLOCKS - 1 - jnp.arange(A_KBLOCKS))[:, None] * TQ
    dist = back + _slab_diff()[None, :]
    idx = jnp.clip(dist, -A_REL_CLIP, A_REL_CLIP) + A_REL_CLIP
    vec = jnp.transpose(rel_bias[idx], (0, 2, 1)).astype(jnp.float32)
    return vec[:, :, None, :]


def _t5_bucket(rel):
    nb = T5_BUCKETS // 2
    max_exact = nb // 2
    ret = jnp.where(rel > 0, nb, 0)
    n = jnp.abs(rel)
    nf = jnp.maximum(n, 1).astype(jnp.float32)
    large = max_exact + (jnp.log(nf / max_exact) / math.log(T5_MAX_DIST / max_exact)
                         * (nb - max_exact)).astype(jnp.int32)
    large = jnp.minimum(large, nb - 1)
    return ret + jnp.where(n < max_exact, n, large)


def _bias_vec_b(t5_table):
    rel = -_slab_diff() - B_PREV
    vec = jnp.transpose(t5_table[_t5_bucket(rel)], (1, 0)).astype(jnp.float32)
    return vec[None, :, None, :]


def kernel(x, a_norm, a_w_in, a_rel_bias, a_w_out, kv_norm, kv_w, t5_bias,
           b_norm, b_w_in, b_sinks, b_w_out, final_norm):
    assert a_norm.shape[0] == 1 and b_norm.shape[0] == 1, "one A layer then one B layer"
    bf = jnp.bfloat16
    scale = HEAD_DIM ** -0.5 * LOG2E

    def q_col_scale(n):
        return jnp.where(jnp.arange(n) < D_MODEL, scale, 1.0).astype(jnp.float32)

    wt_a = _weight_t(a_w_in[0], q_col_scale(a_w_in.shape[2]), a_norm[0], W_T_COLS_A)
    head_src = jnp.pad(kv_w[:, PAIR_W:], ((0, 0), (W_T_COLS_B - PAIR_W, 0)))
    qt, k, vt, gt, nwt = _proj_a(x, wt_a, head_src, b_w_in[0], q_col_scale(b_w_in.shape[2]),
                                 jnp.stack([kv_norm, b_norm[0]]))

    kvw_k = (kv_w[:, :PAIR_W] * kv_norm[:, None]).astype(bf)
    h, ksh, vsht, qbt, gbt = _attn_a(
        x, qt, k, vt, gt, _bias_vecs_a(a_rel_bias[0] * LOG2E), a_w_out[0].astype(bf), kvw_k, nwt)

    sinks = jnp.repeat((b_sinks[0].astype(jnp.float32) * LOG2E).reshape(N_PAIRS, 1, 2), LANES, axis=2)
    return _attn_b(h, qbt, ksh, vsht, gbt, _bias_vec_b(t5_bias * LOG2E), sinks, b_w_out[0].astype(bf),
                   final_norm[None, :])
```

```python
import functools
import math

import jax
import jax.numpy as jnp
from jax import lax
from jax.experimental import pallas as pl
from jax.experimental.pallas import tpu as pltpu

D_MODEL = 1024
HEAD_DIM = 64
N_HEADS = D_MODEL // HEAD_DIM
N_PAIRS = N_HEADS // 2
PAIR_W = 2 * HEAD_DIM
CHUNK = 64
RMS_EPS = 1e-6
A_LEFT_CHUNKS = 8
A_REL_CLIP = 256
B_KV_HEADS = 2
B_LEFT_CHUNKS = 2
T5_BUCKETS = 32
T5_MAX_DIST = 128

TQ = 256
TQB = 512
TQA = 512
A_KBLOCKS = A_LEFT_CHUNKS * CHUNK // TQ + 1
B_PREV = B_LEFT_CHUNKS * CHUNK
TM_PROJ = 1024
FEAT_CHUNK = 256
W_T_COLS_A = 1024
W_T_COLS_B = 512
MASKED = -1e30
BIAS_PERIOD = 2 * TQ
SUBLANES = 8
LANES = 128
MAX_CHAINS = 1
BF16_ROWS = 16
LOG2E = math.log2(math.e)
N_SCORE_BUFS = 4
N_SCORE_BUFS_B = 8
VMEM_LIMIT = 56 * 1024 * 1024

_NT = (((1,), (1,)), ((), ()))
_TN = (((0,), (0,)), ((), ()))


def _rms_scale(xf):
    return lax.rsqrt(jnp.mean(xf * xf, axis=-1, keepdims=True) + RMS_EPS)


def _token_scales(xf):
    col = _rms_scale(xf)
    row = jnp.transpose(jnp.broadcast_to(col, (xf.shape[0], LANES)))[0:1, :]
    return col, row


def _silu(v):
    return v * jax.nn.sigmoid(v)


def _weight_t_kernel(w_ref, cs_ref, gain_ref, o_ref):
    o_ref[...] = ((w_ref[...] * cs_ref[...]).T * gain_ref[...]).astype(jnp.bfloat16)


def _weight_t(w, col_scale, row_gain, tc):
    d, n = w.shape
    return pl.pallas_call(
        _weight_t_kernel,
        grid=(n // tc,),
        in_specs=[
            pl.BlockSpec((d, tc), lambda j: (0, j)),
            pl.BlockSpec((1, tc), lambda j: (0, j)),
            pl.BlockSpec((1, d), lambda j: (0, 0)),
        ],
        out_specs=pl.BlockSpec((tc, d), lambda j: (j, 0)),
        out_shape=jax.ShapeDtypeStruct((n, d), jnp.bfloat16),
        compiler_params=pltpu.CompilerParams(dimension_semantics=("arbitrary",), vmem_limit_bytes=VMEM_LIMIT),
        name="weight_t",
    )(w, col_scale[None, :], row_gain[None, :])


def _proj_a_kernel(x_ref, wt_ref, kvw_ref, kvg_ref, bw_ref, cs_ref, gain_ref,
                   qt_ref, k_ref, vt_ref, gt_ref, nwt_ref, kvwk_ref):
    step = pl.program_id(0) * pl.num_programs(1) + pl.program_id(1)
    head = jnp.concatenate([jnp.zeros((D_MODEL, W_T_COLS_B - PAIR_W), jnp.float32), kvw_ref[:, PAIR_W:]], axis=1)
    src = jnp.where(step == 0, head, bw_ref[...])
    nwt_ref[...] = ((src * cs_ref[...]).T * gain_ref[0]).astype(jnp.bfloat16)
    kvwk_ref[...] = (kvw_ref[:, :PAIR_W] * kvg_ref[...]).astype(jnp.bfloat16)

    xf = x_ref[0]
    xb = xf.astype(jnp.bfloat16)
    r_col, r_row = _token_scales(xf)
    k = lax.dot_general(xb, wt_ref[D_MODEL:2 * D_MODEL, :], _NT,
                        preferred_element_type=jnp.float32) * r_col
    for p in range(N_PAIRS):
        k_ref[0, p] = k[:, p * PAIR_W:(p + 1) * PAIR_W].astype(jnp.bfloat16)
    for out_ref, base, act in ((gt_ref, 3 * D_MODEL, _silu), (vt_ref, 2 * D_MODEL, None), (qt_ref, 0, None)):
        for c in range(0, D_MODEL, FEAT_CHUNK):
            w = wt_ref[base + c:base + c + FEAT_CHUNK, :]
            yt = lax.dot_general(w, xb, _NT, preferred_element_type=jnp.float32) * r_row
            out_ref[0, c:c + FEAT_CHUNK, :] = (yt if act is None else act(yt)).astype(jnp.bfloat16)


def _proj_a(x, wt, kv_w, b_w, b_col_scale, gains):
    b, s, d = x.shape
    n_i = s // TM_PROJ
    n_blocks = 1 + b_w.shape[1] // W_T_COLS_B
    assert n_blocks <= b * n_i and kv_w.shape == (d, 2 * PAIR_W)
    blk = lambda bi, i: jnp.minimum(bi * n_i + i, n_blocks - 1)
    cs_all = jnp.concatenate([jnp.ones((W_T_COLS_B,), jnp.float32), b_col_scale])[None, :]
    feat = jax.ShapeDtypeStruct((b, d, s), jnp.bfloat16)
    return pl.pallas_call(
        _proj_a_kernel,
        grid=(b, n_i),
        in_specs=[
            pl.BlockSpec((1, TM_PROJ, d), lambda bi, i: (bi, i, 0)),
            _const_spec(wt.shape),
            _const_spec(kv_w.shape),
            _const_spec((d, 1)),
            pl.BlockSpec((d, W_T_COLS_B), lambda bi, i: (0, jnp.maximum(blk(bi, i) - 1, 0))),
            pl.BlockSpec((1, W_T_COLS_B), lambda bi, i: (0, blk(bi, i))),
            pl.BlockSpec((1, 1, d), lambda bi, i: (jnp.minimum(blk(bi, i), 1), 0, 0)),
        ],
        out_specs=[
            pl.BlockSpec((1, d, TM_PROJ), lambda bi, i: (bi, 0, i)),
            pl.BlockSpec((1, N_PAIRS, TM_PROJ, PAIR_W), lambda bi, i: (bi, 0, i, 0)),
            pl.BlockSpec((1, d, TM_PROJ), lambda bi, i: (bi, 0, i)),
            pl.BlockSpec((1, d, TM_PROJ), lambda bi, i: (bi, 0, i)),
            pl.BlockSpec((W_T_COLS_B, d), lambda bi, i: (blk(bi, i), 0)),
            pl.BlockSpec((d, PAIR_W), lambda bi, i: (0, 0)),
        ],
        out_shape=[feat, jax.ShapeDtypeStruct((b, N_PAIRS, s, PAIR_W), jnp.bfloat16), feat, feat,
                   jax.ShapeDtypeStruct((n_blocks * W_T_COLS_B, d), jnp.bfloat16),
                   jax.ShapeDtypeStruct((d, PAIR_W), jnp.bfloat16)],
        compiler_params=pltpu.CompilerParams(
            dimension_semantics=("arbitrary", "arbitrary"), vmem_limit_bytes=VMEM_LIMIT),
        name="proj_a",
    )(x, wt, kv_w, gains[0][:, None], b_w, cs_all, gains[:, None, :])


def _chunk_iota(shape, axis):
    return lax.shift_right_logical(lax.broadcasted_iota(jnp.int32, shape, axis), CHUNK.bit_length() - 1)


def _identity_bf16(n):
    return (lax.broadcasted_iota(jnp.int32, (n, n), 0) == lax.broadcasted_iota(jnp.int32, (n, n), 1)).astype(jnp.bfloat16)


def _toeplitz(vec_row, n_keys):
    return pltpu.roll(jnp.broadcast_to(vec_row, (n_keys, BIAS_PERIOD)), 0, 1, stride=1, stride_axis=0)


def _fill_bias_slabs(vec_ref, bias_ref, n_keys, bands):
    n = len(bands)
    kch = _chunk_iota((n_keys, TQ), 0)
    qch = _chunk_iota((n_keys, TQ), 1)

    def head_body(h, carry):
        for j, band in enumerate(bands):
            t = _toeplitz(vec_ref[j, h], n_keys)[:, :TQ]
            if band is not None:
                t = jnp.where(band(kch, qch), t, MASKED)
            bias_ref[h * n + j] = t
        return carry

    lax.fori_loop(0, N_HEADS, head_body, 0)


def _live_rows(band, n_keys):
    chunks_per_half = LANES // CHUNK
    out = []
    for half in range(TQ // LANES):
        qchs = range(half * chunks_per_half, (half + 1) * chunks_per_half)
        live = [kc for kc in range(n_keys // CHUNK) if band is None or any(band(kc, qc) for qc in qchs)]
        out.append((live[0] * CHUNK, (live[-1] + 1) * CHUNK) if live else (0, 0))
    return tuple(out)


def _scores_stage(k_tiles, q_tiles, bias_fns, live, s_ref):
    maxes, row = [], 0
    for half in range(TQ // LANES):
        maxes.append([None] * MAX_CHAINS)
    for kt, qz, bias_fn, live_j in zip(k_tiles, q_tiles, bias_fns, live):
        s = jnp.dot(kt, qz, preferred_element_type=jnp.float32)
        for r in range(s.shape[0] // SUBLANES):
            rows = slice(r * SUBLANES, (r + 1) * SUBLANES)
            for half, (r0, r1) in enumerate(live_j):
                if not r0 <= r * SUBLANES < r1:
                    continue
                lanes = slice(half * LANES, (half + 1) * LANES)
                acc = maxes[half]
                grp = s[rows, lanes] if bias_fn is None else s[rows, lanes] + bias_fn(rows, lanes)
                s_ref[row + r * SUBLANES:row + (r + 1) * SUBLANES, lanes] = grp
                c = r % MAX_CHAINS
                acc[c] = grp if acc[c] is None else jnp.maximum(acc[c], grp)
        row += s.shape[0]
    cols = [jnp.max(functools.reduce(jnp.maximum, [a for a in acc if a is not None]), axis=0, keepdims=True)
            for acc in maxes]
    return jnp.concatenate(cols, axis=1)


def _pv_stage(s_ref, m, v_tiles, live, extra_logit=None):
    if extra_logit is not None:
        m = jnp.maximum(m, extra_logit)
    acc, row = None, 0
    for vt, live_j in zip(v_tiles, live):
        n = vt.shape[1]
        halves = []
        for half, (r0, r1) in enumerate(live_j):
            lanes = slice(half * LANES, (half + 1) * LANES)
            parts = [jnp.zeros((r0, LANES), jnp.bfloat16)] if r0 else []
            if r1 > r0:
                parts.append(jnp.exp2(s_ref[row + r0:row + r1, lanes] - m[:, lanes]).astype(jnp.bfloat16))
            if n > r1:
                parts.append(jnp.zeros((n - r1, LANES), jnp.bfloat16))
            halves.append(parts[0] if len(parts) == 1 else jnp.concatenate(parts, axis=0))
        p = jnp.concatenate(halves, axis=1)
        v_ones = jnp.concatenate([vt, jnp.ones((BF16_ROWS, n), vt.dtype)], axis=0)
        part = jnp.dot(v_ones, p, preferred_element_type=jnp.float32)
        acc = part if acc is None else acc + part
        row += n
    l = acc[HEAD_DIM:HEAD_DIM + 1, :]
    if extra_logit is not None:
        l = l + jnp.exp2(extra_logit - m)
    return acc[:HEAD_DIM, :] / l


def _head_pipeline(n_tiles, scores_fn, pv_fn, s_refs, scores_first, per_pair=2):
    per_group = len(s_refs) // 2
    x_refs, y_refs = s_refs[:per_group], s_refs[per_group:]
    n_groups = n_tiles // per_group

    m = [scores_fn(t // per_pair, t % per_pair, ref) for t, ref in zip(range(per_group), x_refs)]
    for g in range(n_groups):
        src, dst = (x_refs, y_refs) if g % 2 == 0 else (y_refs, x_refs)
        nxt = [(g + 1) * per_group + j for j in range(per_group)] if g + 1 < n_groups else []
        m_next = [scores_fn(t // per_pair, t % per_pair, ref) for t, ref in zip(nxt, dst)] if scores_first else []
        for j in range(per_group):
            if not scores_first and nxt:
                m_next.append(scores_fn(nxt[j] // per_pair, nxt[j] % per_pair, dst[j]))
            t = g * per_group + j
            pv_fn(t // per_pair, t % per_pair, src[j], m[j])
        m = m_next


A_BANDS = (lambda kc, qc: kc >= qc,
           None,
           lambda kc, qc: kc <= qc)
A_LIVE = tuple(_live_rows(band, TQ) for band in A_BANDS)
B_WIN = B_PREV + LANES
B_SUB = PAIR_W
B_BAND = lambda kc, qc: (kc >= qc) & (kc <= qc + B_LEFT_CHUNKS)
B_LIVE = (((0, B_WIN),) * (TQ // LANES),)


def _out_and_residual(zt_ref, wout_ref, res):
    y = lax.dot_general(zt_ref[...], wout_ref[...], _TN, preferred_element_type=jnp.float32)
    return res + y


def _attn_a_kernel(x_ref, qt_ref, kp_ref, ko_ref, vp_ref, vo_ref, gt_ref,
                   bvec_ref, wout_ref, kvwk_ref, nwt_ref,
                   h_ref, ksh_ref, vsht_ref, qbt_ref, gbt_ref, zt_ref, bias_ref, *s_refs):
    i = pl.program_id(1)
    subs = TQA // TQ

    @pl.when((pl.program_id(0) == 0) & (i == 0))
    def _():
        _fill_bias_slabs(bvec_ref, bias_ref, TQ, A_BANDS)
        bias_ref[N_HEADS * A_KBLOCKS] = jnp.full((TQ, TQ), MASKED, jnp.float32)

    n_slabs = N_HEADS * A_KBLOCKS
    rows = lax.broadcasted_iota(jnp.int32, (PAIR_W, TQ), 0)

    def key_block(sb, j):
        n = sb + j
        return n // subs, (n % subs) * TQ

    def scores_fn(sp, hh, s_ref):
        sb, p = divmod(sp, N_PAIRS)
        q2 = qt_ref[0, p, :, sb * TQ:(sb + 1) * TQ]
        own_rows = rows < HEAD_DIM if hh == 0 else rows >= HEAD_DIM
        qz = jnp.where(own_rows, q2, jnp.zeros_like(q2))
        k_tiles, bias_fns = [], []
        for j in range(A_KBLOCKS):
            in_seq = i * subs + sb - (A_KBLOCKS - 1) + j >= 0
            slab = jnp.where(in_seq, (2 * p + hh) * A_KBLOCKS + j, n_slabs)
            own, t0 = key_block(sb, j)
            k_tiles.append((ko_ref if own else kp_ref)[0, p, t0:t0 + TQ, :])
            bias_fns.append(lambda rws, lanes, slab=slab: bias_ref[slab, rws, lanes])
        return _scores_stage(k_tiles, [qz] * A_KBLOCKS, bias_fns, A_LIVE, s_ref)

    def pv_fn(sp, hh, s_ref, m):
        sb, p = divmod(sp, N_PAIRS)
        half = slice(hh * HEAD_DIM, (hh + 1) * HEAD_DIM)
        qcols = slice(sb * TQ, (sb + 1) * TQ)
        v_tiles = []
        for j in range(A_KBLOCKS):
            own, t0 = key_block(sb, j)
            v_tiles.append((vo_ref if own else vp_ref)[0, p, half, t0:t0 + TQ])
        o = _pv_stage(s_ref, m, v_tiles, A_LIVE)
        gate = gt_ref[0, p, half, qcols].astype(jnp.float32)
        row0 = p * PAIR_W + hh * HEAD_DIM
        zt_ref[row0:row0 + HEAD_DIM, qcols] = (o * gate).astype(jnp.bfloat16)

    _head_pipeline(subs * N_HEADS, scores_fn, pv_fn, s_refs, scores_first=True)

    h = _out_and_residual(zt_ref, wout_ref, x_ref[0])
    h_ref[0] = h
    hb = h.astype(jnp.bfloat16)
    r_col, r_row = _token_scales(h)
    ksh_ref[0] = (jnp.dot(hb, kvwk_ref[...], preferred_element_type=jnp.float32) * r_col).astype(jnp.bfloat16)
    for out_ref, base, act in ((gbt_ref, W_T_COLS_B + D_MODEL, _silu), (qbt_ref, W_T_COLS_B, None)):
        for c in range(0, D_MODEL, FEAT_CHUNK):
            first = out_ref is qbt_ref and c == 0
            start = base + c - (PAIR_W if first else 0)
            yt = lax.dot_general(nwt_ref[start:base + c + FEAT_CHUNK, :], hb, _NT,
                                 preferred_element_type=jnp.float32) * r_row
            if first:
                vsht_ref[0] = yt[:PAIR_W].astype(jnp.bfloat16)
                yt = yt[PAIR_W:]
            out_ref[0, c:c + FEAT_CHUNK, :] = (yt if act is None else act(yt)).astype(jnp.bfloat16)


def _const_spec(shape):
    return pl.BlockSpec(shape, lambda bi, i: (0,) * len(shape), pipeline_mode=pl.Buffered(1))


def _attn_a(x, qt, k, vt, gt, bvec, wout, kvwk, nwt):
    b, s, d = x.shape
    qt4 = qt.reshape(b, N_PAIRS, PAIR_W, s)
    vt4 = vt.reshape(b, N_PAIRS, PAIR_W, s)
    gt4 = gt.reshape(b, N_PAIRS, PAIR_W, s)
    assert (A_KBLOCKS - 1) * TQ <= TQA, "the keys before a grid block must fit in one previous block"
    prev = lambda i: jnp.maximum(i - 1, 0)

    def kspec(index):
        return pl.BlockSpec((1, N_PAIRS, TQA, PAIR_W), lambda bi, i: (bi, 0, index(i), 0))

    def vspec(index):
        return pl.BlockSpec((1, N_PAIRS, PAIR_W, TQA), lambda bi, i: (bi, 0, 0, index(i)))

    feat_spec = pl.BlockSpec((1, N_PAIRS, PAIR_W, TQA), lambda bi, i: (bi, 0, 0, i))
    featout_spec = pl.BlockSpec((1, d, TQA), lambda bi, i: (bi, 0, i))
    feat = jax.ShapeDtypeStruct((b, d, s), jnp.bfloat16)
    return pl.pallas_call(
        _attn_a_kernel,
        grid=(b, s // TQA),
        in_specs=[
            pl.BlockSpec((1, TQA, d), lambda bi, i: (bi, i, 0)),
            feat_spec,
            kspec(prev), kspec(lambda i: i),
            vspec(prev), vspec(lambda i: i),
            feat_spec,
            _const_spec(bvec.shape),
            _const_spec(wout.shape),
            _const_spec(kvwk.shape),
            _const_spec(nwt.shape),
        ],
        out_specs=[
            pl.BlockSpec((1, TQA, d), lambda bi, i: (bi, i, 0)),
            pl.BlockSpec((1, TQA, PAIR_W), lambda bi, i: (bi, i, 0)),
            pl.BlockSpec((1, PAIR_W, TQA), lambda bi, i: (bi, 0, i)),
            featout_spec,
            featout_spec,
        ],
        out_shape=[
            jax.ShapeDtypeStruct((b, s, d), jnp.float32),
            jax.ShapeDtypeStruct((b, s, PAIR_W), jnp.bfloat16),
            jax.ShapeDtypeStruct((b, PAIR_W, s), jnp.bfloat16),
            feat, feat,
        ],
        scratch_shapes=[pltpu.VMEM((d, TQA), jnp.bfloat16),
                        pltpu.VMEM((N_HEADS * A_KBLOCKS + 1, TQ, TQ), jnp.float32),
                        *[pltpu.VMEM((A_KBLOCKS * TQ, TQ), jnp.float32)] * N_SCORE_BUFS],
        compiler_params=pltpu.CompilerParams(
            dimension_semantics=("arbitrary", "arbitrary"), vmem_limit_bytes=VMEM_LIMIT),
        name="attn_a",
    )(x, qt4, k, k, vt4, vt4, gt4, bvec, wout, kvwk, nwt)


def _attn_b_kernel(h_ref, qt_ref, kp_ref, ko_ref, vp_ref, vo_ref, gt_ref, vec_ref,
                   sink_ref, wout_ref, fg_ref, out_ref, zt_ref, bias_ref, *s_refs):
    i = pl.program_id(1)

    @pl.when((pl.program_id(0) == 0) & (i == 0))
    def _():
        in_band = B_BAND(_chunk_iota((B_WIN, LANES), 0), _chunk_iota((B_WIN, LANES), 1))
        before_block = lax.broadcasted_iota(jnp.int32, (B_WIN, LANES), 0) < B_PREV

        def pair_body(p, carry):
            for hh in range(2):
                t = jnp.where(in_band, _toeplitz(vec_ref[0, 2 * p + hh], B_WIN)[:, :LANES], MASKED)
                lanes = slice(hh * LANES, (hh + 1) * LANES)
                bias_ref[2 * p, :, lanes] = t.astype(jnp.bfloat16)
                bias_ref[2 * p + 1, :, lanes] = jnp.where(before_block, MASKED, t).astype(jnp.bfloat16)
            return carry

        lax.fori_loop(0, N_PAIRS, pair_body, 0)

    rows = lax.broadcasted_iota(jnp.int32, (PAIR_W, TQ), 0)
    pairs_per_kv = N_PAIRS // B_KV_HEADS
    eye = _identity_bf16(B_SUB)

    def scores_fn(p, u, s_ref):
        qlanes = slice(u * LANES, (u + 1) * LANES)
        q2 = jnp.concatenate([qt_ref[0, p, :HEAD_DIM, qlanes], qt_ref[0, p, HEAD_DIM:, qlanes]], axis=1)
        kv0 = HEAD_DIM * (p // pairs_per_kv)
        kv_rows = (rows >= kv0) & (rows < kv0 + HEAD_DIM)
        qz = jnp.where(kv_rows, jnp.concatenate([q2, q2], axis=0), jnp.zeros((PAIR_W, TQ), q2.dtype))
        if u == 0:
            k_tile = jnp.concatenate([kp_ref[0], ko_ref[0, :LANES, :]], axis=0)
            variant = jnp.where(i == 0, 1, 0)
        else:
            k_tile, variant = ko_ref[0, u * LANES - B_PREV:(u + 1) * LANES, :], 0

        lhs, rhs = [], []
        for r0 in range(0, B_WIN, B_SUB):
            lhs.append(jnp.concatenate([k_tile[r0:r0 + B_SUB, :], eye], axis=1))
            rhs.append(jnp.concatenate([qz, bias_ref[2 * p + variant, r0:r0 + B_SUB, :]], axis=0))
        sub_live = (((0, B_SUB),) * (TQ // LANES),) * len(lhs)
        return _scores_stage(lhs, rhs, [None] * len(lhs), sub_live, s_ref)

    def pv_fn(p, u, s_ref, m):
        qlanes = slice(u * LANES, (u + 1) * LANES)
        kv_rows = slice(HEAD_DIM * (p // pairs_per_kv), HEAD_DIM * (p // pairs_per_kv + 1))
        if u == 0:
            v_tile = jnp.concatenate([vp_ref[0, kv_rows, :], vo_ref[0, kv_rows, :LANES]], axis=1)
        else:
            v_tile = vo_ref[0, kv_rows, u * LANES - B_PREV:(u + 1) * LANES]
        o = _pv_stage(s_ref, m, [v_tile], B_LIVE, extra_logit=sink_ref[p])
        for hh in range(2):
            hrows = slice(hh * HEAD_DIM, (hh + 1) * HEAD_DIM)
            gate = gt_ref[0, p, hrows, qlanes].astype(jnp.float32)
            row0 = p * PAIR_W + hh * HEAD_DIM
            zt_ref[row0:row0 + HEAD_DIM, qlanes] = (o[:, hh * LANES:(hh + 1) * LANES] * gate).astype(jnp.bfloat16)

    _head_pipeline(N_PAIRS * (TQB // LANES), scores_fn, pv_fn, s_refs, scores_first=False, per_pair=TQB // LANES)

    h2 = _out_and_residual(zt_ref, wout_ref, h_ref[0])
    out_ref[0] = (h2 * fg_ref[...]) * _rms_scale(h2)


def _attn_b(h, qbt, ksh, vsht, gbt, vec, sinks, wout, fg):
    b, s, d = h.shape
    qt4 = qbt.reshape(b, N_PAIRS, PAIR_W, s)
    gt4 = gbt.reshape(b, N_PAIRS, PAIR_W, s)
    feat_spec = pl.BlockSpec((1, N_PAIRS, PAIR_W, TQB), lambda bi, i: (bi, 0, 0, i))
    prev = lambda i: jnp.maximum(i * (TQB // B_PREV) - 1, 0)
    return pl.pallas_call(
        _attn_b_kernel,
        grid=(b, s // TQB),
        in_specs=[
            pl.BlockSpec((1, TQB, d), lambda bi, i: (bi, i, 0)),
            feat_spec,
            pl.BlockSpec((1, B_PREV, PAIR_W), lambda bi, i: (bi, prev(i), 0)),
            pl.BlockSpec((1, TQB, PAIR_W), lambda bi, i: (bi, i, 0)),
            pl.BlockSpec((1, PAIR_W, B_PREV), lambda bi, i: (bi, 0, prev(i))),
            pl.BlockSpec((1, PAIR_W, TQB), lambda bi, i: (bi, 0, i)),
            feat_spec,
            _const_spec(vec.shape),
            _const_spec(sinks.shape),
            _const_spec(wout.shape),
            _const_spec(fg.shape),
        ],
        out_specs=pl.BlockSpec((1, TQB, d), lambda bi, i: (bi, i, 0)),
        out_shape=jax.ShapeDtypeStruct((b, s, d), jnp.float32),
        scratch_shapes=[pltpu.VMEM((d, TQB), jnp.bfloat16),
                        pltpu.VMEM((2 * N_PAIRS, B_WIN, TQ), jnp.bfloat16),
                        *[pltpu.VMEM((B_WIN, TQ), jnp.float32)] * N_SCORE_BUFS_B],
        compiler_params=pltpu.CompilerParams(
            dimension_semantics=("arbitrary", "arbitrary"), vmem_limit_bytes=VMEM_LIMIT),
        name="attn_b",
    )(h, qt4, ksh, ksh, vsht, vsht, gt4, vec, sinks, wout, fg)


def _slab_diff():
    u = jnp.arange(BIAS_PERIOD)
    return jnp.where(u < TQ, u, u - BIAS_PERIOD)


def _bias_vecs_a(rel_bias):
    back = (A_KBLOCKS - 1 - jnp.arange(A_KBLOCKS))[:, None] * TQ
    dist = back + _slab_diff()[None, :]
    idx = jnp.clip(dist, -A_REL_CLIP, A_REL_CLIP) + A_REL_CLIP
    vec = jnp.transpose(rel_bias[idx], (0, 2, 1)).astype(jnp.float32)
    return vec[:, :, None, :]


def _t5_bucket(rel):
    nb = T5_BUCKETS // 2
    max_exact = nb // 2
    ret = jnp.where(rel > 0, nb, 0)
    n = jnp.abs(rel)
    nf = jnp.maximum(n, 1).astype(jnp.float32)
    large = max_exact + (jnp.log(nf / max_exact) / math.log(T5_MAX_DIST / max_exact)
                         * (nb - max_exact)).astype(jnp.int32)
    large = jnp.minimum(large, nb - 1)
    return ret + jnp.where(n < max_exact, n, large)


def _bias_vec_b(t5_table):
    rel = -_slab_diff() - B_PREV
    vec = jnp.transpose(t5_table[_t5_bucket(rel)], (1, 0)).astype(jnp.float32)
    return vec[None, :, None, :]


def kernel(x, a_norm, a_w_in, a_rel_bias, a_w_out, kv_norm, kv_w, t5_bias,
           b_norm, b_w_in, b_sinks, b_w_out, final_norm):
    assert a_norm.shape[0] == 1 and b_norm.shape[0] == 1, "one A layer then one B layer"
    bf = jnp.bfloat16
    scale = HEAD_DIM ** -0.5 * LOG2E

    def q_col_scale(n):
        return jnp.where(jnp.arange(n) < D_MODEL, scale, 1.0).astype(jnp.float32)

    wt_a = _weight_t(a_w_in[0], q_col_scale(a_w_in.shape[2]), a_norm[0], W_T_COLS_A)
    qt, k, vt, gt, nwt, kvw_k = _proj_a(x, wt_a, kv_w, b_w_in[0], q_col_scale(b_w_in.shape[2]),
                                        jnp.stack([kv_norm, b_norm[0]]))
    h, ksh, vsht, qbt, gbt = _attn_a(
        x, qt, k, vt, gt, _bias_vecs_a(a_rel_bias[0] * LOG2E), a_w_out[0].astype(bf), kvw_k, nwt)

    sinks = jnp.repeat((b_sinks[0].astype(jnp.float32) * LOG2E).reshape(N_PAIRS, 1, 2), LANES, axis=2)
    return _attn_b(h, qbt, ksh, vsht, gbt, _bias_vec_b(t5_bias * LOG2E), sinks, b_w_out[0].astype(bf),
                   final_norm[None, :])
```

```python
import functools
import math

import jax
import jax.numpy as jnp
from jax import lax
from jax.experimental import pallas as pl
from jax.experimental.pallas import tpu as pltpu

D_MODEL = 1024
HEAD_DIM = 64
N_HEADS = D_MODEL // HEAD_DIM
N_PAIRS = N_HEADS // 2
PAIR_W = 2 * HEAD_DIM
CHUNK = 64
RMS_EPS = 1e-6
A_LEFT_CHUNKS = 8
A_REL_CLIP = 256
B_KV_HEADS = 2
B_LEFT_CHUNKS = 2
T5_BUCKETS = 32
T5_MAX_DIST = 128

TQ = 256
TQB = 512
TQA = 512
A_KBLOCKS = A_LEFT_CHUNKS * CHUNK // TQ + 1
B_PREV = B_LEFT_CHUNKS * CHUNK
TM_PROJ = 1024
FEAT_CHUNK = 256
W_T_COLS_A = 1024
W_T_COLS_B = 512
MASKED = -1e30
BIAS_PERIOD = 2 * TQ
SUBLANES = 8
LANES = 128
MAX_CHAINS = 1
BF16_ROWS = 16
LOG2E = math.log2(math.e)
N_SCORE_BUFS = 4
N_SCORE_BUFS_B = 8
VMEM_LIMIT = 56 * 1024 * 1024

_NT = (((1,), (1,)), ((), ()))
_TN = (((0,), (0,)), ((), ()))


def _rms_scale(xf):
    return lax.rsqrt(jnp.mean(xf * xf, axis=-1, keepdims=True) + RMS_EPS)


def _token_scales(xf):
    col = _rms_scale(xf)
    row = jnp.transpose(jnp.broadcast_to(col, (xf.shape[0], LANES)))[0:1, :]
    return col, row


def _silu(v):
    return v * jax.nn.sigmoid(v)


def _weight_t_kernel(w_ref, cs_ref, gain_ref, o_ref):
    o_ref[...] = ((w_ref[...] * cs_ref[...]).T * gain_ref[...]).astype(jnp.bfloat16)


def _weight_t(w, col_scale, row_gain, tc):
    d, n = w.shape
    return pl.pallas_call(
        _weight_t_kernel,
        grid=(n // tc,),
        in_specs=[
            pl.BlockSpec((d, tc), lambda j: (0, j)),
            pl.BlockSpec((1, tc), lambda j: (0, j)),
            pl.BlockSpec((1, d), lambda j: (0, 0)),
        ],
        out_specs=pl.BlockSpec((tc, d), lambda j: (j, 0)),
        out_shape=jax.ShapeDtypeStruct((n, d), jnp.bfloat16),
        compiler_params=pltpu.CompilerParams(dimension_semantics=("arbitrary",), vmem_limit_bytes=VMEM_LIMIT),
        name="weight_t",
    )(w, col_scale[None, :], row_gain[None, :])


def _proj_a_kernel(x_ref, wt_ref, kvw_ref, kvg_ref, bw_ref, cs_ref, gain_ref, woa_ref, wob_ref,
                   qt_ref, k_ref, vt_ref, gt_ref, nwt_ref, kvwk_ref, woa16_ref, wob16_ref):
    step = pl.program_id(0) * pl.num_programs(1) + pl.program_id(1)
    head = jnp.concatenate([jnp.zeros((D_MODEL, W_T_COLS_B - PAIR_W), jnp.float32), kvw_ref[:, PAIR_W:]], axis=1)
    src = jnp.where(step == 0, head, bw_ref[...])
    nwt_ref[...] = ((src * cs_ref[...]).T * gain_ref[0]).astype(jnp.bfloat16)
    kvwk_ref[...] = (kvw_ref[:, :PAIR_W] * kvg_ref[...]).astype(jnp.bfloat16)
    woa16_ref[...] = woa_ref[0].astype(jnp.bfloat16)
    wob16_ref[...] = wob_ref[0].astype(jnp.bfloat16)

    xf = x_ref[0]
    xb = xf.astype(jnp.bfloat16)
    r_col, r_row = _token_scales(xf)
    k = lax.dot_general(xb, wt_ref[D_MODEL:2 * D_MODEL, :], _NT,
                        preferred_element_type=jnp.float32) * r_col
    for p in range(N_PAIRS):
        k_ref[0, p] = k[:, p * PAIR_W:(p + 1) * PAIR_W].astype(jnp.bfloat16)
    for out_ref, base, act in ((gt_ref, 3 * D_MODEL, _silu), (vt_ref, 2 * D_MODEL, None), (qt_ref, 0, None)):
        for c in range(0, D_MODEL, FEAT_CHUNK):
            w = wt_ref[base + c:base + c + FEAT_CHUNK, :]
            yt = lax.dot_general(w, xb, _NT, preferred_element_type=jnp.float32) * r_row
            out_ref[0, c:c + FEAT_CHUNK, :] = (yt if act is None else act(yt)).astype(jnp.bfloat16)


def _proj_a(x, wt, kv_w, b_w, b_col_scale, gains, a_w_out, b_w_out):
    b, s, d = x.shape
    n_i = s // TM_PROJ
    wo_rows = d // (b * n_i)
    step = lambda bi, i: bi * n_i + i
    n_blocks = 1 + b_w.shape[1] // W_T_COLS_B
    assert n_blocks <= b * n_i and kv_w.shape == (d, 2 * PAIR_W) and d % (b * n_i * BF16_ROWS) == 0
    blk = lambda bi, i: jnp.minimum(bi * n_i + i, n_blocks - 1)
    cs_all = jnp.concatenate([jnp.ones((W_T_COLS_B,), jnp.float32), b_col_scale])[None, :]
    feat = jax.ShapeDtypeStruct((b, d, s), jnp.bfloat16)
    return pl.pallas_call(
        _proj_a_kernel,
        grid=(b, n_i),
        in_specs=[
            pl.BlockSpec((1, TM_PROJ, d), lambda bi, i: (bi, i, 0)),
            _const_spec(wt.shape),
            _const_spec(kv_w.shape),
            _const_spec((d, 1)),
            pl.BlockSpec((d, W_T_COLS_B), lambda bi, i: (0, jnp.maximum(blk(bi, i) - 1, 0))),
            pl.BlockSpec((1, W_T_COLS_B), lambda bi, i: (0, blk(bi, i))),
            pl.BlockSpec((1, 1, d), lambda bi, i: (jnp.minimum(blk(bi, i), 1), 0, 0)),
            pl.BlockSpec((1, wo_rows, d), lambda bi, i: (0, step(bi, i), 0)),
            pl.BlockSpec((1, wo_rows, d), lambda bi, i: (0, step(bi, i), 0)),
        ],
        out_specs=[
            pl.BlockSpec((1, d, TM_PROJ), lambda bi, i: (bi, 0, i)),
            pl.BlockSpec((1, N_PAIRS, TM_PROJ, PAIR_W), lambda bi, i: (bi, 0, i, 0)),
            pl.BlockSpec((1, d, TM_PROJ), lambda bi, i: (bi, 0, i)),
            pl.BlockSpec((1, d, TM_PROJ), lambda bi, i: (bi, 0, i)),
            pl.BlockSpec((W_T_COLS_B, d), lambda bi, i: (blk(bi, i), 0)),
            pl.BlockSpec((d, PAIR_W), lambda bi, i: (0, 0)),
            pl.BlockSpec((wo_rows, d), lambda bi, i: (step(bi, i), 0)),
            pl.BlockSpec((wo_rows, d), lambda bi, i: (step(bi, i), 0)),
        ],
        out_shape=[feat, jax.ShapeDtypeStruct((b, N_PAIRS, s, PAIR_W), jnp.bfloat16), feat, feat,
                   jax.ShapeDtypeStruct((n_blocks * W_T_COLS_B, d), jnp.bfloat16),
                   jax.ShapeDtypeStruct((d, PAIR_W), jnp.bfloat16),
                   jax.ShapeDtypeStruct((d, d), jnp.bfloat16), jax.ShapeDtypeStruct((d, d), jnp.bfloat16)],
        compiler_params=pltpu.CompilerParams(
            dimension_semantics=("arbitrary", "arbitrary"), vmem_limit_bytes=VMEM_LIMIT),
        name="proj_a",
    )(x, wt, kv_w, gains[0][:, None], b_w, cs_all, gains[:, None, :], a_w_out, b_w_out)


def _chunk_iota(shape, axis):
    return lax.shift_right_logical(lax.broadcasted_iota(jnp.int32, shape, axis), CHUNK.bit_length() - 1)


def _identity_bf16(n):
    return (lax.broadcasted_iota(jnp.int32, (n, n), 0) == lax.broadcasted_iota(jnp.int32, (n, n), 1)).astype(jnp.bfloat16)


def _toeplitz(vec_row, n_keys):
    return pltpu.roll(jnp.broadcast_to(vec_row, (n_keys, BIAS_PERIOD)), 0, 1, stride=1, stride_axis=0)


def _fill_bias_slabs(vec_ref, bias_ref, n_keys, bands):
    n = len(bands)
    kch = _chunk_iota((n_keys, TQ), 0)
    qch = _chunk_iota((n_keys, TQ), 1)

    def head_body(h, carry):
        for j, band in enumerate(bands):
            t = _toeplitz(vec_ref[j, h], n_keys)[:, :TQ]
            if band is not None:
                t = jnp.where(band(kch, qch), t, MASKED)
            bias_ref[h * n + j] = t
        return carry

    lax.fori_loop(0, N_HEADS, head_body, 0)


def _live_rows(band, n_keys):
    chunks_per_half = LANES // CHUNK
    out = []
    for half in range(TQ // LANES):
        qchs = range(half * chunks_per_half, (half + 1) * chunks_per_half)
        live = [kc for kc in range(n_keys // CHUNK) if band is None or any(band(kc, qc) for qc in qchs)]
        out.append((live[0] * CHUNK, (live[-1] + 1) * CHUNK) if live else (0, 0))
    return tuple(out)


def _scores_stage(k_tiles, q_tiles, bias_fns, live, s_ref):
    maxes, row = [], 0
    for half in range(TQ // LANES):
        maxes.append([None] * MAX_CHAINS)
    for kt, qz, bias_fn, live_j in zip(k_tiles, q_tiles, bias_fns, live):
        s = jnp.dot(kt, qz, preferred_element_type=jnp.float32)
        for r in range(s.shape[0] // SUBLANES):
            rows = slice(r * SUBLANES, (r + 1) * SUBLANES)
            for half, (r0, r1) in enumerate(live_j):
                if not r0 <= r * SUBLANES < r1:
                    continue
                lanes = slice(half * LANES, (half + 1) * LANES)
                acc = maxes[half]
                grp = s[rows, lanes] if bias_fn is None else s[rows, lanes] + bias_fn(rows, lanes)
                s_ref[row + r * SUBLANES:row + (r + 1) * SUBLANES, lanes] = grp
                c = r % MAX_CHAINS
                acc[c] = grp if acc[c] is None else jnp.maximum(acc[c], grp)
        row += s.shape[0]
    cols = [jnp.max(functools.reduce(jnp.maximum, [a for a in acc if a is not None]), axis=0, keepdims=True)
            for acc in maxes]
    return jnp.concatenate(cols, axis=1)


def _pv_stage(s_ref, m, v_tiles, live, extra_logit=None):
    if extra_logit is not None:
        m = jnp.maximum(m, extra_logit)
    acc, row = None, 0
    for vt, live_j in zip(v_tiles, live):
        n = vt.shape[1]
        halves = []
        for half, (r0, r1) in enumerate(live_j):
            lanes = slice(half * LANES, (half + 1) * LANES)
            parts = [jnp.zeros((r0, LANES), jnp.bfloat16)] if r0 else []
            if r1 > r0:
                parts.append(jnp.exp2(s_ref[row + r0:row + r1, lanes] - m[:, lanes]).astype(jnp.bfloat16))
            if n > r1:
                parts.append(jnp.zeros((n - r1, LANES), jnp.bfloat16))
            halves.append(parts[0] if len(parts) == 1 else jnp.concatenate(parts, axis=0))
        p = jnp.concatenate(halves, axis=1)
        v_ones = jnp.concatenate([vt, jnp.ones((BF16_ROWS, n), vt.dtype)], axis=0)
        part = jnp.dot(v_ones, p, preferred_element_type=jnp.float32)
        acc = part if acc is None else acc + part
        row += n
    l = acc[HEAD_DIM:HEAD_DIM + 1, :]
    if extra_logit is not None:
        l = l + jnp.exp2(extra_logit - m)
    return acc[:HEAD_DIM, :] / l


def _head_pipeline(n_tiles, scores_fn, pv_fn, s_refs, scores_first, per_pair=2):
    per_group = len(s_refs) // 2
    x_refs, y_refs = s_refs[:per_group], s_refs[per_group:]
    n_groups = n_tiles // per_group

    m = [scores_fn(t // per_pair, t % per_pair, ref) for t, ref in zip(range(per_group), x_refs)]
    for g in range(n_groups):
        src, dst = (x_refs, y_refs) if g % 2 == 0 else (y_refs, x_refs)
        nxt = [(g + 1) * per_group + j for j in range(per_group)] if g + 1 < n_groups else []
        m_next = [scores_fn(t // per_pair, t % per_pair, ref) for t, ref in zip(nxt, dst)] if scores_first else []
        for j in range(per_group):
            if not scores_first and nxt:
                m_next.append(scores_fn(nxt[j] // per_pair, nxt[j] % per_pair, dst[j]))
            t = g * per_group + j
            pv_fn(t // per_pair, t % per_pair, src[j], m[j])
        m = m_next


A_BANDS = (lambda kc, qc: kc >= qc,
           None,
           lambda kc, qc: kc <= qc)
A_LIVE = tuple(_live_rows(band, TQ) for band in A_BANDS)
B_WIN = B_PREV + LANES
B_SUB = PAIR_W
B_BAND = lambda kc, qc: (kc >= qc) & (kc <= qc + B_LEFT_CHUNKS)
B_LIVE = (((0, B_WIN),) * (TQ // LANES),)


def _out_and_residual(zt_ref, wout_ref, res):
    y = lax.dot_general(zt_ref[...], wout_ref[...], _TN, preferred_element_type=jnp.float32)
    return res + y


def _attn_a_kernel(x_ref, qt_ref, kp_ref, ko_ref, vp_ref, vo_ref, gt_ref,
                   bvec_ref, wout_ref, kvwk_ref, nwt_ref,
                   h_ref, ksh_ref, vsht_ref, qbt_ref, gbt_ref, zt_ref, bias_ref, *s_refs):
    i = pl.program_id(1)
    subs = TQA // TQ

    @pl.when((pl.program_id(0) == 0) & (i == 0))
    def _():
        _fill_bias_slabs(bvec_ref, bias_ref, TQ, A_BANDS)
        bias_ref[N_HEADS * A_KBLOCKS] = jnp.full((TQ, TQ), MASKED, jnp.float32)

    n_slabs = N_HEADS * A_KBLOCKS
    rows = lax.broadcasted_iota(jnp.int32, (PAIR_W, TQ), 0)

    def key_block(sb, j):
        n = sb + j
        return n // subs, (n % subs) * TQ

    def scores_fn(sp, hh, s_ref):
        sb, p = divmod(sp, N_PAIRS)
        q2 = qt_ref[0, p, :, sb * TQ:(sb + 1) * TQ]
        own_rows = rows < HEAD_DIM if hh == 0 else rows >= HEAD_DIM
        qz = jnp.where(own_rows, q2, jnp.zeros_like(q2))
        k_tiles, bias_fns = [], []
        for j in range(A_KBLOCKS):
            in_seq = i * subs + sb - (A_KBLOCKS - 1) + j >= 0
            slab = jnp.where(in_seq, (2 * p + hh) * A_KBLOCKS + j, n_slabs)
            own, t0 = key_block(sb, j)
            k_tiles.append((ko_ref if own else kp_ref)[0, p, t0:t0 + TQ, :])
            bias_fns.append(lambda rws, lanes, slab=slab: bias_ref[slab, rws, lanes])
        return _scores_stage(k_tiles, [qz] * A_KBLOCKS, bias_fns, A_LIVE, s_ref)

    def pv_fn(sp, hh, s_ref, m):
        sb, p = divmod(sp, N_PAIRS)
        half = slice(hh * HEAD_DIM, (hh + 1) * HEAD_DIM)
        qcols = slice(sb * TQ, (sb + 1) * TQ)
        v_tiles = []
        for j in range(A_KBLOCKS):
            own, t0 = key_block(sb, j)
            v_tiles.append((vo_ref if own else vp_ref)[0, p, half, t0:t0 + TQ])
        o = _pv_stage(s_ref, m, v_tiles, A_LIVE)
        gate = gt_ref[0, p, half, qcols].astype(jnp.float32)
        row0 = p * PAIR_W + hh * HEAD_DIM
        zt_ref[row0:row0 + HEAD_DIM, qcols] = (o * gate).astype(jnp.bfloat16)

    _head_pipeline(subs * N_HEADS, scores_fn, pv_fn, s_refs, scores_first=True)

    h = _out_and_residual(zt_ref, wout_ref, x_ref[0])
    h_ref[0] = h
    hb = h.astype(jnp.bfloat16)
    r_col, r_row = _token_scales(h)
    ksh_ref[0] = (jnp.dot(hb, kvwk_ref[...], preferred_element_type=jnp.float32) * r_col).astype(jnp.bfloat16)
    for out_ref, base, act in ((gbt_ref, W_T_COLS_B + D_MODEL, _silu), (qbt_ref, W_T_COLS_B, None)):
        for c in range(0, D_MODEL, FEAT_CHUNK):
            first = out_ref is qbt_ref and c == 0
            start = base + c - (PAIR_W if first else 0)
            yt = lax.dot_general(nwt_ref[start:base + c + FEAT_CHUNK, :], hb, _NT,
                                 preferred_element_type=jnp.float32) * r_row
            if first:
                vsht_ref[0] = yt[:PAIR_W].astype(jnp.bfloat16)
                yt = yt[PAIR_W:]
            out_ref[0, c:c + FEAT_CHUNK, :] = (yt if act is None else act(yt)).astype(jnp.bfloat16)


def _const_spec(shape):
    return pl.BlockSpec(shape, lambda bi, i: (0,) * len(shape), pipeline_mode=pl.Buffered(1))


def _attn_a(x, qt, k, vt, gt, bvec, wout, kvwk, nwt):
    b, s, d = x.shape
    qt4 = qt.reshape(b, N_PAIRS, PAIR_W, s)
    vt4 = vt.reshape(b, N_PAIRS, PAIR_W, s)
    gt4 = gt.reshape(b, N_PAIRS, PAIR_W, s)
    assert (A_KBLOCKS - 1) * TQ <= TQA, "the keys before a grid block must fit in one previous block"
    prev = lambda i: jnp.maximum(i - 1, 0)

    def kspec(index):
        return pl.BlockSpec((1, N_PAIRS, TQA, PAIR_W), lambda bi, i: (bi, 0, index(i), 0))

    def vspec(index):
        return pl.BlockSpec((1, N_PAIRS, PAIR_W, TQA), lambda bi, i: (bi, 0, 0, index(i)))

    feat_spec = pl.BlockSpec((1, N_PAIRS, PAIR_W, TQA), lambda bi, i: (bi, 0, 0, i))
    featout_spec = pl.BlockSpec((1, d, TQA), lambda bi, i: (bi, 0, i))
    feat = jax.ShapeDtypeStruct((b, d, s), jnp.bfloat16)
    return pl.pallas_call(
        _attn_a_kernel,
        grid=(b, s // TQA),
        in_specs=[
            pl.BlockSpec((1, TQA, d), lambda bi, i: (bi, i, 0)),
            feat_spec,
            kspec(prev), kspec(lambda i: i),
            vspec(prev), vspec(lambda i: i),
            feat_spec,
            _const_spec(bvec.shape),
            _const_spec(wout.shape),
            _const_spec(kvwk.shape),
            _const_spec(nwt.shape),
        ],
        out_specs=[
            pl.BlockSpec((1, TQA, d), lambda bi, i: (bi, i, 0)),
            pl.BlockSpec((1, TQA, PAIR_W), lambda bi, i: (bi, i, 0)),
            pl.BlockSpec((1, PAIR_W, TQA), lambda bi, i: (bi, 0, i)),
            featout_spec,
            featout_spec,
        ],
        out_shape=[
            jax.ShapeDtypeStruct((b, s, d), jnp.float32),
            jax.ShapeDtypeStruct((b, s, PAIR_W), jnp.bfloat16),
            jax.ShapeDtypeStruct((b, PAIR_W, s), jnp.bfloat16),
            feat, feat,
        ],
        scratch_shapes=[pltpu.VMEM((d, TQA), jnp.bfloat16),
                        pltpu.VMEM((N_HEADS * A_KBLOCKS + 1, TQ, TQ), jnp.float32),
                        *[pltpu.VMEM((A_KBLOCKS * TQ, TQ), jnp.float32)] * N_SCORE_BUFS],
        compiler_params=pltpu.CompilerParams(
            dimension_semantics=("arbitrary", "arbitrary"), vmem_limit_bytes=VMEM_LIMIT),
        name="attn_a",
    )(x, qt4, k, k, vt4, vt4, gt4, bvec, wout, kvwk, nwt)


def _attn_b_kernel(h_ref, qt_ref, kp_ref, ko_ref, vp_ref, vo_ref, gt_ref, vec_ref,
                   sink_ref, wout_ref, fg_ref, out_ref, zt_ref, bias_ref, *s_refs):
    i = pl.program_id(1)

    @pl.when((pl.program_id(0) == 0) & (i == 0))
    def _():
        in_band = B_BAND(_chunk_iota((B_WIN, LANES), 0), _chunk_iota((B_WIN, LANES), 1))
        before_block = lax.broadcasted_iota(jnp.int32, (B_WIN, LANES), 0) < B_PREV

        def pair_body(p, carry):
            for hh in range(2):
                t = jnp.where(in_band, _toeplitz(vec_ref[0, 2 * p + hh], B_WIN)[:, :LANES], MASKED)
                lanes = slice(hh * LANES, (hh + 1) * LANES)
                bias_ref[2 * p, :, lanes] = t.astype(jnp.bfloat16)
                bias_ref[2 * p + 1, :, lanes] = jnp.where(before_block, MASKED, t).astype(jnp.bfloat16)
            return carry

        lax.fori_loop(0, N_PAIRS, pair_body, 0)

    rows = lax.broadcasted_iota(jnp.int32, (PAIR_W, TQ), 0)
    pairs_per_kv = N_PAIRS // B_KV_HEADS
    eye = _identity_bf16(B_SUB)

    def scores_fn(p, u, s_ref):
        qlanes = slice(u * LANES, (u + 1) * LANES)
        q2 = jnp.concatenate([qt_ref[0, p, :HEAD_DIM, qlanes], qt_ref[0, p, HEAD_DIM:, qlanes]], axis=1)
        kv0 = HEAD_DIM * (p // pairs_per_kv)
        kv_rows = (rows >= kv0) & (rows < kv0 + HEAD_DIM)
        qz = jnp.where(kv_rows, jnp.concatenate([q2, q2], axis=0), jnp.zeros((PAIR_W, TQ), q2.dtype))
        if u == 0:
            k_tile = jnp.concatenate([kp_ref[0], ko_ref[0, :LANES, :]], axis=0)
            variant = jnp.where(i == 0, 1, 0)
        else:
            k_tile, variant = ko_ref[0, u * LANES - B_PREV:(u + 1) * LANES, :], 0

        lhs, rhs = [], []
        for r0 in range(0, B_WIN, B_SUB):
            lhs.append(jnp.concatenate([k_tile[r0:r0 + B_SUB, :], eye], axis=1))
            rhs.append(jnp.concatenate([qz, bias_ref[2 * p + variant, r0:r0 + B_SUB, :]], axis=0))
        sub_live = (((0, B_SUB),) * (TQ // LANES),) * len(lhs)
        return _scores_stage(lhs, rhs, [None] * len(lhs), sub_live, s_ref)

    def pv_fn(p, u, s_ref, m):
        qlanes = slice(u * LANES, (u + 1) * LANES)
        kv_rows = slice(HEAD_DIM * (p // pairs_per_kv), HEAD_DIM * (p // pairs_per_kv + 1))
        if u == 0:
            v_tile = jnp.concatenate([vp_ref[0, kv_rows, :], vo_ref[0, kv_rows, :LANES]], axis=1)
        else:
            v_tile = vo_ref[0, kv_rows, u * LANES - B_PREV:(u + 1) * LANES]
        o = _pv_stage(s_ref, m, [v_tile], B_LIVE, extra_logit=sink_ref[p])
        for hh in range(2):
            hrows = slice(hh * HEAD_DIM, (hh + 1) * HEAD_DIM)
            gate = gt_ref[0, p, hrows, qlanes].astype(jnp.float32)
            row0 = p * PAIR_W + hh * HEAD_DIM
            zt_ref[row0:row0 + HEAD_DIM, qlanes] = (o[:, hh * LANES:(hh + 1) * LANES] * gate).astype(jnp.bfloat16)

    _head_pipeline(N_PAIRS * (TQB // LANES), scores_fn, pv_fn, s_refs, scores_first=False, per_pair=TQB // LANES)

    h2 = _out_and_residual(zt_ref, wout_ref, h_ref[0])
    out_ref[0] = (h2 * fg_ref[...]) * _rms_scale(h2)


def _attn_b(h, qbt, ksh, vsht, gbt, vec, sinks, wout, fg):
    b, s, d = h.shape
    qt4 = qbt.reshape(b, N_PAIRS, PAIR_W, s)
    gt4 = gbt.reshape(b, N_PAIRS, PAIR_W, s)
    feat_spec = pl.BlockSpec((1, N_PAIRS, PAIR_W, TQB), lambda bi, i: (bi, 0, 0, i))
    prev = lambda i: jnp.maximum(i * (TQB // B_PREV) - 1, 0)
    return pl.pallas_call(
        _attn_b_kernel,
        grid=(b, s // TQB),
        in_specs=[
            pl.BlockSpec((1, TQB, d), lambda bi, i: (bi, i, 0)),
            feat_spec,
            pl.BlockSpec((1, B_PREV, PAIR_W), lambda bi, i: (bi, prev(i), 0)),
            pl.BlockSpec((1, TQB, PAIR_W), lambda bi, i: (bi, i, 0)),
            pl.BlockSpec((1, PAIR_W, B_PREV), lambda bi, i: (bi, 0, prev(i))),
            pl.BlockSpec((1, PAIR_W, TQB), lambda bi, i: (bi, 0, i)),
            feat_spec,
            _const_spec(vec.shape),
            _const_spec(sinks.shape),
            _const_spec(wout.shape),
            _const_spec(fg.shape),
        ],
        out_specs=pl.BlockSpec((1, TQB, d), lambda bi, i: (bi, i, 0)),
        out_shape=jax.ShapeDtypeStruct((b, s, d), jnp.float32),
        scratch_shapes=[pltpu.VMEM((d, TQB), jnp.bfloat16),
                        pltpu.VMEM((2 * N_PAIRS, B_WIN, TQ), jnp.bfloat16),
                        *[pltpu.VMEM((B_WIN, TQ), jnp.float32)] * N_SCORE_BUFS_B],
        compiler_params=pltpu.CompilerParams(
            dimension_semantics=("arbitrary", "arbitrary"), vmem_limit_bytes=VMEM_LIMIT),
        name="attn_b",
    )(h, qt4, ksh, ksh, vsht, vsht, gt4, vec, sinks, wout, fg)


def _slab_diff():
    u = jnp.arange(BIAS_PERIOD)
    return jnp.where(u < TQ, u, u - BIAS_PERIOD)


def _bias_vecs_a(rel_bias):
    back = (A_KBLOCKS - 1 - jnp.arange(A_KBLOCKS))[:, None] * TQ
    dist = back + _slab_diff()[None, :]
    idx = jnp.clip(dist, -A_REL_CLIP, A_REL_CLIP) + A_REL_CLIP
    vec = jnp.transpose(rel_bias[idx], (0, 2, 1)).astype(jnp.float32)
    return vec[:, :, None, :]


def _t5_bucket(rel):
    nb = T5_BUCKETS // 2
    max_exact = nb // 2
    ret = jnp.where(rel > 0, nb, 0)
    n = jnp.abs(rel)
    nf = jnp.maximum(n, 1).astype(jnp.float32)
    large = max_exact + (jnp.log(nf / max_exact) / math.log(T5_MAX_DIST / max_exact)
                         * (nb - max_exact)).astype(jnp.int32)
    large = jnp.minimum(large, nb - 1)
    return ret + jnp.where(n < max_exact, n, large)


def _bias_vec_b(t5_table):
    rel = -_slab_diff() - B_PREV
    vec = jnp.transpose(t5_table[_t5_bucket(rel)], (1, 0)).astype(jnp.float32)
    return vec[None, :, None, :]


def kernel(x, a_norm, a_w_in, a_rel_bias, a_w_out, kv_norm, kv_w, t5_bias,
           b_norm, b_w_in, b_sinks, b_w_out, final_norm):
    assert a_norm.shape[0] == 1 and b_norm.shape[0] == 1, "one A layer then one B layer"
    scale = HEAD_DIM ** -0.5 * LOG2E

    def q_col_scale(n):
        return jnp.where(jnp.arange(n) < D_MODEL, scale, 1.0).astype(jnp.float32)

    wt_a = _weight_t(a_w_in[0], q_col_scale(a_w_in.shape[2]), a_norm[0], W_T_COLS_A)
    qt, k, vt, gt, nwt, kvw_k, wo_a, wo_b = _proj_a(x, wt_a, kv_w, b_w_in[0], q_col_scale(b_w_in.shape[2]),
                                                    jnp.stack([kv_norm, b_norm[0]]), a_w_out, b_w_out)
    h, ksh, vsht, qbt, gbt = _attn_a(
        x, qt, k, vt, gt, _bias_vecs_a(a_rel_bias[0] * LOG2E), wo_a, kvw_k, nwt)

    sinks = jnp.repeat((b_sinks[0].astype(jnp.float32) * LOG2E).reshape(N_PAIRS, 1, 2), LANES, axis=2)
    return _attn_b(h, qbt, ksh, vsht, gbt, _bias_vec_b(t5_bias * LOG2E), sinks, wo_b, final_norm[None, :])
```

```python
import functools
import math

import jax
import jax.numpy as jnp
import numpy as np
from jax import lax
from jax.experimental import pallas as pl
from jax.experimental.pallas import tpu as pltpu

D_MODEL = 1024
HEAD_DIM = 64
N_HEADS = D_MODEL // HEAD_DIM
N_PAIRS = N_HEADS // 2
PAIR_W = 2 * HEAD_DIM
CHUNK = 64
RMS_EPS = 1e-6
A_LEFT_CHUNKS = 8
A_REL_CLIP = 256
B_KV_HEADS = 2
B_LEFT_CHUNKS = 2
T5_BUCKETS = 32
T5_MAX_DIST = 128

TQ = 256
TQB = 512
TQA = 512
A_KBLOCKS = A_LEFT_CHUNKS * CHUNK // TQ + 1
B_PREV = B_LEFT_CHUNKS * CHUNK
TM_PROJ = 1024
FEAT_CHUNK = 256
W_T_COLS_A = 1024
W_T_COLS_B = 512
MASKED = -1e30
BIAS_PERIOD = 2 * TQ
SUBLANES = 8
LANES = 128
MAX_CHAINS = 1
BF16_ROWS = 16
LOG2E = math.log2(math.e)
N_SCORE_BUFS = 4
N_SCORE_BUFS_B = 8
VMEM_LIMIT = 56 * 1024 * 1024

_NT = (((1,), (1,)), ((), ()))
_TN = (((0,), (0,)), ((), ()))


def _rms_scale(xf):
    return lax.rsqrt(jnp.mean(xf * xf, axis=-1, keepdims=True) + RMS_EPS)


def _token_scales(xf):
    col = _rms_scale(xf)
    row = jnp.transpose(jnp.broadcast_to(col, (xf.shape[0], LANES)))[0:1, :]
    return col, row


def _silu(v):
    return v * jax.nn.sigmoid(v)


def _weight_t_kernel(w_ref, cs_ref, gain_ref, o_ref):
    o_ref[...] = ((w_ref[...] * cs_ref[...]).T * gain_ref[...]).astype(jnp.bfloat16)


def _weight_t(w, col_scale, row_gain, tc):
    d, n = w.shape
    return pl.pallas_call(
        _weight_t_kernel,
        grid=(n // tc,),
        in_specs=[
            pl.BlockSpec((d, tc), lambda j: (0, j)),
            pl.BlockSpec((1, tc), lambda j: (0, j)),
            pl.BlockSpec((1, d), lambda j: (0, 0)),
        ],
        out_specs=pl.BlockSpec((tc, d), lambda j: (j, 0)),
        out_shape=jax.ShapeDtypeStruct((n, d), jnp.bfloat16),
        compiler_params=pltpu.CompilerParams(dimension_semantics=("arbitrary",), vmem_limit_bytes=VMEM_LIMIT),
        name="weight_t",
    )(w, col_scale[None, :], row_gain[None, :])


def _proj_a_kernel(x_ref, wt_ref, kvw_ref, kvg_ref, bw_ref, cs_ref, bg_ref, woa_ref, wob_ref,
                   qt_ref, k_ref, vt_ref, gt_ref, nwt_ref, kvwk_ref, woa16_ref, wob16_ref):
    step = pl.program_id(0) * pl.num_programs(1) + pl.program_id(1)
    head = jnp.concatenate([jnp.zeros((D_MODEL, W_T_COLS_B - PAIR_W), jnp.float32), kvw_ref[:, PAIR_W:]], axis=1)
    src = jnp.where(step == 0, head, bw_ref[...])
    gain = jnp.where(step == 0, kvg_ref[...], bg_ref[...])
    nwt_ref[...] = ((src * cs_ref[...]).T * gain).astype(jnp.bfloat16)
    g_rows = jnp.transpose(jnp.broadcast_to(kvg_ref[...], (PAIR_W, D_MODEL)))
    kvwk_ref[...] = (kvw_ref[:, :PAIR_W] * g_rows).astype(jnp.bfloat16)
    woa16_ref[...] = woa_ref[0].astype(jnp.bfloat16)
    wob16_ref[...] = wob_ref[0].astype(jnp.bfloat16)

    xf = x_ref[0]
    xb = xf.astype(jnp.bfloat16)
    r_col, r_row = _token_scales(xf)
    k = lax.dot_general(xb, wt_ref[D_MODEL:2 * D_MODEL, :], _NT,
                        preferred_element_type=jnp.float32) * r_col
    for p in range(N_PAIRS):
        k_ref[0, p] = k[:, p * PAIR_W:(p + 1) * PAIR_W].astype(jnp.bfloat16)
    for out_ref, base, act in ((gt_ref, 3 * D_MODEL, _silu), (vt_ref, 2 * D_MODEL, None), (qt_ref, 0, None)):
        for c in range(0, D_MODEL, FEAT_CHUNK):
            w = wt_ref[base + c:base + c + FEAT_CHUNK, :]
            yt = lax.dot_general(w, xb, _NT, preferred_element_type=jnp.float32) * r_row
            out_ref[0, c:c + FEAT_CHUNK, :] = (yt if act is None else act(yt)).astype(jnp.bfloat16)


def _proj_a(x, wt, kv_w, b_w, b_col_scale, kv_gain, b_gain, a_w_out, b_w_out):
    b, s, d = x.shape
    n_i = s // TM_PROJ
    wo_rows = d // (b * n_i)
    step = lambda bi, i: bi * n_i + i
    n_blocks = 1 + b_w.shape[1] // W_T_COLS_B
    assert n_blocks <= b * n_i and kv_w.shape == (d, 2 * PAIR_W) and d % (b * n_i * BF16_ROWS) == 0
    blk = lambda bi, i: jnp.minimum(bi * n_i + i, n_blocks - 1)
    cs_all = np.concatenate([np.ones((W_T_COLS_B,), np.float32), b_col_scale])[None, :]
    feat = jax.ShapeDtypeStruct((b, d, s), jnp.bfloat16)
    return pl.pallas_call(
        _proj_a_kernel,
        grid=(b, n_i),
        in_specs=[
            pl.BlockSpec((1, TM_PROJ, d), lambda bi, i: (bi, i, 0)),
            _const_spec(wt.shape),
            _const_spec(kv_w.shape),
            _const_spec((1, d)),
            pl.BlockSpec((d, W_T_COLS_B), lambda bi, i: (0, jnp.maximum(blk(bi, i) - 1, 0))),
            pl.BlockSpec((1, W_T_COLS_B), lambda bi, i: (0, blk(bi, i))),
            _const_spec((1, d)),
            pl.BlockSpec((1, wo_rows, d), lambda bi, i: (0, step(bi, i), 0)),
            pl.BlockSpec((1, wo_rows, d), lambda bi, i: (0, step(bi, i), 0)),
        ],
        out_specs=[
            pl.BlockSpec((1, d, TM_PROJ), lambda bi, i: (bi, 0, i)),
            pl.BlockSpec((1, N_PAIRS, TM_PROJ, PAIR_W), lambda bi, i: (bi, 0, i, 0)),
            pl.BlockSpec((1, d, TM_PROJ), lambda bi, i: (bi, 0, i)),
            pl.BlockSpec((1, d, TM_PROJ), lambda bi, i: (bi, 0, i)),
            pl.BlockSpec((W_T_COLS_B, d), lambda bi, i: (blk(bi, i), 0)),
            pl.BlockSpec((d, PAIR_W), lambda bi, i: (0, 0)),
            pl.BlockSpec((wo_rows, d), lambda bi, i: (step(bi, i), 0)),
            pl.BlockSpec((wo_rows, d), lambda bi, i: (step(bi, i), 0)),
        ],
        out_shape=[feat, jax.ShapeDtypeStruct((b, N_PAIRS, s, PAIR_W), jnp.bfloat16), feat, feat,
                   jax.ShapeDtypeStruct((n_blocks * W_T_COLS_B, d), jnp.bfloat16),
                   jax.ShapeDtypeStruct((d, PAIR_W), jnp.bfloat16),
                   jax.ShapeDtypeStruct((d, d), jnp.bfloat16), jax.ShapeDtypeStruct((d, d), jnp.bfloat16)],
        compiler_params=pltpu.CompilerParams(
            dimension_semantics=("arbitrary", "arbitrary"), vmem_limit_bytes=VMEM_LIMIT),
        name="proj_a",
    )(x, wt, kv_w, kv_gain[None, :], b_w, cs_all, b_gain[None, :], a_w_out, b_w_out)


def _chunk_iota(shape, axis):
    return lax.shift_right_logical(lax.broadcasted_iota(jnp.int32, shape, axis), CHUNK.bit_length() - 1)


def _identity_bf16(n):
    return (lax.broadcasted_iota(jnp.int32, (n, n), 0) == lax.broadcasted_iota(jnp.int32, (n, n), 1)).astype(jnp.bfloat16)


def _toeplitz(vec_row, n_keys):
    return pltpu.roll(jnp.broadcast_to(vec_row, (n_keys, BIAS_PERIOD)), 0, 1, stride=1, stride_axis=0)


def _fill_bias_slabs(vec_ref, bias_ref, n_keys, bands):
    n = len(bands)
    kch = _chunk_iota((n_keys, TQ), 0)
    qch = _chunk_iota((n_keys, TQ), 1)

    def head_body(h, carry):
        for j, band in enumerate(bands):
            t = _toeplitz(vec_ref[j, h], n_keys)[:, :TQ]
            if band is not None:
                t = jnp.where(band(kch, qch), t, MASKED)
            bias_ref[h * n + j] = t
        return carry

    lax.fori_loop(0, N_HEADS, head_body, 0)


def _live_rows(band, n_keys):
    chunks_per_half = LANES // CHUNK
    out = []
    for half in range(TQ // LANES):
        qchs = range(half * chunks_per_half, (half + 1) * chunks_per_half)
        live = [kc for kc in range(n_keys // CHUNK) if band is None or any(band(kc, qc) for qc in qchs)]
        out.append((live[0] * CHUNK, (live[-1] + 1) * CHUNK) if live else (0, 0))
    return tuple(out)


def _scores_stage(k_tiles, q_tiles, bias_fns, live, s_ref):
    maxes, row = [], 0
    for half in range(TQ // LANES):
        maxes.append([None] * MAX_CHAINS)
    for kt, qz, bias_fn, live_j in zip(k_tiles, q_tiles, bias_fns, live):
        s = jnp.dot(kt, qz, preferred_element_type=jnp.float32)
        for r in range(s.shape[0] // SUBLANES):
            rows = slice(r * SUBLANES, (r + 1) * SUBLANES)
            for half, (r0, r1) in enumerate(live_j):
                if not r0 <= r * SUBLANES < r1:
                    continue
                lanes = slice(half * LANES, (half + 1) * LANES)
                acc = maxes[half]
                grp = s[rows, lanes] if bias_fn is None else s[rows, lanes] + bias_fn(rows, lanes)
                s_ref[row + r * SUBLANES:row + (r + 1) * SUBLANES, lanes] = grp
                c = r % MAX_CHAINS
                acc[c] = grp if acc[c] is None else jnp.maximum(acc[c], grp)
        row += s.shape[0]
    cols = [jnp.max(functools.reduce(jnp.maximum, [a for a in acc if a is not None]), axis=0, keepdims=True)
            for acc in maxes]
    return jnp.concatenate(cols, axis=1)


def _pv_stage(s_ref, m, v_tiles, live, extra_logit=None):
    if extra_logit is not None:
        m = jnp.maximum(m, extra_logit)
    acc, row = None, 0
    for vt, live_j in zip(v_tiles, live):
        n = vt.shape[1]
        halves = []
        for half, (r0, r1) in enumerate(live_j):
            lanes = slice(half * LANES, (half + 1) * LANES)
            parts = [jnp.zeros((r0, LANES), jnp.bfloat16)] if r0 else []
            if r1 > r0:
                parts.append(jnp.exp2(s_ref[row + r0:row + r1, lanes] - m[:, lanes]).astype(jnp.bfloat16))
            if n > r1:
                parts.append(jnp.zeros((n - r1, LANES), jnp.bfloat16))
            halves.append(parts[0] if len(parts) == 1 else jnp.concatenate(parts, axis=0))
        p = jnp.concatenate(halves, axis=1)
        v_ones = jnp.concatenate([vt, jnp.ones((BF16_ROWS, n), vt.dtype)], axis=0)
        part = jnp.dot(v_ones, p, preferred_element_type=jnp.float32)
        acc = part if acc is None else acc + part
        row += n
    l = acc[HEAD_DIM:HEAD_DIM + 1, :]
    if extra_logit is not None:
        l = l + jnp.exp2(extra_logit - m)
    return acc[:HEAD_DIM, :] / l


def _head_pipeline(n_tiles, scores_fn, pv_fn, s_refs, scores_first, per_pair=2):
    per_group = len(s_refs) // 2
    x_refs, y_refs = s_refs[:per_group], s_refs[per_group:]
    n_groups = n_tiles // per_group

    m = [scores_fn(t // per_pair, t % per_pair, ref) for t, ref in zip(range(per_group), x_refs)]
    for g in range(n_groups):
        src, dst = (x_refs, y_refs) if g % 2 == 0 else (y_refs, x_refs)
        nxt = [(g + 1) * per_group + j for j in range(per_group)] if g + 1 < n_groups else []
        m_next = [scores_fn(t // per_pair, t % per_pair, ref) for t, ref in zip(nxt, dst)] if scores_first else []
        for j in range(per_group):
            if not scores_first and nxt:
                m_next.append(scores_fn(nxt[j] // per_pair, nxt[j] % per_pair, dst[j]))
            t = g * per_group + j
            pv_fn(t // per_pair, t % per_pair, src[j], m[j])
        m = m_next


A_BANDS = (lambda kc, qc: kc >= qc,
           None,
           lambda kc, qc: kc <= qc)
A_LIVE = tuple(_live_rows(band, TQ) for band in A_BANDS)
B_WIN = B_PREV + LANES
B_SUB = PAIR_W
B_BAND = lambda kc, qc: (kc >= qc) & (kc <= qc + B_LEFT_CHUNKS)
B_LIVE = (((0, B_WIN),) * (TQ // LANES),)


def _out_and_residual(zt_ref, wout_ref, res):
    y = lax.dot_general(zt_ref[...], wout_ref[...], _TN, preferred_element_type=jnp.float32)
    return res + y


def _attn_a_kernel(x_ref, qt_ref, kp_ref, ko_ref, vp_ref, vo_ref, gt_ref,
                   bvec_ref, wout_ref, kvwk_ref, nwt_ref,
                   h_ref, ksh_ref, vsht_ref, qbt_ref, gbt_ref, zt_ref, bias_ref, *s_refs):
    i = pl.program_id(1)
    subs = TQA // TQ

    @pl.when((pl.program_id(0) == 0) & (i == 0))
    def _():
        _fill_bias_slabs(bvec_ref, bias_ref, TQ, A_BANDS)
        bias_ref[N_HEADS * A_KBLOCKS] = jnp.full((TQ, TQ), MASKED, jnp.float32)

    n_slabs = N_HEADS * A_KBLOCKS
    rows = lax.broadcasted_iota(jnp.int32, (PAIR_W, TQ), 0)

    def key_block(sb, j):
        n = sb + j
        return n // subs, (n % subs) * TQ

    def scores_fn(sp, hh, s_ref):
        sb, p = divmod(sp, N_PAIRS)
        q2 = qt_ref[0, p, :, sb * TQ:(sb + 1) * TQ]
        own_rows = rows < HEAD_DIM if hh == 0 else rows >= HEAD_DIM
        qz = jnp.where(own_rows, q2, jnp.zeros_like(q2))
        k_tiles, bias_fns = [], []
        for j in range(A_KBLOCKS):
            in_seq = i * subs + sb - (A_KBLOCKS - 1) + j >= 0
            slab = jnp.where(in_seq, (2 * p + hh) * A_KBLOCKS + j, n_slabs)
            own, t0 = key_block(sb, j)
            k_tiles.append((ko_ref if own else kp_ref)[0, p, t0:t0 + TQ, :])
            bias_fns.append(lambda rws, lanes, slab=slab: bias_ref[slab, rws, lanes])
        return _scores_stage(k_tiles, [qz] * A_KBLOCKS, bias_fns, A_LIVE, s_ref)

    def pv_fn(sp, hh, s_ref, m):
        sb, p = divmod(sp, N_PAIRS)
        half = slice(hh * HEAD_DIM, (hh + 1) * HEAD_DIM)
        qcols = slice(sb * TQ, (sb + 1) * TQ)
        v_tiles = []
        for j in range(A_KBLOCKS):
            own, t0 = key_block(sb, j)
            v_tiles.append((vo_ref if own else vp_ref)[0, p, half, t0:t0 + TQ])
        o = _pv_stage(s_ref, m, v_tiles, A_LIVE)
        gate = gt_ref[0, p, half, qcols].astype(jnp.float32)
        row0 = p * PAIR_W + hh * HEAD_DIM
        zt_ref[row0:row0 + HEAD_DIM, qcols] = (o * gate).astype(jnp.bfloat16)

    _head_pipeline(subs * N_HEADS, scores_fn, pv_fn, s_refs, scores_first=True)

    h = _out_and_residual(zt_ref, wout_ref, x_ref[0])
    h_ref[0] = h
    hb = h.astype(jnp.bfloat16)
    r_col, r_row = _token_scales(h)
    ksh_ref[0] = (jnp.dot(hb, kvwk_ref[...], preferred_element_type=jnp.float32) * r_col).astype(jnp.bfloat16)
    for out_ref, base, act in ((gbt_ref, W_T_COLS_B + D_MODEL, _silu), (qbt_ref, W_T_COLS_B, None)):
        for c in range(0, D_MODEL, FEAT_CHUNK):
            first = out_ref is qbt_ref and c == 0
            start = base + c - (PAIR_W if first else 0)
            yt = lax.dot_general(nwt_ref[start:base + c + FEAT_CHUNK, :], hb, _NT,
                                 preferred_element_type=jnp.float32) * r_row
            if first:
                vsht_ref[0] = yt[:PAIR_W].astype(jnp.bfloat16)
                yt = yt[PAIR_W:]
            out_ref[0, c:c + FEAT_CHUNK, :] = (yt if act is None else act(yt)).astype(jnp.bfloat16)


def _const_spec(shape):
    return pl.BlockSpec(shape, lambda bi, i: (0,) * len(shape), pipeline_mode=pl.Buffered(1))


def _attn_a(x, qt, k, vt, gt, bvec, wout, kvwk, nwt):
    b, s, d = x.shape
    qt4 = qt.reshape(b, N_PAIRS, PAIR_W, s)
    vt4 = vt.reshape(b, N_PAIRS, PAIR_W, s)
    gt4 = gt.reshape(b, N_PAIRS, PAIR_W, s)
    assert (A_KBLOCKS - 1) * TQ <= TQA, "the keys before a grid block must fit in one previous block"
    prev = lambda i: jnp.maximum(i - 1, 0)

    def kspec(index):
        return pl.BlockSpec((1, N_PAIRS, TQA, PAIR_W), lambda bi, i: (bi, 0, index(i), 0))

    def vspec(index):
        return pl.BlockSpec((1, N_PAIRS, PAIR_W, TQA), lambda bi, i: (bi, 0, 0, index(i)))

    feat_spec = pl.BlockSpec((1, N_PAIRS, PAIR_W, TQA), lambda bi, i: (bi, 0, 0, i))
    featout_spec = pl.BlockSpec((1, d, TQA), lambda bi, i: (bi, 0, i))
    feat = jax.ShapeDtypeStruct((b, d, s), jnp.bfloat16)
    return pl.pallas_call(
        _attn_a_kernel,
        grid=(b, s // TQA),
        in_specs=[
            pl.BlockSpec((1, TQA, d), lambda bi, i: (bi, i, 0)),
            feat_spec,
            kspec(prev), kspec(lambda i: i),
            vspec(prev), vspec(lambda i: i),
            feat_spec,
            _const_spec(bvec.shape),
            _const_spec(wout.shape),
            _const_spec(kvwk.shape),
            _const_spec(nwt.shape),
        ],
        out_specs=[
            pl.BlockSpec((1, TQA, d), lambda bi, i: (bi, i, 0)),
            pl.BlockSpec((1, TQA, PAIR_W), lambda bi, i: (bi, i, 0)),
            pl.BlockSpec((1, PAIR_W, TQA), lambda bi, i: (bi, 0, i)),
            featout_spec,
            featout_spec,
        ],
        out_shape=[
            jax.ShapeDtypeStruct((b, s, d), jnp.float32),
            jax.ShapeDtypeStruct((b, s, PAIR_W), jnp.bfloat16),
            jax.ShapeDtypeStruct((b, PAIR_W, s), jnp.bfloat16),
            feat, feat,
        ],
        scratch_shapes=[pltpu.VMEM((d, TQA), jnp.bfloat16),
                        pltpu.VMEM((N_HEADS * A_KBLOCKS + 1, TQ, TQ), jnp.float32),
                        *[pltpu.VMEM((A_KBLOCKS * TQ, TQ), jnp.float32)] * N_SCORE_BUFS],
        compiler_params=pltpu.CompilerParams(
            dimension_semantics=("arbitrary", "arbitrary"), vmem_limit_bytes=VMEM_LIMIT),
        name="attn_a",
    )(x, qt4, k, k, vt4, vt4, gt4, bvec, wout, kvwk, nwt)


def _attn_b_kernel(h_ref, qt_ref, kp_ref, ko_ref, vp_ref, vo_ref, gt_ref, vec_ref,
                   sink_ref, wout_ref, fg_ref, out_ref, zt_ref, bias_ref, *s_refs):
    i = pl.program_id(1)

    @pl.when((pl.program_id(0) == 0) & (i == 0))
    def _():
        in_band = B_BAND(_chunk_iota((B_WIN, LANES), 0), _chunk_iota((B_WIN, LANES), 1))
        before_block = lax.broadcasted_iota(jnp.int32, (B_WIN, LANES), 0) < B_PREV

        def pair_body(p, carry):
            for hh in range(2):
                t = jnp.where(in_band, _toeplitz(vec_ref[0, 2 * p + hh], B_WIN)[:, :LANES], MASKED)
                lanes = slice(hh * LANES, (hh + 1) * LANES)
                bias_ref[2 * p, :, lanes] = t.astype(jnp.bfloat16)
                bias_ref[2 * p + 1, :, lanes] = jnp.where(before_block, MASKED, t).astype(jnp.bfloat16)
            return carry

        lax.fori_loop(0, N_PAIRS, pair_body, 0)

    rows = lax.broadcasted_iota(jnp.int32, (PAIR_W, TQ), 0)
    pairs_per_kv = N_PAIRS // B_KV_HEADS
    eye = _identity_bf16(B_SUB)

    def scores_fn(p, u, s_ref):
        qlanes = slice(u * LANES, (u + 1) * LANES)
        q2 = jnp.concatenate([qt_ref[0, p, :HEAD_DIM, qlanes], qt_ref[0, p, HEAD_DIM:, qlanes]], axis=1)
        kv0 = HEAD_DIM * (p // pairs_per_kv)
        kv_rows = (rows >= kv0) & (rows < kv0 + HEAD_DIM)
        qz = jnp.where(kv_rows, jnp.concatenate([q2, q2], axis=0), jnp.zeros((PAIR_W, TQ), q2.dtype))
        if u == 0:
            k_tile = jnp.concatenate([kp_ref[0], ko_ref[0, :LANES, :]], axis=0)
            variant = jnp.where(i == 0, 1, 0)
        else:
            k_tile, variant = ko_ref[0, u * LANES - B_PREV:(u + 1) * LANES, :], 0

        lhs, rhs = [], []
        for r0 in range(0, B_WIN, B_SUB):
            lhs.append(jnp.concatenate([k_tile[r0:r0 + B_SUB, :], eye], axis=1))
            rhs.append(jnp.concatenate([qz, bias_ref[2 * p + variant, r0:r0 + B_SUB, :]], axis=0))
        sub_live = (((0, B_SUB),) * (TQ // LANES),) * len(lhs)
        return _scores_stage(lhs, rhs, [None] * len(lhs), sub_live, s_ref)

    def pv_fn(p, u, s_ref, m):
        qlanes = slice(u * LANES, (u + 1) * LANES)
        kv_rows = slice(HEAD_DIM * (p // pairs_per_kv), HEAD_DIM * (p // pairs_per_kv + 1))
        if u == 0:
            v_tile = jnp.concatenate([vp_ref[0, kv_rows, :], vo_ref[0, kv_rows, :LANES]], axis=1)
        else:
            v_tile = vo_ref[0, kv_rows, u * LANES - B_PREV:(u + 1) * LANES]
        o = _pv_stage(s_ref, m, [v_tile], B_LIVE, extra_logit=sink_ref[p])
        for hh in range(2):
            hrows = slice(hh * HEAD_DIM, (hh + 1) * HEAD_DIM)
            gate = gt_ref[0, p, hrows, qlanes].astype(jnp.float32)
            row0 = p * PAIR_W + hh * HEAD_DIM
            zt_ref[row0:row0 + HEAD_DIM, qlanes] = (o[:, hh * LANES:(hh + 1) * LANES] * gate).astype(jnp.bfloat16)

    _head_pipeline(N_PAIRS * (TQB // LANES), scores_fn, pv_fn, s_refs, scores_first=False, per_pair=TQB // LANES)

    h2 = _out_and_residual(zt_ref, wout_ref, h_ref[0])
    out_ref[0] = (h2 * fg_ref[...]) * _rms_scale(h2)


def _attn_b(h, qbt, ksh, vsht, gbt, vec, sinks, wout, fg):
    b, s, d = h.shape
    qt4 = qbt.reshape(b, N_PAIRS, PAIR_W, s)
    gt4 = gbt.reshape(b, N_PAIRS, PAIR_W, s)
    feat_spec = pl.BlockSpec((1, N_PAIRS, PAIR_W, TQB), lambda bi, i: (bi, 0, 0, i))
    prev = lambda i: jnp.maximum(i * (TQB // B_PREV) - 1, 0)
    return pl.pallas_call(
        _attn_b_kernel,
        grid=(b, s // TQB),
        in_specs=[
            pl.BlockSpec((1, TQB, d), lambda bi, i: (bi, i, 0)),
            feat_spec,
            pl.BlockSpec((1, B_PREV, PAIR_W), lambda bi, i: (bi, prev(i), 0)),
            pl.BlockSpec((1, TQB, PAIR_W), lambda bi, i: (bi, i, 0)),
            pl.BlockSpec((1, PAIR_W, B_PREV), lambda bi, i: (bi, 0, prev(i))),
            pl.BlockSpec((1, PAIR_W, TQB), lambda bi, i: (bi, 0, i)),
            feat_spec,
            _const_spec(vec.shape),
            _const_spec(sinks.shape),
            _const_spec(wout.shape),
            _const_spec(fg.shape),
        ],
        out_specs=pl.BlockSpec((1, TQB, d), lambda bi, i: (bi, i, 0)),
        out_shape=jax.ShapeDtypeStruct((b, s, d), jnp.float32),
        scratch_shapes=[pltpu.VMEM((d, TQB), jnp.bfloat16),
                        pltpu.VMEM((2 * N_PAIRS, B_WIN, TQ), jnp.bfloat16),
                        *[pltpu.VMEM((B_WIN, TQ), jnp.float32)] * N_SCORE_BUFS_B],
        compiler_params=pltpu.CompilerParams(
            dimension_semantics=("arbitrary", "arbitrary"), vmem_limit_bytes=VMEM_LIMIT),
        name="attn_b",
    )(h, qt4, ksh, ksh, vsht, vsht, gt4, vec, sinks, wout, fg)


def _slab_diff(xp=jnp):
    u = xp.arange(BIAS_PERIOD)
    return xp.where(u < TQ, u, u - BIAS_PERIOD)


def _static_take(table, idx):
    parts, lo = [], 0
    while lo < len(idx):
        hi = lo + 1
        step = idx[hi] - idx[lo] if hi < len(idx) else 0
        while step in (0, 1) and hi < len(idx) and idx[hi] - idx[hi - 1] == step:
            hi += 1
        rows = table[idx[lo]:idx[hi - 1] + 1]
        parts.append(jnp.broadcast_to(rows, (hi - lo,) + table.shape[1:]) if step == 0 else rows)
        lo = hi
    return jnp.concatenate(parts, axis=0)


def _bias_vecs_a(rel_bias):
    back = (A_KBLOCKS - 1 - np.arange(A_KBLOCKS))[:, None] * TQ
    dist = back + _slab_diff(np)[None, :]
    idx = np.clip(dist, -A_REL_CLIP, A_REL_CLIP) + A_REL_CLIP
    vec = jnp.stack([_static_take(rel_bias, [int(v) for v in row]) for row in idx])
    return jnp.transpose(vec, (0, 2, 1)).astype(jnp.float32)[:, :, None, :]


def _t5_bucket(rel):
    nb = T5_BUCKETS // 2
    max_exact = nb // 2
    ret = jnp.where(rel > 0, nb, 0)
    n = jnp.abs(rel)
    nf = jnp.maximum(n, 1).astype(jnp.float32)
    large = max_exact + (jnp.log(nf / max_exact) / math.log(T5_MAX_DIST / max_exact)
                         * (nb - max_exact)).astype(jnp.int32)
    large = jnp.minimum(large, nb - 1)
    return ret + jnp.where(n < max_exact, n, large)


def _bias_vec_b(t5_table):
    rel = -_slab_diff() - B_PREV
    vec = jnp.transpose(t5_table[_t5_bucket(rel)], (1, 0)).astype(jnp.float32)
    return vec[None, :, None, :]


def kernel(x, a_norm, a_w_in, a_rel_bias, a_w_out, kv_norm, kv_w, t5_bias,
           b_norm, b_w_in, b_sinks, b_w_out, final_norm):
    assert a_norm.shape[0] == 1 and b_norm.shape[0] == 1, "one A layer then one B layer"
    scale = HEAD_DIM ** -0.5 * LOG2E

    def q_col_scale(n):
        return np.where(np.arange(n) < D_MODEL, scale, 1.0).astype(np.float32)

    wt_a = _weight_t(a_w_in[0], q_col_scale(a_w_in.shape[2]), a_norm[0], W_T_COLS_A)
    qt, k, vt, gt, nwt, kvw_k, wo_a, wo_b = _proj_a(x, wt_a, kv_w, b_w_in[0], q_col_scale(b_w_in.shape[2]),
                                                    kv_norm, b_norm[0], a_w_out, b_w_out)
    h, ksh, vsht, qbt, gbt = _attn_a(
        x, qt, k, vt, gt, _bias_vecs_a(a_rel_bias[0] * LOG2E), wo_a, kvw_k, nwt)

    sinks = jnp.repeat((b_sinks[0].astype(jnp.float32) * LOG2E).reshape(N_PAIRS, 1, 2), LANES, axis=2)
    return _attn_b(h, qbt, ksh, vsht, gbt, _bias_vec_b(t5_bias * LOG2E), sinks, wo_b, final_norm[None, :])
```

```python
import functools
import math

import jax
import jax.numpy as jnp
import numpy as np
from jax import lax
from jax.experimental import pallas as pl
from jax.experimental.pallas import tpu as pltpu

D_MODEL = 1024
HEAD_DIM = 64
N_HEADS = D_MODEL // HEAD_DIM
N_PAIRS = N_HEADS // 2
PAIR_W = 2 * HEAD_DIM
CHUNK = 64
RMS_EPS = 1e-6
A_LEFT_CHUNKS = 8
A_REL_CLIP = 256
B_KV_HEADS = 2
B_LEFT_CHUNKS = 2
T5_BUCKETS = 32
T5_MAX_DIST = 128

TQ = 256
TQB = 512
TQA = 512
A_KBLOCKS = A_LEFT_CHUNKS * CHUNK // TQ + 1
B_PREV = B_LEFT_CHUNKS * CHUNK
TM_PROJ = 1024
FEAT_CHUNK = 256
W_T_COLS_A = 1024
W_T_COLS_B = 512
MASKED = -1e30
BIAS_PERIOD = 2 * TQ
SUBLANES = 8
LANES = 128
MAX_CHAINS = 1
BF16_ROWS = 16
LOG2E = math.log2(math.e)
N_SCORE_BUFS = 4
N_SCORE_BUFS_B = 8
VMEM_LIMIT = 56 * 1024 * 1024

_NT = (((1,), (1,)), ((), ()))
_TN = (((0,), (0,)), ((), ()))


def _rms_scale(xf):
    return lax.rsqrt(jnp.mean(xf * xf, axis=-1, keepdims=True) + RMS_EPS)


def _token_scales(xf):
    col = _rms_scale(xf)
    row = jnp.transpose(jnp.broadcast_to(col, (xf.shape[0], LANES)))[0:1, :]
    return col, row


def _silu(v):
    return v * jax.nn.sigmoid(v)


def _weight_t_kernel(w_ref, cs_ref, gain_ref, o_ref):
    o_ref[...] = ((w_ref[...] * cs_ref[...]).T * gain_ref[...]).astype(jnp.bfloat16)


def _weight_t(w, col_scale, row_gain, tc):
    d, n = w.shape
    return pl.pallas_call(
        _weight_t_kernel,
        grid=(n // tc,),
        in_specs=[
            pl.BlockSpec((d, tc), lambda j: (0, j)),
            pl.BlockSpec((1, tc), lambda j: (0, j)),
            pl.BlockSpec((1, d), lambda j: (0, 0)),
        ],
        out_specs=pl.BlockSpec((tc, d), lambda j: (j, 0)),
        out_shape=jax.ShapeDtypeStruct((n, d), jnp.bfloat16),
        compiler_params=pltpu.CompilerParams(dimension_semantics=("arbitrary",), vmem_limit_bytes=VMEM_LIMIT),
        name="weight_t",
    )(w, col_scale[None, :], row_gain[None, :])


def _proj_a_kernel(x_ref, wt_ref, kvw_ref, kvg_ref, bw_ref, cs_ref, bg_ref, woa_ref, wob_ref,
                   qt_ref, k_ref, vt_ref, gt_ref, nwt_ref, kvwk_ref, woa16_ref, wob16_ref):
    step = pl.program_id(0) * pl.num_programs(1) + pl.program_id(1)
    head = jnp.concatenate([jnp.zeros((D_MODEL, W_T_COLS_B - PAIR_W), jnp.float32), kvw_ref[:, PAIR_W:]], axis=1)
    src = jnp.where(step == 0, head, bw_ref[...])
    gain = jnp.where(step == 0, kvg_ref[...], bg_ref[...])
    nwt_ref[...] = ((src * cs_ref[...]).T * gain).astype(jnp.bfloat16)
    g_rows = jnp.transpose(jnp.broadcast_to(kvg_ref[...], (PAIR_W, D_MODEL)))
    kvwk_ref[...] = (kvw_ref[:, :PAIR_W] * g_rows).astype(jnp.bfloat16)
    woa16_ref[...] = woa_ref[0].astype(jnp.bfloat16)
    wob16_ref[...] = wob_ref[0].astype(jnp.bfloat16)

    xf = x_ref[0]
    xb = xf.astype(jnp.bfloat16)
    r_col, r_row = _token_scales(xf)
    k = lax.dot_general(xb, wt_ref[D_MODEL:2 * D_MODEL, :], _NT,
                        preferred_element_type=jnp.float32) * r_col
    for p in range(N_PAIRS):
        k_ref[0, p] = k[:, p * PAIR_W:(p + 1) * PAIR_W].astype(jnp.bfloat16)
    for out_ref, base, act in ((gt_ref, 3 * D_MODEL, _silu), (vt_ref, 2 * D_MODEL, None), (qt_ref, 0, None)):
        for c in range(0, D_MODEL, FEAT_CHUNK):
            w = wt_ref[base + c:base + c + FEAT_CHUNK, :]
            yt = lax.dot_general(w, xb, _NT, preferred_element_type=jnp.float32) * r_row
            out_ref[0, c:c + FEAT_CHUNK, :] = (yt if act is None else act(yt)).astype(jnp.bfloat16)


def _proj_a(x, wt, kv_w, b_w, b_col_scale, kv_gain, b_gain, a_w_out, b_w_out):
    b, s, d = x.shape
    n_i = s // TM_PROJ
    wo_rows = d // (b * n_i)
    step = lambda bi, i: bi * n_i + i
    n_blocks = 1 + b_w.shape[1] // W_T_COLS_B
    assert n_blocks <= b * n_i and kv_w.shape == (d, 2 * PAIR_W) and d % (b * n_i * BF16_ROWS) == 0
    blk = lambda bi, i: jnp.minimum(bi * n_i + i, n_blocks - 1)
    cs_all = np.concatenate([np.ones((W_T_COLS_B,), np.float32), b_col_scale])[None, :]
    feat = jax.ShapeDtypeStruct((b, d, s), jnp.bfloat16)
    return pl.pallas_call(
        _proj_a_kernel,
        grid=(b, n_i),
        in_specs=[
            pl.BlockSpec((1, TM_PROJ, d), lambda bi, i: (bi, i, 0)),
            _const_spec(wt.shape),
            _const_spec(kv_w.shape),
            _const_spec((1, d)),
            pl.BlockSpec((d, W_T_COLS_B), lambda bi, i: (0, jnp.maximum(blk(bi, i) - 1, 0))),
            pl.BlockSpec((1, W_T_COLS_B), lambda bi, i: (0, blk(bi, i))),
            _const_spec((1, d)),
            pl.BlockSpec((1, wo_rows, d), lambda bi, i: (0, step(bi, i), 0)),
            pl.BlockSpec((1, wo_rows, d), lambda bi, i: (0, step(bi, i), 0)),
        ],
        out_specs=[
            pl.BlockSpec((1, d, TM_PROJ), lambda bi, i: (bi, 0, i)),
            pl.BlockSpec((1, N_PAIRS, TM_PROJ, PAIR_W), lambda bi, i: (bi, 0, i, 0)),
            pl.BlockSpec((1, d, TM_PROJ), lambda bi, i: (bi, 0, i)),
            pl.BlockSpec((1, d, TM_PROJ), lambda bi, i: (bi, 0, i)),
            pl.BlockSpec((W_T_COLS_B, d), lambda bi, i: (blk(bi, i), 0)),
            pl.BlockSpec((d, PAIR_W), lambda bi, i: (0, 0)),
            pl.BlockSpec((wo_rows, d), lambda bi, i: (step(bi, i), 0)),
            pl.BlockSpec((wo_rows, d), lambda bi, i: (step(bi, i), 0)),
        ],
        out_shape=[feat, jax.ShapeDtypeStruct((b, N_PAIRS, s, PAIR_W), jnp.bfloat16), feat, feat,
                   jax.ShapeDtypeStruct((n_blocks * W_T_COLS_B, d), jnp.bfloat16),
                   jax.ShapeDtypeStruct((d, PAIR_W), jnp.bfloat16),
                   jax.ShapeDtypeStruct((d, d), jnp.bfloat16), jax.ShapeDtypeStruct((d, d), jnp.bfloat16)],
        compiler_params=pltpu.CompilerParams(
            dimension_semantics=("arbitrary", "arbitrary"), vmem_limit_bytes=VMEM_LIMIT),
        name="proj_a",
    )(x, wt, kv_w, kv_gain[None, :], b_w, cs_all, b_gain[None, :], a_w_out, b_w_out)


def _chunk_iota(shape, axis):
    return lax.shift_right_logical(lax.broadcasted_iota(jnp.int32, shape, axis), CHUNK.bit_length() - 1)


def _identity_bf16(n):
    return (lax.broadcasted_iota(jnp.int32, (n, n), 0) == lax.broadcasted_iota(jnp.int32, (n, n), 1)).astype(jnp.bfloat16)


def _toeplitz(vec_row, n_keys):
    return pltpu.roll(jnp.broadcast_to(vec_row, (n_keys, BIAS_PERIOD)), 0, 1, stride=1, stride_axis=0)


def _fill_bias_slabs(vec_ref, bias_ref, n_keys, bands):
    n = len(bands)
    kch = _chunk_iota((n_keys, TQ), 0)
    qch = _chunk_iota((n_keys, TQ), 1)

    def head_body(h, carry):
        for j, band in enumerate(bands):
            t = _toeplitz(vec_ref[j, pl.ds(h, 1), :], n_keys)[:, :TQ]
            if band is not None:
                t = jnp.where(band(kch, qch), t, MASKED)
            bias_ref[h * n + j] = t
        return carry

    lax.fori_loop(0, N_HEADS, head_body, 0)


def _live_rows(band, n_keys):
    chunks_per_half = LANES // CHUNK
    out = []
    for half in range(TQ // LANES):
        qchs = range(half * chunks_per_half, (half + 1) * chunks_per_half)
        live = [kc for kc in range(n_keys // CHUNK) if band is None or any(band(kc, qc) for qc in qchs)]
        out.append((live[0] * CHUNK, (live[-1] + 1) * CHUNK) if live else (0, 0))
    return tuple(out)


def _scores_stage(k_tiles, q_tiles, bias_fns, live, s_ref):
    maxes, row = [], 0
    for half in range(TQ // LANES):
        maxes.append([None] * MAX_CHAINS)
    for kt, qz, bias_fn, live_j in zip(k_tiles, q_tiles, bias_fns, live):
        s = jnp.dot(kt, qz, preferred_element_type=jnp.float32)
        for r in range(s.shape[0] // SUBLANES):
            rows = slice(r * SUBLANES, (r + 1) * SUBLANES)
            for half, (r0, r1) in enumerate(live_j):
                if not r0 <= r * SUBLANES < r1:
                    continue
                lanes = slice(half * LANES, (half + 1) * LANES)
                acc = maxes[half]
                grp = s[rows, lanes] if bias_fn is None else s[rows, lanes] + bias_fn(rows, lanes)
                s_ref[row + r * SUBLANES:row + (r + 1) * SUBLANES, lanes] = grp
                c = r % MAX_CHAINS
                acc[c] = grp if acc[c] is None else jnp.maximum(acc[c], grp)
        row += s.shape[0]
    cols = [jnp.max(functools.reduce(jnp.maximum, [a for a in acc if a is not None]), axis=0, keepdims=True)
            for acc in maxes]
    return jnp.concatenate(cols, axis=1)


def _pv_stage(s_ref, m, v_tiles, live, extra_logit=None):
    if extra_logit is not None:
        m = jnp.maximum(m, extra_logit)
    acc, row = None, 0
    for vt, live_j in zip(v_tiles, live):
        n = vt.shape[1]
        halves = []
        for half, (r0, r1) in enumerate(live_j):
            lanes = slice(half * LANES, (half + 1) * LANES)
            parts = [jnp.zeros((r0, LANES), jnp.bfloat16)] if r0 else []
            if r1 > r0:
                parts.append(jnp.exp2(s_ref[row + r0:row + r1, lanes] - m[:, lanes]).astype(jnp.bfloat16))
            if n > r1:
                parts.append(jnp.zeros((n - r1, LANES), jnp.bfloat16))
            halves.append(parts[0] if len(parts) == 1 else jnp.concatenate(parts, axis=0))
        p = jnp.concatenate(halves, axis=1)
        v_ones = jnp.concatenate([vt, jnp.ones((BF16_ROWS, n), vt.dtype)], axis=0)
        part = jnp.dot(v_ones, p, preferred_element_type=jnp.float32)
        acc = part if acc is None else acc + part
        row += n
    l = acc[HEAD_DIM:HEAD_DIM + 1, :]
    if extra_logit is not None:
        l = l + jnp.exp2(extra_logit - m)
    return acc[:HEAD_DIM, :] / l


def _head_pipeline(n_tiles, scores_fn, pv_fn, s_refs, scores_first, per_pair=2):
    per_group = len(s_refs) // 2
    x_refs, y_refs = s_refs[:per_group], s_refs[per_group:]
    n_groups = n_tiles // per_group

    m = [scores_fn(t // per_pair, t % per_pair, ref) for t, ref in zip(range(per_group), x_refs)]
    for g in range(n_groups):
        src, dst = (x_refs, y_refs) if g % 2 == 0 else (y_refs, x_refs)
        nxt = [(g + 1) * per_group + j for j in range(per_group)] if g + 1 < n_groups else []
        m_next = [scores_fn(t // per_pair, t % per_pair, ref) for t, ref in zip(nxt, dst)] if scores_first else []
        for j in range(per_group):
            if not scores_first and nxt:
                m_next.append(scores_fn(nxt[j] // per_pair, nxt[j] % per_pair, dst[j]))
            t = g * per_group + j
            pv_fn(t // per_pair, t % per_pair, src[j], m[j])
        m = m_next


A_BANDS = (lambda kc, qc: kc >= qc,
           None,
           lambda kc, qc: kc <= qc)
A_LIVE = tuple(_live_rows(band, TQ) for band in A_BANDS)
B_WIN = B_PREV + LANES
B_SUB = PAIR_W
B_BAND = lambda kc, qc: (kc >= qc) & (kc <= qc + B_LEFT_CHUNKS)
B_LIVE = (((0, B_WIN),) * (TQ // LANES),)


def _out_and_residual(zt_ref, wout_ref, res):
    y = lax.dot_general(zt_ref[...], wout_ref[...], _TN, preferred_element_type=jnp.float32)
    return res + y


def _attn_a_kernel(x_ref, qt_ref, kp_ref, ko_ref, vp_ref, vo_ref, gt_ref,
                   bvec_ref, wout_ref, kvwk_ref, nwt_ref,
                   h_ref, ksh_ref, vsht_ref, qbt_ref, gbt_ref, zt_ref, bias_ref, *s_refs):
    i = pl.program_id(1)
    subs = TQA // TQ

    @pl.when((pl.program_id(0) == 0) & (i == 0))
    def _():
        _fill_bias_slabs(bvec_ref, bias_ref, TQ, A_BANDS)
        bias_ref[N_HEADS * A_KBLOCKS] = jnp.full((TQ, TQ), MASKED, jnp.float32)

    n_slabs = N_HEADS * A_KBLOCKS
    rows = lax.broadcasted_iota(jnp.int32, (PAIR_W, TQ), 0)

    def key_block(sb, j):
        n = sb + j
        return n // subs, (n % subs) * TQ

    def scores_fn(sp, hh, s_ref):
        sb, p = divmod(sp, N_PAIRS)
        q2 = qt_ref[0, p, :, sb * TQ:(sb + 1) * TQ]
        own_rows = rows < HEAD_DIM if hh == 0 else rows >= HEAD_DIM
        qz = jnp.where(own_rows, q2, jnp.zeros_like(q2))
        k_tiles, bias_fns = [], []
        for j in range(A_KBLOCKS):
            in_seq = i * subs + sb - (A_KBLOCKS - 1) + j >= 0
            slab = jnp.where(in_seq, (2 * p + hh) * A_KBLOCKS + j, n_slabs)
            own, t0 = key_block(sb, j)
            k_tiles.append((ko_ref if own else kp_ref)[0, p, t0:t0 + TQ, :])
            bias_fns.append(lambda rws, lanes, slab=slab: bias_ref[slab, rws, lanes])
        return _scores_stage(k_tiles, [qz] * A_KBLOCKS, bias_fns, A_LIVE, s_ref)

    def pv_fn(sp, hh, s_ref, m):
        sb, p = divmod(sp, N_PAIRS)
        half = slice(hh * HEAD_DIM, (hh + 1) * HEAD_DIM)
        qcols = slice(sb * TQ, (sb + 1) * TQ)
        v_tiles = []
        for j in range(A_KBLOCKS):
            own, t0 = key_block(sb, j)
            v_tiles.append((vo_ref if own else vp_ref)[0, p, half, t0:t0 + TQ])
        o = _pv_stage(s_ref, m, v_tiles, A_LIVE)
        gate = gt_ref[0, p, half, qcols].astype(jnp.float32)
        row0 = p * PAIR_W + hh * HEAD_DIM
        zt_ref[row0:row0 + HEAD_DIM, qcols] = (o * gate).astype(jnp.bfloat16)

    _head_pipeline(subs * N_HEADS, scores_fn, pv_fn, s_refs, scores_first=True)

    h = _out_and_residual(zt_ref, wout_ref, x_ref[0])
    h_ref[0] = h
    hb = h.astype(jnp.bfloat16)
    r_col, r_row = _token_scales(h)
    ksh_ref[0] = (jnp.dot(hb, kvwk_ref[...], preferred_element_type=jnp.float32) * r_col).astype(jnp.bfloat16)
    for out_ref, base, act in ((gbt_ref, W_T_COLS_B + D_MODEL, _silu), (qbt_ref, W_T_COLS_B, None)):
        for c in range(0, D_MODEL, FEAT_CHUNK):
            first = out_ref is qbt_ref and c == 0
            start = base + c - (PAIR_W if first else 0)
            yt = lax.dot_general(nwt_ref[start:base + c + FEAT_CHUNK, :], hb, _NT,
                                 preferred_element_type=jnp.float32) * r_row
            if first:
                vsht_ref[0] = yt[:PAIR_W].astype(jnp.bfloat16)
                yt = yt[PAIR_W:]
            out_ref[0, c:c + FEAT_CHUNK, :] = (yt if act is None else act(yt)).astype(jnp.bfloat16)


def _const_spec(shape):
    return pl.BlockSpec(shape, lambda bi, i: (0,) * len(shape), pipeline_mode=pl.Buffered(1))


def _attn_a(x, qt, k, vt, gt, bvec, wout, kvwk, nwt):
    b, s, d = x.shape
    qt4 = qt.reshape(b, N_PAIRS, PAIR_W, s)
    vt4 = vt.reshape(b, N_PAIRS, PAIR_W, s)
    gt4 = gt.reshape(b, N_PAIRS, PAIR_W, s)
    assert (A_KBLOCKS - 1) * TQ <= TQA, "the keys before a grid block must fit in one previous block"
    prev = lambda i: jnp.maximum(i - 1, 0)

    def kspec(index):
        return pl.BlockSpec((1, N_PAIRS, TQA, PAIR_W), lambda bi, i: (bi, 0, index(i), 0))

    def vspec(index):
        return pl.BlockSpec((1, N_PAIRS, PAIR_W, TQA), lambda bi, i: (bi, 0, 0, index(i)))

    feat_spec = pl.BlockSpec((1, N_PAIRS, PAIR_W, TQA), lambda bi, i: (bi, 0, 0, i))
    featout_spec = pl.BlockSpec((1, d, TQA), lambda bi, i: (bi, 0, i))
    feat = jax.ShapeDtypeStruct((b, d, s), jnp.bfloat16)
    return pl.pallas_call(
        _attn_a_kernel,
        grid=(b, s // TQA),
        in_specs=[
            pl.BlockSpec((1, TQA, d), lambda bi, i: (bi, i, 0)),
            feat_spec,
            kspec(prev), kspec(lambda i: i),
            vspec(prev), vspec(lambda i: i),
            feat_spec,
            _const_spec(bvec.shape),
            _const_spec(wout.shape),
            _const_spec(kvwk.shape),
            _const_spec(nwt.shape),
        ],
        out_specs=[
            pl.BlockSpec((1, TQA, d), lambda bi, i: (bi, i, 0)),
            pl.BlockSpec((1, TQA, PAIR_W), lambda bi, i: (bi, i, 0)),
            pl.BlockSpec((1, PAIR_W, TQA), lambda bi, i: (bi, 0, i)),
            featout_spec,
            featout_spec,
        ],
        out_shape=[
            jax.ShapeDtypeStruct((b, s, d), jnp.float32),
            jax.ShapeDtypeStruct((b, s, PAIR_W), jnp.bfloat16),
            jax.ShapeDtypeStruct((b, PAIR_W, s), jnp.bfloat16),
            feat, feat,
        ],
        scratch_shapes=[pltpu.VMEM((d, TQA), jnp.bfloat16),
                        pltpu.VMEM((N_HEADS * A_KBLOCKS + 1, TQ, TQ), jnp.float32),
                        *[pltpu.VMEM((A_KBLOCKS * TQ, TQ), jnp.float32)] * N_SCORE_BUFS],
        compiler_params=pltpu.CompilerParams(
            dimension_semantics=("arbitrary", "arbitrary"), vmem_limit_bytes=VMEM_LIMIT),
        name="attn_a",
    )(x, qt4, k, k, vt4, vt4, gt4, bvec, wout, kvwk, nwt)


def _attn_b_kernel(h_ref, qt_ref, kp_ref, ko_ref, vp_ref, vo_ref, gt_ref, vec_ref,
                   sink_ref, wout_ref, fg_ref, out_ref, zt_ref, bias_ref, *s_refs):
    i = pl.program_id(1)

    @pl.when((pl.program_id(0) == 0) & (i == 0))
    def _():
        in_band = B_BAND(_chunk_iota((B_WIN, LANES), 0), _chunk_iota((B_WIN, LANES), 1))
        before_block = lax.broadcasted_iota(jnp.int32, (B_WIN, LANES), 0) < B_PREV

        def pair_body(p, carry):
            for hh in range(2):
                t = jnp.where(in_band, _toeplitz(vec_ref[0, pl.ds(2 * p + hh, 1), :], B_WIN)[:, :LANES], MASKED)
                lanes = slice(hh * LANES, (hh + 1) * LANES)
                bias_ref[2 * p, :, lanes] = t.astype(jnp.bfloat16)
                bias_ref[2 * p + 1, :, lanes] = jnp.where(before_block, MASKED, t).astype(jnp.bfloat16)
            return carry

        lax.fori_loop(0, N_PAIRS, pair_body, 0)

    rows = lax.broadcasted_iota(jnp.int32, (PAIR_W, TQ), 0)
    pairs_per_kv = N_PAIRS // B_KV_HEADS
    eye = _identity_bf16(B_SUB)
    head_lanes = lax.broadcasted_iota(jnp.int32, (1, TQ), 1)

    def scores_fn(p, u, s_ref):
        qlanes = slice(u * LANES, (u + 1) * LANES)
        q2 = jnp.concatenate([qt_ref[0, p, :HEAD_DIM, qlanes], qt_ref[0, p, HEAD_DIM:, qlanes]], axis=1)
        kv0 = HEAD_DIM * (p // pairs_per_kv)
        kv_rows = (rows >= kv0) & (rows < kv0 + HEAD_DIM)
        qz = jnp.where(kv_rows, jnp.concatenate([q2, q2], axis=0), jnp.zeros((PAIR_W, TQ), q2.dtype))
        if u == 0:
            k_tile = jnp.concatenate([kp_ref[0], ko_ref[0, :LANES, :]], axis=0)
            variant = jnp.where(i == 0, 1, 0)
        else:
            k_tile, variant = ko_ref[0, u * LANES - B_PREV:(u + 1) * LANES, :], 0

        lhs, rhs = [], []
        for r0 in range(0, B_WIN, B_SUB):
            lhs.append(jnp.concatenate([k_tile[r0:r0 + B_SUB, :], eye], axis=1))
            rhs.append(jnp.concatenate([qz, bias_ref[2 * p + variant, r0:r0 + B_SUB, :]], axis=0))
        sub_live = (((0, B_SUB),) * (TQ // LANES),) * len(lhs)
        return _scores_stage(lhs, rhs, [None] * len(lhs), sub_live, s_ref)

    def pv_fn(p, u, s_ref, m):
        qlanes = slice(u * LANES, (u + 1) * LANES)
        kv_rows = slice(HEAD_DIM * (p // pairs_per_kv), HEAD_DIM * (p // pairs_per_kv + 1))
        if u == 0:
            v_tile = jnp.concatenate([vp_ref[0, kv_rows, :], vo_ref[0, kv_rows, :LANES]], axis=1)
        else:
            v_tile = vo_ref[0, kv_rows, u * LANES - B_PREV:(u + 1) * LANES]
        sink = jnp.where(head_lanes < LANES, sink_ref[0, 2 * p], sink_ref[0, 2 * p + 1]) * LOG2E
        o = _pv_stage(s_ref, m, [v_tile], B_LIVE, extra_logit=sink)
        for hh in range(2):
            hrows = slice(hh * HEAD_DIM, (hh + 1) * HEAD_DIM)
            gate = gt_ref[0, p, hrows, qlanes].astype(jnp.float32)
            row0 = p * PAIR_W + hh * HEAD_DIM
            zt_ref[row0:row0 + HEAD_DIM, qlanes] = (o[:, hh * LANES:(hh + 1) * LANES] * gate).astype(jnp.bfloat16)

    _head_pipeline(N_PAIRS * (TQB // LANES), scores_fn, pv_fn, s_refs, scores_first=False, per_pair=TQB // LANES)

    h2 = _out_and_residual(zt_ref, wout_ref, h_ref[0])
    out_ref[0] = (h2 * fg_ref[...]) * _rms_scale(h2)


def _attn_b(h, qbt, ksh, vsht, gbt, vec, sinks, wout, fg):
    b, s, d = h.shape
    qt4 = qbt.reshape(b, N_PAIRS, PAIR_W, s)
    gt4 = gbt.reshape(b, N_PAIRS, PAIR_W, s)
    feat_spec = pl.BlockSpec((1, N_PAIRS, PAIR_W, TQB), lambda bi, i: (bi, 0, 0, i))
    prev = lambda i: jnp.maximum(i * (TQB // B_PREV) - 1, 0)
    return pl.pallas_call(
        _attn_b_kernel,
        grid=(b, s // TQB),
        in_specs=[
            pl.BlockSpec((1, TQB, d), lambda bi, i: (bi, i, 0)),
            feat_spec,
            pl.BlockSpec((1, B_PREV, PAIR_W), lambda bi, i: (bi, prev(i), 0)),
            pl.BlockSpec((1, TQB, PAIR_W), lambda bi, i: (bi, i, 0)),
            pl.BlockSpec((1, PAIR_W, B_PREV), lambda bi, i: (bi, 0, prev(i))),
            pl.BlockSpec((1, PAIR_W, TQB), lambda bi, i: (bi, 0, i)),
            feat_spec,
            _const_spec(vec.shape),
            pl.BlockSpec(memory_space=pltpu.SMEM),
            _const_spec(wout.shape),
            _const_spec(fg.shape),
        ],
        out_specs=pl.BlockSpec((1, TQB, d), lambda bi, i: (bi, i, 0)),
        out_shape=jax.ShapeDtypeStruct((b, s, d), jnp.float32),
        scratch_shapes=[pltpu.VMEM((d, TQB), jnp.bfloat16),
                        pltpu.VMEM((2 * N_PAIRS, B_WIN, TQ), jnp.bfloat16),
                        *[pltpu.VMEM((B_WIN, TQ), jnp.float32)] * N_SCORE_BUFS_B],
        compiler_params=pltpu.CompilerParams(
            dimension_semantics=("arbitrary", "arbitrary"), vmem_limit_bytes=VMEM_LIMIT),
        name="attn_b",
    )(h, qt4, ksh, ksh, vsht, vsht, gt4, vec, sinks, wout, fg)


def _slab_diff(xp=jnp):
    u = xp.arange(BIAS_PERIOD)
    return xp.where(u < TQ, u, u - BIAS_PERIOD)


def _static_take(table, idx):
    parts, lo = [], 0
    while lo < len(idx):
        hi = lo + 1
        step = idx[hi] - idx[lo] if hi < len(idx) else 0
        while step in (0, 1) and hi < len(idx) and idx[hi] - idx[hi - 1] == step:
            hi += 1
        rows = table[idx[lo]:idx[hi - 1] + 1]
        parts.append(jnp.broadcast_to(rows, (hi - lo,) + table.shape[1:]) if step == 0 else rows)
        lo = hi
    return jnp.concatenate(parts, axis=0)


def _bias_vecs_a(rel_bias):
    back = (A_KBLOCKS - 1 - np.arange(A_KBLOCKS))[:, None] * TQ
    dist = back + _slab_diff(np)[None, :]
    idx = np.clip(dist, -A_REL_CLIP, A_REL_CLIP) + A_REL_CLIP
    vec = jnp.stack([_static_take(rel_bias, [int(v) for v in row]) for row in idx])
    return jnp.transpose(vec, (0, 2, 1)).astype(jnp.float32)


def _t5_bucket(rel):
    nb = T5_BUCKETS // 2
    max_exact = nb // 2
    ret = jnp.where(rel > 0, nb, 0)
    n = jnp.abs(rel)
    nf = jnp.maximum(n, 1).astype(jnp.float32)
    large = max_exact + (jnp.log(nf / max_exact) / math.log(T5_MAX_DIST / max_exact)
                         * (nb - max_exact)).astype(jnp.int32)
    large = jnp.minimum(large, nb - 1)
    return ret + jnp.where(n < max_exact, n, large)


def _bias_vec_b(t5_table):
    rel = -_slab_diff() - B_PREV
    vec = jnp.transpose(t5_table[_t5_bucket(rel)], (1, 0)).astype(jnp.float32)
    return vec[None]


def kernel(x, a_norm, a_w_in, a_rel_bias, a_w_out, kv_norm, kv_w, t5_bias,
           b_norm, b_w_in, b_sinks, b_w_out, final_norm):
    assert a_norm.shape[0] == 1 and b_norm.shape[0] == 1, "one A layer then one B layer"
    scale = HEAD_DIM ** -0.5 * LOG2E

    def q_col_scale(n):
        return np.where(np.arange(n) < D_MODEL, scale, 1.0).astype(np.float32)

    wt_a = _weight_t(a_w_in[0], q_col_scale(a_w_in.shape[2]), a_norm[0], W_T_COLS_A)
    qt, k, vt, gt, nwt, kvw_k, wo_a, wo_b = _proj_a(x, wt_a, kv_w, b_w_in[0], q_col_scale(b_w_in.shape[2]),
                                                    kv_norm, b_norm[0], a_w_out, b_w_out)
    h, ksh, vsht, qbt, gbt = _attn_a(
        x, qt, k, vt, gt, _bias_vecs_a(a_rel_bias[0] * LOG2E), wo_a, kvw_k, nwt)

    return _attn_b(h, qbt, ksh, vsht, gbt, _bias_vec_b(t5_bias * LOG2E), b_sinks.astype(jnp.float32), wo_b,
                   final_norm[None, :])
```

```python
import functools
import math

import jax
import jax.numpy as jnp
import numpy as np
from jax import lax
from jax.experimental import pallas as pl
from jax.experimental.pallas import tpu as pltpu

D_MODEL = 1024
HEAD_DIM = 64
N_HEADS = D_MODEL // HEAD_DIM
N_PAIRS = N_HEADS // 2
PAIR_W = 2 * HEAD_DIM
CHUNK = 64
RMS_EPS = 1e-6
A_LEFT_CHUNKS = 8
A_REL_CLIP = 256
B_KV_HEADS = 2
B_LEFT_CHUNKS = 2
T5_BUCKETS = 32
T5_MAX_DIST = 128

TQ = 256
TQB = 512
TQA = 512
A_KBLOCKS = A_LEFT_CHUNKS * CHUNK // TQ + 1
B_PREV = B_LEFT_CHUNKS * CHUNK
TM_PROJ = 1024
FEAT_CHUNK = 256
W_T_COLS_A = 1024
W_T_COLS_B = 512
MASKED = -1e30
BIAS_PERIOD = 2 * TQ
SUBLANES = 8
LANES = 128
MAX_CHAINS = 1
BF16_ROWS = 16
LOG2E = math.log2(math.e)
N_SCORE_BUFS = 4
N_SCORE_BUFS_B = 8
VMEM_LIMIT = 56 * 1024 * 1024

_NT = (((1,), (1,)), ((), ()))
_TN = (((0,), (0,)), ((), ()))


def _rms_scale(xf):
    return lax.rsqrt(jnp.mean(xf * xf, axis=-1, keepdims=True) + RMS_EPS)


def _token_scales(xf):
    col = _rms_scale(xf)
    row = jnp.transpose(jnp.broadcast_to(col, (xf.shape[0], LANES)))[0:1, :]
    return col, row


def _silu(v):
    return v * jax.nn.sigmoid(v)


def _weight_t_kernel(w_ref, cs_ref, gain_ref, o_ref):
    o_ref[...] = ((w_ref[...] * cs_ref[...]).T * gain_ref[...]).astype(jnp.bfloat16)


def _weight_t(w, col_scale, row_gain, tc):
    d, n = w.shape
    return pl.pallas_call(
        _weight_t_kernel,
        grid=(n // tc,),
        in_specs=[
            pl.BlockSpec((d, tc), lambda j: (0, j)),
            pl.BlockSpec((1, tc), lambda j: (0, j)),
            pl.BlockSpec((1, d), lambda j: (0, 0)),
        ],
        out_specs=pl.BlockSpec((tc, d), lambda j: (j, 0)),
        out_shape=jax.ShapeDtypeStruct((n, d), jnp.bfloat16),
        compiler_params=pltpu.CompilerParams(dimension_semantics=("arbitrary",), vmem_limit_bytes=VMEM_LIMIT),
        name="weight_t",
    )(w, col_scale[None, :], row_gain[None, :])


def _proj_a_kernel(x_ref, wt_ref, kvw_ref, kvg_ref, bw_ref, cs_ref, bg_ref, woa_ref, wob_ref,
                   qt_ref, k_ref, vt_ref, gt_ref, nwt_ref, kvwk_ref, woa16_ref, wob16_ref):
    step = pl.program_id(0) * pl.num_programs(1) + pl.program_id(1)
    head = jnp.concatenate([jnp.zeros((D_MODEL, W_T_COLS_B - PAIR_W), jnp.float32), kvw_ref[:, PAIR_W:]], axis=1)
    src = jnp.where(step == 0, head, bw_ref[...])
    gain = jnp.where(step == 0, kvg_ref[...], bg_ref[...])
    nwt_ref[...] = ((src * cs_ref[...]).T * gain).astype(jnp.bfloat16)
    g_rows = jnp.transpose(jnp.broadcast_to(kvg_ref[...], (PAIR_W, D_MODEL)))
    kvwk_ref[...] = (kvw_ref[:, :PAIR_W] * g_rows).astype(jnp.bfloat16)
    woa16_ref[...] = woa_ref[0].astype(jnp.bfloat16)
    wob16_ref[...] = wob_ref[0].astype(jnp.bfloat16)

    xf = x_ref[0]
    xb = xf.astype(jnp.bfloat16)
    r_col, r_row = _token_scales(xf)
    k = lax.dot_general(xb, wt_ref[D_MODEL:2 * D_MODEL, :], _NT,
                        preferred_element_type=jnp.float32) * r_col
    for p in range(N_PAIRS):
        k_ref[0, p] = k[:, p * PAIR_W:(p + 1) * PAIR_W].astype(jnp.bfloat16)
    for out_ref, base, act in ((gt_ref, 3 * D_MODEL, _silu), (vt_ref, 2 * D_MODEL, None), (qt_ref, 0, None)):
        for c in range(0, D_MODEL, FEAT_CHUNK):
            w = wt_ref[base + c:base + c + FEAT_CHUNK, :]
            yt = lax.dot_general(w, xb, _NT, preferred_element_type=jnp.float32) * r_row
            out_ref[0, c:c + FEAT_CHUNK, :] = (yt if act is None else act(yt)).astype(jnp.bfloat16)


def _proj_a(x, wt, kv_w, b_w, b_col_scale, kv_gain, b_gain, a_w_out, b_w_out):
    b, s, d = x.shape
    n_i = s // TM_PROJ
    wo_rows = d // (b * n_i)
    step = lambda bi, i: bi * n_i + i
    n_blocks = 1 + b_w.shape[1] // W_T_COLS_B
    assert n_blocks <= b * n_i and kv_w.shape == (d, 2 * PAIR_W) and d % (b * n_i * BF16_ROWS) == 0
    blk = lambda bi, i: jnp.minimum(bi * n_i + i, n_blocks - 1)
    cs_all = np.concatenate([np.ones((W_T_COLS_B,), np.float32), b_col_scale])[None, :]
    feat = jax.ShapeDtypeStruct((b, d, s), jnp.bfloat16)
    return pl.pallas_call(
        _proj_a_kernel,
        grid=(b, n_i),
        in_specs=[
            pl.BlockSpec((1, TM_PROJ, d), lambda bi, i: (bi, i, 0)),
            _const_spec(wt.shape),
            _const_spec(kv_w.shape),
            _const_spec((1, d)),
            pl.BlockSpec((d, W_T_COLS_B), lambda bi, i: (0, jnp.maximum(blk(bi, i) - 1, 0))),
            pl.BlockSpec((1, W_T_COLS_B), lambda bi, i: (0, blk(bi, i))),
            _const_spec((1, d)),
            pl.BlockSpec((1, wo_rows, d), lambda bi, i: (0, step(bi, i), 0)),
            pl.BlockSpec((1, wo_rows, d), lambda bi, i: (0, step(bi, i), 0)),
        ],
        out_specs=[
            pl.BlockSpec((1, d, TM_PROJ), lambda bi, i: (bi, 0, i)),
            pl.BlockSpec((1, N_PAIRS, TM_PROJ, PAIR_W), lambda bi, i: (bi, 0, i, 0)),
            pl.BlockSpec((1, d, TM_PROJ), lambda bi, i: (bi, 0, i)),
            pl.BlockSpec((1, d, TM_PROJ), lambda bi, i: (bi, 0, i)),
            pl.BlockSpec((W_T_COLS_B, d), lambda bi, i: (blk(bi, i), 0)),
            pl.BlockSpec((d, PAIR_W), lambda bi, i: (0, 0)),
            pl.BlockSpec((wo_rows, d), lambda bi, i: (step(bi, i), 0)),
            pl.BlockSpec((wo_rows, d), lambda bi, i: (step(bi, i), 0)),
        ],
        out_shape=[feat, jax.ShapeDtypeStruct((b, N_PAIRS, s, PAIR_W), jnp.bfloat16), feat, feat,
                   jax.ShapeDtypeStruct((n_blocks * W_T_COLS_B, d), jnp.bfloat16),
                   jax.ShapeDtypeStruct((d, PAIR_W), jnp.bfloat16),
                   jax.ShapeDtypeStruct((d, d), jnp.bfloat16), jax.ShapeDtypeStruct((d, d), jnp.bfloat16)],
        compiler_params=pltpu.CompilerParams(
            dimension_semantics=("arbitrary", "arbitrary"), vmem_limit_bytes=VMEM_LIMIT),
        name="proj_a",
    )(x, wt, kv_w, kv_gain[None, :], b_w, cs_all, b_gain[None, :], a_w_out, b_w_out)


def _chunk_iota(shape, axis):
    return lax.shift_right_logical(lax.broadcasted_iota(jnp.int32, shape, axis), CHUNK.bit_length() - 1)


def _identity_bf16(n):
    return (lax.broadcasted_iota(jnp.int32, (n, n), 0) == lax.broadcasted_iota(jnp.int32, (n, n), 1)).astype(jnp.bfloat16)


def _toeplitz(vec_row, n_keys):
    return pltpu.roll(jnp.broadcast_to(vec_row, (n_keys, BIAS_PERIOD)), 0, 1, stride=1, stride_axis=0)


def _fill_bias_slabs(vec_ref, bias_ref, n_keys, bands, flat):
    n = len(bands)
    kch = _chunk_iota((n_keys, TQ), 0)
    qch = _chunk_iota((n_keys, TQ), 1)

    def head_body(h, carry):
        for j, band in enumerate(bands):
            row = vec_ref[j, pl.ds(h, 1), :]
            t = (jnp.broadcast_to(row, (n_keys, BIAS_PERIOD)) if flat[j] else _toeplitz(row, n_keys))[:, :TQ]
            if band is not None:
                t = jnp.where(band(kch, qch), t, MASKED)
            bias_ref[h * n + j] = t
        return carry

    lax.fori_loop(0, N_HEADS, head_body, 0)


def _live_rows(band, n_keys):
    chunks_per_half = LANES // CHUNK
    out = []
    for half in range(TQ // LANES):
        qchs = range(half * chunks_per_half, (half + 1) * chunks_per_half)
        live = [kc for kc in range(n_keys // CHUNK) if band is None or any(band(kc, qc) for qc in qchs)]
        out.append((live[0] * CHUNK, (live[-1] + 1) * CHUNK) if live else (0, 0))
    return tuple(out)


def _scores_stage(k_tiles, q_tiles, bias_fns, live, s_ref):
    maxes, row = [], 0
    for half in range(TQ // LANES):
        maxes.append([None] * MAX_CHAINS)
    for kt, qz, bias_fn, live_j in zip(k_tiles, q_tiles, bias_fns, live):
        s = jnp.dot(kt, qz, preferred_element_type=jnp.float32)
        for r in range(s.shape[0] // SUBLANES):
            rows = slice(r * SUBLANES, (r + 1) * SUBLANES)
            for half, (r0, r1) in enumerate(live_j):
                if not r0 <= r * SUBLANES < r1:
                    continue
                lanes = slice(half * LANES, (half + 1) * LANES)
                acc = maxes[half]
                grp = s[rows, lanes] if bias_fn is None else s[rows, lanes] + bias_fn(rows, lanes)
                s_ref[row + r * SUBLANES:row + (r + 1) * SUBLANES, lanes] = grp
                c = r % MAX_CHAINS
                acc[c] = grp if acc[c] is None else jnp.maximum(acc[c], grp)
        row += s.shape[0]
    cols = [jnp.max(functools.reduce(jnp.maximum, [a for a in acc if a is not None]), axis=0, keepdims=True)
            for acc in maxes]
    return jnp.concatenate(cols, axis=1)


def _pv_stage(s_ref, m, v_tiles, live, extra_logit=None):
    if extra_logit is not None:
        m = jnp.maximum(m, extra_logit)
    acc, row = None, 0
    for vt, live_j in zip(v_tiles, live):
        n = vt.shape[1]
        halves = []
        for half, (r0, r1) in enumerate(live_j):
            lanes = slice(half * LANES, (half + 1) * LANES)
            parts = [jnp.zeros((r0, LANES), jnp.bfloat16)] if r0 else []
            if r1 > r0:
                parts.append(jnp.exp2(s_ref[row + r0:row + r1, lanes] - m[:, lanes]).astype(jnp.bfloat16))
            if n > r1:
                parts.append(jnp.zeros((n - r1, LANES), jnp.bfloat16))
            halves.append(parts[0] if len(parts) == 1 else jnp.concatenate(parts, axis=0))
        p = jnp.concatenate(halves, axis=1)
        v_ones = jnp.concatenate([vt, jnp.ones((BF16_ROWS, n), vt.dtype)], axis=0)
        part = jnp.dot(v_ones, p, preferred_element_type=jnp.float32)
        acc = part if acc is None else acc + part
        row += n
    l = acc[HEAD_DIM:HEAD_DIM + 1, :]
    if extra_logit is not None:
        l = l + jnp.exp2(extra_logit - m)
    return acc[:HEAD_DIM, :] / l


def _head_pipeline(n_tiles, scores_fn, pv_fn, s_refs, scores_first, per_pair=2):
    per_group = len(s_refs) // 2
    x_refs, y_refs = s_refs[:per_group], s_refs[per_group:]
    n_groups = n_tiles // per_group

    m = [scores_fn(t // per_pair, t % per_pair, ref) for t, ref in zip(range(per_group), x_refs)]
    for g in range(n_groups):
        src, dst = (x_refs, y_refs) if g % 2 == 0 else (y_refs, x_refs)
        nxt = [(g + 1) * per_group + j for j in range(per_group)] if g + 1 < n_groups else []
        m_next = [scores_fn(t // per_pair, t % per_pair, ref) for t, ref in zip(nxt, dst)] if scores_first else []
        for j in range(per_group):
            if not scores_first and nxt:
                m_next.append(scores_fn(nxt[j] // per_pair, nxt[j] % per_pair, dst[j]))
            t = g * per_group + j
            pv_fn(t // per_pair, t % per_pair, src[j], m[j])
        m = m_next


A_BANDS = (lambda kc, qc: kc >= qc,
           None,
           lambda kc, qc: kc <= qc)
A_LIVE = tuple(_live_rows(band, TQ) for band in A_BANDS)
A_FLAT = tuple((A_KBLOCKS - 1 - j) * TQ - (TQ - 1) >= A_REL_CLIP for j in range(A_KBLOCKS))
B_WIN = B_PREV + LANES
B_SUB = PAIR_W
B_BAND = lambda kc, qc: (kc >= qc) & (kc <= qc + B_LEFT_CHUNKS)
B_LIVE = (((0, B_WIN),) * (TQ // LANES),)


def _out_and_residual(zt_ref, wout_ref, res):
    y = lax.dot_general(zt_ref[...], wout_ref[...], _TN, preferred_element_type=jnp.float32)
    return res + y


def _attn_a_kernel(x_ref, qt_ref, kp_ref, ko_ref, vp_ref, vo_ref, gt_ref,
                   bvec_ref, wout_ref, kvwk_ref, nwt_ref,
                   h_ref, ksh_ref, vsht_ref, qbt_ref, gbt_ref, zt_ref, bias_ref, *s_refs):
    i = pl.program_id(1)
    subs = TQA // TQ

    @pl.when((pl.program_id(0) == 0) & (i == 0))
    def _():
        _fill_bias_slabs(bvec_ref, bias_ref, TQ, A_BANDS, A_FLAT)
        bias_ref[N_HEADS * A_KBLOCKS] = jnp.full((TQ, TQ), MASKED, jnp.float32)

    n_slabs = N_HEADS * A_KBLOCKS
    rows = lax.broadcasted_iota(jnp.int32, (PAIR_W, TQ), 0)

    def key_block(sb, j):
        n = sb + j
        return n // subs, (n % subs) * TQ

    def scores_fn(sp, hh, s_ref):
        sb, p = divmod(sp, N_PAIRS)
        q2 = qt_ref[0, p, :, sb * TQ:(sb + 1) * TQ]
        own_rows = rows < HEAD_DIM if hh == 0 else rows >= HEAD_DIM
        qz = jnp.where(own_rows, q2, jnp.zeros_like(q2))
        k_tiles, bias_fns = [], []
        for j in range(A_KBLOCKS):
            in_seq = i * subs + sb - (A_KBLOCKS - 1) + j >= 0
            slab = jnp.where(in_seq, (2 * p + hh) * A_KBLOCKS + j, n_slabs)
            own, t0 = key_block(sb, j)
            k_tiles.append((ko_ref if own else kp_ref)[0, p, t0:t0 + TQ, :])
            bias_fns.append(lambda rws, lanes, slab=slab: bias_ref[slab, rws, lanes])
        return _scores_stage(k_tiles, [qz] * A_KBLOCKS, bias_fns, A_LIVE, s_ref)

    def pv_fn(sp, hh, s_ref, m):
        sb, p = divmod(sp, N_PAIRS)
        half = slice(hh * HEAD_DIM, (hh + 1) * HEAD_DIM)
        qcols = slice(sb * TQ, (sb + 1) * TQ)
        v_tiles = []
        for j in range(A_KBLOCKS):
            own, t0 = key_block(sb, j)
            v_tiles.append((vo_ref if own else vp_ref)[0, p, half, t0:t0 + TQ])
        o = _pv_stage(s_ref, m, v_tiles, A_LIVE)
        gate = gt_ref[0, p, half, qcols].astype(jnp.float32)
        row0 = p * PAIR_W + hh * HEAD_DIM
        zt_ref[row0:row0 + HEAD_DIM, qcols] = (o * gate).astype(jnp.bfloat16)

    _head_pipeline(subs * N_HEADS, scores_fn, pv_fn, s_refs, scores_first=True)

    h = _out_and_residual(zt_ref, wout_ref, x_ref[0])
    h_ref[0] = h
    hb = h.astype(jnp.bfloat16)
    r_col, r_row = _token_scales(h)
    ksh_ref[0] = (jnp.dot(hb, kvwk_ref[...], preferred_element_type=jnp.float32) * r_col).astype(jnp.bfloat16)
    for out_ref, base, act in ((gbt_ref, W_T_COLS_B + D_MODEL, _silu), (qbt_ref, W_T_COLS_B, None)):
        for c in range(0, D_MODEL, FEAT_CHUNK):
            first = out_ref is qbt_ref and c == 0
            start = base + c - (PAIR_W if first else 0)
            yt = lax.dot_general(nwt_ref[start:base + c + FEAT_CHUNK, :], hb, _NT,
                                 preferred_element_type=jnp.float32) * r_row
            if first:
                vsht_ref[0] = yt[:PAIR_W].astype(jnp.bfloat16)
                yt = yt[PAIR_W:]
            out_ref[0, c:c + FEAT_CHUNK, :] = (yt if act is None else act(yt)).astype(jnp.bfloat16)


def _const_spec(shape):
    return pl.BlockSpec(shape, lambda bi, i: (0,) * len(shape), pipeline_mode=pl.Buffered(1))


def _attn_a(x, qt, k, vt, gt, bvec, wout, kvwk, nwt):
    b, s, d = x.shape
    qt4 = qt.reshape(b, N_PAIRS, PAIR_W, s)
    vt4 = vt.reshape(b, N_PAIRS, PAIR_W, s)
    gt4 = gt.reshape(b, N_PAIRS, PAIR_W, s)
    assert (A_KBLOCKS - 1) * TQ <= TQA, "the keys before a grid block must fit in one previous block"
    prev = lambda i: jnp.maximum(i - 1, 0)

    def kspec(index):
        return pl.BlockSpec((1, N_PAIRS, TQA, PAIR_W), lambda bi, i: (bi, 0, index(i), 0))

    def vspec(index):
        return pl.BlockSpec((1, N_PAIRS, PAIR_W, TQA), lambda bi, i: (bi, 0, 0, index(i)))

    feat_spec = pl.BlockSpec((1, N_PAIRS, PAIR_W, TQA), lambda bi, i: (bi, 0, 0, i))
    featout_spec = pl.BlockSpec((1, d, TQA), lambda bi, i: (bi, 0, i))
    feat = jax.ShapeDtypeStruct((b, d, s), jnp.bfloat16)
    return pl.pallas_call(
        _attn_a_kernel,
        grid=(b, s // TQA),
        in_specs=[
            pl.BlockSpec((1, TQA, d), lambda bi, i: (bi, i, 0)),
            feat_spec,
            kspec(prev), kspec(lambda i: i),
            vspec(prev), vspec(lambda i: i),
            feat_spec,
            _const_spec(bvec.shape),
            _const_spec(wout.shape),
            _const_spec(kvwk.shape),
            _const_spec(nwt.shape),
        ],
        out_specs=[
            pl.BlockSpec((1, TQA, d), lambda bi, i: (bi, i, 0)),
            pl.BlockSpec((1, TQA, PAIR_W), lambda bi, i: (bi, i, 0)),
            pl.BlockSpec((1, PAIR_W, TQA), lambda bi, i: (bi, 0, i)),
            featout_spec,
            featout_spec,
        ],
        out_shape=[
            jax.ShapeDtypeStruct((b, s, d), jnp.float32),
            jax.ShapeDtypeStruct((b, s, PAIR_W), jnp.bfloat16),
            jax.ShapeDtypeStruct((b, PAIR_W, s), jnp.bfloat16),
            feat, feat,
        ],
        scratch_shapes=[pltpu.VMEM((d, TQA), jnp.bfloat16),
                        pltpu.VMEM((N_HEADS * A_KBLOCKS + 1, TQ, TQ), jnp.float32),
                        *[pltpu.VMEM((A_KBLOCKS * TQ, TQ), jnp.float32)] * N_SCORE_BUFS],
        compiler_params=pltpu.CompilerParams(
            dimension_semantics=("arbitrary", "arbitrary"), vmem_limit_bytes=VMEM_LIMIT),
        name="attn_a",
    )(x, qt4, k, k, vt4, vt4, gt4, bvec, wout, kvwk, nwt)


def _attn_b_kernel(h_ref, qt_ref, kp_ref, ko_ref, vp_ref, vo_ref, gt_ref, vec_ref,
                   sink_ref, wout_ref, fg_ref, out_ref, zt_ref, bias_ref, *s_refs):
    i = pl.program_id(1)

    @pl.when((pl.program_id(0) == 0) & (i == 0))
    def _():
        in_band = B_BAND(_chunk_iota((B_WIN, LANES), 0), _chunk_iota((B_WIN, LANES), 1))
        before_block = lax.broadcasted_iota(jnp.int32, (B_WIN, LANES), 0) < B_PREV

        def pair_body(p, carry):
            for hh in range(2):
                t = jnp.where(in_band, _toeplitz(vec_ref[0, pl.ds(2 * p + hh, 1), :], B_WIN)[:, :LANES], MASKED)
                lanes = slice(hh * LANES, (hh + 1) * LANES)
                bias_ref[2 * p, :, lanes] = t.astype(jnp.bfloat16)
                bias_ref[2 * p + 1, :, lanes] = jnp.where(before_block, MASKED, t).astype(jnp.bfloat16)
            return carry

        lax.fori_loop(0, N_PAIRS, pair_body, 0)

    rows = lax.broadcasted_iota(jnp.int32, (PAIR_W, TQ), 0)
    pairs_per_kv = N_PAIRS // B_KV_HEADS
    eye = _identity_bf16(B_SUB)
    head_lanes = lax.broadcasted_iota(jnp.int32, (1, TQ), 1)

    def scores_fn(p, u, s_ref):
        qlanes = slice(u * LANES, (u + 1) * LANES)
        q2 = jnp.concatenate([qt_ref[0, p, :HEAD_DIM, qlanes], qt_ref[0, p, HEAD_DIM:, qlanes]], axis=1)
        kv0 = HEAD_DIM * (p // pairs_per_kv)
        kv_rows = (rows >= kv0) & (rows < kv0 + HEAD_DIM)
        qz = jnp.where(kv_rows, jnp.concatenate([q2, q2], axis=0), jnp.zeros((PAIR_W, TQ), q2.dtype))
        if u == 0:
            k_tile = jnp.concatenate([kp_ref[0], ko_ref[0, :LANES, :]], axis=0)
            variant = jnp.where(i == 0, 1, 0)
        else:
            k_tile, variant = ko_ref[0, u * LANES - B_PREV:(u + 1) * LANES, :], 0

        lhs, rhs = [], []
        for r0 in range(0, B_WIN, B_SUB):
            lhs.append(jnp.concatenate([k_tile[r0:r0 + B_SUB, :], eye], axis=1))
            rhs.append(jnp.concatenate([qz, bias_ref[2 * p + variant, r0:r0 + B_SUB, :]], axis=0))
        sub_live = (((0, B_SUB),) * (TQ // LANES),) * len(lhs)
        return _scores_stage(lhs, rhs, [None] * len(lhs), sub_live, s_ref)

    def pv_fn(p, u, s_ref, m):
        qlanes = slice(u * LANES, (u + 1) * LANES)
        kv_rows = slice(HEAD_DIM * (p // pairs_per_kv), HEAD_DIM * (p // pairs_per_kv + 1))
        if u == 0:
            v_tile = jnp.concatenate([vp_ref[0, kv_rows, :], vo_ref[0, kv_rows, :LANES]], axis=1)
        else:
            v_tile = vo_ref[0, kv_rows, u * LANES - B_PREV:(u + 1) * LANES]
        sink = jnp.where(head_lanes < LANES, sink_ref[0, 2 * p], sink_ref[0, 2 * p + 1]) * LOG2E
        o = _pv_stage(s_ref, m, [v_tile], B_LIVE, extra_logit=sink)
        for hh in range(2):
            hrows = slice(hh * HEAD_DIM, (hh + 1) * HEAD_DIM)
            gate = gt_ref[0, p, hrows, qlanes].astype(jnp.float32)
            row0 = p * PAIR_W + hh * HEAD_DIM
            zt_ref[row0:row0 + HEAD_DIM, qlanes] = (o[:, hh * LANES:(hh + 1) * LANES] * gate).astype(jnp.bfloat16)

    _head_pipeline(N_PAIRS * (TQB // LANES), scores_fn, pv_fn, s_refs, scores_first=False, per_pair=TQB // LANES)

    h2 = _out_and_residual(zt_ref, wout_ref, h_ref[0])
    out_ref[0] = (h2 * fg_ref[...]) * _rms_scale(h2)


def _attn_b(h, qbt, ksh, vsht, gbt, vec, sinks, wout, fg):
    b, s, d = h.shape
    qt4 = qbt.reshape(b, N_PAIRS, PAIR_W, s)
    gt4 = gbt.reshape(b, N_PAIRS, PAIR_W, s)
    feat_spec = pl.BlockSpec((1, N_PAIRS, PAIR_W, TQB), lambda bi, i: (bi, 0, 0, i))
    prev = lambda i: jnp.maximum(i * (TQB // B_PREV) - 1, 0)
    return pl.pallas_call(
        _attn_b_kernel,
        grid=(b, s // TQB),
        in_specs=[
            pl.BlockSpec((1, TQB, d), lambda bi, i: (bi, i, 0)),
            feat_spec,
            pl.BlockSpec((1, B_PREV, PAIR_W), lambda bi, i: (bi, prev(i), 0)),
            pl.BlockSpec((1, TQB, PAIR_W), lambda bi, i: (bi, i, 0)),
            pl.BlockSpec((1, PAIR_W, B_PREV), lambda bi, i: (bi, 0, prev(i))),
            pl.BlockSpec((1, PAIR_W, TQB), lambda bi, i: (bi, 0, i)),
            feat_spec,
            _const_spec(vec.shape),
            pl.BlockSpec(memory_space=pltpu.SMEM),
            _const_spec(wout.shape),
            _const_spec(fg.shape),
        ],
        out_specs=pl.BlockSpec((1, TQB, d), lambda bi, i: (bi, i, 0)),
        out_shape=jax.ShapeDtypeStruct((b, s, d), jnp.float32),
        scratch_shapes=[pltpu.VMEM((d, TQB), jnp.bfloat16),
                        pltpu.VMEM((2 * N_PAIRS, B_WIN, TQ), jnp.bfloat16),
                        *[pltpu.VMEM((B_WIN, TQ), jnp.float32)] * N_SCORE_BUFS_B],
        compiler_params=pltpu.CompilerParams(
            dimension_semantics=("arbitrary", "arbitrary"), vmem_limit_bytes=VMEM_LIMIT),
        name="attn_b",
    )(h, qt4, ksh, ksh, vsht, vsht, gt4, vec, sinks, wout, fg)


def _slab_diff(xp=jnp):
    u = xp.arange(BIAS_PERIOD)
    return xp.where(u < TQ, u, u - BIAS_PERIOD)


def _static_take(table, idx, axis):
    parts, lo = [], 0
    while lo < len(idx):
        hi = lo + 1
        step = idx[hi] - idx[lo] if hi < len(idx) else 0
        while step in (0, 1) and hi < len(idx) and idx[hi] - idx[hi - 1] == step:
            hi += 1
        rows = lax.slice_in_dim(table, idx[lo], idx[hi - 1] + 1, axis=axis)
        shape = table.shape[:axis] + (hi - lo,) + table.shape[axis + 1:]
        parts.append(jnp.broadcast_to(rows, shape) if step == 0 else rows)
        lo = hi
    return jnp.concatenate(parts, axis=axis)


def _bias_vecs_a(rel_bias):
    back = (A_KBLOCKS - 1 - np.arange(A_KBLOCKS))[:, None] * TQ
    dist = back + _slab_diff(np)[None, :]
    idx = np.clip(dist, -A_REL_CLIP, A_REL_CLIP) + A_REL_CLIP
    table_t = jnp.transpose(rel_bias).astype(jnp.float32)
    return jnp.stack([_static_take(table_t, [int(v) for v in row], 1) for row in idx])


def _t5_bucket(rel):
    nb = T5_BUCKETS // 2
    max_exact = nb // 2
    ret = jnp.where(rel > 0, nb, 0)
    n = jnp.abs(rel)
    nf = jnp.maximum(n, 1).astype(jnp.float32)
    large = max_exact + (jnp.log(nf / max_exact) / math.log(T5_MAX_DIST / max_exact)
                         * (nb - max_exact)).astype(jnp.int32)
    large = jnp.minimum(large, nb - 1)
    return ret + jnp.where(n < max_exact, n, large)


def _bias_vec_b(t5_table):
    rel = -_slab_diff() - B_PREV
    return jnp.take(jnp.transpose(t5_table).astype(jnp.float32), _t5_bucket(rel), axis=1)[None]


def kernel(x, a_norm, a_w_in, a_rel_bias, a_w_out, kv_norm, kv_w, t5_bias,
           b_norm, b_w_in, b_sinks, b_w_out, final_norm):
    assert a_norm.shape[0] == 1 and b_norm.shape[0] == 1, "one A layer then one B layer"
    scale = HEAD_DIM ** -0.5 * LOG2E

    def q_col_scale(n):
        return np.where(np.arange(n) < D_MODEL, scale, 1.0).astype(np.float32)

    wt_a = _weight_t(a_w_in[0], q_col_scale(a_w_in.shape[2]), a_norm[0], W_T_COLS_A)
    qt, k, vt, gt, nwt, kvw_k, wo_a, wo_b = _proj_a(x, wt_a, kv_w, b_w_in[0], q_col_scale(b_w_in.shape[2]),
                                                    kv_norm, b_norm[0], a_w_out, b_w_out)
    h, ksh, vsht, qbt, gbt = _attn_a(
        x, qt, k, vt, gt, _bias_vecs_a(a_rel_bias[0] * LOG2E), wo_a, kvw_k, nwt)

    return _attn_b(h, qbt, ksh, vsht, gbt, _bias_vec_b(t5_bias * LOG2E), b_sinks.astype(jnp.float32), wo_b,
                   final_norm[None, :])
```

```python
import functools
import math

import jax
import jax.numpy as jnp
import numpy as np
from jax import lax
from jax.experimental import pallas as pl
from jax.experimental.pallas import tpu as pltpu

D_MODEL = 1024
HEAD_DIM = 64
N_HEADS = D_MODEL // HEAD_DIM
N_PAIRS = N_HEADS // 2
PAIR_W = 2 * HEAD_DIM
CHUNK = 64
RMS_EPS = 1e-6
A_LEFT_CHUNKS = 8
A_REL_CLIP = 256
B_KV_HEADS = 2
B_LEFT_CHUNKS = 2
T5_BUCKETS = 32
T5_MAX_DIST = 128

TQ = 256
TQB = 512
TQA = 512
A_KBLOCKS = A_LEFT_CHUNKS * CHUNK // TQ + 1
B_PREV = B_LEFT_CHUNKS * CHUNK
TM_PROJ = 1024
FEAT_CHUNK = 256
W_T_COLS_A = 1024
W_T_COLS_B = 512
MASKED = -1e30
BIAS_PERIOD = 2 * TQ
SUBLANES = 8
LANES = 128
MAX_CHAINS = 1
BF16_ROWS = 16
LOG2E = math.log2(math.e)
N_SCORE_BUFS = 4
N_SCORE_BUFS_B = 8
VMEM_LIMIT = 56 * 1024 * 1024

_NT = (((1,), (1,)), ((), ()))
_TN = (((0,), (0,)), ((), ()))


def _rms_scale(xf):
    return lax.rsqrt(jnp.mean(xf * xf, axis=-1, keepdims=True) + RMS_EPS)


def _token_scales(xf):
    col = _rms_scale(xf)
    row = jnp.transpose(jnp.broadcast_to(col, (xf.shape[0], LANES)))[0:1, :]
    return col, row


def _silu(v):
    return v * jax.nn.sigmoid(v)


def _weight_t_kernel(w_ref, cs_ref, gain_ref, o_ref):
    o_ref[...] = ((w_ref[...] * cs_ref[...]).T * gain_ref[...]).astype(jnp.bfloat16)


def _weight_t(w, col_scale, row_gain, tc):
    d, n = w.shape
    return pl.pallas_call(
        _weight_t_kernel,
        grid=(n // tc,),
        in_specs=[
            pl.BlockSpec((d, tc), lambda j: (0, j)),
            pl.BlockSpec((1, tc), lambda j: (0, j)),
            pl.BlockSpec((1, d), lambda j: (0, 0)),
        ],
        out_specs=pl.BlockSpec((tc, d), lambda j: (j, 0)),
        out_shape=jax.ShapeDtypeStruct((n, d), jnp.bfloat16),
        compiler_params=pltpu.CompilerParams(dimension_semantics=("arbitrary",), vmem_limit_bytes=VMEM_LIMIT),
        name="weight_t",
    )(w, col_scale[None, :], row_gain[None, :])


def _proj_a_kernel(x_ref, wt_ref, kvw_ref, kvg_ref, bw_ref, cs_ref, bg_ref, woa_ref, wob_ref,
                   qt_ref, k_ref, vt_ref, gt_ref, nwt_ref, kvw16_ref, woa16_ref, wob16_ref):
    nwt_ref[...] = ((bw_ref[...] * cs_ref[...]).T * bg_ref[...]).astype(jnp.bfloat16)
    g_rows = jnp.transpose(jnp.broadcast_to(kvg_ref[...], (2 * PAIR_W, D_MODEL)))
    kvw16_ref[...] = (kvw_ref[...] * g_rows).astype(jnp.bfloat16)
    woa16_ref[...] = woa_ref[0].astype(jnp.bfloat16)
    wob16_ref[...] = wob_ref[0].astype(jnp.bfloat16)

    xf = x_ref[0]
    xb = xf.astype(jnp.bfloat16)
    r_col, r_row = _token_scales(xf)
    k = lax.dot_general(xb, wt_ref[D_MODEL:2 * D_MODEL, :], _NT,
                        preferred_element_type=jnp.float32) * r_col
    for p in range(N_PAIRS):
        k_ref[0, p] = k[:, p * PAIR_W:(p + 1) * PAIR_W].astype(jnp.bfloat16)
    for out_ref, base, act in ((gt_ref, 3 * D_MODEL, _silu), (vt_ref, 2 * D_MODEL, None), (qt_ref, 0, None)):
        for c in range(0, D_MODEL, FEAT_CHUNK):
            w = wt_ref[base + c:base + c + FEAT_CHUNK, :]
            yt = lax.dot_general(w, xb, _NT, preferred_element_type=jnp.float32) * r_row
            out_ref[0, c:c + FEAT_CHUNK, :] = (yt if act is None else act(yt)).astype(jnp.bfloat16)


def _proj_a(x, wt, kv_w, b_w, b_col_scale, kv_gain, b_gain, a_w_out, b_w_out):
    b, s, d = x.shape
    n_i = s // TM_PROJ
    wo_rows = d // (b * n_i)
    step = lambda bi, i: bi * n_i + i
    n_blocks = b_w.shape[1] // W_T_COLS_B
    assert n_blocks <= b * n_i and kv_w.shape == (d, 2 * PAIR_W) and d % (b * n_i * BF16_ROWS) == 0
    blk = lambda bi, i: jnp.minimum(bi * n_i + i, n_blocks - 1)
    feat = jax.ShapeDtypeStruct((b, d, s), jnp.bfloat16)
    return pl.pallas_call(
        _proj_a_kernel,
        grid=(b, n_i),
        in_specs=[
            pl.BlockSpec((1, TM_PROJ, d), lambda bi, i: (bi, i, 0)),
            _const_spec(wt.shape),
            _const_spec(kv_w.shape),
            _const_spec((1, d)),
            pl.BlockSpec((d, W_T_COLS_B), lambda bi, i: (0, blk(bi, i))),
            pl.BlockSpec((1, W_T_COLS_B), lambda bi, i: (0, blk(bi, i))),
            _const_spec((1, d)),
            pl.BlockSpec((1, wo_rows, d), lambda bi, i: (0, step(bi, i), 0)),
            pl.BlockSpec((1, wo_rows, d), lambda bi, i: (0, step(bi, i), 0)),
        ],
        out_specs=[
            pl.BlockSpec((1, d, TM_PROJ), lambda bi, i: (bi, 0, i)),
            pl.BlockSpec((1, N_PAIRS, TM_PROJ, PAIR_W), lambda bi, i: (bi, 0, i, 0)),
            pl.BlockSpec((1, d, TM_PROJ), lambda bi, i: (bi, 0, i)),
            pl.BlockSpec((1, d, TM_PROJ), lambda bi, i: (bi, 0, i)),
            pl.BlockSpec((W_T_COLS_B, d), lambda bi, i: (blk(bi, i), 0)),
            pl.BlockSpec((d, 2 * PAIR_W), lambda bi, i: (0, 0)),
            pl.BlockSpec((wo_rows, d), lambda bi, i: (step(bi, i), 0)),
            pl.BlockSpec((wo_rows, d), lambda bi, i: (step(bi, i), 0)),
        ],
        out_shape=[feat, jax.ShapeDtypeStruct((b, N_PAIRS, s, PAIR_W), jnp.bfloat16), feat, feat,
                   jax.ShapeDtypeStruct((n_blocks * W_T_COLS_B, d), jnp.bfloat16),
                   jax.ShapeDtypeStruct((d, 2 * PAIR_W), jnp.bfloat16),
                   jax.ShapeDtypeStruct((d, d), jnp.bfloat16), jax.ShapeDtypeStruct((d, d), jnp.bfloat16)],
        compiler_params=pltpu.CompilerParams(
            dimension_semantics=("arbitrary", "arbitrary"), vmem_limit_bytes=VMEM_LIMIT),
        name="proj_a",
    )(x, wt, kv_w, kv_gain[None, :], b_w, b_col_scale[None, :], b_gain[None, :], a_w_out, b_w_out)


def _chunk_iota(shape, axis):
    return lax.shift_right_logical(lax.broadcasted_iota(jnp.int32, shape, axis), CHUNK.bit_length() - 1)


def _identity_bf16(n):
    return (lax.broadcasted_iota(jnp.int32, (n, n), 0) == lax.broadcasted_iota(jnp.int32, (n, n), 1)).astype(jnp.bfloat16)


def _toeplitz(vec_row, n_keys):
    return pltpu.roll(jnp.broadcast_to(vec_row, (n_keys, BIAS_PERIOD)), 0, 1, stride=1, stride_axis=0)


def _fill_bias_slabs(vec_ref, bias_ref, n_keys, bands):
    n = len(bands)
    kch = _chunk_iota((n_keys, TQ), 0)
    qch = _chunk_iota((n_keys, TQ), 1)

    def head_body(h, carry):
        for j, band in enumerate(bands):
            t = _toeplitz(vec_ref[j, pl.ds(h, 1), :], n_keys)[:, :TQ]
            if band is not None:
                t = jnp.where(band(kch, qch), t, MASKED)
            bias_ref[h * n + j] = t
        return carry

    lax.fori_loop(0, N_HEADS, head_body, 0)


def _live_rows(band, n_keys):
    chunks_per_half = LANES // CHUNK
    out = []
    for half in range(TQ // LANES):
        qchs = range(half * chunks_per_half, (half + 1) * chunks_per_half)
        live = [kc for kc in range(n_keys // CHUNK) if band is None or any(band(kc, qc) for qc in qchs)]
        out.append((live[0] * CHUNK, (live[-1] + 1) * CHUNK) if live else (0, 0))
    return tuple(out)


def _scores_stage(k_tiles, q_tiles, bias_fns, live, s_ref):
    maxes, row = [], 0
    for half in range(TQ // LANES):
        maxes.append([None] * MAX_CHAINS)
    for kt, qz, bias_fn, live_j in zip(k_tiles, q_tiles, bias_fns, live):
        s = jnp.dot(kt, qz, preferred_element_type=jnp.float32)
        for r in range(s.shape[0] // SUBLANES):
            rows = slice(r * SUBLANES, (r + 1) * SUBLANES)
            for half, (r0, r1) in enumerate(live_j):
                if not r0 <= r * SUBLANES < r1:
                    continue
                lanes = slice(half * LANES, (half + 1) * LANES)
                acc = maxes[half]
                grp = s[rows, lanes] if bias_fn is None else s[rows, lanes] + bias_fn(rows, lanes)
                s_ref[row + r * SUBLANES:row + (r + 1) * SUBLANES, lanes] = grp
                c = r % MAX_CHAINS
                acc[c] = grp if acc[c] is None else jnp.maximum(acc[c], grp)
        row += s.shape[0]
    cols = [jnp.max(functools.reduce(jnp.maximum, [a for a in acc if a is not None]), axis=0, keepdims=True)
            for acc in maxes]
    return jnp.concatenate(cols, axis=1)


def _pv_stage(s_ref, m, v_tiles, live, extra_logit=None):
    if extra_logit is not None:
        m = jnp.maximum(m, extra_logit)
    acc, row = None, 0
    for vt, live_j in zip(v_tiles, live):
        n = vt.shape[1]
        halves = []
        for half, (r0, r1) in enumerate(live_j):
            lanes = slice(half * LANES, (half + 1) * LANES)
            parts = [jnp.zeros((r0, LANES), jnp.bfloat16)] if r0 else []
            if r1 > r0:
                parts.append(jnp.exp2(s_ref[row + r0:row + r1, lanes] - m[:, lanes]).astype(jnp.bfloat16))
            if n > r1:
                parts.append(jnp.zeros((n - r1, LANES), jnp.bfloat16))
            halves.append(parts[0] if len(parts) == 1 else jnp.concatenate(parts, axis=0))
        p = jnp.concatenate(halves, axis=1)
        v_ones = jnp.concatenate([vt, jnp.ones((BF16_ROWS, n), vt.dtype)], axis=0)
        part = jnp.dot(v_ones, p, preferred_element_type=jnp.float32)
        acc = part if acc is None else acc + part
        row += n
    l = acc[HEAD_DIM:HEAD_DIM + 1, :]
    if extra_logit is not None:
        l = l + jnp.exp2(extra_logit - m)
    return acc[:HEAD_DIM, :] / l


def _head_pipeline(n_tiles, scores_fn, pv_fn, s_refs, scores_first, per_pair=2):
    per_group = len(s_refs) // 2
    x_refs, y_refs = s_refs[:per_group], s_refs[per_group:]
    n_groups = n_tiles // per_group

    m = [scores_fn(t // per_pair, t % per_pair, ref) for t, ref in zip(range(per_group), x_refs)]
    for g in range(n_groups):
        src, dst = (x_refs, y_refs) if g % 2 == 0 else (y_refs, x_refs)
        nxt = [(g + 1) * per_group + j for j in range(per_group)] if g + 1 < n_groups else []
        m_next = [scores_fn(t // per_pair, t % per_pair, ref) for t, ref in zip(nxt, dst)] if scores_first else []
        for j in range(per_group):
            if not scores_first and nxt:
                m_next.append(scores_fn(nxt[j] // per_pair, nxt[j] % per_pair, dst[j]))
            t = g * per_group + j
            pv_fn(t // per_pair, t % per_pair, src[j], m[j])
        m = m_next


A_BANDS = (lambda kc, qc: kc >= qc,
           None,
           lambda kc, qc: kc <= qc)
A_LIVE = tuple(_live_rows(band, TQ) for band in A_BANDS)
B_WIN = B_PREV + LANES
B_SUB = PAIR_W
B_BAND = lambda kc, qc: (kc >= qc) & (kc <= qc + B_LEFT_CHUNKS)
B_LIVE = (((0, B_WIN),) * (TQ // LANES),)


def _out_and_residual(zt_ref, wout_ref, res):
    y = lax.dot_general(zt_ref[...], wout_ref[...], _TN, preferred_element_type=jnp.float32)
    return res + y


def _attn_a_kernel(x_ref, qt_ref, kp_ref, ko_ref, vp_ref, vo_ref, gt_ref,
                   bvec_ref, wout_ref, kvw_ref, nwt_ref,
                   h_ref, ksh_ref, vsht_ref, qbt_ref, gbt_ref, zt_ref, bias_ref, *s_refs):
    i = pl.program_id(1)
    subs = TQA // TQ

    @pl.when((pl.program_id(0) == 0) & (i == 0))
    def _():
        _fill_bias_slabs(bvec_ref, bias_ref, TQ, A_BANDS)
        bias_ref[N_HEADS * A_KBLOCKS] = jnp.full((TQ, TQ), MASKED, jnp.float32)

    n_slabs = N_HEADS * A_KBLOCKS
    rows = lax.broadcasted_iota(jnp.int32, (PAIR_W, TQ), 0)

    def key_block(sb, j):
        n = sb + j
        return n // subs, (n % subs) * TQ

    def scores_fn(sp, hh, s_ref):
        sb, p = divmod(sp, N_PAIRS)
        q2 = qt_ref[0, p, :, sb * TQ:(sb + 1) * TQ]
        own_rows = rows < HEAD_DIM if hh == 0 else rows >= HEAD_DIM
        qz = jnp.where(own_rows, q2, jnp.zeros_like(q2))
        k_tiles, bias_fns = [], []
        for j in range(A_KBLOCKS):
            in_seq = i * subs + sb - (A_KBLOCKS - 1) + j >= 0
            slab = jnp.where(in_seq, (2 * p + hh) * A_KBLOCKS + j, n_slabs)
            own, t0 = key_block(sb, j)
            k_tiles.append((ko_ref if own else kp_ref)[0, p, t0:t0 + TQ, :])
            bias_fns.append(lambda rws, lanes, slab=slab: bias_ref[slab, rws, lanes])
        return _scores_stage(k_tiles, [qz] * A_KBLOCKS, bias_fns, A_LIVE, s_ref)

    def pv_fn(sp, hh, s_ref, m):
        sb, p = divmod(sp, N_PAIRS)
        half = slice(hh * HEAD_DIM, (hh + 1) * HEAD_DIM)
        qcols = slice(sb * TQ, (sb + 1) * TQ)
        v_tiles = []
        for j in range(A_KBLOCKS):
            own, t0 = key_block(sb, j)
            v_tiles.append((vo_ref if own else vp_ref)[0, p, half, t0:t0 + TQ])
        o = _pv_stage(s_ref, m, v_tiles, A_LIVE)
        gate = gt_ref[0, p, half, qcols].astype(jnp.float32)
        row0 = p * PAIR_W + hh * HEAD_DIM
        zt_ref[row0:row0 + HEAD_DIM, qcols] = (o * gate).astype(jnp.bfloat16)

    _head_pipeline(subs * N_HEADS, scores_fn, pv_fn, s_refs, scores_first=True)

    h = _out_and_residual(zt_ref, wout_ref, x_ref[0])
    h_ref[0] = h
    hb = h.astype(jnp.bfloat16)
    r_col, r_row = _token_scales(h)
    kv = jnp.dot(hb, kvw_ref[...], preferred_element_type=jnp.float32) * r_col
    ksh_ref[0] = kv[:, :PAIR_W].astype(jnp.bfloat16)
    vsht_ref[0] = jnp.transpose(kv[:, PAIR_W:]).astype(jnp.bfloat16)
    for out_ref, base, act in ((gbt_ref, D_MODEL, _silu), (qbt_ref, 0, None)):
        for c in range(0, D_MODEL, FEAT_CHUNK):
            yt = lax.dot_general(nwt_ref[base + c:base + c + FEAT_CHUNK, :], hb, _NT,
                                 preferred_element_type=jnp.float32) * r_row
            out_ref[0, c:c + FEAT_CHUNK, :] = (yt if act is None else act(yt)).astype(jnp.bfloat16)


def _const_spec(shape):
    return pl.BlockSpec(shape, lambda bi, i: (0,) * len(shape), pipeline_mode=pl.Buffered(1))


def _attn_a(x, qt, k, vt, gt, bvec, wout, kvw, nwt):
    b, s, d = x.shape
    qt4 = qt.reshape(b, N_PAIRS, PAIR_W, s)
    vt4 = vt.reshape(b, N_PAIRS, PAIR_W, s)
    gt4 = gt.reshape(b, N_PAIRS, PAIR_W, s)
    assert (A_KBLOCKS - 1) * TQ <= TQA, "the keys before a grid block must fit in one previous block"
    prev = lambda i: jnp.maximum(i - 1, 0)

    def kspec(index):
        return pl.BlockSpec((1, N_PAIRS, TQA, PAIR_W), lambda bi, i: (bi, 0, index(i), 0))

    def vspec(index):
        return pl.BlockSpec((1, N_PAIRS, PAIR_W, TQA), lambda bi, i: (bi, 0, 0, index(i)))

    feat_spec = pl.BlockSpec((1, N_PAIRS, PAIR_W, TQA), lambda bi, i: (bi, 0, 0, i))
    featout_spec = pl.BlockSpec((1, d, TQA), lambda bi, i: (bi, 0, i))
    feat = jax.ShapeDtypeStruct((b, d, s), jnp.bfloat16)
    return pl.pallas_call(
        _attn_a_kernel,
        grid=(b, s // TQA),
        in_specs=[
            pl.BlockSpec((1, TQA, d), lambda bi, i: (bi, i, 0)),
            feat_spec,
            kspec(prev), kspec(lambda i: i),
            vspec(prev), vspec(lambda i: i),
            feat_spec,
            _const_spec(bvec.shape),
            _const_spec(wout.shape),
            _const_spec(kvw.shape),
            _const_spec(nwt.shape),
        ],
        out_specs=[
            pl.BlockSpec((1, TQA, d), lambda bi, i: (bi, i, 0)),
            pl.BlockSpec((1, TQA, PAIR_W), lambda bi, i: (bi, i, 0)),
            pl.BlockSpec((1, PAIR_W, TQA), lambda bi, i: (bi, 0, i)),
            featout_spec,
            featout_spec,
        ],
        out_shape=[
            jax.ShapeDtypeStruct((b, s, d), jnp.float32),
            jax.ShapeDtypeStruct((b, s, PAIR_W), jnp.bfloat16),
            jax.ShapeDtypeStruct((b, PAIR_W, s), jnp.bfloat16),
            feat, feat,
        ],
        scratch_shapes=[pltpu.VMEM((d, TQA), jnp.bfloat16),
                        pltpu.VMEM((N_HEADS * A_KBLOCKS + 1, TQ, TQ), jnp.float32),
                        *[pltpu.VMEM((A_KBLOCKS * TQ, TQ), jnp.float32)] * N_SCORE_BUFS],
        compiler_params=pltpu.CompilerParams(
            dimension_semantics=("arbitrary", "arbitrary"), vmem_limit_bytes=VMEM_LIMIT),
        name="attn_a",
    )(x, qt4, k, k, vt4, vt4, gt4, bvec, wout, kvw, nwt)


def _attn_b_kernel(h_ref, qt_ref, kp_ref, ko_ref, vp_ref, vo_ref, gt_ref, vec_ref,
                   sink_ref, wout_ref, fg_ref, out_ref, zt_ref, bias_ref, *s_refs):
    i = pl.program_id(1)

    @pl.when((pl.program_id(0) == 0) & (i == 0))
    def _():
        in_band = B_BAND(_chunk_iota((B_WIN, LANES), 0), _chunk_iota((B_WIN, LANES), 1))
        before_block = lax.broadcasted_iota(jnp.int32, (B_WIN, LANES), 0) < B_PREV

        def pair_body(p, carry):
            for hh in range(2):
                t = jnp.where(in_band, _toeplitz(vec_ref[0, pl.ds(2 * p + hh, 1), :], B_WIN)[:, :LANES], MASKED)
                lanes = slice(hh * LANES, (hh + 1) * LANES)
                bias_ref[2 * p, :, lanes] = t.astype(jnp.bfloat16)
                bias_ref[2 * p + 1, :, lanes] = jnp.where(before_block, MASKED, t).astype(jnp.bfloat16)
            return carry

        lax.fori_loop(0, N_PAIRS, pair_body, 0)

    rows = lax.broadcasted_iota(jnp.int32, (PAIR_W, TQ), 0)
    pairs_per_kv = N_PAIRS // B_KV_HEADS
    eye = _identity_bf16(B_SUB)
    head_lanes = lax.broadcasted_iota(jnp.int32, (1, TQ), 1)

    def scores_fn(p, u, s_ref):
        qlanes = slice(u * LANES, (u + 1) * LANES)
        q2 = jnp.concatenate([qt_ref[0, p, :HEAD_DIM, qlanes], qt_ref[0, p, HEAD_DIM:, qlanes]], axis=1)
        kv0 = HEAD_DIM * (p // pairs_per_kv)
        kv_rows = (rows >= kv0) & (rows < kv0 + HEAD_DIM)
        qz = jnp.where(kv_rows, jnp.concatenate([q2, q2], axis=0), jnp.zeros((PAIR_W, TQ), q2.dtype))
        if u == 0:
            k_tile = jnp.concatenate([kp_ref[0], ko_ref[0, :LANES, :]], axis=0)
            variant = jnp.where(i == 0, 1, 0)
        else:
            k_tile, variant = ko_ref[0, u * LANES - B_PREV:(u + 1) * LANES, :], 0

        lhs, rhs = [], []
        for r0 in range(0, B_WIN, B_SUB):
            lhs.append(jnp.concatenate([k_tile[r0:r0 + B_SUB, :], eye], axis=1))
            rhs.append(jnp.concatenate([qz, bias_ref[2 * p + variant, r0:r0 + B_SUB, :]], axis=0))
        sub_live = (((0, B_SUB),) * (TQ // LANES),) * len(lhs)
        return _scores_stage(lhs, rhs, [None] * len(lhs), sub_live, s_ref)

    def pv_fn(p, u, s_ref, m):
        qlanes = slice(u * LANES, (u + 1) * LANES)
        kv_rows = slice(HEAD_DIM * (p // pairs_per_kv), HEAD_DIM * (p // pairs_per_kv + 1))
        if u == 0:
            v_tile = jnp.concatenate([vp_ref[0, kv_rows, :], vo_ref[0, kv_rows, :LANES]], axis=1)
        else:
            v_tile = vo_ref[0, kv_rows, u * LANES - B_PREV:(u + 1) * LANES]
        sink = jnp.where(head_lanes < LANES, sink_ref[0, 2 * p], sink_ref[0, 2 * p + 1]) * LOG2E
        o = _pv_stage(s_ref, m, [v_tile], B_LIVE, extra_logit=sink)
        for hh in range(2):
            hrows = slice(hh * HEAD_DIM, (hh + 1) * HEAD_DIM)
            gate = gt_ref[0, p, hrows, qlanes].astype(jnp.float32)
            row0 = p * PAIR_W + hh * HEAD_DIM
            zt_ref[row0:row0 + HEAD_DIM, qlanes] = (o[:, hh * LANES:(hh + 1) * LANES] * gate).astype(jnp.bfloat16)

    _head_pipeline(N_PAIRS * (TQB // LANES), scores_fn, pv_fn, s_refs, scores_first=False, per_pair=TQB // LANES)

    h2 = _out_and_residual(zt_ref, wout_ref, h_ref[0])
    out_ref[0] = (h2 * fg_ref[...]) * _rms_scale(h2)


def _attn_b(h, qbt, ksh, vsht, gbt, vec, sinks, wout, fg):
    b, s, d = h.shape
    qt4 = qbt.reshape(b, N_PAIRS, PAIR_W, s)
    gt4 = gbt.reshape(b, N_PAIRS, PAIR_W, s)
    feat_spec = pl.BlockSpec((1, N_PAIRS, PAIR_W, TQB), lambda bi, i: (bi, 0, 0, i))
    prev = lambda i: jnp.maximum(i * (TQB // B_PREV) - 1, 0)
    return pl.pallas_call(
        _attn_b_kernel,
        grid=(b, s // TQB),
        in_specs=[
            pl.BlockSpec((1, TQB, d), lambda bi, i: (bi, i, 0)),
            feat_spec,
            pl.BlockSpec((1, B_PREV, PAIR_W), lambda bi, i: (bi, prev(i), 0)),
            pl.BlockSpec((1, TQB, PAIR_W), lambda bi, i: (bi, i, 0)),
            pl.BlockSpec((1, PAIR_W, B_PREV), lambda bi, i: (bi, 0, prev(i))),
            pl.BlockSpec((1, PAIR_W, TQB), lambda bi, i: (bi, 0, i)),
            feat_spec,
            _const_spec(vec.shape),
            pl.BlockSpec(memory_space=pltpu.SMEM),
            _const_spec(wout.shape),
            _const_spec(fg.shape),
        ],
        out_specs=pl.BlockSpec((1, TQB, d), lambda bi, i: (bi, i, 0)),
        out_shape=jax.ShapeDtypeStruct((b, s, d), jnp.float32),
        scratch_shapes=[pltpu.VMEM((d, TQB), jnp.bfloat16),
                        pltpu.VMEM((2 * N_PAIRS, B_WIN, TQ), jnp.bfloat16),
                        *[pltpu.VMEM((B_WIN, TQ), jnp.float32)] * N_SCORE_BUFS_B],
        compiler_params=pltpu.CompilerParams(
            dimension_semantics=("arbitrary", "arbitrary"), vmem_limit_bytes=VMEM_LIMIT),
        name="attn_b",
    )(h, qt4, ksh, ksh, vsht, vsht, gt4, vec, sinks, wout, fg)


def _slab_diff(xp=jnp):
    u = xp.arange(BIAS_PERIOD)
    return xp.where(u < TQ, u, u - BIAS_PERIOD)


def _static_take(table, idx):
    parts, lo = [], 0
    while lo < len(idx):
        hi = lo + 1
        step = idx[hi] - idx[lo] if hi < len(idx) else 0
        while step in (0, 1) and hi < len(idx) and idx[hi] - idx[hi - 1] == step:
            hi += 1
        rows = table[idx[lo]:idx[hi - 1] + 1]
        parts.append(jnp.broadcast_to(rows, (hi - lo,) + table.shape[1:]) if step == 0 else rows)
        lo = hi
    return jnp.concatenate(parts, axis=0)


def _bias_vecs_a(rel_bias):
    back = (A_KBLOCKS - 1 - np.arange(A_KBLOCKS))[:, None] * TQ
    dist = back + _slab_diff(np)[None, :]
    idx = np.clip(dist, -A_REL_CLIP, A_REL_CLIP) + A_REL_CLIP
    vec = jnp.stack([_static_take(rel_bias, [int(v) for v in row]) for row in idx])
    return jnp.transpose(vec, (0, 2, 1)).astype(jnp.float32)


def _t5_bucket(rel):
    nb = T5_BUCKETS // 2
    max_exact = nb // 2
    ret = jnp.where(rel > 0, nb, 0)
    n = jnp.abs(rel)
    nf = jnp.maximum(n, 1).astype(jnp.float32)
    large = max_exact + (jnp.log(nf / max_exact) / math.log(T5_MAX_DIST / max_exact)
                         * (nb - max_exact)).astype(jnp.int32)
    large = jnp.minimum(large, nb - 1)
    return ret + jnp.where(n < max_exact, n, large)


def _bias_vec_b(t5_table):
    rel = -_slab_diff() - B_PREV
    vec = jnp.transpose(t5_table[_t5_bucket(rel)], (1, 0)).astype(jnp.float32)
    return vec[None]


def kernel(x, a_norm, a_w_in, a_rel_bias, a_w_out, kv_norm, kv_w, t5_bias,
           b_norm, b_w_in, b_sinks, b_w_out, final_norm):
    assert a_norm.shape[0] == 1 and b_norm.shape[0] == 1, "one A layer then one B layer"
    scale = HEAD_DIM ** -0.5 * LOG2E

    def q_col_scale(n):
        return np.where(np.arange(n) < D_MODEL, scale, 1.0).astype(np.float32)

    wt_a = _weight_t(a_w_in[0], q_col_scale(a_w_in.shape[2]), a_norm[0], W_T_COLS_A)
    qt, k, vt, gt, nwt, kvw16, wo_a, wo_b = _proj_a(x, wt_a, kv_w, b_w_in[0], q_col_scale(b_w_in.shape[2]),
                                                    kv_norm, b_norm[0], a_w_out, b_w_out)
    h, ksh, vsht, qbt, gbt = _attn_a(
        x, qt, k, vt, gt, _bias_vecs_a(a_rel_bias[0] * LOG2E), wo_a, kvw16, nwt)

    return _attn_b(h, qbt, ksh, vsht, gbt, _bias_vec_b(t5_bias * LOG2E), b_sinks.astype(jnp.float32), wo_b,
                   final_norm[None, :])
```

```python
import functools
import math

import jax
import jax.numpy as jnp
import numpy as np
from jax import lax
from jax.experimental import pallas as pl
from jax.experimental.pallas import tpu as pltpu

D_MODEL = 1024
HEAD_DIM = 64
N_HEADS = D_MODEL // HEAD_DIM
N_PAIRS = N_HEADS // 2
PAIR_W = 2 * HEAD_DIM
CHUNK = 64
RMS_EPS = 1e-6
A_LEFT_CHUNKS = 8
A_REL_CLIP = 256
B_KV_HEADS = 2
B_LEFT_CHUNKS = 2
T5_BUCKETS = 32
T5_MAX_DIST = 128

TQ = 256
TQB = 512
TQA = 512
A_KBLOCKS = A_LEFT_CHUNKS * CHUNK // TQ + 1
B_PREV = B_LEFT_CHUNKS * CHUNK
TM_PROJ = 1024
FEAT_CHUNK = 256
W_T_COLS_A = 2048
W_T_COLS_B = 512
MASKED = -1e30
BIAS_PERIOD = 2 * TQ
SUBLANES = 8
LANES = 128
MAX_CHAINS = 1
BF16_ROWS = 16
LOG2E = math.log2(math.e)
N_SCORE_BUFS = 4
N_SCORE_BUFS_B = 8
VMEM_LIMIT = 56 * 1024 * 1024

_NT = (((1,), (1,)), ((), ()))
_TN = (((0,), (0,)), ((), ()))


def _rms_scale(xf):
    return lax.rsqrt(jnp.mean(xf * xf, axis=-1, keepdims=True) + RMS_EPS)


def _token_scales(xf):
    col = _rms_scale(xf)
    row = jnp.transpose(jnp.broadcast_to(col, (xf.shape[0], LANES)))[0:1, :]
    return col, row


def _silu(v):
    return v * jax.nn.sigmoid(v)


def _weight_t_kernel(w_ref, cs_ref, gain_ref, o_ref):
    o_ref[...] = ((w_ref[...] * cs_ref[...]).T * gain_ref[...]).astype(jnp.bfloat16)


def _weight_t(w, col_scale, row_gain, tc):
    d, n = w.shape
    return pl.pallas_call(
        _weight_t_kernel,
        grid=(n // tc,),
        in_specs=[
            pl.BlockSpec((d, tc), lambda j: (0, j)),
            pl.BlockSpec((1, tc), lambda j: (0, j)),
            pl.BlockSpec((1, d), lambda j: (0, 0)),
        ],
        out_specs=pl.BlockSpec((tc, d), lambda j: (j, 0)),
        out_shape=jax.ShapeDtypeStruct((n, d), jnp.bfloat16),
        compiler_params=pltpu.CompilerParams(dimension_semantics=("arbitrary",), vmem_limit_bytes=VMEM_LIMIT),
        name="weight_t",
    )(w, col_scale[None, :], row_gain[None, :])


def _proj_a_kernel(x_ref, wt_ref, kvw_ref, kvg_ref, bw_ref, cs_ref, bg_ref, woa_ref, wob_ref,
                   qt_ref, k_ref, vt_ref, gt_ref, nwt_ref, kvw16_ref, woa16_ref, wob16_ref):
    nwt_ref[...] = ((bw_ref[...] * cs_ref[...]).T * bg_ref[...]).astype(jnp.bfloat16)
    g_rows = jnp.transpose(jnp.broadcast_to(kvg_ref[...], (2 * PAIR_W, D_MODEL)))
    kvw16_ref[...] = (kvw_ref[...] * g_rows).astype(jnp.bfloat16)
    woa16_ref[...] = woa_ref[0].astype(jnp.bfloat16)
    wob16_ref[...] = wob_ref[0].astype(jnp.bfloat16)

    xf = x_ref[0]
    xb = xf.astype(jnp.bfloat16)
    r_col, r_row = _token_scales(xf)
    k = lax.dot_general(xb, wt_ref[D_MODEL:2 * D_MODEL, :], _NT,
                        preferred_element_type=jnp.float32) * r_col
    for p in range(N_PAIRS):
        k_ref[0, p] = k[:, p * PAIR_W:(p + 1) * PAIR_W].astype(jnp.bfloat16)
    for out_ref, base, act in ((gt_ref, 3 * D_MODEL, _silu), (vt_ref, 2 * D_MODEL, None), (qt_ref, 0, None)):
        for c in range(0, D_MODEL, FEAT_CHUNK):
            w = wt_ref[base + c:base + c + FEAT_CHUNK, :]
            yt = lax.dot_general(w, xb, _NT, preferred_element_type=jnp.float32) * r_row
            out_ref[0, c:c + FEAT_CHUNK, :] = (yt if act is None else act(yt)).astype(jnp.bfloat16)


def _proj_a(x, wt, kv_w, b_w, b_col_scale, kv_gain, b_gain, a_w_out, b_w_out):
    b, s, d = x.shape
    n_i = s // TM_PROJ
    wo_rows = d // (b * n_i)
    step = lambda bi, i: bi * n_i + i
    n_blocks = b_w.shape[1] // W_T_COLS_B
    assert n_blocks <= b * n_i and kv_w.shape == (d, 2 * PAIR_W) and d % (b * n_i * BF16_ROWS) == 0
    blk = lambda bi, i: jnp.minimum(bi * n_i + i, n_blocks - 1)
    feat = jax.ShapeDtypeStruct((b, d, s), jnp.bfloat16)
    return pl.pallas_call(
        _proj_a_kernel,
        grid=(b, n_i),
        in_specs=[
            pl.BlockSpec((1, TM_PROJ, d), lambda bi, i: (bi, i, 0)),
            _const_spec(wt.shape),
            _const_spec(kv_w.shape),
            _const_spec((1, d)),
            pl.BlockSpec((d, W_T_COLS_B), lambda bi, i: (0, blk(bi, i))),
            pl.BlockSpec((1, W_T_COLS_B), lambda bi, i: (0, blk(bi, i))),
            _const_spec((1, d)),
            pl.BlockSpec((1, wo_rows, d), lambda bi, i: (0, step(bi, i), 0)),
            pl.BlockSpec((1, wo_rows, d), lambda bi, i: (0, step(bi, i), 0)),
        ],
        out_specs=[
            pl.BlockSpec((1, d, TM_PROJ), lambda bi, i: (bi, 0, i)),
            pl.BlockSpec((1, N_PAIRS, TM_PROJ, PAIR_W), lambda bi, i: (bi, 0, i, 0)),
            pl.BlockSpec((1, d, TM_PROJ), lambda bi, i: (bi, 0, i)),
            pl.BlockSpec((1, d, TM_PROJ), lambda bi, i: (bi, 0, i)),
            pl.BlockSpec((W_T_COLS_B, d), lambda bi, i: (blk(bi, i), 0)),
            pl.BlockSpec((d, 2 * PAIR_W), lambda bi, i: (0, 0)),
            pl.BlockSpec((wo_rows, d), lambda bi, i: (step(bi, i), 0)),
            pl.BlockSpec((wo_rows, d), lambda bi, i: (step(bi, i), 0)),
        ],
        out_shape=[feat, jax.ShapeDtypeStruct((b, N_PAIRS, s, PAIR_W), jnp.bfloat16), feat, feat,
                   jax.ShapeDtypeStruct((n_blocks * W_T_COLS_B, d), jnp.bfloat16),
                   jax.ShapeDtypeStruct((d, 2 * PAIR_W), jnp.bfloat16),
                   jax.ShapeDtypeStruct((d, d), jnp.bfloat16), jax.ShapeDtypeStruct((d, d), jnp.bfloat16)],
        compiler_params=pltpu.CompilerParams(
            dimension_semantics=("arbitrary", "arbitrary"), vmem_limit_bytes=VMEM_LIMIT),
        name="proj_a",
    )(x, wt, kv_w, kv_gain[None, :], b_w, b_col_scale[None, :], b_gain[None, :], a_w_out, b_w_out)


def _chunk_iota(shape, axis):
    return lax.shift_right_logical(lax.broadcasted_iota(jnp.int32, shape, axis), CHUNK.bit_length() - 1)


def _identity_bf16(n):
    return (lax.broadcasted_iota(jnp.int32, (n, n), 0) == lax.broadcasted_iota(jnp.int32, (n, n), 1)).astype(jnp.bfloat16)


def _toeplitz(vec_row, n_keys):
    return pltpu.roll(jnp.broadcast_to(vec_row, (n_keys, BIAS_PERIOD)), 0, 1, stride=1, stride_axis=0)


def _fill_bias_slabs(vec_ref, bias_ref, n_keys, bands):
    n = len(bands)
    kch = _chunk_iota((n_keys, TQ), 0)
    qch = _chunk_iota((n_keys, TQ), 1)

    def head_body(h, carry):
        for j, band in enumerate(bands):
            t = _toeplitz(vec_ref[j, pl.ds(h, 1), :], n_keys)[:, :TQ]
            if band is not None:
                t = jnp.where(band(kch, qch), t, MASKED)
            bias_ref[h * n + j] = t
        return carry

    lax.fori_loop(0, N_HEADS, head_body, 0)


def _live_rows(band, n_keys):
    chunks_per_half = LANES // CHUNK
    out = []
    for half in range(TQ // LANES):
        qchs = range(half * chunks_per_half, (half + 1) * chunks_per_half)
        live = [kc for kc in range(n_keys // CHUNK) if band is None or any(band(kc, qc) for qc in qchs)]
        out.append((live[0] * CHUNK, (live[-1] + 1) * CHUNK) if live else (0, 0))
    return tuple(out)


def _scores_stage(k_tiles, q_tiles, bias_fns, live, s_ref):
    maxes, row = [], 0
    for half in range(TQ // LANES):
        maxes.append([None] * MAX_CHAINS)
    for kt, qz, bias_fn, live_j in zip(k_tiles, q_tiles, bias_fns, live):
        s = jnp.dot(kt, qz, preferred_element_type=jnp.float32)
        for r in range(s.shape[0] // SUBLANES):
            rows = slice(r * SUBLANES, (r + 1) * SUBLANES)
            for half, (r0, r1) in enumerate(live_j):
                if not r0 <= r * SUBLANES < r1:
                    continue
                lanes = slice(half * LANES, (half + 1) * LANES)
                acc = maxes[half]
                grp = s[rows, lanes] if bias_fn is None else s[rows, lanes] + bias_fn(rows, lanes)
                s_ref[row + r * SUBLANES:row + (r + 1) * SUBLANES, lanes] = grp
                c = r % MAX_CHAINS
                acc[c] = grp if acc[c] is None else jnp.maximum(acc[c], grp)
        row += s.shape[0]
    cols = [jnp.max(functools.reduce(jnp.maximum, [a for a in acc if a is not None]), axis=0, keepdims=True)
            for acc in maxes]
    return jnp.concatenate(cols, axis=1)


def _pv_stage(s_ref, m, v_tiles, live, extra_logit=None):
    if extra_logit is not None:
        m = jnp.maximum(m, extra_logit)
    acc, row = None, 0
    for vt, live_j in zip(v_tiles, live):
        n = vt.shape[1]
        halves = []
        for half, (r0, r1) in enumerate(live_j):
            lanes = slice(half * LANES, (half + 1) * LANES)
            parts = [jnp.zeros((r0, LANES), jnp.bfloat16)] if r0 else []
            if r1 > r0:
                parts.append(jnp.exp2(s_ref[row + r0:row + r1, lanes] - m[:, lanes]).astype(jnp.bfloat16))
            if n > r1:
                parts.append(jnp.zeros((n - r1, LANES), jnp.bfloat16))
            halves.append(parts[0] if len(parts) == 1 else jnp.concatenate(parts, axis=0))
        p = jnp.concatenate(halves, axis=1)
        v_ones = jnp.concatenate([vt, jnp.ones((BF16_ROWS, n), vt.dtype)], axis=0)
        part = jnp.dot(v_ones, p, preferred_element_type=jnp.float32)
        acc = part if acc is None else acc + part
        row += n
    l = acc[HEAD_DIM:HEAD_DIM + 1, :]
    if extra_logit is not None:
        l = l + jnp.exp2(extra_logit - m)
    return acc[:HEAD_DIM, :] / l


def _head_pipeline(n_tiles, scores_fn, pv_fn, s_refs, scores_first, per_pair=2):
    per_group = len(s_refs) // 2
    x_refs, y_refs = s_refs[:per_group], s_refs[per_group:]
    n_groups = n_tiles // per_group

    m = [scores_fn(t // per_pair, t % per_pair, ref) for t, ref in zip(range(per_group), x_refs)]
    for g in range(n_groups):
        src, dst = (x_refs, y_refs) if g % 2 == 0 else (y_refs, x_refs)
        nxt = [(g + 1) * per_group + j for j in range(per_group)] if g + 1 < n_groups else []
        m_next = [scores_fn(t // per_pair, t % per_pair, ref) for t, ref in zip(nxt, dst)] if scores_first else []
        for j in range(per_group):
            if not scores_first and nxt:
                m_next.append(scores_fn(nxt[j] // per_pair, nxt[j] % per_pair, dst[j]))
            t = g * per_group + j
            pv_fn(t // per_pair, t % per_pair, src[j], m[j])
        m = m_next


A_BANDS = (lambda kc, qc: kc >= qc,
           None,
           lambda kc, qc: kc <= qc)
A_LIVE = tuple(_live_rows(band, TQ) for band in A_BANDS)
B_WIN = B_PREV + LANES
B_SUB = PAIR_W
B_BAND = lambda kc, qc: (kc >= qc) & (kc <= qc + B_LEFT_CHUNKS)
B_LIVE = (((0, B_WIN),) * (TQ // LANES),)


def _out_and_residual(zt_ref, wout_ref, res):
    y = lax.dot_general(zt_ref[...], wout_ref[...], _TN, preferred_element_type=jnp.float32)
    return res + y


def _attn_a_kernel(x_ref, qt_ref, kp_ref, ko_ref, vp_ref, vo_ref, gt_ref,
                   bvec_ref, wout_ref, kvw_ref, nwt_ref,
                   h_ref, ksh_ref, vsht_ref, qbt_ref, gbt_ref, zt_ref, bias_ref, *s_refs):
    i = pl.program_id(1)
    subs = TQA // TQ

    @pl.when((pl.program_id(0) == 0) & (i == 0))
    def _():
        _fill_bias_slabs(bvec_ref, bias_ref, TQ, A_BANDS)
        bias_ref[N_HEADS * A_KBLOCKS] = jnp.full((TQ, TQ), MASKED, jnp.float32)

    n_slabs = N_HEADS * A_KBLOCKS
    rows = lax.broadcasted_iota(jnp.int32, (PAIR_W, TQ), 0)

    def key_block(sb, j):
        n = sb + j
        return n // subs, (n % subs) * TQ

    def scores_fn(sp, hh, s_ref):
        sb, p = divmod(sp, N_PAIRS)
        q2 = qt_ref[0, p, :, sb * TQ:(sb + 1) * TQ]
        own_rows = rows < HEAD_DIM if hh == 0 else rows >= HEAD_DIM
        qz = jnp.where(own_rows, q2, jnp.zeros_like(q2))
        k_tiles, bias_fns = [], []
        for j in range(A_KBLOCKS):
            in_seq = i * subs + sb - (A_KBLOCKS - 1) + j >= 0
            slab = jnp.where(in_seq, (2 * p + hh) * A_KBLOCKS + j, n_slabs)
            own, t0 = key_block(sb, j)
            k_tiles.append((ko_ref if own else kp_ref)[0, p, t0:t0 + TQ, :])
            bias_fns.append(lambda rws, lanes, slab=slab: bias_ref[slab, rws, lanes])
        return _scores_stage(k_tiles, [qz] * A_KBLOCKS, bias_fns, A_LIVE, s_ref)

    def pv_fn(sp, hh, s_ref, m):
        sb, p = divmod(sp, N_PAIRS)
        half = slice(hh * HEAD_DIM, (hh + 1) * HEAD_DIM)
        qcols = slice(sb * TQ, (sb + 1) * TQ)
        v_tiles = []
        for j in range(A_KBLOCKS):
            own, t0 = key_block(sb, j)
            v_tiles.append((vo_ref if own else vp_ref)[0, p, half, t0:t0 + TQ])
        o = _pv_stage(s_ref, m, v_tiles, A_LIVE)
        gate = gt_ref[0, p, half, qcols].astype(jnp.float32)
        row0 = p * PAIR_W + hh * HEAD_DIM
        zt_ref[row0:row0 + HEAD_DIM, qcols] = (o * gate).astype(jnp.bfloat16)

    _head_pipeline(subs * N_HEADS, scores_fn, pv_fn, s_refs, scores_first=True)

    h = _out_and_residual(zt_ref, wout_ref, x_ref[0])
    h_ref[0] = h
    hb = h.astype(jnp.bfloat16)
    r_col, r_row = _token_scales(h)
    kv = jnp.dot(hb, kvw_ref[...], preferred_element_type=jnp.float32) * r_col
    ksh_ref[0] = kv[:, :PAIR_W].astype(jnp.bfloat16)
    vsht_ref[0] = jnp.transpose(kv[:, PAIR_W:]).astype(jnp.bfloat16)
    for out_ref, base, act in ((gbt_ref, D_MODEL, _silu), (qbt_ref, 0, None)):
        for c in range(0, D_MODEL, FEAT_CHUNK):
            yt = lax.dot_general(nwt_ref[base + c:base + c + FEAT_CHUNK, :], hb, _NT,
                                 preferred_element_type=jnp.float32) * r_row
            out_ref[0, c:c + FEAT_CHUNK, :] = (yt if act is None else act(yt)).astype(jnp.bfloat16)


def _const_spec(shape):
    return pl.BlockSpec(shape, lambda bi, i: (0,) * len(shape), pipeline_mode=pl.Buffered(1))


def _attn_a(x, qt, k, vt, gt, bvec, wout, kvw, nwt):
    b, s, d = x.shape
    qt4 = qt.reshape(b, N_PAIRS, PAIR_W, s)
    vt4 = vt.reshape(b, N_PAIRS, PAIR_W, s)
    gt4 = gt.reshape(b, N_PAIRS, PAIR_W, s)
    assert (A_KBLOCKS - 1) * TQ <= TQA, "the keys before a grid block must fit in one previous block"
    prev = lambda i: jnp.maximum(i - 1, 0)

    def kspec(index):
        return pl.BlockSpec((1, N_PAIRS, TQA, PAIR_W), lambda bi, i: (bi, 0, index(i), 0))

    def vspec(index):
        return pl.BlockSpec((1, N_PAIRS, PAIR_W, TQA), lambda bi, i: (bi, 0, 0, index(i)))

    feat_spec = pl.BlockSpec((1, N_PAIRS, PAIR_W, TQA), lambda bi, i: (bi, 0, 0, i))
    featout_spec = pl.BlockSpec((1, d, TQA), lambda bi, i: (bi, 0, i))
    feat = jax.ShapeDtypeStruct((b, d, s), jnp.bfloat16)
    return pl.pallas_call(
        _attn_a_kernel,
        grid=(b, s // TQA),
        in_specs=[
            pl.BlockSpec((1, TQA, d), lambda bi, i: (bi, i, 0)),
            feat_spec,
            kspec(prev), kspec(lambda i: i),
            vspec(prev), vspec(lambda i: i),
            feat_spec,
            _const_spec(bvec.shape),
            _const_spec(wout.shape),
            _const_spec(kvw.shape),
            _const_spec(nwt.shape),
        ],
        out_specs=[
            pl.BlockSpec((1, TQA, d), lambda bi, i: (bi, i, 0)),
            pl.BlockSpec((1, TQA, PAIR_W), lambda bi, i: (bi, i, 0)),
            pl.BlockSpec((1, PAIR_W, TQA), lambda bi, i: (bi, 0, i)),
            featout_spec,
            featout_spec,
        ],
        out_shape=[
            jax.ShapeDtypeStruct((b, s, d), jnp.float32),
            jax.ShapeDtypeStruct((b, s, PAIR_W), jnp.bfloat16),
            jax.ShapeDtypeStruct((b, PAIR_W, s), jnp.bfloat16),
            feat, feat,
        ],
        scratch_shapes=[pltpu.VMEM((d, TQA), jnp.bfloat16),
                        pltpu.VMEM((N_HEADS * A_KBLOCKS + 1, TQ, TQ), jnp.float32),
                        *[pltpu.VMEM((A_KBLOCKS * TQ, TQ), jnp.float32)] * N_SCORE_BUFS],
        compiler_params=pltpu.CompilerParams(
            dimension_semantics=("arbitrary", "arbitrary"), vmem_limit_bytes=VMEM_LIMIT),
        name="attn_a",
    )(x, qt4, k, k, vt4, vt4, gt4, bvec, wout, kvw, nwt)


def _attn_b_kernel(h_ref, qt_ref, kp_ref, ko_ref, vp_ref, vo_ref, gt_ref, vec_ref,
                   sink_ref, wout_ref, fg_ref, out_ref, zt_ref, bias_ref, *s_refs):
    i = pl.program_id(1)

    @pl.when((pl.program_id(0) == 0) & (i == 0))
    def _():
        in_band = B_BAND(_chunk_iota((B_WIN, LANES), 0), _chunk_iota((B_WIN, LANES), 1))
        before_block = lax.broadcasted_iota(jnp.int32, (B_WIN, LANES), 0) < B_PREV

        def pair_body(p, carry):
            for hh in range(2):
                t = jnp.where(in_band, _toeplitz(vec_ref[0, pl.ds(2 * p + hh, 1), :], B_WIN)[:, :LANES], MASKED)
                lanes = slice(hh * LANES, (hh + 1) * LANES)
                bias_ref[2 * p, :, lanes] = t.astype(jnp.bfloat16)
                bias_ref[2 * p + 1, :, lanes] = jnp.where(before_block, MASKED, t).astype(jnp.bfloat16)
            return carry

        lax.fori_loop(0, N_PAIRS, pair_body, 0)

    rows = lax.broadcasted_iota(jnp.int32, (PAIR_W, TQ), 0)
    pairs_per_kv = N_PAIRS // B_KV_HEADS
    eye = _identity_bf16(B_SUB)
    head_lanes = lax.broadcasted_iota(jnp.int32, (1, TQ), 1)

    def scores_fn(p, u, s_ref):
        qlanes = slice(u * LANES, (u + 1) * LANES)
        q2 = jnp.concatenate([qt_ref[0, p, :HEAD_DIM, qlanes], qt_ref[0, p, HEAD_DIM:, qlanes]], axis=1)
        kv0 = HEAD_DIM * (p // pairs_per_kv)
        kv_rows = (rows >= kv0) & (rows < kv0 + HEAD_DIM)
        qz = jnp.where(kv_rows, jnp.concatenate([q2, q2], axis=0), jnp.zeros((PAIR_W, TQ), q2.dtype))
        if u == 0:
            k_tile = jnp.concatenate([kp_ref[0], ko_ref[0, :LANES, :]], axis=0)
            variant = jnp.where(i == 0, 1, 0)
        else:
            k_tile, variant = ko_ref[0, u * LANES - B_PREV:(u + 1) * LANES, :], 0

        lhs, rhs = [], []
        for r0 in range(0, B_WIN, B_SUB):
            lhs.append(jnp.concatenate([k_tile[r0:r0 + B_SUB, :], eye], axis=1))
            rhs.append(jnp.concatenate([qz, bias_ref[2 * p + variant, r0:r0 + B_SUB, :]], axis=0))
        sub_live = (((0, B_SUB),) * (TQ // LANES),) * len(lhs)
        return _scores_stage(lhs, rhs, [None] * len(lhs), sub_live, s_ref)

    def pv_fn(p, u, s_ref, m):
        qlanes = slice(u * LANES, (u + 1) * LANES)
        kv_rows = slice(HEAD_DIM * (p // pairs_per_kv), HEAD_DIM * (p // pairs_per_kv + 1))
        if u == 0:
            v_tile = jnp.concatenate([vp_ref[0, kv_rows, :], vo_ref[0, kv_rows, :LANES]], axis=1)
        else:
            v_tile = vo_ref[0, kv_rows, u * LANES - B_PREV:(u + 1) * LANES]
        sink = jnp.where(head_lanes < LANES, sink_ref[0, 2 * p], sink_ref[0, 2 * p + 1]) * LOG2E
        o = _pv_stage(s_ref, m, [v_tile], B_LIVE, extra_logit=sink)
        for hh in range(2):
            hrows = slice(hh * HEAD_DIM, (hh + 1) * HEAD_DIM)
            gate = gt_ref[0, p, hrows, qlanes].astype(jnp.float32)
            row0 = p * PAIR_W + hh * HEAD_DIM
            zt_ref[row0:row0 + HEAD_DIM, qlanes] = (o[:, hh * LANES:(hh + 1) * LANES] * gate).astype(jnp.bfloat16)

    _head_pipeline(N_PAIRS * (TQB // LANES), scores_fn, pv_fn, s_refs, scores_first=False, per_pair=TQB // LANES)

    h2 = _out_and_residual(zt_ref, wout_ref, h_ref[0])
    out_ref[0] = (h2 * fg_ref[...]) * _rms_scale(h2)


def _attn_b(h, qbt, ksh, vsht, gbt, vec, sinks, wout, fg):
    b, s, d = h.shape
    qt4 = qbt.reshape(b, N_PAIRS, PAIR_W, s)
    gt4 = gbt.reshape(b, N_PAIRS, PAIR_W, s)
    feat_spec = pl.BlockSpec((1, N_PAIRS, PAIR_W, TQB), lambda bi, i: (bi, 0, 0, i))
    prev = lambda i: jnp.maximum(i * (TQB // B_PREV) - 1, 0)
    return pl.pallas_call(
        _attn_b_kernel,
        grid=(b, s // TQB),
        in_specs=[
            pl.BlockSpec((1, TQB, d), lambda bi, i: (bi, i, 0)),
            feat_spec,
            pl.BlockSpec((1, B_PREV, PAIR_W), lambda bi, i: (bi, prev(i), 0)),
            pl.BlockSpec((1, TQB, PAIR_W), lambda bi, i: (bi, i, 0)),
            pl.BlockSpec((1, PAIR_W, B_PREV), lambda bi, i: (bi, 0, prev(i))),
            pl.BlockSpec((1, PAIR_W, TQB), lambda bi, i: (bi, 0, i)),
            feat_spec,
            _const_spec(vec.shape),
            pl.BlockSpec(memory_space=pltpu.SMEM),
            _const_spec(wout.shape),
            _const_spec(fg.shape),
        ],
        out_specs=pl.BlockSpec((1, TQB, d), lambda bi, i: (bi, i, 0)),
        out_shape=jax.ShapeDtypeStruct((b, s, d), jnp.float32),
        scratch_shapes=[pltpu.VMEM((d, TQB), jnp.bfloat16),
                        pltpu.VMEM((2 * N_PAIRS, B_WIN, TQ), jnp.bfloat16),
                        *[pltpu.VMEM((B_WIN, TQ), jnp.float32)] * N_SCORE_BUFS_B],
        compiler_params=pltpu.CompilerParams(
            dimension_semantics=("arbitrary", "arbitrary"), vmem_limit_bytes=VMEM_LIMIT),
        name="attn_b",
    )(h, qt4, ksh, ksh, vsht, vsht, gt4, vec, sinks, wout, fg)


def _slab_diff(xp=jnp):
    u = xp.arange(BIAS_PERIOD)
    return xp.where(u < TQ, u, u - BIAS_PERIOD)


def _static_take(table, idx):
    parts, lo = [], 0
    while lo < len(idx):
        hi = lo + 1
        step = idx[hi] - idx[lo] if hi < len(idx) else 0
        while step in (0, 1) and hi < len(idx) and idx[hi] - idx[hi - 1] == step:
            hi += 1
        rows = table[idx[lo]:idx[hi - 1] + 1]
        parts.append(jnp.broadcast_to(rows, (hi - lo,) + table.shape[1:]) if step == 0 else rows)
        lo = hi
    return jnp.concatenate(parts, axis=0)


def _bias_vecs_a(rel_bias):
    back = (A_KBLOCKS - 1 - np.arange(A_KBLOCKS))[:, None] * TQ
    dist = back + _slab_diff(np)[None, :]
    idx = np.clip(dist, -A_REL_CLIP, A_REL_CLIP) + A_REL_CLIP
    vec = jnp.stack([_static_take(rel_bias, [int(v) for v in row]) for row in idx])
    return jnp.transpose(vec, (0, 2, 1)).astype(jnp.float32)


def _t5_bucket(rel):
    nb = T5_BUCKETS // 2
    max_exact = nb // 2
    ret = jnp.where(rel > 0, nb, 0)
    n = jnp.abs(rel)
    nf = jnp.maximum(n, 1).astype(jnp.float32)
    large = max_exact + (jnp.log(nf / max_exact) / math.log(T5_MAX_DIST / max_exact)
                         * (nb - max_exact)).astype(jnp.int32)
    large = jnp.minimum(large, nb - 1)
    return ret + jnp.where(n < max_exact, n, large)


def _bias_vec_b(t5_table):
    rel = -_slab_diff() - B_PREV
    vec = jnp.transpose(t5_table[_t5_bucket(rel)], (1, 0)).astype(jnp.float32)
    return vec[None]


def kernel(x, a_norm, a_w_in, a_rel_bias, a_w_out, kv_norm, kv_w, t5_bias,
           b_norm, b_w_in, b_sinks, b_w_out, final_norm):
    assert a_norm.shape[0] == 1 and b_norm.shape[0] == 1, "one A layer then one B layer"
    scale = HEAD_DIM ** -0.5 * LOG2E

    def q_col_scale(n):
        return np.where(np.arange(n) < D_MODEL, scale, 1.0).astype(np.float32)

    wt_a = _weight_t(a_w_in[0], q_col_scale(a_w_in.shape[2]), a_norm[0], W_T_COLS_A)
    qt, k, vt, gt, nwt, kvw16, wo_a, wo_b = _proj_a(x, wt_a, kv_w, b_w_in[0], q_col_scale(b_w_in.shape[2]),
                                                    kv_norm, b_norm[0], a_w_out, b_w_out)
    h, ksh, vsht, qbt, gbt = _attn_a(
        x, qt, k, vt, gt, _bias_vecs_a(a_rel_bias[0] * LOG2E), wo_a, kvw16, nwt)

    return _attn_b(h, qbt, ksh, vsht, gbt, _bias_vec_b(t5_bias * LOG2E), b_sinks.astype(jnp.float32), wo_b,
                   final_norm[None, :])
```

```python
import functools
import math

import jax
import jax.numpy as jnp
import numpy as np
from jax import lax
from jax.experimental import pallas as pl
from jax.experimental.pallas import tpu as pltpu

D_MODEL = 1024
HEAD_DIM = 64
N_HEADS = D_MODEL // HEAD_DIM
N_PAIRS = N_HEADS // 2
PAIR_W = 2 * HEAD_DIM
CHUNK = 64
RMS_EPS = 1e-6
A_LEFT_CHUNKS = 8
A_REL_CLIP = 256
B_KV_HEADS = 2
B_LEFT_CHUNKS = 2
T5_BUCKETS = 32
T5_MAX_DIST = 128

TQ = 256
TQB = 512
TQA = 512
A_KBLOCKS = A_LEFT_CHUNKS * CHUNK // TQ + 1
B_PREV = B_LEFT_CHUNKS * CHUNK
TM_PROJ = 1024
FEAT_CHUNK = 256
W_T_COLS_A = 2048
W_T_COLS_B = 512
MASKED = -1e30
BIAS_PERIOD = 2 * TQ
SUBLANES = 8
LANES = 128
MAX_CHAINS = 1
BF16_ROWS = 16
LOG2E = math.log2(math.e)
N_SCORE_BUFS = 4
N_SCORE_BUFS_B = 8
VMEM_LIMIT = 56 * 1024 * 1024

_NT = (((1,), (1,)), ((), ()))
_TN = (((0,), (0,)), ((), ()))


def _rms_scale(xf):
    return lax.rsqrt(jnp.mean(xf * xf, axis=-1, keepdims=True) + RMS_EPS)


def _token_scales(xf):
    col = _rms_scale(xf)
    row = jnp.transpose(jnp.broadcast_to(col, (xf.shape[0], LANES)))[0:1, :]
    return col, row


def _silu(v):
    return v * jax.nn.sigmoid(v)


def _weight_t_kernel(w_ref, cs_ref, gain_ref, o_ref):
    g_col = jnp.transpose(jnp.broadcast_to(gain_ref[...], (LANES, w_ref.shape[0])))[:, :1]
    o_ref[...] = jnp.transpose(((w_ref[...] * cs_ref[...]) * g_col).astype(jnp.bfloat16))


def _weight_t(w, col_scale, row_gain, tc):
    d, n = w.shape
    return pl.pallas_call(
        _weight_t_kernel,
        grid=(n // tc,),
        in_specs=[
            pl.BlockSpec((d, tc), lambda j: (0, j)),
            pl.BlockSpec((1, tc), lambda j: (0, j)),
            pl.BlockSpec((1, d), lambda j: (0, 0)),
        ],
        out_specs=pl.BlockSpec((tc, d), lambda j: (j, 0)),
        out_shape=jax.ShapeDtypeStruct((n, d), jnp.bfloat16),
        compiler_params=pltpu.CompilerParams(dimension_semantics=("arbitrary",), vmem_limit_bytes=VMEM_LIMIT),
        name="weight_t",
    )(w, col_scale[None, :], row_gain[None, :])


def _proj_a_kernel(x_ref, wt_ref, kvw_ref, kvg_ref, bw_ref, cs_ref, bg_ref, woa_ref, wob_ref,
                   qt_ref, k_ref, vt_ref, gt_ref, nwt_ref, kvw16_ref, woa16_ref, wob16_ref):
    nwt_ref[...] = ((bw_ref[...] * cs_ref[...]).T * bg_ref[...]).astype(jnp.bfloat16)
    g_rows = jnp.transpose(jnp.broadcast_to(kvg_ref[...], (2 * PAIR_W, D_MODEL)))
    kvw16_ref[...] = (kvw_ref[...] * g_rows).astype(jnp.bfloat16)
    woa16_ref[...] = woa_ref[0].astype(jnp.bfloat16)
    wob16_ref[...] = wob_ref[0].astype(jnp.bfloat16)

    xf = x_ref[0]
    xb = xf.astype(jnp.bfloat16)
    r_col, r_row = _token_scales(xf)
    k = lax.dot_general(xb, wt_ref[D_MODEL:2 * D_MODEL, :], _NT,
                        preferred_element_type=jnp.float32) * r_col
    for p in range(N_PAIRS):
        k_ref[0, p] = k[:, p * PAIR_W:(p + 1) * PAIR_W].astype(jnp.bfloat16)
    for out_ref, base, act in ((gt_ref, 3 * D_MODEL, _silu), (vt_ref, 2 * D_MODEL, None), (qt_ref, 0, None)):
        for c in range(0, D_MODEL, FEAT_CHUNK):
            w = wt_ref[base + c:base + c + FEAT_CHUNK, :]
            yt = lax.dot_general(w, xb, _NT, preferred_element_type=jnp.float32) * r_row
            out_ref[0, c:c + FEAT_CHUNK, :] = (yt if act is None else act(yt)).astype(jnp.bfloat16)


def _proj_a(x, wt, kv_w, b_w, b_col_scale, kv_gain, b_gain, a_w_out, b_w_out):
    b, s, d = x.shape
    n_i = s // TM_PROJ
    wo_rows = d // (b * n_i)
    step = lambda bi, i: bi * n_i + i
    n_blocks = b_w.shape[1] // W_T_COLS_B
    assert n_blocks <= b * n_i and kv_w.shape == (d, 2 * PAIR_W) and d % (b * n_i * BF16_ROWS) == 0
    blk = lambda bi, i: jnp.minimum(bi * n_i + i, n_blocks - 1)
    feat = jax.ShapeDtypeStruct((b, d, s), jnp.bfloat16)
    return pl.pallas_call(
        _proj_a_kernel,
        grid=(b, n_i),
        in_specs=[
            pl.BlockSpec((1, TM_PROJ, d), lambda bi, i: (bi, i, 0)),
            _const_spec(wt.shape),
            _const_spec(kv_w.shape),
            _const_spec((1, d)),
            pl.BlockSpec((d, W_T_COLS_B), lambda bi, i: (0, blk(bi, i))),
            pl.BlockSpec((1, W_T_COLS_B), lambda bi, i: (0, blk(bi, i))),
            _const_spec((1, d)),
            pl.BlockSpec((1, wo_rows, d), lambda bi, i: (0, step(bi, i), 0)),
            pl.BlockSpec((1, wo_rows, d), lambda bi, i: (0, step(bi, i), 0)),
        ],
        out_specs=[
            pl.BlockSpec((1, d, TM_PROJ), lambda bi, i: (bi, 0, i)),
            pl.BlockSpec((1, N_PAIRS, TM_PROJ, PAIR_W), lambda bi, i: (bi, 0, i, 0)),
            pl.BlockSpec((1, d, TM_PROJ), lambda bi, i: (bi, 0, i)),
            pl.BlockSpec((1, d, TM_PROJ), lambda bi, i: (bi, 0, i)),
            pl.BlockSpec((W_T_COLS_B, d), lambda bi, i: (blk(bi, i), 0)),
            pl.BlockSpec((d, 2 * PAIR_W), lambda bi, i: (0, 0)),
            pl.BlockSpec((wo_rows, d), lambda bi, i: (step(bi, i), 0)),
            pl.BlockSpec((wo_rows, d), lambda bi, i: (step(bi, i), 0)),
        ],
        out_shape=[feat, jax.ShapeDtypeStruct((b, N_PAIRS, s, PAIR_W), jnp.bfloat16), feat, feat,
                   jax.ShapeDtypeStruct((n_blocks * W_T_COLS_B, d), jnp.bfloat16),
                   jax.ShapeDtypeStruct((d, 2 * PAIR_W), jnp.bfloat16),
                   jax.ShapeDtypeStruct((d, d), jnp.bfloat16), jax.ShapeDtypeStruct((d, d), jnp.bfloat16)],
        compiler_params=pltpu.CompilerParams(
            dimension_semantics=("arbitrary", "arbitrary"), vmem_limit_bytes=VMEM_LIMIT),
        name="proj_a",
    )(x, wt, kv_w, kv_gain[None, :], b_w, b_col_scale[None, :], b_gain[None, :], a_w_out, b_w_out)


def _chunk_iota(shape, axis):
    return lax.shift_right_logical(lax.broadcasted_iota(jnp.int32, shape, axis), CHUNK.bit_length() - 1)


def _identity_bf16(n):
    return (lax.broadcasted_iota(jnp.int32, (n, n), 0) == lax.broadcasted_iota(jnp.int32, (n, n), 1)).astype(jnp.bfloat16)


def _toeplitz(vec_row, n_keys):
    return pltpu.roll(jnp.broadcast_to(vec_row, (n_keys, BIAS_PERIOD)), 0, 1, stride=1, stride_axis=0)


def _fill_bias_slabs(vec_ref, bias_ref, n_keys, bands, flat):
    n = len(bands)
    kch = _chunk_iota((n_keys, TQ), 0)
    qch = _chunk_iota((n_keys, TQ), 1)

    def head_body(h, carry):
        for j, band in enumerate(bands):
            row = vec_ref[j, pl.ds(h, 1), :]
            t = (jnp.broadcast_to(row, (n_keys, BIAS_PERIOD)) if flat[j] else _toeplitz(row, n_keys))[:, :TQ]
            if band is not None:
                t = jnp.where(band(kch, qch), t, MASKED)
            bias_ref[h * n + j] = t
        return carry

    lax.fori_loop(0, N_HEADS, head_body, 0)


def _live_rows(band, n_keys):
    chunks_per_half = LANES // CHUNK
    out = []
    for half in range(TQ // LANES):
        qchs = range(half * chunks_per_half, (half + 1) * chunks_per_half)
        live = [kc for kc in range(n_keys // CHUNK) if band is None or any(band(kc, qc) for qc in qchs)]
        out.append((live[0] * CHUNK, (live[-1] + 1) * CHUNK) if live else (0, 0))
    return tuple(out)


def _scores_stage(k_tiles, q_tiles, bias_fns, live, s_ref):
    maxes, row = [], 0
    for half in range(TQ // LANES):
        maxes.append([None] * MAX_CHAINS)
    for kt, qz, bias_fn, live_j in zip(k_tiles, q_tiles, bias_fns, live):
        s = jnp.dot(kt, qz, preferred_element_type=jnp.float32)
        for r in range(s.shape[0] // SUBLANES):
            rows = slice(r * SUBLANES, (r + 1) * SUBLANES)
            for half, (r0, r1) in enumerate(live_j):
                if not r0 <= r * SUBLANES < r1:
                    continue
                lanes = slice(half * LANES, (half + 1) * LANES)
                acc = maxes[half]
                grp = s[rows, lanes] if bias_fn is None else s[rows, lanes] + bias_fn(rows, lanes)
                s_ref[row + r * SUBLANES:row + (r + 1) * SUBLANES, lanes] = grp
                c = r % MAX_CHAINS
                acc[c] = grp if acc[c] is None else jnp.maximum(acc[c], grp)
        row += s.shape[0]
    cols = [jnp.max(functools.reduce(jnp.maximum, [a for a in acc if a is not None]), axis=0, keepdims=True)
            for acc in maxes]
    return jnp.concatenate(cols, axis=1)


def _pv_stage(s_ref, m, v_tiles, live, extra_logit=None):
    if extra_logit is not None:
        m = jnp.maximum(m, extra_logit)
    acc, row = None, 0
    for vt, live_j in zip(v_tiles, live):
        n = vt.shape[1]
        halves = []
        for half, (r0, r1) in enumerate(live_j):
            lanes = slice(half * LANES, (half + 1) * LANES)
            parts = [jnp.zeros((r0, LANES), jnp.bfloat16)] if r0 else []
            if r1 > r0:
                parts.append(jnp.exp2(s_ref[row + r0:row + r1, lanes] - m[:, lanes]).astype(jnp.bfloat16))
            if n > r1:
                parts.append(jnp.zeros((n - r1, LANES), jnp.bfloat16))
            halves.append(parts[0] if len(parts) == 1 else jnp.concatenate(parts, axis=0))
        p = jnp.concatenate(halves, axis=1)
        v_ones = jnp.concatenate([vt, jnp.ones((BF16_ROWS, n), vt.dtype)], axis=0)
        part = jnp.dot(v_ones, p, preferred_element_type=jnp.float32)
        acc = part if acc is None else acc + part
        row += n
    l = acc[HEAD_DIM:HEAD_DIM + 1, :]
    if extra_logit is not None:
        l = l + jnp.exp2(extra_logit - m)
    return acc[:HEAD_DIM, :] / l


def _head_pipeline(n_tiles, scores_fn, pv_fn, s_refs, scores_first, per_pair=2):
    per_group = len(s_refs) // 2
    x_refs, y_refs = s_refs[:per_group], s_refs[per_group:]
    n_groups = n_tiles // per_group

    m = [scores_fn(t // per_pair, t % per_pair, ref) for t, ref in zip(range(per_group), x_refs)]
    for g in range(n_groups):
        src, dst = (x_refs, y_refs) if g % 2 == 0 else (y_refs, x_refs)
        nxt = [(g + 1) * per_group + j for j in range(per_group)] if g + 1 < n_groups else []
        m_next = [scores_fn(t // per_pair, t % per_pair, ref) for t, ref in zip(nxt, dst)] if scores_first else []
        for j in range(per_group):
            if not scores_first and nxt:
                m_next.append(scores_fn(nxt[j] // per_pair, nxt[j] % per_pair, dst[j]))
            t = g * per_group + j
            pv_fn(t // per_pair, t % per_pair, src[j], m[j])
        m = m_next


A_BANDS = (lambda kc, qc: kc >= qc,
           None,
           lambda kc, qc: kc <= qc)
A_LIVE = tuple(_live_rows(band, TQ) for band in A_BANDS)
A_FLAT = tuple((A_KBLOCKS - 1 - j) * TQ - (TQ - 1) >= A_REL_CLIP for j in range(A_KBLOCKS))
B_WIN = B_PREV + LANES
B_SUB = PAIR_W
B_BAND = lambda kc, qc: (kc >= qc) & (kc <= qc + B_LEFT_CHUNKS)
B_LIVE = (((0, B_WIN),) * (TQ // LANES),)


def _out_and_residual(zt_ref, wout_ref, res):
    y = lax.dot_general(zt_ref[...], wout_ref[...], _TN, preferred_element_type=jnp.float32)
    return res + y


def _attn_a_kernel(x_ref, qt_ref, kp_ref, ko_ref, vp_ref, vo_ref, gt_ref,
                   bvec_ref, wout_ref, kvw_ref, nwt_ref,
                   h_ref, ksh_ref, vsht_ref, qbt_ref, gbt_ref, zt_ref, bias_ref, *s_refs):
    i = pl.program_id(1)
    subs = TQA // TQ

    @pl.when((pl.program_id(0) == 0) & (i == 0))
    def _():
        _fill_bias_slabs(bvec_ref, bias_ref, TQ, A_BANDS, A_FLAT)
        bias_ref[N_HEADS * A_KBLOCKS] = jnp.full((TQ, TQ), MASKED, jnp.float32)

    n_slabs = N_HEADS * A_KBLOCKS
    rows = lax.broadcasted_iota(jnp.int32, (PAIR_W, TQ), 0)

    def key_block(sb, j):
        n = sb + j
        return n // subs, (n % subs) * TQ

    def scores_fn(sp, hh, s_ref):
        sb, p = divmod(sp, N_PAIRS)
        q2 = qt_ref[0, p, :, sb * TQ:(sb + 1) * TQ]
        own_rows = rows < HEAD_DIM if hh == 0 else rows >= HEAD_DIM
        qz = jnp.where(own_rows, q2, jnp.zeros_like(q2))
        k_tiles, bias_fns = [], []
        for j in range(A_KBLOCKS):
            in_seq = i * subs + sb - (A_KBLOCKS - 1) + j >= 0
            slab = jnp.where(in_seq, (2 * p + hh) * A_KBLOCKS + j, n_slabs)
            own, t0 = key_block(sb, j)
            k_tiles.append((ko_ref if own else kp_ref)[0, p, t0:t0 + TQ, :])
            bias_fns.append(lambda rws, lanes, slab=slab: bias_ref[slab, rws, lanes])
        return _scores_stage(k_tiles, [qz] * A_KBLOCKS, bias_fns, A_LIVE, s_ref)

    def pv_fn(sp, hh, s_ref, m):
        sb, p = divmod(sp, N_PAIRS)
        half = slice(hh * HEAD_DIM, (hh + 1) * HEAD_DIM)
        qcols = slice(sb * TQ, (sb + 1) * TQ)
        v_tiles = []
        for j in range(A_KBLOCKS):
            own, t0 = key_block(sb, j)
            v_tiles.append((vo_ref if own else vp_ref)[0, p, half, t0:t0 + TQ])
        o = _pv_stage(s_ref, m, v_tiles, A_LIVE)
        gate = gt_ref[0, p, half, qcols].astype(jnp.float32)
        row0 = p * PAIR_W + hh * HEAD_DIM
        zt_ref[row0:row0 + HEAD_DIM, qcols] = (o * gate).astype(jnp.bfloat16)

    _head_pipeline(subs * N_HEADS, scores_fn, pv_fn, s_refs, scores_first=True)

    h = _out_and_residual(zt_ref, wout_ref, x_ref[0])
    h_ref[0] = h
    hb = h.astype(jnp.bfloat16)
    r_col, r_row = _token_scales(h)
    kv = jnp.dot(hb, kvw_ref[...], preferred_element_type=jnp.float32) * r_col
    ksh_ref[0] = kv[:, :PAIR_W].astype(jnp.bfloat16)
    vsht_ref[0] = jnp.transpose(kv[:, PAIR_W:]).astype(jnp.bfloat16)
    for out_ref, base, act in ((gbt_ref, D_MODEL, _silu), (qbt_ref, 0, None)):
        for c in range(0, D_MODEL, FEAT_CHUNK):
            yt = lax.dot_general(nwt_ref[base + c:base + c + FEAT_CHUNK, :], hb, _NT,
                                 preferred_element_type=jnp.float32) * r_row
            out_ref[0, c:c + FEAT_CHUNK, :] = (yt if act is None else act(yt)).astype(jnp.bfloat16)


def _const_spec(shape):
    return pl.BlockSpec(shape, lambda bi, i: (0,) * len(shape), pipeline_mode=pl.Buffered(1))


def _attn_a(x, qt, k, vt, gt, bvec, wout, kvw, nwt):
    b, s, d = x.shape
    qt4 = qt.reshape(b, N_PAIRS, PAIR_W, s)
    vt4 = vt.reshape(b, N_PAIRS, PAIR_W, s)
    gt4 = gt.reshape(b, N_PAIRS, PAIR_W, s)
    assert (A_KBLOCKS - 1) * TQ <= TQA, "the keys before a grid block must fit in one previous block"
    prev = lambda i: jnp.maximum(i - 1, 0)

    def kspec(index):
        return pl.BlockSpec((1, N_PAIRS, TQA, PAIR_W), lambda bi, i: (bi, 0, index(i), 0))

    def vspec(index):
        return pl.BlockSpec((1, N_PAIRS, PAIR_W, TQA), lambda bi, i: (bi, 0, 0, index(i)))

    feat_spec = pl.BlockSpec((1, N_PAIRS, PAIR_W, TQA), lambda bi, i: (bi, 0, 0, i))
    featout_spec = pl.BlockSpec((1, d, TQA), lambda bi, i: (bi, 0, i))
    feat = jax.ShapeDtypeStruct((b, d, s), jnp.bfloat16)
    return pl.pallas_call(
        _attn_a_kernel,
        grid=(b, s // TQA),
        in_specs=[
            pl.BlockSpec((1, TQA, d), lambda bi, i: (bi, i, 0)),
            feat_spec,
            kspec(prev), kspec(lambda i: i),
            vspec(prev), vspec(lambda i: i),
            feat_spec,
            _const_spec(bvec.shape),
            _const_spec(wout.shape),
            _const_spec(kvw.shape),
            _const_spec(nwt.shape),
        ],
        out_specs=[
            pl.BlockSpec((1, TQA, d), lambda bi, i: (bi, i, 0)),
            pl.BlockSpec((1, TQA, PAIR_W), lambda bi, i: (bi, i, 0)),
            pl.BlockSpec((1, PAIR_W, TQA), lambda bi, i: (bi, 0, i)),
            featout_spec,
            featout_spec,
        ],
        out_shape=[
            jax.ShapeDtypeStruct((b, s, d), jnp.float32),
            jax.ShapeDtypeStruct((b, s, PAIR_W), jnp.bfloat16),
            jax.ShapeDtypeStruct((b, PAIR_W, s), jnp.bfloat16),
            feat, feat,
        ],
        scratch_shapes=[pltpu.VMEM((d, TQA), jnp.bfloat16),
                        pltpu.VMEM((N_HEADS * A_KBLOCKS + 1, TQ, TQ), jnp.float32),
                        *[pltpu.VMEM((A_KBLOCKS * TQ, TQ), jnp.float32)] * N_SCORE_BUFS],
        compiler_params=pltpu.CompilerParams(
            dimension_semantics=("arbitrary", "arbitrary"), vmem_limit_bytes=VMEM_LIMIT),
        name="attn_a",
    )(x, qt4, k, k, vt4, vt4, gt4, bvec, wout, kvw, nwt)


def _attn_b_kernel(h_ref, qt_ref, kp_ref, ko_ref, vp_ref, vo_ref, gt_ref, vec_ref,
                   sink_ref, wout_ref, fg_ref, out_ref, zt_ref, bias_ref, *s_refs):
    i = pl.program_id(1)

    @pl.when((pl.program_id(0) == 0) & (i == 0))
    def _():
        in_band = B_BAND(_chunk_iota((B_WIN, LANES), 0), _chunk_iota((B_WIN, LANES), 1))
        before_block = lax.broadcasted_iota(jnp.int32, (B_WIN, LANES), 0) < B_PREV

        def pair_body(p, carry):
            for hh in range(2):
                t = jnp.where(in_band, _toeplitz(vec_ref[0, pl.ds(2 * p + hh, 1), :], B_WIN)[:, :LANES], MASKED)
                lanes = slice(hh * LANES, (hh + 1) * LANES)
                bias_ref[2 * p, :, lanes] = t.astype(jnp.bfloat16)
                bias_ref[2 * p + 1, :, lanes] = jnp.where(before_block, MASKED, t).astype(jnp.bfloat16)
            return carry

        lax.fori_loop(0, N_PAIRS, pair_body, 0)

    rows = lax.broadcasted_iota(jnp.int32, (PAIR_W, TQ), 0)
    pairs_per_kv = N_PAIRS // B_KV_HEADS
    eye = _identity_bf16(B_SUB)
    head_lanes = lax.broadcasted_iota(jnp.int32, (1, TQ), 1)

    def scores_fn(p, u, s_ref):
        qlanes = slice(u * LANES, (u + 1) * LANES)
        q2 = jnp.concatenate([qt_ref[0, p, :HEAD_DIM, qlanes], qt_ref[0, p, HEAD_DIM:, qlanes]], axis=1)
        kv0 = HEAD_DIM * (p // pairs_per_kv)
        kv_rows = (rows >= kv0) & (rows < kv0 + HEAD_DIM)
        qz = jnp.where(kv_rows, jnp.concatenate([q2, q2], axis=0), jnp.zeros((PAIR_W, TQ), q2.dtype))
        if u == 0:
            k_tile = jnp.concatenate([kp_ref[0], ko_ref[0, :LANES, :]], axis=0)
            variant = jnp.where(i == 0, 1, 0)
        else:
            k_tile, variant = ko_ref[0, u * LANES - B_PREV:(u + 1) * LANES, :], 0

        lhs, rhs = [], []
        for r0 in range(0, B_WIN, B_SUB):
            lhs.append(jnp.concatenate([k_tile[r0:r0 + B_SUB, :], eye], axis=1))
            rhs.append(jnp.concatenate([qz, bias_ref[2 * p + variant, r0:r0 + B_SUB, :]], axis=0))
        sub_live = (((0, B_SUB),) * (TQ // LANES),) * len(lhs)
        return _scores_stage(lhs, rhs, [None] * len(lhs), sub_live, s_ref)

    def pv_fn(p, u, s_ref, m):
        qlanes = slice(u * LANES, (u + 1) * LANES)
        kv_rows = slice(HEAD_DIM * (p // pairs_per_kv), HEAD_DIM * (p // pairs_per_kv + 1))
        if u == 0:
            v_tile = jnp.concatenate([vp_ref[0, kv_rows, :], vo_ref[0, kv_rows, :LANES]], axis=1)
        else:
            v_tile = vo_ref[0, kv_rows, u * LANES - B_PREV:(u + 1) * LANES]
        sink = jnp.where(head_lanes < LANES, sink_ref[0, 2 * p], sink_ref[0, 2 * p + 1]) * LOG2E
        o = _pv_stage(s_ref, m, [v_tile], B_LIVE, extra_logit=sink)
        for hh in range(2):
            hrows = slice(hh * HEAD_DIM, (hh + 1) * HEAD_DIM)
            gate = gt_ref[0, p, hrows, qlanes].astype(jnp.float32)
            row0 = p * PAIR_W + hh * HEAD_DIM
            zt_ref[row0:row0 + HEAD_DIM, qlanes] = (o[:, hh * LANES:(hh + 1) * LANES] * gate).astype(jnp.bfloat16)

    _head_pipeline(N_PAIRS * (TQB // LANES), scores_fn, pv_fn, s_refs, scores_first=False, per_pair=TQB // LANES)

    h2 = _out_and_residual(zt_ref, wout_ref, h_ref[0])
    out_ref[0] = (h2 * fg_ref[...]) * _rms_scale(h2)


def _attn_b(h, qbt, ksh, vsht, gbt, vec, sinks, wout, fg):
    b, s, d = h.shape
    qt4 = qbt.reshape(b, N_PAIRS, PAIR_W, s)
    gt4 = gbt.reshape(b, N_PAIRS, PAIR_W, s)
    feat_spec = pl.BlockSpec((1, N_PAIRS, PAIR_W, TQB), lambda bi, i: (bi, 0, 0, i))
    prev = lambda i: jnp.maximum(i * (TQB // B_PREV) - 1, 0)
    return pl.pallas_call(
        _attn_b_kernel,
        grid=(b, s // TQB),
        in_specs=[
            pl.BlockSpec((1, TQB, d), lambda bi, i: (bi, i, 0)),
            feat_spec,
            pl.BlockSpec((1, B_PREV, PAIR_W), lambda bi, i: (bi, prev(i), 0)),
            pl.BlockSpec((1, TQB, PAIR_W), lambda bi, i: (bi, i, 0)),
            pl.BlockSpec((1, PAIR_W, B_PREV), lambda bi, i: (bi, 0, prev(i))),
            pl.BlockSpec((1, PAIR_W, TQB), lambda bi, i: (bi, 0, i)),
            feat_spec,
            _const_spec(vec.shape),
            pl.BlockSpec(memory_space=pltpu.SMEM),
            _const_spec(wout.shape),
            _const_spec(fg.shape),
        ],
        out_specs=pl.BlockSpec((1, TQB, d), lambda bi, i: (bi, i, 0)),
        out_shape=jax.ShapeDtypeStruct((b, s, d), jnp.float32),
        scratch_shapes=[pltpu.VMEM((d, TQB), jnp.bfloat16),
                        pltpu.VMEM((2 * N_PAIRS, B_WIN, TQ), jnp.bfloat16),
                        *[pltpu.VMEM((B_WIN, TQ), jnp.float32)] * N_SCORE_BUFS_B],
        compiler_params=pltpu.CompilerParams(
            dimension_semantics=("arbitrary", "arbitrary"), vmem_limit_bytes=VMEM_LIMIT),
        name="attn_b",
    )(h, qt4, ksh, ksh, vsht, vsht, gt4, vec, sinks, wout, fg)


def _slab_diff(xp=jnp):
    u = xp.arange(BIAS_PERIOD)
    return xp.where(u < TQ, u, u - BIAS_PERIOD)


def _static_take(table, idx):
    parts, lo = [], 0
    while lo < len(idx):
        hi = lo + 1
        step = idx[hi] - idx[lo] if hi < len(idx) else 0
        while step in (0, 1) and hi < len(idx) and idx[hi] - idx[hi - 1] == step:
            hi += 1
        rows = table[idx[lo]:idx[hi - 1] + 1]
        parts.append(jnp.broadcast_to(rows, (hi - lo,) + table.shape[1:]) if step == 0 else rows)
        lo = hi
    return jnp.concatenate(parts, axis=0)


def _bias_vecs_a(rel_bias):
    back = (A_KBLOCKS - 1 - np.arange(A_KBLOCKS))[:, None] * TQ
    dist = back + _slab_diff(np)[None, :]
    idx = np.clip(dist, -A_REL_CLIP, A_REL_CLIP) + A_REL_CLIP
    vec = jnp.stack([_static_take(rel_bias, [int(v) for v in row]) for row in idx])
    return jnp.transpose(vec, (0, 2, 1)).astype(jnp.float32)


def _t5_bucket(rel):
    nb = T5_BUCKETS // 2
    max_exact = nb // 2
    ret = jnp.where(rel > 0, nb, 0)
    n = jnp.abs(rel)
    nf = jnp.maximum(n, 1).astype(jnp.float32)
    large = max_exact + (jnp.log(nf / max_exact) / math.log(T5_MAX_DIST / max_exact)
                         * (nb - max_exact)).astype(jnp.int32)
    large = jnp.minimum(large, nb - 1)
    return ret + jnp.where(n < max_exact, n, large)


def _bias_vec_b(t5_table):
    rel = -_slab_diff() - B_PREV
    vec = jnp.transpose(t5_table[_t5_bucket(rel)], (1, 0)).astype(jnp.float32)
    return vec[None]


def kernel(x, a_norm, a_w_in, a_rel_bias, a_w_out, kv_norm, kv_w, t5_bias,
           b_norm, b_w_in, b_sinks, b_w_out, final_norm):
    assert a_norm.shape[0] == 1 and b_norm.shape[0] == 1, "one A layer then one B layer"
    scale = HEAD_DIM ** -0.5 * LOG2E

    def q_col_scale(n):
        return np.where(np.arange(n) < D_MODEL, scale, 1.0).astype(np.float32)

    wt_a = _weight_t(a_w_in[0], q_col_scale(a_w_in.shape[2]), a_norm[0], W_T_COLS_A)
    qt, k, vt, gt, nwt, kvw16, wo_a, wo_b = _proj_a(x, wt_a, kv_w, b_w_in[0], q_col_scale(b_w_in.shape[2]),
                                                    kv_norm, b_norm[0], a_w_out, b_w_out)
    h, ksh, vsht, qbt, gbt = _attn_a(
        x, qt, k, vt, gt, _bias_vecs_a(a_rel_bias[0] * LOG2E), wo_a, kvw16, nwt)

    return _attn_b(h, qbt, ksh, vsht, gbt, _bias_vec_b(t5_bias * LOG2E), b_sinks.astype(jnp.float32), wo_b,
                   final_norm[None, :])
```

```python
import functools
import math

import jax
import jax.numpy as jnp
import numpy as np
from jax import lax
from jax.experimental import pallas as pl
from jax.experimental.pallas import tpu as pltpu

D_MODEL = 1024
HEAD_DIM = 64
N_HEADS = D_MODEL // HEAD_DIM
N_PAIRS = N_HEADS // 2
PAIR_W = 2 * HEAD_DIM
CHUNK = 64
RMS_EPS = 1e-6
A_LEFT_CHUNKS = 8
A_REL_CLIP = 256
B_KV_HEADS = 2
B_LEFT_CHUNKS = 2
T5_BUCKETS = 32
T5_MAX_DIST = 128

TQ = 256
TQB = 512
TQA = 512
A_KBLOCKS = A_LEFT_CHUNKS * CHUNK // TQ + 1
B_PREV = B_LEFT_CHUNKS * CHUNK
TM_PROJ = 1024
FEAT_CHUNK = 256
W_T_COLS_A = 2048
W_T_COLS_B = 512
MASKED = -1e30
BIAS_PERIOD = 2 * TQ
SUBLANES = 8
LANES = 128
MAX_CHAINS = 1
BF16_ROWS = 16
LOG2E = math.log2(math.e)
N_SCORE_BUFS = 4
N_SCORE_BUFS_B = 8
VMEM_LIMIT = 56 * 1024 * 1024

_NT = (((1,), (1,)), ((), ()))
_TN = (((0,), (0,)), ((), ()))


def _rms_scale(xf):
    return lax.rsqrt(jnp.mean(xf * xf, axis=-1, keepdims=True) + RMS_EPS)


def _token_scales(xf):
    col = _rms_scale(xf)
    row = jnp.transpose(jnp.broadcast_to(col, (xf.shape[0], LANES)))[0:1, :]
    return col, row


def _silu(v):
    return v * jax.nn.sigmoid(v)


def _weight_t_kernel(w_ref, cs_ref, gain_ref, o_ref):
    g_col = jnp.transpose(jnp.broadcast_to(gain_ref[...], (LANES, w_ref.shape[0])))[:, :1]
    o_ref[...] = jnp.transpose(((w_ref[...] * cs_ref[...]) * g_col).astype(jnp.bfloat16))


def _weight_t(w, col_scale, row_gain, tc):
    d, n = w.shape
    return pl.pallas_call(
        _weight_t_kernel,
        grid=(n // tc,),
        in_specs=[
            pl.BlockSpec((d, tc), lambda j: (0, j)),
            pl.BlockSpec((1, tc), lambda j: (0, j)),
            pl.BlockSpec((1, d), lambda j: (0, 0)),
        ],
        out_specs=pl.BlockSpec((tc, d), lambda j: (j, 0)),
        out_shape=jax.ShapeDtypeStruct((n, d), jnp.bfloat16),
        compiler_params=pltpu.CompilerParams(dimension_semantics=("arbitrary",), vmem_limit_bytes=VMEM_LIMIT),
        name="weight_t",
    )(w, col_scale[None, :], row_gain[None, :])


def _proj_a_kernel(x_ref, wt_ref, kvw_ref, kvg_ref, bw_ref, cs_ref, bg_ref, woa_ref, wob_ref,
                   qt_ref, k_ref, vt_ref, gt_ref, nwt_ref, kvw16_ref, woa16_ref, wob16_ref):
    nwt_ref[...] = ((bw_ref[...] * cs_ref[...]).T * bg_ref[...]).astype(jnp.bfloat16)
    g_rows = jnp.transpose(jnp.broadcast_to(kvg_ref[...], (2 * PAIR_W, D_MODEL)))
    kvw16_ref[...] = (kvw_ref[...] * g_rows).astype(jnp.bfloat16)
    woa16_ref[...] = woa_ref[0].astype(jnp.bfloat16)
    wob16_ref[...] = wob_ref[0].astype(jnp.bfloat16)

    xf = x_ref[0]
    xb = xf.astype(jnp.bfloat16)
    r_col, r_row = _token_scales(xf)
    k = lax.dot_general(xb, wt_ref[D_MODEL:2 * D_MODEL, :], _NT,
                        preferred_element_type=jnp.float32) * r_col
    for p in range(N_PAIRS):
        k_ref[0, p] = k[:, p * PAIR_W:(p + 1) * PAIR_W].astype(jnp.bfloat16)
    for out_ref, base, act in ((gt_ref, 3 * D_MODEL, _silu), (vt_ref, 2 * D_MODEL, None), (qt_ref, 0, None)):
        for c in range(0, D_MODEL, FEAT_CHUNK):
            w = wt_ref[base + c:base + c + FEAT_CHUNK, :]
            yt = lax.dot_general(w, xb, _NT, preferred_element_type=jnp.float32) * r_row
            out_ref[0, c:c + FEAT_CHUNK, :] = (yt if act is None else act(yt)).astype(jnp.bfloat16)


def _proj_a(x, wt, kv_w, b_w, b_col_scale, kv_gain, b_gain, a_w_out, b_w_out):
    b, s, d = x.shape
    n_i = s // TM_PROJ
    wo_rows = d // (b * n_i)
    step = lambda bi, i: bi * n_i + i
    n_blocks = b_w.shape[1] // W_T_COLS_B
    assert n_blocks <= b * n_i and kv_w.shape == (d, 2 * PAIR_W) and d % (b * n_i * BF16_ROWS) == 0
    blk = lambda bi, i: jnp.minimum(bi * n_i + i, n_blocks - 1)
    feat = jax.ShapeDtypeStruct((b, d, s), jnp.bfloat16)
    return pl.pallas_call(
        _proj_a_kernel,
        grid=(b, n_i),
        in_specs=[
            pl.BlockSpec((1, TM_PROJ, d), lambda bi, i: (bi, i, 0)),
            _const_spec(wt.shape),
            _const_spec(kv_w.shape),
            _const_spec((1, d)),
            pl.BlockSpec((d, W_T_COLS_B), lambda bi, i: (0, blk(bi, i))),
            pl.BlockSpec((1, W_T_COLS_B), lambda bi, i: (0, blk(bi, i))),
            _const_spec((1, d)),
            pl.BlockSpec((1, wo_rows, d), lambda bi, i: (0, step(bi, i), 0)),
            pl.BlockSpec((1, wo_rows, d), lambda bi, i: (0, step(bi, i), 0)),
        ],
        out_specs=[
            pl.BlockSpec((1, d, TM_PROJ), lambda bi, i: (bi, 0, i)),
            pl.BlockSpec((1, N_PAIRS, TM_PROJ, PAIR_W), lambda bi, i: (bi, 0, i, 0)),
            pl.BlockSpec((1, d, TM_PROJ), lambda bi, i: (bi, 0, i)),
            pl.BlockSpec((1, d, TM_PROJ), lambda bi, i: (bi, 0, i)),
            pl.BlockSpec((W_T_COLS_B, d), lambda bi, i: (blk(bi, i), 0)),
            pl.BlockSpec((d, 2 * PAIR_W), lambda bi, i: (0, 0)),
            pl.BlockSpec((wo_rows, d), lambda bi, i: (step(bi, i), 0)),
            pl.BlockSpec((wo_rows, d), lambda bi, i: (step(bi, i), 0)),
        ],
        out_shape=[feat, jax.ShapeDtypeStruct((b, N_PAIRS, s, PAIR_W), jnp.bfloat16), feat, feat,
                   jax.ShapeDtypeStruct((n_blocks * W_T_COLS_B, d), jnp.bfloat16),
                   jax.ShapeDtypeStruct((d, 2 * PAIR_W), jnp.bfloat16),
                   jax.ShapeDtypeStruct((d, d), jnp.bfloat16), jax.ShapeDtypeStruct((d, d), jnp.bfloat16)],
        compiler_params=pltpu.CompilerParams(
            dimension_semantics=("arbitrary", "arbitrary"), vmem_limit_bytes=VMEM_LIMIT),
        name="proj_a",
    )(x, wt, kv_w, kv_gain[None, :], b_w, b_col_scale[None, :], b_gain[None, :], a_w_out, b_w_out)


def _chunk_iota(shape, axis):
    return lax.shift_right_logical(lax.broadcasted_iota(jnp.int32, shape, axis), CHUNK.bit_length() - 1)


def _identity_bf16(n):
    return (lax.broadcasted_iota(jnp.int32, (n, n), 0) == lax.broadcasted_iota(jnp.int32, (n, n), 1)).astype(jnp.bfloat16)


def _toeplitz(vec_row, n_keys):
    return pltpu.roll(jnp.broadcast_to(vec_row, (n_keys, BIAS_PERIOD)), 0, 1, stride=1, stride_axis=0)


def _fill_bias_slabs(vec_ref, bias_ref, n_keys, bands, flat):
    n = len(bands)
    kch = _chunk_iota((n_keys, TQ), 0)
    qch = _chunk_iota((n_keys, TQ), 1)

    def head_body(h, carry):
        for j, band in enumerate(bands):
            row = vec_ref[j, pl.ds(h, 1), :]
            t = (jnp.broadcast_to(row, (n_keys, BIAS_PERIOD)) if flat[j] else _toeplitz(row, n_keys))[:, :TQ]
            if band is not None:
                t = jnp.where(band(kch, qch), t, MASKED)
            bias_ref[h * n + j] = t
        return carry

    lax.fori_loop(0, N_HEADS, head_body, 0)


def _live_rows(band, n_keys):
    chunks_per_half = LANES // CHUNK
    out = []
    for half in range(TQ // LANES):
        qchs = range(half * chunks_per_half, (half + 1) * chunks_per_half)
        live = [kc for kc in range(n_keys // CHUNK) if band is None or any(band(kc, qc) for qc in qchs)]
        out.append((live[0] * CHUNK, (live[-1] + 1) * CHUNK) if live else (0, 0))
    return tuple(out)


def _scores_stage(k_tiles, q_tiles, bias_fns, live, s_ref):
    maxes, row = [], 0
    for half in range(TQ // LANES):
        maxes.append([None] * MAX_CHAINS)
    for kt, qz, bias_fn, live_j in zip(k_tiles, q_tiles, bias_fns, live):
        s = jnp.dot(kt, qz, preferred_element_type=jnp.float32)
        for r in range(s.shape[0] // SUBLANES):
            rows = slice(r * SUBLANES, (r + 1) * SUBLANES)
            for half, (r0, r1) in enumerate(live_j):
                if not r0 <= r * SUBLANES < r1:
                    continue
                lanes = slice(half * LANES, (half + 1) * LANES)
                acc = maxes[half]
                grp = s[rows, lanes] if bias_fn is None else s[rows, lanes] + bias_fn(rows, lanes)
                s_ref[row + r * SUBLANES:row + (r + 1) * SUBLANES, lanes] = grp
                c = r % MAX_CHAINS
                acc[c] = grp if acc[c] is None else jnp.maximum(acc[c], grp)
        row += s.shape[0]
    cols = [jnp.max(functools.reduce(jnp.maximum, [a for a in acc if a is not None]), axis=0, keepdims=True)
            for acc in maxes]
    return jnp.concatenate(cols, axis=1)


def _pv_stage(s_ref, m, v_tiles, live, extra_logit=None, row_reciprocal=False):
    if extra_logit is not None:
        m = jnp.maximum(m, extra_logit)
    acc, row = None, 0
    for vt, live_j in zip(v_tiles, live):
        n = vt.shape[1]
        halves = []
        for half, (r0, r1) in enumerate(live_j):
            lanes = slice(half * LANES, (half + 1) * LANES)
            parts = [jnp.zeros((r0, LANES), jnp.bfloat16)] if r0 else []
            if r1 > r0:
                parts.append(jnp.exp2(s_ref[row + r0:row + r1, lanes] - m[:, lanes]).astype(jnp.bfloat16))
            if n > r1:
                parts.append(jnp.zeros((n - r1, LANES), jnp.bfloat16))
            halves.append(parts[0] if len(parts) == 1 else jnp.concatenate(parts, axis=0))
        p = jnp.concatenate(halves, axis=1)
        v_ones = jnp.concatenate([vt, jnp.ones((BF16_ROWS, n), vt.dtype)], axis=0)
        part = jnp.dot(v_ones, p, preferred_element_type=jnp.float32)
        acc = part if acc is None else acc + part
        row += n
    l = acc[HEAD_DIM:HEAD_DIM + 1, :]
    if extra_logit is not None:
        l = l + jnp.exp2(extra_logit - m)
    return acc[:HEAD_DIM, :] * (1.0 / l) if row_reciprocal else acc[:HEAD_DIM, :] / l


def _head_pipeline(n_tiles, scores_fn, pv_fn, s_refs, scores_first, per_pair=2):
    per_group = len(s_refs) // 2
    x_refs, y_refs = s_refs[:per_group], s_refs[per_group:]
    n_groups = n_tiles // per_group

    m = [scores_fn(t // per_pair, t % per_pair, ref) for t, ref in zip(range(per_group), x_refs)]
    for g in range(n_groups):
        src, dst = (x_refs, y_refs) if g % 2 == 0 else (y_refs, x_refs)
        nxt = [(g + 1) * per_group + j for j in range(per_group)] if g + 1 < n_groups else []
        m_next = [scores_fn(t // per_pair, t % per_pair, ref) for t, ref in zip(nxt, dst)] if scores_first else []
        for j in range(per_group):
            if not scores_first and nxt:
                m_next.append(scores_fn(nxt[j] // per_pair, nxt[j] % per_pair, dst[j]))
            t = g * per_group + j
            pv_fn(t // per_pair, t % per_pair, src[j], m[j])
        m = m_next


A_BANDS = (lambda kc, qc: kc >= qc,
           None,
           lambda kc, qc: kc <= qc)
A_LIVE = tuple(_live_rows(band, TQ) for band in A_BANDS)
A_FLAT = tuple((A_KBLOCKS - 1 - j) * TQ - (TQ - 1) >= A_REL_CLIP for j in range(A_KBLOCKS))
B_WIN = B_PREV + LANES
B_SUB = PAIR_W
B_BAND = lambda kc, qc: (kc >= qc) & (kc <= qc + B_LEFT_CHUNKS)
B_LIVE = (((0, B_WIN),) * (TQ // LANES),)


def _out_and_residual(zt_ref, wout_ref, res):
    y = lax.dot_general(zt_ref[...], wout_ref[...], _TN, preferred_element_type=jnp.float32)
    return res + y


def _attn_a_kernel(x_ref, qt_ref, kp_ref, ko_ref, vp_ref, vo_ref, gt_ref,
                   bvec_ref, wout_ref, kvw_ref, nwt_ref,
                   h_ref, ksh_ref, vsht_ref, qbt_ref, gbt_ref, zt_ref, bias_ref, *s_refs):
    i = pl.program_id(1)
    subs = TQA // TQ

    @pl.when((pl.program_id(0) == 0) & (i == 0))
    def _():
        _fill_bias_slabs(bvec_ref, bias_ref, TQ, A_BANDS, A_FLAT)
        bias_ref[N_HEADS * A_KBLOCKS] = jnp.full((TQ, TQ), MASKED, jnp.float32)

    n_slabs = N_HEADS * A_KBLOCKS
    rows = lax.broadcasted_iota(jnp.int32, (PAIR_W, TQ), 0)

    def key_block(sb, j):
        n = sb + j
        return n // subs, (n % subs) * TQ

    def scores_fn(sp, hh, s_ref):
        sb, p = divmod(sp, N_PAIRS)
        q2 = qt_ref[0, p, :, sb * TQ:(sb + 1) * TQ]
        own_rows = rows < HEAD_DIM if hh == 0 else rows >= HEAD_DIM
        qz = jnp.where(own_rows, q2, jnp.zeros_like(q2))
        k_tiles, bias_fns = [], []
        for j in range(A_KBLOCKS):
            in_seq = i * subs + sb - (A_KBLOCKS - 1) + j >= 0
            slab = jnp.where(in_seq, (2 * p + hh) * A_KBLOCKS + j, n_slabs)
            own, t0 = key_block(sb, j)
            k_tiles.append((ko_ref if own else kp_ref)[0, p, t0:t0 + TQ, :])
            bias_fns.append(lambda rws, lanes, slab=slab: bias_ref[slab, rws, lanes])
        return _scores_stage(k_tiles, [qz] * A_KBLOCKS, bias_fns, A_LIVE, s_ref)

    def pv_fn(sp, hh, s_ref, m):
        sb, p = divmod(sp, N_PAIRS)
        half = slice(hh * HEAD_DIM, (hh + 1) * HEAD_DIM)
        qcols = slice(sb * TQ, (sb + 1) * TQ)
        v_tiles = []
        for j in range(A_KBLOCKS):
            own, t0 = key_block(sb, j)
            v_tiles.append((vo_ref if own else vp_ref)[0, p, half, t0:t0 + TQ])
        o = _pv_stage(s_ref, m, v_tiles, A_LIVE, row_reciprocal=True)
        gate = gt_ref[0, p, half, qcols].astype(jnp.float32)
        row0 = p * PAIR_W + hh * HEAD_DIM
        zt_ref[row0:row0 + HEAD_DIM, qcols] = (o * gate).astype(jnp.bfloat16)

    _head_pipeline(subs * N_HEADS, scores_fn, pv_fn, s_refs, scores_first=True)

    h = _out_and_residual(zt_ref, wout_ref, x_ref[0])
    h_ref[0] = h
    hb = h.astype(jnp.bfloat16)
    r_col, r_row = _token_scales(h)
    kv = jnp.dot(hb, kvw_ref[...], preferred_element_type=jnp.float32) * r_col
    ksh_ref[0] = kv[:, :PAIR_W].astype(jnp.bfloat16)
    vsht_ref[0] = jnp.transpose(kv[:, PAIR_W:]).astype(jnp.bfloat16)
    for out_ref, base, act in ((gbt_ref, D_MODEL, _silu), (qbt_ref, 0, None)):
        for c in range(0, D_MODEL, FEAT_CHUNK):
            yt = lax.dot_general(nwt_ref[base + c:base + c + FEAT_CHUNK, :], hb, _NT,
                                 preferred_element_type=jnp.float32) * r_row
            out_ref[0, c:c + FEAT_CHUNK, :] = (yt if act is None else act(yt)).astype(jnp.bfloat16)


def _const_spec(shape):
    return pl.BlockSpec(shape, lambda bi, i: (0,) * len(shape), pipeline_mode=pl.Buffered(1))


def _attn_a(x, qt, k, vt, gt, bvec, wout, kvw, nwt):
    b, s, d = x.shape
    qt4 = qt.reshape(b, N_PAIRS, PAIR_W, s)
    vt4 = vt.reshape(b, N_PAIRS, PAIR_W, s)
    gt4 = gt.reshape(b, N_PAIRS, PAIR_W, s)
    assert (A_KBLOCKS - 1) * TQ <= TQA, "the keys before a grid block must fit in one previous block"
    prev = lambda i: jnp.maximum(i - 1, 0)

    def kspec(index):
        return pl.BlockSpec((1, N_PAIRS, TQA, PAIR_W), lambda bi, i: (bi, 0, index(i), 0))

    def vspec(index):
        return pl.BlockSpec((1, N_PAIRS, PAIR_W, TQA), lambda bi, i: (bi, 0, 0, index(i)))

    feat_spec = pl.BlockSpec((1, N_PAIRS, PAIR_W, TQA), lambda bi, i: (bi, 0, 0, i))
    featout_spec = pl.BlockSpec((1, d, TQA), lambda bi, i: (bi, 0, i))
    feat = jax.ShapeDtypeStruct((b, d, s), jnp.bfloat16)
    return pl.pallas_call(
        _attn_a_kernel,
        grid=(b, s // TQA),
        in_specs=[
            pl.BlockSpec((1, TQA, d), lambda bi, i: (bi, i, 0)),
            feat_spec,
            kspec(prev), kspec(lambda i: i),
            vspec(prev), vspec(lambda i: i),
            feat_spec,
            _const_spec(bvec.shape),
            _const_spec(wout.shape),
            _const_spec(kvw.shape),
            _const_spec(nwt.shape),
        ],
        out_specs=[
            pl.BlockSpec((1, TQA, d), lambda bi, i: (bi, i, 0)),
            pl.BlockSpec((1, TQA, PAIR_W), lambda bi, i: (bi, i, 0)),
            pl.BlockSpec((1, PAIR_W, TQA), lambda bi, i: (bi, 0, i)),
            featout_spec,
            featout_spec,
        ],
        out_shape=[
            jax.ShapeDtypeStruct((b, s, d), jnp.float32),
            jax.ShapeDtypeStruct((b, s, PAIR_W), jnp.bfloat16),
            jax.ShapeDtypeStruct((b, PAIR_W, s), jnp.bfloat16),
            feat, feat,
        ],
        scratch_shapes=[pltpu.VMEM((d, TQA), jnp.bfloat16),
                        pltpu.VMEM((N_HEADS * A_KBLOCKS + 1, TQ, TQ), jnp.float32),
                        *[pltpu.VMEM((A_KBLOCKS * TQ, TQ), jnp.float32)] * N_SCORE_BUFS],
        compiler_params=pltpu.CompilerParams(
            dimension_semantics=("arbitrary", "arbitrary"), vmem_limit_bytes=VMEM_LIMIT),
        name="attn_a",
    )(x, qt4, k, k, vt4, vt4, gt4, bvec, wout, kvw, nwt)


def _attn_b_kernel(h_ref, qt_ref, kp_ref, ko_ref, vp_ref, vo_ref, gt_ref, vec_ref,
                   sink_ref, wout_ref, fg_ref, out_ref, zt_ref, bias_ref, *s_refs):
    i = pl.program_id(1)

    @pl.when((pl.program_id(0) == 0) & (i == 0))
    def _():
        in_band = B_BAND(_chunk_iota((B_WIN, LANES), 0), _chunk_iota((B_WIN, LANES), 1))
        before_block = lax.broadcasted_iota(jnp.int32, (B_WIN, LANES), 0) < B_PREV

        def pair_body(p, carry):
            for hh in range(2):
                t = jnp.where(in_band, _toeplitz(vec_ref[0, pl.ds(2 * p + hh, 1), :], B_WIN)[:, :LANES], MASKED)
                lanes = slice(hh * LANES, (hh + 1) * LANES)
                bias_ref[2 * p, :, lanes] = t.astype(jnp.bfloat16)
                bias_ref[2 * p + 1, :, lanes] = jnp.where(before_block, MASKED, t).astype(jnp.bfloat16)
            return carry

        lax.fori_loop(0, N_PAIRS, pair_body, 0)

    rows = lax.broadcasted_iota(jnp.int32, (PAIR_W, TQ), 0)
    pairs_per_kv = N_PAIRS // B_KV_HEADS
    eye = _identity_bf16(B_SUB)
    head_lanes = lax.broadcasted_iota(jnp.int32, (1, TQ), 1)

    def scores_fn(p, u, s_ref):
        qlanes = slice(u * LANES, (u + 1) * LANES)
        q2 = jnp.concatenate([qt_ref[0, p, :HEAD_DIM, qlanes], qt_ref[0, p, HEAD_DIM:, qlanes]], axis=1)
        kv0 = HEAD_DIM * (p // pairs_per_kv)
        kv_rows = (rows >= kv0) & (rows < kv0 + HEAD_DIM)
        qz = jnp.where(kv_rows, jnp.concatenate([q2, q2], axis=0), jnp.zeros((PAIR_W, TQ), q2.dtype))
        if u == 0:
            k_tile = jnp.concatenate([kp_ref[0], ko_ref[0, :LANES, :]], axis=0)
            variant = jnp.where(i == 0, 1, 0)
        else:
            k_tile, variant = ko_ref[0, u * LANES - B_PREV:(u + 1) * LANES, :], 0

        lhs, rhs = [], []
        for r0 in range(0, B_WIN, B_SUB):
            lhs.append(jnp.concatenate([k_tile[r0:r0 + B_SUB, :], eye], axis=1))
            rhs.append(jnp.concatenate([qz, bias_ref[2 * p + variant, r0:r0 + B_SUB, :]], axis=0))
        sub_live = (((0, B_SUB),) * (TQ // LANES),) * len(lhs)
        return _scores_stage(lhs, rhs, [None] * len(lhs), sub_live, s_ref)

    def pv_fn(p, u, s_ref, m):
        qlanes = slice(u * LANES, (u + 1) * LANES)
        kv_rows = slice(HEAD_DIM * (p // pairs_per_kv), HEAD_DIM * (p // pairs_per_kv + 1))
        if u == 0:
            v_tile = jnp.concatenate([vp_ref[0, kv_rows, :], vo_ref[0, kv_rows, :LANES]], axis=1)
        else:
            v_tile = vo_ref[0, kv_rows, u * LANES - B_PREV:(u + 1) * LANES]
        sink = jnp.where(head_lanes < LANES, sink_ref[0, 2 * p], sink_ref[0, 2 * p + 1]) * LOG2E
        o = _pv_stage(s_ref, m, [v_tile], B_LIVE, extra_logit=sink)
        for hh in range(2):
            hrows = slice(hh * HEAD_DIM, (hh + 1) * HEAD_DIM)
            gate = gt_ref[0, p, hrows, qlanes].astype(jnp.float32)
            row0 = p * PAIR_W + hh * HEAD_DIM
            zt_ref[row0:row0 + HEAD_DIM, qlanes] = (o[:, hh * LANES:(hh + 1) * LANES] * gate).astype(jnp.bfloat16)

    _head_pipeline(N_PAIRS * (TQB // LANES), scores_fn, pv_fn, s_refs, scores_first=False, per_pair=TQB // LANES)

    h2 = _out_and_residual(zt_ref, wout_ref, h_ref[0])
    out_ref[0] = (h2 * fg_ref[...]) * _rms_scale(h2)


def _attn_b(h, qbt, ksh, vsht, gbt, vec, sinks, wout, fg):
    b, s, d = h.shape
    qt4 = qbt.reshape(b, N_PAIRS, PAIR_W, s)
    gt4 = gbt.reshape(b, N_PAIRS, PAIR_W, s)
    feat_spec = pl.BlockSpec((1, N_PAIRS, PAIR_W, TQB), lambda bi, i: (bi, 0, 0, i))
    prev = lambda i: jnp.maximum(i * (TQB // B_PREV) - 1, 0)
    return pl.pallas_call(
        _attn_b_kernel,
        grid=(b, s // TQB),
        in_specs=[
            pl.BlockSpec((1, TQB, d), lambda bi, i: (bi, i, 0)),
            feat_spec,
            pl.BlockSpec((1, B_PREV, PAIR_W), lambda bi, i: (bi, prev(i), 0)),
            pl.BlockSpec((1, TQB, PAIR_W), lambda bi, i: (bi, i, 0)),
            pl.BlockSpec((1, PAIR_W, B_PREV), lambda bi, i: (bi, 0, prev(i))),
            pl.BlockSpec((1, PAIR_W, TQB), lambda bi, i: (bi, 0, i)),
            feat_spec,
            _const_spec(vec.shape),
            pl.BlockSpec(memory_space=pltpu.SMEM),
            _const_spec(wout.shape),
            _const_spec(fg.shape),
        ],
        out_specs=pl.BlockSpec((1, TQB, d), lambda bi, i: (bi, i, 0)),
        out_shape=jax.ShapeDtypeStruct((b, s, d), jnp.float32),
        scratch_shapes=[pltpu.VMEM((d, TQB), jnp.bfloat16),
                        pltpu.VMEM((2 * N_PAIRS, B_WIN, TQ), jnp.bfloat16),
                        *[pltpu.VMEM((B_WIN, TQ), jnp.float32)] * N_SCORE_BUFS_B],
        compiler_params=pltpu.CompilerParams(
            dimension_semantics=("arbitrary", "arbitrary"), vmem_limit_bytes=VMEM_LIMIT),
        name="attn_b",
    )(h, qt4, ksh, ksh, vsht, vsht, gt4, vec, sinks, wout, fg)


def _slab_diff(xp=jnp):
    u = xp.arange(BIAS_PERIOD)
    return xp.where(u < TQ, u, u - BIAS_PERIOD)


def _static_take(table, idx):
    parts, lo = [], 0
    while lo < len(idx):
        hi = lo + 1
        step = idx[hi] - idx[lo] if hi < len(idx) else 0
        while step in (0, 1) and hi < len(idx) and idx[hi] - idx[hi - 1] == step:
            hi += 1
        rows = table[idx[lo]:idx[hi - 1] + 1]
        parts.append(jnp.broadcast_to(rows, (hi - lo,) + table.shape[1:]) if step == 0 else rows)
        lo = hi
    return jnp.concatenate(parts, axis=0)


def _bias_vecs_a(rel_bias):
    back = (A_KBLOCKS - 1 - np.arange(A_KBLOCKS))[:, None] * TQ
    dist = back + _slab_diff(np)[None, :]
    idx = np.clip(dist, -A_REL_CLIP, A_REL_CLIP) + A_REL_CLIP
    vec = jnp.stack([_static_take(rel_bias, [int(v) for v in row]) for row in idx])
    return jnp.transpose(vec, (0, 2, 1)).astype(jnp.float32)


def _t5_bucket(rel):
    nb = T5_BUCKETS // 2
    max_exact = nb // 2
    ret = jnp.where(rel > 0, nb, 0)
    n = jnp.abs(rel)
    nf = jnp.maximum(n, 1).astype(jnp.float32)
    large = max_exact + (jnp.log(nf / max_exact) / math.log(T5_MAX_DIST / max_exact)
                         * (nb - max_exact)).astype(jnp.int32)
    large = jnp.minimum(large, nb - 1)
    return ret + jnp.where(n < max_exact, n, large)


def _bias_vec_b(t5_table):
    rel = -_slab_diff() - B_PREV
    vec = jnp.transpose(t5_table[_t5_bucket(rel)], (1, 0)).astype(jnp.float32)
    return vec[None]


def kernel(x, a_norm, a_w_in, a_rel_bias, a_w_out, kv_norm, kv_w, t5_bias,
           b_norm, b_w_in, b_sinks, b_w_out, final_norm):
    assert a_norm.shape[0] == 1 and b_norm.shape[0] == 1, "one A layer then one B layer"
    scale = HEAD_DIM ** -0.5 * LOG2E

    def q_col_scale(n):
        return np.where(np.arange(n) < D_MODEL, scale, 1.0).astype(np.float32)

    wt_a = _weight_t(a_w_in[0], q_col_scale(a_w_in.shape[2]), a_norm[0], W_T_COLS_A)
    qt, k, vt, gt, nwt, kvw16, wo_a, wo_b = _proj_a(x, wt_a, kv_w, b_w_in[0], q_col_scale(b_w_in.shape[2]),
                                                    kv_norm, b_norm[0], a_w_out, b_w_out)
    h, ksh, vsht, qbt, gbt = _attn_a(
        x, qt, k, vt, gt, _bias_vecs_a(a_rel_bias[0] * LOG2E), wo_a, kvw16, nwt)

    return _attn_b(h, qbt, ksh, vsht, gbt, _bias_vec_b(t5_bias * LOG2E), b_sinks.astype(jnp.float32), wo_b,
                   final_norm[None, :])
```

```python
import functools
import math

import jax
import jax.numpy as jnp
import numpy as np
from jax import lax
from jax.experimental import pallas as pl
from jax.experimental.pallas import tpu as pltpu

D_MODEL = 1024
HEAD_DIM = 64
N_HEADS = D_MODEL // HEAD_DIM
N_PAIRS = N_HEADS // 2
PAIR_W = 2 * HEAD_DIM
CHUNK = 64
RMS_EPS = 1e-6
A_LEFT_CHUNKS = 8
A_REL_CLIP = 256
B_KV_HEADS = 2
B_LEFT_CHUNKS = 2
T5_BUCKETS = 32
T5_MAX_DIST = 128

TQ = 256
TQB = 512
TQA = 512
A_KBLOCKS = A_LEFT_CHUNKS * CHUNK // TQ + 1
B_PREV = B_LEFT_CHUNKS * CHUNK
TM_PROJ = 1024
FEAT_CHUNK = 256
W_T_COLS_A = 2048
W_T_COLS_B = 512
MASKED = -1e30
BIAS_PERIOD = 2 * TQ
SUBLANES = 8
LANES = 128
MAX_CHAINS = 1
BF16_ROWS = 16
LOG2E = math.log2(math.e)
N_SCORE_BUFS = 4
N_SCORE_BUFS_B = 8
VMEM_LIMIT = 56 * 1024 * 1024

_NT = (((1,), (1,)), ((), ()))
_TN = (((0,), (0,)), ((), ()))


def _rms_scale(xf):
    return lax.rsqrt(jnp.mean(xf * xf, axis=-1, keepdims=True) + RMS_EPS)


def _token_scales(xf):
    col = _rms_scale(xf)
    row = jnp.transpose(jnp.broadcast_to(col, (xf.shape[0], LANES)))[0:1, :]
    return col, row


def _silu(v):
    return v * jax.nn.sigmoid(v)


def _weight_t_kernel(w_ref, cs_ref, gain_ref, o_ref):
    g_col = jnp.transpose(jnp.broadcast_to(gain_ref[...], (LANES, w_ref.shape[0])))[:, :1]
    o_ref[...] = jnp.transpose(((w_ref[...] * cs_ref[...]) * g_col).astype(jnp.bfloat16))


def _weight_t(w, col_scale, row_gain, tc):
    d, n = w.shape
    return pl.pallas_call(
        _weight_t_kernel,
        grid=(n // tc,),
        in_specs=[
            pl.BlockSpec((d, tc), lambda j: (0, j)),
            pl.BlockSpec((1, tc), lambda j: (0, j)),
            pl.BlockSpec((1, d), lambda j: (0, 0)),
        ],
        out_specs=pl.BlockSpec((tc, d), lambda j: (j, 0)),
        out_shape=jax.ShapeDtypeStruct((n, d), jnp.bfloat16),
        compiler_params=pltpu.CompilerParams(dimension_semantics=("arbitrary",), vmem_limit_bytes=VMEM_LIMIT),
        name="weight_t",
    )(w, col_scale[None, :], row_gain[None, :])


def _proj_a_kernel(x_ref, wt_ref, kvw_ref, kvg_ref, bw_ref, cs_ref, bg_ref, woa_ref, wob_ref,
                   qt_ref, k_ref, vt_ref, gt_ref, nwt_ref, kvw16_ref, woa16_ref, wob16_ref):
    nwt_ref[...] = ((bw_ref[...] * cs_ref[...]).T * bg_ref[...]).astype(jnp.bfloat16)
    g_rows = jnp.transpose(jnp.broadcast_to(kvg_ref[...], (2 * PAIR_W, D_MODEL)))
    kvw16_ref[...] = (kvw_ref[...] * g_rows).astype(jnp.bfloat16)
    woa16_ref[...] = woa_ref[0].astype(jnp.bfloat16)
    wob16_ref[...] = wob_ref[0].astype(jnp.bfloat16)

    xf = x_ref[0]
    xb = xf.astype(jnp.bfloat16)
    r_col, r_row = _token_scales(xf)
    k = lax.dot_general(xb, wt_ref[D_MODEL:2 * D_MODEL, :], _NT,
                        preferred_element_type=jnp.float32) * r_col
    for p in range(N_PAIRS):
        k_ref[0, p] = k[:, p * PAIR_W:(p + 1) * PAIR_W].astype(jnp.bfloat16)
    for out_ref, base, act in ((gt_ref, 3 * D_MODEL, _silu), (vt_ref, 2 * D_MODEL, None), (qt_ref, 0, None)):
        for c in range(0, D_MODEL, FEAT_CHUNK):
            w = wt_ref[base + c:base + c + FEAT_CHUNK, :]
            yt = lax.dot_general(w, xb, _NT, preferred_element_type=jnp.float32) * r_row
            out_ref[0, c:c + FEAT_CHUNK, :] = (yt if act is None else act(yt)).astype(jnp.bfloat16)


def _proj_a(x, wt, kv_w, b_w, b_col_scale, kv_gain, b_gain, a_w_out, b_w_out):
    b, s, d = x.shape
    n_i = s // TM_PROJ
    wo_rows = d // (b * n_i)
    step = lambda bi, i: bi * n_i + i
    n_blocks = b_w.shape[1] // W_T_COLS_B
    assert n_blocks <= b * n_i and kv_w.shape == (d, 2 * PAIR_W) and d % (b * n_i * BF16_ROWS) == 0
    blk = lambda bi, i: jnp.minimum(bi * n_i + i, n_blocks - 1)
    feat = jax.ShapeDtypeStruct((b, d, s), jnp.bfloat16)
    return pl.pallas_call(
        _proj_a_kernel,
        grid=(b, n_i),
        in_specs=[
            pl.BlockSpec((1, TM_PROJ, d), lambda bi, i: (bi, i, 0)),
            _const_spec(wt.shape),
            _const_spec(kv_w.shape),
            _const_spec((1, d)),
            pl.BlockSpec((d, W_T_COLS_B), lambda bi, i: (0, blk(bi, i))),
            pl.BlockSpec((1, W_T_COLS_B), lambda bi, i: (0, blk(bi, i))),
            _const_spec((1, d)),
            pl.BlockSpec((1, wo_rows, d), lambda bi, i: (0, step(bi, i), 0)),
            pl.BlockSpec((1, wo_rows, d), lambda bi, i: (0, step(bi, i), 0)),
        ],
        out_specs=[
            pl.BlockSpec((1, d, TM_PROJ), lambda bi, i: (bi, 0, i)),
            pl.BlockSpec((1, N_PAIRS, TM_PROJ, PAIR_W), lambda bi, i: (bi, 0, i, 0)),
            pl.BlockSpec((1, d, TM_PROJ), lambda bi, i: (bi, 0, i)),
            pl.BlockSpec((1, d, TM_PROJ), lambda bi, i: (bi, 0, i)),
            pl.BlockSpec((W_T_COLS_B, d), lambda bi, i: (blk(bi, i), 0)),
            pl.BlockSpec((d, 2 * PAIR_W), lambda bi, i: (0, 0)),
            pl.BlockSpec((wo_rows, d), lambda bi, i: (step(bi, i), 0)),
            pl.BlockSpec((wo_rows, d), lambda bi, i: (step(bi, i), 0)),
        ],
        out_shape=[feat, jax.ShapeDtypeStruct((b, N_PAIRS, s, PAIR_W), jnp.bfloat16), feat, feat,
                   jax.ShapeDtypeStruct((n_blocks * W_T_COLS_B, d), jnp.bfloat16),
                   jax.ShapeDtypeStruct((d, 2 * PAIR_W), jnp.bfloat16),
                   jax.ShapeDtypeStruct((d, d), jnp.bfloat16), jax.ShapeDtypeStruct((d, d), jnp.bfloat16)],
        compiler_params=pltpu.CompilerParams(
            dimension_semantics=("arbitrary", "arbitrary"), vmem_limit_bytes=VMEM_LIMIT),
        name="proj_a",
    )(x, wt, kv_w, kv_gain[None, :], b_w, b_col_scale[None, :], b_gain[None, :], a_w_out, b_w_out)


def _chunk_iota(shape, axis):
    return lax.shift_right_logical(lax.broadcasted_iota(jnp.int32, shape, axis), CHUNK.bit_length() - 1)


def _identity_bf16(n):
    return (lax.broadcasted_iota(jnp.int32, (n, n), 0) == lax.broadcasted_iota(jnp.int32, (n, n), 1)).astype(jnp.bfloat16)


def _toeplitz(vec_row, n_keys):
    return pltpu.roll(jnp.broadcast_to(vec_row, (n_keys, BIAS_PERIOD)), 0, 1, stride=1, stride_axis=0)


def _fill_bias_slabs(vec_ref, bias_ref, n_keys, bands, flat):
    n = len(bands)
    kch = _chunk_iota((n_keys, TQ), 0)
    qch = _chunk_iota((n_keys, TQ), 1)

    def head_body(h, carry):
        for j, band in enumerate(bands):
            row = vec_ref[j, pl.ds(h, 1), :]
            t = (jnp.broadcast_to(row, (n_keys, BIAS_PERIOD)) if flat[j] else _toeplitz(row, n_keys))[:, :TQ]
            if band is not None:
                t = jnp.where(band(kch, qch), t, MASKED)
            bias_ref[h * n + j] = t
        return carry

    lax.fori_loop(0, N_HEADS, head_body, 0)


def _live_rows(band, n_keys):
    chunks_per_half = LANES // CHUNK
    out = []
    for half in range(TQ // LANES):
        qchs = range(half * chunks_per_half, (half + 1) * chunks_per_half)
        live = [kc for kc in range(n_keys // CHUNK) if band is None or any(band(kc, qc) for qc in qchs)]
        out.append((live[0] * CHUNK, (live[-1] + 1) * CHUNK) if live else (0, 0))
    return tuple(out)


def _scores_stage(k_tiles, q_tiles, bias_fns, live, s_ref):
    maxes, row = [], 0
    for half in range(TQ // LANES):
        maxes.append([None] * MAX_CHAINS)
    for kt, qz, bias_fn, live_j in zip(k_tiles, q_tiles, bias_fns, live):
        s = jnp.dot(kt, qz, preferred_element_type=jnp.float32)
        for r in range(s.shape[0] // SUBLANES):
            rows = slice(r * SUBLANES, (r + 1) * SUBLANES)
            for half, (r0, r1) in enumerate(live_j):
                if not r0 <= r * SUBLANES < r1:
                    continue
                lanes = slice(half * LANES, (half + 1) * LANES)
                acc = maxes[half]
                grp = s[rows, lanes] if bias_fn is None else s[rows, lanes] + bias_fn(rows, lanes)
                s_ref[row + r * SUBLANES:row + (r + 1) * SUBLANES, lanes] = grp
                c = r % MAX_CHAINS
                acc[c] = grp if acc[c] is None else jnp.maximum(acc[c], grp)
        row += s.shape[0]
    cols = [jnp.max(functools.reduce(jnp.maximum, [a for a in acc if a is not None]), axis=0, keepdims=True)
            for acc in maxes]
    return jnp.concatenate(cols, axis=1)


def _pv_stage(s_ref, m, v_tiles, live, extra_logit=None, row_reciprocal=False):
    if extra_logit is not None:
        m = jnp.maximum(m, extra_logit)
    acc, row = None, 0
    for vt, live_j in zip(v_tiles, live):
        n = vt.shape[1]
        halves = []
        for half, (r0, r1) in enumerate(live_j):
            lanes = slice(half * LANES, (half + 1) * LANES)
            parts = [jnp.zeros((r0, LANES), jnp.bfloat16)] if r0 else []
            if r1 > r0:
                parts.append(jnp.exp2(s_ref[row + r0:row + r1, lanes] - m[:, lanes]).astype(jnp.bfloat16))
            if n > r1:
                parts.append(jnp.zeros((n - r1, LANES), jnp.bfloat16))
            halves.append(parts[0] if len(parts) == 1 else jnp.concatenate(parts, axis=0))
        p = jnp.concatenate(halves, axis=1)
        v_ones = jnp.concatenate([vt, jnp.ones((BF16_ROWS, n), vt.dtype)], axis=0)
        part = jnp.dot(v_ones, p, preferred_element_type=jnp.float32)
        acc = part if acc is None else acc + part
        row += n
    l = acc[HEAD_DIM:HEAD_DIM + 1, :]
    if extra_logit is not None:
        l = l + jnp.exp2(extra_logit - m)
    return acc[:HEAD_DIM, :] * (1.0 / l) if row_reciprocal else acc[:HEAD_DIM, :] / l


def _head_pipeline(n_tiles, scores_fn, pv_fn, s_refs, scores_first, per_pair=2):
    per_group = len(s_refs) // 2
    x_refs, y_refs = s_refs[:per_group], s_refs[per_group:]
    n_groups = n_tiles // per_group

    m = [scores_fn(t // per_pair, t % per_pair, ref) for t, ref in zip(range(per_group), x_refs)]
    for g in range(n_groups):
        src, dst = (x_refs, y_refs) if g % 2 == 0 else (y_refs, x_refs)
        nxt = [(g + 1) * per_group + j for j in range(per_group)] if g + 1 < n_groups else []
        m_next = [scores_fn(t // per_pair, t % per_pair, ref) for t, ref in zip(nxt, dst)] if scores_first else []
        for j in range(per_group):
            if not scores_first and nxt:
                m_next.append(scores_fn(nxt[j] // per_pair, nxt[j] % per_pair, dst[j]))
            t = g * per_group + j
            pv_fn(t // per_pair, t % per_pair, src[j], m[j])
        m = m_next


A_BANDS = (lambda kc, qc: kc >= qc,
           None,
           lambda kc, qc: kc <= qc)
A_LIVE = tuple(_live_rows(band, TQ) for band in A_BANDS)
A_FLAT = tuple((A_KBLOCKS - 1 - j) * TQ - (TQ - 1) >= A_REL_CLIP for j in range(A_KBLOCKS))
B_WIN = B_PREV + LANES
B_SUB = PAIR_W
B_BAND = lambda kc, qc: (kc >= qc) & (kc <= qc + B_LEFT_CHUNKS)
B_LIVE = (((0, B_WIN),) * (TQ // LANES),)


def _out_and_residual(zt_ref, wout_ref, res):
    y = lax.dot_general(zt_ref[...], wout_ref[...], _TN, preferred_element_type=jnp.float32)
    return res + y


def _attn_a_kernel(x_ref, qt_ref, kp_ref, ko_ref, vp_ref, vo_ref, gt_ref,
                   bvec_ref, wout_ref, kvw_ref, nwt_ref,
                   h_ref, ksh_ref, vsht_ref, qbt_ref, gbt_ref, zt_ref, bias_ref, *s_refs):
    i = pl.program_id(1)
    subs = TQA // TQ

    @pl.when((pl.program_id(0) == 0) & (i == 0))
    def _():
        _fill_bias_slabs(bvec_ref, bias_ref, TQ, A_BANDS, A_FLAT)
        bias_ref[N_HEADS * A_KBLOCKS] = jnp.full((TQ, TQ), MASKED, jnp.float32)

    n_slabs = N_HEADS * A_KBLOCKS
    rows = lax.broadcasted_iota(jnp.int32, (PAIR_W, TQ), 0)

    def key_block(sb, j):
        n = sb + j
        return n // subs, (n % subs) * TQ

    def scores_fn(sp, hh, s_ref):
        sb, p = divmod(sp, N_PAIRS)
        q2 = qt_ref[0, p, :, sb * TQ:(sb + 1) * TQ]
        own_rows = rows < HEAD_DIM if hh == 0 else rows >= HEAD_DIM
        qz = jnp.where(own_rows, q2, jnp.zeros_like(q2))
        k_tiles, bias_fns = [], []
        for j in range(A_KBLOCKS):
            in_seq = i * subs + sb - (A_KBLOCKS - 1) + j >= 0
            slab = jnp.where(in_seq, (2 * p + hh) * A_KBLOCKS + j, n_slabs)
            own, t0 = key_block(sb, j)
            k_tiles.append((ko_ref if own else kp_ref)[0, p, t0:t0 + TQ, :])
            bias_fns.append(lambda rws, lanes, slab=slab: bias_ref[slab, rws, lanes])
        return _scores_stage(k_tiles, [qz] * A_KBLOCKS, bias_fns, A_LIVE, s_ref)

    def pv_fn(sp, hh, s_ref, m):
        sb, p = divmod(sp, N_PAIRS)
        half = slice(hh * HEAD_DIM, (hh + 1) * HEAD_DIM)
        qcols = slice(sb * TQ, (sb + 1) * TQ)
        v_tiles = []
        for j in range(A_KBLOCKS):
            own, t0 = key_block(sb, j)
            v_tiles.append((vo_ref if own else vp_ref)[0, p, half, t0:t0 + TQ])
        o = _pv_stage(s_ref, m, v_tiles, A_LIVE, row_reciprocal=True)
        gate = gt_ref[0, p, half, qcols].astype(jnp.float32)
        row0 = p * PAIR_W + hh * HEAD_DIM
        zt_ref[row0:row0 + HEAD_DIM, qcols] = (o * gate).astype(jnp.bfloat16)

    _head_pipeline(subs * N_HEADS, scores_fn, pv_fn, s_refs, scores_first=True)

    h = _out_and_residual(zt_ref, wout_ref, x_ref[0])
    h_ref[0] = h
    hb = h.astype(jnp.bfloat16)
    r_col, r_row = _token_scales(h)
    kv = jnp.dot(hb, kvw_ref[...], preferred_element_type=jnp.float32) * r_col
    ksh_ref[0] = kv[:, :PAIR_W].astype(jnp.bfloat16)
    vsht_ref[0] = jnp.transpose(kv[:, PAIR_W:]).astype(jnp.bfloat16)
    for out_ref, base, act in ((gbt_ref, D_MODEL, _silu), (qbt_ref, 0, None)):
        for c in range(0, D_MODEL, FEAT_CHUNK):
            yt = lax.dot_general(nwt_ref[base + c:base + c + FEAT_CHUNK, :], hb, _NT,
                                 preferred_element_type=jnp.float32) * r_row
            out_ref[0, c:c + FEAT_CHUNK, :] = (yt if act is None else act(yt)).astype(jnp.bfloat16)


def _const_spec(shape):
    return pl.BlockSpec(shape, lambda bi, i: (0,) * len(shape), pipeline_mode=pl.Buffered(1))


def _attn_a(x, qt, k, vt, gt, bvec, wout, kvw, nwt):
    b, s, d = x.shape
    qt4 = qt.reshape(b, N_PAIRS, PAIR_W, s)
    vt4 = vt.reshape(b, N_PAIRS, PAIR_W, s)
    gt4 = gt.reshape(b, N_PAIRS, PAIR_W, s)
    assert (A_KBLOCKS - 1) * TQ <= TQA, "the keys before a grid block must fit in one previous block"
    prev = lambda i: jnp.maximum(i - 1, 0)

    def kspec(index):
        return pl.BlockSpec((1, N_PAIRS, TQA, PAIR_W), lambda bi, i: (bi, 0, index(i), 0))

    def vspec(index):
        return pl.BlockSpec((1, N_PAIRS, PAIR_W, TQA), lambda bi, i: (bi, 0, 0, index(i)))

    feat_spec = pl.BlockSpec((1, N_PAIRS, PAIR_W, TQA), lambda bi, i: (bi, 0, 0, i))
    featout_spec = pl.BlockSpec((1, d, TQA), lambda bi, i: (bi, 0, i))
    feat = jax.ShapeDtypeStruct((b, d, s), jnp.bfloat16)
    return pl.pallas_call(
        _attn_a_kernel,
        grid=(b, s // TQA),
        in_specs=[
            pl.BlockSpec((1, TQA, d), lambda bi, i: (bi, i, 0)),
            feat_spec,
            kspec(prev), kspec(lambda i: i),
            vspec(prev), vspec(lambda i: i),
            feat_spec,
            _const_spec(bvec.shape),
            _const_spec(wout.shape),
            _const_spec(kvw.shape),
            _const_spec(nwt.shape),
        ],
        out_specs=[
            pl.BlockSpec((1, TQA, d), lambda bi, i: (bi, i, 0)),
            pl.BlockSpec((1, TQA, PAIR_W), lambda bi, i: (bi, i, 0)),
            pl.BlockSpec((1, PAIR_W, TQA), lambda bi, i: (bi, 0, i)),
            featout_spec,
            featout_spec,
        ],
        out_shape=[
            jax.ShapeDtypeStruct((b, s, d), jnp.float32),
            jax.ShapeDtypeStruct((b, s, PAIR_W), jnp.bfloat16),
            jax.ShapeDtypeStruct((b, PAIR_W, s), jnp.bfloat16),
            feat, feat,
        ],
        scratch_shapes=[pltpu.VMEM((d, TQA), jnp.bfloat16),
                        pltpu.VMEM((N_HEADS * A_KBLOCKS + 1, TQ, TQ), jnp.float32),
                        *[pltpu.VMEM((A_KBLOCKS * TQ, TQ), jnp.float32)] * N_SCORE_BUFS],
        compiler_params=pltpu.CompilerParams(
            dimension_semantics=("arbitrary", "arbitrary"), vmem_limit_bytes=VMEM_LIMIT),
        name="attn_a",
    )(x, qt4, k, k, vt4, vt4, gt4, bvec, wout, kvw, nwt)


def _attn_b_kernel(h_ref, qt_ref, kp_ref, ko_ref, vp_ref, vo_ref, gt_ref, vec_ref,
                   sink_ref, wout_ref, fg_ref, out_ref, zt_ref, bias_ref, *s_refs):
    i = pl.program_id(1)

    @pl.when((pl.program_id(0) == 0) & (i == 0))
    def _():
        in_band = B_BAND(_chunk_iota((B_WIN, LANES), 0), _chunk_iota((B_WIN, LANES), 1))
        before_block = lax.broadcasted_iota(jnp.int32, (B_WIN, LANES), 0) < B_PREV

        def pair_body(p, carry):
            for hh in range(2):
                t = jnp.where(in_band, _toeplitz(vec_ref[0, pl.ds(2 * p + hh, 1), :], B_WIN)[:, :LANES], MASKED)
                lanes = slice(hh * LANES, (hh + 1) * LANES)
                bias_ref[2 * p, :, lanes] = t.astype(jnp.bfloat16)
                bias_ref[2 * p + 1, :, lanes] = jnp.where(before_block, MASKED, t).astype(jnp.bfloat16)
            return carry

        lax.fori_loop(0, N_PAIRS, pair_body, 0)

    rows = lax.broadcasted_iota(jnp.int32, (PAIR_W, TQ), 0)
    pairs_per_kv = N_PAIRS // B_KV_HEADS
    eye = _identity_bf16(B_SUB)
    head_lanes = lax.broadcasted_iota(jnp.int32, (1, TQ), 1)

    def scores_fn(p, u, s_ref):
        qlanes = slice(u * LANES, (u + 1) * LANES)
        q2 = jnp.concatenate([qt_ref[0, p, :HEAD_DIM, qlanes], qt_ref[0, p, HEAD_DIM:, qlanes]], axis=1)
        kv0 = HEAD_DIM * (p // pairs_per_kv)
        kv_rows = (rows >= kv0) & (rows < kv0 + HEAD_DIM)
        qz = jnp.where(kv_rows, jnp.concatenate([q2, q2], axis=0), jnp.zeros((PAIR_W, TQ), q2.dtype))
        if u == 0:
            k_tile = jnp.concatenate([kp_ref[0], ko_ref[0, :LANES, :]], axis=0)
            variant = jnp.where(i == 0, 1, 0)
        else:
            k_tile, variant = ko_ref[0, u * LANES - B_PREV:(u + 1) * LANES, :], 0

        lhs, rhs = [], []
        for r0 in range(0, B_WIN, B_SUB):
            lhs.append(jnp.concatenate([k_tile[r0:r0 + B_SUB, :], eye], axis=1))
            rhs.append(jnp.concatenate([qz, bias_ref[2 * p + variant, r0:r0 + B_SUB, :]], axis=0))
        sub_live = (((0, B_SUB),) * (TQ // LANES),) * len(lhs)
        return _scores_stage(lhs, rhs, [None] * len(lhs), sub_live, s_ref)

    def pv_fn(p, u, s_ref, m):
        qlanes = slice(u * LANES, (u + 1) * LANES)
        kv_rows = slice(HEAD_DIM * (p // pairs_per_kv), HEAD_DIM * (p // pairs_per_kv + 1))
        if u == 0:
            v_tile = jnp.concatenate([vp_ref[0, kv_rows, :], vo_ref[0, kv_rows, :LANES]], axis=1)
        else:
            v_tile = vo_ref[0, kv_rows, u * LANES - B_PREV:(u + 1) * LANES]
        sink = jnp.where(head_lanes < LANES, sink_ref[0, 2 * p], sink_ref[0, 2 * p + 1]) * LOG2E
        o = _pv_stage(s_ref, m, [v_tile], B_LIVE, extra_logit=sink, row_reciprocal=True)
        for hh in range(2):
            hrows = slice(hh * HEAD_DIM, (hh + 1) * HEAD_DIM)
            gate = gt_ref[0, p, hrows, qlanes].astype(jnp.float32)
            row0 = p * PAIR_W + hh * HEAD_DIM
            zt_ref[row0:row0 + HEAD_DIM, qlanes] = (o[:, hh * LANES:(hh + 1) * LANES] * gate).astype(jnp.bfloat16)

    _head_pipeline(N_PAIRS * (TQB // LANES), scores_fn, pv_fn, s_refs, scores_first=False, per_pair=TQB // LANES)

    h2 = _out_and_residual(zt_ref, wout_ref, h_ref[0])
    out_ref[0] = (h2 * fg_ref[...]) * _rms_scale(h2)


def _attn_b(h, qbt, ksh, vsht, gbt, vec, sinks, wout, fg):
    b, s, d = h.shape
    qt4 = qbt.reshape(b, N_PAIRS, PAIR_W, s)
    gt4 = gbt.reshape(b, N_PAIRS, PAIR_W, s)
    feat_spec = pl.BlockSpec((1, N_PAIRS, PAIR_W, TQB), lambda bi, i: (bi, 0, 0, i))
    prev = lambda i: jnp.maximum(i * (TQB // B_PREV) - 1, 0)
    return pl.pallas_call(
        _attn_b_kernel,
        grid=(b, s // TQB),
        in_specs=[
            pl.BlockSpec((1, TQB, d), lambda bi, i: (bi, i, 0)),
            feat_spec,
            pl.BlockSpec((1, B_PREV, PAIR_W), lambda bi, i: (bi, prev(i), 0)),
            pl.BlockSpec((1, TQB, PAIR_W), lambda bi, i: (bi, i, 0)),
            pl.BlockSpec((1, PAIR_W, B_PREV), lambda bi, i: (bi, 0, prev(i))),
            pl.BlockSpec((1, PAIR_W, TQB), lambda bi, i: (bi, 0, i)),
            feat_spec,
            _const_spec(vec.shape),
            pl.BlockSpec(memory_space=pltpu.SMEM),
            _const_spec(wout.shape),
            _const_spec(fg.shape),
        ],
        out_specs=pl.BlockSpec((1, TQB, d), lambda bi, i: (bi, i, 0)),
        out_shape=jax.ShapeDtypeStruct((b, s, d), jnp.float32),
        scratch_shapes=[pltpu.VMEM((d, TQB), jnp.bfloat16),
                        pltpu.VMEM((2 * N_PAIRS, B_WIN, TQ), jnp.bfloat16),
                        *[pltpu.VMEM((B_WIN, TQ), jnp.float32)] * N_SCORE_BUFS_B],
        compiler_params=pltpu.CompilerParams(
            dimension_semantics=("arbitrary", "arbitrary"), vmem_limit_bytes=VMEM_LIMIT),
        name="attn_b",
    )(h, qt4, ksh, ksh, vsht, vsht, gt4, vec, sinks, wout, fg)


def _slab_diff(xp=jnp):
    u = xp.arange(BIAS_PERIOD)
    return xp.where(u < TQ, u, u - BIAS_PERIOD)


def _static_take(table, idx):
    parts, lo = [], 0
    while lo < len(idx):
        hi = lo + 1
        step = idx[hi] - idx[lo] if hi < len(idx) else 0
        while step in (0, 1) and hi < len(idx) and idx[hi] - idx[hi - 1] == step:
            hi += 1
        rows = table[idx[lo]:idx[hi - 1] + 1]
        parts.append(jnp.broadcast_to(rows, (hi - lo,) + table.shape[1:]) if step == 0 else rows)
        lo = hi
    return jnp.concatenate(parts, axis=0)


def _bias_vecs_a(rel_bias):
    back = (A_KBLOCKS - 1 - np.arange(A_KBLOCKS))[:, None] * TQ
    dist = back + _slab_diff(np)[None, :]
    idx = np.clip(dist, -A_REL_CLIP, A_REL_CLIP) + A_REL_CLIP
    vec = jnp.stack([_static_take(rel_bias, [int(v) for v in row]) for row in idx])
    return jnp.transpose(vec, (0, 2, 1)).astype(jnp.float32)


def _t5_bucket(rel):
    nb = T5_BUCKETS // 2
    max_exact = nb // 2
    ret = jnp.where(rel > 0, nb, 0)
    n = jnp.abs(rel)
    nf = jnp.maximum(n, 1).astype(jnp.float32)
    large = max_exact + (jnp.log(nf / max_exact) / math.log(T5_MAX_DIST / max_exact)
                         * (nb - max_exact)).astype(jnp.int32)
    large = jnp.minimum(large, nb - 1)
    return ret + jnp.where(n < max_exact, n, large)


def _bias_vec_b(t5_table):
    rel = -_slab_diff() - B_PREV
    vec = jnp.transpose(t5_table[_t5_bucket(rel)], (1, 0)).astype(jnp.float32)
    return vec[None]


def kernel(x, a_norm, a_w_in, a_rel_bias, a_w_out, kv_norm, kv_w, t5_bias,
           b_norm, b_w_in, b_sinks, b_w_out, final_norm):
    assert a_norm.shape[0] == 1 and b_norm.shape[0] == 1, "one A layer then one B layer"
    scale = HEAD_DIM ** -0.5 * LOG2E

    def q_col_scale(n):
        return np.where(np.arange(n) < D_MODEL, scale, 1.0).astype(np.float32)

    wt_a = _weight_t(a_w_in[0], q_col_scale(a_w_in.shape[2]), a_norm[0], W_T_COLS_A)
    qt, k, vt, gt, nwt, kvw16, wo_a, wo_b = _proj_a(x, wt_a, kv_w, b_w_in[0], q_col_scale(b_w_in.shape[2]),
                                                    kv_norm, b_norm[0], a_w_out, b_w_out)
    h, ksh, vsht, qbt, gbt = _attn_a(
        x, qt, k, vt, gt, _bias_vecs_a(a_rel_bias[0] * LOG2E), wo_a, kvw16, nwt)

    return _attn_b(h, qbt, ksh, vsht, gbt, _bias_vec_b(t5_bias * LOG2E), b_sinks.astype(jnp.float32), wo_b,
                   final_norm[None, :])
```

```python
import functools
import math

import jax
import jax.numpy as jnp
import numpy as np
from jax import lax
from jax.experimental import pallas as pl
from jax.experimental.pallas import tpu as pltpu

D_MODEL = 1024
HEAD_DIM = 64
N_HEADS = D_MODEL // HEAD_DIM
N_PAIRS = N_HEADS // 2
PAIR_W = 2 * HEAD_DIM
CHUNK = 64
RMS_EPS = 1e-6
A_LEFT_CHUNKS = 8
A_REL_CLIP = 256
B_KV_HEADS = 2
B_LEFT_CHUNKS = 2
T5_BUCKETS = 32
T5_MAX_DIST = 128

TQ = 256
TQB = 512
TQA = 512
A_KBLOCKS = A_LEFT_CHUNKS * CHUNK // TQ + 1
B_PREV = B_LEFT_CHUNKS * CHUNK
TM_PROJ = 1024
FEAT_CHUNK = 256
W_T_COLS_A = 2048
W_T_COLS_B = 512
MASKED = -1e30
BIAS_PERIOD = 2 * TQ
SUBLANES = 8
LANES = 128
MAX_CHAINS = 1
BF16_ROWS = 16
LOG2E = math.log2(math.e)
N_SCORE_BUFS = 4
N_SCORE_BUFS_B = 8
VMEM_LIMIT = 56 * 1024 * 1024

_NT = (((1,), (1,)), ((), ()))
_TN = (((0,), (0,)), ((), ()))


def _rms_scale(xf):
    return lax.rsqrt(jnp.mean(xf * xf, axis=-1, keepdims=True) + RMS_EPS)


def _token_scales(xf):
    col = _rms_scale(xf)
    row = jnp.transpose(jnp.broadcast_to(col, (xf.shape[0], LANES)))[0:1, :]
    return col, row


def _silu(v):
    return v * (0.5 * jnp.tanh(0.5 * v) + 0.5)


def _weight_t_kernel(w_ref, cs_ref, gain_ref, o_ref):
    g_col = jnp.transpose(jnp.broadcast_to(gain_ref[...], (LANES, w_ref.shape[0])))[:, :1]
    o_ref[...] = jnp.transpose(((w_ref[...] * cs_ref[...]) * g_col).astype(jnp.bfloat16))


def _weight_t(w, col_scale, row_gain, tc):
    d, n = w.shape
    return pl.pallas_call(
        _weight_t_kernel,
        grid=(n // tc,),
        in_specs=[
            pl.BlockSpec((d, tc), lambda j: (0, j)),
            pl.BlockSpec((1, tc), lambda j: (0, j)),
            pl.BlockSpec((1, d), lambda j: (0, 0)),
        ],
        out_specs=pl.BlockSpec((tc, d), lambda j: (j, 0)),
        out_shape=jax.ShapeDtypeStruct((n, d), jnp.bfloat16),
        compiler_params=pltpu.CompilerParams(dimension_semantics=("arbitrary",), vmem_limit_bytes=VMEM_LIMIT),
        name="weight_t",
    )(w, col_scale[None, :], row_gain[None, :])


def _proj_a_kernel(x_ref, wt_ref, kvw_ref, kvg_ref, bw_ref, cs_ref, bg_ref, woa_ref, wob_ref,
                   qt_ref, k_ref, vt_ref, gt_ref, nwt_ref, kvw16_ref, woa16_ref, wob16_ref):
    nwt_ref[...] = ((bw_ref[...] * cs_ref[...]).T * bg_ref[...]).astype(jnp.bfloat16)
    g_rows = jnp.transpose(jnp.broadcast_to(kvg_ref[...], (2 * PAIR_W, D_MODEL)))
    kvw16_ref[...] = (kvw_ref[...] * g_rows).astype(jnp.bfloat16)
    woa16_ref[...] = woa_ref[0].astype(jnp.bfloat16)
    wob16_ref[...] = wob_ref[0].astype(jnp.bfloat16)

    xf = x_ref[0]
    xb = xf.astype(jnp.bfloat16)
    r_col, r_row = _token_scales(xf)
    k = lax.dot_general(xb, wt_ref[D_MODEL:2 * D_MODEL, :], _NT,
                        preferred_element_type=jnp.float32) * r_col
    for p in range(N_PAIRS):
        k_ref[0, p] = k[:, p * PAIR_W:(p + 1) * PAIR_W].astype(jnp.bfloat16)
    for out_ref, base, act in ((gt_ref, 3 * D_MODEL, _silu), (vt_ref, 2 * D_MODEL, None), (qt_ref, 0, None)):
        for c in range(0, D_MODEL, FEAT_CHUNK):
            w = wt_ref[base + c:base + c + FEAT_CHUNK, :]
            yt = lax.dot_general(w, xb, _NT, preferred_element_type=jnp.float32) * r_row
            out_ref[0, c:c + FEAT_CHUNK, :] = (yt if act is None else act(yt)).astype(jnp.bfloat16)


def _proj_a(x, wt, kv_w, b_w, b_col_scale, kv_gain, b_gain, a_w_out, b_w_out):
    b, s, d = x.shape
    n_i = s // TM_PROJ
    wo_rows = d // (b * n_i)
    step = lambda bi, i: bi * n_i + i
    n_blocks = b_w.shape[1] // W_T_COLS_B
    assert n_blocks <= b * n_i and kv_w.shape == (d, 2 * PAIR_W) and d % (b * n_i * BF16_ROWS) == 0
    blk = lambda bi, i: jnp.minimum(bi * n_i + i, n_blocks - 1)
    feat = jax.ShapeDtypeStruct((b, d, s), jnp.bfloat16)
    return pl.pallas_call(
        _proj_a_kernel,
        grid=(b, n_i),
        in_specs=[
            pl.BlockSpec((1, TM_PROJ, d), lambda bi, i: (bi, i, 0)),
            _const_spec(wt.shape),
            _const_spec(kv_w.shape),
            _const_spec((1, d)),
            pl.BlockSpec((d, W_T_COLS_B), lambda bi, i: (0, blk(bi, i))),
            pl.BlockSpec((1, W_T_COLS_B), lambda bi, i: (0, blk(bi, i))),
            _const_spec((1, d)),
            pl.BlockSpec((1, wo_rows, d), lambda bi, i: (0, step(bi, i), 0)),
            pl.BlockSpec((1, wo_rows, d), lambda bi, i: (0, step(bi, i), 0)),
        ],
        out_specs=[
            pl.BlockSpec((1, d, TM_PROJ), lambda bi, i: (bi, 0, i)),
            pl.BlockSpec((1, N_PAIRS, TM_PROJ, PAIR_W), lambda bi, i: (bi, 0, i, 0)),
            pl.BlockSpec((1, d, TM_PROJ), lambda bi, i: (bi, 0, i)),
            pl.BlockSpec((1, d, TM_PROJ), lambda bi, i: (bi, 0, i)),
            pl.BlockSpec((W_T_COLS_B, d), lambda bi, i: (blk(bi, i), 0)),
            pl.BlockSpec((d, 2 * PAIR_W), lambda bi, i: (0, 0)),
            pl.BlockSpec((wo_rows, d), lambda bi, i: (step(bi, i), 0)),
            pl.BlockSpec((wo_rows, d), lambda bi, i: (step(bi, i), 0)),
        ],
        out_shape=[feat, jax.ShapeDtypeStruct((b, N_PAIRS, s, PAIR_W), jnp.bfloat16), feat, feat,
                   jax.ShapeDtypeStruct((n_blocks * W_T_COLS_B, d), jnp.bfloat16),
                   jax.ShapeDtypeStruct((d, 2 * PAIR_W), jnp.bfloat16),
                   jax.ShapeDtypeStruct((d, d), jnp.bfloat16), jax.ShapeDtypeStruct((d, d), jnp.bfloat16)],
        compiler_params=pltpu.CompilerParams(
            dimension_semantics=("arbitrary", "arbitrary"), vmem_limit_bytes=VMEM_LIMIT),
        name="proj_a",
    )(x, wt, kv_w, kv_gain[None, :], b_w, b_col_scale[None, :], b_gain[None, :], a_w_out, b_w_out)


def _chunk_iota(shape, axis):
    return lax.shift_right_logical(lax.broadcasted_iota(jnp.int32, shape, axis), CHUNK.bit_length() - 1)


def _identity_bf16(n):
    return (lax.broadcasted_iota(jnp.int32, (n, n), 0) == lax.broadcasted_iota(jnp.int32, (n, n), 1)).astype(jnp.bfloat16)


def _toeplitz(vec_row, n_keys):
    return pltpu.roll(jnp.broadcast_to(vec_row, (n_keys, BIAS_PERIOD)), 0, 1, stride=1, stride_axis=0)


def _fill_bias_slabs(vec_ref, bias_ref, n_keys, bands, flat):
    n = len(bands)
    kch = _chunk_iota((n_keys, TQ), 0)
    qch = _chunk_iota((n_keys, TQ), 1)

    def head_body(h, carry):
        for j, band in enumerate(bands):
            row = vec_ref[j, pl.ds(h, 1), :]
            t = (jnp.broadcast_to(row, (n_keys, BIAS_PERIOD)) if flat[j] else _toeplitz(row, n_keys))[:, :TQ]
            if band is not None:
                t = jnp.where(band(kch, qch), t, MASKED)
            bias_ref[h * n + j] = t
        return carry

    lax.fori_loop(0, N_HEADS, head_body, 0)


def _live_rows(band, n_keys):
    chunks_per_half = LANES // CHUNK
    out = []
    for half in range(TQ // LANES):
        qchs = range(half * chunks_per_half, (half + 1) * chunks_per_half)
        live = [kc for kc in range(n_keys // CHUNK) if band is None or any(band(kc, qc) for qc in qchs)]
        out.append((live[0] * CHUNK, (live[-1] + 1) * CHUNK) if live else (0, 0))
    return tuple(out)


def _scores_stage(k_tiles, q_tiles, bias_fns, live, s_ref):
    maxes, row = [], 0
    for half in range(TQ // LANES):
        maxes.append([None] * MAX_CHAINS)
    for kt, qz, bias_fn, live_j in zip(k_tiles, q_tiles, bias_fns, live):
        s = jnp.dot(kt, qz, preferred_element_type=jnp.float32)
        for r in range(s.shape[0] // SUBLANES):
            rows = slice(r * SUBLANES, (r + 1) * SUBLANES)
            for half, (r0, r1) in enumerate(live_j):
                if not r0 <= r * SUBLANES < r1:
                    continue
                lanes = slice(half * LANES, (half + 1) * LANES)
                acc = maxes[half]
                grp = s[rows, lanes] if bias_fn is None else s[rows, lanes] + bias_fn(rows, lanes)
                s_ref[row + r * SUBLANES:row + (r + 1) * SUBLANES, lanes] = grp
                c = r % MAX_CHAINS
                acc[c] = grp if acc[c] is None else jnp.maximum(acc[c], grp)
        row += s.shape[0]
    cols = [jnp.max(functools.reduce(jnp.maximum, [a for a in acc if a is not None]), axis=0, keepdims=True)
            for acc in maxes]
    return jnp.concatenate(cols, axis=1)


def _pv_stage(s_ref, m, v_tiles, live, extra_logit=None, row_reciprocal=False):
    if extra_logit is not None:
        m = jnp.maximum(m, extra_logit)
    acc, row = None, 0
    for vt, live_j in zip(v_tiles, live):
        n = vt.shape[1]
        halves = []
        for half, (r0, r1) in enumerate(live_j):
            lanes = slice(half * LANES, (half + 1) * LANES)
            parts = [jnp.zeros((r0, LANES), jnp.bfloat16)] if r0 else []
            if r1 > r0:
                parts.append(jnp.exp2(s_ref[row + r0:row + r1, lanes] - m[:, lanes]).astype(jnp.bfloat16))
            if n > r1:
                parts.append(jnp.zeros((n - r1, LANES), jnp.bfloat16))
            halves.append(parts[0] if len(parts) == 1 else jnp.concatenate(parts, axis=0))
        p = jnp.concatenate(halves, axis=1)
        v_ones = jnp.concatenate([vt, jnp.ones((BF16_ROWS, n), vt.dtype)], axis=0)
        part = jnp.dot(v_ones, p, preferred_element_type=jnp.float32)
        acc = part if acc is None else acc + part
        row += n
    l = acc[HEAD_DIM:HEAD_DIM + 1, :]
    if extra_logit is not None:
        l = l + jnp.exp2(extra_logit - m)
    return acc[:HEAD_DIM, :] * (1.0 / l) if row_reciprocal else acc[:HEAD_DIM, :] / l


def _head_pipeline(n_tiles, scores_fn, pv_fn, s_refs, scores_first, per_pair=2):
    per_group = len(s_refs) // 2
    x_refs, y_refs = s_refs[:per_group], s_refs[per_group:]
    n_groups = n_tiles // per_group

    m = [scores_fn(t // per_pair, t % per_pair, ref) for t, ref in zip(range(per_group), x_refs)]
    for g in range(n_groups):
        src, dst = (x_refs, y_refs) if g % 2 == 0 else (y_refs, x_refs)
        nxt = [(g + 1) * per_group + j for j in range(per_group)] if g + 1 < n_groups else []
        m_next = [scores_fn(t // per_pair, t % per_pair, ref) for t, ref in zip(nxt, dst)] if scores_first else []
        for j in range(per_group):
            if not scores_first and nxt:
                m_next.append(scores_fn(nxt[j] // per_pair, nxt[j] % per_pair, dst[j]))
            t = g * per_group + j
            pv_fn(t // per_pair, t % per_pair, src[j], m[j])
        m = m_next


A_BANDS = (lambda kc, qc: kc >= qc,
           None,
           lambda kc, qc: kc <= qc)
A_LIVE = tuple(_live_rows(band, TQ) for band in A_BANDS)
A_FLAT = tuple((A_KBLOCKS - 1 - j) * TQ - (TQ - 1) >= A_REL_CLIP for j in range(A_KBLOCKS))
B_WIN = B_PREV + LANES
B_SUB = PAIR_W
B_BAND = lambda kc, qc: (kc >= qc) & (kc <= qc + B_LEFT_CHUNKS)
B_LIVE = (((0, B_WIN),) * (TQ // LANES),)


def _out_and_residual(zt_ref, wout_ref, res):
    y = lax.dot_general(zt_ref[...], wout_ref[...], _TN, preferred_element_type=jnp.float32)
    return res + y


def _attn_a_kernel(x_ref, qt_ref, kp_ref, ko_ref, vp_ref, vo_ref, gt_ref,
                   bvec_ref, wout_ref, kvw_ref, nwt_ref,
                   h_ref, ksh_ref, vsht_ref, qbt_ref, gbt_ref, zt_ref, bias_ref, *s_refs):
    i = pl.program_id(1)
    subs = TQA // TQ

    @pl.when((pl.program_id(0) == 0) & (i == 0))
    def _():
        _fill_bias_slabs(bvec_ref, bias_ref, TQ, A_BANDS, A_FLAT)
        bias_ref[N_HEADS * A_KBLOCKS] = jnp.full((TQ, TQ), MASKED, jnp.float32)

    n_slabs = N_HEADS * A_KBLOCKS
    rows = lax.broadcasted_iota(jnp.int32, (PAIR_W, TQ), 0)

    def key_block(sb, j):
        n = sb + j
        return n // subs, (n % subs) * TQ

    def scores_fn(sp, hh, s_ref):
        sb, p = divmod(sp, N_PAIRS)
        q2 = qt_ref[0, p, :, sb * TQ:(sb + 1) * TQ]
        own_rows = rows < HEAD_DIM if hh == 0 else rows >= HEAD_DIM
        qz = jnp.where(own_rows, q2, jnp.zeros_like(q2))
        k_tiles, bias_fns = [], []
        for j in range(A_KBLOCKS):
            in_seq = i * subs + sb - (A_KBLOCKS - 1) + j >= 0
            slab = jnp.where(in_seq, (2 * p + hh) * A_KBLOCKS + j, n_slabs)
            own, t0 = key_block(sb, j)
            k_tiles.append((ko_ref if own else kp_ref)[0, p, t0:t0 + TQ, :])
            bias_fns.append(lambda rws, lanes, slab=slab: bias_ref[slab, rws, lanes])
        return _scores_stage(k_tiles, [qz] * A_KBLOCKS, bias_fns, A_LIVE, s_ref)

    def pv_fn(sp, hh, s_ref, m):
        sb, p = divmod(sp, N_PAIRS)
        half = slice(hh * HEAD_DIM, (hh + 1) * HEAD_DIM)
        qcols = slice(sb * TQ, (sb + 1) * TQ)
        v_tiles = []
        for j in range(A_KBLOCKS):
            own, t0 = key_block(sb, j)
            v_tiles.append((vo_ref if own else vp_ref)[0, p, half, t0:t0 + TQ])
        o = _pv_stage(s_ref, m, v_tiles, A_LIVE, row_reciprocal=True)
        gate = gt_ref[0, p, half, qcols].astype(jnp.float32)
        row0 = p * PAIR_W + hh * HEAD_DIM
        zt_ref[row0:row0 + HEAD_DIM, qcols] = (o * gate).astype(jnp.bfloat16)

    _head_pipeline(subs * N_HEADS, scores_fn, pv_fn, s_refs, scores_first=True)

    h = _out_and_residual(zt_ref, wout_ref, x_ref[0])
    h_ref[0] = h
    hb = h.astype(jnp.bfloat16)
    r_col, r_row = _token_scales(h)
    kv = jnp.dot(hb, kvw_ref[...], preferred_element_type=jnp.float32) * r_col
    ksh_ref[0] = kv[:, :PAIR_W].astype(jnp.bfloat16)
    vsht_ref[0] = jnp.transpose(kv[:, PAIR_W:]).astype(jnp.bfloat16)
    for out_ref, base, act in ((gbt_ref, D_MODEL, _silu), (qbt_ref, 0, None)):
        for c in range(0, D_MODEL, FEAT_CHUNK):
            yt = lax.dot_general(nwt_ref[base + c:base + c + FEAT_CHUNK, :], hb, _NT,
                                 preferred_element_type=jnp.float32) * r_row
            out_ref[0, c:c + FEAT_CHUNK, :] = (yt if act is None else act(yt)).astype(jnp.bfloat16)


def _const_spec(shape):
    return pl.BlockSpec(shape, lambda bi, i: (0,) * len(shape), pipeline_mode=pl.Buffered(1))


def _attn_a(x, qt, k, vt, gt, bvec, wout, kvw, nwt):
    b, s, d = x.shape
    qt4 = qt.reshape(b, N_PAIRS, PAIR_W, s)
    vt4 = vt.reshape(b, N_PAIRS, PAIR_W, s)
    gt4 = gt.reshape(b, N_PAIRS, PAIR_W, s)
    assert (A_KBLOCKS - 1) * TQ <= TQA, "the keys before a grid block must fit in one previous block"
    prev = lambda i: jnp.maximum(i - 1, 0)

    def kspec(index):
        return pl.BlockSpec((1, N_PAIRS, TQA, PAIR_W), lambda bi, i: (bi, 0, index(i), 0))

    def vspec(index):
        return pl.BlockSpec((1, N_PAIRS, PAIR_W, TQA), lambda bi, i: (bi, 0, 0, index(i)))

    feat_spec = pl.BlockSpec((1, N_PAIRS, PAIR_W, TQA), lambda bi, i: (bi, 0, 0, i))
    featout_spec = pl.BlockSpec((1, d, TQA), lambda bi, i: (bi, 0, i))
    feat = jax.ShapeDtypeStruct((b, d, s), jnp.bfloat16)
    return pl.pallas_call(
        _attn_a_kernel,
        grid=(b, s // TQA),
        in_specs=[
            pl.BlockSpec((1, TQA, d), lambda bi, i: (bi, i, 0)),
            feat_spec,
            kspec(prev), kspec(lambda i: i),
            vspec(prev), vspec(lambda i: i),
            feat_spec,
            _const_spec(bvec.shape),
            _const_spec(wout.shape),
            _const_spec(kvw.shape),
            _const_spec(nwt.shape),
        ],
        out_specs=[
            pl.BlockSpec((1, TQA, d), lambda bi, i: (bi, i, 0)),
            pl.BlockSpec((1, TQA, PAIR_W), lambda bi, i: (bi, i, 0)),
            pl.BlockSpec((1, PAIR_W, TQA), lambda bi, i: (bi, 0, i)),
            featout_spec,
            featout_spec,
        ],
        out_shape=[
            jax.ShapeDtypeStruct((b, s, d), jnp.float32),
            jax.ShapeDtypeStruct((b, s, PAIR_W), jnp.bfloat16),
            jax.ShapeDtypeStruct((b, PAIR_W, s), jnp.bfloat16),
            feat, feat,
        ],
        scratch_shapes=[pltpu.VMEM((d, TQA), jnp.bfloat16),
                        pltpu.VMEM((N_HEADS * A_KBLOCKS + 1, TQ, TQ), jnp.float32),
                        *[pltpu.VMEM((A_KBLOCKS * TQ, TQ), jnp.float32)] * N_SCORE_BUFS],
        compiler_params=pltpu.CompilerParams(
            dimension_semantics=("arbitrary", "arbitrary"), vmem_limit_bytes=VMEM_LIMIT),
        name="attn_a",
    )(x, qt4, k, k, vt4, vt4, gt4, bvec, wout, kvw, nwt)


def _attn_b_kernel(h_ref, qt_ref, kp_ref, ko_ref, vp_ref, vo_ref, gt_ref, vec_ref,
                   sink_ref, wout_ref, fg_ref, out_ref, zt_ref, bias_ref, *s_refs):
    i = pl.program_id(1)

    @pl.when((pl.program_id(0) == 0) & (i == 0))
    def _():
        in_band = B_BAND(_chunk_iota((B_WIN, LANES), 0), _chunk_iota((B_WIN, LANES), 1))
        before_block = lax.broadcasted_iota(jnp.int32, (B_WIN, LANES), 0) < B_PREV

        def pair_body(p, carry):
            for hh in range(2):
                t = jnp.where(in_band, _toeplitz(vec_ref[0, pl.ds(2 * p + hh, 1), :], B_WIN)[:, :LANES], MASKED)
                lanes = slice(hh * LANES, (hh + 1) * LANES)
                bias_ref[2 * p, :, lanes] = t.astype(jnp.bfloat16)
                bias_ref[2 * p + 1, :, lanes] = jnp.where(before_block, MASKED, t).astype(jnp.bfloat16)
            return carry

        lax.fori_loop(0, N_PAIRS, pair_body, 0)

    rows = lax.broadcasted_iota(jnp.int32, (PAIR_W, TQ), 0)
    pairs_per_kv = N_PAIRS // B_KV_HEADS
    eye = _identity_bf16(B_SUB)
    head_lanes = lax.broadcasted_iota(jnp.int32, (1, TQ), 1)

    def scores_fn(p, u, s_ref):
        qlanes = slice(u * LANES, (u + 1) * LANES)
        q2 = jnp.concatenate([qt_ref[0, p, :HEAD_DIM, qlanes], qt_ref[0, p, HEAD_DIM:, qlanes]], axis=1)
        kv0 = HEAD_DIM * (p // pairs_per_kv)
        kv_rows = (rows >= kv0) & (rows < kv0 + HEAD_DIM)
        qz = jnp.where(kv_rows, jnp.concatenate([q2, q2], axis=0), jnp.zeros((PAIR_W, TQ), q2.dtype))
        if u == 0:
            k_tile = jnp.concatenate([kp_ref[0], ko_ref[0, :LANES, :]], axis=0)
            variant = jnp.where(i == 0, 1, 0)
        else:
            k_tile, variant = ko_ref[0, u * LANES - B_PREV:(u + 1) * LANES, :], 0

        lhs, rhs = [], []
        for r0 in range(0, B_WIN, B_SUB):
            lhs.append(jnp.concatenate([k_tile[r0:r0 + B_SUB, :], eye], axis=1))
            rhs.append(jnp.concatenate([qz, bias_ref[2 * p + variant, r0:r0 + B_SUB, :]], axis=0))
        sub_live = (((0, B_SUB),) * (TQ // LANES),) * len(lhs)
        return _scores_stage(lhs, rhs, [None] * len(lhs), sub_live, s_ref)

    def pv_fn(p, u, s_ref, m):
        qlanes = slice(u * LANES, (u + 1) * LANES)
        kv_rows = slice(HEAD_DIM * (p // pairs_per_kv), HEAD_DIM * (p // pairs_per_kv + 1))
        if u == 0:
            v_tile = jnp.concatenate([vp_ref[0, kv_rows, :], vo_ref[0, kv_rows, :LANES]], axis=1)
        else:
            v_tile = vo_ref[0, kv_rows, u * LANES - B_PREV:(u + 1) * LANES]
        sink = jnp.where(head_lanes < LANES, sink_ref[0, 2 * p], sink_ref[0, 2 * p + 1]) * LOG2E
        o = _pv_stage(s_ref, m, [v_tile], B_LIVE, extra_logit=sink)
        for hh in range(2):
            hrows = slice(hh * HEAD_DIM, (hh + 1) * HEAD_DIM)
            gate = gt_ref[0, p, hrows, qlanes].astype(jnp.float32)
            row0 = p * PAIR_W + hh * HEAD_DIM
            zt_ref[row0:row0 + HEAD_DIM, qlanes] = (o[:, hh * LANES:(hh + 1) * LANES] * gate).astype(jnp.bfloat16)

    _head_pipeline(N_PAIRS * (TQB // LANES), scores_fn, pv_fn, s_refs, scores_first=False, per_pair=TQB // LANES)

    h2 = _out_and_residual(zt_ref, wout_ref, h_ref[0])
    out_ref[0] = (h2 * fg_ref[...]) * _rms_scale(h2)


def _attn_b(h, qbt, ksh, vsht, gbt, vec, sinks, wout, fg):
    b, s, d = h.shape
    qt4 = qbt.reshape(b, N_PAIRS, PAIR_W, s)
    gt4 = gbt.reshape(b, N_PAIRS, PAIR_W, s)
    feat_spec = pl.BlockSpec((1, N_PAIRS, PAIR_W, TQB), lambda bi, i: (bi, 0, 0, i))
    prev = lambda i: jnp.maximum(i * (TQB // B_PREV) - 1, 0)
    return pl.pallas_call(
        _attn_b_kernel,
        grid=(b, s // TQB),
        in_specs=[
            pl.BlockSpec((1, TQB, d), lambda bi, i: (bi, i, 0)),
            feat_spec,
            pl.BlockSpec((1, B_PREV, PAIR_W), lambda bi, i: (bi, prev(i), 0)),
            pl.BlockSpec((1, TQB, PAIR_W), lambda bi, i: (bi, i, 0)),
            pl.BlockSpec((1, PAIR_W, B_PREV), lambda bi, i: (bi, 0, prev(i))),
            pl.BlockSpec((1, PAIR_W, TQB), lambda bi, i: (bi, 0, i)),
            feat_spec,
            _const_spec(vec.shape),
            pl.BlockSpec(memory_space=pltpu.SMEM),
            _const_spec(wout.shape),
            _const_spec(fg.shape),
        ],
        out_specs=pl.BlockSpec((1, TQB, d), lambda bi, i: (bi, i, 0)),
        out_shape=jax.ShapeDtypeStruct((b, s, d), jnp.float32),
        scratch_shapes=[pltpu.VMEM((d, TQB), jnp.bfloat16),
                        pltpu.VMEM((2 * N_PAIRS, B_WIN, TQ), jnp.bfloat16),
                        *[pltpu.VMEM((B_WIN, TQ), jnp.float32)] * N_SCORE_BUFS_B],
        compiler_params=pltpu.CompilerParams(
            dimension_semantics=("arbitrary", "arbitrary"), vmem_limit_bytes=VMEM_LIMIT),
        name="attn_b",
    )(h, qt4, ksh, ksh, vsht, vsht, gt4, vec, sinks, wout, fg)


def _slab_diff(xp=jnp):
    u = xp.arange(BIAS_PERIOD)
    return xp.where(u < TQ, u, u - BIAS_PERIOD)


def _static_take(table, idx):
    parts, lo = [], 0
    while lo < len(idx):
        hi = lo + 1
        step = idx[hi] - idx[lo] if hi < len(idx) else 0
        while step in (0, 1) and hi < len(idx) and idx[hi] - idx[hi - 1] == step:
            hi += 1
        rows = table[idx[lo]:idx[hi - 1] + 1]
        parts.append(jnp.broadcast_to(rows, (hi - lo,) + table.shape[1:]) if step == 0 else rows)
        lo = hi
    return jnp.concatenate(parts, axis=0)


def _bias_vecs_a(rel_bias):
    back = (A_KBLOCKS - 1 - np.arange(A_KBLOCKS))[:, None] * TQ
    dist = back + _slab_diff(np)[None, :]
    idx = np.clip(dist, -A_REL_CLIP, A_REL_CLIP) + A_REL_CLIP
    vec = jnp.stack([_static_take(rel_bias, [int(v) for v in row]) for row in idx])
    return jnp.transpose(vec, (0, 2, 1)).astype(jnp.float32)


def _t5_bucket(rel):
    nb = T5_BUCKETS // 2
    max_exact = nb // 2
    ret = jnp.where(rel > 0, nb, 0)
    n = jnp.abs(rel)
    nf = jnp.maximum(n, 1).astype(jnp.float32)
    large = max_exact + (jnp.log(nf / max_exact) / math.log(T5_MAX_DIST / max_exact)
                         * (nb - max_exact)).astype(jnp.int32)
    large = jnp.minimum(large, nb - 1)
    return ret + jnp.where(n < max_exact, n, large)


def _bias_vec_b(t5_table):
    rel = -_slab_diff() - B_PREV
    vec = jnp.transpose(t5_table[_t5_bucket(rel)], (1, 0)).astype(jnp.float32)
    return vec[None]


def kernel(x, a_norm, a_w_in, a_rel_bias, a_w_out, kv_norm, kv_w, t5_bias,
           b_norm, b_w_in, b_sinks, b_w_out, final_norm):
    assert a_norm.shape[0] == 1 and b_norm.shape[0] == 1, "one A layer then one B layer"
    scale = HEAD_DIM ** -0.5 * LOG2E

    def q_col_scale(n):
        return np.where(np.arange(n) < D_MODEL, scale, 1.0).astype(np.float32)

    wt_a = _weight_t(a_w_in[0], q_col_scale(a_w_in.shape[2]), a_norm[0], W_T_COLS_A)
    qt, k, vt, gt, nwt, kvw16, wo_a, wo_b = _proj_a(x, wt_a, kv_w, b_w_in[0], q_col_scale(b_w_in.shape[2]),
                                                    kv_norm, b_norm[0], a_w_out, b_w_out)
    h, ksh, vsht, qbt, gbt = _attn_a(
        x, qt, k, vt, gt, _bias_vecs_a(a_rel_bias[0] * LOG2E), wo_a, kvw16, nwt)

    return _attn_b(h, qbt, ksh, vsht, gbt, _bias_vec_b(t5_bias * LOG2E), b_sinks.astype(jnp.float32), wo_b,
                   final_norm[None, :])
```

```python
import functools
import math

import jax
import jax.numpy as jnp
import numpy as np
from jax import lax
from jax.experimental import pallas as pl
from jax.experimental.pallas import tpu as pltpu

D_MODEL = 1024
HEAD_DIM = 64
N_HEADS = D_MODEL // HEAD_DIM
N_PAIRS = N_HEADS // 2
PAIR_W = 2 * HEAD_DIM
CHUNK = 64
RMS_EPS = 1e-6
A_LEFT_CHUNKS = 8
A_REL_CLIP = 256
B_KV_HEADS = 2
B_LEFT_CHUNKS = 2
T5_BUCKETS = 32
T5_MAX_DIST = 128

TQ = 256
TQB = 512
TQA = 512
A_KBLOCKS = A_LEFT_CHUNKS * CHUNK // TQ + 1
B_PREV = B_LEFT_CHUNKS * CHUNK
TM_PROJ = 1024
FEAT_CHUNK = 256
W_T_COLS_A = 2048
W_T_COLS_B = 512
MASKED = -1e30
BIAS_PERIOD = 2 * TQ
SUBLANES = 8
LANES = 128
MAX_CHAINS = 1
BF16_ROWS = 16
LOG2E = math.log2(math.e)
N_SCORE_BUFS = 4
N_SCORE_BUFS_B = 8
VMEM_LIMIT = 56 * 1024 * 1024

_NT = (((1,), (1,)), ((), ()))
_TN = (((0,), (0,)), ((), ()))


def _rms_scale(xf):
    return lax.rsqrt(jnp.mean(xf * xf, axis=-1, keepdims=True) + RMS_EPS)


def _token_scales(xf):
    col = _rms_scale(xf)
    row = jnp.transpose(jnp.broadcast_to(col, (xf.shape[0], LANES)))[0:1, :]
    return col, row


def _silu(v):
    return v * jax.nn.sigmoid(v)


def _weight_t_kernel(w_ref, cs_ref, gain_ref, o_ref):
    g_col = jnp.transpose(jnp.broadcast_to(gain_ref[...], (LANES, w_ref.shape[0])))[:, :1]
    o_ref[...] = jnp.transpose(((w_ref[...] * cs_ref[...]) * g_col).astype(jnp.bfloat16))


def _weight_t(w, col_scale, row_gain, tc):
    d, n = w.shape
    return pl.pallas_call(
        _weight_t_kernel,
        grid=(n // tc,),
        in_specs=[
            pl.BlockSpec((d, tc), lambda j: (0, j)),
            pl.BlockSpec((1, tc), lambda j: (0, j)),
            pl.BlockSpec((1, d), lambda j: (0, 0)),
        ],
        out_specs=pl.BlockSpec((tc, d), lambda j: (j, 0)),
        out_shape=jax.ShapeDtypeStruct((n, d), jnp.bfloat16),
        compiler_params=pltpu.CompilerParams(dimension_semantics=("arbitrary",), vmem_limit_bytes=VMEM_LIMIT),
        name="weight_t",
    )(w, col_scale[None, :], row_gain[None, :])


def _proj_a_kernel(x_ref, wt_ref, kvw_ref, kvg_ref, bw_ref, cs_ref, bg_ref, woa_ref, wob_ref,
                   qt_ref, k_ref, vt_ref, gt_ref, nwt_ref, kvw16_ref, woa16_ref, wob16_ref):
    nwt_ref[...] = ((bw_ref[...] * cs_ref[...]).T * bg_ref[...]).astype(jnp.bfloat16)
    g_rows = jnp.transpose(jnp.broadcast_to(kvg_ref[...], (2 * PAIR_W, D_MODEL)))
    kvw16_ref[...] = (kvw_ref[...] * g_rows).astype(jnp.bfloat16)
    woa16_ref[...] = woa_ref[0].astype(jnp.bfloat16)
    wob16_ref[...] = wob_ref[0].astype(jnp.bfloat16)

    xf = x_ref[0]
    xb = xf.astype(jnp.bfloat16)
    r_col, r_row = _token_scales(xf)
    k = lax.dot_general(xb, wt_ref[D_MODEL:2 * D_MODEL, :], _NT,
                        preferred_element_type=jnp.float32) * r_col
    for p in range(N_PAIRS):
        k_ref[0, p] = k[:, p * PAIR_W:(p + 1) * PAIR_W].astype(jnp.bfloat16)
    for out_ref, base, act in ((gt_ref, 3 * D_MODEL, _silu), (vt_ref, 2 * D_MODEL, None), (qt_ref, 0, None)):
        for c in range(0, D_MODEL, FEAT_CHUNK):
            w = wt_ref[base + c:base + c + FEAT_CHUNK, :]
            yt = lax.dot_general(w, xb, _NT, preferred_element_type=jnp.float32) * r_row
            out_ref[0, c:c + FEAT_CHUNK, :] = (yt if act is None else act(yt)).astype(jnp.bfloat16)


def _proj_a(x, wt, kv_w, b_w, b_col_scale, kv_gain, b_gain, a_w_out, b_w_out):
    b, s, d = x.shape
    n_i = s // TM_PROJ
    wo_rows = d // (b * n_i)
    step = lambda bi, i: bi * n_i + i
    n_blocks = b_w.shape[1] // W_T_COLS_B
    assert n_blocks <= b * n_i and kv_w.shape == (d, 2 * PAIR_W) and d % (b * n_i * BF16_ROWS) == 0
    blk = lambda bi, i: jnp.minimum(bi * n_i + i, n_blocks - 1)
    feat = jax.ShapeDtypeStruct((b, d, s), jnp.bfloat16)
    return pl.pallas_call(
        _proj_a_kernel,
        grid=(b, n_i),
        in_specs=[
            pl.BlockSpec((1, TM_PROJ, d), lambda bi, i: (bi, i, 0)),
            _const_spec(wt.shape),
            _const_spec(kv_w.shape),
            _const_spec((1, d)),
            pl.BlockSpec((d, W_T_COLS_B), lambda bi, i: (0, blk(bi, i))),
            pl.BlockSpec((1, W_T_COLS_B), lambda bi, i: (0, blk(bi, i))),
            _const_spec((1, d)),
            pl.BlockSpec((1, wo_rows, d), lambda bi, i: (0, step(bi, i), 0)),
            pl.BlockSpec((1, wo_rows, d), lambda bi, i: (0, step(bi, i), 0)),
        ],
        out_specs=[
            pl.BlockSpec((1, d, TM_PROJ), lambda bi, i: (bi, 0, i)),
            pl.BlockSpec((1, N_PAIRS, TM_PROJ, PAIR_W), lambda bi, i: (bi, 0, i, 0)),
            pl.BlockSpec((1, d, TM_PROJ), lambda bi, i: (bi, 0, i)),
            pl.BlockSpec((1, d, TM_PROJ), lambda bi, i: (bi, 0, i)),
            pl.BlockSpec((W_T_COLS_B, d), lambda bi, i: (blk(bi, i), 0)),
            pl.BlockSpec((d, 2 * PAIR_W), lambda bi, i: (0, 0)),
            pl.BlockSpec((wo_rows, d), lambda bi, i: (step(bi, i), 0)),
            pl.BlockSpec((wo_rows, d), lambda bi, i: (step(bi, i), 0)),
        ],
        out_shape=[feat, jax.ShapeDtypeStruct((b, N_PAIRS, s, PAIR_W), jnp.bfloat16), feat, feat,
                   jax.ShapeDtypeStruct((n_blocks * W_T_COLS_B, d), jnp.bfloat16),
                   jax.ShapeDtypeStruct((d, 2 * PAIR_W), jnp.bfloat16),
                   jax.ShapeDtypeStruct((d, d), jnp.bfloat16), jax.ShapeDtypeStruct((d, d), jnp.bfloat16)],
        compiler_params=pltpu.CompilerParams(
            dimension_semantics=("arbitrary", "arbitrary"), vmem_limit_bytes=VMEM_LIMIT),
        name="proj_a",
    )(x, wt, kv_w, kv_gain[None, :], b_w, b_col_scale[None, :], b_gain[None, :], a_w_out, b_w_out)


def _chunk_iota(shape, axis):
    return lax.shift_right_logical(lax.broadcasted_iota(jnp.int32, shape, axis), CHUNK.bit_length() - 1)


def _identity_bf16(n):
    return (lax.broadcasted_iota(jnp.int32, (n, n), 0) == lax.broadcasted_iota(jnp.int32, (n, n), 1)).astype(jnp.bfloat16)


def _toeplitz(vec_row, n_keys):
    return pltpu.roll(jnp.broadcast_to(vec_row, (n_keys, BIAS_PERIOD)), 0, 1, stride=1, stride_axis=0)


def _fill_bias_slabs(vec_ref, bias_ref, n_keys, bands, flat):
    n = len(bands)
    kch = _chunk_iota((n_keys, TQ), 0)
    qch = _chunk_iota((n_keys, TQ), 1)

    def head_body(h, carry):
        for j, band in enumerate(bands):
            row = vec_ref[j, pl.ds(h, 1), :]
            t = (jnp.broadcast_to(row, (n_keys, BIAS_PERIOD)) if flat[j] else _toeplitz(row, n_keys))[:, :TQ]
            if band is not None:
                t = jnp.where(band(kch, qch), t, MASKED)
            bias_ref[h * n + j] = t
        return carry

    lax.fori_loop(0, N_HEADS, head_body, 0)


def _live_rows(band, n_keys):
    chunks_per_half = LANES // CHUNK
    out = []
    for half in range(TQ // LANES):
        qchs = range(half * chunks_per_half, (half + 1) * chunks_per_half)
        live = [kc for kc in range(n_keys // CHUNK) if band is None or any(band(kc, qc) for qc in qchs)]
        out.append((live[0] * CHUNK, (live[-1] + 1) * CHUNK) if live else (0, 0))
    return tuple(out)


def _scores_stage(k_tiles, q_tiles, bias_fns, live, s_ref):
    maxes, row = [], 0
    for half in range(TQ // LANES):
        maxes.append([None] * MAX_CHAINS)
    for kt, qz, bias_fn, live_j in zip(k_tiles, q_tiles, bias_fns, live):
        s = jnp.dot(kt, qz, preferred_element_type=jnp.float32)
        for r in range(s.shape[0] // SUBLANES):
            rows = slice(r * SUBLANES, (r + 1) * SUBLANES)
            for half, (r0, r1) in enumerate(live_j):
                if not r0 <= r * SUBLANES < r1:
                    continue
                lanes = slice(half * LANES, (half + 1) * LANES)
                acc = maxes[half]
                grp = s[rows, lanes] if bias_fn is None else s[rows, lanes] + bias_fn(rows, lanes)
                s_ref[row + r * SUBLANES:row + (r + 1) * SUBLANES, lanes] = grp
                c = r % MAX_CHAINS
                acc[c] = grp if acc[c] is None else jnp.maximum(acc[c], grp)
        row += s.shape[0]
    cols = [jnp.max(functools.reduce(jnp.maximum, [a for a in acc if a is not None]), axis=0, keepdims=True)
            for acc in maxes]
    return jnp.concatenate(cols, axis=1)


def _pv_stage(s_ref, m, v_tiles, live, extra_logit=None, row_reciprocal=False):
    if extra_logit is not None:
        m = jnp.maximum(m, extra_logit)
    acc, row = None, 0
    for vt, live_j in zip(v_tiles, live):
        n = vt.shape[1]
        halves = []
        for half, (r0, r1) in enumerate(live_j):
            lanes = slice(half * LANES, (half + 1) * LANES)
            parts = [jnp.zeros((r0, LANES), jnp.bfloat16)] if r0 else []
            if r1 > r0:
                parts.append(jnp.exp2(s_ref[row + r0:row + r1, lanes] - m[:, lanes]).astype(jnp.bfloat16))
            if n > r1:
                parts.append(jnp.zeros((n - r1, LANES), jnp.bfloat16))
            halves.append(parts[0] if len(parts) == 1 else jnp.concatenate(parts, axis=0))
        p = jnp.concatenate(halves, axis=1)
        v_ones = jnp.concatenate([vt, jnp.ones((BF16_ROWS, n), vt.dtype)], axis=0)
        part = jnp.dot(v_ones, p, preferred_element_type=jnp.float32)
        acc = part if acc is None else acc + part
        row += n
    l = acc[HEAD_DIM:HEAD_DIM + 1, :]
    if extra_logit is not None:
        l = l + jnp.exp2(extra_logit - m)
    return acc[:HEAD_DIM, :] * (1.0 / l) if row_reciprocal else acc[:HEAD_DIM, :] / l


def _head_pipeline(n_tiles, scores_fn, pv_fn, s_refs, scores_first, per_pair=2):
    per_group = len(s_refs) // 2
    x_refs, y_refs = s_refs[:per_group], s_refs[per_group:]
    n_groups = n_tiles // per_group

    m = [scores_fn(t // per_pair, t % per_pair, ref) for t, ref in zip(range(per_group), x_refs)]
    for g in range(n_groups):
        src, dst = (x_refs, y_refs) if g % 2 == 0 else (y_refs, x_refs)
        nxt = [(g + 1) * per_group + j for j in range(per_group)] if g + 1 < n_groups else []
        m_next = [scores_fn(t // per_pair, t % per_pair, ref) for t, ref in zip(nxt, dst)] if scores_first else []
        for j in range(per_group):
            if not scores_first and nxt:
                m_next.append(scores_fn(nxt[j] // per_pair, nxt[j] % per_pair, dst[j]))
            t = g * per_group + j
            pv_fn(t // per_pair, t % per_pair, src[j], m[j])
        m = m_next


A_BANDS = (lambda kc, qc: kc >= qc,
           None,
           lambda kc, qc: kc <= qc)
A_LIVE = tuple(_live_rows(band, TQ) for band in A_BANDS)
A_FLAT = tuple((A_KBLOCKS - 1 - j) * TQ - (TQ - 1) >= A_REL_CLIP for j in range(A_KBLOCKS))
B_WIN = B_PREV + LANES
B_SUB = PAIR_W
B_BAND = lambda kc, qc: (kc >= qc) & (kc <= qc + B_LEFT_CHUNKS)
B_LIVE = (((0, B_WIN),) * (TQ // LANES),)


def _out_and_residual(zt_ref, wout_ref, res):
    y = lax.dot_general(zt_ref[...], wout_ref[...], _TN, preferred_element_type=jnp.float32)
    return res + y


def _attn_a_kernel(x_ref, qt_ref, kp_ref, ko_ref, vp_ref, vo_ref, gt_ref,
                   bvec_ref, wout_hbm, kvw_hbm, nwt_hbm,
                   h_ref, ksh_ref, vsht_ref, qbt_ref, gbt_ref, zt_ref, bias_ref,
                   wout_ref, kvw_ref, nwt_ref, w_sems, *s_refs):
    i = pl.program_id(1)
    subs = TQA // TQ

    @pl.when((pl.program_id(0) == 0) & (i == 0))
    def _():
        copies = [pltpu.make_async_copy(src, dst, w_sems.at[n]) for n, (src, dst) in
                  enumerate(((wout_hbm, wout_ref), (kvw_hbm, kvw_ref), (nwt_hbm, nwt_ref)))]
        for copy in copies:
            copy.start()
        _fill_bias_slabs(bvec_ref, bias_ref, TQ, A_BANDS, A_FLAT)
        bias_ref[N_HEADS * A_KBLOCKS] = jnp.full((TQ, TQ), MASKED, jnp.float32)
        for copy in copies:
            copy.wait()

    n_slabs = N_HEADS * A_KBLOCKS
    rows = lax.broadcasted_iota(jnp.int32, (PAIR_W, TQ), 0)

    def key_block(sb, j):
        n = sb + j
        return n // subs, (n % subs) * TQ

    def scores_fn(sp, hh, s_ref):
        sb, p = divmod(sp, N_PAIRS)
        q2 = qt_ref[0, p, :, sb * TQ:(sb + 1) * TQ]
        own_rows = rows < HEAD_DIM if hh == 0 else rows >= HEAD_DIM
        qz = jnp.where(own_rows, q2, jnp.zeros_like(q2))
        k_tiles, bias_fns = [], []
        for j in range(A_KBLOCKS):
            in_seq = i * subs + sb - (A_KBLOCKS - 1) + j >= 0
            slab = jnp.where(in_seq, (2 * p + hh) * A_KBLOCKS + j, n_slabs)
            own, t0 = key_block(sb, j)
            k_tiles.append((ko_ref if own else kp_ref)[0, p, t0:t0 + TQ, :])
            bias_fns.append(lambda rws, lanes, slab=slab: bias_ref[slab, rws, lanes])
        return _scores_stage(k_tiles, [qz] * A_KBLOCKS, bias_fns, A_LIVE, s_ref)

    def pv_fn(sp, hh, s_ref, m):
        sb, p = divmod(sp, N_PAIRS)
        half = slice(hh * HEAD_DIM, (hh + 1) * HEAD_DIM)
        qcols = slice(sb * TQ, (sb + 1) * TQ)
        v_tiles = []
        for j in range(A_KBLOCKS):
            own, t0 = key_block(sb, j)
            v_tiles.append((vo_ref if own else vp_ref)[0, p, half, t0:t0 + TQ])
        o = _pv_stage(s_ref, m, v_tiles, A_LIVE, row_reciprocal=True)
        gate = gt_ref[0, p, half, qcols].astype(jnp.float32)
        row0 = p * PAIR_W + hh * HEAD_DIM
        zt_ref[row0:row0 + HEAD_DIM, qcols] = (o * gate).astype(jnp.bfloat16)

    _head_pipeline(subs * N_HEADS, scores_fn, pv_fn, s_refs, scores_first=True)

    h = _out_and_residual(zt_ref, wout_ref, x_ref[0])
    h_ref[0] = h
    hb = h.astype(jnp.bfloat16)
    r_col, r_row = _token_scales(h)
    kv = jnp.dot(hb, kvw_ref[...], preferred_element_type=jnp.float32) * r_col
    ksh_ref[0] = kv[:, :PAIR_W].astype(jnp.bfloat16)
    vsht_ref[0] = jnp.transpose(kv[:, PAIR_W:]).astype(jnp.bfloat16)
    for out_ref, base, act in ((gbt_ref, D_MODEL, _silu), (qbt_ref, 0, None)):
        for c in range(0, D_MODEL, FEAT_CHUNK):
            yt = lax.dot_general(nwt_ref[base + c:base + c + FEAT_CHUNK, :], hb, _NT,
                                 preferred_element_type=jnp.float32) * r_row
            out_ref[0, c:c + FEAT_CHUNK, :] = (yt if act is None else act(yt)).astype(jnp.bfloat16)


def _const_spec(shape):
    return pl.BlockSpec(shape, lambda bi, i: (0,) * len(shape), pipeline_mode=pl.Buffered(1))


def _attn_a(x, qt, k, vt, gt, bvec, wout, kvw, nwt):
    b, s, d = x.shape
    qt4 = qt.reshape(b, N_PAIRS, PAIR_W, s)
    vt4 = vt.reshape(b, N_PAIRS, PAIR_W, s)
    gt4 = gt.reshape(b, N_PAIRS, PAIR_W, s)
    assert (A_KBLOCKS - 1) * TQ <= TQA, "the keys before a grid block must fit in one previous block"
    prev = lambda i: jnp.maximum(i - 1, 0)

    def kspec(index):
        return pl.BlockSpec((1, N_PAIRS, TQA, PAIR_W), lambda bi, i: (bi, 0, index(i), 0))

    def vspec(index):
        return pl.BlockSpec((1, N_PAIRS, PAIR_W, TQA), lambda bi, i: (bi, 0, 0, index(i)))

    feat_spec = pl.BlockSpec((1, N_PAIRS, PAIR_W, TQA), lambda bi, i: (bi, 0, 0, i))
    featout_spec = pl.BlockSpec((1, d, TQA), lambda bi, i: (bi, 0, i))
    feat = jax.ShapeDtypeStruct((b, d, s), jnp.bfloat16)
    return pl.pallas_call(
        _attn_a_kernel,
        grid=(b, s // TQA),
        in_specs=[
            pl.BlockSpec((1, TQA, d), lambda bi, i: (bi, i, 0)),
            feat_spec,
            kspec(prev), kspec(lambda i: i),
            vspec(prev), vspec(lambda i: i),
            feat_spec,
            _const_spec(bvec.shape),
            pl.BlockSpec(memory_space=pl.ANY),
            pl.BlockSpec(memory_space=pl.ANY),
            pl.BlockSpec(memory_space=pl.ANY),
        ],
        out_specs=[
            pl.BlockSpec((1, TQA, d), lambda bi, i: (bi, i, 0)),
            pl.BlockSpec((1, TQA, PAIR_W), lambda bi, i: (bi, i, 0)),
            pl.BlockSpec((1, PAIR_W, TQA), lambda bi, i: (bi, 0, i)),
            featout_spec,
            featout_spec,
        ],
        out_shape=[
            jax.ShapeDtypeStruct((b, s, d), jnp.float32),
            jax.ShapeDtypeStruct((b, s, PAIR_W), jnp.bfloat16),
            jax.ShapeDtypeStruct((b, PAIR_W, s), jnp.bfloat16),
            feat, feat,
        ],
        scratch_shapes=[pltpu.VMEM((d, TQA), jnp.bfloat16),
                        pltpu.VMEM((N_HEADS * A_KBLOCKS + 1, TQ, TQ), jnp.float32),
                        pltpu.VMEM(wout.shape, wout.dtype), pltpu.VMEM(kvw.shape, kvw.dtype),
                        pltpu.VMEM(nwt.shape, nwt.dtype), pltpu.SemaphoreType.DMA((3,)),
                        *[pltpu.VMEM((A_KBLOCKS * TQ, TQ), jnp.float32)] * N_SCORE_BUFS],
        compiler_params=pltpu.CompilerParams(
            dimension_semantics=("arbitrary", "arbitrary"), vmem_limit_bytes=VMEM_LIMIT),
        name="attn_a",
    )(x, qt4, k, k, vt4, vt4, gt4, bvec, wout, kvw, nwt)


def _attn_b_kernel(h_ref, qt_ref, kp_ref, ko_ref, vp_ref, vo_ref, gt_ref, vec_ref,
                   sink_ref, wout_ref, fg_ref, out_ref, zt_ref, bias_ref, *s_refs):
    i = pl.program_id(1)

    @pl.when((pl.program_id(0) == 0) & (i == 0))
    def _():
        in_band = B_BAND(_chunk_iota((B_WIN, LANES), 0), _chunk_iota((B_WIN, LANES), 1))
        before_block = lax.broadcasted_iota(jnp.int32, (B_WIN, LANES), 0) < B_PREV

        def pair_body(p, carry):
            for hh in range(2):
                t = jnp.where(in_band, _toeplitz(vec_ref[0, pl.ds(2 * p + hh, 1), :], B_WIN)[:, :LANES], MASKED)
                lanes = slice(hh * LANES, (hh + 1) * LANES)
                bias_ref[2 * p, :, lanes] = t.astype(jnp.bfloat16)
                bias_ref[2 * p + 1, :, lanes] = jnp.where(before_block, MASKED, t).astype(jnp.bfloat16)
            return carry

        lax.fori_loop(0, N_PAIRS, pair_body, 0)

    rows = lax.broadcasted_iota(jnp.int32, (PAIR_W, TQ), 0)
    pairs_per_kv = N_PAIRS // B_KV_HEADS
    eye = _identity_bf16(B_SUB)
    head_lanes = lax.broadcasted_iota(jnp.int32, (1, TQ), 1)

    def scores_fn(p, u, s_ref):
        qlanes = slice(u * LANES, (u + 1) * LANES)
        q2 = jnp.concatenate([qt_ref[0, p, :HEAD_DIM, qlanes], qt_ref[0, p, HEAD_DIM:, qlanes]], axis=1)
        kv0 = HEAD_DIM * (p // pairs_per_kv)
        kv_rows = (rows >= kv0) & (rows < kv0 + HEAD_DIM)
        qz = jnp.where(kv_rows, jnp.concatenate([q2, q2], axis=0), jnp.zeros((PAIR_W, TQ), q2.dtype))
        if u == 0:
            k_tile = jnp.concatenate([kp_ref[0], ko_ref[0, :LANES, :]], axis=0)
            variant = jnp.where(i == 0, 1, 0)
        else:
            k_tile, variant = ko_ref[0, u * LANES - B_PREV:(u + 1) * LANES, :], 0

        lhs, rhs = [], []
        for r0 in range(0, B_WIN, B_SUB):
            lhs.append(jnp.concatenate([k_tile[r0:r0 + B_SUB, :], eye], axis=1))
            rhs.append(jnp.concatenate([qz, bias_ref[2 * p + variant, r0:r0 + B_SUB, :]], axis=0))
        sub_live = (((0, B_SUB),) * (TQ // LANES),) * len(lhs)
        return _scores_stage(lhs, rhs, [None] * len(lhs), sub_live, s_ref)

    def pv_fn(p, u, s_ref, m):
        qlanes = slice(u * LANES, (u + 1) * LANES)
        kv_rows = slice(HEAD_DIM * (p // pairs_per_kv), HEAD_DIM * (p // pairs_per_kv + 1))
        if u == 0:
            v_tile = jnp.concatenate([vp_ref[0, kv_rows, :], vo_ref[0, kv_rows, :LANES]], axis=1)
        else:
            v_tile = vo_ref[0, kv_rows, u * LANES - B_PREV:(u + 1) * LANES]
        sink = jnp.where(head_lanes < LANES, sink_ref[0, 2 * p], sink_ref[0, 2 * p + 1]) * LOG2E
        o = _pv_stage(s_ref, m, [v_tile], B_LIVE, extra_logit=sink)
        for hh in range(2):
            hrows = slice(hh * HEAD_DIM, (hh + 1) * HEAD_DIM)
            gate = gt_ref[0, p, hrows, qlanes].astype(jnp.float32)
            row0 = p * PAIR_W + hh * HEAD_DIM
            zt_ref[row0:row0 + HEAD_DIM, qlanes] = (o[:, hh * LANES:(hh + 1) * LANES] * gate).astype(jnp.bfloat16)

    _head_pipeline(N_PAIRS * (TQB // LANES), scores_fn, pv_fn, s_refs, scores_first=False, per_pair=TQB // LANES)

    h2 = _out_and_residual(zt_ref, wout_ref, h_ref[0])
    out_ref[0] = (h2 * fg_ref[...]) * _rms_scale(h2)


def _attn_b(h, qbt, ksh, vsht, gbt, vec, sinks, wout, fg):
    b, s, d = h.shape
    qt4 = qbt.reshape(b, N_PAIRS, PAIR_W, s)
    gt4 = gbt.reshape(b, N_PAIRS, PAIR_W, s)
    feat_spec = pl.BlockSpec((1, N_PAIRS, PAIR_W, TQB), lambda bi, i: (bi, 0, 0, i))
    prev = lambda i: jnp.maximum(i * (TQB // B_PREV) - 1, 0)
    return pl.pallas_call(
        _attn_b_kernel,
        grid=(b, s // TQB),
        in_specs=[
            pl.BlockSpec((1, TQB, d), lambda bi, i: (bi, i, 0)),
            feat_spec,
            pl.BlockSpec((1, B_PREV, PAIR_W), lambda bi, i: (bi, prev(i), 0)),
            pl.BlockSpec((1, TQB, PAIR_W), lambda bi, i: (bi, i, 0)),
            pl.BlockSpec((1, PAIR_W, B_PREV), lambda bi, i: (bi, 0, prev(i))),
            pl.BlockSpec((1, PAIR_W, TQB), lambda bi, i: (bi, 0, i)),
            feat_spec,
            _const_spec(vec.shape),
            pl.BlockSpec(memory_space=pltpu.SMEM),
            _const_spec(wout.shape),
            _const_spec(fg.shape),
        ],
        out_specs=pl.BlockSpec((1, TQB, d), lambda bi, i: (bi, i, 0)),
        out_shape=jax.ShapeDtypeStruct((b, s, d), jnp.float32),
        scratch_shapes=[pltpu.VMEM((d, TQB), jnp.bfloat16),
                        pltpu.VMEM((2 * N_PAIRS, B_WIN, TQ), jnp.bfloat16),
                        *[pltpu.VMEM((B_WIN, TQ), jnp.float32)] * N_SCORE_BUFS_B],
        compiler_params=pltpu.CompilerParams(
            dimension_semantics=("arbitrary", "arbitrary"), vmem_limit_bytes=VMEM_LIMIT),
        name="attn_b",
    )(h, qt4, ksh, ksh, vsht, vsht, gt4, vec, sinks, wout, fg)


def _slab_diff(xp=jnp):
    u = xp.arange(BIAS_PERIOD)
    return xp.where(u < TQ, u, u - BIAS_PERIOD)


def _static_take(table, idx):
    parts, lo = [], 0
    while lo < len(idx):
        hi = lo + 1
        step = idx[hi] - idx[lo] if hi < len(idx) else 0
        while step in (0, 1) and hi < len(idx) and idx[hi] - idx[hi - 1] == step:
            hi += 1
        rows = table[idx[lo]:idx[hi - 1] + 1]
        parts.append(jnp.broadcast_to(rows, (hi - lo,) + table.shape[1:]) if step == 0 else rows)
        lo = hi
    return jnp.concatenate(parts, axis=0)


def _bias_vecs_a(rel_bias):
    back = (A_KBLOCKS - 1 - np.arange(A_KBLOCKS))[:, None] * TQ
    dist = back + _slab_diff(np)[None, :]
    idx = np.clip(dist, -A_REL_CLIP, A_REL_CLIP) + A_REL_CLIP
    vec = jnp.stack([_static_take(rel_bias, [int(v) for v in row]) for row in idx])
    return jnp.transpose(vec, (0, 2, 1)).astype(jnp.float32)


def _t5_bucket(rel):
    nb = T5_BUCKETS // 2
    max_exact = nb // 2
    ret = jnp.where(rel > 0, nb, 0)
    n = jnp.abs(rel)
    nf = jnp.maximum(n, 1).astype(jnp.float32)
    large = max_exact + (jnp.log(nf / max_exact) / math.log(T5_MAX_DIST / max_exact)
                         * (nb - max_exact)).astype(jnp.int32)
    large = jnp.minimum(large, nb - 1)
    return ret + jnp.where(n < max_exact, n, large)


def _bias_vec_b(t5_table):
    rel = -_slab_diff() - B_PREV
    vec = jnp.transpose(t5_table[_t5_bucket(rel)], (1, 0)).astype(jnp.float32)
    return vec[None]


def kernel(x, a_norm, a_w_in, a_rel_bias, a_w_out, kv_norm, kv_w, t5_bias,
           b_norm, b_w_in, b_sinks, b_w_out, final_norm):
    assert a_norm.shape[0] == 1 and b_norm.shape[0] == 1, "one A layer then one B layer"
    scale = HEAD_DIM ** -0.5 * LOG2E

    def q_col_scale(n):
        return np.where(np.arange(n) < D_MODEL, scale, 1.0).astype(np.float32)

    wt_a = _weight_t(a_w_in[0], q_col_scale(a_w_in.shape[2]), a_norm[0], W_T_COLS_A)
    qt, k, vt, gt, nwt, kvw16, wo_a, wo_b = _proj_a(x, wt_a, kv_w, b_w_in[0], q_col_scale(b_w_in.shape[2]),
                                                    kv_norm, b_norm[0], a_w_out, b_w_out)
    h, ksh, vsht, qbt, gbt = _attn_a(
        x, qt, k, vt, gt, _bias_vecs_a(a_rel_bias[0] * LOG2E), wo_a, kvw16, nwt)

    return _attn_b(h, qbt, ksh, vsht, gbt, _bias_vec_b(t5_bias * LOG2E), b_sinks.astype(jnp.float32), wo_b,
                   final_norm[None, :])
```
